```python
import math
import jax, jax.numpy as jnp
from jax import lax
import numpy as np

D_MODEL = 1024
BATCH = 32
SEQ = 2048
DEPTH = 4

SSM_HEAD_DIM = 64
SSM_HEADS = D_MODEL // SSM_HEAD_DIM
SSM_WIDTH = SSM_HEADS * SSM_HEAD_DIM
SSM_GROUPS = 2
SSM_STATE = 128
SSM_CONV = 7
SSM_CHUNK = 128
ATTN_HEAD_DIM = 64
ATTN_HEADS = D_MODEL // ATTN_HEAD_DIM
ATTN_KV_HEADS = ATTN_HEADS // 4
ATTN_WIDTH = ATTN_HEADS * ATTN_HEAD_DIM
KV_WIDTH = ATTN_KV_HEADS * ATTN_HEAD_DIM
WINDOW = 128
ATTN_BLOCK = 128
KEY_SPAN = ATTN_BLOCK + 2 * WINDOW
REL_BUCKETS = 32
REL_MAX_DIST = 128
MIX_WIDTH = SSM_WIDTH + ATTN_WIDTH
D_FF = 256 * ((8 * D_MODEL // 3 + 255) // 256)
FFN_CONV = 3
NORM_EPS = 1e-6

BC_WIDTH = SSM_GROUPS * SSM_STATE
CONV_CH = SSM_WIDTH + 2 * BC_WIDTH
Z_END = SSM_WIDTH
XBC_END = Z_END + CONV_CH
DT_END = XBC_END + 2 * SSM_HEADS
Q_END = DT_END + ATTN_WIDTH
K_END = Q_END + KV_WIDTH
IN_COLS = K_END + KV_WIDTH

kernel_name = "hymba_ssd_swa_convffn_encoder"


def rms_norm(x, w):
    xf = x.astype(jnp.float32)
    y = xf * lax.rsqrt(jnp.mean(xf * xf, axis=-1, keepdims=True) + NORM_EPS)
    return (y * w.astype(jnp.float32)).astype(x.dtype)


def depthwise_conv_centered(x, w, b):
    k, ch = w.shape
    pad = k // 2
    y = lax.conv_general_dilated(
        x, w[:, None, :].astype(x.dtype), window_strides=(1,), padding=[(pad, pad)],
        dimension_numbers=("NWC", "WIO", "NWC"), feature_group_count=ch)
    return y + b.astype(x.dtype)


def t5_bucket(rel):
    half = REL_BUCKETS // 2
    max_exact = half // 2
    ret = jnp.where(rel > 0, half, 0)
    n = jnp.abs(rel)
    nf = jnp.maximum(n, 1).astype(jnp.float32)
    large = max_exact + (jnp.log(nf / max_exact) / math.log(REL_MAX_DIST / max_exact)
                         * (half - max_exact)).astype(jnp.int32)
    large = jnp.minimum(large, half - 1)
    return ret + jnp.where(n < max_exact, n, large)


def ssd_chunked(x, dt, a, b, c):
    f32 = jnp.float32
    bsz, seq, nh, hp = x.shape
    ng, ns = b.shape[-2:]
    rep = nh // ng
    nc, cl = seq // SSM_CHUNK, SSM_CHUNK
    xdt = (x.astype(f32) * dt[..., None]).reshape(bsz, nc, cl, ng, rep, hp)
    a_dt = (dt * a).reshape(bsz, nc, cl, ng, rep).transpose(0, 3, 4, 1, 2)
    a_cum = jnp.cumsum(a_dt, axis=-1)
    b = b.astype(f32).reshape(bsz, nc, cl, ng, ns)
    c = c.astype(f32).reshape(bsz, nc, cl, ng, ns)
    seg = a_cum[..., :, None] - a_cum[..., None, :]
    lower = jnp.tril(jnp.ones((cl, cl), dtype=bool))
    decay = jnp.exp(jnp.where(lower, seg, -jnp.inf))
    cb = jnp.einsum("bclgn,bcsgn->bgcls", c, b)
    mix = cb[:, :, None] * decay
    y_diag = jnp.einsum("bgrcls,bcsgrp->bclgrp", mix, xdt)
    decay_states = jnp.exp(a_cum[..., -1:] - a_cum)
    states = jnp.einsum("bclgn,bgrcl,bclgrp->cbgrpn", b, decay_states, xdt)
    chunk_decay = jnp.exp(a_cum[..., -1]).transpose(3, 0, 1, 2)

    def step(h, inp):
        st, dec = inp
        return h * dec[..., None, None] + st, h

    h0 = jnp.zeros((bsz, ng, rep, hp, ns), f32)
    _, prev = lax.scan(step, h0, (states, chunk_decay))
    y_off = jnp.einsum("bclgn,cbgrpn,bgrcl->bclgrp", c, prev, jnp.exp(a_cum))
    return (y_diag + y_off).reshape(bsz, seq, nh, hp)


def windowed_gqa(q, k, v, sink, band_bias):
    bsz, seq, nh, hd = q.shape
    nkv = k.shape[2]
    rep = nh // nkv
    nblk = seq // ATTN_BLOCK
    qb_all = q.reshape(bsz, nblk, ATTN_BLOCK, nkv, rep, hd).transpose(1, 0, 2, 3, 4, 5)
    kp = jnp.pad(k, ((0, 0), (WINDOW, WINDOW), (0, 0), (0, 0)))
    vp = jnp.pad(v, ((0, 0), (WINDOW, WINDOW), (0, 0), (0, 0)))
    rel = jnp.arange(KEY_SPAN)[None, :] - WINDOW - jnp.arange(ATTN_BLOCK)[:, None]
    band = jnp.abs(rel) <= WINDOW
    bias = band_bias.reshape(nkv, rep, ATTN_BLOCK, KEY_SPAN)
    sink_l = sink.astype(jnp.float32).reshape(nkv, rep, 1, 1)
    scale = hd ** -0.5

    def one_block(args):
        qb, n = args
        start = n * ATTN_BLOCK
        kb = lax.dynamic_slice_in_dim(kp, start, KEY_SPAN, axis=1)
        vb = lax.dynamic_slice_in_dim(vp, start, KEY_SPAN, axis=1)
        kpos = start - WINDOW + jnp.arange(KEY_SPAN)
        valid = band & ((kpos >= 0) & (kpos < seq))[None, :]
        s = jnp.einsum("bqgrd,bkgd->bgrqk", qb, kb,
                       preferred_element_type=jnp.float32) * scale + bias
        s = jnp.where(valid, s, -jnp.inf)
        m = jnp.maximum(jnp.max(s, axis=-1, keepdims=True), sink_l)
        p = jnp.exp(s - m)
        denom = jnp.sum(p, axis=-1, keepdims=True) + jnp.exp(sink_l - m)
        return jnp.einsum("bgrqk,bkgd->bqgrd", (p / denom).astype(vb.dtype), vb)

    out = lax.map(one_block, (qb_all, jnp.arange(nblk)))
    return out.transpose(1, 0, 2, 3, 4, 5).reshape(bsz, seq, nh * hd)


def _fwd_setup_inputs(seed: int = 0) -> dict:
    key = jax.random.key(seed)
    ks = jax.random.split(key, 20)
    f32 = jnp.float32

    def nrm(k, shape, scale):
        return jax.random.normal(k, shape, f32) * scale

    dt0 = jnp.exp(jax.random.uniform(ks[6], (DEPTH, 2, SSM_HEADS), f32,
                                     math.log(1e-3), math.log(1e-1)))
    return {
        "x": nrm(ks[0], (BATCH, SEQ, D_MODEL), 1.0),
        "rel_bias": nrm(ks[1], (REL_BUCKETS, ATTN_HEADS), 0.5),
        "norm1_w": 1.0 + nrm(ks[2], (DEPTH, D_MODEL), 0.05),
        "w_in": nrm(ks[3], (DEPTH, D_MODEL, IN_COLS), D_MODEL ** -0.5),
        "conv_w": nrm(ks[4], (DEPTH, SSM_CONV, CONV_CH), SSM_CONV ** -0.5),
        "conv_b": nrm(ks[5], (DEPTH, CONV_CH), 0.01),
        "dt_bias": dt0 + jnp.log(-jnp.expm1(-dt0)),
        "a_log": jnp.log(jax.random.uniform(ks[7], (DEPTH, 2, SSM_HEADS), f32, 1.0, 16.0)),
        "d_skip": 1.0 + nrm(ks[8], (DEPTH, SSM_HEADS), 0.1),
        "ssm_norm_w": 1.0 + nrm(ks[9], (DEPTH, SSM_WIDTH), 0.05),
        "attn_sink": nrm(ks[10], (DEPTH, ATTN_HEADS), 0.5),
        "w_out": nrm(ks[11], (DEPTH, MIX_WIDTH, D_MODEL), MIX_WIDTH ** -0.5),
        "norm2_w": 1.0 + nrm(ks[12], (DEPTH, D_MODEL), 0.05),
        "w_up": nrm(ks[13], (DEPTH, D_MODEL, 2 * D_FF), D_MODEL ** -0.5),
        "ffn_conv_w": nrm(ks[14], (DEPTH, FFN_CONV, D_FF), FFN_CONV ** -0.5),
        "ffn_conv_b": nrm(ks[15], (DEPTH, D_FF), 0.01),
        "w_down": nrm(ks[16], (DEPTH, D_FF, D_MODEL), D_FF ** -0.5),
        "final_norm_w": 1.0 + nrm(ks[17], (D_MODEL,), 0.05),
    }


def _fwd_reference(x, rel_bias, norm1_w, w_in, conv_w, conv_b, dt_bias, a_log, d_skip,
              ssm_norm_w, attn_sink, w_out, norm2_w, w_up, ffn_conv_w, ffn_conv_b,
              w_down, final_norm_w):
    bsz, seq, _ = x.shape
    f32 = jnp.float32
    rel = jnp.arange(KEY_SPAN)[None, :] - WINDOW - jnp.arange(ATTN_BLOCK)[:, None]
    band_bias = rel_bias.astype(f32)[t5_bucket(rel)].transpose(2, 0, 1)

    for i in range(DEPTH):
        h = rms_norm(x, norm1_w[i])
        proj = h @ w_in[i]
        z, xbc, dt_raw, q, k, v = jnp.split(proj, [Z_END, XBC_END, DT_END, Q_END, K_END], axis=-1)

        xbc = jax.nn.silu(depthwise_conv_centered(xbc, conv_w[i], conv_b[i]))
        xs, bm, cm = jnp.split(xbc, [SSM_WIDTH, SSM_WIDTH + BC_WIDTH], axis=-1)
        xs = xs.reshape(bsz, seq, SSM_HEADS, SSM_HEAD_DIM)
        bm = bm.reshape(bsz, seq, SSM_GROUPS, SSM_STATE)
        cm = cm.reshape(bsz, seq, SSM_GROUPS, SSM_STATE)
        dt = jax.nn.softplus(dt_raw.astype(f32).reshape(bsz, seq, 2, SSM_HEADS)
                             + dt_bias[i].astype(f32))
        a = -jnp.exp(a_log[i].astype(f32))
        y_fwd = ssd_chunked(xs, dt[:, :, 0], a[0], bm, cm)
        y_bwd = jnp.flip(ssd_chunked(jnp.flip(xs, 1), jnp.flip(dt[:, :, 1], 1), a[1],
                                     jnp.flip(bm, 1), jnp.flip(cm, 1)), 1)
        y_ssm = y_fwd + y_bwd + d_skip[i].astype(f32)[:, None] * xs.astype(f32)
        y_ssm = y_ssm.reshape(bsz, seq, SSM_GROUPS, SSM_WIDTH // SSM_GROUPS) \
            * jax.nn.silu(z.astype(f32)).reshape(bsz, seq, SSM_GROUPS, SSM_WIDTH // SSM_GROUPS)
        y_ssm = rms_norm(y_ssm, ssm_norm_w[i].reshape(SSM_GROUPS, -1)).reshape(bsz, seq, SSM_WIDTH)

        y_attn = windowed_gqa(q.reshape(bsz, seq, ATTN_HEADS, ATTN_HEAD_DIM),
                              k.reshape(bsz, seq, ATTN_KV_HEADS, ATTN_HEAD_DIM),
                              v.reshape(bsz, seq, ATTN_KV_HEADS, ATTN_HEAD_DIM),
                              attn_sink[i], band_bias)

        mixed = jnp.concatenate([y_ssm.astype(x.dtype), y_attn.astype(x.dtype)], axis=-1)
        x = x + mixed @ w_out[i]

        h = rms_norm(x, norm2_w[i])
        g, u = jnp.split(h @ w_up[i], [D_FF], axis=-1)
        g = depthwise_conv_centered(g, ffn_conv_w[i], ffn_conv_b[i])
        x = x + (jax.nn.silu(g) * u) @ w_down[i]

    return rms_norm(x, final_norm_w)


import jax as _jax
import jax.numpy as _jnp

TWIN_FORMAT = 'train_step'
FWD_PARAMS = ['x', 'rel_bias', 'norm1_w', 'w_in', 'conv_w', 'conv_b', 'dt_bias', 'a_log', 'd_skip', 'ssm_norm_w', 'attn_sink', 'w_out', 'norm2_w', 'w_up', 'ffn_conv_w', 'ffn_conv_b', 'w_down', 'final_norm_w']
TWIN_WEIGHTS = ['rel_bias', 'norm1_w', 'w_in', 'conv_w', 'conv_b', 'dt_bias', 'a_log', 'd_skip', 'ssm_norm_w', 'attn_sink', 'w_out', 'norm2_w', 'w_up', 'ffn_conv_w', 'ffn_conv_b', 'w_down', 'final_norm_w']
TWIN_DIFF_INPUT = 'x'
TWIN_INPUTS = ['x', 'rel_bias', 'norm1_w', 'w_in', 'conv_w', 'conv_b', 'dt_bias', 'a_log', 'd_skip', 'ssm_norm_w', 'attn_sink', 'w_out', 'norm2_w', 'w_up', 'ffn_conv_w', 'ffn_conv_b', 'w_down', 'final_norm_w', 'loss_target', 'm_rel_bias', 'm_norm1_w', 'm_w_in', 'm_conv_w', 'm_conv_b', 'm_dt_bias', 'm_a_log', 'm_d_skip', 'm_ssm_norm_w', 'm_attn_sink', 'm_w_out', 'm_norm2_w', 'm_w_up', 'm_ffn_conv_w', 'm_ffn_conv_b', 'm_w_down', 'm_final_norm_w', 'v_rel_bias', 'v_norm1_w', 'v_w_in', 'v_conv_w', 'v_conv_b', 'v_dt_bias', 'v_a_log', 'v_d_skip', 'v_ssm_norm_w', 'v_attn_sink', 'v_w_out', 'v_norm2_w', 'v_w_up', 'v_ffn_conv_w', 'v_ffn_conv_b', 'v_w_down', 'v_final_norm_w']
TWIN_OUTPUTS = ['loss', 'grad_x', 'grad_rel_bias', 'grad_norm1_w', 'grad_w_in', 'grad_conv_w', 'grad_conv_b', 'grad_dt_bias', 'grad_a_log', 'grad_d_skip', 'grad_ssm_norm_w', 'grad_attn_sink', 'grad_w_out', 'grad_norm2_w', 'grad_w_up', 'grad_ffn_conv_w', 'grad_ffn_conv_b', 'grad_w_down', 'grad_final_norm_w', 'delta_rel_bias', 'delta_norm1_w', 'delta_w_in', 'delta_conv_w', 'delta_conv_b', 'delta_dt_bias', 'delta_a_log', 'delta_d_skip', 'delta_ssm_norm_w', 'delta_attn_sink', 'delta_w_out', 'delta_norm2_w', 'delta_w_up', 'delta_ffn_conv_w', 'delta_ffn_conv_b', 'delta_w_down', 'delta_final_norm_w', 'new_m_rel_bias', 'new_m_norm1_w', 'new_m_w_in', 'new_m_conv_w', 'new_m_conv_b', 'new_m_dt_bias', 'new_m_a_log', 'new_m_d_skip', 'new_m_ssm_norm_w', 'new_m_attn_sink', 'new_m_w_out', 'new_m_norm2_w', 'new_m_w_up', 'new_m_ffn_conv_w', 'new_m_ffn_conv_b', 'new_m_w_down', 'new_m_final_norm_w', 'new_v_rel_bias', 'new_v_norm1_w', 'new_v_w_in', 'new_v_conv_w', 'new_v_conv_b', 'new_v_dt_bias', 'new_v_a_log', 'new_v_d_skip', 'new_v_ssm_norm_w', 'new_v_attn_sink', 'new_v_w_out', 'new_v_norm2_w', 'new_v_w_up', 'new_v_ffn_conv_w', 'new_v_ffn_conv_b', 'new_v_w_down', 'new_v_final_norm_w']
TWIN_LEAF_KINDS = {'loss': 'loss', 'grad_x': 'grad_x', 'grad_rel_bias': 'grad_w', 'grad_norm1_w': 'grad_w', 'grad_w_in': 'grad_w', 'grad_conv_w': 'grad_w', 'grad_conv_b': 'grad_w', 'grad_dt_bias': 'grad_w', 'grad_a_log': 'grad_w', 'grad_d_skip': 'grad_w', 'grad_ssm_norm_w': 'grad_w', 'grad_attn_sink': 'grad_w', 'grad_w_out': 'grad_w', 'grad_norm2_w': 'grad_w', 'grad_w_up': 'grad_w', 'grad_ffn_conv_w': 'grad_w', 'grad_ffn_conv_b': 'grad_w', 'grad_w_down': 'grad_w', 'grad_final_norm_w': 'grad_w', 'delta_rel_bias': 'delta_w', 'delta_norm1_w': 'delta_w', 'delta_w_in': 'delta_w', 'delta_conv_w': 'delta_w', 'delta_conv_b': 'delta_w', 'delta_dt_bias': 'delta_w', 'delta_a_log': 'delta_w', 'delta_d_skip': 'delta_w', 'delta_ssm_norm_w': 'delta_w', 'delta_attn_sink': 'delta_w', 'delta_w_out': 'delta_w', 'delta_norm2_w': 'delta_w', 'delta_w_up': 'delta_w', 'delta_ffn_conv_w': 'delta_w', 'delta_ffn_conv_b': 'delta_w', 'delta_w_down': 'delta_w', 'delta_final_norm_w': 'delta_w', 'new_m_rel_bias': 'new_m', 'new_m_norm1_w': 'new_m', 'new_m_w_in': 'new_m', 'new_m_conv_w': 'new_m', 'new_m_conv_b': 'new_m', 'new_m_dt_bias': 'new_m', 'new_m_a_log': 'new_m', 'new_m_d_skip': 'new_m', 'new_m_ssm_norm_w': 'new_m', 'new_m_attn_sink': 'new_m', 'new_m_w_out': 'new_m', 'new_m_norm2_w': 'new_m', 'new_m_w_up': 'new_m', 'new_m_ffn_conv_w': 'new_m', 'new_m_ffn_conv_b': 'new_m', 'new_m_w_down': 'new_m', 'new_m_final_norm_w': 'new_m', 'new_v_rel_bias': 'new_v', 'new_v_norm1_w': 'new_v', 'new_v_w_in': 'new_v', 'new_v_conv_w': 'new_v', 'new_v_conv_b': 'new_v', 'new_v_dt_bias': 'new_v', 'new_v_a_log': 'new_v', 'new_v_d_skip': 'new_v', 'new_v_ssm_norm_w': 'new_v', 'new_v_attn_sink': 'new_v', 'new_v_w_out': 'new_v', 'new_v_norm2_w': 'new_v', 'new_v_w_up': 'new_v', 'new_v_ffn_conv_w': 'new_v', 'new_v_ffn_conv_b': 'new_v', 'new_v_w_down': 'new_v', 'new_v_final_norm_w': 'new_v'}


def _forward(args):
    return _fwd_reference(*[args[k] for k in FWD_PARAMS])


def _output_shape():
    out = _jax.eval_shape(lambda: _forward(_fwd_setup_inputs(0)))
    return out.shape, out.dtype

N_MICROBATCH = 1
ADAM_LR = 0.001
ADAM_B1 = 0.9
ADAM_B2 = 0.999
ADAM_EPS = 1e-08
ADAM_WD = 0.01
ADAM_STEP = 10
PER_EXAMPLE_BATCH_AXIS = {'x': 0, 'loss_target': 0}
SHARED_INPUTS = []
_WEIGHT_DTYPES = {'rel_bias': _jnp.float32, 'norm1_w': _jnp.float32, 'w_in': _jnp.float32, 'conv_w': _jnp.float32, 'conv_b': _jnp.float32, 'dt_bias': _jnp.float32, 'a_log': _jnp.float32, 'd_skip': _jnp.float32, 'ssm_norm_w': _jnp.float32, 'attn_sink': _jnp.float32, 'w_out': _jnp.float32, 'norm2_w': _jnp.float32, 'w_up': _jnp.float32, 'ffn_conv_w': _jnp.float32, 'ffn_conv_b': _jnp.float32, 'w_down': _jnp.float32, 'final_norm_w': _jnp.float32}
MOMENT_SCALE = {'rel_bias': 5.870385e-02, 'norm1_w': 2.450652e-01, 'w_in': 1.228518e-01, 'conv_w': 1.363827e-01, 'conv_b': 2.308013e-01, 'dt_bias': 4.417526e-01, 'a_log': 4.241068e-01, 'd_skip': 7.438641e-01, 'ssm_norm_w': 1.790268e-01, 'attn_sink': 1.202931e-03, 'w_out': 1.699329e-01, 'norm2_w': 1.722361e-01, 'w_up': 7.289787e-02, 'ffn_conv_w': 7.390088e-02, 'ffn_conv_b': 7.252380e-02, 'w_down': 1.193025e-01, 'final_norm_w': 6.396017e+01}


def _to_microbatches(a, axis):
    t = _jnp.moveaxis(a, axis, 0)
    t = t.reshape((N_MICROBATCH, t.shape[0] // N_MICROBATCH) + t.shape[1:])
    return _jnp.moveaxis(t, 1, axis + 1)


def setup_inputs(seed: int = 0) -> dict:
    inp = _fwd_setup_inputs(seed)
    key = _jax.random.fold_in(_jax.random.key(seed), 7919)
    shape, _ = _output_shape()
    out = dict(inp)
    out["loss_target"] = _jax.random.normal(_jax.random.fold_in(key, 0), shape, _jnp.float32)
    for i, name in enumerate(TWIN_WEIGHTS):
        w = inp[name].astype(_jnp.float32)
        if MOMENT_SCALE is None:
            s = _jnp.sqrt(_jnp.mean(_jnp.square(w)) + 1e-30)
        else:
            s = MOMENT_SCALE[name]
        km, kv = _jax.random.split(_jax.random.fold_in(key, i + 1))
        out[name] = w
        out["m_" + name] = s * _jax.random.normal(km, w.shape, _jnp.float32)
        out["v_" + name] = (s * s) * _jax.random.uniform(kv, w.shape, _jnp.float32, 0.5, 1.5)
    if N_MICROBATCH > 1:
        for name, axis in PER_EXAMPLE_BATCH_AXIS.items():
            out[name] = _to_microbatches(out[name], axis)
    return {'x': out['x'], 'rel_bias': out['rel_bias'], 'norm1_w': out['norm1_w'], 'w_in': out['w_in'], 'conv_w': out['conv_w'], 'conv_b': out['conv_b'], 'dt_bias': out['dt_bias'], 'a_log': out['a_log'], 'd_skip': out['d_skip'], 'ssm_norm_w': out['ssm_norm_w'], 'attn_sink': out['attn_sink'], 'w_out': out['w_out'], 'norm2_w': out['norm2_w'], 'w_up': out['w_up'], 'ffn_conv_w': out['ffn_conv_w'], 'ffn_conv_b': out['ffn_conv_b'], 'w_down': out['w_down'], 'final_norm_w': out['final_norm_w'], 'loss_target': out['loss_target'], 'm_rel_bias': out['m_rel_bias'], 'm_norm1_w': out['m_norm1_w'], 'm_w_in': out['m_w_in'], 'm_conv_w': out['m_conv_w'], 'm_conv_b': out['m_conv_b'], 'm_dt_bias': out['m_dt_bias'], 'm_a_log': out['m_a_log'], 'm_d_skip': out['m_d_skip'], 'm_ssm_norm_w': out['m_ssm_norm_w'], 'm_attn_sink': out['m_attn_sink'], 'm_w_out': out['m_w_out'], 'm_norm2_w': out['m_norm2_w'], 'm_w_up': out['m_w_up'], 'm_ffn_conv_w': out['m_ffn_conv_w'], 'm_ffn_conv_b': out['m_ffn_conv_b'], 'm_w_down': out['m_w_down'], 'm_final_norm_w': out['m_final_norm_w'], 'v_rel_bias': out['v_rel_bias'], 'v_norm1_w': out['v_norm1_w'], 'v_w_in': out['v_w_in'], 'v_conv_w': out['v_conv_w'], 'v_conv_b': out['v_conv_b'], 'v_dt_bias': out['v_dt_bias'], 'v_a_log': out['v_a_log'], 'v_d_skip': out['v_d_skip'], 'v_ssm_norm_w': out['v_ssm_norm_w'], 'v_attn_sink': out['v_attn_sink'], 'v_w_out': out['v_w_out'], 'v_norm2_w': out['v_norm2_w'], 'v_w_up': out['v_w_up'], 'v_ffn_conv_w': out['v_ffn_conv_w'], 'v_ffn_conv_b': out['v_ffn_conv_b'], 'v_w_down': out['v_w_down'], 'v_final_norm_w': out['v_final_norm_w']}


def _loss(weights, diff, rest, loss_target):
    with _jax.named_scope("forward"):
        args = {**rest, TWIN_DIFF_INPUT: diff, **{k: w.astype(_WEIGHT_DTYPES[k]) for k, w in weights.items()}}
        y = _forward(args)
    with _jax.named_scope("loss_head"):
        err = _jnp.square(y.astype(_jnp.float32) - loss_target)
        return 0.5 * _jnp.sum(_jnp.mean(err, axis=-1)) if err.ndim else 0.5 * err


def _adamw(w, g, m, v):
    m = ADAM_B1 * m + (1.0 - ADAM_B1) * g
    v = ADAM_B2 * v + (1.0 - ADAM_B2) * _jnp.square(g)
    m_hat = m / (1.0 - ADAM_B1 ** ADAM_STEP)
    v_hat = v / (1.0 - ADAM_B2 ** ADAM_STEP)
    delta = -ADAM_LR * (m_hat / (_jnp.sqrt(v_hat) + ADAM_EPS) + ADAM_WD * w)
    return delta, m, v


def reference(x, rel_bias, norm1_w, w_in, conv_w, conv_b, dt_bias, a_log, d_skip, ssm_norm_w, attn_sink, w_out, norm2_w, w_up, ffn_conv_w, ffn_conv_b, w_down, final_norm_w, loss_target, m_rel_bias, m_norm1_w, m_w_in, m_conv_w, m_conv_b, m_dt_bias, m_a_log, m_d_skip, m_ssm_norm_w, m_attn_sink, m_w_out, m_norm2_w, m_w_up, m_ffn_conv_w, m_ffn_conv_b, m_w_down, m_final_norm_w, v_rel_bias, v_norm1_w, v_w_in, v_conv_w, v_conv_b, v_dt_bias, v_a_log, v_d_skip, v_ssm_norm_w, v_attn_sink, v_w_out, v_norm2_w, v_w_up, v_ffn_conv_w, v_ffn_conv_b, v_w_down, v_final_norm_w):
    given = dict(x=x, rel_bias=rel_bias, norm1_w=norm1_w, w_in=w_in, conv_w=conv_w, conv_b=conv_b, dt_bias=dt_bias, a_log=a_log, d_skip=d_skip, ssm_norm_w=ssm_norm_w, attn_sink=attn_sink, w_out=w_out, norm2_w=norm2_w, w_up=w_up, ffn_conv_w=ffn_conv_w, ffn_conv_b=ffn_conv_b, w_down=w_down, final_norm_w=final_norm_w, loss_target=loss_target, m_rel_bias=m_rel_bias, m_norm1_w=m_norm1_w, m_w_in=m_w_in, m_conv_w=m_conv_w, m_conv_b=m_conv_b, m_dt_bias=m_dt_bias, m_a_log=m_a_log, m_d_skip=m_d_skip, m_ssm_norm_w=m_ssm_norm_w, m_attn_sink=m_attn_sink, m_w_out=m_w_out, m_norm2_w=m_norm2_w, m_w_up=m_w_up, m_ffn_conv_w=m_ffn_conv_w, m_ffn_conv_b=m_ffn_conv_b, m_w_down=m_w_down, m_final_norm_w=m_final_norm_w, v_rel_bias=v_rel_bias, v_norm1_w=v_norm1_w, v_w_in=v_w_in, v_conv_w=v_conv_w, v_conv_b=v_conv_b, v_dt_bias=v_dt_bias, v_a_log=v_a_log, v_d_skip=v_d_skip, v_ssm_norm_w=v_ssm_norm_w, v_attn_sink=v_attn_sink, v_w_out=v_w_out, v_norm2_w=v_norm2_w, v_w_up=v_w_up, v_ffn_conv_w=v_ffn_conv_w, v_ffn_conv_b=v_ffn_conv_b, v_w_down=v_w_down, v_final_norm_w=v_final_norm_w)
    weights = {n: given[n] for n in TWIN_WEIGHTS}
    shared = {n: given[n] for n in SHARED_INPUTS}
    per_example = {n: given[n] for n in ['x']}
    grad_fn = _jax.value_and_grad(_loss, argnums=(0, 1))

    def one_microbatch(ex, loss_target):
        ex = dict(ex)
        diff = ex.pop(TWIN_DIFF_INPUT)
        return grad_fn(weights, diff, {**shared, **ex}, loss_target)

    if N_MICROBATCH == 1:
        loss, (grad_w, grad_x) = one_microbatch(per_example, given["loss_target"])
    else:
        def body(carry, xs):
            loss_sum, grad_sum = carry
            l_k, (gw_k, gx_k) = one_microbatch(xs[0], xs[1])
            with _jax.named_scope("update"):
                return (loss_sum + l_k, _jax.tree.map(_jnp.add, grad_sum, gw_k)), gx_k

        init = (_jnp.zeros((), _jnp.float32), _jax.tree.map(_jnp.zeros_like, weights))
        (loss, grad_w), grad_x = _jax.lax.scan(body, init, (per_example, given["loss_target"]))
    with _jax.named_scope("update"):
        delta_w, new_m, new_v = {}, {}, {}
        for n in TWIN_WEIGHTS:
            delta_w[n], new_m[n], new_v[n] = _adamw(weights[n], grad_w[n], given["m_" + n], given["v_" + n])
    return (loss, grad_x, *[grad_w[n] for n in TWIN_WEIGHTS], *[delta_w[n] for n in TWIN_WEIGHTS],
            *[new_m[n] for n in TWIN_WEIGHTS], *[new_v[n] for n in TWIN_WEIGHTS])
```

```python
import functools
import math

import jax
import jax.numpy as jnp
import numpy as np
from jax import lax
from jax.experimental import pallas as pl
from jax.experimental.pallas import tpu as pltpu

F32 = jnp.float32
BF16 = jnp.bfloat16
BS = pl.BlockSpec
SDS = jax.ShapeDtypeStruct
MESH = pl.DeviceIdType.MESH

D_MODEL = 1024
DEPTH = 4
SSM_HEADS = 16
SSM_WIDTH = 1024
BC_WIDTH = 256
CONV_CH = 1536
SSM_CONV = 7
CHUNK = 128
ATTN_HEADS = 16
KV_HEADS = 4
HEAD_DIM = 64
WINDOW = 128
BLOCK = 128
KEY_SPAN = 384
REL_BUCKETS = 32
REL_MAX_DIST = 128
D_FF = 2816
FFN_CONV = 3
NORM_EPS = 1e-6
Z_END = 1024
XBC_END = 2560
DT_END = 2592
Q_END = 3616
K_END = 3872
IN_COLS = 4128
P_COLS = 4224
ADAM_LR, ADAM_B1, ADAM_B2, ADAM_EPS, ADAM_WD, ADAM_STEP = 0.001, 0.9, 0.999, 1e-08, 0.01, 10
NEG = -1e30
N_DEV = 8
N_CHIP = 4
LANE = 128
VMEM_LIMIT_BYTES = 48 * 1024 * 1024


def _pc(body, *, name, grid, in_specs, out_specs, out_shape, scratch_shapes=()):
    return pl.pallas_call(
        body, name=name, grid=grid, in_specs=in_specs, out_specs=out_specs, out_shape=out_shape,
        scratch_shapes=list(scratch_shapes),
        compiler_params=pltpu.CompilerParams(dimension_semantics=("arbitrary",) * len(grid),
                                             vmem_limit_bytes=VMEM_LIMIT_BYTES))


def _div_tile(n, pref, mult):
    t = min(pref, n)
    t -= t % mult
    while t >= mult:
        if n % t == 0:
            return t
        t -= mult
    return n


def _mm(a, b, *, name, ta=False, tb=False, add=None, out_dtype=F32, tm=1024, tn=1024, tk=512):
    if ta:
        K, M = a.shape
    else:
        M, K = a.shape
    N = b.shape[0] if tb else b.shape[1]
    tm, tn, tk = _div_tile(M, tm, LANE), _div_tile(N, tn, LANE), _div_tile(K, tk, LANE)
    nk = K // tk
    dims = (((0,) if ta else (1,), (1,) if tb else (0,)), ((), ()))

    def body(*refs):
        if add is None:
            a_ref, b_ref, o_ref, acc_ref = refs
        else:
            a_ref, b_ref, add_ref, o_ref, acc_ref = refs
        k = pl.program_id(2)

        @pl.when(k == 0)
        def _():
            acc_ref[...] = jnp.zeros_like(acc_ref)

        acc_ref[...] += lax.dot_general(a_ref[...], b_ref[...], dims, preferred_element_type=F32)

        @pl.when(k == nk - 1)
        def _():
            r = acc_ref[...]
            if add is not None:
                r = r + add_ref[...]
            o_ref[...] = r.astype(out_dtype)

    a_spec = BS((tk, tm), lambda i, j, k: (k, i)) if ta else BS((tm, tk), lambda i, j, k: (i, k))
    b_spec = BS((tn, tk), lambda i, j, k: (j, k)) if tb else BS((tk, tn), lambda i, j, k: (k, j))
    in_specs, args = [a_spec, b_spec], [a, b]
    if add is not None:
        in_specs.append(BS((tm, tn), lambda i, j, k: (i, j)))
        args.append(add)
    return _pc(body, name=name, grid=(M // tm, N // tn, nk), in_specs=in_specs,
               out_specs=BS((tm, tn), lambda i, j, k: (i, j)), out_shape=SDS((M, N), out_dtype),
               scratch_shapes=[pltpu.VMEM((tm, tn), F32)])(*args)


def _dot(a, b, dims):
    return lax.dot_general(a.astype(BF16), b.astype(BF16), (dims, ((), ())), preferred_element_type=F32)


@jax.custom_vjp
def _nn(a, b):
    return _dot(a, b, ((1,), (0,)))


@jax.custom_vjp
def _nt(a, b):
    return _dot(a, b, ((1,), (1,)))


@jax.custom_vjp
def _tn(a, b):
    return _dot(a, b, ((0,), (0,)))


_nn.defvjp(lambda a, b: (_nn(a, b), (a, b)), lambda r, g: (_nt(g, r[1]), _tn(r[0], g)))
_nt.defvjp(lambda a, b: (_nt(a, b), (a, b)), lambda r, g: (_nn(g, r[1]), _tn(g, r[0])))
_tn.defvjp(lambda a, b: (_tn(a, b), (a, b)), lambda r, g: (_nt(r[1], g), _nn(r[0], g)))


def _hdot(m, x):
    return lax.dot_general(m, x, (((1,), (0,)), ((), ())), precision=lax.Precision.HIGHEST, preferred_element_type=F32)


@jax.custom_vjp
def _cumdot(m, mt, x):
    return _hdot(m, x)


_cumdot.defvjp(lambda m, mt, x: (_hdot(m, x), (m, mt)),
               lambda r, g: (jnp.zeros_like(r[0]), jnp.zeros_like(r[1]), _hdot(r[1], g)))


def _sigmoid(x):
    return 1.0 / (1.0 + jnp.exp(-x))


def _softplus(x):
    return jnp.maximum(x, 0.0) + jnp.log(1.0 + jnp.exp(-jnp.abs(x)))


def _rms(x, w):
    return x * lax.rsqrt(jnp.mean(x * x, axis=-1, keepdims=True) + NORM_EPS) * w


def _rmsnorm_fwd(x2, w, *, name):
    T, D = x2.shape
    tr = _div_tile(T, 512, 8)

    def body(x_ref, w_ref, o_ref):
        o_ref[...] = _rms(x_ref[...], w_ref[...]).astype(BF16)

    return _pc(body, name=name, grid=(T // tr,),
               in_specs=[BS((tr, D), lambda i: (i, 0)), BS((1, D), lambda i: (0, 0))],
               out_specs=BS((tr, D), lambda i: (i, 0)), out_shape=SDS((T, D), BF16))(x2, w.reshape(1, D))


def _rmsnorm_bwd(x2, w, dh, resid, *, name):
    T, D = x2.shape
    tr = _div_tile(T, 512, 8)

    def body(x_ref, w_ref, dh_ref, r_ref, dx_ref, dw_ref):
        _, vjp = jax.vjp(_rms, x_ref[...], w_ref[...])
        dx, dw = vjp(dh_ref[...])
        dx_ref[...] = dx + r_ref[...]

        @pl.when(pl.program_id(0) == 0)
        def _():
            dw_ref[...] = jnp.zeros_like(dw_ref)

        dw_ref[...] += dw

    row = BS((tr, D), lambda i: (i, 0))
    one = BS((1, D), lambda i: (0, 0))
    return _pc(body, name=name, grid=(T // tr,), in_specs=[row, one, row, row], out_specs=[row, one],
               out_shape=[SDS((T, D), F32), SDS((1, D), F32)])(x2, w.reshape(1, D), dh, resid)


def _shift_rows(x, s):
    if s == 0:
        return x
    n = x.shape[0]
    t = lax.broadcasted_iota(jnp.int32, (n, 1), 0)
    r = pltpu.roll(x, (-s) % n, 0)
    return jnp.where((t + s >= 0) & (t + s < n), r, 0.0)


def _conv_pre(x, w_ref, b_ref, taps):
    c = b_ref[...] + jnp.zeros_like(x)
    for k in range(taps):
        c = c + w_ref[k:k + 1, :] * _shift_rows(x, k - taps // 2)
    return c


def _conv_fwd(x3, x_blk0, w, b, *, taps, ct, gate_blk0=None, out_dtype, name):
    B, L, _ = x3.shape
    C = w.shape[1]
    wp = jnp.zeros((8, C), F32).at[:taps].set(w)

    def body(*refs):
        if gate_blk0 is None:
            x_ref, w_ref, b_ref, o_ref = refs
        else:
            x_ref, u_ref, w_ref, b_ref, o_ref = refs
        c = _conv_pre(x_ref[0], w_ref, b_ref, taps)
        y = c * _sigmoid(c)
        if gate_blk0 is not None:
            y = y * u_ref[0]
        o_ref[0] = y.astype(out_dtype)

    in_specs = [BS((1, L, ct), lambda bi, j: (bi, 0, x_blk0 + j))]
    args = [x3]
    if gate_blk0 is not None:
        in_specs.append(BS((1, L, ct), lambda bi, j: (bi, 0, gate_blk0 + j)))
        args.append(x3)
    in_specs += [BS((8, ct), lambda bi, j: (0, j)), BS((1, ct), lambda bi, j: (0, j))]
    args += [wp, b.reshape(1, C)]
    return _pc(body, name=name, grid=(B, C // ct), in_specs=in_specs,
               out_specs=BS((1, L, ct), lambda bi, j: (bi, 0, j)), out_shape=SDS((B, L, C), out_dtype))(*args)


def _conv_bwd(x3, x_blk0, w, b, dy3, *, taps, ct, gate_blk0=None, name):
    B, L, _ = x3.shape
    C = w.shape[1]
    wp = jnp.zeros((8, C), F32).at[:taps].set(w)
    gated = gate_blk0 is not None

    def body(*refs):
        if gated:
            x_ref, u_ref, w_ref, b_ref, dy_ref, dx_ref, du_ref, dw_ref, db_ref = refs
        else:
            x_ref, w_ref, b_ref, dy_ref, dx_ref, dw_ref, db_ref = refs
        x = x_ref[0]
        dy = dy_ref[0].astype(F32)
        c = _conv_pre(x, w_ref, b_ref, taps)
        sg = _sigmoid(c)
        dsilu = sg * (1.0 + c * (1.0 - sg))
        if gated:
            du_ref[0] = (dy * (c * sg)).astype(BF16)
            dc = dy * u_ref[0] * dsilu
        else:
            dc = dy * dsilu
        dx = jnp.zeros_like(x)
        dw_ref[0] = jnp.zeros((8, ct), F32)
        for k in range(taps):
            s = k - taps // 2
            dx = dx + w_ref[k:k + 1, :] * _shift_rows(dc, -s)
            dw_ref[0, k:k + 1, :] = jnp.sum(dc * _shift_rows(x, s), axis=0, keepdims=True)
        dx_ref[0] = dx.astype(BF16)
        db_ref[0] = jnp.sum(dc, axis=0, keepdims=True)

    xs = BS((1, L, ct), lambda bi, j: (bi, 0, x_blk0 + j))
    ys = BS((1, L, ct), lambda bi, j: (bi, 0, j))
    in_specs, args = [xs], [x3]
    if gated:
        in_specs.append(BS((1, L, ct), lambda bi, j: (bi, 0, gate_blk0 + j)))
        args.append(x3)
    in_specs += [BS((8, ct), lambda bi, j: (0, j)), BS((1, ct), lambda bi, j: (0, j)), ys]
    args += [wp, b.reshape(1, C), dy3]
    out_specs = [ys] + ([ys] if gated else []) + [BS((1, 8, ct), lambda bi, j: (bi, 0, j)), BS((1, 1, ct), lambda bi, j: (bi, 0, j))]
    out_shape = [SDS((B, L, C), BF16)] + ([SDS((B, L, C), BF16)] if gated else []) + [SDS((B, 8, C), F32), SDS((B, 1, C), F32)]
    return _pc(body, name=name, grid=(B, C // ct), in_specs=in_specs, out_specs=out_specs, out_shape=out_shape)(*args)


def _tri(reverse):
    r = lax.broadcasted_iota(jnp.int32, (CHUNK, CHUNK), 0)
    c = lax.broadcasted_iota(jnp.int32, (CHUNK, CHUNK), 1)
    return (c >= r) if reverse else (c <= r)


def _ssd_chunk(hprev, xs, bm, cm, dtraw, dtb, alog, *, col0, reverse):
    mask = _tri(reverse)
    tri = mask.astype(F32)
    trit = _tri(not reverse).astype(F32)
    eye = lax.broadcasted_iota(jnp.int32, (CHUNK, CHUNK), 0) == lax.broadcasted_iota(jnp.int32, (CHUNK, CHUNK), 1)
    lane = lax.broadcasted_iota(jnp.int32, (1, LANE), 1)
    first = lane < HEAD_DIM
    dtc = _softplus(dtraw + dtb)
    adt = dtc * (-jnp.exp(alog))

    def col(v, c):
        return jnp.sum(jnp.where(lane == c, v, 0.0), axis=1, keepdims=True)

    dte = jnp.where(first, col(dtc, col0), col(dtc, col0 + 1))
    adte = jnp.where(first, col(adt, col0), col(adt, col0 + 1))
    cum = _cumdot(tri, trit, adte)
    tot = jnp.sum(adte, axis=0, keepdims=True)
    xdt = xs * dte
    cb = _nt(cm, bm)
    y = _nn(cm, hprev) * jnp.exp(cum)
    for j in range(2):
        cj = col(cum, HEAD_DIM * j)
        rj = jnp.sum(jnp.where(eye, cj, 0.0), axis=0, keepdims=True)
        dec = jnp.exp(jnp.where(mask, cj - rj, NEG))
        y = y + _nn(cb * dec, jnp.where(first if j == 0 else ~first, xdt, 0.0))
    hnext = hprev * jnp.exp(tot) + _tn(bm, xdt * jnp.exp(tot - cum))
    return y, hnext


def _ssd_specs(B, L):
    def lanes(blk):
        return BS((1, L, LANE), blk)

    return [
        lanes(lambda b, p: (b, 0, p)),
        lanes(lambda b, p: (b, 0, 8 + p // 4)),
        lanes(lambda b, p: (b, 0, 10 + p // 4)),
        lanes(lambda b, p: (b, 0, P_COLS // LANE - 1)),
        BS((1, LANE), lambda b, p: (0, 0)),
        BS((1, LANE), lambda b, p: (0, 0)),
        BS((1, LANE), lambda b, p: (0, p)),
    ]


def _ssd_fwd(xbc_act, proj, dtb, alog, dskip, *, name):
    B, L, _ = xbc_act.shape
    nc = L // CHUNK

    def body(xs_ref, b_ref, c_ref, dt_ref, dtb_ref, alog_ref, dsk_ref, y_ref, hs_ref):
        p = pl.program_id(1)
        dtb_v, alog_v = dtb_ref[...], alog_ref[...]
        for d in range(2):
            def step(i, h, d=d):
                ci = i if d == 0 else nc - 1 - i
                rows = pl.ds(pl.multiple_of(ci * CHUNK, CHUNK), CHUNK)
                xs = xs_ref[0, rows, :]
                y, hn = _ssd_chunk(h, xs, b_ref[0, rows, :], c_ref[0, rows, :], dt_ref[0, rows, :], dtb_v, alog_v,
                                   col0=SSM_HEADS * d + 2 * p, reverse=d == 1)
                hs_ref[0, 0, d, ci] = h
                if d == 0:
                    y_ref[0, rows, :] = y + dsk_ref[...] * xs
                else:
                    y_ref[0, rows, :] += y
                return hn

            lax.fori_loop(0, nc, step, jnp.zeros((LANE, LANE), F32))

    return _pc(body, name=name, grid=(B, 8), in_specs=_ssd_specs(B, L),
               out_specs=[BS((1, L, LANE), lambda b, p: (b, 0, p)),
                          BS((1, 1, 2, nc, LANE, LANE), lambda b, p: (b, p, 0, 0, 0, 0))],
               out_shape=[SDS((B, L, SSM_WIDTH), F32), SDS((B, 8, 2, nc, LANE, LANE), F32)])(
        xbc_act, xbc_act, xbc_act, proj, dtb, alog, dskip)


def _ssd_bwd(xbc_act, proj, dtb, alog, dskip, hs, dy, *, name):
    B, L, _ = xbc_act.shape
    nc = L // CHUNK

    def body(xs_ref, b_ref, c_ref, dt_ref, dtb_ref, alog_ref, dsk_ref, hs_ref, dy_ref,
             dxs_ref, db_ref, dc_ref, ddt_ref, ddtb_ref, dalog_ref, ddsk_ref):
        p = pl.program_id(1)
        dtb_v, alog_v = dtb_ref[...], alog_ref[...]

        @pl.when(p % 4 == 0)
        def _():
            db_ref[...] = jnp.zeros_like(db_ref)
            dc_ref[...] = jnp.zeros_like(dc_ref)

        @pl.when(p == 0)
        def _():
            ddt_ref[...] = jnp.zeros_like(ddt_ref)

        dxs_ref[0] = dy_ref[0] * dsk_ref[...]
        ddsk_ref[0] = jnp.sum(dy_ref[0] * xs_ref[0], axis=0, keepdims=True)
        zero_row = jnp.zeros((1, LANE), F32)
        acc = (zero_row, zero_row)
        for d in range(2):
            def step(i, carry, d=d):
                dh, g_dtb, g_alog = carry
                ci = nc - 1 - i if d == 0 else i
                rows = pl.ds(pl.multiple_of(ci * CHUNK, CHUNK), CHUNK)
                fn = functools.partial(_ssd_chunk, col0=SSM_HEADS * d + 2 * p, reverse=d == 1)
                _, vjp = jax.vjp(fn, hs_ref[0, 0, d, ci], xs_ref[0, rows, :], b_ref[0, rows, :], c_ref[0, rows, :],
                                 dt_ref[0, rows, :], dtb_v, alog_v)
                g_h, g_xs, g_b, g_c, g_dt, g_dtb1, g_alog1 = vjp((dy_ref[0, rows, :], dh))
                dxs_ref[0, rows, :] += g_xs
                db_ref[0, rows, :] += g_b
                dc_ref[0, rows, :] += g_c
                ddt_ref[0, rows, :] += g_dt
                return g_h, g_dtb + g_dtb1, g_alog + g_alog1

            _, a0, a1 = lax.fori_loop(0, nc, step, (jnp.zeros((LANE, LANE), F32),) + acc)
            acc = (a0, a1)
        ddtb_ref[0, 0] = acc[0]
        dalog_ref[0, 0] = acc[1]

    lanes = lambda blk: BS((1, L, LANE), blk)
    in_specs = _ssd_specs(B, L) + [BS((1, 1, 2, nc, LANE, LANE), lambda b, p: (b, p, 0, 0, 0, 0)), lanes(lambda b, p: (b, 0, p))]
    out_specs = [lanes(lambda b, p: (b, 0, p)), lanes(lambda b, p: (b, 0, p // 4)), lanes(lambda b, p: (b, 0, p // 4)),
                 lanes(lambda b, p: (b, 0, 0)), BS((1, 1, 1, LANE), lambda b, p: (b, p, 0, 0)),
                 BS((1, 1, 1, LANE), lambda b, p: (b, p, 0, 0)), BS((1, 1, LANE), lambda b, p: (b, 0, p))]
    out_shape = [SDS((B, L, SSM_WIDTH), F32), SDS((B, L, BC_WIDTH), F32), SDS((B, L, BC_WIDTH), F32), SDS((B, L, LANE), F32),
                 SDS((B, 8, 1, LANE), F32), SDS((B, 8, 1, LANE), F32), SDS((B, 1, SSM_WIDTH), F32)]
    return _pc(body, name=name, grid=(B, 8), in_specs=in_specs, out_specs=out_specs, out_shape=out_shape)(
        xbc_act, xbc_act, xbc_act, proj, dtb, alog, dskip, hs, dy)


def _gate_norm(yp, z, w):
    v = yp * (z * _sigmoid(z))
    return v * lax.rsqrt(jnp.mean(v * v, axis=-1, keepdims=True) + NORM_EPS) * w


def _gate_fwd(ypre2, proj2, w, *, name):
    T = ypre2.shape[0]
    tr = _div_tile(T, 512, 8)
    G = 512

    def body(y_ref, z_ref, w_ref, o_ref):
        o_ref[...] = _gate_norm(y_ref[...], z_ref[...], w_ref[...]).astype(BF16)

    return _pc(body, name=name, grid=(T // tr, 2),
               in_specs=[BS((tr, G), lambda i, g: (i, g)), BS((tr, G), lambda i, g: (i, 2 + g)), BS((1, G), lambda i, g: (0, g))],
               out_specs=BS((tr, G), lambda i, g: (i, g)), out_shape=SDS((T, SSM_WIDTH), BF16))(ypre2, proj2, w.reshape(1, -1))


def _gate_bwd(ypre2, proj2, w, dy, *, name):
    T = ypre2.shape[0]
    tr = _div_tile(T, 512, 8)
    G = 512

    def body(y_ref, z_ref, w_ref, dy_ref, dyp_ref, dz_ref, dw_ref):
        _, vjp = jax.vjp(_gate_norm, y_ref[...], z_ref[...], w_ref[...])
        dyp, dz, dw = vjp(dy_ref[...])
        dyp_ref[...] = dyp
        dz_ref[...] = dz.astype(BF16)
        dw_ref[0] = dw

    tile = BS((tr, G), lambda i, g: (i, g))
    return _pc(body, name=name, grid=(T // tr, 2),
               in_specs=[tile, BS((tr, G), lambda i, g: (i, 2 + g)), BS((1, G), lambda i, g: (0, g)), tile],
               out_specs=[tile, tile, BS((1, 1, G), lambda i, g: (i, 0, g))],
               out_shape=[SDS((T, SSM_WIDTH), F32), SDS((T, SSM_WIDTH), BF16), SDS((T // tr, 1, SSM_WIDTH), F32)])(
        ypre2, proj2, w.reshape(1, -1), dy)


def _attn_block(q, k, v, bias, sink, valid):
    s = _nt(q, k) * (HEAD_DIM ** -0.5) + bias
    s = jnp.where(valid, s, NEG)
    m = jnp.maximum(jnp.max(s, axis=-1, keepdims=True), sink)
    p = jnp.exp(s - m)
    den = jnp.sum(p, axis=-1, keepdims=True) + jnp.exp(sink - m)
    return _nn(p / den, v)


def _attn_valid(n, L):
    i = lax.broadcasted_iota(jnp.int32, (4 * BLOCK, KEY_SPAN), 0) % BLOCK
    j = lax.broadcasted_iota(jnp.int32, (4 * BLOCK, KEY_SPAN), 1)
    kpos = n * BLOCK - WINDOW + j
    return (jnp.abs(j - WINDOW - i) <= WINDOW) & (kpos >= 0) & (kpos < L)


def _attn_in_specs():
    kv = lambda o: BS((1, 1, BLOCK, HEAD_DIM), lambda g, b, n, o=o: (b, g, n + o, 0))
    return [BS((1, 4, BLOCK, HEAD_DIM), lambda g, b, n: (b, g, n, 0)), kv(0), kv(1), kv(2), kv(0), kv(1), kv(2),
            BS((4, BLOCK, KEY_SPAN), lambda g, b, n: (g, 0, 0)), BS((4 * BLOCK, 1), lambda g, b, n: (g, 0))]


def _attn_fwd(qt, kp, vp, bias, sinkcol, *, name):
    B, _, L, _ = qt.shape

    def body(q_ref, k0, k1, k2, v0, v1, v2, bias_ref, sink_ref, o_ref):
        n = pl.program_id(2)
        k = jnp.concatenate([k0[0, 0], k1[0, 0], k2[0, 0]], axis=0)
        v = jnp.concatenate([v0[0, 0], v1[0, 0], v2[0, 0]], axis=0)
        out = _attn_block(q_ref[0].reshape(4 * BLOCK, HEAD_DIM), k, v, bias_ref[...].reshape(4 * BLOCK, KEY_SPAN),
                          sink_ref[...], _attn_valid(n, L))
        o_ref[0] = out.reshape(4, BLOCK, HEAD_DIM).astype(BF16)

    return _pc(body, name=name, grid=(KV_HEADS, B, L // BLOCK), in_specs=_attn_in_specs(),
               out_specs=BS((1, 4, BLOCK, HEAD_DIM), lambda g, b, n: (b, g, n, 0)),
               out_shape=SDS((B, ATTN_HEADS, L, HEAD_DIM), BF16))(qt, kp, kp, kp, vp, vp, vp, bias, sinkcol)


def _attn_bwd(qt, kp, vp, bias, sinkcol, dout, *, name):
    B, _, L, _ = qt.shape
    LP = L + 2 * WINDOW

    def body(q_ref, k0, k1, k2, v0, v1, v2, bias_ref, sink_ref, do_ref, dq_ref, dk_ref, dv_ref, dbias_ref, dsink_ref):
        b, n = pl.program_id(1), pl.program_id(2)
        k = jnp.concatenate([k0[0, 0], k1[0, 0], k2[0, 0]], axis=0).astype(F32)
        v = jnp.concatenate([v0[0, 0], v1[0, 0], v2[0, 0]], axis=0).astype(F32)
        fn = functools.partial(_attn_block, valid=_attn_valid(n, L))
        _, vjp = jax.vjp(fn, q_ref[0].reshape(4 * BLOCK, HEAD_DIM).astype(F32), k, v,
                         bias_ref[...].reshape(4 * BLOCK, KEY_SPAN), sink_ref[...])
        dq, dk, dv, dbias, dsink = vjp(do_ref[0].reshape(4 * BLOCK, HEAD_DIM).astype(F32))
        dq_ref[0] = dq.reshape(4, BLOCK, HEAD_DIM).astype(BF16)

        @pl.when(n == 0)
        def _():
            dk_ref[...] = jnp.zeros_like(dk_ref)
            dv_ref[...] = jnp.zeros_like(dv_ref)

        @pl.when((n == 0) & (b == 0))
        def _():
            dbias_ref[...] = jnp.zeros_like(dbias_ref)
            dsink_ref[...] = jnp.zeros_like(dsink_ref)

        rows = pl.ds(pl.multiple_of(n * BLOCK, BLOCK), KEY_SPAN)
        dk_ref[0, 0, rows, :] += dk
        dv_ref[0, 0, rows, :] += dv
        dbias_ref[...] += dbias.reshape(4, BLOCK, KEY_SPAN)
        dsink_ref[...] += dsink

    qspec = BS((1, 4, BLOCK, HEAD_DIM), lambda g, b, n: (b, g, n, 0))
    kvout = BS((1, 1, LP, HEAD_DIM), lambda g, b, n: (b, g, 0, 0))
    return _pc(body, name=name, grid=(KV_HEADS, B, L // BLOCK), in_specs=_attn_in_specs() + [qspec],
               out_specs=[qspec, kvout, kvout, BS((4, BLOCK, KEY_SPAN), lambda g, b, n: (g, 0, 0)),
                          BS((4 * BLOCK, 1), lambda g, b, n: (g, 0))],
               out_shape=[SDS((B, ATTN_HEADS, L, HEAD_DIM), BF16), SDS((B, KV_HEADS, LP, HEAD_DIM), F32),
                          SDS((B, KV_HEADS, LP, HEAD_DIM), F32), SDS((ATTN_HEADS, BLOCK, KEY_SPAN), F32),
                          SDS((ATTN_HEADS * BLOCK, 1), F32)])(qt, kp, kp, kp, vp, vp, vp, bias, sinkcol, dout)


def _t5_bucket(rel):
    half = REL_BUCKETS // 2
    max_exact = half // 2
    ret = jnp.where(rel > 0, half, 0)
    n = jnp.abs(rel)
    nf = jnp.maximum(n, 1).astype(F32)
    large = max_exact + (jnp.log(nf / max_exact) / math.log(REL_MAX_DIST / max_exact) * (half - max_exact)).astype(jnp.int32)
    large = jnp.minimum(large, half - 1)
    return ret + jnp.where(n < max_exact, n, large)


def _bucket_table():
    rel = jnp.arange(KEY_SPAN)[None, :] - WINDOW - jnp.arange(BLOCK)[:, None]
    return _t5_bucket(rel).astype(jnp.int32)


def _bias_expand(rel_bias, bucket, *, name):
    rbt = jnp.zeros((ATTN_HEADS, 1, LANE), F32).at[:, 0, :REL_BUCKETS].set(rel_bias.T)

    def body(rb_ref, bk_ref, o_ref):
        lane = lax.broadcasted_iota(jnp.int32, (1, LANE), 1)
        row = rb_ref[0]
        bk = bk_ref[...]
        acc = jnp.zeros((BLOCK, KEY_SPAN), F32)
        for r in range(REL_BUCKETS):
            val = jnp.sum(jnp.where(lane == r, row, 0.0), axis=1, keepdims=True)
            acc = jnp.where(bk == r, val, acc)
        o_ref[0] = acc

    return _pc(body, name=name, grid=(ATTN_HEADS,),
               in_specs=[BS((1, 1, LANE), lambda h: (h, 0, 0)), BS((BLOCK, KEY_SPAN), lambda h: (0, 0))],
               out_specs=BS((1, BLOCK, KEY_SPAN), lambda h: (h, 0, 0)), out_shape=SDS((ATTN_HEADS, BLOCK, KEY_SPAN), F32))(rbt, bucket)


def _bias_reduce(dbias, bucket, *, name):
    def body(db_ref, bk_ref, o_ref):
        lane = lax.broadcasted_iota(jnp.int32, (1, LANE), 1)
        x = db_ref[0]
        bk = bk_ref[...]
        acc = jnp.zeros((1, LANE), F32)
        for r in range(REL_BUCKETS):
            part = jnp.sum(jnp.where(bk == r, x, 0.0), axis=1, keepdims=True)
            acc = jnp.where(lane == r, jnp.sum(part, axis=0, keepdims=True), acc)
        o_ref[0] = acc

    out = _pc(body, name=name, grid=(ATTN_HEADS,),
              in_specs=[BS((1, BLOCK, KEY_SPAN), lambda h: (h, 0, 0)), BS((BLOCK, KEY_SPAN), lambda h: (0, 0))],
              out_specs=BS((1, 1, LANE), lambda h: (h, 0, 0)), out_shape=SDS((ATTN_HEADS, 1, LANE), F32))(dbias, bucket)
    return out[:, 0, :REL_BUCKETS].T


def _loss_head(x2, w, target, *, name):
    T, D = x2.shape
    tr = _div_tile(T, 512, 8)

    def tile_loss(x, w, t):
        err = _rms(x, w) - t
        return 0.5 * jnp.sum(jnp.mean(err * err, axis=-1, keepdims=True), axis=0, keepdims=True)

    def body(x_ref, w_ref, t_ref, loss_ref, dx_ref, dw_ref):
        t = t_ref[...]
        l, vjp = jax.vjp(lambda x, w: tile_loss(x, w, t), x_ref[...], w_ref[...])
        dx, dw = vjp(jnp.ones((1, 1), F32))
        dx_ref[...] = dx

        @pl.when(pl.program_id(0) == 0)
        def _():
            dw_ref[...] = jnp.zeros_like(dw_ref)
            loss_ref[...] = jnp.zeros_like(loss_ref)

        dw_ref[...] += dw
        loss_ref[...] += l + jnp.zeros((1, LANE), F32)

    row = BS((tr, D), lambda i: (i, 0))
    one = BS((1, D), lambda i: (0, 0))
    return _pc(body, name=name, grid=(T // tr,), in_specs=[row, one, row],
               out_specs=[BS((1, LANE), lambda i: (0, 0)), row, one],
               out_shape=[SDS((1, LANE), F32), SDS((T, D), F32), SDS((1, D), F32)])(x2, w.reshape(1, D), target)


def _adamw(w2, g2, m2, v2, *, name):
    R, C = w2.shape
    tr = _div_tile(R, 256, 8)
    c1 = 1.0 - ADAM_B1 ** ADAM_STEP
    c2 = 1.0 - ADAM_B2 ** ADAM_STEP

    def body(w_ref, g_ref, m_ref, v_ref, d_ref, nm_ref, nv_ref):
        g = g_ref[...]
        m = ADAM_B1 * m_ref[...] + (1.0 - ADAM_B1) * g
        v = ADAM_B2 * v_ref[...] + (1.0 - ADAM_B2) * (g * g)
        d_ref[...] = -ADAM_LR * ((m / c1) / (jnp.sqrt(v / c2) + ADAM_EPS) + ADAM_WD * w_ref[...])
        nm_ref[...] = m
        nv_ref[...] = v

    t = BS((tr, C), lambda i: (i, 0))
    return _pc(body, name=name, grid=(R // tr,), in_specs=[t, t, t, t], out_specs=[t, t, t],
               out_shape=[SDS((R, C), F32)] * 3)(w2, g2, m2, v2)


def _place():
    return lax.axis_index("x"), lax.axis_index("y"), lax.axis_index("c")


def _gather_weights(shards, *, name):
    na = len(shards)
    half = DEPTH // 2

    def body(*refs):
        ins, outs = refs[:na], refs[na:2 * na]
        send_sems, recv_sems, local_sems = refs[2 * na:]
        x, y, c = _place()
        me_chip = 2 * x + y
        chips = [(1 - x, y), (x, 1 - y), (1 - x, 1 - y)]
        mine = pl.ds(c * half, half)
        theirs = pl.ds((1 - c) * half, half)
        started = []
        for a in range(na):
            local = pltpu.make_async_copy(ins[a], outs[a].at[me_chip], local_sems.at[a])
            local.start()
            started.append(local)
        sends = []
        for a in range(na):
            for k, (px, py) in enumerate(chips):
                cp = pltpu.make_async_remote_copy(ins[a].at[mine], outs[a].at[me_chip, mine], send_sems.at[a, k], recv_sems.at[a, k],
                                                  device_id=(px, py, c), device_id_type=MESH)
                cp.start()
                sends.append(cp)
        for a in range(na):
            for k, (px, py) in enumerate(chips):
                got = outs[a].at[2 * px + py, mine]
                pltpu.make_async_remote_copy(got, got, send_sems.at[a, k], recv_sems.at[a, k],
                                             device_id=(px, py, c), device_id_type=MESH).wait_recv()
                cp = pltpu.make_async_remote_copy(got, got, send_sems.at[a, 3 + k], recv_sems.at[a, 3 + k],
                                                  device_id=(x, y, 1 - c), device_id_type=MESH)
                cp.start()
                sends.append(cp)
        for a in range(na):
            for k, (px, py) in enumerate(chips):
                got = outs[a].at[2 * px + py, theirs]
                pltpu.make_async_remote_copy(got, got, send_sems.at[a, 3 + k], recv_sems.at[a, 3 + k],
                                             device_id=(x, y, 1 - c), device_id_type=MESH).wait_recv()
        for cp in sends:
            cp.wait_send()
        for cp in started:
            cp.wait()

    any_spec = BS(memory_space=pl.ANY)
    return pl.pallas_call(
        body, name=name, in_specs=[any_spec] * na, out_specs=[any_spec] * na,
        out_shape=[SDS((N_CHIP,) + s.shape, s.dtype) for s in shards],
        scratch_shapes=[pltpu.SemaphoreType.DMA((na, 6)), pltpu.SemaphoreType.DMA((na, 6)), pltpu.SemaphoreType.DMA((na,))],
        compiler_params=pltpu.CompilerParams(has_side_effects=True))(*shards)


def _scatter_grads(bufs, *, name):
    na = len(bufs)
    half = DEPTH // 2

    def body(*refs):
        ins, outs = refs[:na], refs[na:2 * na]
        send_sems, recv_sems, local_sems = refs[2 * na:]
        x, y, c = _place()
        me = 4 * x + 2 * y + c
        started, sends = [], []
        for a in range(na):
            local = pltpu.make_async_copy(ins[a].at[2 * x + y, pl.ds(c * half, half)], outs[a].at[me], local_sems.at[a])
            local.start()
            started.append(local)
        for a in range(na):
            for r in range(1, N_DEV):
                tx, ty, tc = x ^ (r >> 2), y ^ ((r >> 1) & 1), c ^ (r & 1)
                cp = pltpu.make_async_remote_copy(ins[a].at[2 * tx + ty, pl.ds(tc * half, half)], outs[a].at[me],
                                                  send_sems.at[a, r - 1], recv_sems.at[a, r - 1],
                                                  device_id=(tx, ty, tc), device_id_type=MESH)
                cp.start()
                sends.append(cp)
        for a in range(na):
            for r in range(1, N_DEV):
                tx, ty, tc = x ^ (r >> 2), y ^ ((r >> 1) & 1), c ^ (r & 1)
                got = outs[a].at[4 * tx + 2 * ty + tc]
                pltpu.make_async_remote_copy(got, got, send_sems.at[a, r - 1], recv_sems.at[a, r - 1],
                                             device_id=(tx, ty, tc), device_id_type=MESH).wait_recv()
        for cp in sends:
            cp.wait_send()
        for cp in started:
            cp.wait()

    any_spec = BS(memory_space=pl.ANY)
    return pl.pallas_call(
        body, name=name, in_specs=[any_spec] * na, out_specs=[any_spec] * na,
        out_shape=[SDS((N_DEV, half) + b.shape[2:], b.dtype) for b in bufs],
        scratch_shapes=[pltpu.SemaphoreType.DMA((na, N_DEV - 1)), pltpu.SemaphoreType.DMA((na, N_DEV - 1)),
                        pltpu.SemaphoreType.DMA((na,))],
        compiler_params=pltpu.CompilerParams(has_side_effects=True))(*bufs)


def _sum_sources(parts, *, name):
    _, R, C = parts.shape
    tr = _div_tile(R, 256, 16)

    def body(p_ref, o_ref):
        acc = p_ref[0].astype(F32)
        for s in range(1, N_DEV):
            acc = acc + p_ref[s].astype(F32)
        o_ref[...] = acc

    return _pc(body, name=name, grid=(R // tr,), in_specs=[BS((N_DEV, tr, C), lambda i: (0, i, 0))],
               out_specs=BS((tr, C), lambda i: (i, 0)), out_shape=SDS((R, C), F32))(parts)


def _join_halves(halves, *, name):
    na = len(halves)
    half = DEPTH // 2

    def body(*refs):
        ins, outs = refs[:na], refs[na:2 * na]
        send_sems, recv_sems, local_sems = refs[2 * na:]
        x, y, c = _place()
        mine = pl.ds(c * half, half)
        theirs = pl.ds((1 - c) * half, half)
        cps = []
        for a in range(na):
            local = pltpu.make_async_copy(ins[a], outs[a].at[mine], local_sems.at[a])
            local.start()
            cp = pltpu.make_async_remote_copy(ins[a], outs[a].at[mine], send_sems.at[a], recv_sems.at[a],
                                              device_id=(x, y, 1 - c), device_id_type=MESH)
            cp.start()
            cps.append((local, cp))
        for a in range(na):
            got = outs[a].at[theirs]
            pltpu.make_async_remote_copy(got, got, send_sems.at[a], recv_sems.at[a],
                                         device_id=(x, y, 1 - c), device_id_type=MESH).wait_recv()
        for local, cp in cps:
            cp.wait_send()
            local.wait()

    any_spec = BS(memory_space=pl.ANY)
    return pl.pallas_call(
        body, name=name, in_specs=[any_spec] * na, out_specs=[any_spec] * na,
        out_shape=[SDS((DEPTH,) + h.shape[1:], h.dtype) for h in halves],
        scratch_shapes=[pltpu.SemaphoreType.DMA((na,)), pltpu.SemaphoreType.DMA((na,)), pltpu.SemaphoreType.DMA((na,))],
        compiler_params=pltpu.CompilerParams(has_side_effects=True))(*halves)


def _allreduce_small(vec, *, name):
    R = vec.shape[0]

    def body(v_ref, o_ref, all_ref, send_sems, recv_sems):
        x, y, c = _place()
        me = 4 * x + 2 * y + c
        all_ref[me] = v_ref[...]
        sends = []
        for r in range(1, N_DEV):
            tgt = (x ^ (r >> 2), y ^ ((r >> 1) & 1), c ^ (r & 1))
            cp = pltpu.make_async_remote_copy(v_ref, all_ref.at[me], send_sems.at[r - 1], recv_sems.at[r - 1],
                                              device_id=tgt, device_id_type=MESH)
            cp.start()
            sends.append(cp)
        for r in range(1, N_DEV):
            tx, ty, tc = x ^ (r >> 2), y ^ ((r >> 1) & 1), c ^ (r & 1)
            got = all_ref.at[4 * tx + 2 * ty + tc]
            pltpu.make_async_remote_copy(got, got, send_sems.at[r - 1], recv_sems.at[r - 1],
                                         device_id=(tx, ty, tc), device_id_type=MESH).wait_recv()
        for cp in sends:
            cp.wait_send()
        acc = all_ref[0]
        for s in range(1, N_DEV):
            acc = acc + all_ref[s]
        o_ref[...] = acc

    vm = BS(memory_space=pltpu.VMEM)
    return pl.pallas_call(
        body, name=name, in_specs=[vm], out_specs=vm, out_shape=SDS((R, LANE), F32),
        scratch_shapes=[pltpu.VMEM((N_DEV, R, LANE), F32), pltpu.SemaphoreType.DMA((N_DEV - 1,)), pltpu.SemaphoreType.DMA((N_DEV - 1,))],
        compiler_params=pltpu.CompilerParams(has_side_effects=True, vmem_limit_bytes=VMEM_LIMIT_BYTES))(vec)


def _pack(arrs):
    rows = []
    for a in arrs:
        f = a.reshape(-1).astype(F32)
        n = -(-f.shape[0] // LANE) * LANE
        rows.append(jnp.pad(f, (0, n - f.shape[0])).reshape(-1, LANE))
    v = jnp.concatenate(rows, axis=0)
    pad = -v.shape[0] % 8
    return jnp.pad(v, ((0, pad), (0, 0)))


def _unpack(v, shapes):
    out, r = [], 0
    for s in shapes:
        n = int(np.prod(s)) if len(s) else 1
        nr = -(-n // LANE)
        out.append(v[r:r + nr].reshape(-1)[:n].reshape(s))
        r += nr
    return out


def _perm_in_cols(w_full):
    z, xbc, dt, q, k, v = (w_full[..., :Z_END], w_full[..., Z_END:XBC_END], w_full[..., XBC_END:DT_END],
                           w_full[..., DT_END:Q_END], w_full[..., Q_END:K_END], w_full[..., K_END:])
    pad = jnp.zeros(dt.shape[:-1] + (LANE - dt.shape[-1],), dt.dtype)
    return jnp.concatenate([q, z, xbc, k, v, dt, pad], axis=-1)


def _unperm_in_cols(g):
    q, z, xbc, k, v, dt = (g[..., :1024], g[..., 1024:2048], g[..., 2048:3584], g[..., 3584:3840], g[..., 3840:4096],
                           g[..., 4096:4096 + 2 * SSM_HEADS])
    return jnp.concatenate([z, xbc, dt, q, k, v], axis=-1)


def _heads_major(a3, nh):
    B, L, _ = a3.shape
    return a3.reshape(B, L, nh, HEAD_DIM).transpose(0, 2, 1, 3)


def _tokens_major(a4):
    B, nh, L, _ = a4.shape
    return a4.transpose(0, 2, 1, 3).reshape(B, L, nh * HEAD_DIM)


def _pad_keys(a4):
    return jnp.pad(a4, ((0, 0), (0, 0), (WINDOW, WINDOW), (0, 0)))


def _dt_cols(a):
    return jnp.pad(a.reshape(1, 2 * SSM_HEADS), ((0, 0), (0, LANE - 2 * SSM_HEADS)))


def _layer_fwd(i, x, wts, small, band_bias):
    B, L, D = x.shape
    T = B * L
    x2 = x.reshape(T, D)
    h = _rmsnorm_fwd(x2, small["norm1_w"][i], name=f"norm1_{i}")
    proj2 = _mm(h, wts["w_in"][i], name=f"in_proj_{i}", tn=1408)
    proj = proj2.reshape(B, L, P_COLS)
    xbc_act = _conv_fwd(proj, 2048 // 256, small["conv_w"][i], small["conv_b"][i], taps=SSM_CONV, ct=256,
                        out_dtype=F32, name=f"ssm_conv_{i}")
    dtb, alog = _dt_cols(small["dt_bias"][i]), _dt_cols(small["a_log"][i])
    dskip = jnp.repeat(small["d_skip"][i], HEAD_DIM).reshape(1, SSM_WIDTH)
    ypre, hs = _ssd_fwd(xbc_act, proj, dtb, alog, dskip, name=f"ssd_{i}")
    y_ssm = _gate_fwd(ypre.reshape(T, SSM_WIDTH), proj2, small["ssm_norm_w"][i], name=f"gate_{i}")
    qt = _heads_major(proj[..., :1024].astype(BF16), ATTN_HEADS)
    kp = _pad_keys(_heads_major(proj[..., 3584:3840].astype(BF16), KV_HEADS))
    vp = _pad_keys(_heads_major(proj[..., 3840:4096].astype(BF16), KV_HEADS))
    sinkcol = jnp.repeat(small["attn_sink"][i], BLOCK).reshape(ATTN_HEADS * BLOCK, 1)
    y_attn = _tokens_major(_attn_fwd(qt, kp, vp, band_bias, sinkcol, name=f"attn_{i}")).reshape(T, D)
    w_out = wts["w_out"][i]
    x_mid = _mm(y_ssm, w_out[:SSM_WIDTH], add=x2, name=f"out_proj_a_{i}")
    x_mid = _mm(y_attn, w_out[SSM_WIDTH:], add=x_mid, name=f"out_proj_b_{i}")
    h2 = _rmsnorm_fwd(x_mid, small["norm2_w"][i], name=f"norm2_{i}")
    gu2 = _mm(h2, wts["w_up"][i], name=f"up_proj_{i}", tn=1408)
    gu = gu2.reshape(B, L, 2 * D_FF)
    act = _conv_fwd(gu, 0, small["ffn_conv_w"][i], small["ffn_conv_b"][i], taps=FFN_CONV, ct=256, gate_blk0=D_FF // 256,
                    out_dtype=BF16, name=f"ffn_conv_{i}")
    x_out = _mm(act.reshape(T, D_FF), wts["w_down"][i], add=x_mid, name=f"down_proj_{i}", tk=1408)
    saved = dict(x2=x2, h=h, proj2=proj2, xbc_act=xbc_act, dtb=dtb, alog=alog, dskip=dskip, ypre=ypre, hs=hs, y_ssm=y_ssm,
                 qt=qt, kp=kp, vp=vp, sinkcol=sinkcol, y_attn=y_attn, x_mid=x_mid, h2=h2, gu=gu, act=act)
    return x_out.reshape(B, L, D), saved


def _layer_bwd(i, dx_out, sv, wts, small, band_bias):
    T, D = dx_out.shape
    B, L = sv["gu"].shape[:2]
    g = {}
    dxb = dx_out.astype(BF16)
    dact = _mm(dxb, wts["w_down"][i], tb=True, out_dtype=BF16, name=f"d_act_{i}")
    g["w_down"] = _mm(sv["act"].reshape(T, D_FF), dxb, ta=True, name=f"dw_down_{i}", tm=1408)
    dg, du, dcw, dcb = _conv_bwd(sv["gu"], 0, small["ffn_conv_w"][i], small["ffn_conv_b"][i], dact.reshape(B, L, D_FF),
                                 taps=FFN_CONV, ct=256, gate_blk0=D_FF // 256, name=f"d_ffn_conv_{i}")
    g["ffn_conv_w"] = jnp.sum(dcw, axis=0)[:FFN_CONV]
    g["ffn_conv_b"] = jnp.sum(dcb, axis=(0, 1))
    dgu = jnp.concatenate([dg, du], axis=-1).reshape(T, 2 * D_FF)
    dh2 = _mm(dgu, wts["w_up"][i], tb=True, name=f"d_h2_{i}", tk=1408)
    g["w_up"] = _mm(sv["h2"], dgu, ta=True, name=f"dw_up_{i}", tn=1408)
    dx_mid, dw2 = _rmsnorm_bwd(sv["x_mid"], small["norm2_w"][i], dh2, dx_out, name=f"d_norm2_{i}")
    g["norm2_w"] = dw2[0]
    dmb = dx_mid.astype(BF16)
    w_out = wts["w_out"][i]
    dy_ssm = _mm(dmb, w_out[:SSM_WIDTH], tb=True, name=f"d_y_ssm_{i}")
    dy_attn = _mm(dmb, w_out[SSM_WIDTH:], tb=True, out_dtype=BF16, name=f"d_y_attn_{i}")
    g["w_out"] = jnp.concatenate([_mm(sv["y_ssm"], dmb, ta=True, name=f"dw_out_a_{i}"),
                                  _mm(sv["y_attn"], dmb, ta=True, name=f"dw_out_b_{i}")], axis=0)
    dypre, dz, dwn = _gate_bwd(sv["ypre"].reshape(T, SSM_WIDTH), sv["proj2"], small["ssm_norm_w"][i], dy_ssm, name=f"d_gate_{i}")
    g["ssm_norm_w"] = jnp.sum(dwn, axis=(0, 1))
    proj = sv["proj2"].reshape(B, L, P_COLS)
    dxs, dbm, dcm, ddt, ddtb, dalog, ddsk = _ssd_bwd(sv["xbc_act"], proj, sv["dtb"], sv["alog"], sv["dskip"], sv["hs"],
                                                    dypre.reshape(B, L, SSM_WIDTH), name=f"d_ssd_{i}")
    g["dt_bias"] = jnp.sum(ddtb, axis=(0, 1, 2))[:2 * SSM_HEADS].reshape(2, SSM_HEADS)
    g["a_log"] = jnp.sum(dalog, axis=(0, 1, 2))[:2 * SSM_HEADS].reshape(2, SSM_HEADS)
    g["d_skip"] = jnp.sum(ddsk.reshape(B, SSM_HEADS, HEAD_DIM), axis=(0, 2))
    dxbc_act = jnp.concatenate([dxs, dbm, dcm], axis=-1)
    dxbc, dcw, dcb = _conv_bwd(proj, 2048 // 256, small["conv_w"][i], small["conv_b"][i], dxbc_act, taps=SSM_CONV, ct=256,
                               name=f"d_ssm_conv_{i}")
    g["conv_w"] = jnp.sum(dcw, axis=0)[:SSM_CONV]
    g["conv_b"] = jnp.sum(dcb, axis=(0, 1))
    dout = _heads_major(dy_attn.reshape(B, L, D), ATTN_HEADS)
    dqt, dkp, dvp, dbias, dsink = _attn_bwd(sv["qt"], sv["kp"], sv["vp"], band_bias, sv["sinkcol"], dout, name=f"d_attn_{i}")
    g["attn_sink"] = jnp.sum(dsink.reshape(ATTN_HEADS, BLOCK), axis=1)
    dq = _tokens_major(dqt)
    dk = _tokens_major(dkp[:, :, WINDOW:WINDOW + L].astype(BF16))
    dv = _tokens_major(dvp[:, :, WINDOW:WINDOW + L].astype(BF16))
    dproj = jnp.concatenate([dq, dz.reshape(B, L, SSM_WIDTH), dxbc, dk, dv, ddt.astype(BF16)], axis=-1).reshape(T, P_COLS)
    dh = _mm(dproj, wts["w_in"][i], tb=True, name=f"d_h_{i}", tk=1408)
    g["w_in"] = _unperm_in_cols(_mm(sv["h"], dproj, ta=True, name=f"dw_in_{i}", tn=1408))
    dx_in, dw1 = _rmsnorm_bwd(sv["x2"], small["norm1_w"][i], dh, dx_mid, name=f"d_norm1_{i}")
    g["norm1_w"] = dw1[0]
    return dx_in, g, dbias


_BIG = ("w_in", "w_out", "w_up", "w_down")
_BIG_AXIS = {"w_in": 2, "w_out": 1, "w_up": 2, "w_down": 1}
_SMALL = ("rel_bias", "norm1_w", "conv_w", "conv_b", "dt_bias", "a_log", "d_skip", "ssm_norm_w", "attn_sink", "norm2_w",
          "ffn_conv_w", "ffn_conv_b", "final_norm_w")
_SMALL_SHARDED = ("conv_w", "ffn_conv_w")
_ORDER = ("rel_bias", "norm1_w", "w_in", "conv_w", "conv_b", "dt_bias", "a_log", "d_skip", "ssm_norm_w", "attn_sink", "w_out",
          "norm2_w", "w_up", "ffn_conv_w", "ffn_conv_b", "w_down", "final_norm_w")


def _local_step(x, target, wts, small):
    B, L, D = x.shape
    bucket = _bucket_table()
    band_bias = _bias_expand(small["rel_bias"], bucket, name="band_bias")
    saved = []
    for i in range(DEPTH):
        x, sv = _layer_fwd(i, x, wts, small, band_bias)
        saved.append(sv)
    loss, dx, dwf = _loss_head(x.reshape(B * L, D), small["final_norm_w"], target.reshape(B * L, D), name="loss_head")
    per_layer = []
    dbias = jnp.zeros((ATTN_HEADS, BLOCK, KEY_SPAN), F32)
    for i in reversed(range(DEPTH)):
        dx, g, dbias_i = _layer_bwd(i, dx, saved[i], wts, small, band_bias)
        dbias = dbias + dbias_i
        per_layer.append(g)
    per_layer.reverse()
    grads = {k: jnp.stack([g[k] for g in per_layer]) for k in per_layer[0]}
    grads["rel_bias"] = _bias_reduce(dbias, bucket, name="d_rel_bias")
    grads["final_norm_w"] = dwf[0]
    return loss, dx.reshape(B, L, D), grads


def _split_by_chip(g, axis):
    shp = g.shape
    n = shp[axis] // N_CHIP
    g = g.reshape(shp[:axis] + (N_CHIP, n) + shp[axis + 1:])
    return jnp.moveaxis(g, axis, 0)


def _join_chips(a, axis):
    a = jnp.moveaxis(a, 0, axis)
    shp = a.shape
    return a.reshape(shp[:axis] + (shp[axis] * shp[axis + 1],) + shp[axis + 2:])


def kernel(x, rel_bias, norm1_w, w_in, conv_w, conv_b, dt_bias, a_log, d_skip, ssm_norm_w, attn_sink, w_out, norm2_w, w_up, ffn_conv_w, ffn_conv_b, w_down, final_norm_w, loss_target, m_rel_bias, m_norm1_w, m_w_in, m_conv_w, m_conv_b, m_dt_bias, m_a_log, m_d_skip, m_ssm_norm_w, m_attn_sink, m_w_out, m_norm2_w, m_w_up, m_ffn_conv_w, m_ffn_conv_b, m_w_down, m_final_norm_w, v_rel_bias, v_norm1_w, v_w_in, v_conv_w, v_conv_b, v_dt_bias, v_a_log, v_d_skip, v_ssm_norm_w, v_attn_sink, v_w_out, v_norm2_w, v_w_up, v_ffn_conv_w, v_ffn_conv_b, v_w_down, v_final_norm_w):
    w = dict(rel_bias=rel_bias, norm1_w=norm1_w, w_in=w_in, conv_w=conv_w, conv_b=conv_b, dt_bias=dt_bias, a_log=a_log,
             d_skip=d_skip, ssm_norm_w=ssm_norm_w, attn_sink=attn_sink, w_out=w_out, norm2_w=norm2_w, w_up=w_up,
             ffn_conv_w=ffn_conv_w, ffn_conv_b=ffn_conv_b, w_down=w_down, final_norm_w=final_norm_w)
    m = dict(rel_bias=m_rel_bias, norm1_w=m_norm1_w, w_in=m_w_in, conv_w=m_conv_w, conv_b=m_conv_b, dt_bias=m_dt_bias,
             a_log=m_a_log, d_skip=m_d_skip, ssm_norm_w=m_ssm_norm_w, attn_sink=m_attn_sink, w_out=m_w_out, norm2_w=m_norm2_w,
             w_up=m_w_up, ffn_conv_w=m_ffn_conv_w, ffn_conv_b=m_ffn_conv_b, w_down=m_w_down, final_norm_w=m_final_norm_w)
    v = dict(rel_bias=v_rel_bias, norm1_w=v_norm1_w, w_in=v_w_in, conv_w=v_conv_w, conv_b=v_conv_b, dt_bias=v_dt_bias,
             a_log=v_a_log, d_skip=v_d_skip, ssm_norm_w=v_ssm_norm_w, attn_sink=v_attn_sink, w_out=v_w_out, norm2_w=v_norm2_w,
             w_up=v_w_up, ffn_conv_w=v_ffn_conv_w, ffn_conv_b=v_ffn_conv_b, w_down=v_w_down, final_norm_w=v_final_norm_w)
    my_chip = 2 * lax.axis_index("x") + lax.axis_index("y")

    gathered = _gather_weights([w[k].astype(BF16) for k in _BIG], name="gather_weights")
    wts = {k: _join_chips(a, _BIG_AXIS[k]) for k, a in zip(_BIG, gathered)}
    wts["w_in"] = _perm_in_cols(wts["w_in"])
    conv_shapes = [(DEPTH, SSM_CONV, CONV_CH), (DEPTH, FFN_CONV, D_FF)]
    placed = [lax.dynamic_update_slice_in_dim(jnp.zeros(s, F32), w[k], my_chip * w[k].shape[2], axis=2)
              for k, s in zip(_SMALL_SHARDED, conv_shapes)]
    lead = (lax.axis_index("c") == 0).astype(F32)
    conv_full = _unpack(_allreduce_small(_pack([p * lead for p in placed]), name="gather_conv_weights"), conv_shapes)
    small = {k: w[k] for k in _SMALL}
    small["conv_w"], small["ffn_conv_w"] = conv_full

    loss_part, grad_x, gp = _local_step(x, loss_target, wts, small)

    small_shapes = [small[k].shape for k in _SMALL] + [()]
    red = _unpack(_allreduce_small(_pack([gp[k] for k in _SMALL] + [loss_part[0, :1]]), name="reduce_small"), small_shapes)
    gsmall = dict(zip(_SMALL, red[:-1]))
    loss = red[-1]
    for k in _SMALL_SHARDED:
        n = w[k].shape[2]
        gsmall[k] = lax.dynamic_slice_in_dim(gsmall[k], my_chip * n, n, axis=2)

    parts = _scatter_grads([_split_by_chip(gp[k], _BIG_AXIS[k]).astype(BF16) for k in _BIG], name="scatter_grads")
    halves = []
    for k, p in zip(_BIG, parts):
        shp = p.shape
        s = _sum_sources(p.reshape(N_DEV, shp[1] * shp[2], shp[3]), name=f"sum_{k}")
        halves.append(s.reshape(shp[1:]))
    gbig = dict(zip(_BIG, _join_halves(halves, name="join_halves")))

    grad, delta, new_m, new_v = {}, {}, {}, {}
    for k in _BIG:
        shp = w[k].shape
        two = lambda a: a.reshape(shp[0] * shp[1], shp[2])
        d_, m_, v_ = _adamw(two(w[k]), two(gbig[k]), two(m[k]), two(v[k]), name=f"adamw_{k}")
        grad[k], delta[k], new_m[k], new_v[k] = gbig[k], d_.reshape(shp), m_.reshape(shp), v_.reshape(shp)
    shapes = [w[k].shape for k in _SMALL]
    d_, m_, v_ = _adamw(_pack([w[k] for k in _SMALL]), _pack([gsmall[k] for k in _SMALL]), _pack([m[k] for k in _SMALL]),
                        _pack([v[k] for k in _SMALL]), name="adamw_small")
    for k, a, b_, c_ in zip(_SMALL, _unpack(d_, shapes), _unpack(m_, shapes), _unpack(v_, shapes)):
        grad[k], delta[k], new_m[k], new_v[k] = gsmall[k], a, b_, c_
    return (loss, grad_x, *[grad[k] for k in _ORDER], *[delta[k] for k in _ORDER], *[new_m[k] for k in _ORDER],
            *[new_v[k] for k in _ORDER])
```

```python
import functools
import math

import jax
import jax.numpy as jnp
import numpy as np
from jax import lax
from jax.experimental import pallas as pl
from jax.experimental.pallas import tpu as pltpu

F32 = jnp.float32
BF16 = jnp.bfloat16
BS = pl.BlockSpec
SDS = jax.ShapeDtypeStruct
MESH = pl.DeviceIdType.MESH

D_MODEL = 1024
DEPTH = 4
SSM_HEADS = 16
SSM_WIDTH = 1024
BC_WIDTH = 256
CONV_CH = 1536
SSM_CONV = 7
CHUNK = 128
ATTN_HEADS = 16
KV_HEADS = 4
HEAD_DIM = 64
WINDOW = 128
BLOCK = 128
KEY_SPAN = 384
REL_BUCKETS = 32
REL_MAX_DIST = 128
D_FF = 2816
FFN_CONV = 3
NORM_EPS = 1e-6
Z_END = 1024
XBC_END = 2560
DT_END = 2592
Q_END = 3616
K_END = 3872
IN_COLS = 4128
P_COLS = 4224
ADAM_LR, ADAM_B1, ADAM_B2, ADAM_EPS, ADAM_WD, ADAM_STEP = 0.001, 0.9, 0.999, 1e-08, 0.01, 10
NEG = -1e30
N_DEV = 8
N_CHIP = 4
LANE = 128
VMEM_LIMIT_BYTES = 48 * 1024 * 1024


def _pc(body, *, name, grid, in_specs, out_specs, out_shape, scratch_shapes=()):
    return pl.pallas_call(
        body, name=name, grid=grid, in_specs=in_specs, out_specs=out_specs, out_shape=out_shape,
        scratch_shapes=list(scratch_shapes),
        compiler_params=pltpu.CompilerParams(dimension_semantics=("arbitrary",) * len(grid),
                                             vmem_limit_bytes=VMEM_LIMIT_BYTES))


def _div_tile(n, pref, mult):
    t = min(pref, n)
    t -= t % mult
    while t >= mult:
        if n % t == 0:
            return t
        t -= mult
    return n


def _mm(a, b, *, name, ta=False, tb=False, add=None, out_dtype=F32, tm=1024, tn=1024, tk=1024):
    if ta:
        K, M = a.shape
    else:
        M, K = a.shape
    N = b.shape[0] if tb else b.shape[1]
    tm, tn, tk = _div_tile(M, tm, LANE), _div_tile(N, tn, LANE), _div_tile(K, tk, LANE)
    nk = K // tk
    dims = (((0,) if ta else (1,), (1,) if tb else (0,)), ((), ()))

    def body_single(*refs):
        r = lax.dot_general(refs[0][...], refs[1][...], dims, preferred_element_type=F32)
        if add is not None:
            r = r + refs[2][...]
        refs[-1][...] = r.astype(out_dtype)

    def body(*refs):
        if add is None:
            a_ref, b_ref, o_ref, acc_ref = refs
        else:
            a_ref, b_ref, add_ref, o_ref, acc_ref = refs
        k = pl.program_id(2)

        @pl.when(k == 0)
        def _():
            acc_ref[...] = jnp.zeros_like(acc_ref)

        acc_ref[...] += lax.dot_general(a_ref[...], b_ref[...], dims, preferred_element_type=F32)

        @pl.when(k == nk - 1)
        def _():
            r = acc_ref[...]
            if add is not None:
                r = r + add_ref[...]
            o_ref[...] = r.astype(out_dtype)

    a_spec = BS((tk, tm), lambda i, j, k: (k, i)) if ta else BS((tm, tk), lambda i, j, k: (i, k))
    b_spec = BS((tn, tk), lambda i, j, k: (j, k)) if tb else BS((tk, tn), lambda i, j, k: (k, j))
    in_specs, args = [a_spec, b_spec], [a, b]
    if add is not None:
        in_specs.append(BS((tm, tn), lambda i, j, k: (i, j)))
        args.append(add)
    return _pc(body_single if nk == 1 else body, name=name, grid=(M // tm, N // tn, nk), in_specs=in_specs,
               out_specs=BS((tm, tn), lambda i, j, k: (i, j)), out_shape=SDS((M, N), out_dtype),
               scratch_shapes=[] if nk == 1 else [pltpu.VMEM((tm, tn), F32)])(*args)


def _dot(a, b, dims):
    return lax.dot_general(a.astype(BF16), b.astype(BF16), (dims, ((), ())), preferred_element_type=F32)


@jax.custom_vjp
def _nn(a, b):
    return _dot(a, b, ((1,), (0,)))


@jax.custom_vjp
def _nt(a, b):
    return _dot(a, b, ((1,), (1,)))


@jax.custom_vjp
def _tn(a, b):
    return _dot(a, b, ((0,), (0,)))


_nn.defvjp(lambda a, b: (_nn(a, b), (a, b)), lambda r, g: (_nt(g, r[1]), _tn(r[0], g)))
_nt.defvjp(lambda a, b: (_nt(a, b), (a, b)), lambda r, g: (_nn(g, r[1]), _tn(g, r[0])))
_tn.defvjp(lambda a, b: (_tn(a, b), (a, b)), lambda r, g: (_nt(r[1], g), _nn(r[0], g)))


def _hdot(m, x):
    hi = x.astype(BF16)
    r1 = x - hi.astype(F32)
    lo = r1.astype(BF16)
    lo2 = (r1 - lo.astype(F32)).astype(BF16)
    n = x.shape[1]
    out = lax.dot_general(m.astype(BF16), jnp.concatenate([hi, lo, lo2], axis=1), (((1,), (0,)), ((), ())),
                          preferred_element_type=F32)
    return out[:, :n] + out[:, n:2 * n] + out[:, 2 * n:]


@jax.custom_vjp
def _cumdot(m, mt, x):
    return _hdot(m, x)


_cumdot.defvjp(lambda m, mt, x: (_hdot(m, x), (m, mt)),
               lambda r, g: (jnp.zeros_like(r[0]), jnp.zeros_like(r[1]), _hdot(r[1], g)))


def _sigmoid(x):
    return 1.0 / (1.0 + jnp.exp(-x))


def _softplus(x):
    return jnp.maximum(x, 0.0) + jnp.log(1.0 + jnp.exp(-jnp.abs(x)))


def _rms(x, w):
    return x * lax.rsqrt(jnp.mean(x * x, axis=-1, keepdims=True) + NORM_EPS) * w


def _rmsnorm_fwd(x2, w, *, name):
    T, D = x2.shape
    tr = _div_tile(T, 512, 8)

    def body(x_ref, w_ref, o_ref):
        o_ref[...] = _rms(x_ref[...], w_ref[...]).astype(BF16)

    return _pc(body, name=name, grid=(T // tr,),
               in_specs=[BS((tr, D), lambda i: (i, 0)), BS((1, D), lambda i: (0, 0))],
               out_specs=BS((tr, D), lambda i: (i, 0)), out_shape=SDS((T, D), BF16))(x2, w.reshape(1, D))


def _rmsnorm_bwd(x2, w, dh, resid, *, name):
    T, D = x2.shape
    tr = _div_tile(T, 512, 8)

    def body(x_ref, w_ref, dh_ref, r_ref, dx_ref, dw_ref):
        _, vjp = jax.vjp(_rms, x_ref[...], w_ref[...])
        dx, dw = vjp(dh_ref[...])
        dx_ref[...] = dx + r_ref[...]

        @pl.when(pl.program_id(0) == 0)
        def _():
            dw_ref[...] = jnp.zeros_like(dw_ref)

        dw_ref[...] += dw

    row = BS((tr, D), lambda i: (i, 0))
    one = BS((1, D), lambda i: (0, 0))
    return _pc(body, name=name, grid=(T // tr,), in_specs=[row, one, row, row], out_specs=[row, one],
               out_shape=[SDS((T, D), F32), SDS((1, D), F32)])(x2, w.reshape(1, D), dh, resid)


def _shift_rows(x, s):
    if s == 0:
        return x
    n = x.shape[0]
    t = lax.broadcasted_iota(jnp.int32, (n, 1), 0)
    r = pltpu.roll(x, (-s) % n, 0)
    return jnp.where((t + s >= 0) & (t + s < n), r, 0.0)


def _conv_pre(x, w_ref, b_ref, taps):
    c = b_ref[...] + jnp.zeros_like(x)
    for k in range(taps):
        c = c + w_ref[k:k + 1, :] * _shift_rows(x, k - taps // 2)
    return c


def _conv_fwd(x3, x_blk0, w, b, *, taps, ct, gate_blk0=None, out_dtype, name):
    B, L, _ = x3.shape
    C = w.shape[1]
    wp = jnp.zeros((8, C), F32).at[:taps].set(w)

    def body(*refs):
        if gate_blk0 is None:
            x_ref, w_ref, b_ref, o_ref = refs
        else:
            x_ref, u_ref, w_ref, b_ref, o_ref = refs
        c = _conv_pre(x_ref[0], w_ref, b_ref, taps)
        y = c * _sigmoid(c)
        if gate_blk0 is not None:
            y = y * u_ref[0]
        o_ref[0] = y.astype(out_dtype)

    in_specs = [BS((1, L, ct), lambda bi, j: (bi, 0, x_blk0 + j))]
    args = [x3]
    if gate_blk0 is not None:
        in_specs.append(BS((1, L, ct), lambda bi, j: (bi, 0, gate_blk0 + j)))
        args.append(x3)
    in_specs += [BS((8, ct), lambda bi, j: (0, j)), BS((1, ct), lambda bi, j: (0, j))]
    args += [wp, b.reshape(1, C)]
    return _pc(body, name=name, grid=(B, C // ct), in_specs=in_specs,
               out_specs=BS((1, L, ct), lambda bi, j: (bi, 0, j)), out_shape=SDS((B, L, C), out_dtype))(*args)


def _conv_bwd(x3, x_blk0, w, b, dy3, *, taps, ct, gate_blk0=None, name):
    B, L, _ = x3.shape
    C = w.shape[1]
    wp = jnp.zeros((8, C), F32).at[:taps].set(w)
    gated = gate_blk0 is not None

    def body(*refs):
        if gated:
            x_ref, u_ref, w_ref, b_ref, dy_ref, dx_ref, du_ref, dw_ref, db_ref = refs
        else:
            x_ref, w_ref, b_ref, dy_ref, dx_ref, dw_ref, db_ref = refs
        x = x_ref[0]
        dy = dy_ref[0].astype(F32)
        c = _conv_pre(x, w_ref, b_ref, taps)
        sg = _sigmoid(c)
        dsilu = sg * (1.0 + c * (1.0 - sg))
        if gated:
            du_ref[0] = (dy * (c * sg)).astype(BF16)
            dc = dy * u_ref[0] * dsilu
        else:
            dc = dy * dsilu
        dx = jnp.zeros_like(x)
        dw_ref[0] = jnp.zeros((8, ct), F32)
        for k in range(taps):
            s = k - taps // 2
            dx = dx + w_ref[k:k + 1, :] * _shift_rows(dc, -s)
            dw_ref[0, k:k + 1, :] = jnp.sum(dc * _shift_rows(x, s), axis=0, keepdims=True)
        dx_ref[0] = dx.astype(BF16)
        db_ref[0] = jnp.sum(dc, axis=0, keepdims=True)

    xs = BS((1, L, ct), lambda bi, j: (bi, 0, x_blk0 + j))
    ys = BS((1, L, ct), lambda bi, j: (bi, 0, j))
    in_specs, args = [xs], [x3]
    if gated:
        in_specs.append(BS((1, L, ct), lambda bi, j: (bi, 0, gate_blk0 + j)))
        args.append(x3)
    in_specs += [BS((8, ct), lambda bi, j: (0, j)), BS((1, ct), lambda bi, j: (0, j)), ys]
    args += [wp, b.reshape(1, C), dy3]
    out_specs = [ys] + ([ys] if gated else []) + [BS((1, 8, ct), lambda bi, j: (bi, 0, j)), BS((1, 1, ct), lambda bi, j: (bi, 0, j))]
    out_shape = [SDS((B, L, C), BF16)] + ([SDS((B, L, C), BF16)] if gated else []) + [SDS((B, 8, C), F32), SDS((B, 1, C), F32)]
    return _pc(body, name=name, grid=(B, C // ct), in_specs=in_specs, out_specs=out_specs, out_shape=out_shape)(*args)


def _tri(reverse):
    r = lax.broadcasted_iota(jnp.int32, (CHUNK, CHUNK), 0)
    c = lax.broadcasted_iota(jnp.int32, (CHUNK, CHUNK), 1)
    return (c >= r) if reverse else (c <= r)


PAIRS = 2
QUADS = SSM_HEADS // (2 * PAIRS)
QW = PAIRS * LANE


def _ssd_chunk(h0, h1, x0, x1, bm, cm, dtraw, dtb, alog, *, col0, reverse):
    mask = _tri(reverse)
    eye = lax.broadcasted_iota(jnp.int32, (CHUNK, CHUNK), 0) == lax.broadcasted_iota(jnp.int32, (CHUNK, CHUNK), 1)
    lane = lax.broadcasted_iota(jnp.int32, (1, LANE), 1)
    first = lane < HEAD_DIM
    dtc = _softplus(dtraw + dtb)
    adt = dtc * (-jnp.exp(alog))
    cumc = _cumdot(mask.astype(F32), _tri(not reverse).astype(F32), adt)
    totc = jnp.sum(adt, axis=0, keepdims=True)
    cb = _nt(cm, bm)

    def col(v, c):
        return jnp.sum(jnp.where(lane == c, v, 0.0), axis=1, keepdims=True)

    outs, states = [], []
    for p, (hprev, xs) in enumerate(((h0, x0), (h1, x1))):
        c0 = col0 + 2 * p
        cj = (col(cumc, c0), col(cumc, c0 + 1))
        cum = jnp.where(first, cj[0], cj[1])
        tot = jnp.where(first, col(totc, c0), col(totc, c0 + 1))
        xdt = xs * jnp.where(first, col(dtc, c0), col(dtc, c0 + 1))
        y = _nn(cm, hprev) * jnp.exp(cum)
        for j in range(2):
            rj = jnp.sum(jnp.where(eye, cj[j], 0.0), axis=0, keepdims=True)
            dec = jnp.exp(jnp.where(mask, cj[j] - rj, NEG))
            y = y + _nn(cb * dec, jnp.where(first if j == 0 else ~first, xdt, 0.0))
        outs.append(y)
        states.append(hprev * jnp.exp(tot) + _tn(bm, xdt * jnp.exp(tot - cum)))
    return outs[0], outs[1], states[0], states[1]


def _ssd_specs(B, L):
    def lanes(w, blk):
        return BS((1, L, w), blk)

    return [
        lanes(QW, lambda b, q: (b, 0, q)),
        lanes(LANE, lambda b, q: (b, 0, 8 + q // 2)),
        lanes(LANE, lambda b, q: (b, 0, 10 + q // 2)),
        lanes(LANE, lambda b, q: (b, 0, P_COLS // LANE - 1)),
        BS((1, LANE), lambda b, q: (0, 0)),
        BS((1, LANE), lambda b, q: (0, 0)),
        BS((1, QW), lambda b, q: (0, q)),
    ]


def _ssd_slot(d, ci):
    return ci if d == 0 else ci + 1


def _ssd_fwd(xbc_act, proj, dtb, alog, dskip, *, name):
    B, L, _ = xbc_act.shape
    nc = L // CHUNK

    def body(xs_ref, b_ref, c_ref, dt_ref, dtb_ref, alog_ref, dsk_ref, y_ref, hs_ref):
        q = pl.program_id(1)
        dtb_v, alog_v = dtb_ref[...], alog_ref[...]
        y_ref[0] = dsk_ref[...] * xs_ref[0]
        hs_ref[0, 0, 0, 0] = jnp.zeros((LANE, QW), F32)
        hs_ref[0, 0, 1, nc] = jnp.zeros((LANE, QW), F32)

        def step(i, carry):
            cis = (i, nc - 1 - i)
            rows = [pl.ds(pl.multiple_of(ci * CHUNK, CHUNK), CHUNK) for ci in cis]
            res = []
            for d in range(2):
                cur = _ssd_slot(d, cis[d])
                res.append(_ssd_chunk(
                    hs_ref[0, 0, d, cur, :, :LANE], hs_ref[0, 0, d, cur, :, LANE:], xs_ref[0, rows[d], :LANE],
                    xs_ref[0, rows[d], LANE:], b_ref[0, rows[d], :], c_ref[0, rows[d], :], dt_ref[0, rows[d], :], dtb_v, alog_v,
                    col0=SSM_HEADS * d + 2 * PAIRS * q, reverse=d == 1))
            for d in range(2):
                y0, y1, n0, n1 = res[d]
                nxt = _ssd_slot(d, cis[d] + 1 if d == 0 else cis[d] - 1)
                hs_ref[0, 0, d, nxt, :, :LANE] = n0
                hs_ref[0, 0, d, nxt, :, LANE:] = n1
                y_ref[0, rows[d], :LANE] += y0
                y_ref[0, rows[d], LANE:] += y1
            return carry

        lax.fori_loop(0, nc, step, 0)

    return _pc(body, name=name, grid=(B, QUADS), in_specs=_ssd_specs(B, L),
               out_specs=[BS((1, L, QW), lambda b, q: (b, 0, q)),
                          BS((1, 1, 2, nc + 1, LANE, QW), lambda b, q: (b, q, 0, 0, 0, 0))],
               out_shape=[SDS((B, L, SSM_WIDTH), F32), SDS((B, QUADS, 2, nc + 1, LANE, QW), F32)])(
        xbc_act, xbc_act, xbc_act, proj, dtb, alog, dskip)


def _ssd_bwd(xbc_act, proj, dtb, alog, dskip, hs, dy, *, name):
    B, L, _ = xbc_act.shape
    nc = L // CHUNK

    def body(xs_ref, b_ref, c_ref, dt_ref, dtb_ref, alog_ref, dsk_ref, hs_ref, dy_ref,
             dxs_ref, db_ref, dc_ref, ddt_ref, ddtb_ref, dalog_ref, ddsk_ref, dh_ref):
        q = pl.program_id(1)
        dtb_v, alog_v = dtb_ref[...], alog_ref[...]

        @pl.when(q % 2 == 0)
        def _():
            db_ref[...] = jnp.zeros_like(db_ref)
            dc_ref[...] = jnp.zeros_like(dc_ref)

        @pl.when(q == 0)
        def _():
            ddt_ref[...] = jnp.zeros_like(ddt_ref)

        dxs_ref[0] = dy_ref[0] * dsk_ref[...]
        ddsk_ref[0] = jnp.sum(dy_ref[0] * xs_ref[0], axis=0, keepdims=True)
        dh_ref[...] = jnp.zeros_like(dh_ref)

        def step(i, carry):
            g_dtb, g_alog = carry
            cis = (nc - 1 - i, i)
            rows = [pl.ds(pl.multiple_of(ci * CHUNK, CHUNK), CHUNK) for ci in cis]
            res = []
            for d in range(2):
                cur = _ssd_slot(d, cis[d])
                fn = functools.partial(_ssd_chunk, col0=SSM_HEADS * d + 2 * PAIRS * q, reverse=d == 1)
                _, vjp = jax.vjp(fn, hs_ref[0, 0, d, cur, :, :LANE], hs_ref[0, 0, d, cur, :, LANE:], xs_ref[0, rows[d], :LANE],
                                 xs_ref[0, rows[d], LANE:], b_ref[0, rows[d], :], c_ref[0, rows[d], :], dt_ref[0, rows[d], :],
                                 dtb_v, alog_v)
                res.append(vjp((dy_ref[0, rows[d], :LANE], dy_ref[0, rows[d], LANE:], dh_ref[d, :, :LANE], dh_ref[d, :, LANE:])))
            for d in range(2):
                g_h0, g_h1, g_x0, g_x1, g_b, g_c, g_dt, g_dtb1, g_alog1 = res[d]
                dh_ref[d, :, :LANE] = g_h0
                dh_ref[d, :, LANE:] = g_h1
                dxs_ref[0, rows[d], :LANE] += g_x0
                dxs_ref[0, rows[d], LANE:] += g_x1
                db_ref[0, rows[d], :] += g_b
                dc_ref[0, rows[d], :] += g_c
                ddt_ref[0, rows[d], :] += g_dt
                g_dtb, g_alog = g_dtb + g_dtb1, g_alog + g_alog1
            return g_dtb, g_alog

        zero_row = jnp.zeros((1, LANE), F32)
        a0, a1 = lax.fori_loop(0, nc, step, (zero_row, zero_row))
        ddtb_ref[0, 0] = a0
        dalog_ref[0, 0] = a1

    lanes = lambda w, blk: BS((1, L, w), blk)
    in_specs = _ssd_specs(B, L) + [BS((1, 1, 2, nc + 1, LANE, QW), lambda b, q: (b, q, 0, 0, 0, 0)), lanes(QW, lambda b, q: (b, 0, q))]
    out_specs = [lanes(QW, lambda b, q: (b, 0, q)), lanes(LANE, lambda b, q: (b, 0, q // 2)), lanes(LANE, lambda b, q: (b, 0, q // 2)),
                 lanes(LANE, lambda b, q: (b, 0, 0)), BS((1, 1, 1, LANE), lambda b, q: (b, q, 0, 0)),
                 BS((1, 1, 1, LANE), lambda b, q: (b, q, 0, 0)), BS((1, 1, QW), lambda b, q: (b, 0, q))]
    out_shape = [SDS((B, L, SSM_WIDTH), F32), SDS((B, L, BC_WIDTH), F32), SDS((B, L, BC_WIDTH), F32), SDS((B, L, LANE), F32),
                 SDS((B, QUADS, 1, LANE), F32), SDS((B, QUADS, 1, LANE), F32), SDS((B, 1, SSM_WIDTH), F32)]
    return _pc(body, name=name, grid=(B, QUADS), in_specs=in_specs, out_specs=out_specs, out_shape=out_shape,
               scratch_shapes=[pltpu.VMEM((2, LANE, QW), F32)])(
        xbc_act, xbc_act, xbc_act, proj, dtb, alog, dskip, hs, dy)


def _gate_norm(yp, z, w):
    v = yp * (z * _sigmoid(z))
    return v * lax.rsqrt(jnp.mean(v * v, axis=-1, keepdims=True) + NORM_EPS) * w


def _gate_fwd(ypre2, proj2, w, *, name):
    T = ypre2.shape[0]
    tr = _div_tile(T, 512, 8)
    G = 512

    def body(y_ref, z_ref, w_ref, o_ref):
        o_ref[...] = _gate_norm(y_ref[...], z_ref[...], w_ref[...]).astype(BF16)

    return _pc(body, name=name, grid=(T // tr, 2),
               in_specs=[BS((tr, G), lambda i, g: (i, g)), BS((tr, G), lambda i, g: (i, 2 + g)), BS((1, G), lambda i, g: (0, g))],
               out_specs=BS((tr, G), lambda i, g: (i, g)), out_shape=SDS((T, SSM_WIDTH), BF16))(ypre2, proj2, w.reshape(1, -1))


def _gate_bwd(ypre2, proj2, w, dy, *, name):
    T = ypre2.shape[0]
    tr = _div_tile(T, 512, 8)
    G = 512

    def body(y_ref, z_ref, w_ref, dy_ref, dyp_ref, dz_ref, dw_ref):
        _, vjp = jax.vjp(_gate_norm, y_ref[...], z_ref[...], w_ref[...])
        dyp, dz, dw = vjp(dy_ref[...])
        dyp_ref[...] = dyp
        dz_ref[...] = dz.astype(BF16)
        dw_ref[0] = dw

    tile = BS((tr, G), lambda i, g: (i, g))
    return _pc(body, name=name, grid=(T // tr, 2),
               in_specs=[tile, BS((tr, G), lambda i, g: (i, 2 + g)), BS((1, G), lambda i, g: (0, g)), tile],
               out_specs=[tile, tile, BS((1, 1, G), lambda i, g: (i, 0, g))],
               out_shape=[SDS((T, SSM_WIDTH), F32), SDS((T, SSM_WIDTH), BF16), SDS((T // tr, 1, SSM_WIDTH), F32)])(
        ypre2, proj2, w.reshape(1, -1), dy)


def _attn_block(q, k, v, bias, sink, valid):
    s = _nt(q, k) * (HEAD_DIM ** -0.5) + bias
    s = jnp.where(valid, s, NEG)
    m = jnp.maximum(jnp.max(s, axis=-1, keepdims=True), sink)
    p = jnp.exp(s - m)
    den = jnp.sum(p, axis=-1, keepdims=True) + jnp.exp(sink - m)
    return _nn(p / den, v)


def _attn_valid(n, L):
    i = lax.broadcasted_iota(jnp.int32, (4 * BLOCK, KEY_SPAN), 0) % BLOCK
    j = lax.broadcasted_iota(jnp.int32, (4 * BLOCK, KEY_SPAN), 1)
    kpos = n * BLOCK - WINDOW + j
    return (jnp.abs(j - WINDOW - i) <= WINDOW) & (kpos >= 0) & (kpos < L)


def _attn_in_specs():
    kv = lambda o: BS((1, 1, BLOCK, HEAD_DIM), lambda g, b, n, o=o: (b, g, n + o, 0))
    return [BS((1, 4, BLOCK, HEAD_DIM), lambda g, b, n: (b, g, n, 0)), kv(0), kv(1), kv(2), kv(0), kv(1), kv(2),
            BS((4, BLOCK, KEY_SPAN), lambda g, b, n: (g, 0, 0)), BS((4 * BLOCK, 1), lambda g, b, n: (g, 0))]


def _attn_fwd(qt, kp, vp, bias, sinkcol, *, name):
    B, _, L, _ = qt.shape

    def body(q_ref, k0, k1, k2, v0, v1, v2, bias_ref, sink_ref, o_ref):
        n = pl.program_id(2)
        k = jnp.concatenate([k0[0, 0], k1[0, 0], k2[0, 0]], axis=0)
        v = jnp.concatenate([v0[0, 0], v1[0, 0], v2[0, 0]], axis=0)
        out = _attn_block(q_ref[0].reshape(4 * BLOCK, HEAD_DIM), k, v, bias_ref[...].reshape(4 * BLOCK, KEY_SPAN),
                          sink_ref[...], _attn_valid(n, L))
        o_ref[0] = out.reshape(4, BLOCK, HEAD_DIM).astype(BF16)

    return _pc(body, name=name, grid=(KV_HEADS, B, L // BLOCK), in_specs=_attn_in_specs(),
               out_specs=BS((1, 4, BLOCK, HEAD_DIM), lambda g, b, n: (b, g, n, 0)),
               out_shape=SDS((B, ATTN_HEADS, L, HEAD_DIM), BF16))(qt, kp, kp, kp, vp, vp, vp, bias, sinkcol)


def _attn_bwd(qt, kp, vp, bias, sinkcol, dout, *, name):
    B, _, L, _ = qt.shape
    LP = L + 2 * WINDOW

    def body(q_ref, k0, k1, k2, v0, v1, v2, bias_ref, sink_ref, do_ref, dq_ref, dk_ref, dv_ref, dbias_ref, dsink_ref):
        b, n = pl.program_id(1), pl.program_id(2)
        k = jnp.concatenate([k0[0, 0], k1[0, 0], k2[0, 0]], axis=0).astype(F32)
        v = jnp.concatenate([v0[0, 0], v1[0, 0], v2[0, 0]], axis=0).astype(F32)
        fn = functools.partial(_attn_block, valid=_attn_valid(n, L))
        _, vjp = jax.vjp(fn, q_ref[0].reshape(4 * BLOCK, HEAD_DIM).astype(F32), k, v,
                         bias_ref[...].reshape(4 * BLOCK, KEY_SPAN), sink_ref[...])
        dq, dk, dv, dbias, dsink = vjp(do_ref[0].reshape(4 * BLOCK, HEAD_DIM).astype(F32))
        dq_ref[0] = dq.reshape(4, BLOCK, HEAD_DIM).astype(BF16)

        @pl.when(n == 0)
        def _():
            dk_ref[...] = jnp.zeros_like(dk_ref)
            dv_ref[...] = jnp.zeros_like(dv_ref)

        @pl.when((n == 0) & (b == 0))
        def _():
            dbias_ref[...] = jnp.zeros_like(dbias_ref)
            dsink_ref[...] = jnp.zeros_like(dsink_ref)

        rows = pl.ds(pl.multiple_of(n * BLOCK, BLOCK), KEY_SPAN)
        dk_ref[0, 0, rows, :] += dk
        dv_ref[0, 0, rows, :] += dv
        dbias_ref[...] += dbias.reshape(4, BLOCK, KEY_SPAN)
        dsink_ref[...] += dsink

    qspec = BS((1, 4, BLOCK, HEAD_DIM), lambda g, b, n: (b, g, n, 0))
    kvout = BS((1, 1, LP, HEAD_DIM), lambda g, b, n: (b, g, 0, 0))
    return _pc(body, name=name, grid=(KV_HEADS, B, L // BLOCK), in_specs=_attn_in_specs() + [qspec],
               out_specs=[qspec, kvout, kvout, BS((4, BLOCK, KEY_SPAN), lambda g, b, n: (g, 0, 0)),
                          BS((4 * BLOCK, 1), lambda g, b, n: (g, 0))],
               out_shape=[SDS((B, ATTN_HEADS, L, HEAD_DIM), BF16), SDS((B, KV_HEADS, LP, HEAD_DIM), F32),
                          SDS((B, KV_HEADS, LP, HEAD_DIM), F32), SDS((ATTN_HEADS, BLOCK, KEY_SPAN), F32),
                          SDS((ATTN_HEADS * BLOCK, 1), F32)])(qt, kp, kp, kp, vp, vp, vp, bias, sinkcol, dout)


def _t5_bucket(rel):
    half = REL_BUCKETS // 2
    max_exact = half // 2
    ret = jnp.where(rel > 0, half, 0)
    n = jnp.abs(rel)
    nf = jnp.maximum(n, 1).astype(F32)
    large = max_exact + (jnp.log(nf / max_exact) / math.log(REL_MAX_DIST / max_exact) * (half - max_exact)).astype(jnp.int32)
    large = jnp.minimum(large, half - 1)
    return ret + jnp.where(n < max_exact, n, large)


def _bucket_table():
    rel = jnp.arange(KEY_SPAN)[None, :] - WINDOW - jnp.arange(BLOCK)[:, None]
    return _t5_bucket(rel).astype(jnp.int32)


def _bias_expand(rel_bias, bucket, *, name):
    rbt = jnp.zeros((ATTN_HEADS, 1, LANE), F32).at[:, 0, :REL_BUCKETS].set(rel_bias.T)

    def body(rb_ref, bk_ref, o_ref):
        lane = lax.broadcasted_iota(jnp.int32, (1, LANE), 1)
        row = rb_ref[0]
        bk = bk_ref[...]
        acc = jnp.zeros((BLOCK, KEY_SPAN), F32)
        for r in range(REL_BUCKETS):
            val = jnp.sum(jnp.where(lane == r, row, 0.0), axis=1, keepdims=True)
            acc = jnp.where(bk == r, val, acc)
        o_ref[0] = acc

    return _pc(body, name=name, grid=(ATTN_HEADS,),
               in_specs=[BS((1, 1, LANE), lambda h: (h, 0, 0)), BS((BLOCK, KEY_SPAN), lambda h: (0, 0))],
               out_specs=BS((1, BLOCK, KEY_SPAN), lambda h: (h, 0, 0)), out_shape=SDS((ATTN_HEADS, BLOCK, KEY_SPAN), F32))(rbt, bucket)


def _bias_reduce(dbias, bucket, *, name):
    def body(db_ref, bk_ref, o_ref):
        lane = lax.broadcasted_iota(jnp.int32, (1, LANE), 1)
        x = db_ref[0]
        bk = bk_ref[...]
        acc = jnp.zeros((1, LANE), F32)
        for r in range(REL_BUCKETS):
            part = jnp.sum(jnp.where(bk == r, x, 0.0), axis=1, keepdims=True)
            acc = jnp.where(lane == r, jnp.sum(part, axis=0, keepdims=True), acc)
        o_ref[0] = acc

    out = _pc(body, name=name, grid=(ATTN_HEADS,),
              in_specs=[BS((1, BLOCK, KEY_SPAN), lambda h: (h, 0, 0)), BS((BLOCK, KEY_SPAN), lambda h: (0, 0))],
              out_specs=BS((1, 1, LANE), lambda h: (h, 0, 0)), out_shape=SDS((ATTN_HEADS, 1, LANE), F32))(dbias, bucket)
    return out[:, 0, :REL_BUCKETS].T


def _loss_head(x2, w, target, *, name):
    T, D = x2.shape
    tr = _div_tile(T, 512, 8)

    def tile_loss(x, w, t):
        err = _rms(x, w) - t
        return 0.5 * jnp.sum(jnp.mean(err * err, axis=-1, keepdims=True), axis=0, keepdims=True)

    def body(x_ref, w_ref, t_ref, loss_ref, dx_ref, dw_ref):
        t = t_ref[...]
        l, vjp = jax.vjp(lambda x, w: tile_loss(x, w, t), x_ref[...], w_ref[...])
        dx, dw = vjp(jnp.ones((1, 1), F32))
        dx_ref[...] = dx

        @pl.when(pl.program_id(0) == 0)
        def _():
            dw_ref[...] = jnp.zeros_like(dw_ref)
            loss_ref[...] = jnp.zeros_like(loss_ref)

        dw_ref[...] += dw
        loss_ref[...] += l + jnp.zeros((1, LANE), F32)

    row = BS((tr, D), lambda i: (i, 0))
    one = BS((1, D), lambda i: (0, 0))
    return _pc(body, name=name, grid=(T // tr,), in_specs=[row, one, row],
               out_specs=[BS((1, LANE), lambda i: (0, 0)), row, one],
               out_shape=[SDS((1, LANE), F32), SDS((T, D), F32), SDS((1, D), F32)])(x2, w.reshape(1, D), target)


def _adamw(w2, g2, m2, v2, *, name):
    R, C = w2.shape
    tr = _div_tile(R, 256, 8)
    c1 = 1.0 - ADAM_B1 ** ADAM_STEP
    c2 = 1.0 - ADAM_B2 ** ADAM_STEP

    def body(w_ref, g_ref, m_ref, v_ref, d_ref, nm_ref, nv_ref):
        g = g_ref[...]
        m = ADAM_B1 * m_ref[...] + (1.0 - ADAM_B1) * g
        v = ADAM_B2 * v_ref[...] + (1.0 - ADAM_B2) * (g * g)
        d_ref[...] = -ADAM_LR * ((m / c1) / (jnp.sqrt(v / c2) + ADAM_EPS) + ADAM_WD * w_ref[...])
        nm_ref[...] = m
        nv_ref[...] = v

    t = BS((tr, C), lambda i: (i, 0))
    return _pc(body, name=name, grid=(R // tr,), in_specs=[t, t, t, t], out_specs=[t, t, t],
               out_shape=[SDS((R, C), F32)] * 3)(w2, g2, m2, v2)


def _place():
    return lax.axis_index("x"), lax.axis_index("y"), lax.axis_index("c")


def _gather_weights(shards, *, name):
    na = len(shards)
    half = DEPTH // 2

    def body(*refs):
        ins, outs = refs[:na], refs[na:2 * na]
        send_sems, recv_sems = refs[2 * na:]
        x, y, c = _place()
        me_chip = 2 * x + y
        chips = [(1 - x, y), (x, 1 - y), (1 - x, 1 - y)]
        mine = pl.ds(c * half, half)
        theirs = pl.ds((1 - c) * half, half)
        sends = []
        for a in range(na):
            for k, (px, py) in enumerate(chips):
                cp = pltpu.make_async_remote_copy(ins[a].at[mine], outs[a].at[me_chip, mine], send_sems.at[a, k], recv_sems.at[a, k],
                                                  device_id=(px, py, c), device_id_type=MESH)
                cp.start()
                sends.append(cp)
        for a in range(na):
            for k, (px, py) in enumerate(chips):
                got = outs[a].at[2 * px + py, mine]
                pltpu.make_async_remote_copy(got, got, send_sems.at[a, k], recv_sems.at[a, k],
                                             device_id=(px, py, c), device_id_type=MESH).wait_recv()
                cp = pltpu.make_async_remote_copy(got, got, send_sems.at[a, 3 + k], recv_sems.at[a, 3 + k],
                                                  device_id=(x, y, 1 - c), device_id_type=MESH)
                cp.start()
                sends.append(cp)
        for a in range(na):
            for k, (px, py) in enumerate(chips):
                got = outs[a].at[2 * px + py, theirs]
                pltpu.make_async_remote_copy(got, got, send_sems.at[a, 3 + k], recv_sems.at[a, 3 + k],
                                             device_id=(x, y, 1 - c), device_id_type=MESH).wait_recv()
        for cp in sends:
            cp.wait_send()

    any_spec = BS(memory_space=pl.ANY)
    return pl.pallas_call(
        body, name=name, in_specs=[any_spec] * na, out_specs=[any_spec] * na,
        out_shape=[SDS((N_CHIP,) + s.shape, s.dtype) for s in shards],
        scratch_shapes=[pltpu.SemaphoreType.DMA((na, 6)), pltpu.SemaphoreType.DMA((na, 6))],
        compiler_params=pltpu.CompilerParams(has_side_effects=True))(*shards)


def _scatter_grads(bufs, *, name):
    na = len(bufs)
    half = DEPTH // 2

    def body(*refs):
        ins, outs = refs[:na], refs[na:2 * na]
        send_sems, recv_sems = refs[2 * na:]
        x, y, c = _place()
        me = 4 * x + 2 * y + c
        sends = []
        for a in range(na):
            for r in range(1, N_DEV):
                tx, ty, tc = x ^ (r >> 2), y ^ ((r >> 1) & 1), c ^ (r & 1)
                cp = pltpu.make_async_remote_copy(ins[a].at[2 * tx + ty, pl.ds(tc * half, half)], outs[a].at[me],
                                                  send_sems.at[a, r - 1], recv_sems.at[a, r - 1],
                                                  device_id=(tx, ty, tc), device_id_type=MESH)
                cp.start()
                sends.append(cp)
        for a in range(na):
            for r in range(1, N_DEV):
                tx, ty, tc = x ^ (r >> 2), y ^ ((r >> 1) & 1), c ^ (r & 1)
                got = outs[a].at[4 * tx + 2 * ty + tc]
                pltpu.make_async_remote_copy(got, got, send_sems.at[a, r - 1], recv_sems.at[a, r - 1],
                                             device_id=(tx, ty, tc), device_id_type=MESH).wait_recv()
        for cp in sends:
            cp.wait_send()

    any_spec = BS(memory_space=pl.ANY)
    return pl.pallas_call(
        body, name=name, in_specs=[any_spec] * na, out_specs=[any_spec] * na,
        out_shape=[SDS((N_DEV, half) + b.shape[2:], b.dtype) for b in bufs],
        scratch_shapes=[pltpu.SemaphoreType.DMA((na, N_DEV - 1)), pltpu.SemaphoreType.DMA((na, N_DEV - 1))],
        compiler_params=pltpu.CompilerParams(has_side_effects=True))(*bufs)


def _sum_sources(parts, *, name):
    _, R, C = parts.shape
    tr = _div_tile(R, 256, 16)

    def body(p_ref, o_ref):
        acc = p_ref[0].astype(F32)
        for s in range(1, N_DEV):
            acc = acc + p_ref[s].astype(F32)
        o_ref[...] = acc

    return _pc(body, name=name, grid=(R // tr,), in_specs=[BS((N_DEV, tr, C), lambda i: (0, i, 0))],
               out_specs=BS((tr, C), lambda i: (i, 0)), out_shape=SDS((R, C), F32))(parts)


def _join_halves(halves, *, name):
    na = len(halves)

    def body(*refs):
        ins, outs = refs[:na], refs[na:2 * na]
        send_sems, recv_sems = refs[2 * na:]
        x, y, c = _place()
        cps = []
        for a in range(na):
            cp = pltpu.make_async_remote_copy(ins[a], outs[a], send_sems.at[a], recv_sems.at[a],
                                              device_id=(x, y, 1 - c), device_id_type=MESH)
            cp.start()
            cps.append(cp)
        for cp in cps:
            cp.wait_recv()
        for cp in cps:
            cp.wait_send()

    any_spec = BS(memory_space=pl.ANY)
    return pl.pallas_call(
        body, name=name, in_specs=[any_spec] * na, out_specs=[any_spec] * na,
        out_shape=[SDS(h.shape, h.dtype) for h in halves],
        scratch_shapes=[pltpu.SemaphoreType.DMA((na,)), pltpu.SemaphoreType.DMA((na,))],
        compiler_params=pltpu.CompilerParams(has_side_effects=True))(*halves)


def _allreduce_small(vec, *, name):
    R = vec.shape[0]

    def body(v_ref, o_ref, all_ref, send_sems, recv_sems):
        x, y, c = _place()
        me = 4 * x + 2 * y + c
        all_ref[me] = v_ref[...]
        sends = []
        for r in range(1, N_DEV):
            tgt = (x ^ (r >> 2), y ^ ((r >> 1) & 1), c ^ (r & 1))
            cp = pltpu.make_async_remote_copy(v_ref, all_ref.at[me], send_sems.at[r - 1], recv_sems.at[r - 1],
                                              device_id=tgt, device_id_type=MESH)
            cp.start()
            sends.append(cp)
        for r in range(1, N_DEV):
            tx, ty, tc = x ^ (r >> 2), y ^ ((r >> 1) & 1), c ^ (r & 1)
            got = all_ref.at[4 * tx + 2 * ty + tc]
            pltpu.make_async_remote_copy(got, got, send_sems.at[r - 1], recv_sems.at[r - 1],
                                         device_id=(tx, ty, tc), device_id_type=MESH).wait_recv()
        for cp in sends:
            cp.wait_send()
        acc = all_ref[0]
        for s in range(1, N_DEV):
            acc = acc + all_ref[s]
        o_ref[...] = acc

    vm = BS(memory_space=pltpu.VMEM)
    return pl.pallas_call(
        body, name=name, in_specs=[vm], out_specs=vm, out_shape=SDS((R, LANE), F32),
        scratch_shapes=[pltpu.VMEM((N_DEV, R, LANE), F32), pltpu.SemaphoreType.DMA((N_DEV - 1,)), pltpu.SemaphoreType.DMA((N_DEV - 1,))],
        compiler_params=pltpu.CompilerParams(has_side_effects=True, vmem_limit_bytes=VMEM_LIMIT_BYTES))(vec)


def _pack(arrs):
    rows = []
    for a in arrs:
        f = a.reshape(-1).astype(F32)
        n = -(-f.shape[0] // LANE) * LANE
        rows.append(jnp.pad(f, (0, n - f.shape[0])).reshape(-1, LANE))
    v = jnp.concatenate(rows, axis=0)
    pad = -v.shape[0] % 8
    return jnp.pad(v, ((0, pad), (0, 0)))


def _unpack(v, shapes):
    out, r = [], 0
    for s in shapes:
        n = int(np.prod(s)) if len(s) else 1
        nr = -(-n // LANE)
        out.append(v[r:r + nr].reshape(-1)[:n].reshape(s))
        r += nr
    return out


def _perm_in_cols(w_full):
    z, xbc, dt, q, k, v = (w_full[..., :Z_END], w_full[..., Z_END:XBC_END], w_full[..., XBC_END:DT_END],
                           w_full[..., DT_END:Q_END], w_full[..., Q_END:K_END], w_full[..., K_END:])
    pad = jnp.zeros(dt.shape[:-1] + (LANE - dt.shape[-1],), dt.dtype)
    return jnp.concatenate([q, z, xbc, k, v, dt, pad], axis=-1)


def _unperm_in_cols(g):
    q, z, xbc, k, v, dt = (g[..., :1024], g[..., 1024:2048], g[..., 2048:3584], g[..., 3584:3840], g[..., 3840:4096],
                           g[..., 4096:4096 + 2 * SSM_HEADS])
    return jnp.concatenate([z, xbc, dt, q, k, v], axis=-1)


def _heads_major(a3, nh):
    B, L, _ = a3.shape
    return a3.reshape(B, L, nh, HEAD_DIM).transpose(0, 2, 1, 3)


def _tokens_major(a4):
    B, nh, L, _ = a4.shape
    return a4.transpose(0, 2, 1, 3).reshape(B, L, nh * HEAD_DIM)


def _pad_keys(a4):
    return jnp.pad(a4, ((0, 0), (0, 0), (WINDOW, WINDOW), (0, 0)))


def _dt_cols(a):
    return jnp.pad(a.reshape(1, 2 * SSM_HEADS), ((0, 0), (0, LANE - 2 * SSM_HEADS)))


def _layer_fwd(i, x, wts, small, band_bias):
    B, L, D = x.shape
    T = B * L
    x2 = x.reshape(T, D)
    h = _rmsnorm_fwd(x2, small["norm1_w"][i], name=f"norm1_{i}")
    proj2 = _mm(h, wts["w_in"][i], name=f"in_proj_{i}", tn=1408)
    proj = proj2.reshape(B, L, P_COLS)
    xbc_act = _conv_fwd(proj, 2048 // 256, small["conv_w"][i], small["conv_b"][i], taps=SSM_CONV, ct=256,
                        out_dtype=F32, name=f"ssm_conv_{i}")
    dtb, alog = _dt_cols(small["dt_bias"][i]), _dt_cols(small["a_log"][i])
    dskip = jnp.repeat(small["d_skip"][i], HEAD_DIM).reshape(1, SSM_WIDTH)
    ypre, hs = _ssd_fwd(xbc_act, proj, dtb, alog, dskip, name=f"ssd_{i}")
    y_ssm = _gate_fwd(ypre.reshape(T, SSM_WIDTH), proj2, small["ssm_norm_w"][i], name=f"gate_{i}")
    qt = _heads_major(proj[..., :1024].astype(BF16), ATTN_HEADS)
    kp = _pad_keys(_heads_major(proj[..., 3584:3840].astype(BF16), KV_HEADS))
    vp = _pad_keys(_heads_major(proj[..., 3840:4096].astype(BF16), KV_HEADS))
    sinkcol = jnp.repeat(small["attn_sink"][i], BLOCK).reshape(ATTN_HEADS * BLOCK, 1)
    y_attn = _tokens_major(_attn_fwd(qt, kp, vp, band_bias, sinkcol, name=f"attn_{i}")).reshape(T, D)
    w_out = wts["w_out"][i]
    x_mid = _mm(y_ssm, w_out[:SSM_WIDTH], add=x2, name=f"out_proj_a_{i}")
    x_mid = _mm(y_attn, w_out[SSM_WIDTH:], add=x_mid, name=f"out_proj_b_{i}")
    h2 = _rmsnorm_fwd(x_mid, small["norm2_w"][i], name=f"norm2_{i}")
    gu2 = _mm(h2, wts["w_up"][i], name=f"up_proj_{i}", tn=1408)
    gu = gu2.reshape(B, L, 2 * D_FF)
    act = _conv_fwd(gu, 0, small["ffn_conv_w"][i], small["ffn_conv_b"][i], taps=FFN_CONV, ct=256, gate_blk0=D_FF // 256,
                    out_dtype=BF16, name=f"ffn_conv_{i}")
    x_out = _mm(act.reshape(T, D_FF), wts["w_down"][i], add=x_mid, name=f"down_proj_{i}", tk=1408)
    saved = dict(x2=x2, h=h, proj2=proj2, xbc_act=xbc_act, dtb=dtb, alog=alog, dskip=dskip, ypre=ypre, hs=hs, y_ssm=y_ssm,
                 qt=qt, kp=kp, vp=vp, sinkcol=sinkcol, y_attn=y_attn, x_mid=x_mid, h2=h2, gu=gu, act=act)
    return x_out.reshape(B, L, D), saved


def _layer_bwd(i, dx_out, sv, wts, small, band_bias):
    T, D = dx_out.shape
    B, L = sv["gu"].shape[:2]
    g = {}
    dxb = dx_out.astype(BF16)
    dact = _mm(dxb, wts["w_down"][i], tb=True, out_dtype=BF16, name=f"d_act_{i}", tn=1408)
    g["w_down"] = _mm(sv["act"].reshape(T, D_FF), dxb, ta=True, name=f"dw_down_{i}", tm=1408)
    dg, du, dcw, dcb = _conv_bwd(sv["gu"], 0, small["ffn_conv_w"][i], small["ffn_conv_b"][i], dact.reshape(B, L, D_FF),
                                 taps=FFN_CONV, ct=256, gate_blk0=D_FF // 256, name=f"d_ffn_conv_{i}")
    g["ffn_conv_w"] = jnp.sum(dcw, axis=0)[:FFN_CONV]
    g["ffn_conv_b"] = jnp.sum(dcb, axis=(0, 1))
    dgu = jnp.concatenate([dg, du], axis=-1).reshape(T, 2 * D_FF)
    dh2 = _mm(dgu, wts["w_up"][i], tb=True, name=f"d_h2_{i}", tk=1408)
    g["w_up"] = _mm(sv["h2"], dgu, ta=True, name=f"dw_up_{i}", tn=1408)
    dx_mid, dw2 = _rmsnorm_bwd(sv["x_mid"], small["norm2_w"][i], dh2, dx_out, name=f"d_norm2_{i}")
    g["norm2_w"] = dw2[0]
    dmb = dx_mid.astype(BF16)
    w_out = wts["w_out"][i]
    dy_ssm = _mm(dmb, w_out[:SSM_WIDTH], tb=True, name=f"d_y_ssm_{i}")
    dy_attn = _mm(dmb, w_out[SSM_WIDTH:], tb=True, out_dtype=BF16, name=f"d_y_attn_{i}")
    g["w_out"] = jnp.concatenate([_mm(sv["y_ssm"], dmb, ta=True, name=f"dw_out_a_{i}"),
                                  _mm(sv["y_attn"], dmb, ta=True, name=f"dw_out_b_{i}")], axis=0)
    dypre, dz, dwn = _gate_bwd(sv["ypre"].reshape(T, SSM_WIDTH), sv["proj2"], small["ssm_norm_w"][i], dy_ssm, name=f"d_gate_{i}")
    g["ssm_norm_w"] = jnp.sum(dwn, axis=(0, 1))
    proj = sv["proj2"].reshape(B, L, P_COLS)
    dxs, dbm, dcm, ddt, ddtb, dalog, ddsk = _ssd_bwd(sv["xbc_act"], proj, sv["dtb"], sv["alog"], sv["dskip"], sv["hs"],
                                                    dypre.reshape(B, L, SSM_WIDTH), name=f"d_ssd_{i}")
    g["dt_bias"] = jnp.sum(ddtb, axis=(0, 1, 2))[:2 * SSM_HEADS].reshape(2, SSM_HEADS)
    g["a_log"] = jnp.sum(dalog, axis=(0, 1, 2))[:2 * SSM_HEADS].reshape(2, SSM_HEADS)
    g["d_skip"] = jnp.sum(ddsk.reshape(B, SSM_HEADS, HEAD_DIM), axis=(0, 2))
    dxbc_act = jnp.concatenate([dxs, dbm, dcm], axis=-1)
    dxbc, dcw, dcb = _conv_bwd(proj, 2048 // 256, small["conv_w"][i], small["conv_b"][i], dxbc_act, taps=SSM_CONV, ct=256,
                               name=f"d_ssm_conv_{i}")
    g["conv_w"] = jnp.sum(dcw, axis=0)[:SSM_CONV]
    g["conv_b"] = jnp.sum(dcb, axis=(0, 1))
    dout = _heads_major(dy_attn.reshape(B, L, D), ATTN_HEADS)
    dqt, dkp, dvp, dbias, dsink = _attn_bwd(sv["qt"], sv["kp"], sv["vp"], band_bias, sv["sinkcol"], dout, name=f"d_attn_{i}")
    g["attn_sink"] = jnp.sum(dsink.reshape(ATTN_HEADS, BLOCK), axis=1)
    dq = _tokens_major(dqt)
    dk = _tokens_major(dkp[:, :, WINDOW:WINDOW + L].astype(BF16))
    dv = _tokens_major(dvp[:, :, WINDOW:WINDOW + L].astype(BF16))
    dproj = jnp.concatenate([dq, dz.reshape(B, L, SSM_WIDTH), dxbc, dk, dv, ddt.astype(BF16)], axis=-1).reshape(T, P_COLS)
    dh = _mm(dproj, wts["w_in"][i], tb=True, name=f"d_h_{i}", tk=1408)
    g["w_in"] = _unperm_in_cols(_mm(sv["h"], dproj, ta=True, name=f"dw_in_{i}", tn=1408))
    dx_in, dw1 = _rmsnorm_bwd(sv["x2"], small["norm1_w"][i], dh, dx_mid, name=f"d_norm1_{i}")
    g["norm1_w"] = dw1[0]
    return dx_in, g, dbias


_BIG = ("w_in", "w_out", "w_up", "w_down")
_BIG_AXIS = {"w_in": 2, "w_out": 1, "w_up": 2, "w_down": 1}
_SMALL = ("rel_bias", "norm1_w", "conv_w", "conv_b", "dt_bias", "a_log", "d_skip", "ssm_norm_w", "attn_sink", "norm2_w",
          "ffn_conv_w", "ffn_conv_b", "final_norm_w")
_SMALL_SHARDED = ("conv_w", "ffn_conv_w")
_ORDER = ("rel_bias", "norm1_w", "w_in", "conv_w", "conv_b", "dt_bias", "a_log", "d_skip", "ssm_norm_w", "attn_sink", "w_out",
          "norm2_w", "w_up", "ffn_conv_w", "ffn_conv_b", "w_down", "final_norm_w")


def _local_step(x, target, wts, small):
    B, L, D = x.shape
    bucket = _bucket_table()
    band_bias = _bias_expand(small["rel_bias"], bucket, name="band_bias")
    saved = []
    for i in range(DEPTH):
        x, sv = _layer_fwd(i, x, wts, small, band_bias)
        saved.append(sv)
    loss, dx, dwf = _loss_head(x.reshape(B * L, D), small["final_norm_w"], target.reshape(B * L, D), name="loss_head")
    per_layer = []
    dbias = jnp.zeros((ATTN_HEADS, BLOCK, KEY_SPAN), F32)
    for i in reversed(range(DEPTH)):
        dx, g, dbias_i = _layer_bwd(i, dx, saved[i], wts, small, band_bias)
        dbias = dbias + dbias_i
        per_layer.append(g)
    per_layer.reverse()
    grads = {k: jnp.stack([g[k] for g in per_layer]) for k in per_layer[0]}
    grads["rel_bias"] = _bias_reduce(dbias, bucket, name="d_rel_bias")
    grads["final_norm_w"] = dwf[0]
    return loss, dx.reshape(B, L, D), grads


def _split_by_chip(g, axis):
    shp = g.shape
    n = shp[axis] // N_CHIP
    g = g.reshape(shp[:axis] + (N_CHIP, n) + shp[axis + 1:])
    return jnp.moveaxis(g, axis, 0)


def _join_chips(a, axis):
    a = jnp.moveaxis(a, 0, axis)
    shp = a.shape
    return a.reshape(shp[:axis] + (shp[axis] * shp[axis + 1],) + shp[axis + 2:])


def kernel(x, rel_bias, norm1_w, w_in, conv_w, conv_b, dt_bias, a_log, d_skip, ssm_norm_w, attn_sink, w_out, norm2_w, w_up, ffn_conv_w, ffn_conv_b, w_down, final_norm_w, loss_target, m_rel_bias, m_norm1_w, m_w_in, m_conv_w, m_conv_b, m_dt_bias, m_a_log, m_d_skip, m_ssm_norm_w, m_attn_sink, m_w_out, m_norm2_w, m_w_up, m_ffn_conv_w, m_ffn_conv_b, m_w_down, m_final_norm_w, v_rel_bias, v_norm1_w, v_w_in, v_conv_w, v_conv_b, v_dt_bias, v_a_log, v_d_skip, v_ssm_norm_w, v_attn_sink, v_w_out, v_norm2_w, v_w_up, v_ffn_conv_w, v_ffn_conv_b, v_w_down, v_final_norm_w):
    w = dict(rel_bias=rel_bias, norm1_w=norm1_w, w_in=w_in, conv_w=conv_w, conv_b=conv_b, dt_bias=dt_bias, a_log=a_log,
             d_skip=d_skip, ssm_norm_w=ssm_norm_w, attn_sink=attn_sink, w_out=w_out, norm2_w=norm2_w, w_up=w_up,
             ffn_conv_w=ffn_conv_w, ffn_conv_b=ffn_conv_b, w_down=w_down, final_norm_w=final_norm_w)
    m = dict(rel_bias=m_rel_bias, norm1_w=m_norm1_w, w_in=m_w_in, conv_w=m_conv_w, conv_b=m_conv_b, dt_bias=m_dt_bias,
             a_log=m_a_log, d_skip=m_d_skip, ssm_norm_w=m_ssm_norm_w, attn_sink=m_attn_sink, w_out=m_w_out, norm2_w=m_norm2_w,
             w_up=m_w_up, ffn_conv_w=m_ffn_conv_w, ffn_conv_b=m_ffn_conv_b, w_down=m_w_down, final_norm_w=m_final_norm_w)
    v = dict(rel_bias=v_rel_bias, norm1_w=v_norm1_w, w_in=v_w_in, conv_w=v_conv_w, conv_b=v_conv_b, dt_bias=v_dt_bias,
             a_log=v_a_log, d_skip=v_d_skip, ssm_norm_w=v_ssm_norm_w, attn_sink=v_attn_sink, w_out=v_w_out, norm2_w=v_norm2_w,
             w_up=v_w_up, ffn_conv_w=v_ffn_conv_w, ffn_conv_b=v_ffn_conv_b, w_down=v_w_down, final_norm_w=v_final_norm_w)
    my_chip = 2 * lax.axis_index("x") + lax.axis_index("y")

    shards = [w[k].astype(BF16) for k in _BIG]
    gathered = _gather_weights(shards, name="gather_weights")
    gathered = [lax.dynamic_update_index_in_dim(g_, s_, my_chip, 0) for g_, s_ in zip(gathered, shards)]
    wts = {k: _join_chips(a, _BIG_AXIS[k]) for k, a in zip(_BIG, gathered)}
    wts["w_in"] = _perm_in_cols(wts["w_in"])
    conv_shapes = [(DEPTH, SSM_CONV, CONV_CH), (DEPTH, FFN_CONV, D_FF)]
    placed = [lax.dynamic_update_slice_in_dim(jnp.zeros(s, F32), w[k], my_chip * w[k].shape[2], axis=2)
              for k, s in zip(_SMALL_SHARDED, conv_shapes)]
    lead = (lax.axis_index("c") == 0).astype(F32)
    conv_full = _unpack(_allreduce_small(_pack([p * lead for p in placed]), name="gather_conv_weights"), conv_shapes)
    small = {k: w[k] for k in _SMALL}
    small["conv_w"], small["ffn_conv_w"] = conv_full

    loss_part, grad_x, gp = _local_step(x, loss_target, wts, small)

    small_shapes = [small[k].shape for k in _SMALL] + [()]
    red = _unpack(_allreduce_small(_pack([gp[k] for k in _SMALL] + [loss_part[0, :1]]), name="reduce_small"), small_shapes)
    gsmall = dict(zip(_SMALL, red[:-1]))
    loss = red[-1]
    for k in _SMALL_SHARDED:
        n = w[k].shape[2]
        gsmall[k] = lax.dynamic_slice_in_dim(gsmall[k], my_chip * n, n, axis=2)

    core = lax.axis_index("c")
    me = 2 * my_chip + core
    half = DEPTH // 2
    bufs = [_split_by_chip(gp[k], _BIG_AXIS[k]).astype(BF16) for k in _BIG]
    parts = _scatter_grads(bufs, name="scatter_grads")
    halves = []
    for k, b_, p in zip(_BIG, bufs, parts):
        own = lax.dynamic_slice_in_dim(lax.dynamic_index_in_dim(b_, my_chip, 0, keepdims=False), core * half, half, axis=0)
        p = lax.dynamic_update_index_in_dim(p, own, me, 0)
        shp = p.shape
        s = _sum_sources(p.reshape(N_DEV, shp[1] * shp[2], shp[3]), name=f"sum_{k}")
        halves.append(s.reshape(shp[1:]))
    others = _join_halves(halves, name="join_halves")
    gbig = {}
    for k, mine_, theirs_ in zip(_BIG, halves, others):
        full = jnp.zeros((DEPTH,) + mine_.shape[1:], F32)
        full = lax.dynamic_update_slice_in_dim(full, mine_, core * half, axis=0)
        gbig[k] = lax.dynamic_update_slice_in_dim(full, theirs_, (1 - core) * half, axis=0)

    grad, delta, new_m, new_v = {}, {}, {}, {}
    for k in _BIG:
        shp = w[k].shape
        two = lambda a: a.reshape(shp[0] * shp[1], shp[2])
        d_, m_, v_ = _adamw(two(w[k]), two(gbig[k]), two(m[k]), two(v[k]), name=f"adamw_{k}")
        grad[k], delta[k], new_m[k], new_v[k] = gbig[k], d_.reshape(shp), m_.reshape(shp), v_.reshape(shp)
    shapes = [w[k].shape for k in _SMALL]
    d_, m_, v_ = _adamw(_pack([w[k] for k in _SMALL]), _pack([gsmall[k] for k in _SMALL]), _pack([m[k] for k in _SMALL]),
                        _pack([v[k] for k in _SMALL]), name="adamw_small")
    for k, a, b_, c_ in zip(_SMALL, _unpack(d_, shapes), _unpack(m_, shapes), _unpack(v_, shapes)):
        grad[k], delta[k], new_m[k], new_v[k] = gsmall[k], a, b_, c_
    return (loss, grad_x, *[grad[k] for k in _ORDER], *[delta[k] for k in _ORDER], *[new_m[k] for k in _ORDER],
            *[new_v[k] for k in _ORDER])
```

```python
import functools
import math

import jax
import jax.numpy as jnp
import numpy as np
from jax import lax
from jax.experimental import pallas as pl
from jax.experimental.pallas import tpu as pltpu

F32 = jnp.float32
BF16 = jnp.bfloat16
BS = pl.BlockSpec
SDS = jax.ShapeDtypeStruct
MESH = pl.DeviceIdType.MESH

D_MODEL = 1024
DEPTH = 4
SSM_HEADS = 16
SSM_WIDTH = 1024
BC_WIDTH = 256
CONV_CH = 1536
SSM_CONV = 7
CHUNK = 128
ATTN_HEADS = 16
KV_HEADS = 4
HEAD_DIM = 64
WINDOW = 128
BLOCK = 128
KEY_SPAN = 384
REL_BUCKETS = 32
REL_MAX_DIST = 128
D_FF = 2816
FFN_CONV = 3
NORM_EPS = 1e-6
Z_END = 1024
XBC_END = 2560
DT_END = 2592
Q_END = 3616
K_END = 3872
IN_COLS = 4128
P_COLS = 4224
ADAM_LR, ADAM_B1, ADAM_B2, ADAM_EPS, ADAM_WD, ADAM_STEP = 0.001, 0.9, 0.999, 1e-08, 0.01, 10
NEG = -1e30
N_DEV = 8
N_CHIP = 4
LANE = 128
VMEM_LIMIT_BYTES = 48 * 1024 * 1024


def _pc(body, *, name, grid, in_specs, out_specs, out_shape, scratch_shapes=()):
    return pl.pallas_call(
        body, name=name, grid=grid, in_specs=in_specs, out_specs=out_specs, out_shape=out_shape,
        scratch_shapes=list(scratch_shapes),
        compiler_params=pltpu.CompilerParams(dimension_semantics=("arbitrary",) * len(grid),
                                             vmem_limit_bytes=VMEM_LIMIT_BYTES))


def _div_tile(n, pref, mult):
    t = min(pref, n)
    t -= t % mult
    while t >= mult:
        if n % t == 0:
            return t
        t -= mult
    return n


def _mm(a, b, *, name, ta=False, tb=False, add=None, out_dtype=F32, tm=1024, tn=1024, tk=1024):
    if ta:
        K, M = a.shape
    else:
        M, K = a.shape
    N = b.shape[0] if tb else b.shape[1]
    tm, tn, tk = _div_tile(M, tm, LANE), _div_tile(N, tn, LANE), _div_tile(K, tk, LANE)
    nk = K // tk
    dims = (((0,) if ta else (1,), (1,) if tb else (0,)), ((), ()))

    def body_single(*refs):
        r = lax.dot_general(refs[0][...], refs[1][...], dims, preferred_element_type=F32)
        if add is not None:
            r = r + refs[2][...]
        refs[-1][...] = r.astype(out_dtype)

    def body(*refs):
        if add is None:
            a_ref, b_ref, o_ref, acc_ref = refs
        else:
            a_ref, b_ref, add_ref, o_ref, acc_ref = refs
        k = pl.program_id(2)

        @pl.when(k == 0)
        def _():
            acc_ref[...] = jnp.zeros_like(acc_ref)

        acc_ref[...] += lax.dot_general(a_ref[...], b_ref[...], dims, preferred_element_type=F32)

        @pl.when(k == nk - 1)
        def _():
            r = acc_ref[...]
            if add is not None:
                r = r + add_ref[...]
            o_ref[...] = r.astype(out_dtype)

    a_spec = BS((tk, tm), lambda i, j, k: (k, i)) if ta else BS((tm, tk), lambda i, j, k: (i, k))
    b_spec = BS((tn, tk), lambda i, j, k: (j, k)) if tb else BS((tk, tn), lambda i, j, k: (k, j))
    in_specs, args = [a_spec, b_spec], [a, b]
    if add is not None:
        in_specs.append(BS((tm, tn), lambda i, j, k: (i, j)))
        args.append(add)
    return _pc(body_single if nk == 1 else body, name=name, grid=(M // tm, N // tn, nk), in_specs=in_specs,
               out_specs=BS((tm, tn), lambda i, j, k: (i, j)), out_shape=SDS((M, N), out_dtype),
               scratch_shapes=[] if nk == 1 else [pltpu.VMEM((tm, tn), F32)])(*args)


def _dot(a, b, dims):
    return lax.dot_general(a.astype(BF16), b.astype(BF16), (dims, ((), ())), preferred_element_type=F32)


@jax.custom_vjp
def _nn(a, b):
    return _dot(a, b, ((1,), (0,)))


@jax.custom_vjp
def _nt(a, b):
    return _dot(a, b, ((1,), (1,)))


@jax.custom_vjp
def _tn(a, b):
    return _dot(a, b, ((0,), (0,)))


_nn.defvjp(lambda a, b: (_nn(a, b), (a, b)), lambda r, g: (_nt(g, r[1]), _tn(r[0], g)))
_nt.defvjp(lambda a, b: (_nt(a, b), (a, b)), lambda r, g: (_nn(g, r[1]), _tn(g, r[0])))
_tn.defvjp(lambda a, b: (_tn(a, b), (a, b)), lambda r, g: (_nt(r[1], g), _nn(r[0], g)))


def _hdot(m, x):
    hi = x.astype(BF16)
    r1 = x - hi.astype(F32)
    lo = r1.astype(BF16)
    lo2 = (r1 - lo.astype(F32)).astype(BF16)
    n = x.shape[1]
    out = lax.dot_general(m.astype(BF16), jnp.concatenate([hi, lo, lo2], axis=1), (((1,), (0,)), ((), ())),
                          preferred_element_type=F32)
    return out[:, :n] + out[:, n:2 * n] + out[:, 2 * n:]


@jax.custom_vjp
def _cumdot(m, mt, x):
    return _hdot(m, x)


_cumdot.defvjp(lambda m, mt, x: (_hdot(m, x), (m, mt)),
               lambda r, g: (jnp.zeros_like(r[0]), jnp.zeros_like(r[1]), _hdot(r[1], g)))


def _sigmoid(x):
    return 1.0 / (1.0 + jnp.exp(-x))


def _softplus(x):
    return jnp.maximum(x, 0.0) + jnp.log(1.0 + jnp.exp(-jnp.abs(x)))


def _rms(x, w):
    return x * lax.rsqrt(jnp.mean(x * x, axis=-1, keepdims=True) + NORM_EPS) * w


def _rmsnorm_fwd(x2, w, *, name):
    T, D = x2.shape
    tr = _div_tile(T, 512, 8)

    def body(x_ref, w_ref, o_ref):
        o_ref[...] = _rms(x_ref[...], w_ref[...]).astype(BF16)

    return _pc(body, name=name, grid=(T // tr,),
               in_specs=[BS((tr, D), lambda i: (i, 0)), BS((1, D), lambda i: (0, 0))],
               out_specs=BS((tr, D), lambda i: (i, 0)), out_shape=SDS((T, D), BF16))(x2, w.reshape(1, D))


def _rmsnorm_bwd(x2, w, dh, resid, *, name):
    T, D = x2.shape
    tr = _div_tile(T, 512, 8)

    def body(x_ref, w_ref, dh_ref, r_ref, dx_ref, dw_ref):
        _, vjp = jax.vjp(_rms, x_ref[...], w_ref[...])
        dx, dw = vjp(dh_ref[...])
        dx_ref[...] = dx + r_ref[...]

        @pl.when(pl.program_id(0) == 0)
        def _():
            dw_ref[...] = jnp.zeros_like(dw_ref)

        dw_ref[...] += dw

    row = BS((tr, D), lambda i: (i, 0))
    one = BS((1, D), lambda i: (0, 0))
    return _pc(body, name=name, grid=(T // tr,), in_specs=[row, one, row, row], out_specs=[row, one],
               out_shape=[SDS((T, D), F32), SDS((1, D), F32)])(x2, w.reshape(1, D), dh, resid)


ROW_PAD = 8


def _pad_rows(x):
    return jnp.concatenate([x, jnp.zeros((ROW_PAD, x.shape[1]), x.dtype)], axis=0)


def _shift_rows(xp, s):
    n = xp.shape[0] - ROW_PAD
    return xp[:n] if s == 0 else pltpu.roll(xp, (-s) % xp.shape[0], 0)[:n]


def _conv_taps(x, taps):
    xp = _pad_rows(x)
    return [_shift_rows(xp, k - taps // 2) for k in range(taps)]


def _conv_pre(xs, w_ref, b_ref):
    c = b_ref[...] + w_ref[0:1, :] * xs[0]
    for k in range(1, len(xs)):
        c = c + w_ref[k:k + 1, :] * xs[k]
    return c


def _conv_fwd(x3, x_blk0, w, b, *, taps, ct, gate_blk0=None, out_dtype, name):
    B, L, _ = x3.shape
    C = w.shape[1]
    wp = jnp.zeros((8, C), F32).at[:taps].set(w)

    def body(*refs):
        if gate_blk0 is None:
            x_ref, w_ref, b_ref, o_ref = refs
        else:
            x_ref, u_ref, w_ref, b_ref, o_ref = refs
        c = _conv_pre(_conv_taps(x_ref[0], taps), w_ref, b_ref)
        y = c * _sigmoid(c)
        if gate_blk0 is not None:
            y = y * u_ref[0]
        o_ref[0] = y.astype(out_dtype)

    in_specs = [BS((1, L, ct), lambda bi, j: (bi, 0, x_blk0 + j))]
    args = [x3]
    if gate_blk0 is not None:
        in_specs.append(BS((1, L, ct), lambda bi, j: (bi, 0, gate_blk0 + j)))
        args.append(x3)
    in_specs += [BS((8, ct), lambda bi, j: (0, j)), BS((1, ct), lambda bi, j: (0, j))]
    args += [wp, b.reshape(1, C)]
    return _pc(body, name=name, grid=(B, C // ct), in_specs=in_specs,
               out_specs=BS((1, L, ct), lambda bi, j: (bi, 0, j)), out_shape=SDS((B, L, C), out_dtype))(*args)


def _conv_bwd(x3, x_blk0, w, b, dy3, *, taps, ct, gate_blk0=None, name):
    B, L, _ = x3.shape
    C = w.shape[1]
    wp = jnp.zeros((8, C), F32).at[:taps].set(w)
    gated = gate_blk0 is not None

    def body(*refs):
        if gated:
            x_ref, u_ref, w_ref, b_ref, dy_ref, dx_ref, du_ref, dw_ref, db_ref = refs
        else:
            x_ref, w_ref, b_ref, dy_ref, dx_ref, dw_ref, db_ref = refs
        xs = _conv_taps(x_ref[0], taps)
        dy = dy_ref[0].astype(F32)
        c = _conv_pre(xs, w_ref, b_ref)
        sg = _sigmoid(c)
        dsilu = sg * (1.0 + c * (1.0 - sg))
        if gated:
            du_ref[0] = (dy * (c * sg)).astype(BF16)
            dc = dy * u_ref[0] * dsilu
        else:
            dc = dy * dsilu
        dcp = _pad_rows(dc)
        dx = jnp.zeros_like(dc)
        dw_ref[0] = jnp.zeros((8, ct), F32)
        for k in range(taps):
            dx = dx + w_ref[k:k + 1, :] * _shift_rows(dcp, taps // 2 - k)
            dw_ref[0, k:k + 1, :] = jnp.sum(dc * xs[k], axis=0, keepdims=True)
        dx_ref[0] = dx.astype(BF16)
        db_ref[0] = jnp.sum(dc, axis=0, keepdims=True)

    xs = BS((1, L, ct), lambda bi, j: (bi, 0, x_blk0 + j))
    ys = BS((1, L, ct), lambda bi, j: (bi, 0, j))
    in_specs, args = [xs], [x3]
    if gated:
        in_specs.append(BS((1, L, ct), lambda bi, j: (bi, 0, gate_blk0 + j)))
        args.append(x3)
    in_specs += [BS((8, ct), lambda bi, j: (0, j)), BS((1, ct), lambda bi, j: (0, j)), ys]
    args += [wp, b.reshape(1, C), dy3]
    out_specs = [ys] + ([ys] if gated else []) + [BS((1, 8, ct), lambda bi, j: (bi, 0, j)), BS((1, 1, ct), lambda bi, j: (bi, 0, j))]
    out_shape = [SDS((B, L, C), BF16)] + ([SDS((B, L, C), BF16)] if gated else []) + [SDS((B, 8, C), F32), SDS((B, 1, C), F32)]
    return _pc(body, name=name, grid=(B, C // ct), in_specs=in_specs, out_specs=out_specs, out_shape=out_shape)(*args)


def _tri(reverse):
    r = lax.broadcasted_iota(jnp.int32, (CHUNK, CHUNK), 0)
    c = lax.broadcasted_iota(jnp.int32, (CHUNK, CHUNK), 1)
    return (c >= r) if reverse else (c <= r)


PAIRS = 2
QUADS = SSM_HEADS // (2 * PAIRS)
QW = PAIRS * LANE


def _ssd_chunk(h0, h1, x0, x1, bm, cm, dtraw, dtb, alog, *, col0, reverse):
    mask = _tri(reverse)
    eye = lax.broadcasted_iota(jnp.int32, (CHUNK, CHUNK), 0) == lax.broadcasted_iota(jnp.int32, (CHUNK, CHUNK), 1)
    lane = lax.broadcasted_iota(jnp.int32, (1, LANE), 1)
    first = lane < HEAD_DIM
    dtc = _softplus(dtraw + dtb)
    adt = dtc * (-jnp.exp(alog))
    cumc = _cumdot(mask.astype(F32), _tri(not reverse).astype(F32), adt)
    totc = jnp.sum(adt, axis=0, keepdims=True)
    cb = _nt(cm, bm)

    def col(v, c):
        return jnp.sum(jnp.where(lane == c, v, 0.0), axis=1, keepdims=True)

    outs, states = [], []
    for p, (hprev, xs) in enumerate(((h0, x0), (h1, x1))):
        c0 = col0 + 2 * p
        cj = (col(cumc, c0), col(cumc, c0 + 1))
        cum = jnp.where(first, cj[0], cj[1])
        tot = jnp.where(first, col(totc, c0), col(totc, c0 + 1))
        xdt = xs * jnp.where(first, col(dtc, c0), col(dtc, c0 + 1))
        y = _nn(cm, hprev) * jnp.exp(cum)
        for j in range(2):
            rj = jnp.sum(jnp.where(eye, cj[j], 0.0), axis=0, keepdims=True)
            dec = jnp.exp(jnp.where(mask, cj[j] - rj, NEG))
            y = y + _nn(cb * dec, jnp.where(first if j == 0 else ~first, xdt, 0.0))
        outs.append(y)
        states.append(hprev * jnp.exp(tot) + _tn(bm, xdt * jnp.exp(tot - cum)))
    return outs[0], outs[1], states[0], states[1]


def _ssd_specs(B, L):
    def lanes(w, blk):
        return BS((1, L, w), blk)

    return [
        lanes(QW, lambda b, q: (b, 0, q)),
        lanes(LANE, lambda b, q: (b, 0, 8 + q // 2)),
        lanes(LANE, lambda b, q: (b, 0, 10 + q // 2)),
        lanes(LANE, lambda b, q: (b, 0, P_COLS // LANE - 1)),
        BS((1, LANE), lambda b, q: (0, 0)),
        BS((1, LANE), lambda b, q: (0, 0)),
        BS((1, QW), lambda b, q: (0, q)),
    ]


def _ssd_slot(d, ci):
    return ci if d == 0 else ci + 1


def _ssd_fwd(xbc_act, proj, dtb, alog, dskip, *, name):
    B, L, _ = xbc_act.shape
    nc = L // CHUNK

    def body(xs_ref, b_ref, c_ref, dt_ref, dtb_ref, alog_ref, dsk_ref, y_ref, hs_ref):
        q = pl.program_id(1)
        dtb_v, alog_v = dtb_ref[...], alog_ref[...]
        y_ref[0] = dsk_ref[...] * xs_ref[0]
        hs_ref[0, 0, 0, 0] = jnp.zeros((LANE, QW), F32)
        hs_ref[0, 0, 1, nc] = jnp.zeros((LANE, QW), F32)

        def step(i, carry):
            cis = (i, nc - 1 - i)
            rows = [pl.ds(pl.multiple_of(ci * CHUNK, CHUNK), CHUNK) for ci in cis]
            res = []
            for d in range(2):
                cur = _ssd_slot(d, cis[d])
                res.append(_ssd_chunk(
                    hs_ref[0, 0, d, cur, :, :LANE], hs_ref[0, 0, d, cur, :, LANE:], xs_ref[0, rows[d], :LANE],
                    xs_ref[0, rows[d], LANE:], b_ref[0, rows[d], :], c_ref[0, rows[d], :], dt_ref[0, rows[d], :], dtb_v, alog_v,
                    col0=SSM_HEADS * d + 2 * PAIRS * q, reverse=d == 1))
            for d in range(2):
                y0, y1, n0, n1 = res[d]
                nxt = _ssd_slot(d, cis[d] + 1 if d == 0 else cis[d] - 1)
                hs_ref[0, 0, d, nxt, :, :LANE] = n0
                hs_ref[0, 0, d, nxt, :, LANE:] = n1
                y_ref[0, rows[d], :LANE] += y0
                y_ref[0, rows[d], LANE:] += y1
            return carry

        lax.fori_loop(0, nc, step, 0)

    return _pc(body, name=name, grid=(B, QUADS), in_specs=_ssd_specs(B, L),
               out_specs=[BS((1, L, QW), lambda b, q: (b, 0, q)),
                          BS((1, 1, 2, nc + 1, LANE, QW), lambda b, q: (b, q, 0, 0, 0, 0))],
               out_shape=[SDS((B, L, SSM_WIDTH), F32), SDS((B, QUADS, 2, nc + 1, LANE, QW), F32)])(
        xbc_act, xbc_act, xbc_act, proj, dtb, alog, dskip)


def _ssd_bwd(xbc_act, proj, dtb, alog, dskip, hs, dy, *, name):
    B, L, _ = xbc_act.shape
    nc = L // CHUNK

    def body(xs_ref, b_ref, c_ref, dt_ref, dtb_ref, alog_ref, dsk_ref, hs_ref, dy_ref,
             dxs_ref, db_ref, dc_ref, ddt_ref, ddtb_ref, dalog_ref, ddsk_ref, dh_ref):
        q = pl.program_id(1)
        dtb_v, alog_v = dtb_ref[...], alog_ref[...]

        @pl.when(q % 2 == 0)
        def _():
            db_ref[...] = jnp.zeros_like(db_ref)
            dc_ref[...] = jnp.zeros_like(dc_ref)

        @pl.when(q == 0)
        def _():
            ddt_ref[...] = jnp.zeros_like(ddt_ref)

        dxs_ref[0] = dy_ref[0] * dsk_ref[...]
        ddsk_ref[0] = jnp.sum(dy_ref[0] * xs_ref[0], axis=0, keepdims=True)
        dh_ref[...] = jnp.zeros_like(dh_ref)

        def step(i, carry):
            g_dtb, g_alog = carry
            cis = (nc - 1 - i, i)
            rows = [pl.ds(pl.multiple_of(ci * CHUNK, CHUNK), CHUNK) for ci in cis]
            res = []
            for d in range(2):
                cur = _ssd_slot(d, cis[d])
                fn = functools.partial(_ssd_chunk, col0=SSM_HEADS * d + 2 * PAIRS * q, reverse=d == 1)
                _, vjp = jax.vjp(fn, hs_ref[0, 0, d, cur, :, :LANE], hs_ref[0, 0, d, cur, :, LANE:], xs_ref[0, rows[d], :LANE],
                                 xs_ref[0, rows[d], LANE:], b_ref[0, rows[d], :], c_ref[0, rows[d], :], dt_ref[0, rows[d], :],
                                 dtb_v, alog_v)
                res.append(vjp((dy_ref[0, rows[d], :LANE], dy_ref[0, rows[d], LANE:], dh_ref[d, :, :LANE], dh_ref[d, :, LANE:])))
            for d in range(2):
                g_h0, g_h1, g_x0, g_x1, g_b, g_c, g_dt, g_dtb1, g_alog1 = res[d]
                dh_ref[d, :, :LANE] = g_h0
                dh_ref[d, :, LANE:] = g_h1
                dxs_ref[0, rows[d], :LANE] += g_x0
                dxs_ref[0, rows[d], LANE:] += g_x1
                db_ref[0, rows[d], :] += g_b
                dc_ref[0, rows[d], :] += g_c
                ddt_ref[0, rows[d], :] += g_dt
                g_dtb, g_alog = g_dtb + g_dtb1, g_alog + g_alog1
            return g_dtb, g_alog

        zero_row = jnp.zeros((1, LANE), F32)
        a0, a1 = lax.fori_loop(0, nc, step, (zero_row, zero_row))
        ddtb_ref[0, 0] = a0
        dalog_ref[0, 0] = a1

    lanes = lambda w, blk: BS((1, L, w), blk)
    in_specs = _ssd_specs(B, L) + [BS((1, 1, 2, nc + 1, LANE, QW), lambda b, q: (b, q, 0, 0, 0, 0)), lanes(QW, lambda b, q: (b, 0, q))]
    out_specs = [lanes(QW, lambda b, q: (b, 0, q)), lanes(LANE, lambda b, q: (b, 0, q // 2)), lanes(LANE, lambda b, q: (b, 0, q // 2)),
                 lanes(LANE, lambda b, q: (b, 0, 0)), BS((1, 1, 1, LANE), lambda b, q: (b, q, 0, 0)),
                 BS((1, 1, 1, LANE), lambda b, q: (b, q, 0, 0)), BS((1, 1, QW), lambda b, q: (b, 0, q))]
    out_shape = [SDS((B, L, SSM_WIDTH), F32), SDS((B, L, BC_WIDTH), F32), SDS((B, L, BC_WIDTH), F32), SDS((B, L, LANE), F32),
                 SDS((B, QUADS, 1, LANE), F32), SDS((B, QUADS, 1, LANE), F32), SDS((B, 1, SSM_WIDTH), F32)]
    return _pc(body, name=name, grid=(B, QUADS), in_specs=in_specs, out_specs=out_specs, out_shape=out_shape,
               scratch_shapes=[pltpu.VMEM((2, LANE, QW), F32)])(
        xbc_act, xbc_act, xbc_act, proj, dtb, alog, dskip, hs, dy)


def _gate_norm(yp, z, w):
    v = yp * (z * _sigmoid(z))
    return v * lax.rsqrt(jnp.mean(v * v, axis=-1, keepdims=True) + NORM_EPS) * w


def _gate_fwd(ypre2, proj2, w, *, name):
    T = ypre2.shape[0]
    tr = _div_tile(T, 512, 8)
    G = 512

    def body(y_ref, z_ref, w_ref, o_ref):
        o_ref[...] = _gate_norm(y_ref[...], z_ref[...], w_ref[...]).astype(BF16)

    return _pc(body, name=name, grid=(T // tr, 2),
               in_specs=[BS((tr, G), lambda i, g: (i, g)), BS((tr, G), lambda i, g: (i, 2 + g)), BS((1, G), lambda i, g: (0, g))],
               out_specs=BS((tr, G), lambda i, g: (i, g)), out_shape=SDS((T, SSM_WIDTH), BF16))(ypre2, proj2, w.reshape(1, -1))


def _gate_bwd(ypre2, proj2, w, dy, *, name):
    T = ypre2.shape[0]
    tr = _div_tile(T, 512, 8)
    G = 512

    def body(y_ref, z_ref, w_ref, dy_ref, dyp_ref, dz_ref, dw_ref):
        _, vjp = jax.vjp(_gate_norm, y_ref[...], z_ref[...], w_ref[...])
        dyp, dz, dw = vjp(dy_ref[...])
        dyp_ref[...] = dyp
        dz_ref[...] = dz.astype(BF16)
        dw_ref[0] = dw

    tile = BS((tr, G), lambda i, g: (i, g))
    return _pc(body, name=name, grid=(T // tr, 2),
               in_specs=[tile, BS((tr, G), lambda i, g: (i, 2 + g)), BS((1, G), lambda i, g: (0, g)), tile],
               out_specs=[tile, tile, BS((1, 1, G), lambda i, g: (i, 0, g))],
               out_shape=[SDS((T, SSM_WIDTH), F32), SDS((T, SSM_WIDTH), BF16), SDS((T // tr, 1, SSM_WIDTH), F32)])(
        ypre2, proj2, w.reshape(1, -1), dy)


def _first_half():
    return lax.broadcasted_iota(jnp.int32, (1, LANE), 1) < HEAD_DIM


def _dup_kv_head(pair, odd):
    rolled = pltpu.roll(pair, HEAD_DIM, 1)
    return jnp.where(_first_half(), rolled, pair) if odd else jnp.where(_first_half(), pair, rolled)


def _stack_heads(quad):
    first = _first_half()
    lo, hi = quad[:, :LANE], quad[:, LANE:]
    return jnp.concatenate([jnp.where(first, lo, 0.0), jnp.where(first, 0.0, lo), jnp.where(first, hi, 0.0),
                            jnp.where(first, 0.0, hi)], axis=0)


def _unstack_heads(o):
    first = _first_half()
    return jnp.concatenate([jnp.where(first, o[:BLOCK], o[BLOCK:2 * BLOCK]), jnp.where(first, o[2 * BLOCK:3 * BLOCK], o[3 * BLOCK:])], axis=1)


def _fold_kv_head(d, odd):
    tot = d + pltpu.roll(d, HEAD_DIM, 1)
    return jnp.where(_first_half(), 0.0, tot) if odd else jnp.where(_first_half(), tot, 0.0)


def _attn_probs(qs, kd, bias, sink, colneg):
    s = lax.dot_general(qs, kd, (((1,), (1,)), ((), ())), preferred_element_type=F32) + bias + colneg
    m = jnp.maximum(jnp.max(s, axis=-1, keepdims=True), sink)
    p = jnp.exp(s - m)
    ps = jnp.exp(sink - m)
    inv = 1.0 / (jnp.sum(p, axis=-1, keepdims=True) + ps)
    return p * inv, ps * inv


def _attn_colneg(n, L):
    kpos = n * BLOCK - WINDOW + lax.broadcasted_iota(jnp.int32, (1, KEY_SPAN), 1)
    return jnp.where((kpos >= 0) & (kpos < L), 0.0, NEG)


def _attn_in_specs(L):
    nblk = L // BLOCK
    kv = lambda o, col: BS((1, BLOCK, 4 * HEAD_DIM), lambda b, n: (b, jnp.clip(n + o, 0, nblk - 1), col))
    kcol, vcol = 3584 // 256, 3840 // 256
    return [BS((1, BLOCK, ATTN_HEADS * HEAD_DIM), lambda b, n: (b, n, 0)), kv(-1, kcol), kv(0, kcol), kv(1, kcol),
            kv(-1, vcol), kv(0, vcol), kv(1, vcol),
            BS((ATTN_HEADS, BLOCK, KEY_SPAN), lambda b, n: (0, 0, 0)), BS((ATTN_HEADS * BLOCK, 1), lambda b, n: (0, 0))]


def _attn_fwd(proj, bias, sinkcol, *, name):
    B, L, _ = proj.shape

    def body(q_ref, k0, k1, k2, v0, v1, v2, bias_ref, sink_ref, o_ref):
        colneg = _attn_colneg(pl.program_id(1), L)
        kcat = jnp.concatenate([k0[0], k1[0], k2[0]], axis=0)
        vcat = jnp.concatenate([v0[0], v1[0], v2[0]], axis=0)
        for g in range(KV_HEADS):
            pair = slice(LANE * (g // 2), LANE * (g // 2) + LANE)
            quad = slice(4 * HEAD_DIM * g, 4 * HEAD_DIM * (g + 1))
            kd = _dup_kv_head(kcat[:, pair], g % 2).astype(BF16)
            vd = _dup_kv_head(vcat[:, pair], g % 2).astype(BF16)
            qs = (_stack_heads(q_ref[0, :, quad]) * HEAD_DIM ** -0.5).astype(BF16)
            pn, _ = _attn_probs(qs, kd, bias_ref[4 * g:4 * g + 4].reshape(4 * BLOCK, KEY_SPAN),
                                sink_ref[4 * BLOCK * g:4 * BLOCK * (g + 1)], colneg)
            o = lax.dot_general(pn.astype(BF16), vd, (((1,), (0,)), ((), ())), preferred_element_type=F32)
            o_ref[0, :, quad] = _unstack_heads(o).astype(BF16)

    return _pc(body, name=name, grid=(B, L // BLOCK), in_specs=_attn_in_specs(L),
               out_specs=BS((1, BLOCK, ATTN_HEADS * HEAD_DIM), lambda b, n: (b, n, 0)),
               out_shape=SDS((B, L, ATTN_HEADS * HEAD_DIM), BF16))(proj, proj, proj, proj, proj, proj, proj, bias, sinkcol)


def _attn_bwd(proj, bias, sinkcol, dout, *, name):
    B, L, _ = proj.shape
    nblk = L // BLOCK
    nn, nt, tn = (((1,), (0,)), ((), ())), (((1,), (1,)), ((), ())), (((0,), (0,)), ((), ()))

    def body(q_ref, k0, k1, k2, v0, v1, v2, bias_ref, sink_ref, do_ref, dq_ref, dk_ref, dv_ref, dbias_ref, dsink_ref):
        b, n = pl.program_id(0), pl.program_id(1)

        @pl.when(n == 0)
        def _():
            dk_ref[...] = jnp.zeros_like(dk_ref)
            dv_ref[...] = jnp.zeros_like(dv_ref)

        @pl.when((n == 0) & (b == 0))
        def _():
            dbias_ref[...] = jnp.zeros_like(dbias_ref)
            dsink_ref[...] = jnp.zeros_like(dsink_ref)

        colneg = _attn_colneg(n, L)
        kcat = jnp.concatenate([k0[0], k1[0], k2[0]], axis=0)
        vcat = jnp.concatenate([v0[0], v1[0], v2[0]], axis=0)
        krows = [pl.ds(pl.multiple_of(jnp.clip(n + o, 0, nblk - 1) * BLOCK, BLOCK), BLOCK) for o in (-1, 0, 1)]
        for g in range(KV_HEADS):
            pair = slice(LANE * (g // 2), LANE * (g // 2) + LANE)
            quad = slice(4 * HEAD_DIM * g, 4 * HEAD_DIM * (g + 1))
            rows = slice(4 * BLOCK * g, 4 * BLOCK * (g + 1))
            kd = _dup_kv_head(kcat[:, pair], g % 2).astype(BF16)
            vd = _dup_kv_head(vcat[:, pair], g % 2).astype(BF16)
            qs = (_stack_heads(q_ref[0, :, quad]) * HEAD_DIM ** -0.5).astype(BF16)
            dos = _stack_heads(do_ref[0, :, quad].astype(F32)).astype(BF16)
            pn, psink = _attn_probs(qs, kd, bias_ref[4 * g:4 * g + 4].reshape(4 * BLOCK, KEY_SPAN), sink_ref[rows], colneg)
            dpn = lax.dot_general(dos, vd, nt, preferred_element_type=F32)
            r = jnp.sum(dpn * pn, axis=-1, keepdims=True)
            ds = pn * (dpn - r)
            dsb = ds.astype(BF16)
            dvd = lax.dot_general(pn.astype(BF16), dos, tn, preferred_element_type=F32)
            dkd = lax.dot_general(dsb, qs, tn, preferred_element_type=F32)
            dqs = lax.dot_general(dsb, kd, nn, preferred_element_type=F32) * HEAD_DIM ** -0.5
            dq_ref[0, :, quad] = _unstack_heads(dqs).astype(BF16)
            dk_g, dv_g = _fold_kv_head(dkd, g % 2), _fold_kv_head(dvd, g % 2)
            for o in range(3):
                dk_ref[0, krows[o], pair] += dk_g[o * BLOCK:(o + 1) * BLOCK]
                dv_ref[0, krows[o], pair] += dv_g[o * BLOCK:(o + 1) * BLOCK]
            dbias_ref[4 * g:4 * g + 4] += ds.reshape(4, BLOCK, KEY_SPAN)
            dsink_ref[rows] += -psink * r

    qspec = BS((1, BLOCK, ATTN_HEADS * HEAD_DIM), lambda b, n: (b, n, 0))
    kvout = BS((1, L, 4 * HEAD_DIM), lambda b, n: (b, 0, 0))
    return _pc(body, name=name, grid=(B, nblk), in_specs=_attn_in_specs(L) + [qspec],
               out_specs=[qspec, kvout, kvout, BS((ATTN_HEADS, BLOCK, KEY_SPAN), lambda b, n: (0, 0, 0)),
                          BS((ATTN_HEADS * BLOCK, 1), lambda b, n: (0, 0))],
               out_shape=[SDS((B, L, ATTN_HEADS * HEAD_DIM), BF16), SDS((B, L, 4 * HEAD_DIM), F32), SDS((B, L, 4 * HEAD_DIM), F32),
                          SDS((ATTN_HEADS, BLOCK, KEY_SPAN), F32), SDS((ATTN_HEADS * BLOCK, 1), F32)])(
        proj, proj, proj, proj, proj, proj, proj, bias, sinkcol, dout)


def _t5_bucket(rel):
    half = REL_BUCKETS // 2
    max_exact = half // 2
    ret = jnp.where(rel > 0, half, 0)
    n = jnp.abs(rel)
    nf = jnp.maximum(n, 1).astype(F32)
    large = max_exact + (jnp.log(nf / max_exact) / math.log(REL_MAX_DIST / max_exact) * (half - max_exact)).astype(jnp.int32)
    large = jnp.minimum(large, half - 1)
    return ret + jnp.where(n < max_exact, n, large)


def _bucket_table():
    rel = jnp.arange(KEY_SPAN)[None, :] - WINDOW - jnp.arange(BLOCK)[:, None]
    return _t5_bucket(rel).astype(jnp.int32)


def _bias_expand(rel_bias, bucket, *, name):
    rbt = jnp.zeros((ATTN_HEADS, 1, LANE), F32).at[:, 0, :REL_BUCKETS].set(rel_bias.T)

    def body(rb_ref, bk_ref, o_ref):
        lane = lax.broadcasted_iota(jnp.int32, (1, LANE), 1)
        row = rb_ref[0]
        bk = bk_ref[...]
        acc = jnp.zeros((BLOCK, KEY_SPAN), F32)
        for r in range(REL_BUCKETS):
            val = jnp.sum(jnp.where(lane == r, row, 0.0), axis=1, keepdims=True)
            acc = jnp.where(bk == r, val, acc)
        rel = (lax.broadcasted_iota(jnp.int32, (BLOCK, KEY_SPAN), 1) - WINDOW
               - lax.broadcasted_iota(jnp.int32, (BLOCK, KEY_SPAN), 0))
        o_ref[0] = jnp.where(jnp.abs(rel) <= WINDOW, acc, NEG)

    return _pc(body, name=name, grid=(ATTN_HEADS,),
               in_specs=[BS((1, 1, LANE), lambda h: (h, 0, 0)), BS((BLOCK, KEY_SPAN), lambda h: (0, 0))],
               out_specs=BS((1, BLOCK, KEY_SPAN), lambda h: (h, 0, 0)), out_shape=SDS((ATTN_HEADS, BLOCK, KEY_SPAN), F32))(rbt, bucket)


def _bias_reduce(dbias, bucket, *, name):
    def body(db_ref, bk_ref, o_ref):
        lane = lax.broadcasted_iota(jnp.int32, (1, LANE), 1)
        x = db_ref[0]
        bk = bk_ref[...]
        acc = jnp.zeros((1, LANE), F32)
        for r in range(REL_BUCKETS):
            part = jnp.sum(jnp.where(bk == r, x, 0.0), axis=1, keepdims=True)
            acc = jnp.where(lane == r, jnp.sum(part, axis=0, keepdims=True), acc)
        o_ref[0] = acc

    out = _pc(body, name=name, grid=(ATTN_HEADS,),
              in_specs=[BS((1, BLOCK, KEY_SPAN), lambda h: (h, 0, 0)), BS((BLOCK, KEY_SPAN), lambda h: (0, 0))],
              out_specs=BS((1, 1, LANE), lambda h: (h, 0, 0)), out_shape=SDS((ATTN_HEADS, 1, LANE), F32))(dbias, bucket)
    return out[:, 0, :REL_BUCKETS].T


def _loss_head(x2, w, target, *, name):
    T, D = x2.shape
    tr = _div_tile(T, 512, 8)

    def tile_loss(x, w, t):
        err = _rms(x, w) - t
        return 0.5 * jnp.sum(jnp.mean(err * err, axis=-1, keepdims=True), axis=0, keepdims=True)

    def body(x_ref, w_ref, t_ref, loss_ref, dx_ref, dw_ref):
        t = t_ref[...]
        l, vjp = jax.vjp(lambda x, w: tile_loss(x, w, t), x_ref[...], w_ref[...])
        dx, dw = vjp(jnp.ones((1, 1), F32))
        dx_ref[...] = dx

        @pl.when(pl.program_id(0) == 0)
        def _():
            dw_ref[...] = jnp.zeros_like(dw_ref)
            loss_ref[...] = jnp.zeros_like(loss_ref)

        dw_ref[...] += dw
        loss_ref[...] += l + jnp.zeros((1, LANE), F32)

    row = BS((tr, D), lambda i: (i, 0))
    one = BS((1, D), lambda i: (0, 0))
    return _pc(body, name=name, grid=(T // tr,), in_specs=[row, one, row],
               out_specs=[BS((1, LANE), lambda i: (0, 0)), row, one],
               out_shape=[SDS((1, LANE), F32), SDS((T, D), F32), SDS((1, D), F32)])(x2, w.reshape(1, D), target)


def _adamw(w2, g2, m2, v2, *, name):
    R, C = w2.shape
    tr = _div_tile(R, 256, 8)
    c1 = 1.0 - ADAM_B1 ** ADAM_STEP
    c2 = 1.0 - ADAM_B2 ** ADAM_STEP

    def body(w_ref, g_ref, m_ref, v_ref, d_ref, nm_ref, nv_ref):
        g = g_ref[...]
        m = ADAM_B1 * m_ref[...] + (1.0 - ADAM_B1) * g
        v = ADAM_B2 * v_ref[...] + (1.0 - ADAM_B2) * (g * g)
        d_ref[...] = -ADAM_LR * ((m / c1) / (jnp.sqrt(v / c2) + ADAM_EPS) + ADAM_WD * w_ref[...])
        nm_ref[...] = m
        nv_ref[...] = v

    t = BS((tr, C), lambda i: (i, 0))
    return _pc(body, name=name, grid=(R // tr,), in_specs=[t, t, t, t], out_specs=[t, t, t],
               out_shape=[SDS((R, C), F32)] * 3)(w2, g2, m2, v2)


def _place():
    return lax.axis_index("x"), lax.axis_index("y"), lax.axis_index("c")


def _gather_weights(shards, *, name):
    na = len(shards)
    half = DEPTH // 2

    def body(*refs):
        ins, outs = refs[:na], refs[na:2 * na]
        send_sems, recv_sems = refs[2 * na:]
        x, y, c = _place()
        me_chip = 2 * x + y
        chips = [(1 - x, y), (x, 1 - y), (1 - x, 1 - y)]
        mine = pl.ds(c * half, half)
        theirs = pl.ds((1 - c) * half, half)
        sends = []
        for a in range(na):
            for k, (px, py) in enumerate(chips):
                cp = pltpu.make_async_remote_copy(ins[a].at[mine], outs[a].at[me_chip, mine], send_sems.at[a, k], recv_sems.at[a, k],
                                                  device_id=(px, py, c), device_id_type=MESH)
                cp.start()
                sends.append(cp)
        for a in range(na):
            for k, (px, py) in enumerate(chips):
                got = outs[a].at[2 * px + py, mine]
                pltpu.make_async_remote_copy(got, got, send_sems.at[a, k], recv_sems.at[a, k],
                                             device_id=(px, py, c), device_id_type=MESH).wait_recv()
                cp = pltpu.make_async_remote_copy(got, got, send_sems.at[a, 3 + k], recv_sems.at[a, 3 + k],
                                                  device_id=(x, y, 1 - c), device_id_type=MESH)
                cp.start()
                sends.append(cp)
        for a in range(na):
            for k, (px, py) in enumerate(chips):
                got = outs[a].at[2 * px + py, theirs]
                pltpu.make_async_remote_copy(got, got, send_sems.at[a, 3 + k], recv_sems.at[a, 3 + k],
                                             device_id=(x, y, 1 - c), device_id_type=MESH).wait_recv()
        for cp in sends:
            cp.wait_send()

    any_spec = BS(memory_space=pl.ANY)
    return pl.pallas_call(
        body, name=name, in_specs=[any_spec] * na, out_specs=[any_spec] * na,
        out_shape=[SDS((N_CHIP,) + s.shape, s.dtype) for s in shards],
        scratch_shapes=[pltpu.SemaphoreType.DMA((na, 6)), pltpu.SemaphoreType.DMA((na, 6))],
        compiler_params=pltpu.CompilerParams(has_side_effects=True))(*shards)


def _scatter_grads(bufs, *, name):
    na = len(bufs)
    half = DEPTH // 2

    def body(*refs):
        ins, outs = refs[:na], refs[na:2 * na]
        send_sems, recv_sems = refs[2 * na:]
        x, y, c = _place()
        me = 4 * x + 2 * y + c
        sends = []
        for a in range(na):
            for r in range(1, N_DEV):
                tx, ty, tc = x ^ (r >> 2), y ^ ((r >> 1) & 1), c ^ (r & 1)
                cp = pltpu.make_async_remote_copy(ins[a].at[2 * tx + ty, pl.ds(tc * half, half)], outs[a].at[me],
                                                  send_sems.at[a, r - 1], recv_sems.at[a, r - 1],
                                                  device_id=(tx, ty, tc), device_id_type=MESH)
                cp.start()
                sends.append(cp)
        for a in range(na):
            for r in range(1, N_DEV):
                tx, ty, tc = x ^ (r >> 2), y ^ ((r >> 1) & 1), c ^ (r & 1)
                got = outs[a].at[4 * tx + 2 * ty + tc]
                pltpu.make_async_remote_copy(got, got, send_sems.at[a, r - 1], recv_sems.at[a, r - 1],
                                             device_id=(tx, ty, tc), device_id_type=MESH).wait_recv()
        for cp in sends:
            cp.wait_send()

    any_spec = BS(memory_space=pl.ANY)
    return pl.pallas_call(
        body, name=name, in_specs=[any_spec] * na, out_specs=[any_spec] * na,
        out_shape=[SDS((N_DEV, half) + b.shape[2:], b.dtype) for b in bufs],
        scratch_shapes=[pltpu.SemaphoreType.DMA((na, N_DEV - 1)), pltpu.SemaphoreType.DMA((na, N_DEV - 1))],
        compiler_params=pltpu.CompilerParams(has_side_effects=True))(*bufs)


def _sum_sources(parts, *, name):
    _, R, C = parts.shape
    tr = _div_tile(R, 256, 16)

    def body(p_ref, o_ref):
        acc = p_ref[0].astype(F32)
        for s in range(1, N_DEV):
            acc = acc + p_ref[s].astype(F32)
        o_ref[...] = acc

    return _pc(body, name=name, grid=(R // tr,), in_specs=[BS((N_DEV, tr, C), lambda i: (0, i, 0))],
               out_specs=BS((tr, C), lambda i: (i, 0)), out_shape=SDS((R, C), F32))(parts)


def _join_halves(halves, *, name):
    na = len(halves)

    def body(*refs):
        ins, outs = refs[:na], refs[na:2 * na]
        send_sems, recv_sems = refs[2 * na:]
        x, y, c = _place()
        cps = []
        for a in range(na):
            cp = pltpu.make_async_remote_copy(ins[a], outs[a], send_sems.at[a], recv_sems.at[a],
                                              device_id=(x, y, 1 - c), device_id_type=MESH)
            cp.start()
            cps.append(cp)
        for cp in cps:
            cp.wait_recv()
        for cp in cps:
            cp.wait_send()

    any_spec = BS(memory_space=pl.ANY)
    return pl.pallas_call(
        body, name=name, in_specs=[any_spec] * na, out_specs=[any_spec] * na,
        out_shape=[SDS(h.shape, h.dtype) for h in halves],
        scratch_shapes=[pltpu.SemaphoreType.DMA((na,)), pltpu.SemaphoreType.DMA((na,))],
        compiler_params=pltpu.CompilerParams(has_side_effects=True))(*halves)


def _allreduce_small(vec, *, name):
    R = vec.shape[0]

    def body(v_ref, o_ref, all_ref, send_sems, recv_sems):
        x, y, c = _place()
        me = 4 * x + 2 * y + c
        all_ref[me] = v_ref[...]
        sends = []
        for r in range(1, N_DEV):
            tgt = (x ^ (r >> 2), y ^ ((r >> 1) & 1), c ^ (r & 1))
            cp = pltpu.make_async_remote_copy(v_ref, all_ref.at[me], send_sems.at[r - 1], recv_sems.at[r - 1],
                                              device_id=tgt, device_id_type=MESH)
            cp.start()
            sends.append(cp)
        for r in range(1, N_DEV):
            tx, ty, tc = x ^ (r >> 2), y ^ ((r >> 1) & 1), c ^ (r & 1)
            got = all_ref.at[4 * tx + 2 * ty + tc]
            pltpu.make_async_remote_copy(got, got, send_sems.at[r - 1], recv_sems.at[r - 1],
                                         device_id=(tx, ty, tc), device_id_type=MESH).wait_recv()
        for cp in sends:
            cp.wait_send()
        acc = all_ref[0]
        for s in range(1, N_DEV):
            acc = acc + all_ref[s]
        o_ref[...] = acc

    vm = BS(memory_space=pltpu.VMEM)
    return pl.pallas_call(
        body, name=name, in_specs=[vm], out_specs=vm, out_shape=SDS((R, LANE), F32),
        scratch_shapes=[pltpu.VMEM((N_DEV, R, LANE), F32), pltpu.SemaphoreType.DMA((N_DEV - 1,)), pltpu.SemaphoreType.DMA((N_DEV - 1,))],
        compiler_params=pltpu.CompilerParams(has_side_effects=True, vmem_limit_bytes=VMEM_LIMIT_BYTES))(vec)


def _pack(arrs):
    rows = []
    for a in arrs:
        f = a.reshape(-1).astype(F32)
        n = -(-f.shape[0] // LANE) * LANE
        rows.append(jnp.pad(f, (0, n - f.shape[0])).reshape(-1, LANE))
    v = jnp.concatenate(rows, axis=0)
    pad = -v.shape[0] % 8
    return jnp.pad(v, ((0, pad), (0, 0)))


def _unpack(v, shapes):
    out, r = [], 0
    for s in shapes:
        n = int(np.prod(s)) if len(s) else 1
        nr = -(-n // LANE)
        out.append(v[r:r + nr].reshape(-1)[:n].reshape(s))
        r += nr
    return out


def _perm_in_cols(w_full):
    z, xbc, dt, q, k, v = (w_full[..., :Z_END], w_full[..., Z_END:XBC_END], w_full[..., XBC_END:DT_END],
                           w_full[..., DT_END:Q_END], w_full[..., Q_END:K_END], w_full[..., K_END:])
    pad = jnp.zeros(dt.shape[:-1] + (LANE - dt.shape[-1],), dt.dtype)
    return jnp.concatenate([q, z, xbc, k, v, dt, pad], axis=-1)


def _unperm_in_cols(g):
    q, z, xbc, k, v, dt = (g[..., :1024], g[..., 1024:2048], g[..., 2048:3584], g[..., 3584:3840], g[..., 3840:4096],
                           g[..., 4096:4096 + 2 * SSM_HEADS])
    return jnp.concatenate([z, xbc, dt, q, k, v], axis=-1)


def _dt_cols(a):
    return jnp.pad(a.reshape(1, 2 * SSM_HEADS), ((0, 0), (0, LANE - 2 * SSM_HEADS)))


def _layer_fwd(i, x, wts, small, band_bias):
    B, L, D = x.shape
    T = B * L
    x2 = x.reshape(T, D)
    h = _rmsnorm_fwd(x2, small["norm1_w"][i], name=f"norm1_{i}")
    proj2 = _mm(h, wts["w_in"][i], name=f"in_proj_{i}", tn=1408)
    proj = proj2.reshape(B, L, P_COLS)
    xbc_act = _conv_fwd(proj, 2048 // 256, small["conv_w"][i], small["conv_b"][i], taps=SSM_CONV, ct=256,
                        out_dtype=F32, name=f"ssm_conv_{i}")
    dtb, alog = _dt_cols(small["dt_bias"][i]), _dt_cols(small["a_log"][i])
    dskip = jnp.repeat(small["d_skip"][i], HEAD_DIM).reshape(1, SSM_WIDTH)
    ypre, hs = _ssd_fwd(xbc_act, proj, dtb, alog, dskip, name=f"ssd_{i}")
    y_ssm = _gate_fwd(ypre.reshape(T, SSM_WIDTH), proj2, small["ssm_norm_w"][i], name=f"gate_{i}")
    sinkcol = jnp.repeat(small["attn_sink"][i], BLOCK).reshape(ATTN_HEADS * BLOCK, 1)
    y_attn = _attn_fwd(proj, band_bias, sinkcol, name=f"attn_{i}").reshape(T, D)
    w_out = wts["w_out"][i]
    x_mid = _mm(y_ssm, w_out[:SSM_WIDTH], add=x2, name=f"out_proj_a_{i}")
    x_mid = _mm(y_attn, w_out[SSM_WIDTH:], add=x_mid, name=f"out_proj_b_{i}")
    h2 = _rmsnorm_fwd(x_mid, small["norm2_w"][i], name=f"norm2_{i}")
    gu2 = _mm(h2, wts["w_up"][i], name=f"up_proj_{i}", tn=1408)
    gu = gu2.reshape(B, L, 2 * D_FF)
    act = _conv_fwd(gu, 0, small["ffn_conv_w"][i], small["ffn_conv_b"][i], taps=FFN_CONV, ct=256, gate_blk0=D_FF // 256,
                    out_dtype=BF16, name=f"ffn_conv_{i}")
    x_out = _mm(act.reshape(T, D_FF), wts["w_down"][i], add=x_mid, name=f"down_proj_{i}", tk=1408)
    saved = dict(x2=x2, h=h, proj2=proj2, xbc_act=xbc_act, dtb=dtb, alog=alog, dskip=dskip, ypre=ypre, hs=hs, y_ssm=y_ssm,
                 sinkcol=sinkcol, y_attn=y_attn, x_mid=x_mid, h2=h2, gu=gu, act=act)
    return x_out.reshape(B, L, D), saved


def _layer_bwd(i, dx_out, sv, wts, small, band_bias):
    T, D = dx_out.shape
    B, L = sv["gu"].shape[:2]
    g = {}
    dxb = dx_out.astype(BF16)
    dact = _mm(dxb, wts["w_down"][i], tb=True, out_dtype=BF16, name=f"d_act_{i}", tn=1408)
    g["w_down"] = _mm(sv["act"].reshape(T, D_FF), dxb, ta=True, name=f"dw_down_{i}", tm=1408)
    dg, du, dcw, dcb = _conv_bwd(sv["gu"], 0, small["ffn_conv_w"][i], small["ffn_conv_b"][i], dact.reshape(B, L, D_FF),
                                 taps=FFN_CONV, ct=256, gate_blk0=D_FF // 256, name=f"d_ffn_conv_{i}")
    g["ffn_conv_w"] = jnp.sum(dcw, axis=0)[:FFN_CONV]
    g["ffn_conv_b"] = jnp.sum(dcb, axis=(0, 1))
    dgu = jnp.concatenate([dg, du], axis=-1).reshape(T, 2 * D_FF)
    dh2 = _mm(dgu, wts["w_up"][i], tb=True, name=f"d_h2_{i}", tk=1408)
    g["w_up"] = _mm(sv["h2"], dgu, ta=True, name=f"dw_up_{i}", tn=1408)
    dx_mid, dw2 = _rmsnorm_bwd(sv["x_mid"], small["norm2_w"][i], dh2, dx_out, name=f"d_norm2_{i}")
    g["norm2_w"] = dw2[0]
    dmb = dx_mid.astype(BF16)
    w_out = wts["w_out"][i]
    dy_ssm = _mm(dmb, w_out[:SSM_WIDTH], tb=True, name=f"d_y_ssm_{i}")
    dy_attn = _mm(dmb, w_out[SSM_WIDTH:], tb=True, out_dtype=BF16, name=f"d_y_attn_{i}")
    g["w_out"] = jnp.concatenate([_mm(sv["y_ssm"], dmb, ta=True, name=f"dw_out_a_{i}"),
                                  _mm(sv["y_attn"], dmb, ta=True, name=f"dw_out_b_{i}")], axis=0)
    dypre, dz, dwn = _gate_bwd(sv["ypre"].reshape(T, SSM_WIDTH), sv["proj2"], small["ssm_norm_w"][i], dy_ssm, name=f"d_gate_{i}")
    g["ssm_norm_w"] = jnp.sum(dwn, axis=(0, 1))
    proj = sv["proj2"].reshape(B, L, P_COLS)
    dxs, dbm, dcm, ddt, ddtb, dalog, ddsk = _ssd_bwd(sv["xbc_act"], proj, sv["dtb"], sv["alog"], sv["dskip"], sv["hs"],
                                                    dypre.reshape(B, L, SSM_WIDTH), name=f"d_ssd_{i}")
    g["dt_bias"] = jnp.sum(ddtb, axis=(0, 1, 2))[:2 * SSM_HEADS].reshape(2, SSM_HEADS)
    g["a_log"] = jnp.sum(dalog, axis=(0, 1, 2))[:2 * SSM_HEADS].reshape(2, SSM_HEADS)
    g["d_skip"] = jnp.sum(ddsk.reshape(B, SSM_HEADS, HEAD_DIM), axis=(0, 2))
    dxbc_act = jnp.concatenate([dxs, dbm, dcm], axis=-1)
    dxbc, dcw, dcb = _conv_bwd(proj, 2048 // 256, small["conv_w"][i], small["conv_b"][i], dxbc_act, taps=SSM_CONV, ct=256,
                               name=f"d_ssm_conv_{i}")
    g["conv_w"] = jnp.sum(dcw, axis=0)[:SSM_CONV]
    g["conv_b"] = jnp.sum(dcb, axis=(0, 1))
    dq, dk, dv, dbias, dsink = _attn_bwd(proj, band_bias, sv["sinkcol"], dy_attn.reshape(B, L, D), name=f"d_attn_{i}")
    g["attn_sink"] = jnp.sum(dsink.reshape(ATTN_HEADS, BLOCK), axis=1)
    dproj = jnp.concatenate([dq, dz.reshape(B, L, SSM_WIDTH), dxbc, dk.astype(BF16), dv.astype(BF16), ddt.astype(BF16)],
                            axis=-1).reshape(T, P_COLS)
    dh = _mm(dproj, wts["w_in"][i], tb=True, name=f"d_h_{i}", tk=1408)
    g["w_in"] = _unperm_in_cols(_mm(sv["h"], dproj, ta=True, name=f"dw_in_{i}", tn=1408))
    dx_in, dw1 = _rmsnorm_bwd(sv["x2"], small["norm1_w"][i], dh, dx_mid, name=f"d_norm1_{i}")
    g["norm1_w"] = dw1[0]
    return dx_in, g, dbias


_BIG = ("w_in", "w_out", "w_up", "w_down")
_BIG_AXIS = {"w_in": 2, "w_out": 1, "w_up": 2, "w_down": 1}
_SMALL = ("rel_bias", "norm1_w", "conv_w", "conv_b", "dt_bias", "a_log", "d_skip", "ssm_norm_w", "attn_sink", "norm2_w",
          "ffn_conv_w", "ffn_conv_b", "final_norm_w")
_SMALL_SHARDED = ("conv_w", "ffn_conv_w")
_ORDER = ("rel_bias", "norm1_w", "w_in", "conv_w", "conv_b", "dt_bias", "a_log", "d_skip", "ssm_norm_w", "attn_sink", "w_out",
          "norm2_w", "w_up", "ffn_conv_w", "ffn_conv_b", "w_down", "final_norm_w")


def _local_step(x, target, wts, small):
    B, L, D = x.shape
    bucket = _bucket_table()
    band_bias = _bias_expand(small["rel_bias"], bucket, name="band_bias")
    saved = []
    for i in range(DEPTH):
        x, sv = _layer_fwd(i, x, wts, small, band_bias)
        saved.append(sv)
    loss, dx, dwf = _loss_head(x.reshape(B * L, D), small["final_norm_w"], target.reshape(B * L, D), name="loss_head")
    per_layer = []
    dbias = jnp.zeros((ATTN_HEADS, BLOCK, KEY_SPAN), F32)
    for i in reversed(range(DEPTH)):
        dx, g, dbias_i = _layer_bwd(i, dx, saved[i], wts, small, band_bias)
        dbias = dbias + dbias_i
        per_layer.append(g)
    per_layer.reverse()
    grads = {k: jnp.stack([g[k] for g in per_layer]) for k in per_layer[0]}
    grads["rel_bias"] = _bias_reduce(dbias, bucket, name="d_rel_bias")
    grads["final_norm_w"] = dwf[0]
    return loss, dx.reshape(B, L, D), grads


def _split_by_chip(g, axis):
    shp = g.shape
    n = shp[axis] // N_CHIP
    g = g.reshape(shp[:axis] + (N_CHIP, n) + shp[axis + 1:])
    return jnp.moveaxis(g, axis, 0)


def _join_chips(a, axis):
    a = jnp.moveaxis(a, 0, axis)
    shp = a.shape
    return a.reshape(shp[:axis] + (shp[axis] * shp[axis + 1],) + shp[axis + 2:])


def kernel(x, rel_bias, norm1_w, w_in, conv_w, conv_b, dt_bias, a_log, d_skip, ssm_norm_w, attn_sink, w_out, norm2_w, w_up, ffn_conv_w, ffn_conv_b, w_down, final_norm_w, loss_target, m_rel_bias, m_norm1_w, m_w_in, m_conv_w, m_conv_b, m_dt_bias, m_a_log, m_d_skip, m_ssm_norm_w, m_attn_sink, m_w_out, m_norm2_w, m_w_up, m_ffn_conv_w, m_ffn_conv_b, m_w_down, m_final_norm_w, v_rel_bias, v_norm1_w, v_w_in, v_conv_w, v_conv_b, v_dt_bias, v_a_log, v_d_skip, v_ssm_norm_w, v_attn_sink, v_w_out, v_norm2_w, v_w_up, v_ffn_conv_w, v_ffn_conv_b, v_w_down, v_final_norm_w):
    w = dict(rel_bias=rel_bias, norm1_w=norm1_w, w_in=w_in, conv_w=conv_w, conv_b=conv_b, dt_bias=dt_bias, a_log=a_log,
             d_skip=d_skip, ssm_norm_w=ssm_norm_w, attn_sink=attn_sink, w_out=w_out, norm2_w=norm2_w, w_up=w_up,
             ffn_conv_w=ffn_conv_w, ffn_conv_b=ffn_conv_b, w_down=w_down, final_norm_w=final_norm_w)
    m = dict(rel_bias=m_rel_bias, norm1_w=m_norm1_w, w_in=m_w_in, conv_w=m_conv_w, conv_b=m_conv_b, dt_bias=m_dt_bias,
             a_log=m_a_log, d_skip=m_d_skip, ssm_norm_w=m_ssm_norm_w, attn_sink=m_attn_sink, w_out=m_w_out, norm2_w=m_norm2_w,
             w_up=m_w_up, ffn_conv_w=m_ffn_conv_w, ffn_conv_b=m_ffn_conv_b, w_down=m_w_down, final_norm_w=m_final_norm_w)
    v = dict(rel_bias=v_rel_bias, norm1_w=v_norm1_w, w_in=v_w_in, conv_w=v_conv_w, conv_b=v_conv_b, dt_bias=v_dt_bias,
             a_log=v_a_log, d_skip=v_d_skip, ssm_norm_w=v_ssm_norm_w, attn_sink=v_attn_sink, w_out=v_w_out, norm2_w=v_norm2_w,
             w_up=v_w_up, ffn_conv_w=v_ffn_conv_w, ffn_conv_b=v_ffn_conv_b, w_down=v_w_down, final_norm_w=v_final_norm_w)
    my_chip = 2 * lax.axis_index("x") + lax.axis_index("y")

    shards = [w[k].astype(BF16) for k in _BIG]
    gathered = _gather_weights(shards, name="gather_weights")
    gathered = [lax.dynamic_update_index_in_dim(g_, s_, my_chip, 0) for g_, s_ in zip(gathered, shards)]
    wts = {k: _join_chips(a, _BIG_AXIS[k]) for k, a in zip(_BIG, gathered)}
    wts["w_in"] = _perm_in_cols(wts["w_in"])
    conv_shapes = [(DEPTH, SSM_CONV, CONV_CH), (DEPTH, FFN_CONV, D_FF)]
    placed = [lax.dynamic_update_slice_in_dim(jnp.zeros(s, F32), w[k], my_chip * w[k].shape[2], axis=2)
              for k, s in zip(_SMALL_SHARDED, conv_shapes)]
    lead = (lax.axis_index("c") == 0).astype(F32)
    conv_full = _unpack(_allreduce_small(_pack([p * lead for p in placed]), name="gather_conv_weights"), conv_shapes)
    small = {k: w[k] for k in _SMALL}
    small["conv_w"], small["ffn_conv_w"] = conv_full

    loss_part, grad_x, gp = _local_step(x, loss_target, wts, small)

    small_shapes = [small[k].shape for k in _SMALL] + [()]
    red = _unpack(_allreduce_small(_pack([gp[k] for k in _SMALL] + [loss_part[0, :1]]), name="reduce_small"), small_shapes)
    gsmall = dict(zip(_SMALL, red[:-1]))
    loss = red[-1]
    for k in _SMALL_SHARDED:
        n = w[k].shape[2]
        gsmall[k] = lax.dynamic_slice_in_dim(gsmall[k], my_chip * n, n, axis=2)

    core = lax.axis_index("c")
    me = 2 * my_chip + core
    half = DEPTH // 2
    bufs = [_split_by_chip(gp[k], _BIG_AXIS[k]).astype(BF16) for k in _BIG]
    parts = _scatter_grads(bufs, name="scatter_grads")
    halves = []
    for k, b_, p in zip(_BIG, bufs, parts):
        own = lax.dynamic_slice_in_dim(lax.dynamic_index_in_dim(b_, my_chip, 0, keepdims=False), core * half, half, axis=0)
        p = lax.dynamic_update_index_in_dim(p, own, me, 0)
        shp = p.shape
        s = _sum_sources(p.reshape(N_DEV, shp[1] * shp[2], shp[3]), name=f"sum_{k}")
        halves.append(s.reshape(shp[1:]))
    others = _join_halves(halves, name="join_halves")
    gbig = {}
    for k, mine_, theirs_ in zip(_BIG, halves, others):
        full = jnp.zeros((DEPTH,) + mine_.shape[1:], F32)
        full = lax.dynamic_update_slice_in_dim(full, mine_, core * half, axis=0)
        gbig[k] = lax.dynamic_update_slice_in_dim(full, theirs_, (1 - core) * half, axis=0)

    grad, delta, new_m, new_v = {}, {}, {}, {}
    for k in _BIG:
        shp = w[k].shape
        two = lambda a: a.reshape(shp[0] * shp[1], shp[2])
        d_, m_, v_ = _adamw(two(w[k]), two(gbig[k]), two(m[k]), two(v[k]), name=f"adamw_{k}")
        grad[k], delta[k], new_m[k], new_v[k] = gbig[k], d_.reshape(shp), m_.reshape(shp), v_.reshape(shp)
    shapes = [w[k].shape for k in _SMALL]
    d_, m_, v_ = _adamw(_pack([w[k] for k in _SMALL]), _pack([gsmall[k] for k in _SMALL]), _pack([m[k] for k in _SMALL]),
                        _pack([v[k] for k in _SMALL]), name="adamw_small")
    for k, a, b_, c_ in zip(_SMALL, _unpack(d_, shapes), _unpack(m_, shapes), _unpack(v_, shapes)):
        grad[k], delta[k], new_m[k], new_v[k] = gsmall[k], a, b_, c_
    return (loss, grad_x, *[grad[k] for k in _ORDER], *[delta[k] for k in _ORDER], *[new_m[k] for k in _ORDER],
            *[new_v[k] for k in _ORDER])
```

```python
import functools
import math

import jax
import jax.numpy as jnp
import numpy as np
from jax import lax
from jax.experimental import pallas as pl
from jax.experimental.pallas import tpu as pltpu

F32 = jnp.float32
BF16 = jnp.bfloat16
BS = pl.BlockSpec
SDS = jax.ShapeDtypeStruct
MESH = pl.DeviceIdType.MESH

D_MODEL = 1024
DEPTH = 4
SSM_HEADS = 16
SSM_WIDTH = 1024
BC_WIDTH = 256
CONV_CH = 1536
SSM_CONV = 7
CHUNK = 128
ATTN_HEADS = 16
KV_HEADS = 4
HEAD_DIM = 64
WINDOW = 128
BLOCK = 128
KEY_SPAN = 384
REL_BUCKETS = 32
REL_MAX_DIST = 128
D_FF = 2816
FFN_CONV = 3
NORM_EPS = 1e-6
Z_END = 1024
XBC_END = 2560
DT_END = 2592
Q_END = 3616
K_END = 3872
IN_COLS = 4128
P_COLS = 4224
ADAM_LR, ADAM_B1, ADAM_B2, ADAM_EPS, ADAM_WD, ADAM_STEP = 0.001, 0.9, 0.999, 1e-08, 0.01, 10
NEG = -1e30
N_DEV = 8
N_CHIP = 4
LANE = 128
VMEM_LIMIT_BYTES = 48 * 1024 * 1024


def _pc(body, *, name, grid, in_specs, out_specs, out_shape, scratch_shapes=()):
    return pl.pallas_call(
        body, name=name, grid=grid, in_specs=in_specs, out_specs=out_specs, out_shape=out_shape,
        scratch_shapes=list(scratch_shapes),
        compiler_params=pltpu.CompilerParams(dimension_semantics=("arbitrary",) * len(grid),
                                             vmem_limit_bytes=VMEM_LIMIT_BYTES))


def _div_tile(n, pref, mult):
    t = min(pref, n)
    t -= t % mult
    while t >= mult:
        if n % t == 0:
            return t
        t -= mult
    return n


def _mm(a, b, *, name, ta=False, tb=False, add=None, out_dtype=F32, tm=1024, tn=1024, tk=1024):
    if ta:
        K, M = a.shape
    else:
        M, K = a.shape
    N = b.shape[0] if tb else b.shape[1]
    tm, tn, tk = _div_tile(M, tm, LANE), _div_tile(N, tn, LANE), _div_tile(K, tk, LANE)
    nk = K // tk
    dims = (((0,) if ta else (1,), (1,) if tb else (0,)), ((), ()))

    def body_single(*refs):
        r = lax.dot_general(refs[0][...], refs[1][...], dims, preferred_element_type=F32)
        if add is not None:
            r = r + refs[2][...]
        refs[-1][...] = r.astype(out_dtype)

    def body(*refs):
        if add is None:
            a_ref, b_ref, o_ref, acc_ref = refs
        else:
            a_ref, b_ref, add_ref, o_ref, acc_ref = refs
        k = pl.program_id(2)

        @pl.when(k == 0)
        def _():
            acc_ref[...] = jnp.zeros_like(acc_ref)

        acc_ref[...] += lax.dot_general(a_ref[...], b_ref[...], dims, preferred_element_type=F32)

        @pl.when(k == nk - 1)
        def _():
            r = acc_ref[...]
            if add is not None:
                r = r + add_ref[...]
            o_ref[...] = r.astype(out_dtype)

    a_spec = BS((tk, tm), lambda i, j, k: (k, i)) if ta else BS((tm, tk), lambda i, j, k: (i, k))
    b_spec = BS((tn, tk), lambda i, j, k: (j, k)) if tb else BS((tk, tn), lambda i, j, k: (k, j))
    in_specs, args = [a_spec, b_spec], [a, b]
    if add is not None:
        in_specs.append(BS((tm, tn), lambda i, j, k: (i, j)))
        args.append(add)
    return _pc(body_single if nk == 1 else body, name=name, grid=(M // tm, N // tn, nk), in_specs=in_specs,
               out_specs=BS((tm, tn), lambda i, j, k: (i, j)), out_shape=SDS((M, N), out_dtype),
               scratch_shapes=[] if nk == 1 else [pltpu.VMEM((tm, tn), F32)])(*args)


def _dot(a, b, dims):
    return lax.dot_general(a.astype(BF16), b.astype(BF16), (dims, ((), ())), preferred_element_type=F32)


@jax.custom_vjp
def _nn(a, b):
    return _dot(a, b, ((1,), (0,)))


@jax.custom_vjp
def _nt(a, b):
    return _dot(a, b, ((1,), (1,)))


@jax.custom_vjp
def _tn(a, b):
    return _dot(a, b, ((0,), (0,)))


_nn.defvjp(lambda a, b: (_nn(a, b), (a, b)), lambda r, g: (_nt(g, r[1]), _tn(r[0], g)))
_nt.defvjp(lambda a, b: (_nt(a, b), (a, b)), lambda r, g: (_nn(g, r[1]), _tn(g, r[0])))
_tn.defvjp(lambda a, b: (_tn(a, b), (a, b)), lambda r, g: (_nt(r[1], g), _nn(r[0], g)))


def _hdot(m, x):
    hi = x.astype(BF16)
    r1 = x - hi.astype(F32)
    lo = r1.astype(BF16)
    lo2 = (r1 - lo.astype(F32)).astype(BF16)
    n = x.shape[1]
    out = lax.dot_general(m.astype(BF16), jnp.concatenate([hi, lo, lo2], axis=1), (((1,), (0,)), ((), ())),
                          preferred_element_type=F32)
    return out[:, :n] + out[:, n:2 * n] + out[:, 2 * n:]


@jax.custom_vjp
def _cumdot(m, mt, x):
    return _hdot(m, x)


_cumdot.defvjp(lambda m, mt, x: (_hdot(m, x), (m, mt)),
               lambda r, g: (jnp.zeros_like(r[0]), jnp.zeros_like(r[1]), _hdot(r[1], g)))


def _sigmoid(x):
    return 1.0 / (1.0 + jnp.exp(-x))


def _softplus(x):
    return jnp.maximum(x, 0.0) + jnp.log(1.0 + jnp.exp(-jnp.abs(x)))


def _rms(x, w):
    return x * lax.rsqrt(jnp.mean(x * x, axis=-1, keepdims=True) + NORM_EPS) * w


def _rmsnorm_fwd(x2, w, *, name):
    T, D = x2.shape
    tr = _div_tile(T, 512, 8)

    def body(x_ref, w_ref, o_ref):
        o_ref[...] = _rms(x_ref[...], w_ref[...]).astype(BF16)

    return _pc(body, name=name, grid=(T // tr,),
               in_specs=[BS((tr, D), lambda i: (i, 0)), BS((1, D), lambda i: (0, 0))],
               out_specs=BS((tr, D), lambda i: (i, 0)), out_shape=SDS((T, D), BF16))(x2, w.reshape(1, D))


def _rmsnorm_bwd(x2, w, dh, resid, *, name):
    T, D = x2.shape
    tr = _div_tile(T, 512, 8)

    def body(x_ref, w_ref, dh_ref, r_ref, dx_ref, dw_ref):
        _, vjp = jax.vjp(_rms, x_ref[...], w_ref[...])
        dx, dw = vjp(dh_ref[...])
        dx_ref[...] = dx + r_ref[...]

        @pl.when(pl.program_id(0) == 0)
        def _():
            dw_ref[...] = jnp.zeros_like(dw_ref)

        dw_ref[...] += dw

    row = BS((tr, D), lambda i: (i, 0))
    one = BS((1, D), lambda i: (0, 0))
    return _pc(body, name=name, grid=(T // tr,), in_specs=[row, one, row, row], out_specs=[row, one],
               out_shape=[SDS((T, D), F32), SDS((1, D), F32)])(x2, w.reshape(1, D), dh, resid)


ROW_PAD = 8


def _pad_rows(x):
    return jnp.concatenate([x, jnp.zeros((ROW_PAD, x.shape[1]), x.dtype)], axis=0)


def _shift_rows(xp, s):
    n = xp.shape[0] - ROW_PAD
    return xp[:n] if s == 0 else pltpu.roll(xp, (-s) % xp.shape[0], 0)[:n]


def _conv_taps(x, taps):
    xp = _pad_rows(x)
    return [_shift_rows(xp, k - taps // 2) for k in range(taps)]


def _conv_pre(xs, w_ref, b_ref):
    c = b_ref[...] + w_ref[0:1, :] * xs[0]
    for k in range(1, len(xs)):
        c = c + w_ref[k:k + 1, :] * xs[k]
    return c


def _conv_fwd(x3, x_blk0, w, b, *, taps, ct, gate_blk0=None, out_dtype, name):
    B, L, _ = x3.shape
    C = w.shape[1]
    wp = jnp.zeros((8, C), F32).at[:taps].set(w)

    def body(*refs):
        if gate_blk0 is None:
            x_ref, w_ref, b_ref, o_ref = refs
        else:
            x_ref, u_ref, w_ref, b_ref, o_ref = refs
        c = _conv_pre(_conv_taps(x_ref[0], taps), w_ref, b_ref)
        y = c * _sigmoid(c)
        if gate_blk0 is not None:
            y = y * u_ref[0]
        o_ref[0] = y.astype(out_dtype)

    in_specs = [BS((1, L, ct), lambda bi, j: (bi, 0, x_blk0 + j))]
    args = [x3]
    if gate_blk0 is not None:
        in_specs.append(BS((1, L, ct), lambda bi, j: (bi, 0, gate_blk0 + j)))
        args.append(x3)
    in_specs += [BS((8, ct), lambda bi, j: (0, j)), BS((1, ct), lambda bi, j: (0, j))]
    args += [wp, b.reshape(1, C)]
    return _pc(body, name=name, grid=(B, C // ct), in_specs=in_specs,
               out_specs=BS((1, L, ct), lambda bi, j: (bi, 0, j)), out_shape=SDS((B, L, C), out_dtype))(*args)


def _conv_bwd(x3, x_blk0, w, b, dy3, *, taps, ct, gate_blk0=None, name):
    B, L, _ = x3.shape
    C = w.shape[1]
    wp = jnp.zeros((8, C), F32).at[:taps].set(w)
    gated = gate_blk0 is not None

    def body(*refs):
        if gated:
            x_ref, u_ref, w_ref, b_ref, dy_ref, dx_ref, du_ref, dw_ref, db_ref = refs
        else:
            x_ref, w_ref, b_ref, dy_ref, dx_ref, dw_ref, db_ref = refs
        xs = _conv_taps(x_ref[0], taps)
        dy = dy_ref[0].astype(F32)
        c = _conv_pre(xs, w_ref, b_ref)
        sg = _sigmoid(c)
        dsilu = sg * (1.0 + c * (1.0 - sg))
        if gated:
            du_ref[0] = (dy * (c * sg)).astype(BF16)
            dc = dy * u_ref[0] * dsilu
        else:
            dc = dy * dsilu
        dcp = _pad_rows(dc)
        dx = jnp.zeros_like(dc)
        dw_ref[0] = jnp.zeros((8, ct), F32)
        for k in range(taps):
            dx = dx + w_ref[k:k + 1, :] * _shift_rows(dcp, taps // 2 - k)
            dw_ref[0, k:k + 1, :] = jnp.sum(dc * xs[k], axis=0, keepdims=True)
        dx_ref[0] = dx.astype(BF16)
        db_ref[0] = jnp.sum(dc, axis=0, keepdims=True)

    xs = BS((1, L, ct), lambda bi, j: (bi, 0, x_blk0 + j))
    ys = BS((1, L, ct), lambda bi, j: (bi, 0, j))
    in_specs, args = [xs], [x3]
    if gated:
        in_specs.append(BS((1, L, ct), lambda bi, j: (bi, 0, gate_blk0 + j)))
        args.append(x3)
    in_specs += [BS((8, ct), lambda bi, j: (0, j)), BS((1, ct), lambda bi, j: (0, j)), ys]
    args += [wp, b.reshape(1, C), dy3]
    out_specs = [ys] + ([ys] if gated else []) + [BS((1, 8, ct), lambda bi, j: (bi, 0, j)), BS((1, 1, ct), lambda bi, j: (bi, 0, j))]
    out_shape = [SDS((B, L, C), BF16)] + ([SDS((B, L, C), BF16)] if gated else []) + [SDS((B, 8, C), F32), SDS((B, 1, C), F32)]
    return _pc(body, name=name, grid=(B, C // ct), in_specs=in_specs, out_specs=out_specs, out_shape=out_shape)(*args)


def _tri(reverse):
    r = lax.broadcasted_iota(jnp.int32, (CHUNK, CHUNK), 0)
    c = lax.broadcasted_iota(jnp.int32, (CHUNK, CHUNK), 1)
    return (c >= r) if reverse else (c <= r)


PAIRS = 2
QUADS = SSM_HEADS // (2 * PAIRS)
QW = PAIRS * LANE


def _ssd_chunk(h0, h1, x0, x1, bm, cm, dtraw, dtb, alog, *, col0, reverse):
    mask = _tri(reverse)
    eye = lax.broadcasted_iota(jnp.int32, (CHUNK, CHUNK), 0) == lax.broadcasted_iota(jnp.int32, (CHUNK, CHUNK), 1)
    lane = lax.broadcasted_iota(jnp.int32, (1, LANE), 1)
    first = lane < HEAD_DIM
    dtc = _softplus(dtraw + dtb)
    adt = dtc * (-jnp.exp(alog))
    cumc = _cumdot(mask.astype(F32), _tri(not reverse).astype(F32), adt)
    totc = jnp.sum(adt, axis=0, keepdims=True)
    cb = _nt(cm, bm)

    def col(v, c):
        return jnp.sum(jnp.where(lane == c, v, 0.0), axis=1, keepdims=True)

    outs, states = [], []
    for p, (hprev, xs) in enumerate(((h0, x0), (h1, x1))):
        c0 = col0 + 2 * p
        cj = (col(cumc, c0), col(cumc, c0 + 1))
        cum = jnp.where(first, cj[0], cj[1])
        tot = jnp.where(first, col(totc, c0), col(totc, c0 + 1))
        xdt = xs * jnp.where(first, col(dtc, c0), col(dtc, c0 + 1))
        y = _nn(cm, hprev) * jnp.exp(cum)
        for j in range(2):
            rj = jnp.sum(jnp.where(eye, cj[j], 0.0), axis=0, keepdims=True)
            dec = jnp.exp(jnp.where(mask, cj[j] - rj, NEG))
            y = y + _nn(cb * dec, jnp.where(first if j == 0 else ~first, xdt, 0.0))
        outs.append(y)
        states.append(hprev * jnp.exp(tot) + _tn(bm, xdt * jnp.exp(tot - cum)))
    return outs[0], outs[1], states[0], states[1]


def _ssd_specs(B, L):
    def lanes(w, blk):
        return BS((1, L, w), blk)

    return [
        lanes(QW, lambda b, q: (b, 0, q)),
        lanes(LANE, lambda b, q: (b, 0, 8 + q // 2)),
        lanes(LANE, lambda b, q: (b, 0, 10 + q // 2)),
        lanes(LANE, lambda b, q: (b, 0, P_COLS // LANE - 1)),
        BS((1, LANE), lambda b, q: (0, 0)),
        BS((1, LANE), lambda b, q: (0, 0)),
        BS((1, QW), lambda b, q: (0, q)),
    ]


def _ssd_slot(d, ci):
    return ci if d == 0 else ci + 1


def _ssd_fwd(xbc_act, proj, dtb, alog, dskip, *, name):
    B, L, _ = xbc_act.shape
    nc = L // CHUNK

    def body(xs_ref, b_ref, c_ref, dt_ref, dtb_ref, alog_ref, dsk_ref, y_ref, hs_ref):
        q = pl.program_id(1)
        dtb_v, alog_v = dtb_ref[...], alog_ref[...]
        y_ref[0] = dsk_ref[...] * xs_ref[0]
        hs_ref[0, 0, 0, 0] = jnp.zeros((LANE, QW), F32)
        hs_ref[0, 0, 1, nc] = jnp.zeros((LANE, QW), F32)

        def step(i, carry):
            cis = (i, nc - 1 - i)
            rows = [pl.ds(pl.multiple_of(ci * CHUNK, CHUNK), CHUNK) for ci in cis]
            res = []
            for d in range(2):
                cur = _ssd_slot(d, cis[d])
                res.append(_ssd_chunk(
                    hs_ref[0, 0, d, cur, :, :LANE], hs_ref[0, 0, d, cur, :, LANE:], xs_ref[0, rows[d], :LANE],
                    xs_ref[0, rows[d], LANE:], b_ref[0, rows[d], :], c_ref[0, rows[d], :], dt_ref[0, rows[d], :], dtb_v, alog_v,
                    col0=SSM_HEADS * d + 2 * PAIRS * q, reverse=d == 1))
            for d in range(2):
                y0, y1, n0, n1 = res[d]
                nxt = _ssd_slot(d, cis[d] + 1 if d == 0 else cis[d] - 1)
                hs_ref[0, 0, d, nxt, :, :LANE] = n0
                hs_ref[0, 0, d, nxt, :, LANE:] = n1
                y_ref[0, rows[d], :LANE] += y0
                y_ref[0, rows[d], LANE:] += y1
            return carry

        lax.fori_loop(0, nc, step, 0, unroll=2)

    return _pc(body, name=name, grid=(B, QUADS), in_specs=_ssd_specs(B, L),
               out_specs=[BS((1, L, QW), lambda b, q: (b, 0, q)),
                          BS((1, 1, 2, nc + 1, LANE, QW), lambda b, q: (b, q, 0, 0, 0, 0))],
               out_shape=[SDS((B, L, SSM_WIDTH), F32), SDS((B, QUADS, 2, nc + 1, LANE, QW), F32)])(
        xbc_act, xbc_act, xbc_act, proj, dtb, alog, dskip)


def _ssd_bwd(xbc_act, proj, dtb, alog, dskip, hs, dy, *, name):
    B, L, _ = xbc_act.shape
    nc = L // CHUNK

    def body(xs_ref, b_ref, c_ref, dt_ref, dtb_ref, alog_ref, dsk_ref, hs_ref, dy_ref,
             dxs_ref, db_ref, dc_ref, ddt_ref, ddtb_ref, dalog_ref, ddsk_ref, dh_ref):
        q = pl.program_id(1)
        dtb_v, alog_v = dtb_ref[...], alog_ref[...]

        @pl.when(q % 2 == 0)
        def _():
            db_ref[...] = jnp.zeros_like(db_ref)
            dc_ref[...] = jnp.zeros_like(dc_ref)

        @pl.when(q == 0)
        def _():
            ddt_ref[...] = jnp.zeros_like(ddt_ref)

        dxs_ref[0] = dy_ref[0] * dsk_ref[...]
        ddsk_ref[0] = jnp.sum(dy_ref[0] * xs_ref[0], axis=0, keepdims=True)
        dh_ref[...] = jnp.zeros_like(dh_ref)

        def step(i, carry):
            g_dtb, g_alog = carry
            cis = (nc - 1 - i, i)
            rows = [pl.ds(pl.multiple_of(ci * CHUNK, CHUNK), CHUNK) for ci in cis]
            res = []
            for d in range(2):
                cur = _ssd_slot(d, cis[d])
                fn = functools.partial(_ssd_chunk, col0=SSM_HEADS * d + 2 * PAIRS * q, reverse=d == 1)
                _, vjp = jax.vjp(fn, hs_ref[0, 0, d, cur, :, :LANE], hs_ref[0, 0, d, cur, :, LANE:], xs_ref[0, rows[d], :LANE],
                                 xs_ref[0, rows[d], LANE:], b_ref[0, rows[d], :], c_ref[0, rows[d], :], dt_ref[0, rows[d], :],
                                 dtb_v, alog_v)
                res.append(vjp((dy_ref[0, rows[d], :LANE], dy_ref[0, rows[d], LANE:], dh_ref[d, :, :LANE], dh_ref[d, :, LANE:])))
            for d in range(2):
                g_h0, g_h1, g_x0, g_x1, g_b, g_c, g_dt, g_dtb1, g_alog1 = res[d]
                dh_ref[d, :, :LANE] = g_h0
                dh_ref[d, :, LANE:] = g_h1
                dxs_ref[0, rows[d], :LANE] += g_x0
                dxs_ref[0, rows[d], LANE:] += g_x1
                db_ref[0, rows[d], :] += g_b
                dc_ref[0, rows[d], :] += g_c
                ddt_ref[0, rows[d], :] += g_dt
                g_dtb, g_alog = g_dtb + g_dtb1, g_alog + g_alog1
            return g_dtb, g_alog

        zero_row = jnp.zeros((1, LANE), F32)
        a0, a1 = lax.fori_loop(0, nc, step, (zero_row, zero_row))
        ddtb_ref[0, 0] = a0
        dalog_ref[0, 0] = a1

    lanes = lambda w, blk: BS((1, L, w), blk)
    in_specs = _ssd_specs(B, L) + [BS((1, 1, 2, nc + 1, LANE, QW), lambda b, q: (b, q, 0, 0, 0, 0)), lanes(QW, lambda b, q: (b, 0, q))]
    out_specs = [lanes(QW, lambda b, q: (b, 0, q)), lanes(LANE, lambda b, q: (b, 0, q // 2)), lanes(LANE, lambda b, q: (b, 0, q // 2)),
                 lanes(LANE, lambda b, q: (b, 0, 0)), BS((1, 1, 1, LANE), lambda b, q: (b, q, 0, 0)),
                 BS((1, 1, 1, LANE), lambda b, q: (b, q, 0, 0)), BS((1, 1, QW), lambda b, q: (b, 0, q))]
    out_shape = [SDS((B, L, SSM_WIDTH), F32), SDS((B, L, BC_WIDTH), F32), SDS((B, L, BC_WIDTH), F32), SDS((B, L, LANE), F32),
                 SDS((B, QUADS, 1, LANE), F32), SDS((B, QUADS, 1, LANE), F32), SDS((B, 1, SSM_WIDTH), F32)]
    return _pc(body, name=name, grid=(B, QUADS), in_specs=in_specs, out_specs=out_specs, out_shape=out_shape,
               scratch_shapes=[pltpu.VMEM((2, LANE, QW), F32)])(
        xbc_act, xbc_act, xbc_act, proj, dtb, alog, dskip, hs, dy)


def _gate_norm(yp, z, w):
    v = yp * (z * _sigmoid(z))
    return v * lax.rsqrt(jnp.mean(v * v, axis=-1, keepdims=True) + NORM_EPS) * w


def _gate_fwd(ypre2, proj2, w, *, name):
    T = ypre2.shape[0]
    tr = _div_tile(T, 512, 8)
    G = 512

    def body(y_ref, z_ref, w_ref, o_ref):
        o_ref[...] = _gate_norm(y_ref[...], z_ref[...], w_ref[...]).astype(BF16)

    return _pc(body, name=name, grid=(T // tr, 2),
               in_specs=[BS((tr, G), lambda i, g: (i, g)), BS((tr, G), lambda i, g: (i, 2 + g)), BS((1, G), lambda i, g: (0, g))],
               out_specs=BS((tr, G), lambda i, g: (i, g)), out_shape=SDS((T, SSM_WIDTH), BF16))(ypre2, proj2, w.reshape(1, -1))


def _gate_bwd(ypre2, proj2, w, dy, *, name):
    T = ypre2.shape[0]
    tr = _div_tile(T, 512, 8)
    G = 512

    def body(y_ref, z_ref, w_ref, dy_ref, dyp_ref, dz_ref, dw_ref):
        _, vjp = jax.vjp(_gate_norm, y_ref[...], z_ref[...], w_ref[...])
        dyp, dz, dw = vjp(dy_ref[...])
        dyp_ref[...] = dyp
        dz_ref[...] = dz.astype(BF16)
        dw_ref[0] = dw

    tile = BS((tr, G), lambda i, g: (i, g))
    return _pc(body, name=name, grid=(T // tr, 2),
               in_specs=[tile, BS((tr, G), lambda i, g: (i, 2 + g)), BS((1, G), lambda i, g: (0, g)), tile],
               out_specs=[tile, tile, BS((1, 1, G), lambda i, g: (i, 0, g))],
               out_shape=[SDS((T, SSM_WIDTH), F32), SDS((T, SSM_WIDTH), BF16), SDS((T // tr, 1, SSM_WIDTH), F32)])(
        ypre2, proj2, w.reshape(1, -1), dy)


def _first_half():
    return lax.broadcasted_iota(jnp.int32, (1, LANE), 1) < HEAD_DIM


def _dup_kv_head(pair, odd):
    rolled = pltpu.roll(pair, HEAD_DIM, 1)
    return jnp.where(_first_half(), rolled, pair) if odd else jnp.where(_first_half(), pair, rolled)


def _stack_heads(quad):
    first = _first_half()
    lo, hi = quad[:, :LANE], quad[:, LANE:]
    return jnp.concatenate([jnp.where(first, lo, 0.0), jnp.where(first, 0.0, lo), jnp.where(first, hi, 0.0),
                            jnp.where(first, 0.0, hi)], axis=0)


def _unstack_heads(o):
    first = _first_half()
    return jnp.concatenate([jnp.where(first, o[:BLOCK], o[BLOCK:2 * BLOCK]), jnp.where(first, o[2 * BLOCK:3 * BLOCK], o[3 * BLOCK:])], axis=1)


def _fold_kv_head(d, odd):
    tot = d + pltpu.roll(d, HEAD_DIM, 1)
    return jnp.where(_first_half(), 0.0, tot) if odd else jnp.where(_first_half(), tot, 0.0)


def _attn_softmax(s, sink):
    m = jnp.maximum(jnp.max(s, axis=-1, keepdims=True), sink)
    p = jnp.exp(s - m)
    ps = jnp.exp(sink - m)
    inv = 1.0 / (jnp.sum(p, axis=-1, keepdims=True) + ps)
    return p * inv, ps * inv


def _attn_colneg(n, L):
    kpos = n * BLOCK - WINDOW + lax.broadcasted_iota(jnp.int32, (1, KEY_SPAN), 1)
    return jnp.where((kpos >= 0) & (kpos < L), 0.0, NEG)


def _attn_in_specs(L):
    nblk = L // BLOCK
    kv = lambda o, col: BS((1, BLOCK, 4 * HEAD_DIM), lambda b, n: (b, jnp.clip(n + o, 0, nblk - 1), col))
    kcol, vcol = 3584 // 256, 3840 // 256
    return [BS((1, BLOCK, ATTN_HEADS * HEAD_DIM), lambda b, n: (b, n, 0)), kv(-1, kcol), kv(0, kcol), kv(1, kcol),
            kv(-1, vcol), kv(0, vcol), kv(1, vcol),
            BS((ATTN_HEADS, BLOCK, KEY_SPAN), lambda b, n: (0, 0, 0)), BS((ATTN_HEADS * BLOCK, 1), lambda b, n: (0, 0))]


def _attn_fwd(proj, bias, sinkcol, *, name):
    B, L, _ = proj.shape

    def body(q_ref, k0, k1, k2, v0, v1, v2, bias_ref, sink_ref, o_ref):
        colneg = _attn_colneg(pl.program_id(1), L)
        kcat = jnp.concatenate([k0[0], k1[0], k2[0]], axis=0)
        vcat = jnp.concatenate([v0[0], v1[0], v2[0]], axis=0)
        scores, probs = [], []
        for g in range(KV_HEADS):
            pair = slice(LANE * (g // 2), LANE * (g // 2) + LANE)
            quad = slice(4 * HEAD_DIM * g, 4 * HEAD_DIM * (g + 1))
            kd = _dup_kv_head(kcat[:, pair], g % 2).astype(BF16)
            qs = (_stack_heads(q_ref[0, :, quad]) * HEAD_DIM ** -0.5).astype(BF16)
            scores.append(lax.dot_general(qs, kd, (((1,), (1,)), ((), ())), preferred_element_type=F32))
        for g in range(KV_HEADS):
            pn, _ = _attn_softmax(scores[g] + bias_ref[4 * g:4 * g + 4].reshape(4 * BLOCK, KEY_SPAN) + colneg,
                                  sink_ref[4 * BLOCK * g:4 * BLOCK * (g + 1)])
            probs.append(pn.astype(BF16))
        for g in range(KV_HEADS):
            pair = slice(LANE * (g // 2), LANE * (g // 2) + LANE)
            quad = slice(4 * HEAD_DIM * g, 4 * HEAD_DIM * (g + 1))
            vd = _dup_kv_head(vcat[:, pair], g % 2).astype(BF16)
            o = lax.dot_general(probs[g], vd, (((1,), (0,)), ((), ())), preferred_element_type=F32)
            o_ref[0, :, quad] = _unstack_heads(o).astype(BF16)

    return _pc(body, name=name, grid=(B, L // BLOCK), in_specs=_attn_in_specs(L),
               out_specs=BS((1, BLOCK, ATTN_HEADS * HEAD_DIM), lambda b, n: (b, n, 0)),
               out_shape=SDS((B, L, ATTN_HEADS * HEAD_DIM), BF16))(proj, proj, proj, proj, proj, proj, proj, bias, sinkcol)


def _attn_bwd(proj, bias, sinkcol, dout, *, name):
    B, L, _ = proj.shape
    nblk = L // BLOCK
    nn, nt, tn = (((1,), (0,)), ((), ())), (((1,), (1,)), ((), ())), (((0,), (0,)), ((), ()))

    def body(q_ref, k0, k1, k2, v0, v1, v2, bias_ref, sink_ref, do_ref, dq_ref, dk_ref, dv_ref, dbias_ref, dsink_ref):
        b, n = pl.program_id(0), pl.program_id(1)

        @pl.when(n == 0)
        def _():
            dk_ref[...] = jnp.zeros_like(dk_ref)
            dv_ref[...] = jnp.zeros_like(dv_ref)

        @pl.when((n == 0) & (b == 0))
        def _():
            dbias_ref[...] = jnp.zeros_like(dbias_ref)
            dsink_ref[...] = jnp.zeros_like(dsink_ref)

        colneg = _attn_colneg(n, L)
        kcat = jnp.concatenate([k0[0], k1[0], k2[0]], axis=0)
        vcat = jnp.concatenate([v0[0], v1[0], v2[0]], axis=0)
        krows = [pl.ds(pl.multiple_of(jnp.clip(n + o, 0, nblk - 1) * BLOCK, BLOCK), BLOCK) for o in (-1, 0, 1)]
        ops, mids = [], []
        for g in range(KV_HEADS):
            pair = slice(LANE * (g // 2), LANE * (g // 2) + LANE)
            quad = slice(4 * HEAD_DIM * g, 4 * HEAD_DIM * (g + 1))
            kd = _dup_kv_head(kcat[:, pair], g % 2).astype(BF16)
            vd = _dup_kv_head(vcat[:, pair], g % 2).astype(BF16)
            qs = (_stack_heads(q_ref[0, :, quad]) * HEAD_DIM ** -0.5).astype(BF16)
            dos = _stack_heads(do_ref[0, :, quad].astype(F32)).astype(BF16)
            ops.append((kd, qs, dos, lax.dot_general(qs, kd, nt, preferred_element_type=F32),
                        lax.dot_general(dos, vd, nt, preferred_element_type=F32)))
        for g in range(KV_HEADS):
            rows = slice(4 * BLOCK * g, 4 * BLOCK * (g + 1))
            _, _, _, s, dpn = ops[g]
            pn, psink = _attn_softmax(s + bias_ref[4 * g:4 * g + 4].reshape(4 * BLOCK, KEY_SPAN) + colneg, sink_ref[rows])
            r = jnp.sum(dpn * pn, axis=-1, keepdims=True)
            ds = pn * (dpn - r)
            dbias_ref[4 * g:4 * g + 4] += ds.reshape(4, BLOCK, KEY_SPAN)
            dsink_ref[rows] += -psink * r
            mids.append((pn.astype(BF16), ds.astype(BF16)))
        for g in range(KV_HEADS):
            pair = slice(LANE * (g // 2), LANE * (g // 2) + LANE)
            quad = slice(4 * HEAD_DIM * g, 4 * HEAD_DIM * (g + 1))
            kd, qs, dos, _, _ = ops[g]
            pnb, dsb = mids[g]
            dvd = lax.dot_general(pnb, dos, tn, preferred_element_type=F32)
            dkd = lax.dot_general(dsb, qs, tn, preferred_element_type=F32)
            dqs = lax.dot_general(dsb, kd, nn, preferred_element_type=F32) * HEAD_DIM ** -0.5
            dq_ref[0, :, quad] = _unstack_heads(dqs).astype(BF16)
            dk_g, dv_g = _fold_kv_head(dkd, g % 2), _fold_kv_head(dvd, g % 2)
            for o in range(3):
                dk_ref[0, krows[o], pair] += dk_g[o * BLOCK:(o + 1) * BLOCK]
                dv_ref[0, krows[o], pair] += dv_g[o * BLOCK:(o + 1) * BLOCK]

    qspec = BS((1, BLOCK, ATTN_HEADS * HEAD_DIM), lambda b, n: (b, n, 0))
    kvout = BS((1, L, 4 * HEAD_DIM), lambda b, n: (b, 0, 0))
    return _pc(body, name=name, grid=(B, nblk), in_specs=_attn_in_specs(L) + [qspec],
               out_specs=[qspec, kvout, kvout, BS((ATTN_HEADS, BLOCK, KEY_SPAN), lambda b, n: (0, 0, 0)),
                          BS((ATTN_HEADS * BLOCK, 1), lambda b, n: (0, 0))],
               out_shape=[SDS((B, L, ATTN_HEADS * HEAD_DIM), BF16), SDS((B, L, 4 * HEAD_DIM), F32), SDS((B, L, 4 * HEAD_DIM), F32),
                          SDS((ATTN_HEADS, BLOCK, KEY_SPAN), F32), SDS((ATTN_HEADS * BLOCK, 1), F32)])(
        proj, proj, proj, proj, proj, proj, proj, bias, sinkcol, dout)


def _t5_bucket(rel):
    half = REL_BUCKETS // 2
    max_exact = half // 2
    ret = jnp.where(rel > 0, half, 0)
    n = jnp.abs(rel)
    nf = jnp.maximum(n, 1).astype(F32)
    large = max_exact + (jnp.log(nf / max_exact) / math.log(REL_MAX_DIST / max_exact) * (half - max_exact)).astype(jnp.int32)
    large = jnp.minimum(large, half - 1)
    return ret + jnp.where(n < max_exact, n, large)


def _bucket_table():
    rel = jnp.arange(KEY_SPAN)[None, :] - WINDOW - jnp.arange(BLOCK)[:, None]
    return _t5_bucket(rel).astype(jnp.int32)


def _bias_expand(rel_bias, bucket, *, name):
    rbt = jnp.zeros((ATTN_HEADS, 1, LANE), F32).at[:, 0, :REL_BUCKETS].set(rel_bias.T)

    def body(rb_ref, bk_ref, o_ref):
        lane = lax.broadcasted_iota(jnp.int32, (1, LANE), 1)
        row = rb_ref[0]
        bk = bk_ref[...]
        acc = jnp.zeros((BLOCK, KEY_SPAN), F32)
        for r in range(REL_BUCKETS):
            val = jnp.sum(jnp.where(lane == r, row, 0.0), axis=1, keepdims=True)
            acc = jnp.where(bk == r, val, acc)
        rel = (lax.broadcasted_iota(jnp.int32, (BLOCK, KEY_SPAN), 1) - WINDOW
               - lax.broadcasted_iota(jnp.int32, (BLOCK, KEY_SPAN), 0))
        o_ref[0] = jnp.where(jnp.abs(rel) <= WINDOW, acc, NEG)

    return _pc(body, name=name, grid=(ATTN_HEADS,),
               in_specs=[BS((1, 1, LANE), lambda h: (h, 0, 0)), BS((BLOCK, KEY_SPAN), lambda h: (0, 0))],
               out_specs=BS((1, BLOCK, KEY_SPAN), lambda h: (h, 0, 0)), out_shape=SDS((ATTN_HEADS, BLOCK, KEY_SPAN), F32))(rbt, bucket)


def _bias_reduce(dbias, bucket, *, name):
    def body(db_ref, bk_ref, o_ref):
        lane = lax.broadcasted_iota(jnp.int32, (1, LANE), 1)
        x = db_ref[0]
        bk = bk_ref[...]
        acc = jnp.zeros((1, LANE), F32)
        for r in range(REL_BUCKETS):
            part = jnp.sum(jnp.where(bk == r, x, 0.0), axis=1, keepdims=True)
            acc = jnp.where(lane == r, jnp.sum(part, axis=0, keepdims=True), acc)
        o_ref[0] = acc

    out = _pc(body, name=name, grid=(ATTN_HEADS,),
              in_specs=[BS((1, BLOCK, KEY_SPAN), lambda h: (h, 0, 0)), BS((BLOCK, KEY_SPAN), lambda h: (0, 0))],
              out_specs=BS((1, 1, LANE), lambda h: (h, 0, 0)), out_shape=SDS((ATTN_HEADS, 1, LANE), F32))(dbias, bucket)
    return out[:, 0, :REL_BUCKETS].T


def _loss_head(x2, w, target, *, name):
    T, D = x2.shape
    tr = _div_tile(T, 512, 8)

    def tile_loss(x, w, t):
        err = _rms(x, w) - t
        return 0.5 * jnp.sum(jnp.mean(err * err, axis=-1, keepdims=True), axis=0, keepdims=True)

    def body(x_ref, w_ref, t_ref, loss_ref, dx_ref, dw_ref):
        t = t_ref[...]
        l, vjp = jax.vjp(lambda x, w: tile_loss(x, w, t), x_ref[...], w_ref[...])
        dx, dw = vjp(jnp.ones((1, 1), F32))
        dx_ref[...] = dx

        @pl.when(pl.program_id(0) == 0)
        def _():
            dw_ref[...] = jnp.zeros_like(dw_ref)
            loss_ref[...] = jnp.zeros_like(loss_ref)

        dw_ref[...] += dw
        loss_ref[...] += l + jnp.zeros((1, LANE), F32)

    row = BS((tr, D), lambda i: (i, 0))
    one = BS((1, D), lambda i: (0, 0))
    return _pc(body, name=name, grid=(T // tr,), in_specs=[row, one, row],
               out_specs=[BS((1, LANE), lambda i: (0, 0)), row, one],
               out_shape=[SDS((1, LANE), F32), SDS((T, D), F32), SDS((1, D), F32)])(x2, w.reshape(1, D), target)


def _adamw(w2, g2, m2, v2, *, name):
    R, C = w2.shape
    tr = _div_tile(R, 256, 8)
    c1 = 1.0 - ADAM_B1 ** ADAM_STEP
    c2 = 1.0 - ADAM_B2 ** ADAM_STEP

    def body(w_ref, g_ref, m_ref, v_ref, d_ref, nm_ref, nv_ref):
        g = g_ref[...]
        m = ADAM_B1 * m_ref[...] + (1.0 - ADAM_B1) * g
        v = ADAM_B2 * v_ref[...] + (1.0 - ADAM_B2) * (g * g)
        d_ref[...] = -ADAM_LR * ((m / c1) / (jnp.sqrt(v / c2) + ADAM_EPS) + ADAM_WD * w_ref[...])
        nm_ref[...] = m
        nv_ref[...] = v

    t = BS((tr, C), lambda i: (i, 0))
    return _pc(body, name=name, grid=(R // tr,), in_specs=[t, t, t, t], out_specs=[t, t, t],
               out_shape=[SDS((R, C), F32)] * 3)(w2, g2, m2, v2)


def _place():
    return lax.axis_index("x"), lax.axis_index("y"), lax.axis_index("c")


def _gather_weights(shards, *, name):
    na = len(shards)
    half = DEPTH // 2

    def body(*refs):
        ins, outs = refs[:na], refs[na:2 * na]
        send_sems, recv_sems = refs[2 * na:]
        x, y, c = _place()
        me_chip = 2 * x + y
        chips = [(1 - x, y), (x, 1 - y), (1 - x, 1 - y)]
        mine = pl.ds(c * half, half)
        theirs = pl.ds((1 - c) * half, half)
        sends = []
        for a in range(na):
            for k, (px, py) in enumerate(chips):
                cp = pltpu.make_async_remote_copy(ins[a].at[mine], outs[a].at[me_chip, mine], send_sems.at[a, k], recv_sems.at[a, k],
                                                  device_id=(px, py, c), device_id_type=MESH)
                cp.start()
                sends.append(cp)
        for a in range(na):
            for k, (px, py) in enumerate(chips):
                got = outs[a].at[2 * px + py, mine]
                pltpu.make_async_remote_copy(got, got, send_sems.at[a, k], recv_sems.at[a, k],
                                             device_id=(px, py, c), device_id_type=MESH).wait_recv()
                cp = pltpu.make_async_remote_copy(got, got, send_sems.at[a, 3 + k], recv_sems.at[a, 3 + k],
                                                  device_id=(x, y, 1 - c), device_id_type=MESH)
                cp.start()
                sends.append(cp)
        for a in range(na):
            for k, (px, py) in enumerate(chips):
                got = outs[a].at[2 * px + py, theirs]
                pltpu.make_async_remote_copy(got, got, send_sems.at[a, 3 + k], recv_sems.at[a, 3 + k],
                                             device_id=(x, y, 1 - c), device_id_type=MESH).wait_recv()
        for cp in sends:
            cp.wait_send()

    any_spec = BS(memory_space=pl.ANY)
    return pl.pallas_call(
        body, name=name, in_specs=[any_spec] * na, out_specs=[any_spec] * na,
        out_shape=[SDS((N_CHIP,) + s.shape, s.dtype) for s in shards],
        scratch_shapes=[pltpu.SemaphoreType.DMA((na, 6)), pltpu.SemaphoreType.DMA((na, 6))],
        compiler_params=pltpu.CompilerParams(has_side_effects=True))(*shards)


def _scatter_grads(bufs, *, name):
    na = len(bufs)
    half = DEPTH // 2

    def body(*refs):
        ins, outs = refs[:na], refs[na:2 * na]
        send_sems, recv_sems = refs[2 * na:]
        x, y, c = _place()
        me = 4 * x + 2 * y + c
        sends = []
        for a in range(na):
            for r in range(1, N_DEV):
                tx, ty, tc = x ^ (r >> 2), y ^ ((r >> 1) & 1), c ^ (r & 1)
                cp = pltpu.make_async_remote_copy(ins[a].at[2 * tx + ty, pl.ds(tc * half, half)], outs[a].at[me],
                                                  send_sems.at[a, r - 1], recv_sems.at[a, r - 1],
                                                  device_id=(tx, ty, tc), device_id_type=MESH)
                cp.start()
                sends.append(cp)
        for a in range(na):
            for r in range(1, N_DEV):
                tx, ty, tc = x ^ (r >> 2), y ^ ((r >> 1) & 1), c ^ (r & 1)
                got = outs[a].at[4 * tx + 2 * ty + tc]
                pltpu.make_async_remote_copy(got, got, send_sems.at[a, r - 1], recv_sems.at[a, r - 1],
                                             device_id=(tx, ty, tc), device_id_type=MESH).wait_recv()
        for cp in sends:
            cp.wait_send()

    any_spec = BS(memory_space=pl.ANY)
    return pl.pallas_call(
        body, name=name, in_specs=[any_spec] * na, out_specs=[any_spec] * na,
        out_shape=[SDS((N_DEV, half) + b.shape[2:], b.dtype) for b in bufs],
        scratch_shapes=[pltpu.SemaphoreType.DMA((na, N_DEV - 1)), pltpu.SemaphoreType.DMA((na, N_DEV - 1))],
        compiler_params=pltpu.CompilerParams(has_side_effects=True))(*bufs)


def _sum_sources(parts, *, name):
    _, R, C = parts.shape
    tr = _div_tile(R, 256, 16)

    def body(p_ref, o_ref):
        acc = p_ref[0].astype(F32)
        for s in range(1, N_DEV):
            acc = acc + p_ref[s].astype(F32)
        o_ref[...] = acc

    return _pc(body, name=name, grid=(R // tr,), in_specs=[BS((N_DEV, tr, C), lambda i: (0, i, 0))],
               out_specs=BS((tr, C), lambda i: (i, 0)), out_shape=SDS((R, C), F32))(parts)


def _join_halves(halves, *, name):
    na = len(halves)

    def body(*refs):
        ins, outs = refs[:na], refs[na:2 * na]
        send_sems, recv_sems = refs[2 * na:]
        x, y, c = _place()
        cps = []
        for a in range(na):
            cp = pltpu.make_async_remote_copy(ins[a], outs[a], send_sems.at[a], recv_sems.at[a],
                                              device_id=(x, y, 1 - c), device_id_type=MESH)
            cp.start()
            cps.append(cp)
        for cp in cps:
            cp.wait_recv()
        for cp in cps:
            cp.wait_send()

    any_spec = BS(memory_space=pl.ANY)
    return pl.pallas_call(
        body, name=name, in_specs=[any_spec] * na, out_specs=[any_spec] * na,
        out_shape=[SDS(h.shape, h.dtype) for h in halves],
        scratch_shapes=[pltpu.SemaphoreType.DMA((na,)), pltpu.SemaphoreType.DMA((na,))],
        compiler_params=pltpu.CompilerParams(has_side_effects=True))(*halves)


def _allreduce_small(vec, *, name):
    R = vec.shape[0]

    def body(v_ref, o_ref, all_ref, send_sems, recv_sems):
        x, y, c = _place()
        me = 4 * x + 2 * y + c
        all_ref[me] = v_ref[...]
        sends = []
        for r in range(1, N_DEV):
            tgt = (x ^ (r >> 2), y ^ ((r >> 1) & 1), c ^ (r & 1))
            cp = pltpu.make_async_remote_copy(v_ref, all_ref.at[me], send_sems.at[r - 1], recv_sems.at[r - 1],
                                              device_id=tgt, device_id_type=MESH)
            cp.start()
            sends.append(cp)
        for r in range(1, N_DEV):
            tx, ty, tc = x ^ (r >> 2), y ^ ((r >> 1) & 1), c ^ (r & 1)
            got = all_ref.at[4 * tx + 2 * ty + tc]
            pltpu.make_async_remote_copy(got, got, send_sems.at[r - 1], recv_sems.at[r - 1],
                                         device_id=(tx, ty, tc), device_id_type=MESH).wait_recv()
        for cp in sends:
            cp.wait_send()
        acc = all_ref[0]
        for s in range(1, N_DEV):
            acc = acc + all_ref[s]
        o_ref[...] = acc

    vm = BS(memory_space=pltpu.VMEM)
    return pl.pallas_call(
        body, name=name, in_specs=[vm], out_specs=vm, out_shape=SDS((R, LANE), F32),
        scratch_shapes=[pltpu.VMEM((N_DEV, R, LANE), F32), pltpu.SemaphoreType.DMA((N_DEV - 1,)), pltpu.SemaphoreType.DMA((N_DEV - 1,))],
        compiler_params=pltpu.CompilerParams(has_side_effects=True, vmem_limit_bytes=VMEM_LIMIT_BYTES))(vec)


def _pack(arrs):
    rows = []
    for a in arrs:
        f = a.reshape(-1).astype(F32)
        n = -(-f.shape[0] // LANE) * LANE
        rows.append(jnp.pad(f, (0, n - f.shape[0])).reshape(-1, LANE))
    v = jnp.concatenate(rows, axis=0)
    pad = -v.shape[0] % 8
    return jnp.pad(v, ((0, pad), (0, 0)))


def _unpack(v, shapes):
    out, r = [], 0
    for s in shapes:
        n = int(np.prod(s)) if len(s) else 1
        nr = -(-n // LANE)
        out.append(v[r:r + nr].reshape(-1)[:n].reshape(s))
        r += nr
    return out


def _perm_in_cols(w_full):
    z, xbc, dt, q, k, v = (w_full[..., :Z_END], w_full[..., Z_END:XBC_END], w_full[..., XBC_END:DT_END],
                           w_full[..., DT_END:Q_END], w_full[..., Q_END:K_END], w_full[..., K_END:])
    pad = jnp.zeros(dt.shape[:-1] + (LANE - dt.shape[-1],), dt.dtype)
    return jnp.concatenate([q, z, xbc, k, v, dt, pad], axis=-1)


def _unperm_in_cols(g):
    q, z, xbc, k, v, dt = (g[..., :1024], g[..., 1024:2048], g[..., 2048:3584], g[..., 3584:3840], g[..., 3840:4096],
                           g[..., 4096:4096 + 2 * SSM_HEADS])
    return jnp.concatenate([z, xbc, dt, q, k, v], axis=-1)


def _dt_cols(a):
    return jnp.pad(a.reshape(1, 2 * SSM_HEADS), ((0, 0), (0, LANE - 2 * SSM_HEADS)))


def _layer_fwd(i, x, wts, small, band_bias):
    B, L, D = x.shape
    T = B * L
    x2 = x.reshape(T, D)
    h = _rmsnorm_fwd(x2, small["norm1_w"][i], name=f"norm1_{i}")
    proj2 = _mm(h, wts["w_in"][i], name=f"in_proj_{i}", tn=1408)
    proj = proj2.reshape(B, L, P_COLS)
    xbc_act = _conv_fwd(proj, 2048 // 256, small["conv_w"][i], small["conv_b"][i], taps=SSM_CONV, ct=256,
                        out_dtype=F32, name=f"ssm_conv_{i}")
    dtb, alog = _dt_cols(small["dt_bias"][i]), _dt_cols(small["a_log"][i])
    dskip = jnp.repeat(small["d_skip"][i], HEAD_DIM).reshape(1, SSM_WIDTH)
    ypre, hs = _ssd_fwd(xbc_act, proj, dtb, alog, dskip, name=f"ssd_{i}")
    y_ssm = _gate_fwd(ypre.reshape(T, SSM_WIDTH), proj2, small["ssm_norm_w"][i], name=f"gate_{i}")
    sinkcol = jnp.repeat(small["attn_sink"][i], BLOCK).reshape(ATTN_HEADS * BLOCK, 1)
    y_attn = _attn_fwd(proj, band_bias, sinkcol, name=f"attn_{i}").reshape(T, D)
    w_out = wts["w_out"][i]
    x_mid = _mm(y_ssm, w_out[:SSM_WIDTH], add=x2, name=f"out_proj_a_{i}")
    x_mid = _mm(y_attn, w_out[SSM_WIDTH:], add=x_mid, name=f"out_proj_b_{i}")
    h2 = _rmsnorm_fwd(x_mid, small["norm2_w"][i], name=f"norm2_{i}")
    gu2 = _mm(h2, wts["w_up"][i], name=f"up_proj_{i}", tn=1408)
    gu = gu2.reshape(B, L, 2 * D_FF)
    act = _conv_fwd(gu, 0, small["ffn_conv_w"][i], small["ffn_conv_b"][i], taps=FFN_CONV, ct=256, gate_blk0=D_FF // 256,
                    out_dtype=BF16, name=f"ffn_conv_{i}")
    x_out = _mm(act.reshape(T, D_FF), wts["w_down"][i], add=x_mid, name=f"down_proj_{i}", tk=1408)
    saved = dict(x2=x2, h=h, proj2=proj2, xbc_act=xbc_act, dtb=dtb, alog=alog, dskip=dskip, ypre=ypre, hs=hs, y_ssm=y_ssm,
                 sinkcol=sinkcol, y_attn=y_attn, x_mid=x_mid, h2=h2, gu=gu, act=act)
    return x_out.reshape(B, L, D), saved


def _layer_bwd(i, dx_out, sv, wts, small, band_bias):
    T, D = dx_out.shape
    B, L = sv["gu"].shape[:2]
    g = {}
    dxb = dx_out.astype(BF16)
    dact = _mm(dxb, wts["w_down"][i], tb=True, out_dtype=BF16, name=f"d_act_{i}", tn=1408)
    g["w_down"] = _mm(sv["act"].reshape(T, D_FF), dxb, ta=True, name=f"dw_down_{i}", tm=1408)
    dg, du, dcw, dcb = _conv_bwd(sv["gu"], 0, small["ffn_conv_w"][i], small["ffn_conv_b"][i], dact.reshape(B, L, D_FF),
                                 taps=FFN_CONV, ct=256, gate_blk0=D_FF // 256, name=f"d_ffn_conv_{i}")
    g["ffn_conv_w"] = jnp.sum(dcw, axis=0)[:FFN_CONV]
    g["ffn_conv_b"] = jnp.sum(dcb, axis=(0, 1))
    dgu = jnp.concatenate([dg, du], axis=-1).reshape(T, 2 * D_FF)
    dh2 = _mm(dgu, wts["w_up"][i], tb=True, name=f"d_h2_{i}", tk=1408)
    g["w_up"] = _mm(sv["h2"], dgu, ta=True, name=f"dw_up_{i}", tn=1408)
    dx_mid, dw2 = _rmsnorm_bwd(sv["x_mid"], small["norm2_w"][i], dh2, dx_out, name=f"d_norm2_{i}")
    g["norm2_w"] = dw2[0]
    dmb = dx_mid.astype(BF16)
    w_out = wts["w_out"][i]
    dy_ssm = _mm(dmb, w_out[:SSM_WIDTH], tb=True, name=f"d_y_ssm_{i}")
    dy_attn = _mm(dmb, w_out[SSM_WIDTH:], tb=True, out_dtype=BF16, name=f"d_y_attn_{i}")
    g["w_out"] = jnp.concatenate([_mm(sv["y_ssm"], dmb, ta=True, name=f"dw_out_a_{i}"),
                                  _mm(sv["y_attn"], dmb, ta=True, name=f"dw_out_b_{i}")], axis=0)
    dypre, dz, dwn = _gate_bwd(sv["ypre"].reshape(T, SSM_WIDTH), sv["proj2"], small["ssm_norm_w"][i], dy_ssm, name=f"d_gate_{i}")
    g["ssm_norm_w"] = jnp.sum(dwn, axis=(0, 1))
    proj = sv["proj2"].reshape(B, L, P_COLS)
    dxs, dbm, dcm, ddt, ddtb, dalog, ddsk = _ssd_bwd(sv["xbc_act"], proj, sv["dtb"], sv["alog"], sv["dskip"], sv["hs"],
                                                    dypre.reshape(B, L, SSM_WIDTH), name=f"d_ssd_{i}")
    g["dt_bias"] = jnp.sum(ddtb, axis=(0, 1, 2))[:2 * SSM_HEADS].reshape(2, SSM_HEADS)
    g["a_log"] = jnp.sum(dalog, axis=(0, 1, 2))[:2 * SSM_HEADS].reshape(2, SSM_HEADS)
    g["d_skip"] = jnp.sum(ddsk.reshape(B, SSM_HEADS, HEAD_DIM), axis=(0, 2))
    dxbc_act = jnp.concatenate([dxs, dbm, dcm], axis=-1)
    dxbc, dcw, dcb = _conv_bwd(proj, 2048 // 256, small["conv_w"][i], small["conv_b"][i], dxbc_act, taps=SSM_CONV, ct=256,
                               name=f"d_ssm_conv_{i}")
    g["conv_w"] = jnp.sum(dcw, axis=0)[:SSM_CONV]
    g["conv_b"] = jnp.sum(dcb, axis=(0, 1))
    dq, dk, dv, dbias, dsink = _attn_bwd(proj, band_bias, sv["sinkcol"], dy_attn.reshape(B, L, D), name=f"d_attn_{i}")
    g["attn_sink"] = jnp.sum(dsink.reshape(ATTN_HEADS, BLOCK), axis=1)
    dproj = jnp.concatenate([dq, dz.reshape(B, L, SSM_WIDTH), dxbc, dk.astype(BF16), dv.astype(BF16), ddt.astype(BF16)],
                            axis=-1).reshape(T, P_COLS)
    dh = _mm(dproj, wts["w_in"][i], tb=True, name=f"d_h_{i}", tk=1408)
    g["w_in"] = _unperm_in_cols(_mm(sv["h"], dproj, ta=True, name=f"dw_in_{i}", tn=1408))
    dx_in, dw1 = _rmsnorm_bwd(sv["x2"], small["norm1_w"][i], dh, dx_mid, name=f"d_norm1_{i}")
    g["norm1_w"] = dw1[0]
    return dx_in, g, dbias


_BIG = ("w_in", "w_out", "w_up", "w_down")
_BIG_AXIS = {"w_in": 2, "w_out": 1, "w_up": 2, "w_down": 1}
_SMALL = ("rel_bias", "norm1_w", "conv_w", "conv_b", "dt_bias", "a_log", "d_skip", "ssm_norm_w", "attn_sink", "norm2_w",
          "ffn_conv_w", "ffn_conv_b", "final_norm_w")
_SMALL_SHARDED = ("conv_w", "ffn_conv_w")
_ORDER = ("rel_bias", "norm1_w", "w_in", "conv_w", "conv_b", "dt_bias", "a_log", "d_skip", "ssm_norm_w", "attn_sink", "w_out",
          "norm2_w", "w_up", "ffn_conv_w", "ffn_conv_b", "w_down", "final_norm_w")


def _local_step(x, target, wts, small):
    B, L, D = x.shape
    bucket = _bucket_table()
    band_bias = _bias_expand(small["rel_bias"], bucket, name="band_bias")
    saved = []
    for i in range(DEPTH):
        x, sv = _layer_fwd(i, x, wts, small, band_bias)
        saved.append(sv)
    loss, dx, dwf = _loss_head(x.reshape(B * L, D), small["final_norm_w"], target.reshape(B * L, D), name="loss_head")
    per_layer = []
    dbias = jnp.zeros((ATTN_HEADS, BLOCK, KEY_SPAN), F32)
    for i in reversed(range(DEPTH)):
        dx, g, dbias_i = _layer_bwd(i, dx, saved[i], wts, small, band_bias)
        dbias = dbias + dbias_i
        per_layer.append(g)
    per_layer.reverse()
    grads = {k: jnp.stack([g[k] for g in per_layer]) for k in per_layer[0]}
    grads["rel_bias"] = _bias_reduce(dbias, bucket, name="d_rel_bias")
    grads["final_norm_w"] = dwf[0]
    return loss, dx.reshape(B, L, D), grads


def _split_by_chip(g, axis):
    shp = g.shape
    n = shp[axis] // N_CHIP
    g = g.reshape(shp[:axis] + (N_CHIP, n) + shp[axis + 1:])
    return jnp.moveaxis(g, axis, 0)


def _join_chips(a, axis):
    a = jnp.moveaxis(a, 0, axis)
    shp = a.shape
    return a.reshape(shp[:axis] + (shp[axis] * shp[axis + 1],) + shp[axis + 2:])


def kernel(x, rel_bias, norm1_w, w_in, conv_w, conv_b, dt_bias, a_log, d_skip, ssm_norm_w, attn_sink, w_out, norm2_w, w_up, ffn_conv_w, ffn_conv_b, w_down, final_norm_w, loss_target, m_rel_bias, m_norm1_w, m_w_in, m_conv_w, m_conv_b, m_dt_bias, m_a_log, m_d_skip, m_ssm_norm_w, m_attn_sink, m_w_out, m_norm2_w, m_w_up, m_ffn_conv_w, m_ffn_conv_b, m_w_down, m_final_norm_w, v_rel_bias, v_norm1_w, v_w_in, v_conv_w, v_conv_b, v_dt_bias, v_a_log, v_d_skip, v_ssm_norm_w, v_attn_sink, v_w_out, v_norm2_w, v_w_up, v_ffn_conv_w, v_ffn_conv_b, v_w_down, v_final_norm_w):
    w = dict(rel_bias=rel_bias, norm1_w=norm1_w, w_in=w_in, conv_w=conv_w, conv_b=conv_b, dt_bias=dt_bias, a_log=a_log,
             d_skip=d_skip, ssm_norm_w=ssm_norm_w, attn_sink=attn_sink, w_out=w_out, norm2_w=norm2_w, w_up=w_up,
             ffn_conv_w=ffn_conv_w, ffn_conv_b=ffn_conv_b, w_down=w_down, final_norm_w=final_norm_w)
    m = dict(rel_bias=m_rel_bias, norm1_w=m_norm1_w, w_in=m_w_in, conv_w=m_conv_w, conv_b=m_conv_b, dt_bias=m_dt_bias,
             a_log=m_a_log, d_skip=m_d_skip, ssm_norm_w=m_ssm_norm_w, attn_sink=m_attn_sink, w_out=m_w_out, norm2_w=m_norm2_w,
             w_up=m_w_up, ffn_conv_w=m_ffn_conv_w, ffn_conv_b=m_ffn_conv_b, w_down=m_w_down, final_norm_w=m_final_norm_w)
    v = dict(rel_bias=v_rel_bias, norm1_w=v_norm1_w, w_in=v_w_in, conv_w=v_conv_w, conv_b=v_conv_b, dt_bias=v_dt_bias,
             a_log=v_a_log, d_skip=v_d_skip, ssm_norm_w=v_ssm_norm_w, attn_sink=v_attn_sink, w_out=v_w_out, norm2_w=v_norm2_w,
             w_up=v_w_up, ffn_conv_w=v_ffn_conv_w, ffn_conv_b=v_ffn_conv_b, w_down=v_w_down, final_norm_w=v_final_norm_w)
    my_chip = 2 * lax.axis_index("x") + lax.axis_index("y")

    shards = [w[k].astype(BF16) for k in _BIG]
    gathered = _gather_weights(shards, name="gather_weights")
    gathered = [lax.dynamic_update_index_in_dim(g_, s_, my_chip, 0) for g_, s_ in zip(gathered, shards)]
    wts = {k: _join_chips(a, _BIG_AXIS[k]) for k, a in zip(_BIG, gathered)}
    wts["w_in"] = _perm_in_cols(wts["w_in"])
    conv_shapes = [(DEPTH, SSM_CONV, CONV_CH), (DEPTH, FFN_CONV, D_FF)]
    placed = [lax.dynamic_update_slice_in_dim(jnp.zeros(s, F32), w[k], my_chip * w[k].shape[2], axis=2)
              for k, s in zip(_SMALL_SHARDED, conv_shapes)]
    lead = (lax.axis_index("c") == 0).astype(F32)
    conv_full = _unpack(_allreduce_small(_pack([p * lead for p in placed]), name="gather_conv_weights"), conv_shapes)
    small = {k: w[k] for k in _SMALL}
    small["conv_w"], small["ffn_conv_w"] = conv_full

    loss_part, grad_x, gp = _local_step(x, loss_target, wts, small)

    small_shapes = [small[k].shape for k in _SMALL] + [()]
    red = _unpack(_allreduce_small(_pack([gp[k] for k in _SMALL] + [loss_part[0, :1]]), name="reduce_small"), small_shapes)
    gsmall = dict(zip(_SMALL, red[:-1]))
    loss = red[-1]
    for k in _SMALL_SHARDED:
        n = w[k].shape[2]
        gsmall[k] = lax.dynamic_slice_in_dim(gsmall[k], my_chip * n, n, axis=2)

    core = lax.axis_index("c")
    me = 2 * my_chip + core
    half = DEPTH // 2
    bufs = [_split_by_chip(gp[k], _BIG_AXIS[k]).astype(BF16) for k in _BIG]
    parts = _scatter_grads(bufs, name="scatter_grads")
    halves = []
    for k, b_, p in zip(_BIG, bufs, parts):
        own = lax.dynamic_slice_in_dim(lax.dynamic_index_in_dim(b_, my_chip, 0, keepdims=False), core * half, half, axis=0)
        p = lax.dynamic_update_index_in_dim(p, own, me, 0)
        shp = p.shape
        s = _sum_sources(p.reshape(N_DEV, shp[1] * shp[2], shp[3]), name=f"sum_{k}")
        halves.append(s.reshape(shp[1:]))
    others = _join_halves(halves, name="join_halves")
    gbig = {}
    for k, mine_, theirs_ in zip(_BIG, halves, others):
        full = jnp.zeros((DEPTH,) + mine_.shape[1:], F32)
        full = lax.dynamic_update_slice_in_dim(full, mine_, core * half, axis=0)
        gbig[k] = lax.dynamic_update_slice_in_dim(full, theirs_, (1 - core) * half, axis=0)

    grad, delta, new_m, new_v = {}, {}, {}, {}
    for k in _BIG:
        shp = w[k].shape
        two = lambda a: a.reshape(shp[0] * shp[1], shp[2])
        d_, m_, v_ = _adamw(two(w[k]), two(gbig[k]), two(m[k]), two(v[k]), name=f"adamw_{k}")
        grad[k], delta[k], new_m[k], new_v[k] = gbig[k], d_.reshape(shp), m_.reshape(shp), v_.reshape(shp)
    shapes = [w[k].shape for k in _SMALL]
    d_, m_, v_ = _adamw(_pack([w[k] for k in _SMALL]), _pack([gsmall[k] for k in _SMALL]), _pack([m[k] for k in _SMALL]),
                        _pack([v[k] for k in _SMALL]), name="adamw_small")
    for k, a, b_, c_ in zip(_SMALL, _unpack(d_, shapes), _unpack(m_, shapes), _unpack(v_, shapes)):
        grad[k], delta[k], new_m[k], new_v[k] = gsmall[k], a, b_, c_
    return (loss, grad_x, *[grad[k] for k in _ORDER], *[delta[k] for k in _ORDER], *[new_m[k] for k in _ORDER],
            *[new_v[k] for k in _ORDER])
```

```python
import functools
import math

import jax
import jax.numpy as jnp
import numpy as np
from jax import lax
from jax.experimental import pallas as pl
from jax.experimental.pallas import tpu as pltpu

F32 = jnp.float32
BF16 = jnp.bfloat16
BS = pl.BlockSpec
SDS = jax.ShapeDtypeStruct
MESH = pl.DeviceIdType.MESH

D_MODEL = 1024
DEPTH = 4
SSM_HEADS = 16
SSM_WIDTH = 1024
BC_WIDTH = 256
CONV_CH = 1536
SSM_CONV = 7
CHUNK = 128
ATTN_HEADS = 16
KV_HEADS = 4
HEAD_DIM = 64
WINDOW = 128
BLOCK = 128
KEY_SPAN = 384
REL_BUCKETS = 32
REL_MAX_DIST = 128
D_FF = 2816
FFN_CONV = 3
NORM_EPS = 1e-6
Z_END = 1024
XBC_END = 2560
DT_END = 2592
Q_END = 3616
K_END = 3872
IN_COLS = 4128
P_COLS = 4224
ADAM_LR, ADAM_B1, ADAM_B2, ADAM_EPS, ADAM_WD, ADAM_STEP = 0.001, 0.9, 0.999, 1e-08, 0.01, 10
NEG = -1e30
N_DEV = 8
N_CHIP = 4
LANE = 128
VMEM_LIMIT_BYTES = 48 * 1024 * 1024


def _pc(body, *, name, grid, in_specs, out_specs, out_shape, scratch_shapes=()):
    return pl.pallas_call(
        body, name=name, grid=grid, in_specs=in_specs, out_specs=out_specs, out_shape=out_shape,
        scratch_shapes=list(scratch_shapes),
        compiler_params=pltpu.CompilerParams(dimension_semantics=("arbitrary",) * len(grid),
                                             vmem_limit_bytes=VMEM_LIMIT_BYTES))


def _split_rider_refs(refs, n_in, n_out, rider):
    n_rin = len(rider["ins"]) + len(rider["prev"])
    n_rout = len(rider["out_shape"])
    core = refs[:n_in] + refs[n_in + n_rin:n_in + n_rin + n_out] + refs[n_in + n_rin + n_out + n_rout + len(rider["scratch"]):]
    rins = refs[n_in:n_in + len(rider["ins"])]
    routs = refs[n_in + n_rin + n_out:n_in + n_rin + n_out + n_rout]
    sems = refs[n_in + n_rin + n_out + n_rout:n_in + n_rin + n_out + n_rout + len(rider["scratch"])]
    return core, rins, routs, sems


def _pc_carry(body, args, *, name, grid, in_specs, out_specs, out_shape, scratch_shapes=(), rider=None):
    if rider is None:
        return _pc(body, name=name, grid=grid, in_specs=in_specs, out_specs=out_specs, out_shape=out_shape,
                   scratch_shapes=scratch_shapes)(*args), None
    n_in, n_out = len(in_specs), len(out_shape)
    any_spec = BS(memory_space=pl.ANY)

    def full(*refs):
        core, rins, routs, sems = _split_rider_refs(refs, n_in, n_out, rider)
        ids = [pl.program_id(d) for d in range(len(grid))]
        first = functools.reduce(jnp.logical_and, [i == 0 for i in ids])
        last = functools.reduce(jnp.logical_and, [i == g - 1 for i, g in zip(ids, grid)])

        @pl.when(first)
        def _():
            rider["start"](rins, routs, sems)

        body(*core)

        @pl.when(last)
        def _():
            rider["finish"](rins, routs, sems)

    n_rin = len(rider["ins"])
    outs = pl.pallas_call(
        full, name=name, grid=grid,
        in_specs=list(in_specs) + [any_spec] * (n_rin + len(rider["prev"])),
        out_specs=list(out_specs) + [any_spec] * len(rider["out_shape"]),
        out_shape=list(out_shape) + list(rider["out_shape"]),
        scratch_shapes=list(rider["scratch"]) + list(scratch_shapes),
        input_output_aliases={n_in + n_rin + t: n_out + t for t in range(len(rider["prev"]))},
        compiler_params=pltpu.CompilerParams(dimension_semantics=("arbitrary",) * len(grid), vmem_limit_bytes=VMEM_LIMIT_BYTES,
                                             has_side_effects=True))(*args, *rider["ins"], *rider["prev"])
    return outs[:n_out], outs[n_out:]


def _run_rider(rider, *, name):
    any_spec = BS(memory_space=pl.ANY)
    n_rin = len(rider["ins"])

    def body(*refs):
        _, rins, routs, sems = _split_rider_refs(refs, 0, 0, rider)
        rider["start"](rins, routs, sems)
        rider["finish"](rins, routs, sems)

    return pl.pallas_call(
        body, name=name, in_specs=[any_spec] * (n_rin + len(rider["prev"])), out_specs=[any_spec] * len(rider["out_shape"]),
        out_shape=list(rider["out_shape"]), scratch_shapes=list(rider["scratch"]),
        input_output_aliases={n_rin + t: t for t in range(len(rider["prev"]))},
        compiler_params=pltpu.CompilerParams(has_side_effects=True))(*rider["ins"], *rider["prev"])


def _div_tile(n, pref, mult):
    t = min(pref, n)
    t -= t % mult
    while t >= mult:
        if n % t == 0:
            return t
        t -= mult
    return n


def _mm(a, b, *, name, ta=False, tb=False, add=None, out_dtype=F32, tm=1024, tn=1024, tk=1024):
    if ta:
        K, M = a.shape
    else:
        M, K = a.shape
    N = b.shape[0] if tb else b.shape[1]
    tm, tn, tk = _div_tile(M, tm, LANE), _div_tile(N, tn, LANE), _div_tile(K, tk, LANE)
    nk = K // tk
    dims = (((0,) if ta else (1,), (1,) if tb else (0,)), ((), ()))

    def body_single(*refs):
        r = lax.dot_general(refs[0][...], refs[1][...], dims, preferred_element_type=F32)
        if add is not None:
            r = r + refs[2][...]
        refs[-1][...] = r.astype(out_dtype)

    def body(*refs):
        if add is None:
            a_ref, b_ref, o_ref, acc_ref = refs
        else:
            a_ref, b_ref, add_ref, o_ref, acc_ref = refs
        k = pl.program_id(2)

        @pl.when(k == 0)
        def _():
            acc_ref[...] = jnp.zeros_like(acc_ref)

        acc_ref[...] += lax.dot_general(a_ref[...], b_ref[...], dims, preferred_element_type=F32)

        @pl.when(k == nk - 1)
        def _():
            r = acc_ref[...]
            if add is not None:
                r = r + add_ref[...]
            o_ref[...] = r.astype(out_dtype)

    a_spec = BS((tk, tm), lambda i, j, k: (k, i)) if ta else BS((tm, tk), lambda i, j, k: (i, k))
    b_spec = BS((tn, tk), lambda i, j, k: (j, k)) if tb else BS((tk, tn), lambda i, j, k: (k, j))
    in_specs, args = [a_spec, b_spec], [a, b]
    if add is not None:
        in_specs.append(BS((tm, tn), lambda i, j, k: (i, j)))
        args.append(add)
    return _pc(body_single if nk == 1 else body, name=name, grid=(M // tm, N // tn, nk), in_specs=in_specs,
               out_specs=BS((tm, tn), lambda i, j, k: (i, j)), out_shape=SDS((M, N), out_dtype),
               scratch_shapes=[] if nk == 1 else [pltpu.VMEM((tm, tn), F32)])(*args)


def _dot(a, b, dims):
    return lax.dot_general(a.astype(BF16), b.astype(BF16), (dims, ((), ())), preferred_element_type=F32)


@jax.custom_vjp
def _nn(a, b):
    return _dot(a, b, ((1,), (0,)))


@jax.custom_vjp
def _nt(a, b):
    return _dot(a, b, ((1,), (1,)))


@jax.custom_vjp
def _tn(a, b):
    return _dot(a, b, ((0,), (0,)))


_nn.defvjp(lambda a, b: (_nn(a, b), (a, b)), lambda r, g: (_nt(g, r[1]), _tn(r[0], g)))
_nt.defvjp(lambda a, b: (_nt(a, b), (a, b)), lambda r, g: (_nn(g, r[1]), _tn(g, r[0])))
_tn.defvjp(lambda a, b: (_tn(a, b), (a, b)), lambda r, g: (_nt(r[1], g), _nn(r[0], g)))


def _hdot(m, x):
    hi = x.astype(BF16)
    r1 = x - hi.astype(F32)
    lo = r1.astype(BF16)
    lo2 = (r1 - lo.astype(F32)).astype(BF16)
    n = x.shape[1]
    out = lax.dot_general(m.astype(BF16), jnp.concatenate([hi, lo, lo2], axis=1), (((1,), (0,)), ((), ())),
                          preferred_element_type=F32)
    return out[:, :n] + out[:, n:2 * n] + out[:, 2 * n:]


@jax.custom_vjp
def _cumdot(m, mt, x):
    return _hdot(m, x)


_cumdot.defvjp(lambda m, mt, x: (_hdot(m, x), (m, mt)),
               lambda r, g: (jnp.zeros_like(r[0]), jnp.zeros_like(r[1]), _hdot(r[1], g)))


def _sigmoid(x):
    return 1.0 / (1.0 + jnp.exp(-x))


def _softplus(x):
    return jnp.maximum(x, 0.0) + jnp.log(1.0 + jnp.exp(-jnp.abs(x)))


def _rms(x, w):
    return x * lax.rsqrt(jnp.mean(x * x, axis=-1, keepdims=True) + NORM_EPS) * w


def _rmsnorm_fwd(x2, w, *, name):
    T, D = x2.shape
    tr = _div_tile(T, 512, 8)

    def body(x_ref, w_ref, o_ref):
        o_ref[...] = _rms(x_ref[...], w_ref[...]).astype(BF16)

    return _pc(body, name=name, grid=(T // tr,),
               in_specs=[BS((tr, D), lambda i: (i, 0)), BS((1, D), lambda i: (0, 0))],
               out_specs=BS((tr, D), lambda i: (i, 0)), out_shape=SDS((T, D), BF16))(x2, w.reshape(1, D))


def _rmsnorm_bwd(x2, w, dh, resid, *, name):
    T, D = x2.shape
    tr = _div_tile(T, 512, 8)

    def body(x_ref, w_ref, dh_ref, r_ref, dx_ref, dw_ref):
        _, vjp = jax.vjp(_rms, x_ref[...], w_ref[...])
        dx, dw = vjp(dh_ref[...])
        dx_ref[...] = dx + r_ref[...]

        @pl.when(pl.program_id(0) == 0)
        def _():
            dw_ref[...] = jnp.zeros_like(dw_ref)

        dw_ref[...] += dw

    row = BS((tr, D), lambda i: (i, 0))
    one = BS((1, D), lambda i: (0, 0))
    return _pc(body, name=name, grid=(T // tr,), in_specs=[row, one, row, row], out_specs=[row, one],
               out_shape=[SDS((T, D), F32), SDS((1, D), F32)])(x2, w.reshape(1, D), dh, resid)


ROW_PAD = 8


def _pad_rows(x):
    return jnp.concatenate([x, jnp.zeros((ROW_PAD, x.shape[1]), x.dtype)], axis=0)


def _shift_rows(xp, s):
    n = xp.shape[0] - ROW_PAD
    return xp[:n] if s == 0 else pltpu.roll(xp, (-s) % xp.shape[0], 0)[:n]


def _conv_taps(x, taps):
    xp = _pad_rows(x)
    return [_shift_rows(xp, k - taps // 2) for k in range(taps)]


def _conv_pre(xs, w_ref, b_ref):
    c = b_ref[...] + w_ref[0:1, :] * xs[0]
    for k in range(1, len(xs)):
        c = c + w_ref[k:k + 1, :] * xs[k]
    return c


def _conv_fwd(x3, x_blk0, w, b, *, taps, ct, gate_blk0=None, out_dtype, name):
    B, L, _ = x3.shape
    C = w.shape[1]
    wp = jnp.zeros((8, C), F32).at[:taps].set(w)

    def body(*refs):
        if gate_blk0 is None:
            x_ref, w_ref, b_ref, o_ref = refs
        else:
            x_ref, u_ref, w_ref, b_ref, o_ref = refs
        c = _conv_pre(_conv_taps(x_ref[0], taps), w_ref, b_ref)
        y = c * _sigmoid(c)
        if gate_blk0 is not None:
            y = y * u_ref[0]
        o_ref[0] = y.astype(out_dtype)

    in_specs = [BS((1, L, ct), lambda bi, j: (bi, 0, x_blk0 + j))]
    args = [x3]
    if gate_blk0 is not None:
        in_specs.append(BS((1, L, ct), lambda bi, j: (bi, 0, gate_blk0 + j)))
        args.append(x3)
    in_specs += [BS((8, ct), lambda bi, j: (0, j)), BS((1, ct), lambda bi, j: (0, j))]
    args += [wp, b.reshape(1, C)]
    return _pc(body, name=name, grid=(B, C // ct), in_specs=in_specs,
               out_specs=BS((1, L, ct), lambda bi, j: (bi, 0, j)), out_shape=SDS((B, L, C), out_dtype))(*args)


def _conv_bwd(x3, x_blk0, w, b, dy3, *, taps, ct, gate_blk0=None, name):
    B, L, _ = x3.shape
    C = w.shape[1]
    wp = jnp.zeros((8, C), F32).at[:taps].set(w)
    gated = gate_blk0 is not None

    def body(*refs):
        if gated:
            x_ref, u_ref, w_ref, b_ref, dy_ref, dx_ref, du_ref, dw_ref, db_ref = refs
        else:
            x_ref, w_ref, b_ref, dy_ref, dx_ref, dw_ref, db_ref = refs
        xs = _conv_taps(x_ref[0], taps)
        dy = dy_ref[0].astype(F32)
        c = _conv_pre(xs, w_ref, b_ref)
        sg = _sigmoid(c)
        dsilu = sg * (1.0 + c * (1.0 - sg))
        if gated:
            du_ref[0] = (dy * (c * sg)).astype(BF16)
            dc = dy * u_ref[0] * dsilu
        else:
            dc = dy * dsilu
        dcp = _pad_rows(dc)
        dx = jnp.zeros_like(dc)
        dw_ref[0] = jnp.zeros((8, ct), F32)
        for k in range(taps):
            dx = dx + w_ref[k:k + 1, :] * _shift_rows(dcp, taps // 2 - k)
            dw_ref[0, k:k + 1, :] = jnp.sum(dc * xs[k], axis=0, keepdims=True)
        dx_ref[0] = dx.astype(BF16)
        db_ref[0] = jnp.sum(dc, axis=0, keepdims=True)

    xs = BS((1, L, ct), lambda bi, j: (bi, 0, x_blk0 + j))
    ys = BS((1, L, ct), lambda bi, j: (bi, 0, j))
    in_specs, args = [xs], [x3]
    if gated:
        in_specs.append(BS((1, L, ct), lambda bi, j: (bi, 0, gate_blk0 + j)))
        args.append(x3)
    in_specs += [BS((8, ct), lambda bi, j: (0, j)), BS((1, ct), lambda bi, j: (0, j)), ys]
    args += [wp, b.reshape(1, C), dy3]
    out_specs = [ys] + ([ys] if gated else []) + [BS((1, 8, ct), lambda bi, j: (bi, 0, j)), BS((1, 1, ct), lambda bi, j: (bi, 0, j))]
    out_shape = [SDS((B, L, C), BF16)] + ([SDS((B, L, C), BF16)] if gated else []) + [SDS((B, 8, C), F32), SDS((B, 1, C), F32)]
    return _pc(body, name=name, grid=(B, C // ct), in_specs=in_specs, out_specs=out_specs, out_shape=out_shape)(*args)


def _tri(reverse):
    r = lax.broadcasted_iota(jnp.int32, (CHUNK, CHUNK), 0)
    c = lax.broadcasted_iota(jnp.int32, (CHUNK, CHUNK), 1)
    return (c >= r) if reverse else (c <= r)


PAIRS = 2
QUADS = SSM_HEADS // (2 * PAIRS)
QW = PAIRS * LANE


def _ssd_chunk(h0, h1, x0, x1, bm, cm, dtraw, dtb, alog, *, col0, reverse):
    mask = _tri(reverse)
    eye = lax.broadcasted_iota(jnp.int32, (CHUNK, CHUNK), 0) == lax.broadcasted_iota(jnp.int32, (CHUNK, CHUNK), 1)
    lane = lax.broadcasted_iota(jnp.int32, (1, LANE), 1)
    first = lane < HEAD_DIM
    dtc = _softplus(dtraw + dtb)
    adt = dtc * (-jnp.exp(alog))
    cumc = _cumdot(mask.astype(F32), _tri(not reverse).astype(F32), adt)
    totc = jnp.sum(adt, axis=0, keepdims=True)
    cb = _nt(cm, bm)

    def col(v, c):
        return jnp.sum(jnp.where(lane == c, v, 0.0), axis=1, keepdims=True)

    outs, states = [], []
    for p, (hprev, xs) in enumerate(((h0, x0), (h1, x1))):
        c0 = col0 + 2 * p
        cj = (col(cumc, c0), col(cumc, c0 + 1))
        cum = jnp.where(first, cj[0], cj[1])
        tot = jnp.where(first, col(totc, c0), col(totc, c0 + 1))
        xdt = xs * jnp.where(first, col(dtc, c0), col(dtc, c0 + 1))
        y = _nn(cm, hprev) * jnp.exp(cum)
        for j in range(2):
            rj = jnp.sum(jnp.where(eye, cj[j], 0.0), axis=0, keepdims=True)
            dec = jnp.exp(jnp.where(mask, cj[j] - rj, NEG))
            y = y + _nn(cb * dec, jnp.where(first if j == 0 else ~first, xdt, 0.0))
        outs.append(y)
        states.append(hprev * jnp.exp(tot) + _tn(bm, xdt * jnp.exp(tot - cum)))
    return outs[0], outs[1], states[0], states[1]


def _ssd_specs(B, L):
    def lanes(w, blk):
        return BS((1, L, w), blk)

    return [
        lanes(QW, lambda b, q: (b, 0, q)),
        lanes(LANE, lambda b, q: (b, 0, 8 + q // 2)),
        lanes(LANE, lambda b, q: (b, 0, 10 + q // 2)),
        lanes(LANE, lambda b, q: (b, 0, P_COLS // LANE - 1)),
        BS((1, LANE), lambda b, q: (0, 0)),
        BS((1, LANE), lambda b, q: (0, 0)),
        BS((1, QW), lambda b, q: (0, q)),
    ]


def _ssd_slot(d, ci):
    return ci if d == 0 else ci + 1


def _ssd_fwd(xbc_act, proj, dtb, alog, dskip, *, name):
    B, L, _ = xbc_act.shape
    nc = L // CHUNK

    def body(xs_ref, b_ref, c_ref, dt_ref, dtb_ref, alog_ref, dsk_ref, y_ref, hs_ref):
        q = pl.program_id(1)
        dtb_v, alog_v = dtb_ref[...], alog_ref[...]
        y_ref[0] = dsk_ref[...] * xs_ref[0]
        hs_ref[0, 0, 0, 0] = jnp.zeros((LANE, QW), F32)
        hs_ref[0, 0, 1, nc] = jnp.zeros((LANE, QW), F32)

        def step(i, carry):
            cis = (i, nc - 1 - i)
            rows = [pl.ds(pl.multiple_of(ci * CHUNK, CHUNK), CHUNK) for ci in cis]
            res = []
            for d in range(2):
                cur = _ssd_slot(d, cis[d])
                res.append(_ssd_chunk(
                    hs_ref[0, 0, d, cur, :, :LANE], hs_ref[0, 0, d, cur, :, LANE:], xs_ref[0, rows[d], :LANE],
                    xs_ref[0, rows[d], LANE:], b_ref[0, rows[d], :], c_ref[0, rows[d], :], dt_ref[0, rows[d], :], dtb_v, alog_v,
                    col0=SSM_HEADS * d + 2 * PAIRS * q, reverse=d == 1))
            for d in range(2):
                y0, y1, n0, n1 = res[d]
                nxt = _ssd_slot(d, cis[d] + 1 if d == 0 else cis[d] - 1)
                hs_ref[0, 0, d, nxt, :, :LANE] = n0
                hs_ref[0, 0, d, nxt, :, LANE:] = n1
                y_ref[0, rows[d], :LANE] += y0
                y_ref[0, rows[d], LANE:] += y1
            return carry

        lax.fori_loop(0, nc, step, 0, unroll=2)

    return _pc(body, name=name, grid=(B, QUADS), in_specs=_ssd_specs(B, L),
               out_specs=[BS((1, L, QW), lambda b, q: (b, 0, q)),
                          BS((1, 1, 2, nc + 1, LANE, QW), lambda b, q: (b, q, 0, 0, 0, 0))],
               out_shape=[SDS((B, L, SSM_WIDTH), F32), SDS((B, QUADS, 2, nc + 1, LANE, QW), F32)])(
        xbc_act, xbc_act, xbc_act, proj, dtb, alog, dskip)


def _ssd_bwd(xbc_act, proj, dtb, alog, dskip, hs, dy, *, name):
    B, L, _ = xbc_act.shape
    nc = L // CHUNK

    def body(xs_ref, b_ref, c_ref, dt_ref, dtb_ref, alog_ref, dsk_ref, hs_ref, dy_ref,
             dxs_ref, db_ref, dc_ref, ddt_ref, ddtb_ref, dalog_ref, ddsk_ref, dh_ref):
        q = pl.program_id(1)
        dtb_v, alog_v = dtb_ref[...], alog_ref[...]

        @pl.when(q % 2 == 0)
        def _():
            db_ref[...] = jnp.zeros_like(db_ref)
            dc_ref[...] = jnp.zeros_like(dc_ref)

        @pl.when(q == 0)
        def _():
            ddt_ref[...] = jnp.zeros_like(ddt_ref)

        dxs_ref[0] = dy_ref[0] * dsk_ref[...]
        ddsk_ref[0] = jnp.sum(dy_ref[0] * xs_ref[0], axis=0, keepdims=True)
        dh_ref[...] = jnp.zeros_like(dh_ref)

        def step(i, carry):
            g_dtb, g_alog = carry
            cis = (nc - 1 - i, i)
            rows = [pl.ds(pl.multiple_of(ci * CHUNK, CHUNK), CHUNK) for ci in cis]
            res = []
            for d in range(2):
                cur = _ssd_slot(d, cis[d])
                fn = functools.partial(_ssd_chunk, col0=SSM_HEADS * d + 2 * PAIRS * q, reverse=d == 1)
                _, vjp = jax.vjp(fn, hs_ref[0, 0, d, cur, :, :LANE], hs_ref[0, 0, d, cur, :, LANE:], xs_ref[0, rows[d], :LANE],
                                 xs_ref[0, rows[d], LANE:], b_ref[0, rows[d], :], c_ref[0, rows[d], :], dt_ref[0, rows[d], :],
                                 dtb_v, alog_v)
                res.append(vjp((dy_ref[0, rows[d], :LANE], dy_ref[0, rows[d], LANE:], dh_ref[d, :, :LANE], dh_ref[d, :, LANE:])))
            for d in range(2):
                g_h0, g_h1, g_x0, g_x1, g_b, g_c, g_dt, g_dtb1, g_alog1 = res[d]
                dh_ref[d, :, :LANE] = g_h0
                dh_ref[d, :, LANE:] = g_h1
                dxs_ref[0, rows[d], :LANE] += g_x0
                dxs_ref[0, rows[d], LANE:] += g_x1
                db_ref[0, rows[d], :] += g_b
                dc_ref[0, rows[d], :] += g_c
                ddt_ref[0, rows[d], :] += g_dt
                g_dtb, g_alog = g_dtb + g_dtb1, g_alog + g_alog1
            return g_dtb, g_alog

        zero_row = jnp.zeros((1, LANE), F32)
        a0, a1 = lax.fori_loop(0, nc, step, (zero_row, zero_row))
        ddtb_ref[0, 0] = a0
        dalog_ref[0, 0] = a1

    lanes = lambda w, blk: BS((1, L, w), blk)
    in_specs = _ssd_specs(B, L) + [BS((1, 1, 2, nc + 1, LANE, QW), lambda b, q: (b, q, 0, 0, 0, 0)), lanes(QW, lambda b, q: (b, 0, q))]
    out_specs = [lanes(QW, lambda b, q: (b, 0, q)), lanes(LANE, lambda b, q: (b, 0, q // 2)), lanes(LANE, lambda b, q: (b, 0, q // 2)),
                 lanes(LANE, lambda b, q: (b, 0, 0)), BS((1, 1, 1, LANE), lambda b, q: (b, q, 0, 0)),
                 BS((1, 1, 1, LANE), lambda b, q: (b, q, 0, 0)), BS((1, 1, QW), lambda b, q: (b, 0, q))]
    out_shape = [SDS((B, L, SSM_WIDTH), F32), SDS((B, L, BC_WIDTH), F32), SDS((B, L, BC_WIDTH), F32), SDS((B, L, LANE), F32),
                 SDS((B, QUADS, 1, LANE), F32), SDS((B, QUADS, 1, LANE), F32), SDS((B, 1, SSM_WIDTH), F32)]
    return _pc(body, name=name, grid=(B, QUADS), in_specs=in_specs, out_specs=out_specs, out_shape=out_shape,
               scratch_shapes=[pltpu.VMEM((2, LANE, QW), F32)])(
        xbc_act, xbc_act, xbc_act, proj, dtb, alog, dskip, hs, dy)


def _gate_norm(yp, z, w):
    v = yp * (z * _sigmoid(z))
    return v * lax.rsqrt(jnp.mean(v * v, axis=-1, keepdims=True) + NORM_EPS) * w


def _gate_fwd(ypre2, proj2, w, *, name):
    T = ypre2.shape[0]
    tr = _div_tile(T, 512, 8)
    G = 512

    def body(y_ref, z_ref, w_ref, o_ref):
        o_ref[...] = _gate_norm(y_ref[...], z_ref[...], w_ref[...]).astype(BF16)

    return _pc(body, name=name, grid=(T // tr, 2),
               in_specs=[BS((tr, G), lambda i, g: (i, g)), BS((tr, G), lambda i, g: (i, 2 + g)), BS((1, G), lambda i, g: (0, g))],
               out_specs=BS((tr, G), lambda i, g: (i, g)), out_shape=SDS((T, SSM_WIDTH), BF16))(ypre2, proj2, w.reshape(1, -1))


def _gate_bwd(ypre2, proj2, w, dy, *, name):
    T = ypre2.shape[0]
    tr = _div_tile(T, 512, 8)
    G = 512

    def body(y_ref, z_ref, w_ref, dy_ref, dyp_ref, dz_ref, dw_ref):
        _, vjp = jax.vjp(_gate_norm, y_ref[...], z_ref[...], w_ref[...])
        dyp, dz, dw = vjp(dy_ref[...])
        dyp_ref[...] = dyp
        dz_ref[...] = dz.astype(BF16)
        dw_ref[0] = dw

    tile = BS((tr, G), lambda i, g: (i, g))
    return _pc(body, name=name, grid=(T // tr, 2),
               in_specs=[tile, BS((tr, G), lambda i, g: (i, 2 + g)), BS((1, G), lambda i, g: (0, g)), tile],
               out_specs=[tile, tile, BS((1, 1, G), lambda i, g: (i, 0, g))],
               out_shape=[SDS((T, SSM_WIDTH), F32), SDS((T, SSM_WIDTH), BF16), SDS((T // tr, 1, SSM_WIDTH), F32)])(
        ypre2, proj2, w.reshape(1, -1), dy)


def _first_half():
    return lax.broadcasted_iota(jnp.int32, (1, LANE), 1) < HEAD_DIM


def _dup_kv_head(pair, odd):
    rolled = pltpu.roll(pair, HEAD_DIM, 1)
    return jnp.where(_first_half(), rolled, pair) if odd else jnp.where(_first_half(), pair, rolled)


def _stack_heads(quad):
    first = _first_half()
    lo, hi = quad[:, :LANE], quad[:, LANE:]
    return jnp.concatenate([jnp.where(first, lo, 0.0), jnp.where(first, 0.0, lo), jnp.where(first, hi, 0.0),
                            jnp.where(first, 0.0, hi)], axis=0)


def _unstack_heads(o):
    first = _first_half()
    return jnp.concatenate([jnp.where(first, o[:BLOCK], o[BLOCK:2 * BLOCK]), jnp.where(first, o[2 * BLOCK:3 * BLOCK], o[3 * BLOCK:])], axis=1)


def _fold_kv_head(d, odd):
    tot = d + pltpu.roll(d, HEAD_DIM, 1)
    return jnp.where(_first_half(), 0.0, tot) if odd else jnp.where(_first_half(), tot, 0.0)


def _attn_softmax(s, sink):
    m = jnp.maximum(jnp.max(s, axis=-1, keepdims=True), sink)
    p = jnp.exp(s - m)
    ps = jnp.exp(sink - m)
    inv = 1.0 / (jnp.sum(p, axis=-1, keepdims=True) + ps)
    return p * inv, ps * inv


def _attn_colneg(n, L):
    kpos = n * BLOCK - WINDOW + lax.broadcasted_iota(jnp.int32, (1, KEY_SPAN), 1)
    return jnp.where((kpos >= 0) & (kpos < L), 0.0, NEG)


def _attn_in_specs(L):
    nblk = L // BLOCK
    kv = lambda o, col: BS((1, BLOCK, 4 * HEAD_DIM), lambda b, n: (b, jnp.clip(n + o, 0, nblk - 1), col))
    kcol, vcol = 3584 // 256, 3840 // 256
    return [BS((1, BLOCK, ATTN_HEADS * HEAD_DIM), lambda b, n: (b, n, 0)), kv(-1, kcol), kv(0, kcol), kv(1, kcol),
            kv(-1, vcol), kv(0, vcol), kv(1, vcol),
            BS((ATTN_HEADS, BLOCK, KEY_SPAN), lambda b, n: (0, 0, 0)), BS((ATTN_HEADS * BLOCK, 1), lambda b, n: (0, 0))]


def _attn_fwd(proj, bias, sinkcol, *, name, rider=None):
    B, L, _ = proj.shape

    def body(q_ref, k0, k1, k2, v0, v1, v2, bias_ref, sink_ref, o_ref):
        colneg = _attn_colneg(pl.program_id(1), L)
        kcat = jnp.concatenate([k0[0], k1[0], k2[0]], axis=0)
        vcat = jnp.concatenate([v0[0], v1[0], v2[0]], axis=0)
        scores, probs = [], []
        for g in range(KV_HEADS):
            pair = slice(LANE * (g // 2), LANE * (g // 2) + LANE)
            quad = slice(4 * HEAD_DIM * g, 4 * HEAD_DIM * (g + 1))
            kd = _dup_kv_head(kcat[:, pair], g % 2).astype(BF16)
            qs = (_stack_heads(q_ref[0, :, quad]) * HEAD_DIM ** -0.5).astype(BF16)
            scores.append(lax.dot_general(qs, kd, (((1,), (1,)), ((), ())), preferred_element_type=F32))
        for g in range(KV_HEADS):
            pn, _ = _attn_softmax(scores[g] + bias_ref[4 * g:4 * g + 4].reshape(4 * BLOCK, KEY_SPAN) + colneg,
                                  sink_ref[4 * BLOCK * g:4 * BLOCK * (g + 1)])
            probs.append(pn.astype(BF16))
        for g in range(KV_HEADS):
            pair = slice(LANE * (g // 2), LANE * (g // 2) + LANE)
            quad = slice(4 * HEAD_DIM * g, 4 * HEAD_DIM * (g + 1))
            vd = _dup_kv_head(vcat[:, pair], g % 2).astype(BF16)
            o = lax.dot_general(probs[g], vd, (((1,), (0,)), ((), ())), preferred_element_type=F32)
            o_ref[0, :, quad] = _unstack_heads(o).astype(BF16)

    (out,), carried = _pc_carry(body, (proj, proj, proj, proj, proj, proj, proj, bias, sinkcol), name=name, grid=(B, L // BLOCK),
                                in_specs=_attn_in_specs(L),
                                out_specs=[BS((1, BLOCK, ATTN_HEADS * HEAD_DIM), lambda b, n: (b, n, 0))],
                                out_shape=[SDS((B, L, ATTN_HEADS * HEAD_DIM), BF16)], rider=rider)
    return out, carried


def _attn_bwd(proj, bias, sinkcol, dout, *, name, rider=None):
    B, L, _ = proj.shape
    nblk = L // BLOCK
    nn, nt, tn = (((1,), (0,)), ((), ())), (((1,), (1,)), ((), ())), (((0,), (0,)), ((), ()))

    def body(q_ref, k0, k1, k2, v0, v1, v2, bias_ref, sink_ref, do_ref, dq_ref, dk_ref, dv_ref, dbias_ref, dsink_ref):
        b, n = pl.program_id(0), pl.program_id(1)

        @pl.when(n == 0)
        def _():
            dk_ref[...] = jnp.zeros_like(dk_ref)
            dv_ref[...] = jnp.zeros_like(dv_ref)

        @pl.when((n == 0) & (b == 0))
        def _():
            dbias_ref[...] = jnp.zeros_like(dbias_ref)
            dsink_ref[...] = jnp.zeros_like(dsink_ref)

        colneg = _attn_colneg(n, L)
        kcat = jnp.concatenate([k0[0], k1[0], k2[0]], axis=0)
        vcat = jnp.concatenate([v0[0], v1[0], v2[0]], axis=0)
        krows = [pl.ds(pl.multiple_of(jnp.clip(n + o, 0, nblk - 1) * BLOCK, BLOCK), BLOCK) for o in (-1, 0, 1)]
        ops, mids = [], []
        for g in range(KV_HEADS):
            pair = slice(LANE * (g // 2), LANE * (g // 2) + LANE)
            quad = slice(4 * HEAD_DIM * g, 4 * HEAD_DIM * (g + 1))
            kd = _dup_kv_head(kcat[:, pair], g % 2).astype(BF16)
            vd = _dup_kv_head(vcat[:, pair], g % 2).astype(BF16)
            qs = (_stack_heads(q_ref[0, :, quad]) * HEAD_DIM ** -0.5).astype(BF16)
            dos = _stack_heads(do_ref[0, :, quad].astype(F32)).astype(BF16)
            ops.append((kd, qs, dos, lax.dot_general(qs, kd, nt, preferred_element_type=F32),
                        lax.dot_general(dos, vd, nt, preferred_element_type=F32)))
        for g in range(KV_HEADS):
            rows = slice(4 * BLOCK * g, 4 * BLOCK * (g + 1))
            _, _, _, s, dpn = ops[g]
            pn, psink = _attn_softmax(s + bias_ref[4 * g:4 * g + 4].reshape(4 * BLOCK, KEY_SPAN) + colneg, sink_ref[rows])
            r = jnp.sum(dpn * pn, axis=-1, keepdims=True)
            ds = pn * (dpn - r)
            dbias_ref[4 * g:4 * g + 4] += ds.reshape(4, BLOCK, KEY_SPAN)
            dsink_ref[rows] += -psink * r
            mids.append((pn.astype(BF16), ds.astype(BF16)))
        for g in range(KV_HEADS):
            pair = slice(LANE * (g // 2), LANE * (g // 2) + LANE)
            quad = slice(4 * HEAD_DIM * g, 4 * HEAD_DIM * (g + 1))
            kd, qs, dos, _, _ = ops[g]
            pnb, dsb = mids[g]
            dvd = lax.dot_general(pnb, dos, tn, preferred_element_type=F32)
            dkd = lax.dot_general(dsb, qs, tn, preferred_element_type=F32)
            dqs = lax.dot_general(dsb, kd, nn, preferred_element_type=F32) * HEAD_DIM ** -0.5
            dq_ref[0, :, quad] = _unstack_heads(dqs).astype(BF16)
            dk_g, dv_g = _fold_kv_head(dkd, g % 2), _fold_kv_head(dvd, g % 2)
            for o in range(3):
                dk_ref[0, krows[o], pair] += dk_g[o * BLOCK:(o + 1) * BLOCK]
                dv_ref[0, krows[o], pair] += dv_g[o * BLOCK:(o + 1) * BLOCK]

    qspec = BS((1, BLOCK, ATTN_HEADS * HEAD_DIM), lambda b, n: (b, n, 0))
    kvout = BS((1, L, 4 * HEAD_DIM), lambda b, n: (b, 0, 0))
    outs, carried = _pc_carry(
        body, (proj, proj, proj, proj, proj, proj, proj, bias, sinkcol, dout), name=name, grid=(B, nblk),
        in_specs=_attn_in_specs(L) + [qspec],
        out_specs=[qspec, kvout, kvout, BS((ATTN_HEADS, BLOCK, KEY_SPAN), lambda b, n: (0, 0, 0)),
                   BS((ATTN_HEADS * BLOCK, 1), lambda b, n: (0, 0))],
        out_shape=[SDS((B, L, ATTN_HEADS * HEAD_DIM), BF16), SDS((B, L, 4 * HEAD_DIM), F32), SDS((B, L, 4 * HEAD_DIM), F32),
                   SDS((ATTN_HEADS, BLOCK, KEY_SPAN), F32), SDS((ATTN_HEADS * BLOCK, 1), F32)], rider=rider)
    return (*outs, carried)


def _t5_bucket(rel):
    half = REL_BUCKETS // 2
    max_exact = half // 2
    ret = jnp.where(rel > 0, half, 0)
    n = jnp.abs(rel)
    nf = jnp.maximum(n, 1).astype(F32)
    large = max_exact + (jnp.log(nf / max_exact) / math.log(REL_MAX_DIST / max_exact) * (half - max_exact)).astype(jnp.int32)
    large = jnp.minimum(large, half - 1)
    return ret + jnp.where(n < max_exact, n, large)


def _bucket_table():
    rel = jnp.arange(KEY_SPAN)[None, :] - WINDOW - jnp.arange(BLOCK)[:, None]
    return _t5_bucket(rel).astype(jnp.int32)


def _bias_expand(rel_bias, bucket, *, name):
    rbt = jnp.zeros((ATTN_HEADS, 1, LANE), F32).at[:, 0, :REL_BUCKETS].set(rel_bias.T)

    def body(rb_ref, bk_ref, o_ref):
        lane = lax.broadcasted_iota(jnp.int32, (1, LANE), 1)
        row = rb_ref[0]
        bk = bk_ref[...]
        acc = jnp.zeros((BLOCK, KEY_SPAN), F32)
        for r in range(REL_BUCKETS):
            val = jnp.sum(jnp.where(lane == r, row, 0.0), axis=1, keepdims=True)
            acc = jnp.where(bk == r, val, acc)
        rel = (lax.broadcasted_iota(jnp.int32, (BLOCK, KEY_SPAN), 1) - WINDOW
               - lax.broadcasted_iota(jnp.int32, (BLOCK, KEY_SPAN), 0))
        o_ref[0] = jnp.where(jnp.abs(rel) <= WINDOW, acc, NEG)

    return _pc(body, name=name, grid=(ATTN_HEADS,),
               in_specs=[BS((1, 1, LANE), lambda h: (h, 0, 0)), BS((BLOCK, KEY_SPAN), lambda h: (0, 0))],
               out_specs=BS((1, BLOCK, KEY_SPAN), lambda h: (h, 0, 0)), out_shape=SDS((ATTN_HEADS, BLOCK, KEY_SPAN), F32))(rbt, bucket)


def _bias_reduce(dbias, bucket, *, name):
    def body(db_ref, bk_ref, o_ref):
        lane = lax.broadcasted_iota(jnp.int32, (1, LANE), 1)
        x = db_ref[0]
        bk = bk_ref[...]
        acc = jnp.zeros((1, LANE), F32)
        for r in range(REL_BUCKETS):
            part = jnp.sum(jnp.where(bk == r, x, 0.0), axis=1, keepdims=True)
            acc = jnp.where(lane == r, jnp.sum(part, axis=0, keepdims=True), acc)
        o_ref[0] = acc

    out = _pc(body, name=name, grid=(ATTN_HEADS,),
              in_specs=[BS((1, BLOCK, KEY_SPAN), lambda h: (h, 0, 0)), BS((BLOCK, KEY_SPAN), lambda h: (0, 0))],
              out_specs=BS((1, 1, LANE), lambda h: (h, 0, 0)), out_shape=SDS((ATTN_HEADS, 1, LANE), F32))(dbias, bucket)
    return out[:, 0, :REL_BUCKETS].T


def _loss_head(x2, w, target, *, name):
    T, D = x2.shape
    tr = _div_tile(T, 512, 8)

    def tile_loss(x, w, t):
        err = _rms(x, w) - t
        return 0.5 * jnp.sum(jnp.mean(err * err, axis=-1, keepdims=True), axis=0, keepdims=True)

    def body(x_ref, w_ref, t_ref, loss_ref, dx_ref, dw_ref):
        t = t_ref[...]
        l, vjp = jax.vjp(lambda x, w: tile_loss(x, w, t), x_ref[...], w_ref[...])
        dx, dw = vjp(jnp.ones((1, 1), F32))
        dx_ref[...] = dx

        @pl.when(pl.program_id(0) == 0)
        def _():
            dw_ref[...] = jnp.zeros_like(dw_ref)
            loss_ref[...] = jnp.zeros_like(loss_ref)

        dw_ref[...] += dw
        loss_ref[...] += l + jnp.zeros((1, LANE), F32)

    row = BS((tr, D), lambda i: (i, 0))
    one = BS((1, D), lambda i: (0, 0))
    return _pc(body, name=name, grid=(T // tr,), in_specs=[row, one, row],
               out_specs=[BS((1, LANE), lambda i: (0, 0)), row, one],
               out_shape=[SDS((1, LANE), F32), SDS((T, D), F32), SDS((1, D), F32)])(x2, w.reshape(1, D), target)


def _adamw(w2, g2, m2, v2, *, name):
    R, C = w2.shape
    tr = _div_tile(R, 256, 8)
    c1 = 1.0 - ADAM_B1 ** ADAM_STEP
    c2 = 1.0 - ADAM_B2 ** ADAM_STEP

    def body(w_ref, g_ref, m_ref, v_ref, d_ref, nm_ref, nv_ref):
        g = g_ref[...]
        m = ADAM_B1 * m_ref[...] + (1.0 - ADAM_B1) * g
        v = ADAM_B2 * v_ref[...] + (1.0 - ADAM_B2) * (g * g)
        d_ref[...] = -ADAM_LR * ((m / c1) / (jnp.sqrt(v / c2) + ADAM_EPS) + ADAM_WD * w_ref[...])
        nm_ref[...] = m
        nv_ref[...] = v

    t = BS((tr, C), lambda i: (i, 0))
    return _pc(body, name=name, grid=(R // tr,), in_specs=[t, t, t, t], out_specs=[t, t, t],
               out_shape=[SDS((R, C), F32)] * 3)(w2, g2, m2, v2)


def _place():
    return lax.axis_index("x"), lax.axis_index("y"), lax.axis_index("c")


def _gather_rider(shards):
    na = len(shards)

    def copies(ins, outs, sems):
        send_sems, recv_sems = sems
        x, y, c = _place()
        for a in range(na):
            for k, (px, py) in enumerate([(1 - x, y), (x, 1 - y), (1 - x, 1 - y)]):
                send = functools.partial(pltpu.make_async_remote_copy, ins[a], outs[a].at[2 * x + y], send_sems.at[a, k],
                                         recv_sems.at[a, k], device_id=(px, py, c), device_id_type=MESH)
                got = outs[a].at[2 * px + py]
                arrived = functools.partial(pltpu.make_async_remote_copy, got, got, send_sems.at[a, k], recv_sems.at[a, k],
                                            device_id=(px, py, c), device_id_type=MESH)
                yield send, arrived

    def start(ins, outs, sems):
        for send, _ in copies(ins, outs, sems):
            send().start()

    def finish(ins, outs, sems):
        both = list(copies(ins, outs, sems))
        for _, arrived in both:
            arrived().wait_recv()
        for send, _ in both:
            send().wait_send()

    return dict(ins=list(shards), prev=[], out_shape=[SDS((N_CHIP,) + s.shape, s.dtype) for s in shards],
                scratch=[pltpu.SemaphoreType.DMA((na, 3)), pltpu.SemaphoreType.DMA((na, 3))], start=start, finish=finish)


def _scatter_rider(bufs, layer, prev):
    na = len(bufs)
    h = layer // (DEPTH // 2)

    def copies(ins, outs, sems):
        send_sems, recv_sems, local_sems = sems
        x, y, c = _place()
        me = 4 * x + 2 * y + c
        for a in range(na):
            for j in range(N_CHIP):
                is_self = ((2 * x + y) == j) & (c == h)
                local = functools.partial(pltpu.make_async_copy, ins[a].at[j], outs[a].at[me], local_sems.at[a])
                remote = functools.partial(pltpu.make_async_remote_copy, ins[a].at[j], outs[a].at[me], send_sems.at[a, j],
                                           recv_sems.at[a, me], device_id=(j // 2, j % 2, h), device_id_type=MESH)
                yield is_self, local, remote

    def start(ins, outs, sems):
        for is_self, local, remote in copies(ins, outs, sems):
            pl.when(is_self)(lambda: local().start())
            pl.when(jnp.logical_not(is_self))(lambda: remote().start())

    def finish(ins, outs, sems):
        _, recv_sems, _ = sems
        x, y, c = _place()
        me = 4 * x + 2 * y + c
        for a in range(na):
            for s in range(N_DEV):
                got = outs[a].at[s]
                arrived = functools.partial(pltpu.make_async_remote_copy, got, got, recv_sems.at[a, s], recv_sems.at[a, s],
                                            device_id=(s // 4, (s // 2) % 2, s % 2), device_id_type=MESH)
                pl.when((c == h) & (me != s))(lambda: arrived().wait_recv())
        for is_self, local, remote in copies(ins, outs, sems):
            pl.when(is_self)(lambda: local().wait())
            pl.when(jnp.logical_not(is_self))(lambda: remote().wait_send())

    return dict(ins=list(bufs), prev=list(prev), out_shape=[SDS((N_DEV,) + b.shape[1:], b.dtype) for b in bufs],
                scratch=[pltpu.SemaphoreType.DMA((na, N_CHIP)), pltpu.SemaphoreType.DMA((na, N_DEV)), pltpu.SemaphoreType.DMA((na,))],
                start=start, finish=finish)


def _sum_sources(parts, *, name):
    _, R, C = parts.shape
    tr = _div_tile(R, 256, 16)

    def body(p_ref, o_ref):
        acc = p_ref[0].astype(F32)
        for s in range(1, N_DEV):
            acc = acc + p_ref[s].astype(F32)
        o_ref[...] = acc

    return _pc(body, name=name, grid=(R // tr,), in_specs=[BS((N_DEV, tr, C), lambda i: (0, i, 0))],
               out_specs=BS((tr, C), lambda i: (i, 0)), out_shape=SDS((R, C), F32))(parts)


def _join_halves(halves, *, name):
    na = len(halves)

    def body(*refs):
        ins, outs = refs[:na], refs[na:2 * na]
        send_sems, recv_sems = refs[2 * na:]
        x, y, c = _place()
        cps = []
        for a in range(na):
            cp = pltpu.make_async_remote_copy(ins[a], outs[a], send_sems.at[a], recv_sems.at[a],
                                              device_id=(x, y, 1 - c), device_id_type=MESH)
            cp.start()
            cps.append(cp)
        for cp in cps:
            cp.wait_recv()
        for cp in cps:
            cp.wait_send()

    any_spec = BS(memory_space=pl.ANY)
    return pl.pallas_call(
        body, name=name, in_specs=[any_spec] * na, out_specs=[any_spec] * na,
        out_shape=[SDS(h.shape, h.dtype) for h in halves],
        scratch_shapes=[pltpu.SemaphoreType.DMA((na,)), pltpu.SemaphoreType.DMA((na,))],
        compiler_params=pltpu.CompilerParams(has_side_effects=True))(*halves)


def _allreduce_small(vec, *, name):
    R = vec.shape[0]

    def body(v_ref, o_ref, all_ref, send_sems, recv_sems):
        x, y, c = _place()
        me = 4 * x + 2 * y + c
        all_ref[me] = v_ref[...]
        sends = []
        for r in range(1, N_DEV):
            tgt = (x ^ (r >> 2), y ^ ((r >> 1) & 1), c ^ (r & 1))
            cp = pltpu.make_async_remote_copy(v_ref, all_ref.at[me], send_sems.at[r - 1], recv_sems.at[r - 1],
                                              device_id=tgt, device_id_type=MESH)
            cp.start()
            sends.append(cp)
        for r in range(1, N_DEV):
            tx, ty, tc = x ^ (r >> 2), y ^ ((r >> 1) & 1), c ^ (r & 1)
            got = all_ref.at[4 * tx + 2 * ty + tc]
            pltpu.make_async_remote_copy(got, got, send_sems.at[r - 1], recv_sems.at[r - 1],
                                         device_id=(tx, ty, tc), device_id_type=MESH).wait_recv()
        for cp in sends:
            cp.wait_send()
        acc = all_ref[0]
        for s in range(1, N_DEV):
            acc = acc + all_ref[s]
        o_ref[...] = acc

    vm = BS(memory_space=pltpu.VMEM)
    return pl.pallas_call(
        body, name=name, in_specs=[vm], out_specs=vm, out_shape=SDS((R, LANE), F32),
        scratch_shapes=[pltpu.VMEM((N_DEV, R, LANE), F32), pltpu.SemaphoreType.DMA((N_DEV - 1,)), pltpu.SemaphoreType.DMA((N_DEV - 1,))],
        compiler_params=pltpu.CompilerParams(has_side_effects=True, vmem_limit_bytes=VMEM_LIMIT_BYTES))(vec)


def _pack(arrs):
    rows = []
    for a in arrs:
        f = a.reshape(-1).astype(F32)
        n = -(-f.shape[0] // LANE) * LANE
        rows.append(jnp.pad(f, (0, n - f.shape[0])).reshape(-1, LANE))
    v = jnp.concatenate(rows, axis=0)
    pad = -v.shape[0] % 8
    return jnp.pad(v, ((0, pad), (0, 0)))


def _unpack(v, shapes):
    out, r = [], 0
    for s in shapes:
        n = int(np.prod(s)) if len(s) else 1
        nr = -(-n // LANE)
        out.append(v[r:r + nr].reshape(-1)[:n].reshape(s))
        r += nr
    return out


def _perm_in_cols(w_full):
    z, xbc, dt, q, k, v = (w_full[..., :Z_END], w_full[..., Z_END:XBC_END], w_full[..., XBC_END:DT_END],
                           w_full[..., DT_END:Q_END], w_full[..., Q_END:K_END], w_full[..., K_END:])
    pad = jnp.zeros(dt.shape[:-1] + (LANE - dt.shape[-1],), dt.dtype)
    return jnp.concatenate([q, z, xbc, k, v, dt, pad], axis=-1)


def _unperm_in_cols(g):
    q, z, xbc, k, v, dt = (g[..., :1024], g[..., 1024:2048], g[..., 2048:3584], g[..., 3584:3840], g[..., 3840:4096],
                           g[..., 4096:4096 + 2 * SSM_HEADS])
    return jnp.concatenate([z, xbc, dt, q, k, v], axis=-1)


def _dt_cols(a):
    return jnp.pad(a.reshape(1, 2 * SSM_HEADS), ((0, 0), (0, LANE - 2 * SSM_HEADS)))


def _layer_fwd(i, x, wts, small, band_bias, rider=None):
    B, L, D = x.shape
    T = B * L
    x2 = x.reshape(T, D)
    h = _rmsnorm_fwd(x2, small["norm1_w"][i], name=f"norm1_{i}")
    proj2 = _mm(h, wts["w_in"], name=f"in_proj_{i}", tn=1408)
    proj = proj2.reshape(B, L, P_COLS)
    xbc_act = _conv_fwd(proj, 2048 // 256, small["conv_w"][i], small["conv_b"][i], taps=SSM_CONV, ct=256,
                        out_dtype=F32, name=f"ssm_conv_{i}")
    dtb, alog = _dt_cols(small["dt_bias"][i]), _dt_cols(small["a_log"][i])
    dskip = jnp.repeat(small["d_skip"][i], HEAD_DIM).reshape(1, SSM_WIDTH)
    ypre, hs = _ssd_fwd(xbc_act, proj, dtb, alog, dskip, name=f"ssd_{i}")
    y_ssm = _gate_fwd(ypre.reshape(T, SSM_WIDTH), proj2, small["ssm_norm_w"][i], name=f"gate_{i}")
    sinkcol = jnp.repeat(small["attn_sink"][i], BLOCK).reshape(ATTN_HEADS * BLOCK, 1)
    y_attn, carried = _attn_fwd(proj, band_bias, sinkcol, name=f"attn_{i}", rider=rider)
    y_attn = y_attn.reshape(T, D)
    w_out = wts["w_out"]
    x_mid = _mm(y_ssm, w_out[:SSM_WIDTH], add=x2, name=f"out_proj_a_{i}")
    x_mid = _mm(y_attn, w_out[SSM_WIDTH:], add=x_mid, name=f"out_proj_b_{i}")
    h2 = _rmsnorm_fwd(x_mid, small["norm2_w"][i], name=f"norm2_{i}")
    gu2 = _mm(h2, wts["w_up"], name=f"up_proj_{i}", tn=1408)
    gu = gu2.reshape(B, L, 2 * D_FF)
    act = _conv_fwd(gu, 0, small["ffn_conv_w"][i], small["ffn_conv_b"][i], taps=FFN_CONV, ct=256, gate_blk0=D_FF // 256,
                    out_dtype=BF16, name=f"ffn_conv_{i}")
    x_out = _mm(act.reshape(T, D_FF), wts["w_down"], add=x_mid, name=f"down_proj_{i}", tk=1408)
    saved = dict(x2=x2, h=h, proj2=proj2, xbc_act=xbc_act, dtb=dtb, alog=alog, dskip=dskip, ypre=ypre, hs=hs, y_ssm=y_ssm,
                 sinkcol=sinkcol, y_attn=y_attn, x_mid=x_mid, h2=h2, gu=gu, act=act)
    return x_out.reshape(B, L, D), saved, carried


def _layer_bwd(i, dx_out, sv, wts, small, band_bias, rider=None):
    T, D = dx_out.shape
    B, L = sv["gu"].shape[:2]
    g = {}
    dxb = dx_out.astype(BF16)
    dact = _mm(dxb, wts["w_down"], tb=True, out_dtype=BF16, name=f"d_act_{i}", tn=1408)
    g["w_down"] = _mm(sv["act"].reshape(T, D_FF), dxb, ta=True, name=f"dw_down_{i}", tm=1408)
    dg, du, dcw, dcb = _conv_bwd(sv["gu"], 0, small["ffn_conv_w"][i], small["ffn_conv_b"][i], dact.reshape(B, L, D_FF),
                                 taps=FFN_CONV, ct=256, gate_blk0=D_FF // 256, name=f"d_ffn_conv_{i}")
    g["ffn_conv_w"] = jnp.sum(dcw, axis=0)[:FFN_CONV]
    g["ffn_conv_b"] = jnp.sum(dcb, axis=(0, 1))
    dgu = jnp.concatenate([dg, du], axis=-1).reshape(T, 2 * D_FF)
    dh2 = _mm(dgu, wts["w_up"], tb=True, name=f"d_h2_{i}", tk=1408)
    g["w_up"] = _mm(sv["h2"], dgu, ta=True, name=f"dw_up_{i}", tn=1408)
    dx_mid, dw2 = _rmsnorm_bwd(sv["x_mid"], small["norm2_w"][i], dh2, dx_out, name=f"d_norm2_{i}")
    g["norm2_w"] = dw2[0]
    dmb = dx_mid.astype(BF16)
    w_out = wts["w_out"]
    dy_ssm = _mm(dmb, w_out[:SSM_WIDTH], tb=True, name=f"d_y_ssm_{i}")
    dy_attn = _mm(dmb, w_out[SSM_WIDTH:], tb=True, out_dtype=BF16, name=f"d_y_attn_{i}")
    g["w_out"] = jnp.concatenate([_mm(sv["y_ssm"], dmb, ta=True, name=f"dw_out_a_{i}"),
                                  _mm(sv["y_attn"], dmb, ta=True, name=f"dw_out_b_{i}")], axis=0)
    dypre, dz, dwn = _gate_bwd(sv["ypre"].reshape(T, SSM_WIDTH), sv["proj2"], small["ssm_norm_w"][i], dy_ssm, name=f"d_gate_{i}")
    g["ssm_norm_w"] = jnp.sum(dwn, axis=(0, 1))
    proj = sv["proj2"].reshape(B, L, P_COLS)
    dxs, dbm, dcm, ddt, ddtb, dalog, ddsk = _ssd_bwd(sv["xbc_act"], proj, sv["dtb"], sv["alog"], sv["dskip"], sv["hs"],
                                                    dypre.reshape(B, L, SSM_WIDTH), name=f"d_ssd_{i}")
    g["dt_bias"] = jnp.sum(ddtb, axis=(0, 1, 2))[:2 * SSM_HEADS].reshape(2, SSM_HEADS)
    g["a_log"] = jnp.sum(dalog, axis=(0, 1, 2))[:2 * SSM_HEADS].reshape(2, SSM_HEADS)
    g["d_skip"] = jnp.sum(ddsk.reshape(B, SSM_HEADS, HEAD_DIM), axis=(0, 2))
    dxbc_act = jnp.concatenate([dxs, dbm, dcm], axis=-1)
    dxbc, dcw, dcb = _conv_bwd(proj, 2048 // 256, small["conv_w"][i], small["conv_b"][i], dxbc_act, taps=SSM_CONV, ct=256,
                               name=f"d_ssm_conv_{i}")
    g["conv_w"] = jnp.sum(dcw, axis=0)[:SSM_CONV]
    g["conv_b"] = jnp.sum(dcb, axis=(0, 1))
    dq, dk, dv, dbias, dsink, carried = _attn_bwd(proj, band_bias, sv["sinkcol"], dy_attn.reshape(B, L, D), name=f"d_attn_{i}",
                                                  rider=rider)
    g["attn_sink"] = jnp.sum(dsink.reshape(ATTN_HEADS, BLOCK), axis=1)
    dproj = jnp.concatenate([dq, dz.reshape(B, L, SSM_WIDTH), dxbc, dk.astype(BF16), dv.astype(BF16), ddt.astype(BF16)],
                            axis=-1).reshape(T, P_COLS)
    dh = _mm(dproj, wts["w_in"], tb=True, name=f"d_h_{i}", tk=1408)
    g["w_in"] = _unperm_in_cols(_mm(sv["h"], dproj, ta=True, name=f"dw_in_{i}", tn=1408))
    dx_in, dw1 = _rmsnorm_bwd(sv["x2"], small["norm1_w"][i], dh, dx_mid, name=f"d_norm1_{i}")
    g["norm1_w"] = dw1[0]
    return dx_in, g, dbias, carried


_BIG = ("w_in", "w_out", "w_up", "w_down")
_BIG_AXIS = {"w_in": 2, "w_out": 1, "w_up": 2, "w_down": 1}
_SMALL = ("rel_bias", "norm1_w", "conv_w", "conv_b", "dt_bias", "a_log", "d_skip", "ssm_norm_w", "attn_sink", "norm2_w",
          "ffn_conv_w", "ffn_conv_b", "final_norm_w")
_SMALL_SHARDED = ("conv_w", "ffn_conv_w")
_ORDER = ("rel_bias", "norm1_w", "w_in", "conv_w", "conv_b", "dt_bias", "a_log", "d_skip", "ssm_norm_w", "attn_sink", "w_out",
          "norm2_w", "w_up", "ffn_conv_w", "ffn_conv_b", "w_down", "final_norm_w")


def _local_step(x, target, small, wts=None, exchange=None):
    B, L, D = x.shape
    bucket = _bucket_table()
    band_bias = _bias_expand(small["rel_bias"], bucket, name="band_bias")
    if exchange is not None:
        wts = [exchange["weights"](0, _run_rider(exchange["gather"](0), name="gather_weights_0"))] + [None] * (DEPTH - 1)
    saved = []
    for i in range(DEPTH):
        rider = exchange["gather"](i + 1) if exchange is not None and i + 1 < DEPTH else None
        x, sv, carried = _layer_fwd(i, x, wts[i], small, band_bias, rider)
        if rider is not None:
            wts[i + 1] = exchange["weights"](i + 1, carried)
        saved.append(sv)
    loss, dx, dwf = _loss_head(x.reshape(B * L, D), small["final_norm_w"], target.reshape(B * L, D), name="loss_head")
    per_layer = []
    dbias = jnp.zeros((ATTN_HEADS, BLOCK, KEY_SPAN), F32)
    rider = None
    for i in reversed(range(DEPTH)):
        dx, g, dbias_i, carried = _layer_bwd(i, dx, saved[i], wts[i], small, band_bias, rider)
        if rider is not None:
            exchange["collect"](i + 1, carried)
        if exchange is not None:
            rider = exchange["scatter"](i, {k: g.pop(k) for k in _BIG})
        dbias = dbias + dbias_i
        per_layer.append(g)
    if exchange is not None:
        exchange["collect"](0, _run_rider(rider, name="scatter_grads_0"))
    per_layer.reverse()
    grads = {k: jnp.stack([g[k] for g in per_layer]) for k in per_layer[0]}
    grads["rel_bias"] = _bias_reduce(dbias, bucket, name="d_rel_bias")
    grads["final_norm_w"] = dwf[0]
    return loss, dx.reshape(B, L, D), grads


def _split_by_chip(g, axis):
    shp = g.shape
    n = shp[axis] // N_CHIP
    g = g.reshape(shp[:axis] + (N_CHIP, n) + shp[axis + 1:])
    return jnp.moveaxis(g, axis, 0)


def _join_chips(a, axis):
    a = jnp.moveaxis(a, 0, axis)
    shp = a.shape
    return a.reshape(shp[:axis] + (shp[axis] * shp[axis + 1],) + shp[axis + 2:])


def kernel(x, rel_bias, norm1_w, w_in, conv_w, conv_b, dt_bias, a_log, d_skip, ssm_norm_w, attn_sink, w_out, norm2_w, w_up, ffn_conv_w, ffn_conv_b, w_down, final_norm_w, loss_target, m_rel_bias, m_norm1_w, m_w_in, m_conv_w, m_conv_b, m_dt_bias, m_a_log, m_d_skip, m_ssm_norm_w, m_attn_sink, m_w_out, m_norm2_w, m_w_up, m_ffn_conv_w, m_ffn_conv_b, m_w_down, m_final_norm_w, v_rel_bias, v_norm1_w, v_w_in, v_conv_w, v_conv_b, v_dt_bias, v_a_log, v_d_skip, v_ssm_norm_w, v_attn_sink, v_w_out, v_norm2_w, v_w_up, v_ffn_conv_w, v_ffn_conv_b, v_w_down, v_final_norm_w):
    w = dict(rel_bias=rel_bias, norm1_w=norm1_w, w_in=w_in, conv_w=conv_w, conv_b=conv_b, dt_bias=dt_bias, a_log=a_log,
             d_skip=d_skip, ssm_norm_w=ssm_norm_w, attn_sink=attn_sink, w_out=w_out, norm2_w=norm2_w, w_up=w_up,
             ffn_conv_w=ffn_conv_w, ffn_conv_b=ffn_conv_b, w_down=w_down, final_norm_w=final_norm_w)
    m = dict(rel_bias=m_rel_bias, norm1_w=m_norm1_w, w_in=m_w_in, conv_w=m_conv_w, conv_b=m_conv_b, dt_bias=m_dt_bias,
             a_log=m_a_log, d_skip=m_d_skip, ssm_norm_w=m_ssm_norm_w, attn_sink=m_attn_sink, w_out=m_w_out, norm2_w=m_norm2_w,
             w_up=m_w_up, ffn_conv_w=m_ffn_conv_w, ffn_conv_b=m_ffn_conv_b, w_down=m_w_down, final_norm_w=m_final_norm_w)
    v = dict(rel_bias=v_rel_bias, norm1_w=v_norm1_w, w_in=v_w_in, conv_w=v_conv_w, conv_b=v_conv_b, dt_bias=v_dt_bias,
             a_log=v_a_log, d_skip=v_d_skip, ssm_norm_w=v_ssm_norm_w, attn_sink=v_attn_sink, w_out=v_w_out, norm2_w=v_norm2_w,
             w_up=v_w_up, ffn_conv_w=v_ffn_conv_w, ffn_conv_b=v_ffn_conv_b, w_down=v_w_down, final_norm_w=v_final_norm_w)
    my_chip = 2 * lax.axis_index("x") + lax.axis_index("y")

    shards = {k: w[k].astype(BF16) for k in _BIG}
    received = {}

    def gather(i):
        return _gather_rider([shards[k][i] for k in _BIG])

    def weights(i, carried):
        full = {k: _join_chips(lax.dynamic_update_index_in_dim(g_, shards[k][i], my_chip, 0), _BIG_AXIS[k] - 1)
                for k, g_ in zip(_BIG, carried)}
        full["w_in"] = _perm_in_cols(full["w_in"])
        return full

    def scatter(i, layer_grads):
        bufs = [_split_by_chip(layer_grads[k], _BIG_AXIS[k] - 1).astype(BF16) for k in _BIG]
        return _scatter_rider(bufs, i, received.get(i % 2, []))

    def collect(i, carried):
        received[i % 2] = list(carried)

    conv_shapes = [(DEPTH, SSM_CONV, CONV_CH), (DEPTH, FFN_CONV, D_FF)]
    placed = [lax.dynamic_update_slice_in_dim(jnp.zeros(s, F32), w[k], my_chip * w[k].shape[2], axis=2)
              for k, s in zip(_SMALL_SHARDED, conv_shapes)]
    lead = (lax.axis_index("c") == 0).astype(F32)
    conv_full = _unpack(_allreduce_small(_pack([p * lead for p in placed]), name="gather_conv_weights"), conv_shapes)
    small = {k: w[k] for k in _SMALL}
    small["conv_w"], small["ffn_conv_w"] = conv_full

    loss_part, grad_x, gp = _local_step(x, loss_target, small,
                                        exchange=dict(gather=gather, weights=weights, scatter=scatter, collect=collect))

    small_shapes = [small[k].shape for k in _SMALL] + [()]
    red = _unpack(_allreduce_small(_pack([gp[k] for k in _SMALL] + [loss_part[0, :1]]), name="reduce_small"), small_shapes)
    gsmall = dict(zip(_SMALL, red[:-1]))
    loss = red[-1]
    for k in _SMALL_SHARDED:
        n = w[k].shape[2]
        gsmall[k] = lax.dynamic_slice_in_dim(gsmall[k], my_chip * n, n, axis=2)

    core = lax.axis_index("c")
    half = DEPTH // 2
    halves = [jnp.stack([_sum_sources(received[p][a], name=f"sum_{k}_{p}") for p in range(half)]) for a, k in enumerate(_BIG)]
    others = _join_halves(halves, name="join_halves")
    gbig = {}
    for k, mine_, theirs_ in zip(_BIG, halves, others):
        full = jnp.zeros((DEPTH,) + mine_.shape[1:], F32)
        full = lax.dynamic_update_slice_in_dim(full, mine_, core * half, axis=0)
        gbig[k] = lax.dynamic_update_slice_in_dim(full, theirs_, (1 - core) * half, axis=0)

    grad, delta, new_m, new_v = {}, {}, {}, {}
    for k in _BIG:
        shp = w[k].shape
        two = lambda a: a.reshape(shp[0] * shp[1], shp[2])
        d_, m_, v_ = _adamw(two(w[k]), two(gbig[k]), two(m[k]), two(v[k]), name=f"adamw_{k}")
        grad[k], delta[k], new_m[k], new_v[k] = gbig[k], d_.reshape(shp), m_.reshape(shp), v_.reshape(shp)
    shapes = [w[k].shape for k in _SMALL]
    d_, m_, v_ = _adamw(_pack([w[k] for k in _SMALL]), _pack([gsmall[k] for k in _SMALL]), _pack([m[k] for k in _SMALL]),
                        _pack([v[k] for k in _SMALL]), name="adamw_small")
    for k, a, b_, c_ in zip(_SMALL, _unpack(d_, shapes), _unpack(m_, shapes), _unpack(v_, shapes)):
        grad[k], delta[k], new_m[k], new_v[k] = gsmall[k], a, b_, c_
    return (loss, grad_x, *[grad[k] for k in _ORDER], *[delta[k] for k in _ORDER], *[new_m[k] for k in _ORDER],
            *[new_v[k] for k in _ORDER])
```

```python
import functools
import math

import jax
import jax.numpy as jnp
import numpy as np
from jax import lax
from jax.experimental import pallas as pl
from jax.experimental.pallas import tpu as pltpu

F32 = jnp.float32
BF16 = jnp.bfloat16
BS = pl.BlockSpec
SDS = jax.ShapeDtypeStruct
MESH = pl.DeviceIdType.MESH

D_MODEL = 1024
DEPTH = 4
SSM_HEADS = 16
SSM_WIDTH = 1024
BC_WIDTH = 256
CONV_CH = 1536
SSM_CONV = 7
CHUNK = 128
ATTN_HEADS = 16
KV_HEADS = 4
HEAD_DIM = 64
WINDOW = 128
BLOCK = 128
KEY_SPAN = 384
REL_BUCKETS = 32
REL_MAX_DIST = 128
D_FF = 2816
FFN_CONV = 3
NORM_EPS = 1e-6
Z_END = 1024
XBC_END = 2560
DT_END = 2592
Q_END = 3616
K_END = 3872
IN_COLS = 4128
P_COLS = 4224
ADAM_LR, ADAM_B1, ADAM_B2, ADAM_EPS, ADAM_WD, ADAM_STEP = 0.001, 0.9, 0.999, 1e-08, 0.01, 10
NEG = -1e30
N_DEV = 8
N_CHIP = 4
LANE = 128
VMEM_LIMIT_BYTES = 48 * 1024 * 1024


def _pc(body, *, name, grid, in_specs, out_specs, out_shape, scratch_shapes=()):
    return pl.pallas_call(
        body, name=name, grid=grid, in_specs=in_specs, out_specs=out_specs, out_shape=out_shape,
        scratch_shapes=list(scratch_shapes),
        compiler_params=pltpu.CompilerParams(dimension_semantics=("arbitrary",) * len(grid),
                                             vmem_limit_bytes=VMEM_LIMIT_BYTES))


def _split_rider_refs(refs, n_in, n_out, rider):
    n_rin = len(rider["ins"]) + len(rider["prev"])
    n_rout = len(rider["out_shape"])
    core = refs[:n_in] + refs[n_in + n_rin:n_in + n_rin + n_out] + refs[n_in + n_rin + n_out + n_rout + len(rider["scratch"]):]
    rins = refs[n_in:n_in + len(rider["ins"])]
    routs = refs[n_in + n_rin + n_out:n_in + n_rin + n_out + n_rout]
    sems = refs[n_in + n_rin + n_out + n_rout:n_in + n_rin + n_out + n_rout + len(rider["scratch"])]
    return core, rins, routs, sems


def _pc_carry(body, args, *, name, grid, in_specs, out_specs, out_shape, scratch_shapes=(), rider=None):
    if rider is None:
        return _pc(body, name=name, grid=grid, in_specs=in_specs, out_specs=out_specs, out_shape=out_shape,
                   scratch_shapes=scratch_shapes)(*args), None
    n_in, n_out = len(in_specs), len(out_shape)
    any_spec = BS(memory_space=pl.ANY)

    def full(*refs):
        core, rins, routs, sems = _split_rider_refs(refs, n_in, n_out, rider)
        ids = [pl.program_id(d) for d in range(len(grid))]
        first = functools.reduce(jnp.logical_and, [i == 0 for i in ids])
        last = functools.reduce(jnp.logical_and, [i == g - 1 for i, g in zip(ids, grid)])

        @pl.when(first)
        def _():
            rider["start"](rins, routs, sems)

        body(*core)

        @pl.when(last)
        def _():
            rider["finish"](rins, routs, sems)

    n_rin = len(rider["ins"])
    outs = pl.pallas_call(
        full, name=name, grid=grid,
        in_specs=list(in_specs) + [any_spec] * (n_rin + len(rider["prev"])),
        out_specs=list(out_specs) + [any_spec] * len(rider["out_shape"]),
        out_shape=list(out_shape) + list(rider["out_shape"]),
        scratch_shapes=list(rider["scratch"]) + list(scratch_shapes),
        input_output_aliases={n_in + n_rin + t: n_out + t for t in range(len(rider["prev"]))},
        compiler_params=pltpu.CompilerParams(dimension_semantics=("arbitrary",) * len(grid), vmem_limit_bytes=VMEM_LIMIT_BYTES,
                                             has_side_effects=True))(*args, *rider["ins"], *rider["prev"])
    return outs[:n_out], outs[n_out:]


def _run_rider(rider, *, name):
    any_spec = BS(memory_space=pl.ANY)
    n_rin = len(rider["ins"])

    def body(*refs):
        _, rins, routs, sems = _split_rider_refs(refs, 0, 0, rider)
        rider["start"](rins, routs, sems)
        rider["finish"](rins, routs, sems)

    return pl.pallas_call(
        body, name=name, in_specs=[any_spec] * (n_rin + len(rider["prev"])), out_specs=[any_spec] * len(rider["out_shape"]),
        out_shape=list(rider["out_shape"]), scratch_shapes=list(rider["scratch"]),
        input_output_aliases={n_rin + t: t for t in range(len(rider["prev"]))},
        compiler_params=pltpu.CompilerParams(has_side_effects=True))(*rider["ins"], *rider["prev"])


def _div_tile(n, pref, mult):
    t = min(pref, n)
    t -= t % mult
    while t >= mult:
        if n % t == 0:
            return t
        t -= mult
    return n


def _mm(a, b, *, name, ta=False, tb=False, add=None, out_dtype=F32, tm=1024, tn=1024, tk=1024):
    if ta:
        K, M = a.shape
    else:
        M, K = a.shape
    N = b.shape[0] if tb else b.shape[1]
    tm, tn, tk = _div_tile(M, tm, LANE), _div_tile(N, tn, LANE), _div_tile(K, tk, LANE)
    nk = K // tk
    dims = (((0,) if ta else (1,), (1,) if tb else (0,)), ((), ()))

    def body_single(*refs):
        r = lax.dot_general(refs[0][...], refs[1][...], dims, preferred_element_type=F32)
        if add is not None:
            r = r + refs[2][...]
        refs[-1][...] = r.astype(out_dtype)

    def body(*refs):
        if add is None:
            a_ref, b_ref, o_ref, acc_ref = refs
        else:
            a_ref, b_ref, add_ref, o_ref, acc_ref = refs
        k = pl.program_id(2)

        @pl.when(k == 0)
        def _():
            acc_ref[...] = jnp.zeros_like(acc_ref)

        acc_ref[...] += lax.dot_general(a_ref[...], b_ref[...], dims, preferred_element_type=F32)

        @pl.when(k == nk - 1)
        def _():
            r = acc_ref[...]
            if add is not None:
                r = r + add_ref[...]
            o_ref[...] = r.astype(out_dtype)

    a_spec = BS((tk, tm), lambda i, j, k: (k, i)) if ta else BS((tm, tk), lambda i, j, k: (i, k))
    b_spec = BS((tn, tk), lambda i, j, k: (j, k)) if tb else BS((tk, tn), lambda i, j, k: (k, j))
    in_specs, args = [a_spec, b_spec], [a, b]
    if add is not None:
        in_specs.append(BS((tm, tn), lambda i, j, k: (i, j)))
        args.append(add)
    return _pc(body_single if nk == 1 else body, name=name, grid=(M // tm, N // tn, nk), in_specs=in_specs,
               out_specs=BS((tm, tn), lambda i, j, k: (i, j)), out_shape=SDS((M, N), out_dtype),
               scratch_shapes=[] if nk == 1 else [pltpu.VMEM((tm, tn), F32)])(*args)


def _dot(a, b, dims):
    return lax.dot_general(a.astype(BF16), b.astype(BF16), (dims, ((), ())), preferred_element_type=F32)


@jax.custom_vjp
def _nn(a, b):
    return _dot(a, b, ((1,), (0,)))


@jax.custom_vjp
def _nt(a, b):
    return _dot(a, b, ((1,), (1,)))


@jax.custom_vjp
def _tn(a, b):
    return _dot(a, b, ((0,), (0,)))


_nn.defvjp(lambda a, b: (_nn(a, b), (a, b)), lambda r, g: (_nt(g, r[1]), _tn(r[0], g)))
_nt.defvjp(lambda a, b: (_nt(a, b), (a, b)), lambda r, g: (_nn(g, r[1]), _tn(g, r[0])))
_tn.defvjp(lambda a, b: (_tn(a, b), (a, b)), lambda r, g: (_nt(r[1], g), _nn(r[0], g)))


def _hdot(m, x):
    hi = x.astype(BF16)
    r1 = x - hi.astype(F32)
    lo = r1.astype(BF16)
    lo2 = (r1 - lo.astype(F32)).astype(BF16)
    n = x.shape[1]
    out = lax.dot_general(m.astype(BF16), jnp.concatenate([hi, lo, lo2], axis=1), (((1,), (0,)), ((), ())),
                          preferred_element_type=F32)
    return out[:, :n] + out[:, n:2 * n] + out[:, 2 * n:]


@jax.custom_vjp
def _cumdot(m, mt, x):
    return _hdot(m, x)


_cumdot.defvjp(lambda m, mt, x: (_hdot(m, x), (m, mt)),
               lambda r, g: (jnp.zeros_like(r[0]), jnp.zeros_like(r[1]), _hdot(r[1], g)))


def _sigmoid(x):
    return 1.0 / (1.0 + jnp.exp(-x))


def _softplus(x):
    return jnp.maximum(x, 0.0) + jnp.log(1.0 + jnp.exp(-jnp.abs(x)))


def _rms(x, w):
    return x * lax.rsqrt(jnp.mean(x * x, axis=-1, keepdims=True) + NORM_EPS) * w


def _rmsnorm_fwd(x2, w, *, name):
    T, D = x2.shape
    tr = _div_tile(T, 512, 8)

    def body(x_ref, w_ref, o_ref):
        o_ref[...] = _rms(x_ref[...], w_ref[...]).astype(BF16)

    return _pc(body, name=name, grid=(T // tr,),
               in_specs=[BS((tr, D), lambda i: (i, 0)), BS((1, D), lambda i: (0, 0))],
               out_specs=BS((tr, D), lambda i: (i, 0)), out_shape=SDS((T, D), BF16))(x2, w.reshape(1, D))


def _rmsnorm_bwd(x2, w, dh, resid, *, name):
    T, D = x2.shape
    tr = _div_tile(T, 512, 8)

    def body(x_ref, w_ref, dh_ref, r_ref, dx_ref, dw_ref):
        _, vjp = jax.vjp(_rms, x_ref[...], w_ref[...])
        dx, dw = vjp(dh_ref[...])
        dx_ref[...] = dx + r_ref[...]

        @pl.when(pl.program_id(0) == 0)
        def _():
            dw_ref[...] = jnp.zeros_like(dw_ref)

        dw_ref[...] += dw

    row = BS((tr, D), lambda i: (i, 0))
    one = BS((1, D), lambda i: (0, 0))
    return _pc(body, name=name, grid=(T // tr,), in_specs=[row, one, row, row], out_specs=[row, one],
               out_shape=[SDS((T, D), F32), SDS((1, D), F32)])(x2, w.reshape(1, D), dh, resid)


ROW_PAD = 8


def _pad_rows(x):
    return jnp.concatenate([x, jnp.zeros((ROW_PAD, x.shape[1]), x.dtype)], axis=0)


def _shift_rows(xp, s):
    n = xp.shape[0] - ROW_PAD
    return xp[:n] if s == 0 else pltpu.roll(xp, (-s) % xp.shape[0], 0)[:n]


def _conv_taps(x, taps):
    xp = _pad_rows(x)
    return [_shift_rows(xp, k - taps // 2) for k in range(taps)]


def _conv_pre(xs, w_ref, b_ref):
    c = b_ref[...] + w_ref[0:1, :] * xs[0]
    for k in range(1, len(xs)):
        c = c + w_ref[k:k + 1, :] * xs[k]
    return c


def _conv_fwd(x3, x_blk0, w, b, *, taps, ct, gate_blk0=None, out_dtype, name):
    B, L, _ = x3.shape
    C = w.shape[1]
    wp = jnp.zeros((8, C), F32).at[:taps].set(w)

    def body(*refs):
        if gate_blk0 is None:
            x_ref, w_ref, b_ref, o_ref = refs
        else:
            x_ref, u_ref, w_ref, b_ref, o_ref = refs
        c = _conv_pre(_conv_taps(x_ref[0], taps), w_ref, b_ref)
        y = c * _sigmoid(c)
        if gate_blk0 is not None:
            y = y * u_ref[0]
        o_ref[0] = y.astype(out_dtype)

    in_specs = [BS((1, L, ct), lambda bi, j: (bi, 0, x_blk0 + j))]
    args = [x3]
    if gate_blk0 is not None:
        in_specs.append(BS((1, L, ct), lambda bi, j: (bi, 0, gate_blk0 + j)))
        args.append(x3)
    in_specs += [BS((8, ct), lambda bi, j: (0, j)), BS((1, ct), lambda bi, j: (0, j))]
    args += [wp, b.reshape(1, C)]
    return _pc(body, name=name, grid=(B, C // ct), in_specs=in_specs,
               out_specs=BS((1, L, ct), lambda bi, j: (bi, 0, j)), out_shape=SDS((B, L, C), out_dtype))(*args)


def _conv_bwd(x3, x_blk0, w, b, dy3, *, taps, ct, gate_blk0=None, name):
    B, L, _ = x3.shape
    C = w.shape[1]
    wp = jnp.zeros((8, C), F32).at[:taps].set(w)
    gated = gate_blk0 is not None

    def body(*refs):
        if gated:
            x_ref, u_ref, w_ref, b_ref, dy_ref, dx_ref, du_ref, dw_ref, db_ref = refs
        else:
            x_ref, w_ref, b_ref, dy_ref, dx_ref, dw_ref, db_ref = refs
        xs = _conv_taps(x_ref[0], taps)
        dy = dy_ref[0].astype(F32)
        c = _conv_pre(xs, w_ref, b_ref)
        sg = _sigmoid(c)
        dsilu = sg * (1.0 + c * (1.0 - sg))
        if gated:
            du_ref[0] = (dy * (c * sg)).astype(BF16)
            dc = dy * u_ref[0] * dsilu
        else:
            dc = dy * dsilu
        dcp = _pad_rows(dc)
        dx = jnp.zeros_like(dc)
        dw_ref[0] = jnp.zeros((8, ct), F32)
        for k in range(taps):
            dx = dx + w_ref[k:k + 1, :] * _shift_rows(dcp, taps // 2 - k)
            dw_ref[0, k:k + 1, :] = jnp.sum(dc * xs[k], axis=0, keepdims=True)
        dx_ref[0] = dx.astype(BF16)
        db_ref[0] = jnp.sum(dc, axis=0, keepdims=True)

    xs = BS((1, L, ct), lambda bi, j: (bi, 0, x_blk0 + j))
    ys = BS((1, L, ct), lambda bi, j: (bi, 0, j))
    in_specs, args = [xs], [x3]
    if gated:
        in_specs.append(BS((1, L, ct), lambda bi, j: (bi, 0, gate_blk0 + j)))
        args.append(x3)
    in_specs += [BS((8, ct), lambda bi, j: (0, j)), BS((1, ct), lambda bi, j: (0, j)), ys]
    args += [wp, b.reshape(1, C), dy3]
    out_specs = [ys] + ([ys] if gated else []) + [BS((1, 8, ct), lambda bi, j: (bi, 0, j)), BS((1, 1, ct), lambda bi, j: (bi, 0, j))]
    out_shape = [SDS((B, L, C), BF16)] + ([SDS((B, L, C), BF16)] if gated else []) + [SDS((B, 8, C), F32), SDS((B, 1, C), F32)]
    return _pc(body, name=name, grid=(B, C // ct), in_specs=in_specs, out_specs=out_specs, out_shape=out_shape)(*args)


def _tri(reverse):
    r = lax.broadcasted_iota(jnp.int32, (CHUNK, CHUNK), 0)
    c = lax.broadcasted_iota(jnp.int32, (CHUNK, CHUNK), 1)
    return (c >= r) if reverse else (c <= r)


PAIRS = 2
QUADS = SSM_HEADS // (2 * PAIRS)
QW = PAIRS * LANE


def _ssd_chunk(h0, h1, x0, x1, bm, cm, dtc, alog, *, col0, reverse):
    mask = _tri(reverse)
    eye = lax.broadcasted_iota(jnp.int32, (CHUNK, CHUNK), 0) == lax.broadcasted_iota(jnp.int32, (CHUNK, CHUNK), 1)
    lane = lax.broadcasted_iota(jnp.int32, (1, LANE), 1)
    first = lane < HEAD_DIM
    adt = dtc * (-jnp.exp(alog))
    cumc = _cumdot(mask.astype(F32), _tri(not reverse).astype(F32), adt)
    totc = jnp.sum(adt, axis=0, keepdims=True)
    cb = _nt(cm, bm)

    def col(v, c):
        return jnp.sum(jnp.where(lane == c, v, 0.0), axis=1, keepdims=True)

    outs, states = [], []
    for p, (hprev, xs) in enumerate(((h0, x0), (h1, x1))):
        c0 = col0 + 2 * p
        cj = (col(cumc, c0), col(cumc, c0 + 1))
        cum = jnp.where(first, cj[0], cj[1])
        tot = jnp.where(first, col(totc, c0), col(totc, c0 + 1))
        xdt = xs * jnp.where(first, col(dtc, c0), col(dtc, c0 + 1))
        y = _nn(cm, hprev) * jnp.exp(cum)
        for j in range(2):
            rj = jnp.sum(jnp.where(eye, cj[j], 0.0), axis=0, keepdims=True)
            dec = jnp.exp(jnp.where(mask, cj[j] - rj, NEG))
            y = y + _nn(cb * dec, jnp.where(first if j == 0 else ~first, xdt, 0.0))
        outs.append(y)
        states.append(hprev * jnp.exp(tot) + _tn(bm, xdt * jnp.exp(tot - cum)))
    return outs[0], outs[1], states[0], states[1]


def _ssd_specs(B, L):
    def lanes(w, blk):
        return BS((1, L, w), blk)

    return [
        lanes(QW, lambda b, q: (b, 0, q)),
        lanes(LANE, lambda b, q: (b, 0, 8 + q // 2)),
        lanes(LANE, lambda b, q: (b, 0, 10 + q // 2)),
        lanes(LANE, lambda b, q: (b, 0, 0)),
        BS((1, LANE), lambda b, q: (0, 0)),
        BS((1, QW), lambda b, q: (0, q)),
    ]


def _ssd_slot(d, ci):
    return ci if d == 0 else ci + 1


def _ssd_fwd(xbc_act, dtc, alog, dskip, *, name, rider=None):
    B, L, _ = xbc_act.shape
    nc = L // CHUNK

    def body(xs_ref, b_ref, c_ref, dt_ref, alog_ref, dsk_ref, y_ref, hs_ref):
        q = pl.program_id(1)
        alog_v = alog_ref[...]
        y_ref[0] = dsk_ref[...] * xs_ref[0]
        hs_ref[0, 0, 0, 0] = jnp.zeros((LANE, QW), F32)
        hs_ref[0, 0, 1, nc] = jnp.zeros((LANE, QW), F32)

        def step(i, carry):
            cis = (i, nc - 1 - i)
            rows = [pl.ds(pl.multiple_of(ci * CHUNK, CHUNK), CHUNK) for ci in cis]
            res = []
            for d in range(2):
                cur = _ssd_slot(d, cis[d])
                res.append(_ssd_chunk(
                    hs_ref[0, 0, d, cur, :, :LANE], hs_ref[0, 0, d, cur, :, LANE:], xs_ref[0, rows[d], :LANE],
                    xs_ref[0, rows[d], LANE:], b_ref[0, rows[d], :], c_ref[0, rows[d], :], dt_ref[0, rows[d], :], alog_v,
                    col0=SSM_HEADS * d + 2 * PAIRS * q, reverse=d == 1))
            for d in range(2):
                y0, y1, n0, n1 = res[d]
                nxt = _ssd_slot(d, cis[d] + 1 if d == 0 else cis[d] - 1)
                hs_ref[0, 0, d, nxt, :, :LANE] = n0
                hs_ref[0, 0, d, nxt, :, LANE:] = n1
                y_ref[0, rows[d], :LANE] += y0
                y_ref[0, rows[d], LANE:] += y1
            return carry

        lax.fori_loop(0, nc, step, 0, unroll=2)

    (y, hs), carried = _pc_carry(
        body, (xbc_act, xbc_act, xbc_act, dtc, alog, dskip), name=name, grid=(B, QUADS), in_specs=_ssd_specs(B, L),
        out_specs=[BS((1, L, QW), lambda b, q: (b, 0, q)), BS((1, 1, 2, nc + 1, LANE, QW), lambda b, q: (b, q, 0, 0, 0, 0))],
        out_shape=[SDS((B, L, SSM_WIDTH), F32), SDS((B, QUADS, 2, nc + 1, LANE, QW), F32)], rider=rider)
    return y, hs, carried


def _ssd_bwd(xbc_act, dtc, alog, dskip, hs, dy, *, name, rider=None):
    B, L, _ = xbc_act.shape
    nc = L // CHUNK

    def body(xs_ref, b_ref, c_ref, dt_ref, alog_ref, dsk_ref, hs_ref, dy_ref,
             dxs_ref, db_ref, dc_ref, ddt_ref, dalog_ref, ddsk_ref, dh_ref):
        q = pl.program_id(1)
        alog_v = alog_ref[...]

        @pl.when(q % 2 == 0)
        def _():
            db_ref[...] = jnp.zeros_like(db_ref)
            dc_ref[...] = jnp.zeros_like(dc_ref)

        @pl.when(q == 0)
        def _():
            ddt_ref[...] = jnp.zeros_like(ddt_ref)

        dxs_ref[0] = dy_ref[0] * dsk_ref[...]
        ddsk_ref[0] = jnp.sum(dy_ref[0] * xs_ref[0], axis=0, keepdims=True)
        dh_ref[...] = jnp.zeros_like(dh_ref)

        def step(i, carry):
            g_alog = carry
            cis = (nc - 1 - i, i)
            rows = [pl.ds(pl.multiple_of(ci * CHUNK, CHUNK), CHUNK) for ci in cis]
            res = []
            for d in range(2):
                cur = _ssd_slot(d, cis[d])
                fn = functools.partial(_ssd_chunk, col0=SSM_HEADS * d + 2 * PAIRS * q, reverse=d == 1)
                _, vjp = jax.vjp(fn, hs_ref[0, 0, d, cur, :, :LANE], hs_ref[0, 0, d, cur, :, LANE:], xs_ref[0, rows[d], :LANE],
                                 xs_ref[0, rows[d], LANE:], b_ref[0, rows[d], :], c_ref[0, rows[d], :], dt_ref[0, rows[d], :],
                                 alog_v)
                res.append(vjp((dy_ref[0, rows[d], :LANE], dy_ref[0, rows[d], LANE:], dh_ref[d, :, :LANE], dh_ref[d, :, LANE:])))
            for d in range(2):
                g_h0, g_h1, g_x0, g_x1, g_b, g_c, g_dt, g_alog1 = res[d]
                dh_ref[d, :, :LANE] = g_h0
                dh_ref[d, :, LANE:] = g_h1
                dxs_ref[0, rows[d], :LANE] += g_x0
                dxs_ref[0, rows[d], LANE:] += g_x1
                db_ref[0, rows[d], :] += g_b
                dc_ref[0, rows[d], :] += g_c
                ddt_ref[0, rows[d], :] += g_dt
                g_alog = g_alog + g_alog1
            return g_alog

        dalog_ref[0, 0] = lax.fori_loop(0, nc, step, jnp.zeros((1, LANE), F32))

    lanes = lambda w, blk: BS((1, L, w), blk)
    in_specs = _ssd_specs(B, L) + [BS((1, 1, 2, nc + 1, LANE, QW), lambda b, q: (b, q, 0, 0, 0, 0)), lanes(QW, lambda b, q: (b, 0, q))]
    out_specs = [lanes(QW, lambda b, q: (b, 0, q)), lanes(LANE, lambda b, q: (b, 0, q // 2)), lanes(LANE, lambda b, q: (b, 0, q // 2)),
                 lanes(LANE, lambda b, q: (b, 0, 0)), BS((1, 1, 1, LANE), lambda b, q: (b, q, 0, 0)),
                 BS((1, 1, QW), lambda b, q: (b, 0, q))]
    out_shape = [SDS((B, L, SSM_WIDTH), F32), SDS((B, L, BC_WIDTH), F32), SDS((B, L, BC_WIDTH), F32), SDS((B, L, LANE), F32),
                 SDS((B, QUADS, 1, LANE), F32), SDS((B, 1, SSM_WIDTH), F32)]
    outs, carried = _pc_carry(body, (xbc_act, xbc_act, xbc_act, dtc, alog, dskip, hs, dy), name=name, grid=(B, QUADS),
                              in_specs=in_specs, out_specs=out_specs, out_shape=out_shape,
                              scratch_shapes=[pltpu.VMEM((2, LANE, QW), F32)], rider=rider)
    return (*outs, carried)


def _gate_norm(yp, z, w):
    v = yp * (z * _sigmoid(z))
    return v * lax.rsqrt(jnp.mean(v * v, axis=-1, keepdims=True) + NORM_EPS) * w


def _gate_fwd(ypre2, proj2, w, *, name):
    T = ypre2.shape[0]
    tr = _div_tile(T, 512, 8)
    G = 512

    def body(y_ref, z_ref, w_ref, o_ref):
        o_ref[...] = _gate_norm(y_ref[...], z_ref[...], w_ref[...]).astype(BF16)

    return _pc(body, name=name, grid=(T // tr, 2),
               in_specs=[BS((tr, G), lambda i, g: (i, g)), BS((tr, G), lambda i, g: (i, 2 + g)), BS((1, G), lambda i, g: (0, g))],
               out_specs=BS((tr, G), lambda i, g: (i, g)), out_shape=SDS((T, SSM_WIDTH), BF16))(ypre2, proj2, w.reshape(1, -1))


def _gate_bwd(ypre2, proj2, w, dy, *, name):
    T = ypre2.shape[0]
    tr = _div_tile(T, 512, 8)
    G = 512

    def body(y_ref, z_ref, w_ref, dy_ref, dyp_ref, dz_ref, dw_ref):
        _, vjp = jax.vjp(_gate_norm, y_ref[...], z_ref[...], w_ref[...])
        dyp, dz, dw = vjp(dy_ref[...])
        dyp_ref[...] = dyp
        dz_ref[...] = dz.astype(BF16)
        dw_ref[0] = dw

    tile = BS((tr, G), lambda i, g: (i, g))
    return _pc(body, name=name, grid=(T // tr, 2),
               in_specs=[tile, BS((tr, G), lambda i, g: (i, 2 + g)), BS((1, G), lambda i, g: (0, g)), tile],
               out_specs=[tile, tile, BS((1, 1, G), lambda i, g: (i, 0, g))],
               out_shape=[SDS((T, SSM_WIDTH), F32), SDS((T, SSM_WIDTH), BF16), SDS((T // tr, 1, SSM_WIDTH), F32)])(
        ypre2, proj2, w.reshape(1, -1), dy)


def _first_half():
    return lax.broadcasted_iota(jnp.int32, (1, LANE), 1) < HEAD_DIM


def _dup_kv_head(pair, odd):
    rolled = pltpu.roll(pair, HEAD_DIM, 1)
    return jnp.where(_first_half(), rolled, pair) if odd else jnp.where(_first_half(), pair, rolled)


def _stack_heads(quad):
    first = _first_half()
    lo, hi = quad[:, :LANE], quad[:, LANE:]
    return jnp.concatenate([jnp.where(first, lo, 0.0), jnp.where(first, 0.0, lo), jnp.where(first, hi, 0.0),
                            jnp.where(first, 0.0, hi)], axis=0)


def _unstack_heads(o):
    first = _first_half()
    return jnp.concatenate([jnp.where(first, o[:BLOCK], o[BLOCK:2 * BLOCK]), jnp.where(first, o[2 * BLOCK:3 * BLOCK], o[3 * BLOCK:])], axis=1)


def _fold_kv_head(d, odd):
    tot = d + pltpu.roll(d, HEAD_DIM, 1)
    return jnp.where(_first_half(), 0.0, tot) if odd else jnp.where(_first_half(), tot, 0.0)


def _attn_softmax(s, sink):
    m = jnp.maximum(jnp.max(s, axis=-1, keepdims=True), sink)
    p = jnp.exp(s - m)
    ps = jnp.exp(sink - m)
    inv = 1.0 / (jnp.sum(p, axis=-1, keepdims=True) + ps)
    return p * inv, ps * inv


def _attn_colneg(n, L):
    kpos = n * BLOCK - WINDOW + lax.broadcasted_iota(jnp.int32, (1, KEY_SPAN), 1)
    return jnp.where((kpos >= 0) & (kpos < L), 0.0, NEG)


def _attn_in_specs(L):
    nblk = L // BLOCK
    kv = lambda o, col: BS((1, BLOCK, 4 * HEAD_DIM), lambda b, n: (b, jnp.clip(n + o, 0, nblk - 1), col))
    kcol, vcol = 3584 // 256, 3840 // 256
    return [BS((1, BLOCK, ATTN_HEADS * HEAD_DIM), lambda b, n: (b, n, 0)), kv(-1, kcol), kv(0, kcol), kv(1, kcol),
            kv(-1, vcol), kv(0, vcol), kv(1, vcol),
            BS((ATTN_HEADS, BLOCK, KEY_SPAN), lambda b, n: (0, 0, 0)), BS((ATTN_HEADS * BLOCK, 1), lambda b, n: (0, 0))]


def _attn_fwd(proj, bias, sinkcol, *, name, rider=None):
    B, L, _ = proj.shape

    def body(q_ref, k0, k1, k2, v0, v1, v2, bias_ref, sink_ref, o_ref):
        colneg = _attn_colneg(pl.program_id(1), L)
        kcat = jnp.concatenate([k0[0], k1[0], k2[0]], axis=0)
        vcat = jnp.concatenate([v0[0], v1[0], v2[0]], axis=0)
        scores, probs = [], []
        for g in range(KV_HEADS):
            pair = slice(LANE * (g // 2), LANE * (g // 2) + LANE)
            quad = slice(4 * HEAD_DIM * g, 4 * HEAD_DIM * (g + 1))
            kd = _dup_kv_head(kcat[:, pair], g % 2).astype(BF16)
            qs = (_stack_heads(q_ref[0, :, quad]) * HEAD_DIM ** -0.5).astype(BF16)
            scores.append(lax.dot_general(qs, kd, (((1,), (1,)), ((), ())), preferred_element_type=F32))
        for g in range(KV_HEADS):
            pn, _ = _attn_softmax(scores[g] + bias_ref[4 * g:4 * g + 4].reshape(4 * BLOCK, KEY_SPAN) + colneg,
                                  sink_ref[4 * BLOCK * g:4 * BLOCK * (g + 1)])
            probs.append(pn.astype(BF16))
        for g in range(KV_HEADS):
            pair = slice(LANE * (g // 2), LANE * (g // 2) + LANE)
            quad = slice(4 * HEAD_DIM * g, 4 * HEAD_DIM * (g + 1))
            vd = _dup_kv_head(vcat[:, pair], g % 2).astype(BF16)
            o = lax.dot_general(probs[g], vd, (((1,), (0,)), ((), ())), preferred_element_type=F32)
            o_ref[0, :, quad] = _unstack_heads(o).astype(BF16)

    (out,), carried = _pc_carry(body, (proj, proj, proj, proj, proj, proj, proj, bias, sinkcol), name=name, grid=(B, L // BLOCK),
                                in_specs=_attn_in_specs(L),
                                out_specs=[BS((1, BLOCK, ATTN_HEADS * HEAD_DIM), lambda b, n: (b, n, 0))],
                                out_shape=[SDS((B, L, ATTN_HEADS * HEAD_DIM), BF16)], rider=rider)
    return out, carried


def _attn_bwd(proj, bias, sinkcol, dout, *, name, rider=None):
    B, L, _ = proj.shape
    nblk = L // BLOCK
    nn, nt, tn = (((1,), (0,)), ((), ())), (((1,), (1,)), ((), ())), (((0,), (0,)), ((), ()))

    def body(q_ref, k0, k1, k2, v0, v1, v2, bias_ref, sink_ref, do_ref, dq_ref, dk_ref, dv_ref, dbias_ref, dsink_ref):
        b, n = pl.program_id(0), pl.program_id(1)

        @pl.when(n == 0)
        def _():
            dk_ref[...] = jnp.zeros_like(dk_ref)
            dv_ref[...] = jnp.zeros_like(dv_ref)

        @pl.when((n == 0) & (b == 0))
        def _():
            dbias_ref[...] = jnp.zeros_like(dbias_ref)
            dsink_ref[...] = jnp.zeros_like(dsink_ref)

        colneg = _attn_colneg(n, L)
        kcat = jnp.concatenate([k0[0], k1[0], k2[0]], axis=0)
        vcat = jnp.concatenate([v0[0], v1[0], v2[0]], axis=0)
        krows = [pl.ds(pl.multiple_of(jnp.clip(n + o, 0, nblk - 1) * BLOCK, BLOCK), BLOCK) for o in (-1, 0, 1)]
        ops, mids = [], []
        for g in range(KV_HEADS):
            pair = slice(LANE * (g // 2), LANE * (g // 2) + LANE)
            quad = slice(4 * HEAD_DIM * g, 4 * HEAD_DIM * (g + 1))
            kd = _dup_kv_head(kcat[:, pair], g % 2).astype(BF16)
            vd = _dup_kv_head(vcat[:, pair], g % 2).astype(BF16)
            qs = (_stack_heads(q_ref[0, :, quad]) * HEAD_DIM ** -0.5).astype(BF16)
            dos = _stack_heads(do_ref[0, :, quad].astype(F32)).astype(BF16)
            ops.append((kd, qs, dos, lax.dot_general(qs, kd, nt, preferred_element_type=F32),
                        lax.dot_general(dos, vd, nt, preferred_element_type=F32)))
        for g in range(KV_HEADS):
            rows = slice(4 * BLOCK * g, 4 * BLOCK * (g + 1))
            _, _, _, s, dpn = ops[g]
            pn, psink = _attn_softmax(s + bias_ref[4 * g:4 * g + 4].reshape(4 * BLOCK, KEY_SPAN) + colneg, sink_ref[rows])
            r = jnp.sum(dpn * pn, axis=-1, keepdims=True)
            ds = pn * (dpn - r)
            dbias_ref[4 * g:4 * g + 4] += ds.reshape(4, BLOCK, KEY_SPAN)
            dsink_ref[rows] += -psink * r
            mids.append((pn.astype(BF16), ds.astype(BF16)))
        for g in range(KV_HEADS):
            pair = slice(LANE * (g // 2), LANE * (g // 2) + LANE)
            quad = slice(4 * HEAD_DIM * g, 4 * HEAD_DIM * (g + 1))
            kd, qs, dos, _, _ = ops[g]
            pnb, dsb = mids[g]
            dvd = lax.dot_general(pnb, dos, tn, preferred_element_type=F32)
            dkd = lax.dot_general(dsb, qs, tn, preferred_element_type=F32)
            dqs = lax.dot_general(dsb, kd, nn, preferred_element_type=F32) * HEAD_DIM ** -0.5
            dq_ref[0, :, quad] = _unstack_heads(dqs).astype(BF16)
            dk_g, dv_g = _fold_kv_head(dkd, g % 2), _fold_kv_head(dvd, g % 2)
            for o in range(3):
                dk_ref[0, krows[o], pair] += dk_g[o * BLOCK:(o + 1) * BLOCK]
                dv_ref[0, krows[o], pair] += dv_g[o * BLOCK:(o + 1) * BLOCK]

    qspec = BS((1, BLOCK, ATTN_HEADS * HEAD_DIM), lambda b, n: (b, n, 0))
    kvout = BS((1, L, 4 * HEAD_DIM), lambda b, n: (b, 0, 0))
    outs, carried = _pc_carry(
        body, (proj, proj, proj, proj, proj, proj, proj, bias, sinkcol, dout), name=name, grid=(B, nblk),
        in_specs=_attn_in_specs(L) + [qspec],
        out_specs=[qspec, kvout, kvout, BS((ATTN_HEADS, BLOCK, KEY_SPAN), lambda b, n: (0, 0, 0)),
                   BS((ATTN_HEADS * BLOCK, 1), lambda b, n: (0, 0))],
        out_shape=[SDS((B, L, ATTN_HEADS * HEAD_DIM), BF16), SDS((B, L, 4 * HEAD_DIM), F32), SDS((B, L, 4 * HEAD_DIM), F32),
                   SDS((ATTN_HEADS, BLOCK, KEY_SPAN), F32), SDS((ATTN_HEADS * BLOCK, 1), F32)], rider=rider)
    return (*outs, carried)


def _t5_bucket(rel):
    half = REL_BUCKETS // 2
    max_exact = half // 2
    ret = jnp.where(rel > 0, half, 0)
    n = jnp.abs(rel)
    nf = jnp.maximum(n, 1).astype(F32)
    large = max_exact + (jnp.log(nf / max_exact) / math.log(REL_MAX_DIST / max_exact) * (half - max_exact)).astype(jnp.int32)
    large = jnp.minimum(large, half - 1)
    return ret + jnp.where(n < max_exact, n, large)


def _bucket_table():
    rel = jnp.arange(KEY_SPAN)[None, :] - WINDOW - jnp.arange(BLOCK)[:, None]
    return _t5_bucket(rel).astype(jnp.int32)


def _bias_expand(rel_bias, bucket, *, name):
    rbt = jnp.zeros((ATTN_HEADS, 1, LANE), F32).at[:, 0, :REL_BUCKETS].set(rel_bias.T)

    def body(rb_ref, bk_ref, o_ref):
        lane = lax.broadcasted_iota(jnp.int32, (1, LANE), 1)
        row = rb_ref[0]
        bk = bk_ref[...]
        acc = jnp.zeros((BLOCK, KEY_SPAN), F32)
        for r in range(REL_BUCKETS):
            val = jnp.sum(jnp.where(lane == r, row, 0.0), axis=1, keepdims=True)
            acc = jnp.where(bk == r, val, acc)
        rel = (lax.broadcasted_iota(jnp.int32, (BLOCK, KEY_SPAN), 1) - WINDOW
               - lax.broadcasted_iota(jnp.int32, (BLOCK, KEY_SPAN), 0))
        o_ref[0] = jnp.where(jnp.abs(rel) <= WINDOW, acc, NEG)

    return _pc(body, name=name, grid=(ATTN_HEADS,),
               in_specs=[BS((1, 1, LANE), lambda h: (h, 0, 0)), BS((BLOCK, KEY_SPAN), lambda h: (0, 0))],
               out_specs=BS((1, BLOCK, KEY_SPAN), lambda h: (h, 0, 0)), out_shape=SDS((ATTN_HEADS, BLOCK, KEY_SPAN), F32))(rbt, bucket)


def _bias_reduce(dbias, bucket, *, name):
    def body(db_ref, bk_ref, o_ref):
        lane = lax.broadcasted_iota(jnp.int32, (1, LANE), 1)
        x = db_ref[0]
        bk = bk_ref[...]
        acc = jnp.zeros((1, LANE), F32)
        for r in range(REL_BUCKETS):
            part = jnp.sum(jnp.where(bk == r, x, 0.0), axis=1, keepdims=True)
            acc = jnp.where(lane == r, jnp.sum(part, axis=0, keepdims=True), acc)
        o_ref[0] = acc

    out = _pc(body, name=name, grid=(ATTN_HEADS,),
              in_specs=[BS((1, BLOCK, KEY_SPAN), lambda h: (h, 0, 0)), BS((BLOCK, KEY_SPAN), lambda h: (0, 0))],
              out_specs=BS((1, 1, LANE), lambda h: (h, 0, 0)), out_shape=SDS((ATTN_HEADS, 1, LANE), F32))(dbias, bucket)
    return out[:, 0, :REL_BUCKETS].T


def _loss_head(x2, w, target, *, name):
    T, D = x2.shape
    tr = _div_tile(T, 512, 8)

    def tile_loss(x, w, t):
        err = _rms(x, w) - t
        return 0.5 * jnp.sum(jnp.mean(err * err, axis=-1, keepdims=True), axis=0, keepdims=True)

    def body(x_ref, w_ref, t_ref, loss_ref, dx_ref, dw_ref):
        t = t_ref[...]
        l, vjp = jax.vjp(lambda x, w: tile_loss(x, w, t), x_ref[...], w_ref[...])
        dx, dw = vjp(jnp.ones((1, 1), F32))
        dx_ref[...] = dx

        @pl.when(pl.program_id(0) == 0)
        def _():
            dw_ref[...] = jnp.zeros_like(dw_ref)
            loss_ref[...] = jnp.zeros_like(loss_ref)

        dw_ref[...] += dw
        loss_ref[...] += l + jnp.zeros((1, LANE), F32)

    row = BS((tr, D), lambda i: (i, 0))
    one = BS((1, D), lambda i: (0, 0))
    return _pc(body, name=name, grid=(T // tr,), in_specs=[row, one, row],
               out_specs=[BS((1, LANE), lambda i: (0, 0)), row, one],
               out_shape=[SDS((1, LANE), F32), SDS((T, D), F32), SDS((1, D), F32)])(x2, w.reshape(1, D), target)


def _adamw(w2, g2, m2, v2, *, name):
    R, C = w2.shape
    tr = _div_tile(R, 256, 8)
    c1 = 1.0 - ADAM_B1 ** ADAM_STEP
    c2 = 1.0 - ADAM_B2 ** ADAM_STEP

    def body(w_ref, g_ref, m_ref, v_ref, d_ref, nm_ref, nv_ref):
        g = g_ref[...]
        m = ADAM_B1 * m_ref[...] + (1.0 - ADAM_B1) * g
        v = ADAM_B2 * v_ref[...] + (1.0 - ADAM_B2) * (g * g)
        d_ref[...] = -ADAM_LR * ((m / c1) / (jnp.sqrt(v / c2) + ADAM_EPS) + ADAM_WD * w_ref[...])
        nm_ref[...] = m
        nv_ref[...] = v

    t = BS((tr, C), lambda i: (i, 0))
    return _pc(body, name=name, grid=(R // tr,), in_specs=[t, t, t, t], out_specs=[t, t, t],
               out_shape=[SDS((R, C), F32)] * 3)(w2, g2, m2, v2)


def _place():
    return lax.axis_index("x"), lax.axis_index("y"), lax.axis_index("c")


def _gather_rider(shards):
    na = len(shards)

    def copies(ins, outs, sems):
        send_sems, recv_sems = sems
        x, y, c = _place()
        for a in range(na):
            for k, (px, py) in enumerate([(1 - x, y), (x, 1 - y), (1 - x, 1 - y)]):
                send = functools.partial(pltpu.make_async_remote_copy, ins[a], outs[a].at[2 * x + y], send_sems.at[a, k],
                                         recv_sems.at[a, k], device_id=(px, py, c), device_id_type=MESH)
                got = outs[a].at[2 * px + py]
                arrived = functools.partial(pltpu.make_async_remote_copy, got, got, send_sems.at[a, k], recv_sems.at[a, k],
                                            device_id=(px, py, c), device_id_type=MESH)
                yield send, arrived

    def start(ins, outs, sems):
        for send, _ in copies(ins, outs, sems):
            send().start()

    def finish(ins, outs, sems):
        both = list(copies(ins, outs, sems))
        for _, arrived in both:
            arrived().wait_recv()
        for send, _ in both:
            send().wait_send()

    return dict(ins=list(shards), prev=[], out_shape=[SDS((N_CHIP,) + s.shape, s.dtype) for s in shards],
                scratch=[pltpu.SemaphoreType.DMA((na, 3)), pltpu.SemaphoreType.DMA((na, 3))], start=start, finish=finish)


def _scatter_rider(bufs, layer, prev):
    na = len(bufs)
    h = layer // (DEPTH // 2)

    def copies(ins, outs, sems):
        send_sems, recv_sems, local_sems = sems
        x, y, c = _place()
        me = 4 * x + 2 * y + c
        for a in range(na):
            for j in range(N_CHIP):
                is_self = ((2 * x + y) == j) & (c == h)
                local = functools.partial(pltpu.make_async_copy, ins[a].at[j], outs[a].at[me], local_sems.at[a])
                remote = functools.partial(pltpu.make_async_remote_copy, ins[a].at[j], outs[a].at[me], send_sems.at[a, j],
                                           recv_sems.at[a, me], device_id=(j // 2, j % 2, h), device_id_type=MESH)
                yield is_self, local, remote

    def start(ins, outs, sems):
        for is_self, local, remote in copies(ins, outs, sems):
            pl.when(is_self)(lambda: local().start())
            pl.when(jnp.logical_not(is_self))(lambda: remote().start())

    def finish(ins, outs, sems):
        _, recv_sems, _ = sems
        x, y, c = _place()
        me = 4 * x + 2 * y + c
        for a in range(na):
            for s in range(N_DEV):
                got = outs[a].at[s]
                arrived = functools.partial(pltpu.make_async_remote_copy, got, got, recv_sems.at[a, s], recv_sems.at[a, s],
                                            device_id=(s // 4, (s // 2) % 2, s % 2), device_id_type=MESH)
                pl.when((c == h) & (me != s))(lambda: arrived().wait_recv())
        for is_self, local, remote in copies(ins, outs, sems):
            pl.when(is_self)(lambda: local().wait())
            pl.when(jnp.logical_not(is_self))(lambda: remote().wait_send())

    return dict(ins=list(bufs), prev=list(prev), out_shape=[SDS((N_DEV,) + b.shape[1:], b.dtype) for b in bufs],
                scratch=[pltpu.SemaphoreType.DMA((na, N_CHIP)), pltpu.SemaphoreType.DMA((na, N_DEV)), pltpu.SemaphoreType.DMA((na,))],
                start=start, finish=finish)


def _sum_sources(parts, *, name):
    _, R, C = parts.shape
    tr = _div_tile(R, 256, 16)

    def body(p_ref, o_ref):
        acc = p_ref[0].astype(F32)
        for s in range(1, N_DEV):
            acc = acc + p_ref[s].astype(F32)
        o_ref[...] = acc

    return _pc(body, name=name, grid=(R // tr,), in_specs=[BS((N_DEV, tr, C), lambda i: (0, i, 0))],
               out_specs=BS((tr, C), lambda i: (i, 0)), out_shape=SDS((R, C), F32))(parts)


def _join_halves(halves, *, name):
    na = len(halves)

    def body(*refs):
        ins, outs = refs[:na], refs[na:2 * na]
        send_sems, recv_sems = refs[2 * na:]
        x, y, c = _place()
        cps = []
        for a in range(na):
            cp = pltpu.make_async_remote_copy(ins[a], outs[a], send_sems.at[a], recv_sems.at[a],
                                              device_id=(x, y, 1 - c), device_id_type=MESH)
            cp.start()
            cps.append(cp)
        for cp in cps:
            cp.wait_recv()
        for cp in cps:
            cp.wait_send()

    any_spec = BS(memory_space=pl.ANY)
    return pl.pallas_call(
        body, name=name, in_specs=[any_spec] * na, out_specs=[any_spec] * na,
        out_shape=[SDS(h.shape, h.dtype) for h in halves],
        scratch_shapes=[pltpu.SemaphoreType.DMA((na,)), pltpu.SemaphoreType.DMA((na,))],
        compiler_params=pltpu.CompilerParams(has_side_effects=True))(*halves)


def _allreduce_small(vec, *, name):
    R = vec.shape[0]

    def body(v_ref, o_ref, all_ref, send_sems, recv_sems):
        x, y, c = _place()
        me = 4 * x + 2 * y + c
        all_ref[me] = v_ref[...]
        sends = []
        for r in range(1, N_DEV):
            tgt = (x ^ (r >> 2), y ^ ((r >> 1) & 1), c ^ (r & 1))
            cp = pltpu.make_async_remote_copy(v_ref, all_ref.at[me], send_sems.at[r - 1], recv_sems.at[r - 1],
                                              device_id=tgt, device_id_type=MESH)
            cp.start()
            sends.append(cp)
        for r in range(1, N_DEV):
            tx, ty, tc = x ^ (r >> 2), y ^ ((r >> 1) & 1), c ^ (r & 1)
            got = all_ref.at[4 * tx + 2 * ty + tc]
            pltpu.make_async_remote_copy(got, got, send_sems.at[r - 1], recv_sems.at[r - 1],
                                         device_id=(tx, ty, tc), device_id_type=MESH).wait_recv()
        for cp in sends:
            cp.wait_send()
        acc = all_ref[0]
        for s in range(1, N_DEV):
            acc = acc + all_ref[s]
        o_ref[...] = acc

    vm = BS(memory_space=pltpu.VMEM)
    return pl.pallas_call(
        body, name=name, in_specs=[vm], out_specs=vm, out_shape=SDS((R, LANE), F32),
        scratch_shapes=[pltpu.VMEM((N_DEV, R, LANE), F32), pltpu.SemaphoreType.DMA((N_DEV - 1,)), pltpu.SemaphoreType.DMA((N_DEV - 1,))],
        compiler_params=pltpu.CompilerParams(has_side_effects=True, vmem_limit_bytes=VMEM_LIMIT_BYTES))(vec)


def _pack(arrs):
    rows = []
    for a in arrs:
        f = a.reshape(-1).astype(F32)
        n = -(-f.shape[0] // LANE) * LANE
        rows.append(jnp.pad(f, (0, n - f.shape[0])).reshape(-1, LANE))
    v = jnp.concatenate(rows, axis=0)
    pad = -v.shape[0] % 8
    return jnp.pad(v, ((0, pad), (0, 0)))


def _unpack(v, shapes):
    out, r = [], 0
    for s in shapes:
        n = int(np.prod(s)) if len(s) else 1
        nr = -(-n // LANE)
        out.append(v[r:r + nr].reshape(-1)[:n].reshape(s))
        r += nr
    return out


def _perm_in_cols(w_full):
    z, xbc, dt, q, k, v = (w_full[..., :Z_END], w_full[..., Z_END:XBC_END], w_full[..., XBC_END:DT_END],
                           w_full[..., DT_END:Q_END], w_full[..., Q_END:K_END], w_full[..., K_END:])
    pad = jnp.zeros(dt.shape[:-1] + (LANE - dt.shape[-1],), dt.dtype)
    return jnp.concatenate([q, z, xbc, k, v, dt, pad], axis=-1)


def _unperm_in_cols(g):
    q, z, xbc, k, v, dt = (g[..., :1024], g[..., 1024:2048], g[..., 2048:3584], g[..., 3584:3840], g[..., 3840:4096],
                           g[..., 4096:4096 + 2 * SSM_HEADS])
    return jnp.concatenate([z, xbc, dt, q, k, v], axis=-1)


def _dt_cols(a):
    return jnp.pad(a.reshape(1, 2 * SSM_HEADS), ((0, 0), (0, LANE - 2 * SSM_HEADS)))


def _dt_fwd(proj, dtb, *, name):
    B, L, _ = proj.shape

    def body(p_ref, b_ref, o_ref):
        o_ref[0] = _softplus(p_ref[0] + b_ref[...])

    return _pc(body, name=name, grid=(B,),
               in_specs=[BS((1, L, LANE), lambda b: (b, 0, P_COLS // LANE - 1)), BS((1, LANE), lambda b: (0, 0))],
               out_specs=BS((1, L, LANE), lambda b: (b, 0, 0)), out_shape=SDS((B, L, LANE), F32))(proj, dtb)


def _dt_bwd(proj, dtb, ddt, *, name):
    B, L, _ = proj.shape

    def body(p_ref, b_ref, g_ref, o_ref, db_ref):
        g = g_ref[0] * _sigmoid(p_ref[0] + b_ref[...])
        o_ref[0] = g.astype(BF16)
        db_ref[0] = jnp.sum(g, axis=0, keepdims=True)

    row = BS((1, L, LANE), lambda b: (b, 0, 0))
    return _pc(body, name=name, grid=(B,),
               in_specs=[BS((1, L, LANE), lambda b: (b, 0, P_COLS // LANE - 1)), BS((1, LANE), lambda b: (0, 0)), row],
               out_specs=[row, BS((1, 1, LANE), lambda b: (b, 0, 0))],
               out_shape=[SDS((B, L, LANE), BF16), SDS((B, 1, LANE), F32)])(proj, dtb, ddt)


def _layer_fwd(i, x, wts, small, band_bias, riders=None, arrived=None):
    riders = riders or {}
    B, L, D = x.shape
    T = B * L
    x2 = x.reshape(T, D)
    h = _rmsnorm_fwd(x2, small["norm1_w"][i], name=f"norm1_{i}")
    proj2 = _mm(h, wts["w_in"], name=f"in_proj_{i}", tn=1408)
    proj = proj2.reshape(B, L, P_COLS)
    xbc_act = _conv_fwd(proj, 2048 // 256, small["conv_w"][i], small["conv_b"][i], taps=SSM_CONV, ct=256,
                        out_dtype=F32, name=f"ssm_conv_{i}")
    dtb, alog = _dt_cols(small["dt_bias"][i]), _dt_cols(small["a_log"][i])
    dskip = jnp.repeat(small["d_skip"][i], HEAD_DIM).reshape(1, SSM_WIDTH)
    dtc = _dt_fwd(proj, dtb, name=f"dt_{i}")
    ypre, hs, carried = _ssd_fwd(xbc_act, dtc, alog, dskip, name=f"ssd_{i}", rider=riders.get("ssd"))
    if carried is not None:
        arrived("ssd", carried)
    y_ssm = _gate_fwd(ypre.reshape(T, SSM_WIDTH), proj2, small["ssm_norm_w"][i], name=f"gate_{i}")
    sinkcol = jnp.repeat(small["attn_sink"][i], BLOCK).reshape(ATTN_HEADS * BLOCK, 1)
    y_attn, carried = _attn_fwd(proj, band_bias, sinkcol, name=f"attn_{i}", rider=riders.get("attn"))
    if carried is not None:
        arrived("attn", carried)
    y_attn = y_attn.reshape(T, D)
    w_out = wts["w_out"]
    x_mid = _mm(y_ssm, w_out[:SSM_WIDTH], add=x2, name=f"out_proj_a_{i}")
    x_mid = _mm(y_attn, w_out[SSM_WIDTH:], add=x_mid, name=f"out_proj_b_{i}")
    h2 = _rmsnorm_fwd(x_mid, small["norm2_w"][i], name=f"norm2_{i}")
    gu2 = _mm(h2, wts["w_up"], name=f"up_proj_{i}", tn=1408)
    gu = gu2.reshape(B, L, 2 * D_FF)
    act = _conv_fwd(gu, 0, small["ffn_conv_w"][i], small["ffn_conv_b"][i], taps=FFN_CONV, ct=256, gate_blk0=D_FF // 256,
                    out_dtype=BF16, name=f"ffn_conv_{i}")
    x_out = _mm(act.reshape(T, D_FF), wts["w_down"], add=x_mid, name=f"down_proj_{i}", tk=1408)
    saved = dict(x2=x2, h=h, proj2=proj2, xbc_act=xbc_act, dtb=dtb, dtc=dtc, alog=alog, dskip=dskip, ypre=ypre, hs=hs, y_ssm=y_ssm,
                 sinkcol=sinkcol, y_attn=y_attn, x_mid=x_mid, h2=h2, gu=gu, act=act)
    return x_out.reshape(B, L, D), saved


def _layer_bwd(i, dx_out, sv, wts, small, band_bias, attn_rider=None, ssd_rider=None, arrived=None):
    T, D = dx_out.shape
    B, L = sv["gu"].shape[:2]
    g = {}
    dxb = dx_out.astype(BF16)
    dact = _mm(dxb, wts["w_down"], tb=True, out_dtype=BF16, name=f"d_act_{i}", tn=1408)
    g["w_down"] = _mm(sv["act"].reshape(T, D_FF), dxb, ta=True, name=f"dw_down_{i}", tm=1408)
    dg, du, dcw, dcb = _conv_bwd(sv["gu"], 0, small["ffn_conv_w"][i], small["ffn_conv_b"][i], dact.reshape(B, L, D_FF),
                                 taps=FFN_CONV, ct=256, gate_blk0=D_FF // 256, name=f"d_ffn_conv_{i}")
    g["ffn_conv_w"] = jnp.sum(dcw, axis=0)[:FFN_CONV]
    g["ffn_conv_b"] = jnp.sum(dcb, axis=(0, 1))
    dgu = jnp.concatenate([dg, du], axis=-1).reshape(T, 2 * D_FF)
    dh2 = _mm(dgu, wts["w_up"], tb=True, name=f"d_h2_{i}", tk=1408)
    g["w_up"] = _mm(sv["h2"], dgu, ta=True, name=f"dw_up_{i}", tn=1408)
    dx_mid, dw2 = _rmsnorm_bwd(sv["x_mid"], small["norm2_w"][i], dh2, dx_out, name=f"d_norm2_{i}")
    g["norm2_w"] = dw2[0]
    dmb = dx_mid.astype(BF16)
    w_out = wts["w_out"]
    dy_ssm = _mm(dmb, w_out[:SSM_WIDTH], tb=True, name=f"d_y_ssm_{i}")
    dy_attn = _mm(dmb, w_out[SSM_WIDTH:], tb=True, out_dtype=BF16, name=f"d_y_attn_{i}")
    g["w_out"] = jnp.concatenate([_mm(sv["y_ssm"], dmb, ta=True, name=f"dw_out_a_{i}"),
                                  _mm(sv["y_attn"], dmb, ta=True, name=f"dw_out_b_{i}")], axis=0)
    dypre, dz, dwn = _gate_bwd(sv["ypre"].reshape(T, SSM_WIDTH), sv["proj2"], small["ssm_norm_w"][i], dy_ssm, name=f"d_gate_{i}")
    g["ssm_norm_w"] = jnp.sum(dwn, axis=(0, 1))
    proj = sv["proj2"].reshape(B, L, P_COLS)
    dxs, dbm, dcm, ddt, dalog, ddsk, carried = _ssd_bwd(sv["xbc_act"], sv["dtc"], sv["alog"], sv["dskip"], sv["hs"],
                                                        dypre.reshape(B, L, SSM_WIDTH), name=f"d_ssd_{i}",
                                                        rider=ssd_rider(g) if ssd_rider is not None else None)
    if carried is not None:
        arrived("ssd", carried)
    ddt, ddtb = _dt_bwd(proj, sv["dtb"], ddt, name=f"d_dt_{i}")
    g["dt_bias"] = jnp.sum(ddtb, axis=(0, 1))[:2 * SSM_HEADS].reshape(2, SSM_HEADS)
    g["a_log"] = jnp.sum(dalog, axis=(0, 1, 2))[:2 * SSM_HEADS].reshape(2, SSM_HEADS)
    g["d_skip"] = jnp.sum(ddsk.reshape(B, SSM_HEADS, HEAD_DIM), axis=(0, 2))
    dxbc_act = jnp.concatenate([dxs, dbm, dcm], axis=-1)
    dxbc, dcw, dcb = _conv_bwd(proj, 2048 // 256, small["conv_w"][i], small["conv_b"][i], dxbc_act, taps=SSM_CONV, ct=256,
                               name=f"d_ssm_conv_{i}")
    g["conv_w"] = jnp.sum(dcw, axis=0)[:SSM_CONV]
    g["conv_b"] = jnp.sum(dcb, axis=(0, 1))
    dq, dk, dv, dbias, dsink, carried = _attn_bwd(proj, band_bias, sv["sinkcol"], dy_attn.reshape(B, L, D), name=f"d_attn_{i}",
                                                  rider=attn_rider)
    if carried is not None:
        arrived("attn", carried)
    g["attn_sink"] = jnp.sum(dsink.reshape(ATTN_HEADS, BLOCK), axis=1)
    dproj = jnp.concatenate([dq, dz.reshape(B, L, SSM_WIDTH), dxbc, dk.astype(BF16), dv.astype(BF16), ddt], axis=-1).reshape(T, P_COLS)
    dh = _mm(dproj, wts["w_in"], tb=True, name=f"d_h_{i}", tk=1408)
    g["w_in"] = _unperm_in_cols(_mm(sv["h"], dproj, ta=True, name=f"dw_in_{i}", tn=1408))
    dx_in, dw1 = _rmsnorm_bwd(sv["x2"], small["norm1_w"][i], dh, dx_mid, name=f"d_norm1_{i}")
    g["norm1_w"] = dw1[0]
    return dx_in, g, dbias


_BIG = ("w_in", "w_out", "w_up", "w_down")
_BIG_AXIS = {"w_in": 2, "w_out": 1, "w_up": 2, "w_down": 1}
_SMALL = ("rel_bias", "norm1_w", "conv_w", "conv_b", "dt_bias", "a_log", "d_skip", "ssm_norm_w", "attn_sink", "norm2_w",
          "ffn_conv_w", "ffn_conv_b", "final_norm_w")
_SMALL_SHARDED = ("conv_w", "ffn_conv_w")
_ORDER = ("rel_bias", "norm1_w", "w_in", "conv_w", "conv_b", "dt_bias", "a_log", "d_skip", "ssm_norm_w", "attn_sink", "w_out",
          "norm2_w", "w_up", "ffn_conv_w", "ffn_conv_b", "w_down", "final_norm_w")


def _local_step(x, target, small, wts=None, exchange=None):
    B, L, D = x.shape
    bucket = _bucket_table()
    band_bias = _bias_expand(small["rel_bias"], bucket, name="band_bias")

    def fetch(spec):
        return exchange["gather"](spec) if exchange is not None and spec else None

    def fetched(spec, carried):
        for (i, k), full in zip(spec, exchange["weights"](spec, carried)):
            wts[i][k] = full

    if exchange is not None:
        wts = [{} for _ in range(DEPTH)]
        fetched([(0, "w_in")], _run_rider(fetch([(0, "w_in")]), name="gather_w_in_0"))
    saved = []
    for i in range(DEPTH):
        nxt = i + 1 < DEPTH
        if i == 0:
            plan = {"ssd": [(0, "w_out"), (0, "w_up"), (0, "w_down")], "attn": [(1, "w_in"), (1, "w_out"), (1, "w_up")] if nxt else []}
        else:
            plan = {"ssd": [(i, "w_down")] + ([(i + 1, "w_in"), (i + 1, "w_out")] if nxt else []), "attn": [(i + 1, "w_up")] if nxt else []}
        x, sv = _layer_fwd(i, x, wts[i], small, band_bias, {c: fetch(s) for c, s in plan.items()}, lambda c, r: fetched(plan[c], r))
        saved.append(sv)
    loss, dx, dwf = _loss_head(x.reshape(B * L, D), small["final_norm_w"], target.reshape(B * L, D), name="loss_head")
    per_layer = []
    dbias = jnp.zeros((ATTN_HEADS, BLOCK, KEY_SPAN), F32)
    late = None
    for i in reversed(range(DEPTH)):
        own = [(i, "w_down"), (i, "w_up"), (i, "w_out")]
        plan = {"ssd": own, "attn": late[0] if late else []}
        attn_rider = exchange["scatter"](*late) if late else None
        ssd_rider = (lambda g: exchange["scatter"](own, [g[k] for _, k in own])) if exchange is not None else None
        dx, g, dbias_i = _layer_bwd(i, dx, saved[i], wts[i], small, band_bias, attn_rider, ssd_rider,
                                    lambda c, r: exchange["collect"](plan[c], r))
        if exchange is not None:
            late = ([(i, "w_in")], [g["w_in"]])
            for k in _BIG:
                g.pop(k)
        dbias = dbias + dbias_i
        per_layer.append(g)
    if exchange is not None:
        exchange["collect"](late[0], _run_rider(exchange["scatter"](*late), name="scatter_dw_in_0"))
    per_layer.reverse()
    grads = {k: jnp.stack([g[k] for g in per_layer]) for k in per_layer[0]}
    grads["rel_bias"] = _bias_reduce(dbias, bucket, name="d_rel_bias")
    grads["final_norm_w"] = dwf[0]
    return loss, dx.reshape(B, L, D), grads


def _split_by_chip(g, axis):
    shp = g.shape
    n = shp[axis] // N_CHIP
    g = g.reshape(shp[:axis] + (N_CHIP, n) + shp[axis + 1:])
    return jnp.moveaxis(g, axis, 0)


def _join_chips(a, axis):
    a = jnp.moveaxis(a, 0, axis)
    shp = a.shape
    return a.reshape(shp[:axis] + (shp[axis] * shp[axis + 1],) + shp[axis + 2:])


def kernel(x, rel_bias, norm1_w, w_in, conv_w, conv_b, dt_bias, a_log, d_skip, ssm_norm_w, attn_sink, w_out, norm2_w, w_up, ffn_conv_w, ffn_conv_b, w_down, final_norm_w, loss_target, m_rel_bias, m_norm1_w, m_w_in, m_conv_w, m_conv_b, m_dt_bias, m_a_log, m_d_skip, m_ssm_norm_w, m_attn_sink, m_w_out, m_norm2_w, m_w_up, m_ffn_conv_w, m_ffn_conv_b, m_w_down, m_final_norm_w, v_rel_bias, v_norm1_w, v_w_in, v_conv_w, v_conv_b, v_dt_bias, v_a_log, v_d_skip, v_ssm_norm_w, v_attn_sink, v_w_out, v_norm2_w, v_w_up, v_ffn_conv_w, v_ffn_conv_b, v_w_down, v_final_norm_w):
    w = dict(rel_bias=rel_bias, norm1_w=norm1_w, w_in=w_in, conv_w=conv_w, conv_b=conv_b, dt_bias=dt_bias, a_log=a_log,
             d_skip=d_skip, ssm_norm_w=ssm_norm_w, attn_sink=attn_sink, w_out=w_out, norm2_w=norm2_w, w_up=w_up,
             ffn_conv_w=ffn_conv_w, ffn_conv_b=ffn_conv_b, w_down=w_down, final_norm_w=final_norm_w)
    m = dict(rel_bias=m_rel_bias, norm1_w=m_norm1_w, w_in=m_w_in, conv_w=m_conv_w, conv_b=m_conv_b, dt_bias=m_dt_bias,
             a_log=m_a_log, d_skip=m_d_skip, ssm_norm_w=m_ssm_norm_w, attn_sink=m_attn_sink, w_out=m_w_out, norm2_w=m_norm2_w,
             w_up=m_w_up, ffn_conv_w=m_ffn_conv_w, ffn_conv_b=m_ffn_conv_b, w_down=m_w_down, final_norm_w=m_final_norm_w)
    v = dict(rel_bias=v_rel_bias, norm1_w=v_norm1_w, w_in=v_w_in, conv_w=v_conv_w, conv_b=v_conv_b, dt_bias=v_dt_bias,
             a_log=v_a_log, d_skip=v_d_skip, ssm_norm_w=v_ssm_norm_w, attn_sink=v_attn_sink, w_out=v_w_out, norm2_w=v_norm2_w,
             w_up=v_w_up, ffn_conv_w=v_ffn_conv_w, ffn_conv_b=v_ffn_conv_b, w_down=v_w_down, final_norm_w=v_final_norm_w)
    my_chip = 2 * lax.axis_index("x") + lax.axis_index("y")

    shards = {k: w[k].astype(BF16) for k in _BIG}
    received = {}

    def gather(spec):
        return _gather_rider([shards[k][i] for i, k in spec])

    def weights(spec, carried):
        out = []
        for (i, k), g_ in zip(spec, carried):
            full = _join_chips(lax.dynamic_update_index_in_dim(g_, shards[k][i], my_chip, 0), _BIG_AXIS[k] - 1)
            out.append(_perm_in_cols(full) if k == "w_in" else full)
        return out

    def scatter(spec, grads):
        layer = spec[0][0]
        bufs = [_split_by_chip(g_, _BIG_AXIS[k] - 1).astype(BF16) for (_, k), g_ in zip(spec, grads)]
        prev = [received[(layer % 2, k)] for _, k in spec] if layer + 2 < DEPTH else []
        return _scatter_rider(bufs, layer, prev)

    def collect(spec, carried):
        for (i, k), pieces in zip(spec, carried):
            received[(i % 2, k)] = pieces

    conv_shapes = [(DEPTH, SSM_CONV, CONV_CH), (DEPTH, FFN_CONV, D_FF)]
    placed = [lax.dynamic_update_slice_in_dim(jnp.zeros(s, F32), w[k], my_chip * w[k].shape[2], axis=2)
              for k, s in zip(_SMALL_SHARDED, conv_shapes)]
    lead = (lax.axis_index("c") == 0).astype(F32)
    conv_full = _unpack(_allreduce_small(_pack([p * lead for p in placed]), name="gather_conv_weights"), conv_shapes)
    small = {k: w[k] for k in _SMALL}
    small["conv_w"], small["ffn_conv_w"] = conv_full

    loss_part, grad_x, gp = _local_step(x, loss_target, small,
                                        exchange=dict(gather=gather, weights=weights, scatter=scatter, collect=collect))

    small_shapes = [small[k].shape for k in _SMALL] + [()]
    red = _unpack(_allreduce_small(_pack([gp[k] for k in _SMALL] + [loss_part[0, :1]]), name="reduce_small"), small_shapes)
    gsmall = dict(zip(_SMALL, red[:-1]))
    loss = red[-1]
    for k in _SMALL_SHARDED:
        n = w[k].shape[2]
        gsmall[k] = lax.dynamic_slice_in_dim(gsmall[k], my_chip * n, n, axis=2)

    core = lax.axis_index("c")
    half = DEPTH // 2
    halves = [jnp.stack([_sum_sources(received[(p, k)], name=f"sum_{k}_{p}") for p in range(half)]) for k in _BIG]
    others = _join_halves(halves, name="join_halves")
    gbig = {}
    for k, mine_, theirs_ in zip(_BIG, halves, others):
        full = jnp.zeros((DEPTH,) + mine_.shape[1:], F32)
        full = lax.dynamic_update_slice_in_dim(full, mine_, core * half, axis=0)
        gbig[k] = lax.dynamic_update_slice_in_dim(full, theirs_, (1 - core) * half, axis=0)

    grad, delta, new_m, new_v = {}, {}, {}, {}
    for k in _BIG:
        shp = w[k].shape
        two = lambda a: a.reshape(shp[0] * shp[1], shp[2])
        d_, m_, v_ = _adamw(two(w[k]), two(gbig[k]), two(m[k]), two(v[k]), name=f"adamw_{k}")
        grad[k], delta[k], new_m[k], new_v[k] = gbig[k], d_.reshape(shp), m_.reshape(shp), v_.reshape(shp)
    shapes = [w[k].shape for k in _SMALL]
    d_, m_, v_ = _adamw(_pack([w[k] for k in _SMALL]), _pack([gsmall[k] for k in _SMALL]), _pack([m[k] for k in _SMALL]),
                        _pack([v[k] for k in _SMALL]), name="adamw_small")
    for k, a, b_, c_ in zip(_SMALL, _unpack(d_, shapes), _unpack(m_, shapes), _unpack(v_, shapes)):
        grad[k], delta[k], new_m[k], new_v[k] = gsmall[k], a, b_, c_
    return (loss, grad_x, *[grad[k] for k in _ORDER], *[delta[k] for k in _ORDER], *[new_m[k] for k in _ORDER],
            *[new_v[k] for k in _ORDER])
```

```python
import functools
import math

import jax
import jax.numpy as jnp
import numpy as np
from jax import lax
from jax.experimental import pallas as pl
from jax.experimental.pallas import tpu as pltpu

F32 = jnp.float32
BF16 = jnp.bfloat16
BS = pl.BlockSpec
SDS = jax.ShapeDtypeStruct
MESH = pl.DeviceIdType.MESH

D_MODEL = 1024
DEPTH = 4
SSM_HEADS = 16
SSM_WIDTH = 1024
BC_WIDTH = 256
CONV_CH = 1536
SSM_CONV = 7
CHUNK = 128
ATTN_HEADS = 16
KV_HEADS = 4
HEAD_DIM = 64
WINDOW = 128
BLOCK = 128
KEY_SPAN = 384
REL_BUCKETS = 32
REL_MAX_DIST = 128
D_FF = 2816
FFN_CONV = 3
NORM_EPS = 1e-6
Z_END = 1024
XBC_END = 2560
DT_END = 2592
Q_END = 3616
K_END = 3872
IN_COLS = 4128
P_COLS = 4224
ADAM_LR, ADAM_B1, ADAM_B2, ADAM_EPS, ADAM_WD, ADAM_STEP = 0.001, 0.9, 0.999, 1e-08, 0.01, 10
NEG = -1e30
N_DEV = 8
N_CHIP = 4
LANE = 128
VMEM_LIMIT_BYTES = 48 * 1024 * 1024


def _pc(body, *, name, grid, in_specs, out_specs, out_shape, scratch_shapes=()):
    return pl.pallas_call(
        body, name=name, grid=grid, in_specs=in_specs, out_specs=out_specs, out_shape=out_shape,
        scratch_shapes=list(scratch_shapes),
        compiler_params=pltpu.CompilerParams(dimension_semantics=("arbitrary",) * len(grid),
                                             vmem_limit_bytes=VMEM_LIMIT_BYTES))


def _split_rider_refs(refs, n_in, n_out, rider):
    n_rin = len(rider["ins"]) + len(rider["prev"])
    n_rout = len(rider["out_shape"])
    core = refs[:n_in] + refs[n_in + n_rin:n_in + n_rin + n_out] + refs[n_in + n_rin + n_out + n_rout + len(rider["scratch"]):]
    rins = refs[n_in:n_in + len(rider["ins"])]
    routs = refs[n_in + n_rin + n_out:n_in + n_rin + n_out + n_rout]
    sems = refs[n_in + n_rin + n_out + n_rout:n_in + n_rin + n_out + n_rout + len(rider["scratch"])]
    return core, rins, routs, sems


def _pc_carry(body, args, *, name, grid, in_specs, out_specs, out_shape, scratch_shapes=(), rider=None):
    if rider is None:
        return _pc(body, name=name, grid=grid, in_specs=in_specs, out_specs=out_specs, out_shape=out_shape,
                   scratch_shapes=scratch_shapes)(*args), None
    n_in, n_out = len(in_specs), len(out_shape)
    any_spec = BS(memory_space=pl.ANY)

    def full(*refs):
        core, rins, routs, sems = _split_rider_refs(refs, n_in, n_out, rider)
        ids = [pl.program_id(d) for d in range(len(grid))]
        first = functools.reduce(jnp.logical_and, [i == 0 for i in ids])
        last = functools.reduce(jnp.logical_and, [i == g - 1 for i, g in zip(ids, grid)])

        @pl.when(first)
        def _():
            rider["start"](rins, routs, sems)

        body(*core)

        @pl.when(last)
        def _():
            rider["finish"](rins, routs, sems)

    n_rin = len(rider["ins"])
    outs = pl.pallas_call(
        full, name=name, grid=grid,
        in_specs=list(in_specs) + [any_spec] * (n_rin + len(rider["prev"])),
        out_specs=list(out_specs) + [any_spec] * len(rider["out_shape"]),
        out_shape=list(out_shape) + list(rider["out_shape"]),
        scratch_shapes=list(rider["scratch"]) + list(scratch_shapes),
        input_output_aliases={n_in + n_rin + t: n_out + t for t in range(len(rider["prev"]))},
        compiler_params=pltpu.CompilerParams(dimension_semantics=("arbitrary",) * len(grid), vmem_limit_bytes=VMEM_LIMIT_BYTES,
                                             has_side_effects=True))(*args, *rider["ins"], *rider["prev"])
    return outs[:n_out], outs[n_out:]


def _run_rider(rider, *, name):
    any_spec = BS(memory_space=pl.ANY)
    n_rin = len(rider["ins"])

    def body(*refs):
        _, rins, routs, sems = _split_rider_refs(refs, 0, 0, rider)
        rider["start"](rins, routs, sems)
        rider["finish"](rins, routs, sems)

    return pl.pallas_call(
        body, name=name, in_specs=[any_spec] * (n_rin + len(rider["prev"])), out_specs=[any_spec] * len(rider["out_shape"]),
        out_shape=list(rider["out_shape"]), scratch_shapes=list(rider["scratch"]),
        input_output_aliases={n_rin + t: t for t in range(len(rider["prev"]))},
        compiler_params=pltpu.CompilerParams(has_side_effects=True))(*rider["ins"], *rider["prev"])


def _div_tile(n, pref, mult):
    t = min(pref, n)
    t -= t % mult
    while t >= mult:
        if n % t == 0:
            return t
        t -= mult
    return n


def _mm(a, b, *, name, ta=False, tb=False, add=None, out_dtype=F32, tm=1024, tn=1024, tk=1024):
    if ta:
        K, M = a.shape
    else:
        M, K = a.shape
    N = b.shape[0] if tb else b.shape[1]
    tm, tn, tk = _div_tile(M, tm, LANE), _div_tile(N, tn, LANE), _div_tile(K, tk, LANE)
    nk = K // tk
    dims = (((0,) if ta else (1,), (1,) if tb else (0,)), ((), ()))

    def body_single(*refs):
        r = lax.dot_general(refs[0][...], refs[1][...], dims, preferred_element_type=F32)
        if add is not None:
            r = r + refs[2][...]
        refs[-1][...] = r.astype(out_dtype)

    def body(*refs):
        if add is None:
            a_ref, b_ref, o_ref, acc_ref = refs
        else:
            a_ref, b_ref, add_ref, o_ref, acc_ref = refs
        k = pl.program_id(2)

        @pl.when(k == 0)
        def _():
            acc_ref[...] = jnp.zeros_like(acc_ref)

        acc_ref[...] += lax.dot_general(a_ref[...], b_ref[...], dims, preferred_element_type=F32)

        @pl.when(k == nk - 1)
        def _():
            r = acc_ref[...]
            if add is not None:
                r = r + add_ref[...]
            o_ref[...] = r.astype(out_dtype)

    a_spec = BS((tk, tm), lambda i, j, k: (k, i)) if ta else BS((tm, tk), lambda i, j, k: (i, k))
    b_spec = BS((tn, tk), lambda i, j, k: (j, k)) if tb else BS((tk, tn), lambda i, j, k: (k, j))
    in_specs, args = [a_spec, b_spec], [a, b]
    if add is not None:
        in_specs.append(BS((tm, tn), lambda i, j, k: (i, j)))
        args.append(add)
    return _pc(body_single if nk == 1 else body, name=name, grid=(M // tm, N // tn, nk), in_specs=in_specs,
               out_specs=BS((tm, tn), lambda i, j, k: (i, j)), out_shape=SDS((M, N), out_dtype),
               scratch_shapes=[] if nk == 1 else [pltpu.VMEM((tm, tn), F32)])(*args)


def _dot(a, b, dims):
    return lax.dot_general(a.astype(BF16), b.astype(BF16), (dims, ((), ())), preferred_element_type=F32)


@jax.custom_vjp
def _nn(a, b):
    return _dot(a, b, ((1,), (0,)))


@jax.custom_vjp
def _nt(a, b):
    return _dot(a, b, ((1,), (1,)))


@jax.custom_vjp
def _tn(a, b):
    return _dot(a, b, ((0,), (0,)))


_nn.defvjp(lambda a, b: (_nn(a, b), (a, b)), lambda r, g: (_nt(g, r[1]), _tn(r[0], g)))
_nt.defvjp(lambda a, b: (_nt(a, b), (a, b)), lambda r, g: (_nn(g, r[1]), _tn(g, r[0])))
_tn.defvjp(lambda a, b: (_tn(a, b), (a, b)), lambda r, g: (_nt(r[1], g), _nn(r[0], g)))


def _hdot(m, x):
    hi = x.astype(BF16)
    r1 = x - hi.astype(F32)
    lo = r1.astype(BF16)
    lo2 = (r1 - lo.astype(F32)).astype(BF16)
    n = x.shape[1]
    out = lax.dot_general(m.astype(BF16), jnp.concatenate([hi, lo, lo2], axis=1), (((1,), (0,)), ((), ())),
                          preferred_element_type=F32)
    return out[:, :n] + out[:, n:2 * n] + out[:, 2 * n:]


@jax.custom_vjp
def _cumdot(m, mt, x):
    return _hdot(m, x)


_cumdot.defvjp(lambda m, mt, x: (_hdot(m, x), (m, mt)),
               lambda r, g: (jnp.zeros_like(r[0]), jnp.zeros_like(r[1]), _hdot(r[1], g)))


def _sigmoid(x):
    return 1.0 / (1.0 + jnp.exp(-x))


def _softplus(x):
    return jnp.maximum(x, 0.0) + jnp.log(1.0 + jnp.exp(-jnp.abs(x)))


def _rms(x, w):
    return x * lax.rsqrt(jnp.mean(x * x, axis=-1, keepdims=True) + NORM_EPS) * w


def _norm_mm(x2, nw, b, *, name, out_dtype=F32, tm=1024, tn=1024):
    T, D = x2.shape
    N = b.shape[1]
    tm, tn = _div_tile(T, tm, LANE), _div_tile(N, tn, LANE)

    def body(x_ref, w_ref, b_ref, h_ref, o_ref):
        h = _rms(x_ref[...], w_ref[...]).astype(BF16)

        @pl.when(pl.program_id(1) == 0)
        def _():
            h_ref[...] = h

        o_ref[...] = lax.dot_general(h, b_ref[...], (((1,), (0,)), ((), ())), preferred_element_type=F32).astype(out_dtype)

    return _pc(body, name=name, grid=(T // tm, N // tn),
               in_specs=[BS((tm, D), lambda i, j: (i, 0)), BS((1, D), lambda i, j: (0, 0)), BS((D, tn), lambda i, j: (0, j))],
               out_specs=[BS((tm, D), lambda i, j: (i, 0)), BS((tm, tn), lambda i, j: (i, j))],
               out_shape=[SDS((T, D), BF16), SDS((T, N), out_dtype)])(x2, nw.reshape(1, D), b)


def _mm_dnorm(a, b, x2, nw, resid, *, name, tm=512, tk=1024):
    T, K = a.shape
    D = b.shape[0]
    tm, tk = _div_tile(T, tm, LANE), _div_tile(K, tk, LANE)
    nk = K // tk

    def body(a_ref, b_ref, x_ref, w_ref, r_ref, dx_ref, dxb_ref, dw_ref, acc_ref):
        i, k = pl.program_id(0), pl.program_id(1)

        @pl.when(k == 0)
        def _():
            acc_ref[...] = jnp.zeros_like(acc_ref)

        acc_ref[...] += lax.dot_general(a_ref[...], b_ref[...], (((1,), (1,)), ((), ())), preferred_element_type=F32)

        @pl.when(k == nk - 1)
        def _():
            _, vjp = jax.vjp(_rms, x_ref[...], w_ref[...])
            dx, dw = vjp(acc_ref[...])
            dx = dx + r_ref[...]
            dx_ref[...] = dx
            dxb_ref[...] = dx.astype(BF16)

            @pl.when(i == 0)
            def _():
                dw_ref[...] = jnp.zeros_like(dw_ref)

            dw_ref[...] += dw

    row = BS((tm, D), lambda i, k: (i, 0))
    one = BS((1, D), lambda i, k: (0, 0))
    return _pc(body, name=name, grid=(T // tm, nk),
               in_specs=[BS((tm, tk), lambda i, k: (i, k)), BS((D, tk), lambda i, k: (0, k)), row, one, row],
               out_specs=[row, row, one], out_shape=[SDS((T, D), F32), SDS((T, D), BF16), SDS((1, D), F32)],
               scratch_shapes=[pltpu.VMEM((tm, D), F32)])(a, b, x2, nw.reshape(1, D), resid)


ROW_PAD = 8


def _pad_rows(x):
    return jnp.concatenate([x, jnp.zeros((ROW_PAD, x.shape[1]), x.dtype)], axis=0)


def _shift_rows(xp, s):
    n = xp.shape[0] - ROW_PAD
    return xp[:n] if s == 0 else pltpu.roll(xp, (-s) % xp.shape[0], 0)[:n]


def _conv_taps(x, taps):
    xp = _pad_rows(x)
    return [_shift_rows(xp, k - taps // 2) for k in range(taps)]


def _conv_pre(xs, w_ref, b_ref):
    c = b_ref[...] + w_ref[0:1, :] * xs[0]
    for k in range(1, len(xs)):
        c = c + w_ref[k:k + 1, :] * xs[k]
    return c


def _conv_fwd(x3, x_blk0, w, b, *, taps, ct, gate_blk0=None, out_dtype, name):
    B, L, _ = x3.shape
    C = w.shape[1]
    wp = jnp.zeros((8, C), F32).at[:taps].set(w)

    def body(*refs):
        if gate_blk0 is None:
            x_ref, w_ref, b_ref, o_ref = refs
        else:
            x_ref, u_ref, w_ref, b_ref, o_ref = refs
        c = _conv_pre(_conv_taps(x_ref[0].astype(F32), taps), w_ref, b_ref)
        y = c * _sigmoid(c)
        if gate_blk0 is not None:
            y = y * u_ref[0].astype(F32)
        o_ref[0] = y.astype(out_dtype)

    in_specs = [BS((1, L, ct), lambda bi, j: (bi, 0, x_blk0 + j))]
    args = [x3]
    if gate_blk0 is not None:
        in_specs.append(BS((1, L, ct), lambda bi, j: (bi, 0, gate_blk0 + j)))
        args.append(x3)
    in_specs += [BS((8, ct), lambda bi, j: (0, j)), BS((1, ct), lambda bi, j: (0, j))]
    args += [wp, b.reshape(1, C)]
    return _pc(body, name=name, grid=(B, C // ct), in_specs=in_specs,
               out_specs=BS((1, L, ct), lambda bi, j: (bi, 0, j)), out_shape=SDS((B, L, C), out_dtype))(*args)


def _conv_bwd(x3, x_blk0, w, b, dy3, *, taps, ct, gate_blk0=None, name):
    B, L, _ = x3.shape
    C = w.shape[1]
    wp = jnp.zeros((8, C), F32).at[:taps].set(w)
    gated = gate_blk0 is not None

    def body(*refs):
        if gated:
            x_ref, u_ref, w_ref, b_ref, dy_ref, dx_ref, du_ref, dw_ref, db_ref = refs
        else:
            x_ref, w_ref, b_ref, dy_ref, dx_ref, dw_ref, db_ref = refs
        xs = _conv_taps(x_ref[0].astype(F32), taps)
        dy = dy_ref[0].astype(F32)
        c = _conv_pre(xs, w_ref, b_ref)
        sg = _sigmoid(c)
        dsilu = sg * (1.0 + c * (1.0 - sg))
        if gated:
            du_ref[0] = (dy * (c * sg)).astype(BF16)
            dc = dy * u_ref[0].astype(F32) * dsilu
        else:
            dc = dy * dsilu
        dcp = _pad_rows(dc)
        dx = jnp.zeros_like(dc)
        dw_ref[0] = jnp.zeros((8, ct), F32)
        for k in range(taps):
            dx = dx + w_ref[k:k + 1, :] * _shift_rows(dcp, taps // 2 - k)
            dw_ref[0, k:k + 1, :] = jnp.sum(dc * xs[k], axis=0, keepdims=True)
        dx_ref[0] = dx.astype(BF16)
        db_ref[0] = jnp.sum(dc, axis=0, keepdims=True)

    xs = BS((1, L, ct), lambda bi, j: (bi, 0, x_blk0 + j))
    ys = BS((1, L, ct), lambda bi, j: (bi, 0, j))
    in_specs, args = [xs], [x3]
    if gated:
        in_specs.append(BS((1, L, ct), lambda bi, j: (bi, 0, gate_blk0 + j)))
        args.append(x3)
    in_specs += [BS((8, ct), lambda bi, j: (0, j)), BS((1, ct), lambda bi, j: (0, j)), ys]
    args += [wp, b.reshape(1, C), dy3]
    out_specs = [ys] + ([ys] if gated else []) + [BS((1, 8, ct), lambda bi, j: (bi, 0, j)), BS((1, 1, ct), lambda bi, j: (bi, 0, j))]
    out_shape = [SDS((B, L, C), BF16)] + ([SDS((B, L, C), BF16)] if gated else []) + [SDS((B, 8, C), F32), SDS((B, 1, C), F32)]
    return _pc(body, name=name, grid=(B, C // ct), in_specs=in_specs, out_specs=out_specs, out_shape=out_shape)(*args)


def _tri(reverse):
    r = lax.broadcasted_iota(jnp.int32, (CHUNK, CHUNK), 0)
    c = lax.broadcasted_iota(jnp.int32, (CHUNK, CHUNK), 1)
    return (c >= r) if reverse else (c <= r)


PAIRS = 2
QUADS = SSM_HEADS // (2 * PAIRS)
QW = PAIRS * LANE


def _ssd_chunk(h0, h1, x0, x1, bm, cm, dtc, alog, *, col0, reverse):
    mask = _tri(reverse)
    eye = lax.broadcasted_iota(jnp.int32, (CHUNK, CHUNK), 0) == lax.broadcasted_iota(jnp.int32, (CHUNK, CHUNK), 1)
    lane = lax.broadcasted_iota(jnp.int32, (1, LANE), 1)
    first = lane < HEAD_DIM
    adt = dtc * (-jnp.exp(alog))
    cumc = _cumdot(mask.astype(F32), _tri(not reverse).astype(F32), adt)
    totc = jnp.sum(adt, axis=0, keepdims=True)
    cb = _nt(cm, bm)

    def col(v, c):
        return jnp.sum(jnp.where(lane == c, v, 0.0), axis=1, keepdims=True)

    outs, states = [], []
    for p, (hprev, xs) in enumerate(((h0, x0), (h1, x1))):
        c0 = col0 + 2 * p
        cj = (col(cumc, c0), col(cumc, c0 + 1))
        cum = jnp.where(first, cj[0], cj[1])
        tot = jnp.where(first, col(totc, c0), col(totc, c0 + 1))
        xdt = xs * jnp.where(first, col(dtc, c0), col(dtc, c0 + 1))
        y = _nn(cm, hprev) * jnp.exp(cum)
        for j in range(2):
            rj = jnp.sum(jnp.where(eye, cj[j], 0.0), axis=0, keepdims=True)
            dec = jnp.exp(jnp.where(mask, cj[j] - rj, NEG))
            y = y + _nn(cb * dec, jnp.where(first if j == 0 else ~first, xdt, 0.0))
        outs.append(y)
        states.append(hprev * jnp.exp(tot) + _tn(bm, xdt * jnp.exp(tot - cum)))
    return outs[0], outs[1], states[0], states[1]


def _ssd_specs(B, L):
    def lanes(w, blk):
        return BS((1, L, w), blk)

    return [
        lanes(QW, lambda b, q: (b, 0, q)),
        lanes(LANE, lambda b, q: (b, 0, 8 + q // 2)),
        lanes(LANE, lambda b, q: (b, 0, 10 + q // 2)),
        lanes(LANE, lambda b, q: (b, 0, 0)),
        BS((1, LANE), lambda b, q: (0, 0)),
        BS((1, QW), lambda b, q: (0, q)),
    ]


def _ssd_slot(d, ci):
    return ci if d == 0 else ci + 1


def _ssd_fwd(xbc_act, dtc, alog, dskip, *, name, rider=None):
    B, L, _ = xbc_act.shape
    nc = L // CHUNK

    def body(xs_ref, b_ref, c_ref, dt_ref, alog_ref, dsk_ref, y_ref, hs_ref):
        q = pl.program_id(1)
        alog_v = alog_ref[...]
        y_ref[0] = dsk_ref[...] * xs_ref[0]
        hs_ref[0, 0, 0, 0] = jnp.zeros((LANE, QW), F32)
        hs_ref[0, 0, 1, nc] = jnp.zeros((LANE, QW), F32)

        def step(i, carry):
            cis = (i, nc - 1 - i)
            rows = [pl.ds(pl.multiple_of(ci * CHUNK, CHUNK), CHUNK) for ci in cis]
            res = []
            for d in range(2):
                cur = _ssd_slot(d, cis[d])
                res.append(_ssd_chunk(
                    hs_ref[0, 0, d, cur, :, :LANE], hs_ref[0, 0, d, cur, :, LANE:], xs_ref[0, rows[d], :LANE],
                    xs_ref[0, rows[d], LANE:], b_ref[0, rows[d], :], c_ref[0, rows[d], :], dt_ref[0, rows[d], :], alog_v,
                    col0=SSM_HEADS * d + 2 * PAIRS * q, reverse=d == 1))
            for d in range(2):
                y0, y1, n0, n1 = res[d]
                nxt = _ssd_slot(d, cis[d] + 1 if d == 0 else cis[d] - 1)
                hs_ref[0, 0, d, nxt, :, :LANE] = n0
                hs_ref[0, 0, d, nxt, :, LANE:] = n1
                y_ref[0, rows[d], :LANE] += y0
                y_ref[0, rows[d], LANE:] += y1
            return carry

        lax.fori_loop(0, nc, step, 0, unroll=2)

    (y, hs), carried = _pc_carry(
        body, (xbc_act, xbc_act, xbc_act, dtc, alog, dskip), name=name, grid=(B, QUADS), in_specs=_ssd_specs(B, L),
        out_specs=[BS((1, L, QW), lambda b, q: (b, 0, q)), BS((1, 1, 2, nc + 1, LANE, QW), lambda b, q: (b, q, 0, 0, 0, 0))],
        out_shape=[SDS((B, L, SSM_WIDTH), F32), SDS((B, QUADS, 2, nc + 1, LANE, QW), F32)], rider=rider)
    return y, hs, carried


def _ssd_bwd(xbc_act, dtc, alog, dskip, hs, dy, *, name, rider=None):
    B, L, _ = xbc_act.shape
    nc = L // CHUNK

    def body(xs_ref, b_ref, c_ref, dt_ref, alog_ref, dsk_ref, hs_ref, dy_ref,
             dxs_ref, db_ref, dc_ref, ddt_ref, dalog_ref, ddsk_ref, dh_ref):
        q = pl.program_id(1)
        alog_v = alog_ref[...]

        @pl.when(q % 2 == 0)
        def _():
            db_ref[...] = jnp.zeros_like(db_ref)
            dc_ref[...] = jnp.zeros_like(dc_ref)

        @pl.when(q == 0)
        def _():
            ddt_ref[...] = jnp.zeros_like(ddt_ref)

        dxs_ref[0] = dy_ref[0] * dsk_ref[...]
        ddsk_ref[0] = jnp.sum(dy_ref[0] * xs_ref[0], axis=0, keepdims=True)
        dh_ref[...] = jnp.zeros_like(dh_ref)

        def step(i, carry):
            g_alog = carry
            cis = (nc - 1 - i, i)
            rows = [pl.ds(pl.multiple_of(ci * CHUNK, CHUNK), CHUNK) for ci in cis]
            res = []
            for d in range(2):
                cur = _ssd_slot(d, cis[d])
                fn = functools.partial(_ssd_chunk, col0=SSM_HEADS * d + 2 * PAIRS * q, reverse=d == 1)
                _, vjp = jax.vjp(fn, hs_ref[0, 0, d, cur, :, :LANE], hs_ref[0, 0, d, cur, :, LANE:], xs_ref[0, rows[d], :LANE],
                                 xs_ref[0, rows[d], LANE:], b_ref[0, rows[d], :], c_ref[0, rows[d], :], dt_ref[0, rows[d], :],
                                 alog_v)
                res.append(vjp((dy_ref[0, rows[d], :LANE], dy_ref[0, rows[d], LANE:], dh_ref[d, :, :LANE], dh_ref[d, :, LANE:])))
            for d in range(2):
                g_h0, g_h1, g_x0, g_x1, g_b, g_c, g_dt, g_alog1 = res[d]
                dh_ref[d, :, :LANE] = g_h0
                dh_ref[d, :, LANE:] = g_h1
                dxs_ref[0, rows[d], :LANE] += g_x0
                dxs_ref[0, rows[d], LANE:] += g_x1
                db_ref[0, rows[d], :] += g_b
                dc_ref[0, rows[d], :] += g_c
                ddt_ref[0, rows[d], :] += g_dt
                g_alog = g_alog + g_alog1
            return g_alog

        dalog_ref[0, 0] = lax.fori_loop(0, nc, step, jnp.zeros((1, LANE), F32))

    lanes = lambda w, blk: BS((1, L, w), blk)
    in_specs = _ssd_specs(B, L) + [BS((1, 1, 2, nc + 1, LANE, QW), lambda b, q: (b, q, 0, 0, 0, 0)), lanes(QW, lambda b, q: (b, 0, q))]
    out_specs = [lanes(QW, lambda b, q: (b, 0, q)), lanes(LANE, lambda b, q: (b, 0, q // 2)), lanes(LANE, lambda b, q: (b, 0, q // 2)),
                 lanes(LANE, lambda b, q: (b, 0, 0)), BS((1, 1, 1, LANE), lambda b, q: (b, q, 0, 0)),
                 BS((1, 1, QW), lambda b, q: (b, 0, q))]
    out_shape = [SDS((B, L, SSM_WIDTH), F32), SDS((B, L, BC_WIDTH), F32), SDS((B, L, BC_WIDTH), F32), SDS((B, L, LANE), F32),
                 SDS((B, QUADS, 1, LANE), F32), SDS((B, 1, SSM_WIDTH), F32)]
    outs, carried = _pc_carry(body, (xbc_act, xbc_act, xbc_act, dtc, alog, dskip, hs, dy), name=name, grid=(B, QUADS),
                              in_specs=in_specs, out_specs=out_specs, out_shape=out_shape,
                              scratch_shapes=[pltpu.VMEM((2, LANE, QW), F32)], rider=rider)
    return (*outs, carried)


def _gate_norm(yp, z, w):
    v = yp * (z * _sigmoid(z))
    return v * lax.rsqrt(jnp.mean(v * v, axis=-1, keepdims=True) + NORM_EPS) * w


def _gate_fwd(ypre2, proj2, w, *, name):
    T = ypre2.shape[0]
    tr = _div_tile(T, 512, 8)
    G = 512

    def body(y_ref, z_ref, w_ref, o_ref):
        o_ref[...] = _gate_norm(y_ref[...], z_ref[...], w_ref[...]).astype(BF16)

    return _pc(body, name=name, grid=(T // tr, 2),
               in_specs=[BS((tr, G), lambda i, g: (i, g)), BS((tr, G), lambda i, g: (i, 2 + g)), BS((1, G), lambda i, g: (0, g))],
               out_specs=BS((tr, G), lambda i, g: (i, g)), out_shape=SDS((T, SSM_WIDTH), BF16))(ypre2, proj2, w.reshape(1, -1))


def _gate_bwd(ypre2, proj2, w, dy, *, name):
    T = ypre2.shape[0]
    tr = _div_tile(T, 512, 8)
    G = 512

    def body(y_ref, z_ref, w_ref, dy_ref, dyp_ref, dz_ref, dw_ref):
        _, vjp = jax.vjp(_gate_norm, y_ref[...], z_ref[...], w_ref[...])
        dyp, dz, dw = vjp(dy_ref[...])
        dyp_ref[...] = dyp
        dz_ref[...] = dz.astype(BF16)
        dw_ref[0] = dw

    tile = BS((tr, G), lambda i, g: (i, g))
    return _pc(body, name=name, grid=(T // tr, 2),
               in_specs=[tile, BS((tr, G), lambda i, g: (i, 2 + g)), BS((1, G), lambda i, g: (0, g)), tile],
               out_specs=[tile, tile, BS((1, 1, G), lambda i, g: (i, 0, g))],
               out_shape=[SDS((T, SSM_WIDTH), F32), SDS((T, SSM_WIDTH), BF16), SDS((T // tr, 1, SSM_WIDTH), F32)])(
        ypre2, proj2, w.reshape(1, -1), dy)


def _first_half():
    return lax.broadcasted_iota(jnp.int32, (1, LANE), 1) < HEAD_DIM


def _dup_kv_head(pair, odd):
    rolled = pltpu.roll(pair, HEAD_DIM, 1)
    return jnp.where(_first_half(), rolled, pair) if odd else jnp.where(_first_half(), pair, rolled)


def _stack_heads(quad):
    first = _first_half()
    lo, hi = quad[:, :LANE], quad[:, LANE:]
    return jnp.concatenate([jnp.where(first, lo, 0.0), jnp.where(first, 0.0, lo), jnp.where(first, hi, 0.0),
                            jnp.where(first, 0.0, hi)], axis=0)


def _unstack_heads(o):
    first = _first_half()
    return jnp.concatenate([jnp.where(first, o[:BLOCK], o[BLOCK:2 * BLOCK]), jnp.where(first, o[2 * BLOCK:3 * BLOCK], o[3 * BLOCK:])], axis=1)


def _fold_kv_head(d, odd):
    tot = d + pltpu.roll(d, HEAD_DIM, 1)
    return jnp.where(_first_half(), 0.0, tot) if odd else jnp.where(_first_half(), tot, 0.0)


def _attn_softmax(s, sink):
    m = jnp.maximum(jnp.max(s, axis=-1, keepdims=True), sink)
    p = jnp.exp(s - m)
    ps = jnp.exp(sink - m)
    inv = 1.0 / (jnp.sum(p, axis=-1, keepdims=True) + ps)
    return p * inv, ps * inv


def _attn_colneg(n, L):
    kpos = n * BLOCK - WINDOW + lax.broadcasted_iota(jnp.int32, (1, KEY_SPAN), 1)
    return jnp.where((kpos >= 0) & (kpos < L), 0.0, NEG)


def _attn_in_specs(L):
    nblk = L // BLOCK
    kv = lambda o, col: BS((1, BLOCK, 4 * HEAD_DIM), lambda b, n: (b, jnp.clip(n + o, 0, nblk - 1), col))
    kcol, vcol = 3584 // 256, 3840 // 256
    return [BS((1, BLOCK, ATTN_HEADS * HEAD_DIM), lambda b, n: (b, n, 0)), kv(-1, kcol), kv(0, kcol), kv(1, kcol),
            kv(-1, vcol), kv(0, vcol), kv(1, vcol),
            BS((ATTN_HEADS, BLOCK, KEY_SPAN), lambda b, n: (0, 0, 0)), BS((ATTN_HEADS * BLOCK, 1), lambda b, n: (0, 0))]


def _attn_fwd(proj, bias, sinkcol, *, name, rider=None):
    B, L, _ = proj.shape

    def body(q_ref, k0, k1, k2, v0, v1, v2, bias_ref, sink_ref, o_ref):
        colneg = _attn_colneg(pl.program_id(1), L)
        kcat = jnp.concatenate([k0[0], k1[0], k2[0]], axis=0)
        vcat = jnp.concatenate([v0[0], v1[0], v2[0]], axis=0)
        scores, probs = [], []
        for g in range(KV_HEADS):
            pair = slice(LANE * (g // 2), LANE * (g // 2) + LANE)
            quad = slice(4 * HEAD_DIM * g, 4 * HEAD_DIM * (g + 1))
            kd = _dup_kv_head(kcat[:, pair], g % 2).astype(BF16)
            qs = (_stack_heads(q_ref[0, :, quad]) * HEAD_DIM ** -0.5).astype(BF16)
            scores.append(lax.dot_general(qs, kd, (((1,), (1,)), ((), ())), preferred_element_type=F32))
        for g in range(KV_HEADS):
            pn, _ = _attn_softmax(scores[g] + bias_ref[4 * g:4 * g + 4].reshape(4 * BLOCK, KEY_SPAN) + colneg,
                                  sink_ref[4 * BLOCK * g:4 * BLOCK * (g + 1)])
            probs.append(pn.astype(BF16))
        for g in range(KV_HEADS):
            pair = slice(LANE * (g // 2), LANE * (g // 2) + LANE)
            quad = slice(4 * HEAD_DIM * g, 4 * HEAD_DIM * (g + 1))
            vd = _dup_kv_head(vcat[:, pair], g % 2).astype(BF16)
            o = lax.dot_general(probs[g], vd, (((1,), (0,)), ((), ())), preferred_element_type=F32)
            o_ref[0, :, quad] = _unstack_heads(o).astype(BF16)

    (out,), carried = _pc_carry(body, (proj, proj, proj, proj, proj, proj, proj, bias, sinkcol), name=name, grid=(B, L // BLOCK),
                                in_specs=_attn_in_specs(L),
                                out_specs=[BS((1, BLOCK, ATTN_HEADS * HEAD_DIM), lambda b, n: (b, n, 0))],
                                out_shape=[SDS((B, L, ATTN_HEADS * HEAD_DIM), BF16)], rider=rider)
    return out, carried


def _attn_bwd(proj, bias, sinkcol, dout, *, name, rider=None):
    B, L, _ = proj.shape
    nblk = L // BLOCK
    nn, nt, tn = (((1,), (0,)), ((), ())), (((1,), (1,)), ((), ())), (((0,), (0,)), ((), ()))

    def body(q_ref, k0, k1, k2, v0, v1, v2, bias_ref, sink_ref, do_ref, dq_ref, dk_ref, dv_ref, dbias_ref, dsink_ref):
        b, n = pl.program_id(0), pl.program_id(1)

        @pl.when(n == 0)
        def _():
            dk_ref[...] = jnp.zeros_like(dk_ref)
            dv_ref[...] = jnp.zeros_like(dv_ref)

        @pl.when((n == 0) & (b == 0))
        def _():
            dbias_ref[...] = jnp.zeros_like(dbias_ref)
            dsink_ref[...] = jnp.zeros_like(dsink_ref)

        colneg = _attn_colneg(n, L)
        kcat = jnp.concatenate([k0[0], k1[0], k2[0]], axis=0)
        vcat = jnp.concatenate([v0[0], v1[0], v2[0]], axis=0)
        krows = [pl.ds(pl.multiple_of(jnp.clip(n + o, 0, nblk - 1) * BLOCK, BLOCK), BLOCK) for o in (-1, 0, 1)]
        ops, mids = [], []
        for g in range(KV_HEADS):
            pair = slice(LANE * (g // 2), LANE * (g // 2) + LANE)
            quad = slice(4 * HEAD_DIM * g, 4 * HEAD_DIM * (g + 1))
            kd = _dup_kv_head(kcat[:, pair], g % 2).astype(BF16)
            vd = _dup_kv_head(vcat[:, pair], g % 2).astype(BF16)
            qs = (_stack_heads(q_ref[0, :, quad]) * HEAD_DIM ** -0.5).astype(BF16)
            dos = _stack_heads(do_ref[0, :, quad].astype(F32)).astype(BF16)
            ops.append((kd, qs, dos, lax.dot_general(qs, kd, nt, preferred_element_type=F32),
                        lax.dot_general(dos, vd, nt, preferred_element_type=F32)))
        for g in range(KV_HEADS):
            rows = slice(4 * BLOCK * g, 4 * BLOCK * (g + 1))
            _, _, _, s, dpn = ops[g]
            pn, psink = _attn_softmax(s + bias_ref[4 * g:4 * g + 4].reshape(4 * BLOCK, KEY_SPAN) + colneg, sink_ref[rows])
            r = jnp.sum(dpn * pn, axis=-1, keepdims=True)
            ds = pn * (dpn - r)
            dbias_ref[4 * g:4 * g + 4] += ds.reshape(4, BLOCK, KEY_SPAN)
            dsink_ref[rows] += -psink * r
            mids.append((pn.astype(BF16), ds.astype(BF16)))
        for g in range(KV_HEADS):
            pair = slice(LANE * (g // 2), LANE * (g // 2) + LANE)
            quad = slice(4 * HEAD_DIM * g, 4 * HEAD_DIM * (g + 1))
            kd, qs, dos, _, _ = ops[g]
            pnb, dsb = mids[g]
            dvd = lax.dot_general(pnb, dos, tn, preferred_element_type=F32)
            dkd = lax.dot_general(dsb, qs, tn, preferred_element_type=F32)
            dqs = lax.dot_general(dsb, kd, nn, preferred_element_type=F32) * HEAD_DIM ** -0.5
            dq_ref[0, :, quad] = _unstack_heads(dqs).astype(BF16)
            dk_g, dv_g = _fold_kv_head(dkd, g % 2), _fold_kv_head(dvd, g % 2)
            for o in range(3):
                dk_ref[0, krows[o], pair] += dk_g[o * BLOCK:(o + 1) * BLOCK]
                dv_ref[0, krows[o], pair] += dv_g[o * BLOCK:(o + 1) * BLOCK]

    qspec = BS((1, BLOCK, ATTN_HEADS * HEAD_DIM), lambda b, n: (b, n, 0))
    kvout = BS((1, L, 4 * HEAD_DIM), lambda b, n: (b, 0, 0))
    outs, carried = _pc_carry(
        body, (proj, proj, proj, proj, proj, proj, proj, bias, sinkcol, dout), name=name, grid=(B, nblk),
        in_specs=_attn_in_specs(L) + [qspec],
        out_specs=[qspec, kvout, kvout, BS((ATTN_HEADS, BLOCK, KEY_SPAN), lambda b, n: (0, 0, 0)),
                   BS((ATTN_HEADS * BLOCK, 1), lambda b, n: (0, 0))],
        out_shape=[SDS((B, L, ATTN_HEADS * HEAD_DIM), BF16), SDS((B, L, 4 * HEAD_DIM), F32), SDS((B, L, 4 * HEAD_DIM), F32),
                   SDS((ATTN_HEADS, BLOCK, KEY_SPAN), F32), SDS((ATTN_HEADS * BLOCK, 1), F32)], rider=rider)
    return (*outs, carried)


def _t5_bucket(rel):
    half = REL_BUCKETS // 2
    max_exact = half // 2
    ret = jnp.where(rel > 0, half, 0)
    n = jnp.abs(rel)
    nf = jnp.maximum(n, 1).astype(F32)
    large = max_exact + (jnp.log(nf / max_exact) / math.log(REL_MAX_DIST / max_exact) * (half - max_exact)).astype(jnp.int32)
    large = jnp.minimum(large, half - 1)
    return ret + jnp.where(n < max_exact, n, large)


def _bucket_table():
    rel = jnp.arange(KEY_SPAN)[None, :] - WINDOW - jnp.arange(BLOCK)[:, None]
    return _t5_bucket(rel).astype(jnp.int32)


def _bias_expand(rel_bias, bucket, *, name):
    rbt = jnp.zeros((ATTN_HEADS, 1, LANE), F32).at[:, 0, :REL_BUCKETS].set(rel_bias.T)

    def body(rb_ref, bk_ref, o_ref):
        lane = lax.broadcasted_iota(jnp.int32, (1, LANE), 1)
        row = rb_ref[0]
        bk = bk_ref[...]
        acc = jnp.zeros((BLOCK, KEY_SPAN), F32)
        for r in range(REL_BUCKETS):
            val = jnp.sum(jnp.where(lane == r, row, 0.0), axis=1, keepdims=True)
            acc = jnp.where(bk == r, val, acc)
        rel = (lax.broadcasted_iota(jnp.int32, (BLOCK, KEY_SPAN), 1) - WINDOW
               - lax.broadcasted_iota(jnp.int32, (BLOCK, KEY_SPAN), 0))
        o_ref[0] = jnp.where(jnp.abs(rel) <= WINDOW, acc, NEG)

    return _pc(body, name=name, grid=(ATTN_HEADS,),
               in_specs=[BS((1, 1, LANE), lambda h: (h, 0, 0)), BS((BLOCK, KEY_SPAN), lambda h: (0, 0))],
               out_specs=BS((1, BLOCK, KEY_SPAN), lambda h: (h, 0, 0)), out_shape=SDS((ATTN_HEADS, BLOCK, KEY_SPAN), F32))(rbt, bucket)


def _bias_reduce(dbias, bucket, *, name):
    def body(db_ref, bk_ref, o_ref):
        lane = lax.broadcasted_iota(jnp.int32, (1, LANE), 1)
        x = db_ref[0]
        bk = bk_ref[...]
        acc = jnp.zeros((1, LANE), F32)
        for r in range(REL_BUCKETS):
            part = jnp.sum(jnp.where(bk == r, x, 0.0), axis=1, keepdims=True)
            acc = jnp.where(lane == r, jnp.sum(part, axis=0, keepdims=True), acc)
        o_ref[0] = acc

    out = _pc(body, name=name, grid=(ATTN_HEADS,),
              in_specs=[BS((1, BLOCK, KEY_SPAN), lambda h: (h, 0, 0)), BS((BLOCK, KEY_SPAN), lambda h: (0, 0))],
              out_specs=BS((1, 1, LANE), lambda h: (h, 0, 0)), out_shape=SDS((ATTN_HEADS, 1, LANE), F32))(dbias, bucket)
    return out[:, 0, :REL_BUCKETS].T


def _loss_head(x2, w, target, *, name):
    T, D = x2.shape
    tr = _div_tile(T, 512, 8)

    def tile_loss(x, w, t):
        err = _rms(x, w) - t
        return 0.5 * jnp.sum(jnp.mean(err * err, axis=-1, keepdims=True), axis=0, keepdims=True)

    def body(x_ref, w_ref, t_ref, loss_ref, dx_ref, dxb_ref, dw_ref):
        t = t_ref[...]
        l, vjp = jax.vjp(lambda x, w: tile_loss(x, w, t), x_ref[...], w_ref[...])
        dx, dw = vjp(jnp.ones((1, 1), F32))
        dx_ref[...] = dx
        dxb_ref[...] = dx.astype(BF16)

        @pl.when(pl.program_id(0) == 0)
        def _():
            dw_ref[...] = jnp.zeros_like(dw_ref)
            loss_ref[...] = jnp.zeros_like(loss_ref)

        dw_ref[...] += dw
        loss_ref[...] += l + jnp.zeros((1, LANE), F32)

    row = BS((tr, D), lambda i: (i, 0))
    one = BS((1, D), lambda i: (0, 0))
    return _pc(body, name=name, grid=(T // tr,), in_specs=[row, one, row],
               out_specs=[BS((1, LANE), lambda i: (0, 0)), row, row, one],
               out_shape=[SDS((1, LANE), F32), SDS((T, D), F32), SDS((T, D), BF16), SDS((1, D), F32)])(x2, w.reshape(1, D), target)


def _adamw(w2, g2, m2, v2, *, name):
    R, C = w2.shape
    tr = _div_tile(R, 256, 8)
    c1 = 1.0 - ADAM_B1 ** ADAM_STEP
    c2 = 1.0 - ADAM_B2 ** ADAM_STEP

    def body(w_ref, g_ref, m_ref, v_ref, d_ref, nm_ref, nv_ref):
        g = g_ref[...]
        m = ADAM_B1 * m_ref[...] + (1.0 - ADAM_B1) * g
        v = ADAM_B2 * v_ref[...] + (1.0 - ADAM_B2) * (g * g)
        d_ref[...] = -ADAM_LR * ((m / c1) / (jnp.sqrt(v / c2) + ADAM_EPS) + ADAM_WD * w_ref[...])
        nm_ref[...] = m
        nv_ref[...] = v

    t = BS((tr, C), lambda i: (i, 0))
    return _pc(body, name=name, grid=(R // tr,), in_specs=[t, t, t, t], out_specs=[t, t, t],
               out_shape=[SDS((R, C), F32)] * 3)(w2, g2, m2, v2)


def _place():
    return lax.axis_index("x"), lax.axis_index("y"), lax.axis_index("c")


def _gather_rider(shards):
    na = len(shards)

    def copies(ins, outs, sems):
        send_sems, recv_sems = sems
        x, y, c = _place()
        for a in range(na):
            for k, (px, py) in enumerate([(1 - x, y), (x, 1 - y), (1 - x, 1 - y)]):
                send = functools.partial(pltpu.make_async_remote_copy, ins[a], outs[a].at[2 * x + y], send_sems.at[a, k],
                                         recv_sems.at[a, k], device_id=(px, py, c), device_id_type=MESH)
                got = outs[a].at[2 * px + py]
                arrived = functools.partial(pltpu.make_async_remote_copy, got, got, send_sems.at[a, k], recv_sems.at[a, k],
                                            device_id=(px, py, c), device_id_type=MESH)
                yield send, arrived

    def start(ins, outs, sems):
        for send, _ in copies(ins, outs, sems):
            send().start()

    def finish(ins, outs, sems):
        both = list(copies(ins, outs, sems))
        for _, arrived in both:
            arrived().wait_recv()
        for send, _ in both:
            send().wait_send()

    return dict(ins=list(shards), prev=[], out_shape=[SDS((N_CHIP,) + s.shape, s.dtype) for s in shards],
                scratch=[pltpu.SemaphoreType.DMA((na, 3)), pltpu.SemaphoreType.DMA((na, 3))], start=start, finish=finish)


def _scatter_rider(bufs, layer, prev):
    na = len(bufs)
    h = layer // (DEPTH // 2)

    def copies(ins, outs, sems):
        send_sems, recv_sems, local_sems = sems
        x, y, c = _place()
        me = 4 * x + 2 * y + c
        for a in range(na):
            for j in range(N_CHIP):
                is_self = ((2 * x + y) == j) & (c == h)
                local = functools.partial(pltpu.make_async_copy, ins[a].at[j], outs[a].at[me], local_sems.at[a])
                remote = functools.partial(pltpu.make_async_remote_copy, ins[a].at[j], outs[a].at[me], send_sems.at[a, j],
                                           recv_sems.at[a, me], device_id=(j // 2, j % 2, h), device_id_type=MESH)
                yield is_self, local, remote

    def start(ins, outs, sems):
        for is_self, local, remote in copies(ins, outs, sems):
            pl.when(is_self)(lambda: local().start())
            pl.when(jnp.logical_not(is_self))(lambda: remote().start())

    def finish(ins, outs, sems):
        _, recv_sems, _ = sems
        x, y, c = _place()
        me = 4 * x + 2 * y + c
        for a in range(na):
            for s in range(N_DEV):
                got = outs[a].at[s]
                arrived = functools.partial(pltpu.make_async_remote_copy, got, got, recv_sems.at[a, s], recv_sems.at[a, s],
                                            device_id=(s // 4, (s // 2) % 2, s % 2), device_id_type=MESH)
                pl.when((c == h) & (me != s))(lambda: arrived().wait_recv())
        for is_self, local, remote in copies(ins, outs, sems):
            pl.when(is_self)(lambda: local().wait())
            pl.when(jnp.logical_not(is_self))(lambda: remote().wait_send())

    return dict(ins=list(bufs), prev=list(prev), out_shape=[SDS((N_DEV,) + b.shape[1:], b.dtype) for b in bufs],
                scratch=[pltpu.SemaphoreType.DMA((na, N_CHIP)), pltpu.SemaphoreType.DMA((na, N_DEV)), pltpu.SemaphoreType.DMA((na,))],
                start=start, finish=finish)


def _sum_sources(parts, *, name):
    _, R, C = parts.shape
    tr = _div_tile(R, 256, 16)

    def body(p_ref, o_ref):
        acc = p_ref[0].astype(F32)
        for s in range(1, N_DEV):
            acc = acc + p_ref[s].astype(F32)
        o_ref[...] = acc

    return _pc(body, name=name, grid=(R // tr,), in_specs=[BS((N_DEV, tr, C), lambda i: (0, i, 0))],
               out_specs=BS((tr, C), lambda i: (i, 0)), out_shape=SDS((R, C), F32))(parts)


def _join_halves(halves, *, name):
    na = len(halves)

    def body(*refs):
        ins, outs = refs[:na], refs[na:2 * na]
        send_sems, recv_sems = refs[2 * na:]
        x, y, c = _place()
        cps = []
        for a in range(na):
            cp = pltpu.make_async_remote_copy(ins[a], outs[a], send_sems.at[a], recv_sems.at[a],
                                              device_id=(x, y, 1 - c), device_id_type=MESH)
            cp.start()
            cps.append(cp)
        for cp in cps:
            cp.wait_recv()
        for cp in cps:
            cp.wait_send()

    any_spec = BS(memory_space=pl.ANY)
    return pl.pallas_call(
        body, name=name, in_specs=[any_spec] * na, out_specs=[any_spec] * na,
        out_shape=[SDS(h.shape, h.dtype) for h in halves],
        scratch_shapes=[pltpu.SemaphoreType.DMA((na,)), pltpu.SemaphoreType.DMA((na,))],
        compiler_params=pltpu.CompilerParams(has_side_effects=True))(*halves)


def _allreduce_small(vec, *, name):
    R = vec.shape[0]

    def body(v_ref, o_ref, all_ref, send_sems, recv_sems):
        x, y, c = _place()
        me = 4 * x + 2 * y + c
        all_ref[me] = v_ref[...]
        sends = []
        for r in range(1, N_DEV):
            tgt = (x ^ (r >> 2), y ^ ((r >> 1) & 1), c ^ (r & 1))
            cp = pltpu.make_async_remote_copy(v_ref, all_ref.at[me], send_sems.at[r - 1], recv_sems.at[r - 1],
                                              device_id=tgt, device_id_type=MESH)
            cp.start()
            sends.append(cp)
        for r in range(1, N_DEV):
            tx, ty, tc = x ^ (r >> 2), y ^ ((r >> 1) & 1), c ^ (r & 1)
            got = all_ref.at[4 * tx + 2 * ty + tc]
            pltpu.make_async_remote_copy(got, got, send_sems.at[r - 1], recv_sems.at[r - 1],
                                         device_id=(tx, ty, tc), device_id_type=MESH).wait_recv()
        for cp in sends:
            cp.wait_send()
        acc = all_ref[0]
        for s in range(1, N_DEV):
            acc = acc + all_ref[s]
        o_ref[...] = acc

    vm = BS(memory_space=pltpu.VMEM)
    return pl.pallas_call(
        body, name=name, in_specs=[vm], out_specs=vm, out_shape=SDS((R, LANE), F32),
        scratch_shapes=[pltpu.VMEM((N_DEV, R, LANE), F32), pltpu.SemaphoreType.DMA((N_DEV - 1,)), pltpu.SemaphoreType.DMA((N_DEV - 1,))],
        compiler_params=pltpu.CompilerParams(has_side_effects=True, vmem_limit_bytes=VMEM_LIMIT_BYTES))(vec)


def _pack(arrs):
    rows = []
    for a in arrs:
        f = a.reshape(-1).astype(F32)
        n = -(-f.shape[0] // LANE) * LANE
        rows.append(jnp.pad(f, (0, n - f.shape[0])).reshape(-1, LANE))
    v = jnp.concatenate(rows, axis=0)
    pad = -v.shape[0] % 8
    return jnp.pad(v, ((0, pad), (0, 0)))


def _unpack(v, shapes):
    out, r = [], 0
    for s in shapes:
        n = int(np.prod(s)) if len(s) else 1
        nr = -(-n // LANE)
        out.append(v[r:r + nr].reshape(-1)[:n].reshape(s))
        r += nr
    return out


def _perm_in_cols(w_full):
    z, xbc, dt, q, k, v = (w_full[..., :Z_END], w_full[..., Z_END:XBC_END], w_full[..., XBC_END:DT_END],
                           w_full[..., DT_END:Q_END], w_full[..., Q_END:K_END], w_full[..., K_END:])
    pad = jnp.zeros(dt.shape[:-1] + (LANE - dt.shape[-1],), dt.dtype)
    return jnp.concatenate([q, z, xbc, k, v, dt, pad], axis=-1)


def _unperm_in_cols(g):
    q, z, xbc, k, v, dt = (g[..., :1024], g[..., 1024:2048], g[..., 2048:3584], g[..., 3584:3840], g[..., 3840:4096],
                           g[..., 4096:4096 + 2 * SSM_HEADS])
    return jnp.concatenate([z, xbc, dt, q, k, v], axis=-1)


def _dt_cols(a):
    return jnp.pad(a.reshape(1, 2 * SSM_HEADS), ((0, 0), (0, LANE - 2 * SSM_HEADS)))


def _dt_fwd(proj, dtb, *, name):
    B, L, _ = proj.shape

    def body(p_ref, b_ref, o_ref):
        o_ref[0] = _softplus(p_ref[0] + b_ref[...])

    return _pc(body, name=name, grid=(B,),
               in_specs=[BS((1, L, LANE), lambda b: (b, 0, P_COLS // LANE - 1)), BS((1, LANE), lambda b: (0, 0))],
               out_specs=BS((1, L, LANE), lambda b: (b, 0, 0)), out_shape=SDS((B, L, LANE), F32))(proj, dtb)


def _dt_bwd(proj, dtb, ddt, *, name):
    B, L, _ = proj.shape

    def body(p_ref, b_ref, g_ref, o_ref, db_ref):
        g = g_ref[0] * _sigmoid(p_ref[0] + b_ref[...])
        o_ref[0] = g.astype(BF16)
        db_ref[0] = jnp.sum(g, axis=0, keepdims=True)

    row = BS((1, L, LANE), lambda b: (b, 0, 0))
    return _pc(body, name=name, grid=(B,),
               in_specs=[BS((1, L, LANE), lambda b: (b, 0, P_COLS // LANE - 1)), BS((1, LANE), lambda b: (0, 0)), row],
               out_specs=[row, BS((1, 1, LANE), lambda b: (b, 0, 0))],
               out_shape=[SDS((B, L, LANE), BF16), SDS((B, 1, LANE), F32)])(proj, dtb, ddt)


def _layer_fwd(i, x, wts, small, band_bias, riders=None, arrived=None):
    riders = riders or {}
    B, L, D = x.shape
    T = B * L
    x2 = x.reshape(T, D)
    h, proj2 = _norm_mm(x2, small["norm1_w"][i], wts["w_in"], name=f"in_proj_{i}", tn=1408)
    proj = proj2.reshape(B, L, P_COLS)
    xbc_act = _conv_fwd(proj, 2048 // 256, small["conv_w"][i], small["conv_b"][i], taps=SSM_CONV, ct=256,
                        out_dtype=F32, name=f"ssm_conv_{i}")
    dtb, alog = _dt_cols(small["dt_bias"][i]), _dt_cols(small["a_log"][i])
    dskip = jnp.repeat(small["d_skip"][i], HEAD_DIM).reshape(1, SSM_WIDTH)
    dtc = _dt_fwd(proj, dtb, name=f"dt_{i}")
    ypre, hs, carried = _ssd_fwd(xbc_act, dtc, alog, dskip, name=f"ssd_{i}", rider=riders.get("ssd"))
    if carried is not None:
        arrived("ssd", carried)
    y_ssm = _gate_fwd(ypre.reshape(T, SSM_WIDTH), proj2, small["ssm_norm_w"][i], name=f"gate_{i}")
    sinkcol = jnp.repeat(small["attn_sink"][i], BLOCK).reshape(ATTN_HEADS * BLOCK, 1)
    y_attn, carried = _attn_fwd(proj, band_bias, sinkcol, name=f"attn_{i}", rider=riders.get("attn"))
    if carried is not None:
        arrived("attn", carried)
    y_attn = y_attn.reshape(T, D)
    w_out = wts["w_out"]
    x_mid = _mm(y_ssm, w_out[:SSM_WIDTH], add=x2, name=f"out_proj_a_{i}")
    x_mid = _mm(y_attn, w_out[SSM_WIDTH:], add=x_mid, name=f"out_proj_b_{i}")
    h2, gu2 = _norm_mm(x_mid, small["norm2_w"][i], wts["w_up"], name=f"up_proj_{i}", out_dtype=BF16, tn=1408)
    gu = gu2.reshape(B, L, 2 * D_FF)
    act = _conv_fwd(gu, 0, small["ffn_conv_w"][i], small["ffn_conv_b"][i], taps=FFN_CONV, ct=256, gate_blk0=D_FF // 256,
                    out_dtype=BF16, name=f"ffn_conv_{i}")
    x_out = _mm(act.reshape(T, D_FF), wts["w_down"], add=x_mid, name=f"down_proj_{i}", tk=1408)
    saved = dict(x2=x2, h=h, proj2=proj2, xbc_act=xbc_act, dtb=dtb, dtc=dtc, alog=alog, dskip=dskip, ypre=ypre, hs=hs, y_ssm=y_ssm,
                 sinkcol=sinkcol, y_attn=y_attn, x_mid=x_mid, h2=h2, gu=gu, act=act)
    return x_out.reshape(B, L, D), saved


def _layer_bwd(i, dx_out, dxb, sv, wts, small, band_bias, attn_rider=None, ssd_rider=None, arrived=None):
    T, D = dx_out.shape
    B, L = sv["gu"].shape[:2]
    g = {}
    dact = _mm(dxb, wts["w_down"], tb=True, out_dtype=BF16, name=f"d_act_{i}", tn=1408)
    g["w_down"] = _mm(sv["act"].reshape(T, D_FF), dxb, ta=True, name=f"dw_down_{i}", tm=1408)
    dg, du, dcw, dcb = _conv_bwd(sv["gu"], 0, small["ffn_conv_w"][i], small["ffn_conv_b"][i], dact.reshape(B, L, D_FF),
                                 taps=FFN_CONV, ct=256, gate_blk0=D_FF // 256, name=f"d_ffn_conv_{i}")
    g["ffn_conv_w"] = jnp.sum(dcw, axis=0)[:FFN_CONV]
    g["ffn_conv_b"] = jnp.sum(dcb, axis=(0, 1))
    dgu = jnp.concatenate([dg, du], axis=-1).reshape(T, 2 * D_FF)
    dx_mid, dmb, dw2 = _mm_dnorm(dgu, wts["w_up"], sv["x_mid"], small["norm2_w"][i], dx_out, name=f"d_h2_{i}", tk=1408)
    g["w_up"] = _mm(sv["h2"], dgu, ta=True, name=f"dw_up_{i}", tn=1408)
    g["norm2_w"] = dw2[0]
    w_out = wts["w_out"]
    dy_ssm = _mm(dmb, w_out[:SSM_WIDTH], tb=True, name=f"d_y_ssm_{i}")
    dy_attn = _mm(dmb, w_out[SSM_WIDTH:], tb=True, out_dtype=BF16, name=f"d_y_attn_{i}")
    g["w_out"] = jnp.concatenate([_mm(sv["y_ssm"], dmb, ta=True, name=f"dw_out_a_{i}"),
                                  _mm(sv["y_attn"], dmb, ta=True, name=f"dw_out_b_{i}")], axis=0)
    dypre, dz, dwn = _gate_bwd(sv["ypre"].reshape(T, SSM_WIDTH), sv["proj2"], small["ssm_norm_w"][i], dy_ssm, name=f"d_gate_{i}")
    g["ssm_norm_w"] = jnp.sum(dwn, axis=(0, 1))
    proj = sv["proj2"].reshape(B, L, P_COLS)
    dxs, dbm, dcm, ddt, dalog, ddsk, carried = _ssd_bwd(sv["xbc_act"], sv["dtc"], sv["alog"], sv["dskip"], sv["hs"],
                                                        dypre.reshape(B, L, SSM_WIDTH), name=f"d_ssd_{i}",
                                                        rider=ssd_rider(g) if ssd_rider is not None else None)
    if carried is not None:
        arrived("ssd", carried)
    ddt, ddtb = _dt_bwd(proj, sv["dtb"], ddt, name=f"d_dt_{i}")
    g["dt_bias"] = jnp.sum(ddtb, axis=(0, 1))[:2 * SSM_HEADS].reshape(2, SSM_HEADS)
    g["a_log"] = jnp.sum(dalog, axis=(0, 1, 2))[:2 * SSM_HEADS].reshape(2, SSM_HEADS)
    g["d_skip"] = jnp.sum(ddsk.reshape(B, SSM_HEADS, HEAD_DIM), axis=(0, 2))
    dxbc_act = jnp.concatenate([dxs, dbm, dcm], axis=-1)
    dxbc, dcw, dcb = _conv_bwd(proj, 2048 // 256, small["conv_w"][i], small["conv_b"][i], dxbc_act, taps=SSM_CONV, ct=256,
                               name=f"d_ssm_conv_{i}")
    g["conv_w"] = jnp.sum(dcw, axis=0)[:SSM_CONV]
    g["conv_b"] = jnp.sum(dcb, axis=(0, 1))
    dq, dk, dv, dbias, dsink, carried = _attn_bwd(proj, band_bias, sv["sinkcol"], dy_attn.reshape(B, L, D), name=f"d_attn_{i}",
                                                  rider=attn_rider)
    if carried is not None:
        arrived("attn", carried)
    g["attn_sink"] = jnp.sum(dsink.reshape(ATTN_HEADS, BLOCK), axis=1)
    dproj = jnp.concatenate([dq, dz.reshape(B, L, SSM_WIDTH), dxbc, dk.astype(BF16), dv.astype(BF16), ddt], axis=-1).reshape(T, P_COLS)
    dx_in, dx_in_b, dw1 = _mm_dnorm(dproj, wts["w_in"], sv["x2"], small["norm1_w"][i], dx_mid, name=f"d_h_{i}", tk=1408)
    g["w_in"] = _unperm_in_cols(_mm(sv["h"], dproj, ta=True, name=f"dw_in_{i}", tn=1408))
    g["norm1_w"] = dw1[0]
    return dx_in, dx_in_b, g, dbias


_BIG = ("w_in", "w_out", "w_up", "w_down")
_BIG_AXIS = {"w_in": 2, "w_out": 1, "w_up": 2, "w_down": 1}
_SMALL = ("rel_bias", "norm1_w", "conv_w", "conv_b", "dt_bias", "a_log", "d_skip", "ssm_norm_w", "attn_sink", "norm2_w",
          "ffn_conv_w", "ffn_conv_b", "final_norm_w")
_SMALL_SHARDED = ("conv_w", "ffn_conv_w")
_ORDER = ("rel_bias", "norm1_w", "w_in", "conv_w", "conv_b", "dt_bias", "a_log", "d_skip", "ssm_norm_w", "attn_sink", "w_out",
          "norm2_w", "w_up", "ffn_conv_w", "ffn_conv_b", "w_down", "final_norm_w")


def _local_step(x, target, small, wts=None, exchange=None):
    B, L, D = x.shape
    bucket = _bucket_table()
    band_bias = _bias_expand(small["rel_bias"], bucket, name="band_bias")

    def fetch(spec):
        return exchange["gather"](spec) if exchange is not None and spec else None

    def fetched(spec, carried):
        for (i, k), full in zip(spec, exchange["weights"](spec, carried)):
            wts[i][k] = full

    if exchange is not None:
        wts = [{} for _ in range(DEPTH)]
        fetched([(0, "w_in")], _run_rider(fetch([(0, "w_in")]), name="gather_w_in_0"))
    saved = []
    for i in range(DEPTH):
        nxt = i + 1 < DEPTH
        if i == 0:
            plan = {"ssd": [(0, "w_out"), (0, "w_up"), (0, "w_down")], "attn": [(1, "w_in"), (1, "w_out"), (1, "w_up")] if nxt else []}
        else:
            plan = {"ssd": [(i, "w_down")] + ([(i + 1, "w_in"), (i + 1, "w_out")] if nxt else []), "attn": [(i + 1, "w_up")] if nxt else []}
        x, sv = _layer_fwd(i, x, wts[i], small, band_bias, {c: fetch(s) for c, s in plan.items()}, lambda c, r: fetched(plan[c], r))
        saved.append(sv)
    loss, dx, dxb, dwf = _loss_head(x.reshape(B * L, D), small["final_norm_w"], target.reshape(B * L, D), name="loss_head")
    per_layer = []
    dbias = jnp.zeros((ATTN_HEADS, BLOCK, KEY_SPAN), F32)
    late = None
    for i in reversed(range(DEPTH)):
        own = [(i, "w_down"), (i, "w_up"), (i, "w_out")]
        plan = {"ssd": own, "attn": late[0] if late else []}
        attn_rider = exchange["scatter"](*late) if late else None
        ssd_rider = (lambda g: exchange["scatter"](own, [g[k] for _, k in own])) if exchange is not None else None
        dx, dxb, g, dbias_i = _layer_bwd(i, dx, dxb, saved[i], wts[i], small, band_bias, attn_rider, ssd_rider,
                                    lambda c, r: exchange["collect"](plan[c], r))
        if exchange is not None:
            late = ([(i, "w_in")], [g["w_in"]])
            for k in _BIG:
                g.pop(k)
        dbias = dbias + dbias_i
        per_layer.append(g)
    if exchange is not None:
        exchange["collect"](late[0], _run_rider(exchange["scatter"](*late), name="scatter_dw_in_0"))
    per_layer.reverse()
    grads = {k: jnp.stack([g[k] for g in per_layer]) for k in per_layer[0]}
    grads["rel_bias"] = _bias_reduce(dbias, bucket, name="d_rel_bias")
    grads["final_norm_w"] = dwf[0]
    return loss, dx.reshape(B, L, D), grads


def _split_by_chip(g, axis):
    shp = g.shape
    n = shp[axis] // N_CHIP
    g = g.reshape(shp[:axis] + (N_CHIP, n) + shp[axis + 1:])
    return jnp.moveaxis(g, axis, 0)


def _join_chips(a, axis):
    a = jnp.moveaxis(a, 0, axis)
    shp = a.shape
    return a.reshape(shp[:axis] + (shp[axis] * shp[axis + 1],) + shp[axis + 2:])


def kernel(x, rel_bias, norm1_w, w_in, conv_w, conv_b, dt_bias, a_log, d_skip, ssm_norm_w, attn_sink, w_out, norm2_w, w_up, ffn_conv_w, ffn_conv_b, w_down, final_norm_w, loss_target, m_rel_bias, m_norm1_w, m_w_in, m_conv_w, m_conv_b, m_dt_bias, m_a_log, m_d_skip, m_ssm_norm_w, m_attn_sink, m_w_out, m_norm2_w, m_w_up, m_ffn_conv_w, m_ffn_conv_b, m_w_down, m_final_norm_w, v_rel_bias, v_norm1_w, v_w_in, v_conv_w, v_conv_b, v_dt_bias, v_a_log, v_d_skip, v_ssm_norm_w, v_attn_sink, v_w_out, v_norm2_w, v_w_up, v_ffn_conv_w, v_ffn_conv_b, v_w_down, v_final_norm_w):
    w = dict(rel_bias=rel_bias, norm1_w=norm1_w, w_in=w_in, conv_w=conv_w, conv_b=conv_b, dt_bias=dt_bias, a_log=a_log,
             d_skip=d_skip, ssm_norm_w=ssm_norm_w, attn_sink=attn_sink, w_out=w_out, norm2_w=norm2_w, w_up=w_up,
             ffn_conv_w=ffn_conv_w, ffn_conv_b=ffn_conv_b, w_down=w_down, final_norm_w=final_norm_w)
    m = dict(rel_bias=m_rel_bias, norm1_w=m_norm1_w, w_in=m_w_in, conv_w=m_conv_w, conv_b=m_conv_b, dt_bias=m_dt_bias,
             a_log=m_a_log, d_skip=m_d_skip, ssm_norm_w=m_ssm_norm_w, attn_sink=m_attn_sink, w_out=m_w_out, norm2_w=m_norm2_w,
             w_up=m_w_up, ffn_conv_w=m_ffn_conv_w, ffn_conv_b=m_ffn_conv_b, w_down=m_w_down, final_norm_w=m_final_norm_w)
    v = dict(rel_bias=v_rel_bias, norm1_w=v_norm1_w, w_in=v_w_in, conv_w=v_conv_w, conv_b=v_conv_b, dt_bias=v_dt_bias,
             a_log=v_a_log, d_skip=v_d_skip, ssm_norm_w=v_ssm_norm_w, attn_sink=v_attn_sink, w_out=v_w_out, norm2_w=v_norm2_w,
             w_up=v_w_up, ffn_conv_w=v_ffn_conv_w, ffn_conv_b=v_ffn_conv_b, w_down=v_w_down, final_norm_w=v_final_norm_w)
    my_chip = 2 * lax.axis_index("x") + lax.axis_index("y")

    shards = {k: w[k].astype(BF16) for k in _BIG}
    received = {}

    def gather(spec):
        return _gather_rider([shards[k][i] for i, k in spec])

    def weights(spec, carried):
        out = []
        for (i, k), g_ in zip(spec, carried):
            full = _join_chips(lax.dynamic_update_index_in_dim(g_, shards[k][i], my_chip, 0), _BIG_AXIS[k] - 1)
            out.append(_perm_in_cols(full) if k == "w_in" else full)
        return out

    def scatter(spec, grads):
        layer = spec[0][0]
        bufs = [_split_by_chip(g_, _BIG_AXIS[k] - 1).astype(BF16) for (_, k), g_ in zip(spec, grads)]
        prev = [received[(layer % 2, k)] for _, k in spec] if layer + 2 < DEPTH else []
        return _scatter_rider(bufs, layer, prev)

    def collect(spec, carried):
        for (i, k), pieces in zip(spec, carried):
            received[(i % 2, k)] = pieces

    conv_shapes = [(DEPTH, SSM_CONV, CONV_CH), (DEPTH, FFN_CONV, D_FF)]
    placed = [lax.dynamic_update_slice_in_dim(jnp.zeros(s, F32), w[k], my_chip * w[k].shape[2], axis=2)
              for k, s in zip(_SMALL_SHARDED, conv_shapes)]
    lead = (lax.axis_index("c") == 0).astype(F32)
    conv_full = _unpack(_allreduce_small(_pack([p * lead for p in placed]), name="gather_conv_weights"), conv_shapes)
    small = {k: w[k] for k in _SMALL}
    small["conv_w"], small["ffn_conv_w"] = conv_full

    loss_part, grad_x, gp = _local_step(x, loss_target, small,
                                        exchange=dict(gather=gather, weights=weights, scatter=scatter, collect=collect))

    small_shapes = [small[k].shape for k in _SMALL] + [()]
    red = _unpack(_allreduce_small(_pack([gp[k] for k in _SMALL] + [loss_part[0, :1]]), name="reduce_small"), small_shapes)
    gsmall = dict(zip(_SMALL, red[:-1]))
    loss = red[-1]
    for k in _SMALL_SHARDED:
        n = w[k].shape[2]
        gsmall[k] = lax.dynamic_slice_in_dim(gsmall[k], my_chip * n, n, axis=2)

    core = lax.axis_index("c")
    half = DEPTH // 2
    halves = [jnp.stack([_sum_sources(received[(p, k)], name=f"sum_{k}_{p}") for p in range(half)]) for k in _BIG]
    others = _join_halves(halves, name="join_halves")
    gbig = {}
    for k, mine_, theirs_ in zip(_BIG, halves, others):
        full = jnp.zeros((DEPTH,) + mine_.shape[1:], F32)
        full = lax.dynamic_update_slice_in_dim(full, mine_, core * half, axis=0)
        gbig[k] = lax.dynamic_update_slice_in_dim(full, theirs_, (1 - core) * half, axis=0)

    grad, delta, new_m, new_v = {}, {}, {}, {}
    for k in _BIG:
        shp = w[k].shape
        two = lambda a: a.reshape(shp[0] * shp[1], shp[2])
        d_, m_, v_ = _adamw(two(w[k]), two(gbig[k]), two(m[k]), two(v[k]), name=f"adamw_{k}")
        grad[k], delta[k], new_m[k], new_v[k] = gbig[k], d_.reshape(shp), m_.reshape(shp), v_.reshape(shp)
    shapes = [w[k].shape for k in _SMALL]
    d_, m_, v_ = _adamw(_pack([w[k] for k in _SMALL]), _pack([gsmall[k] for k in _SMALL]), _pack([m[k] for k in _SMALL]),
                        _pack([v[k] for k in _SMALL]), name="adamw_small")
    for k, a, b_, c_ in zip(_SMALL, _unpack(d_, shapes), _unpack(m_, shapes), _unpack(v_, shapes)):
        grad[k], delta[k], new_m[k], new_v[k] = gsmall[k], a, b_, c_
    return (loss, grad_x, *[grad[k] for k in _ORDER], *[delta[k] for k in _ORDER], *[new_m[k] for k in _ORDER],
            *[new_v[k] for k in _ORDER])
```

```python
import functools
import math

import jax
import jax.numpy as jnp
import numpy as np
from jax import lax
from jax.experimental import pallas as pl
from jax.experimental.pallas import tpu as pltpu

F32 = jnp.float32
BF16 = jnp.bfloat16
BS = pl.BlockSpec
SDS = jax.ShapeDtypeStruct
MESH = pl.DeviceIdType.MESH

D_MODEL = 1024
DEPTH = 4
SSM_HEADS = 16
SSM_WIDTH = 1024
BC_WIDTH = 256
CONV_CH = 1536
SSM_CONV = 7
CHUNK = 128
ATTN_HEADS = 16
KV_HEADS = 4
HEAD_DIM = 64
WINDOW = 128
BLOCK = 128
KEY_SPAN = 384
REL_BUCKETS = 32
REL_MAX_DIST = 128
D_FF = 2816
FFN_CONV = 3
NORM_EPS = 1e-6
Z_END = 1024
XBC_END = 2560
DT_END = 2592
Q_END = 3616
K_END = 3872
IN_COLS = 4128
P_COLS = 4224
ADAM_LR, ADAM_B1, ADAM_B2, ADAM_EPS, ADAM_WD, ADAM_STEP = 0.001, 0.9, 0.999, 1e-08, 0.01, 10
NEG = -1e30
N_DEV = 8
N_CHIP = 4
LANE = 128
VMEM_LIMIT_BYTES = 48 * 1024 * 1024


def _pc(body, *, name, grid, in_specs, out_specs, out_shape, scratch_shapes=()):
    return pl.pallas_call(
        body, name=name, grid=grid, in_specs=in_specs, out_specs=out_specs, out_shape=out_shape,
        scratch_shapes=list(scratch_shapes),
        compiler_params=pltpu.CompilerParams(dimension_semantics=("arbitrary",) * len(grid),
                                             vmem_limit_bytes=VMEM_LIMIT_BYTES))


def _split_rider_refs(refs, n_in, n_out, rider):
    n_rin = len(rider["ins"]) + len(rider["prev"])
    n_rout = len(rider["out_shape"])
    core = refs[:n_in] + refs[n_in + n_rin:n_in + n_rin + n_out] + refs[n_in + n_rin + n_out + n_rout + len(rider["scratch"]):]
    rins = refs[n_in:n_in + len(rider["ins"])]
    routs = refs[n_in + n_rin + n_out:n_in + n_rin + n_out + n_rout]
    sems = refs[n_in + n_rin + n_out + n_rout:n_in + n_rin + n_out + n_rout + len(rider["scratch"])]
    return core, rins, routs, sems


def _pc_carry(body, args, *, name, grid, in_specs, out_specs, out_shape, scratch_shapes=(), rider=None):
    if rider is None:
        return _pc(body, name=name, grid=grid, in_specs=in_specs, out_specs=out_specs, out_shape=out_shape,
                   scratch_shapes=scratch_shapes)(*args), None
    n_in, n_out = len(in_specs), len(out_shape)
    any_spec = BS(memory_space=pl.ANY)

    def full(*refs):
        core, rins, routs, sems = _split_rider_refs(refs, n_in, n_out, rider)
        ids = [pl.program_id(d) for d in range(len(grid))]
        first = functools.reduce(jnp.logical_and, [i == 0 for i in ids])
        last = functools.reduce(jnp.logical_and, [i == g - 1 for i, g in zip(ids, grid)])

        @pl.when(first)
        def _():
            rider["start"](rins, routs, sems)

        body(*core)

        @pl.when(last)
        def _():
            rider["finish"](rins, routs, sems)

    n_rin = len(rider["ins"])
    outs = pl.pallas_call(
        full, name=name, grid=grid,
        in_specs=list(in_specs) + [any_spec] * (n_rin + len(rider["prev"])),
        out_specs=list(out_specs) + [any_spec] * len(rider["out_shape"]),
        out_shape=list(out_shape) + list(rider["out_shape"]),
        scratch_shapes=list(rider["scratch"]) + list(scratch_shapes),
        input_output_aliases={n_in + n_rin + t: n_out + t for t in range(len(rider["prev"]))},
        compiler_params=pltpu.CompilerParams(dimension_semantics=("arbitrary",) * len(grid), vmem_limit_bytes=VMEM_LIMIT_BYTES,
                                             has_side_effects=True))(*args, *rider["ins"], *rider["prev"])
    return outs[:n_out], outs[n_out:]


def _run_rider(rider, *, name):
    any_spec = BS(memory_space=pl.ANY)
    n_rin = len(rider["ins"])

    def body(*refs):
        _, rins, routs, sems = _split_rider_refs(refs, 0, 0, rider)
        rider["start"](rins, routs, sems)
        rider["finish"](rins, routs, sems)

    return pl.pallas_call(
        body, name=name, in_specs=[any_spec] * (n_rin + len(rider["prev"])), out_specs=[any_spec] * len(rider["out_shape"]),
        out_shape=list(rider["out_shape"]), scratch_shapes=list(rider["scratch"]),
        input_output_aliases={n_rin + t: t for t in range(len(rider["prev"]))},
        compiler_params=pltpu.CompilerParams(has_side_effects=True))(*rider["ins"], *rider["prev"])


def _div_tile(n, pref, mult):
    t = min(pref, n)
    t -= t % mult
    while t >= mult:
        if n % t == 0:
            return t
        t -= mult
    return n


def _mm(a, b, *, name, ta=False, tb=False, add=None, out_dtype=F32, tm=1024, tn=1024, tk=1024):
    if ta:
        K, M = a.shape
    else:
        M, K = a.shape
    N = b.shape[0] if tb else b.shape[1]
    tm, tn, tk = _div_tile(M, tm, LANE), _div_tile(N, tn, LANE), _div_tile(K, tk, LANE)
    nk = K // tk
    dims = (((0,) if ta else (1,), (1,) if tb else (0,)), ((), ()))

    def body_single(*refs):
        r = lax.dot_general(refs[0][...], refs[1][...], dims, preferred_element_type=F32)
        if add is not None:
            r = r + refs[2][...]
        refs[-1][...] = r.astype(out_dtype)

    def body(*refs):
        if add is None:
            a_ref, b_ref, o_ref, acc_ref = refs
        else:
            a_ref, b_ref, add_ref, o_ref, acc_ref = refs
        k = pl.program_id(2)

        @pl.when(k == 0)
        def _():
            acc_ref[...] = jnp.zeros_like(acc_ref)

        acc_ref[...] += lax.dot_general(a_ref[...], b_ref[...], dims, preferred_element_type=F32)

        @pl.when(k == nk - 1)
        def _():
            r = acc_ref[...]
            if add is not None:
                r = r + add_ref[...]
            o_ref[...] = r.astype(out_dtype)

    a_spec = BS((tk, tm), lambda i, j, k: (k, i)) if ta else BS((tm, tk), lambda i, j, k: (i, k))
    b_spec = BS((tn, tk), lambda i, j, k: (j, k)) if tb else BS((tk, tn), lambda i, j, k: (k, j))
    in_specs, args = [a_spec, b_spec], [a, b]
    if add is not None:
        in_specs.append(BS((tm, tn), lambda i, j, k: (i, j)))
        args.append(add)
    return _pc(body_single if nk == 1 else body, name=name, grid=(M // tm, N // tn, nk), in_specs=in_specs,
               out_specs=BS((tm, tn), lambda i, j, k: (i, j)), out_shape=SDS((M, N), out_dtype),
               scratch_shapes=[] if nk == 1 else [pltpu.VMEM((tm, tn), F32)])(*args)


def _dot(a, b, dims):
    return lax.dot_general(a.astype(BF16), b.astype(BF16), (dims, ((), ())), preferred_element_type=F32)


@jax.custom_vjp
def _nn(a, b):
    return _dot(a, b, ((1,), (0,)))


@jax.custom_vjp
def _nt(a, b):
    return _dot(a, b, ((1,), (1,)))


@jax.custom_vjp
def _tn(a, b):
    return _dot(a, b, ((0,), (0,)))


_nn.defvjp(lambda a, b: (_nn(a, b), (a, b)), lambda r, g: (_nt(g, r[1]), _tn(r[0], g)))
_nt.defvjp(lambda a, b: (_nt(a, b), (a, b)), lambda r, g: (_nn(g, r[1]), _tn(g, r[0])))
_tn.defvjp(lambda a, b: (_tn(a, b), (a, b)), lambda r, g: (_nt(r[1], g), _nn(r[0], g)))


def _hdot(m, x):
    hi = x.astype(BF16)
    r1 = x - hi.astype(F32)
    lo = r1.astype(BF16)
    lo2 = (r1 - lo.astype(F32)).astype(BF16)
    n = x.shape[1]
    out = lax.dot_general(m.astype(BF16), jnp.concatenate([hi, lo, lo2], axis=1), (((1,), (0,)), ((), ())),
                          preferred_element_type=F32)
    return out[:, :n] + out[:, n:2 * n] + out[:, 2 * n:]


@jax.custom_vjp
def _cumdot(m, mt, x):
    return _hdot(m, x)


_cumdot.defvjp(lambda m, mt, x: (_hdot(m, x), (m, mt)),
               lambda r, g: (jnp.zeros_like(r[0]), jnp.zeros_like(r[1]), _hdot(r[1], g)))


def _sigmoid(x):
    return 1.0 / (1.0 + jnp.exp(-x))


def _softplus(x):
    return jnp.maximum(x, 0.0) + jnp.log(1.0 + jnp.exp(-jnp.abs(x)))


def _rms(x, w):
    return x * lax.rsqrt(jnp.mean(x * x, axis=-1, keepdims=True) + NORM_EPS) * w


def _norm_mm(x2, nw, b, *, name, out_dtype=F32, tm=1024, tn=1024):
    T, D = x2.shape
    N = b.shape[1]
    tm, tn = _div_tile(T, tm, LANE), _div_tile(N, tn, LANE)

    def body(x_ref, w_ref, b_ref, h_ref, o_ref):
        @pl.when(pl.program_id(1) == 0)
        def _():
            h_ref[...] = _rms(x_ref[...], w_ref[...]).astype(BF16)

        o_ref[...] = lax.dot_general(h_ref[...], b_ref[...], (((1,), (0,)), ((), ())), preferred_element_type=F32).astype(out_dtype)

    return _pc(body, name=name, grid=(T // tm, N // tn),
               in_specs=[BS((tm, D), lambda i, j: (i, 0)), BS((1, D), lambda i, j: (0, 0)), BS((D, tn), lambda i, j: (0, j))],
               out_specs=[BS((tm, D), lambda i, j: (i, 0)), BS((tm, tn), lambda i, j: (i, j))],
               out_shape=[SDS((T, D), BF16), SDS((T, N), out_dtype)])(x2, nw.reshape(1, D), b)


def _mm_dnorm(a, b, x2, nw, resid, *, name, tk=1024):
    T, D = x2.shape
    dh = _mm(a, b, tb=True, name=name, tk=tk)
    tr = _div_tile(T, 512, 16)

    def body(x_ref, w_ref, dh_ref, r_ref, dx_ref, dxb_ref, dw_ref):
        _, vjp = jax.vjp(_rms, x_ref[...], w_ref[...])
        dx, dw = vjp(dh_ref[...])
        dx = dx + r_ref[...]
        dx_ref[...] = dx
        dxb_ref[...] = dx.astype(BF16)

        @pl.when(pl.program_id(0) == 0)
        def _():
            dw_ref[...] = jnp.zeros_like(dw_ref)

        dw_ref[...] += dw

    row = BS((tr, D), lambda i: (i, 0))
    one = BS((1, D), lambda i: (0, 0))
    return _pc(body, name=name + "_norm", grid=(T // tr,), in_specs=[row, one, row, row], out_specs=[row, row, one],
               out_shape=[SDS((T, D), F32), SDS((T, D), BF16), SDS((1, D), F32)])(x2, nw.reshape(1, D), dh, resid)


ROW_PAD = 8


def _pad_rows(x):
    return jnp.concatenate([x, jnp.zeros((ROW_PAD, x.shape[1]), x.dtype)], axis=0)


def _shift_rows(xp, s):
    n = xp.shape[0] - ROW_PAD
    return xp[:n] if s == 0 else pltpu.roll(xp, (-s) % xp.shape[0], 0)[:n]


def _conv_taps(x, taps):
    xp = _pad_rows(x)
    return [_shift_rows(xp, k - taps // 2) for k in range(taps)]


def _conv_pre(xs, w_ref, b_ref):
    c = b_ref[...] + w_ref[0:1, :] * xs[0]
    for k in range(1, len(xs)):
        c = c + w_ref[k:k + 1, :] * xs[k]
    return c


def _conv_fwd(x3, x_blk0, w, b, *, taps, ct, gate_blk0=None, out_dtype, name):
    B, L, _ = x3.shape
    C = w.shape[1]
    wp = jnp.zeros((8, C), F32).at[:taps].set(w)

    def body(*refs):
        if gate_blk0 is None:
            x_ref, w_ref, b_ref, o_ref = refs
        else:
            x_ref, u_ref, w_ref, b_ref, o_ref = refs
        c = _conv_pre(_conv_taps(x_ref[0].astype(F32), taps), w_ref, b_ref)
        y = c * _sigmoid(c)
        if gate_blk0 is not None:
            y = y * u_ref[0].astype(F32)
        o_ref[0] = y.astype(out_dtype)

    in_specs = [BS((1, L, ct), lambda bi, j: (bi, 0, x_blk0 + j))]
    args = [x3]
    if gate_blk0 is not None:
        in_specs.append(BS((1, L, ct), lambda bi, j: (bi, 0, gate_blk0 + j)))
        args.append(x3)
    in_specs += [BS((8, ct), lambda bi, j: (0, j)), BS((1, ct), lambda bi, j: (0, j))]
    args += [wp, b.reshape(1, C)]
    return _pc(body, name=name, grid=(B, C // ct), in_specs=in_specs,
               out_specs=BS((1, L, ct), lambda bi, j: (bi, 0, j)), out_shape=SDS((B, L, C), out_dtype))(*args)


def _conv_bwd(x3, x_blk0, w, b, dy3, *, taps, ct, gate_blk0=None, name):
    B, L, _ = x3.shape
    C = w.shape[1]
    wp = jnp.zeros((8, C), F32).at[:taps].set(w)
    gated = gate_blk0 is not None

    def body(*refs):
        if gated:
            x_ref, u_ref, w_ref, b_ref, dy_ref, dx_ref, du_ref, dw_ref, db_ref = refs
        else:
            x_ref, w_ref, b_ref, dy_ref, dx_ref, dw_ref, db_ref = refs
        xs = _conv_taps(x_ref[0].astype(F32), taps)
        dy = dy_ref[0].astype(F32)
        c = _conv_pre(xs, w_ref, b_ref)
        sg = _sigmoid(c)
        dsilu = sg * (1.0 + c * (1.0 - sg))
        if gated:
            du_ref[0] = (dy * (c * sg)).astype(BF16)
            dc = dy * u_ref[0].astype(F32) * dsilu
        else:
            dc = dy * dsilu
        dcp = _pad_rows(dc)
        dx = jnp.zeros_like(dc)
        dw_ref[0] = jnp.zeros((8, ct), F32)
        for k in range(taps):
            dx = dx + w_ref[k:k + 1, :] * _shift_rows(dcp, taps // 2 - k)
            dw_ref[0, k:k + 1, :] = jnp.sum(dc * xs[k], axis=0, keepdims=True)
        dx_ref[0] = dx.astype(BF16)
        db_ref[0] = jnp.sum(dc, axis=0, keepdims=True)

    xs = BS((1, L, ct), lambda bi, j: (bi, 0, x_blk0 + j))
    ys = BS((1, L, ct), lambda bi, j: (bi, 0, j))
    in_specs, args = [xs], [x3]
    if gated:
        in_specs.append(BS((1, L, ct), lambda bi, j: (bi, 0, gate_blk0 + j)))
        args.append(x3)
    in_specs += [BS((8, ct), lambda bi, j: (0, j)), BS((1, ct), lambda bi, j: (0, j)), ys]
    args += [wp, b.reshape(1, C), dy3]
    out_specs = [ys] + ([ys] if gated else []) + [BS((1, 8, ct), lambda bi, j: (bi, 0, j)), BS((1, 1, ct), lambda bi, j: (bi, 0, j))]
    out_shape = [SDS((B, L, C), BF16)] + ([SDS((B, L, C), BF16)] if gated else []) + [SDS((B, 8, C), F32), SDS((B, 1, C), F32)]
    return _pc(body, name=name, grid=(B, C // ct), in_specs=in_specs, out_specs=out_specs, out_shape=out_shape)(*args)


def _tri(reverse):
    r = lax.broadcasted_iota(jnp.int32, (CHUNK, CHUNK), 0)
    c = lax.broadcasted_iota(jnp.int32, (CHUNK, CHUNK), 1)
    return (c >= r) if reverse else (c <= r)


PAIRS = 2
QUADS = SSM_HEADS // (2 * PAIRS)
QW = PAIRS * LANE


def _ssd_chunk(h0, h1, x0, x1, bm, cm, dtc, alog, *, col0, reverse):
    mask = _tri(reverse)
    eye = lax.broadcasted_iota(jnp.int32, (CHUNK, CHUNK), 0) == lax.broadcasted_iota(jnp.int32, (CHUNK, CHUNK), 1)
    lane = lax.broadcasted_iota(jnp.int32, (1, LANE), 1)
    first = lane < HEAD_DIM
    adt = dtc * (-jnp.exp(alog))
    cumc = _cumdot(mask.astype(F32), _tri(not reverse).astype(F32), adt)
    totc = jnp.sum(adt, axis=0, keepdims=True)
    cb = _nt(cm, bm)

    def col(v, c):
        return jnp.sum(jnp.where(lane == c, v, 0.0), axis=1, keepdims=True)

    outs, states = [], []
    for p, (hprev, xs) in enumerate(((h0, x0), (h1, x1))):
        c0 = col0 + 2 * p
        cj = (col(cumc, c0), col(cumc, c0 + 1))
        cum = jnp.where(first, cj[0], cj[1])
        tot = jnp.where(first, col(totc, c0), col(totc, c0 + 1))
        xdt = xs * jnp.where(first, col(dtc, c0), col(dtc, c0 + 1))
        y = _nn(cm, hprev) * jnp.exp(cum)
        for j in range(2):
            rj = jnp.sum(jnp.where(eye, cj[j], 0.0), axis=0, keepdims=True)
            dec = jnp.exp(jnp.where(mask, cj[j] - rj, NEG))
            y = y + _nn(cb * dec, jnp.where(first if j == 0 else ~first, xdt, 0.0))
        outs.append(y)
        states.append(hprev * jnp.exp(tot) + _tn(bm, xdt * jnp.exp(tot - cum)))
    return outs[0], outs[1], states[0], states[1]


def _ssd_specs(B, L):
    def lanes(w, blk):
        return BS((1, L, w), blk)

    return [
        lanes(QW, lambda b, q: (b, 0, q)),
        lanes(LANE, lambda b, q: (b, 0, 8 + q // 2)),
        lanes(LANE, lambda b, q: (b, 0, 10 + q // 2)),
        lanes(LANE, lambda b, q: (b, 0, 0)),
        BS((1, LANE), lambda b, q: (0, 0)),
        BS((1, QW), lambda b, q: (0, q)),
    ]


def _ssd_slot(d, ci):
    return ci if d == 0 else ci + 1


def _ssd_fwd(xbc_act, dtc, alog, dskip, *, name, rider=None):
    B, L, _ = xbc_act.shape
    nc = L // CHUNK

    def body(xs_ref, b_ref, c_ref, dt_ref, alog_ref, dsk_ref, y_ref, hs_ref):
        q = pl.program_id(1)
        alog_v = alog_ref[...]
        y_ref[0] = dsk_ref[...] * xs_ref[0]
        hs_ref[0, 0, 0, 0] = jnp.zeros((LANE, QW), F32)
        hs_ref[0, 0, 1, nc] = jnp.zeros((LANE, QW), F32)

        def step(i, carry):
            cis = (i, nc - 1 - i)
            rows = [pl.ds(pl.multiple_of(ci * CHUNK, CHUNK), CHUNK) for ci in cis]
            res = []
            for d in range(2):
                cur = _ssd_slot(d, cis[d])
                res.append(_ssd_chunk(
                    hs_ref[0, 0, d, cur, :, :LANE], hs_ref[0, 0, d, cur, :, LANE:], xs_ref[0, rows[d], :LANE],
                    xs_ref[0, rows[d], LANE:], b_ref[0, rows[d], :], c_ref[0, rows[d], :], dt_ref[0, rows[d], :], alog_v,
                    col0=SSM_HEADS * d + 2 * PAIRS * q, reverse=d == 1))
            for d in range(2):
                y0, y1, n0, n1 = res[d]
                nxt = _ssd_slot(d, cis[d] + 1 if d == 0 else cis[d] - 1)
                hs_ref[0, 0, d, nxt, :, :LANE] = n0
                hs_ref[0, 0, d, nxt, :, LANE:] = n1
                y_ref[0, rows[d], :LANE] += y0
                y_ref[0, rows[d], LANE:] += y1
            return carry

        lax.fori_loop(0, nc, step, 0, unroll=2)

    (y, hs), carried = _pc_carry(
        body, (xbc_act, xbc_act, xbc_act, dtc, alog, dskip), name=name, grid=(B, QUADS), in_specs=_ssd_specs(B, L),
        out_specs=[BS((1, L, QW), lambda b, q: (b, 0, q)), BS((1, 1, 2, nc + 1, LANE, QW), lambda b, q: (b, q, 0, 0, 0, 0))],
        out_shape=[SDS((B, L, SSM_WIDTH), F32), SDS((B, QUADS, 2, nc + 1, LANE, QW), F32)], rider=rider)
    return y, hs, carried


def _ssd_bwd(xbc_act, dtc, alog, dskip, hs, dy, *, name, rider=None):
    B, L, _ = xbc_act.shape
    nc = L // CHUNK

    def body(xs_ref, b_ref, c_ref, dt_ref, alog_ref, dsk_ref, hs_ref, dy_ref,
             dxs_ref, db_ref, dc_ref, ddt_ref, dalog_ref, ddsk_ref, dh_ref):
        q = pl.program_id(1)
        alog_v = alog_ref[...]

        @pl.when(q % 2 == 0)
        def _():
            db_ref[...] = jnp.zeros_like(db_ref)
            dc_ref[...] = jnp.zeros_like(dc_ref)

        @pl.when(q == 0)
        def _():
            ddt_ref[...] = jnp.zeros_like(ddt_ref)

        dxs_ref[0] = dy_ref[0] * dsk_ref[...]
        ddsk_ref[0] = jnp.sum(dy_ref[0] * xs_ref[0], axis=0, keepdims=True)
        dh_ref[...] = jnp.zeros_like(dh_ref)

        def step(i, carry):
            g_alog = carry
            cis = (nc - 1 - i, i)
            rows = [pl.ds(pl.multiple_of(ci * CHUNK, CHUNK), CHUNK) for ci in cis]
            res = []
            for d in range(2):
                cur = _ssd_slot(d, cis[d])
                fn = functools.partial(_ssd_chunk, col0=SSM_HEADS * d + 2 * PAIRS * q, reverse=d == 1)
                _, vjp = jax.vjp(fn, hs_ref[0, 0, d, cur, :, :LANE], hs_ref[0, 0, d, cur, :, LANE:], xs_ref[0, rows[d], :LANE],
                                 xs_ref[0, rows[d], LANE:], b_ref[0, rows[d], :], c_ref[0, rows[d], :], dt_ref[0, rows[d], :],
                                 alog_v)
                res.append(vjp((dy_ref[0, rows[d], :LANE], dy_ref[0, rows[d], LANE:], dh_ref[d, :, :LANE], dh_ref[d, :, LANE:])))
            for d in range(2):
                g_h0, g_h1, g_x0, g_x1, g_b, g_c, g_dt, g_alog1 = res[d]
                dh_ref[d, :, :LANE] = g_h0
                dh_ref[d, :, LANE:] = g_h1
                dxs_ref[0, rows[d], :LANE] += g_x0
                dxs_ref[0, rows[d], LANE:] += g_x1
                db_ref[0, rows[d], :] += g_b
                dc_ref[0, rows[d], :] += g_c
                ddt_ref[0, rows[d], :] += g_dt
                g_alog = g_alog + g_alog1
            return g_alog

        dalog_ref[0, 0] = lax.fori_loop(0, nc, step, jnp.zeros((1, LANE), F32))

    lanes = lambda w, blk: BS((1, L, w), blk)
    in_specs = _ssd_specs(B, L) + [BS((1, 1, 2, nc + 1, LANE, QW), lambda b, q: (b, q, 0, 0, 0, 0)), lanes(QW, lambda b, q: (b, 0, q))]
    out_specs = [lanes(QW, lambda b, q: (b, 0, q)), lanes(LANE, lambda b, q: (b, 0, q // 2)), lanes(LANE, lambda b, q: (b, 0, q // 2)),
                 lanes(LANE, lambda b, q: (b, 0, 0)), BS((1, 1, 1, LANE), lambda b, q: (b, q, 0, 0)),
                 BS((1, 1, QW), lambda b, q: (b, 0, q))]
    out_shape = [SDS((B, L, SSM_WIDTH), F32), SDS((B, L, BC_WIDTH), F32), SDS((B, L, BC_WIDTH), F32), SDS((B, L, LANE), F32),
                 SDS((B, QUADS, 1, LANE), F32), SDS((B, 1, SSM_WIDTH), F32)]
    outs, carried = _pc_carry(body, (xbc_act, xbc_act, xbc_act, dtc, alog, dskip, hs, dy), name=name, grid=(B, QUADS),
                              in_specs=in_specs, out_specs=out_specs, out_shape=out_shape,
                              scratch_shapes=[pltpu.VMEM((2, LANE, QW), F32)], rider=rider)
    return (*outs, carried)


def _gate_norm(yp, z, w):
    v = yp * (z * _sigmoid(z))
    return v * lax.rsqrt(jnp.mean(v * v, axis=-1, keepdims=True) + NORM_EPS) * w


def _gate_fwd(ypre2, proj2, w, *, name):
    T = ypre2.shape[0]
    tr = _div_tile(T, 512, 8)
    G = 512

    def body(y_ref, z_ref, w_ref, o_ref):
        o_ref[...] = _gate_norm(y_ref[...], z_ref[...], w_ref[...]).astype(BF16)

    return _pc(body, name=name, grid=(T // tr, 2),
               in_specs=[BS((tr, G), lambda i, g: (i, g)), BS((tr, G), lambda i, g: (i, 2 + g)), BS((1, G), lambda i, g: (0, g))],
               out_specs=BS((tr, G), lambda i, g: (i, g)), out_shape=SDS((T, SSM_WIDTH), BF16))(ypre2, proj2, w.reshape(1, -1))


def _gate_bwd(ypre2, proj2, w, dy, *, name):
    T = ypre2.shape[0]
    tr = _div_tile(T, 512, 8)
    G = 512

    def body(y_ref, z_ref, w_ref, dy_ref, dyp_ref, dz_ref, dw_ref):
        _, vjp = jax.vjp(_gate_norm, y_ref[...], z_ref[...], w_ref[...])
        dyp, dz, dw = vjp(dy_ref[...])
        dyp_ref[...] = dyp
        dz_ref[...] = dz.astype(BF16)
        dw_ref[0] = dw

    tile = BS((tr, G), lambda i, g: (i, g))
    return _pc(body, name=name, grid=(T // tr, 2),
               in_specs=[tile, BS((tr, G), lambda i, g: (i, 2 + g)), BS((1, G), lambda i, g: (0, g)), tile],
               out_specs=[tile, tile, BS((1, 1, G), lambda i, g: (i, 0, g))],
               out_shape=[SDS((T, SSM_WIDTH), F32), SDS((T, SSM_WIDTH), BF16), SDS((T // tr, 1, SSM_WIDTH), F32)])(
        ypre2, proj2, w.reshape(1, -1), dy)


def _first_half():
    return lax.broadcasted_iota(jnp.int32, (1, LANE), 1) < HEAD_DIM


def _dup_kv_head(pair, odd):
    rolled = pltpu.roll(pair, HEAD_DIM, 1)
    return jnp.where(_first_half(), rolled, pair) if odd else jnp.where(_first_half(), pair, rolled)


def _stack_heads(quad):
    first = _first_half()
    lo, hi = quad[:, :LANE], quad[:, LANE:]
    return jnp.concatenate([jnp.where(first, lo, 0.0), jnp.where(first, 0.0, lo), jnp.where(first, hi, 0.0),
                            jnp.where(first, 0.0, hi)], axis=0)


def _unstack_heads(o):
    first = _first_half()
    return jnp.concatenate([jnp.where(first, o[:BLOCK], o[BLOCK:2 * BLOCK]), jnp.where(first, o[2 * BLOCK:3 * BLOCK], o[3 * BLOCK:])], axis=1)


def _fold_kv_head(d, odd):
    tot = d + pltpu.roll(d, HEAD_DIM, 1)
    return jnp.where(_first_half(), 0.0, tot) if odd else jnp.where(_first_half(), tot, 0.0)


def _attn_softmax(s, sink):
    m = jnp.maximum(jnp.max(s, axis=-1, keepdims=True), sink)
    p = jnp.exp(s - m)
    ps = jnp.exp(sink - m)
    inv = 1.0 / (jnp.sum(p, axis=-1, keepdims=True) + ps)
    return p * inv, ps * inv


def _attn_colneg(n, L):
    kpos = n * BLOCK - WINDOW + lax.broadcasted_iota(jnp.int32, (1, KEY_SPAN), 1)
    return jnp.where((kpos >= 0) & (kpos < L), 0.0, NEG)


def _attn_in_specs(L):
    nblk = L // BLOCK
    kv = lambda o, col: BS((1, BLOCK, 4 * HEAD_DIM), lambda b, n: (b, jnp.clip(n + o, 0, nblk - 1), col))
    kcol, vcol = 3584 // 256, 3840 // 256
    return [BS((1, BLOCK, ATTN_HEADS * HEAD_DIM), lambda b, n: (b, n, 0)), kv(-1, kcol), kv(0, kcol), kv(1, kcol),
            kv(-1, vcol), kv(0, vcol), kv(1, vcol),
            BS((ATTN_HEADS, BLOCK, KEY_SPAN), lambda b, n: (0, 0, 0)), BS((ATTN_HEADS * BLOCK, 1), lambda b, n: (0, 0))]


def _attn_fwd(proj, bias, sinkcol, *, name, rider=None):
    B, L, _ = proj.shape

    def body(q_ref, k0, k1, k2, v0, v1, v2, bias_ref, sink_ref, o_ref):
        colneg = _attn_colneg(pl.program_id(1), L)
        kcat = jnp.concatenate([k0[0], k1[0], k2[0]], axis=0)
        vcat = jnp.concatenate([v0[0], v1[0], v2[0]], axis=0)
        scores, probs = [], []
        for g in range(KV_HEADS):
            pair = slice(LANE * (g // 2), LANE * (g // 2) + LANE)
            quad = slice(4 * HEAD_DIM * g, 4 * HEAD_DIM * (g + 1))
            kd = _dup_kv_head(kcat[:, pair], g % 2).astype(BF16)
            qs = (_stack_heads(q_ref[0, :, quad]) * HEAD_DIM ** -0.5).astype(BF16)
            scores.append(lax.dot_general(qs, kd, (((1,), (1,)), ((), ())), preferred_element_type=F32))
        for g in range(KV_HEADS):
            pn, _ = _attn_softmax(scores[g] + bias_ref[4 * g:4 * g + 4].reshape(4 * BLOCK, KEY_SPAN) + colneg,
                                  sink_ref[4 * BLOCK * g:4 * BLOCK * (g + 1)])
            probs.append(pn.astype(BF16))
        for g in range(KV_HEADS):
            pair = slice(LANE * (g // 2), LANE * (g // 2) + LANE)
            quad = slice(4 * HEAD_DIM * g, 4 * HEAD_DIM * (g + 1))
            vd = _dup_kv_head(vcat[:, pair], g % 2).astype(BF16)
            o = lax.dot_general(probs[g], vd, (((1,), (0,)), ((), ())), preferred_element_type=F32)
            o_ref[0, :, quad] = _unstack_heads(o).astype(BF16)

    (out,), carried = _pc_carry(body, (proj, proj, proj, proj, proj, proj, proj, bias, sinkcol), name=name, grid=(B, L // BLOCK),
                                in_specs=_attn_in_specs(L),
                                out_specs=[BS((1, BLOCK, ATTN_HEADS * HEAD_DIM), lambda b, n: (b, n, 0))],
                                out_shape=[SDS((B, L, ATTN_HEADS * HEAD_DIM), BF16)], rider=rider)
    return out, carried


def _attn_bwd(proj, bias, sinkcol, dout, *, name, rider=None):
    B, L, _ = proj.shape
    nblk = L // BLOCK
    nn, nt, tn = (((1,), (0,)), ((), ())), (((1,), (1,)), ((), ())), (((0,), (0,)), ((), ()))

    def body(q_ref, k0, k1, k2, v0, v1, v2, bias_ref, sink_ref, do_ref, dq_ref, dk_ref, dv_ref, dbias_ref, dsink_ref):
        b, n = pl.program_id(0), pl.program_id(1)

        @pl.when(n == 0)
        def _():
            dk_ref[...] = jnp.zeros_like(dk_ref)
            dv_ref[...] = jnp.zeros_like(dv_ref)

        @pl.when((n == 0) & (b == 0))
        def _():
            dbias_ref[...] = jnp.zeros_like(dbias_ref)
            dsink_ref[...] = jnp.zeros_like(dsink_ref)

        colneg = _attn_colneg(n, L)
        kcat = jnp.concatenate([k0[0], k1[0], k2[0]], axis=0)
        vcat = jnp.concatenate([v0[0], v1[0], v2[0]], axis=0)
        krows = [pl.ds(pl.multiple_of(jnp.clip(n + o, 0, nblk - 1) * BLOCK, BLOCK), BLOCK) for o in (-1, 0, 1)]
        ops, mids = [], []
        for g in range(KV_HEADS):
            pair = slice(LANE * (g // 2), LANE * (g // 2) + LANE)
            quad = slice(4 * HEAD_DIM * g, 4 * HEAD_DIM * (g + 1))
            kd = _dup_kv_head(kcat[:, pair], g % 2).astype(BF16)
            vd = _dup_kv_head(vcat[:, pair], g % 2).astype(BF16)
            qs = (_stack_heads(q_ref[0, :, quad]) * HEAD_DIM ** -0.5).astype(BF16)
            dos = _stack_heads(do_ref[0, :, quad].astype(F32)).astype(BF16)
            ops.append((kd, qs, dos, lax.dot_general(qs, kd, nt, preferred_element_type=F32),
                        lax.dot_general(dos, vd, nt, preferred_element_type=F32)))
        for g in range(KV_HEADS):
            rows = slice(4 * BLOCK * g, 4 * BLOCK * (g + 1))
            _, _, _, s, dpn = ops[g]
            pn, psink = _attn_softmax(s + bias_ref[4 * g:4 * g + 4].reshape(4 * BLOCK, KEY_SPAN) + colneg, sink_ref[rows])
            r = jnp.sum(dpn * pn, axis=-1, keepdims=True)
            ds = pn * (dpn - r)
            dbias_ref[4 * g:4 * g + 4] += ds.reshape(4, BLOCK, KEY_SPAN)
            dsink_ref[rows] += -psink * r
            mids.append((pn.astype(BF16), ds.astype(BF16)))
        for g in range(KV_HEADS):
            pair = slice(LANE * (g // 2), LANE * (g // 2) + LANE)
            quad = slice(4 * HEAD_DIM * g, 4 * HEAD_DIM * (g + 1))
            kd, qs, dos, _, _ = ops[g]
            pnb, dsb = mids[g]
            dvd = lax.dot_general(pnb, dos, tn, preferred_element_type=F32)
            dkd = lax.dot_general(dsb, qs, tn, preferred_element_type=F32)
            dqs = lax.dot_general(dsb, kd, nn, preferred_element_type=F32) * HEAD_DIM ** -0.5
            dq_ref[0, :, quad] = _unstack_heads(dqs).astype(BF16)
            dk_g, dv_g = _fold_kv_head(dkd, g % 2), _fold_kv_head(dvd, g % 2)
            for o in range(3):
                dk_ref[0, krows[o], pair] += dk_g[o * BLOCK:(o + 1) * BLOCK]
                dv_ref[0, krows[o], pair] += dv_g[o * BLOCK:(o + 1) * BLOCK]

    qspec = BS((1, BLOCK, ATTN_HEADS * HEAD_DIM), lambda b, n: (b, n, 0))
    kvout = BS((1, L, 4 * HEAD_DIM), lambda b, n: (b, 0, 0))
    outs, carried = _pc_carry(
        body, (proj, proj, proj, proj, proj, proj, proj, bias, sinkcol, dout), name=name, grid=(B, nblk),
        in_specs=_attn_in_specs(L) + [qspec],
        out_specs=[qspec, kvout, kvout, BS((ATTN_HEADS, BLOCK, KEY_SPAN), lambda b, n: (0, 0, 0)),
                   BS((ATTN_HEADS * BLOCK, 1), lambda b, n: (0, 0))],
        out_shape=[SDS((B, L, ATTN_HEADS * HEAD_DIM), BF16), SDS((B, L, 4 * HEAD_DIM), F32), SDS((B, L, 4 * HEAD_DIM), F32),
                   SDS((ATTN_HEADS, BLOCK, KEY_SPAN), F32), SDS((ATTN_HEADS * BLOCK, 1), F32)], rider=rider)
    return (*outs, carried)


def _t5_bucket(rel):
    half = REL_BUCKETS // 2
    max_exact = half // 2
    ret = jnp.where(rel > 0, half, 0)
    n = jnp.abs(rel)
    nf = jnp.maximum(n, 1).astype(F32)
    large = max_exact + (jnp.log(nf / max_exact) / math.log(REL_MAX_DIST / max_exact) * (half - max_exact)).astype(jnp.int32)
    large = jnp.minimum(large, half - 1)
    return ret + jnp.where(n < max_exact, n, large)


def _bucket_table():
    rel = jnp.arange(KEY_SPAN)[None, :] - WINDOW - jnp.arange(BLOCK)[:, None]
    return _t5_bucket(rel).astype(jnp.int32)


def _bias_expand(rel_bias, bucket, *, name):
    rbt = jnp.zeros((ATTN_HEADS, 1, LANE), F32).at[:, 0, :REL_BUCKETS].set(rel_bias.T)

    def body(rb_ref, bk_ref, o_ref):
        lane = lax.broadcasted_iota(jnp.int32, (1, LANE), 1)
        row = rb_ref[0]
        bk = bk_ref[...]
        acc = jnp.zeros((BLOCK, KEY_SPAN), F32)
        for r in range(REL_BUCKETS):
            val = jnp.sum(jnp.where(lane == r, row, 0.0), axis=1, keepdims=True)
            acc = jnp.where(bk == r, val, acc)
        rel = (lax.broadcasted_iota(jnp.int32, (BLOCK, KEY_SPAN), 1) - WINDOW
               - lax.broadcasted_iota(jnp.int32, (BLOCK, KEY_SPAN), 0))
        o_ref[0] = jnp.where(jnp.abs(rel) <= WINDOW, acc, NEG)

    return _pc(body, name=name, grid=(ATTN_HEADS,),
               in_specs=[BS((1, 1, LANE), lambda h: (h, 0, 0)), BS((BLOCK, KEY_SPAN), lambda h: (0, 0))],
               out_specs=BS((1, BLOCK, KEY_SPAN), lambda h: (h, 0, 0)), out_shape=SDS((ATTN_HEADS, BLOCK, KEY_SPAN), F32))(rbt, bucket)


def _bias_reduce(dbias, bucket, *, name):
    def body(db_ref, bk_ref, o_ref):
        lane = lax.broadcasted_iota(jnp.int32, (1, LANE), 1)
        x = db_ref[0]
        bk = bk_ref[...]
        acc = jnp.zeros((1, LANE), F32)
        for r in range(REL_BUCKETS):
            part = jnp.sum(jnp.where(bk == r, x, 0.0), axis=1, keepdims=True)
            acc = jnp.where(lane == r, jnp.sum(part, axis=0, keepdims=True), acc)
        o_ref[0] = acc

    out = _pc(body, name=name, grid=(ATTN_HEADS,),
              in_specs=[BS((1, BLOCK, KEY_SPAN), lambda h: (h, 0, 0)), BS((BLOCK, KEY_SPAN), lambda h: (0, 0))],
              out_specs=BS((1, 1, LANE), lambda h: (h, 0, 0)), out_shape=SDS((ATTN_HEADS, 1, LANE), F32))(dbias, bucket)
    return out[:, 0, :REL_BUCKETS].T


def _loss_head(x2, w, target, *, name):
    T, D = x2.shape
    tr = _div_tile(T, 512, 8)

    def tile_loss(x, w, t):
        err = _rms(x, w) - t
        return 0.5 * jnp.sum(jnp.mean(err * err, axis=-1, keepdims=True), axis=0, keepdims=True)

    def body(x_ref, w_ref, t_ref, loss_ref, dx_ref, dxb_ref, dw_ref):
        t = t_ref[...]
        l, vjp = jax.vjp(lambda x, w: tile_loss(x, w, t), x_ref[...], w_ref[...])
        dx, dw = vjp(jnp.ones((1, 1), F32))
        dx_ref[...] = dx
        dxb_ref[...] = dx.astype(BF16)

        @pl.when(pl.program_id(0) == 0)
        def _():
            dw_ref[...] = jnp.zeros_like(dw_ref)
            loss_ref[...] = jnp.zeros_like(loss_ref)

        dw_ref[...] += dw
        loss_ref[...] += l + jnp.zeros((1, LANE), F32)

    row = BS((tr, D), lambda i: (i, 0))
    one = BS((1, D), lambda i: (0, 0))
    return _pc(body, name=name, grid=(T // tr,), in_specs=[row, one, row],
               out_specs=[BS((1, LANE), lambda i: (0, 0)), row, row, one],
               out_shape=[SDS((1, LANE), F32), SDS((T, D), F32), SDS((T, D), BF16), SDS((1, D), F32)])(x2, w.reshape(1, D), target)


def _adamw(w2, g2, m2, v2, *, name):
    R, C = w2.shape
    tr = _div_tile(R, 256, 8)
    c1 = 1.0 - ADAM_B1 ** ADAM_STEP
    c2 = 1.0 - ADAM_B2 ** ADAM_STEP

    def body(w_ref, g_ref, m_ref, v_ref, d_ref, nm_ref, nv_ref):
        g = g_ref[...]
        m = ADAM_B1 * m_ref[...] + (1.0 - ADAM_B1) * g
        v = ADAM_B2 * v_ref[...] + (1.0 - ADAM_B2) * (g * g)
        d_ref[...] = -ADAM_LR * ((m / c1) / (jnp.sqrt(v / c2) + ADAM_EPS) + ADAM_WD * w_ref[...])
        nm_ref[...] = m
        nv_ref[...] = v

    t = BS((tr, C), lambda i: (i, 0))
    return _pc(body, name=name, grid=(R // tr,), in_specs=[t, t, t, t], out_specs=[t, t, t],
               out_shape=[SDS((R, C), F32)] * 3)(w2, g2, m2, v2)


def _place():
    return lax.axis_index("x"), lax.axis_index("y"), lax.axis_index("c")


def _gather_rider(shards):
    na = len(shards)

    def copies(ins, outs, sems):
        send_sems, recv_sems = sems
        x, y, c = _place()
        for a in range(na):
            for k, (px, py) in enumerate([(1 - x, y), (x, 1 - y), (1 - x, 1 - y)]):
                send = functools.partial(pltpu.make_async_remote_copy, ins[a], outs[a].at[2 * x + y], send_sems.at[a, k],
                                         recv_sems.at[a, k], device_id=(px, py, c), device_id_type=MESH)
                got = outs[a].at[2 * px + py]
                arrived = functools.partial(pltpu.make_async_remote_copy, got, got, send_sems.at[a, k], recv_sems.at[a, k],
                                            device_id=(px, py, c), device_id_type=MESH)
                yield send, arrived

    def start(ins, outs, sems):
        for send, _ in copies(ins, outs, sems):
            send().start()

    def finish(ins, outs, sems):
        both = list(copies(ins, outs, sems))
        for _, arrived in both:
            arrived().wait_recv()
        for send, _ in both:
            send().wait_send()

    return dict(ins=list(shards), prev=[], out_shape=[SDS((N_CHIP,) + s.shape, s.dtype) for s in shards],
                scratch=[pltpu.SemaphoreType.DMA((na, 3)), pltpu.SemaphoreType.DMA((na, 3))], start=start, finish=finish)


def _scatter_rider(bufs, layer, prev):
    na = len(bufs)
    h = layer // (DEPTH // 2)

    def copies(ins, outs, sems):
        send_sems, recv_sems, local_sems = sems
        x, y, c = _place()
        me = 4 * x + 2 * y + c
        for a in range(na):
            for j in range(N_CHIP):
                is_self = ((2 * x + y) == j) & (c == h)
                local = functools.partial(pltpu.make_async_copy, ins[a].at[j], outs[a].at[me], local_sems.at[a])
                remote = functools.partial(pltpu.make_async_remote_copy, ins[a].at[j], outs[a].at[me], send_sems.at[a, j],
                                           recv_sems.at[a, me], device_id=(j // 2, j % 2, h), device_id_type=MESH)
                yield is_self, local, remote

    def start(ins, outs, sems):
        for is_self, local, remote in copies(ins, outs, sems):
            pl.when(is_self)(lambda: local().start())
            pl.when(jnp.logical_not(is_self))(lambda: remote().start())

    def finish(ins, outs, sems):
        _, recv_sems, _ = sems
        x, y, c = _place()
        me = 4 * x + 2 * y + c
        for a in range(na):
            for s in range(N_DEV):
                got = outs[a].at[s]
                arrived = functools.partial(pltpu.make_async_remote_copy, got, got, recv_sems.at[a, s], recv_sems.at[a, s],
                                            device_id=(s // 4, (s // 2) % 2, s % 2), device_id_type=MESH)
                pl.when((c == h) & (me != s))(lambda: arrived().wait_recv())
        for is_self, local, remote in copies(ins, outs, sems):
            pl.when(is_self)(lambda: local().wait())
            pl.when(jnp.logical_not(is_self))(lambda: remote().wait_send())

    return dict(ins=list(bufs), prev=list(prev), out_shape=[SDS((N_DEV,) + b.shape[1:], b.dtype) for b in bufs],
                scratch=[pltpu.SemaphoreType.DMA((na, N_CHIP)), pltpu.SemaphoreType.DMA((na, N_DEV)), pltpu.SemaphoreType.DMA((na,))],
                start=start, finish=finish)


def _sum_sources(parts, *, name):
    _, R, C = parts.shape
    tr = _div_tile(R, 256, 16)

    def body(p_ref, o_ref):
        acc = p_ref[0].astype(F32)
        for s in range(1, N_DEV):
            acc = acc + p_ref[s].astype(F32)
        o_ref[...] = acc

    return _pc(body, name=name, grid=(R // tr,), in_specs=[BS((N_DEV, tr, C), lambda i: (0, i, 0))],
               out_specs=BS((tr, C), lambda i: (i, 0)), out_shape=SDS((R, C), F32))(parts)


def _join_halves(halves, *, name):
    na = len(halves)

    def body(*refs):
        ins, outs = refs[:na], refs[na:2 * na]
        send_sems, recv_sems = refs[2 * na:]
        x, y, c = _place()
        cps = []
        for a in range(na):
            cp = pltpu.make_async_remote_copy(ins[a], outs[a], send_sems.at[a], recv_sems.at[a],
                                              device_id=(x, y, 1 - c), device_id_type=MESH)
            cp.start()
            cps.append(cp)
        for cp in cps:
            cp.wait_recv()
        for cp in cps:
            cp.wait_send()

    any_spec = BS(memory_space=pl.ANY)
    return pl.pallas_call(
        body, name=name, in_specs=[any_spec] * na, out_specs=[any_spec] * na,
        out_shape=[SDS(h.shape, h.dtype) for h in halves],
        scratch_shapes=[pltpu.SemaphoreType.DMA((na,)), pltpu.SemaphoreType.DMA((na,))],
        compiler_params=pltpu.CompilerParams(has_side_effects=True))(*halves)


def _allreduce_small(vec, *, name):
    R = vec.shape[0]

    def body(v_ref, o_ref, all_ref, send_sems, recv_sems):
        x, y, c = _place()
        me = 4 * x + 2 * y + c
        all_ref[me] = v_ref[...]
        sends = []
        for r in range(1, N_DEV):
            tgt = (x ^ (r >> 2), y ^ ((r >> 1) & 1), c ^ (r & 1))
            cp = pltpu.make_async_remote_copy(v_ref, all_ref.at[me], send_sems.at[r - 1], recv_sems.at[r - 1],
                                              device_id=tgt, device_id_type=MESH)
            cp.start()
            sends.append(cp)
        for r in range(1, N_DEV):
            tx, ty, tc = x ^ (r >> 2), y ^ ((r >> 1) & 1), c ^ (r & 1)
            got = all_ref.at[4 * tx + 2 * ty + tc]
            pltpu.make_async_remote_copy(got, got, send_sems.at[r - 1], recv_sems.at[r - 1],
                                         device_id=(tx, ty, tc), device_id_type=MESH).wait_recv()
        for cp in sends:
            cp.wait_send()
        acc = all_ref[0]
        for s in range(1, N_DEV):
            acc = acc + all_ref[s]
        o_ref[...] = acc

    vm = BS(memory_space=pltpu.VMEM)
    return pl.pallas_call(
        body, name=name, in_specs=[vm], out_specs=vm, out_shape=SDS((R, LANE), F32),
        scratch_shapes=[pltpu.VMEM((N_DEV, R, LANE), F32), pltpu.SemaphoreType.DMA((N_DEV - 1,)), pltpu.SemaphoreType.DMA((N_DEV - 1,))],
        compiler_params=pltpu.CompilerParams(has_side_effects=True, vmem_limit_bytes=VMEM_LIMIT_BYTES))(vec)


def _pack(arrs):
    rows = []
    for a in arrs:
        f = a.reshape(-1).astype(F32)
        n = -(-f.shape[0] // LANE) * LANE
        rows.append(jnp.pad(f, (0, n - f.shape[0])).reshape(-1, LANE))
    v = jnp.concatenate(rows, axis=0)
    pad = -v.shape[0] % 8
    return jnp.pad(v, ((0, pad), (0, 0)))


def _unpack(v, shapes):
    out, r = [], 0
    for s in shapes:
        n = int(np.prod(s)) if len(s) else 1
        nr = -(-n // LANE)
        out.append(v[r:r + nr].reshape(-1)[:n].reshape(s))
        r += nr
    return out


def _perm_in_cols(w_full):
    z, xbc, dt, q, k, v = (w_full[..., :Z_END], w_full[..., Z_END:XBC_END], w_full[..., XBC_END:DT_END],
                           w_full[..., DT_END:Q_END], w_full[..., Q_END:K_END], w_full[..., K_END:])
    pad = jnp.zeros(dt.shape[:-1] + (LANE - dt.shape[-1],), dt.dtype)
    return jnp.concatenate([q, z, xbc, k, v, dt, pad], axis=-1)


def _unperm_in_cols(g):
    q, z, xbc, k, v, dt = (g[..., :1024], g[..., 1024:2048], g[..., 2048:3584], g[..., 3584:3840], g[..., 3840:4096],
                           g[..., 4096:4096 + 2 * SSM_HEADS])
    return jnp.concatenate([z, xbc, dt, q, k, v], axis=-1)


def _dt_cols(a):
    return jnp.pad(a.reshape(1, 2 * SSM_HEADS), ((0, 0), (0, LANE - 2 * SSM_HEADS)))


def _dt_fwd(proj, dtb, *, name):
    B, L, _ = proj.shape

    def body(p_ref, b_ref, o_ref):
        o_ref[0] = _softplus(p_ref[0] + b_ref[...])

    return _pc(body, name=name, grid=(B,),
               in_specs=[BS((1, L, LANE), lambda b: (b, 0, P_COLS // LANE - 1)), BS((1, LANE), lambda b: (0, 0))],
               out_specs=BS((1, L, LANE), lambda b: (b, 0, 0)), out_shape=SDS((B, L, LANE), F32))(proj, dtb)


def _dt_bwd(proj, dtb, ddt, *, name):
    B, L, _ = proj.shape

    def body(p_ref, b_ref, g_ref, o_ref, db_ref):
        g = g_ref[0] * _sigmoid(p_ref[0] + b_ref[...])
        o_ref[0] = g.astype(BF16)
        db_ref[0] = jnp.sum(g, axis=0, keepdims=True)

    row = BS((1, L, LANE), lambda b: (b, 0, 0))
    return _pc(body, name=name, grid=(B,),
               in_specs=[BS((1, L, LANE), lambda b: (b, 0, P_COLS // LANE - 1)), BS((1, LANE), lambda b: (0, 0)), row],
               out_specs=[row, BS((1, 1, LANE), lambda b: (b, 0, 0))],
               out_shape=[SDS((B, L, LANE), BF16), SDS((B, 1, LANE), F32)])(proj, dtb, ddt)


def _layer_fwd(i, x, wts, small, band_bias, riders=None, arrived=None):
    riders = riders or {}
    B, L, D = x.shape
    T = B * L
    x2 = x.reshape(T, D)
    h, proj2 = _norm_mm(x2, small["norm1_w"][i], wts["w_in"], name=f"in_proj_{i}", tn=1408)
    proj = proj2.reshape(B, L, P_COLS)
    xbc_act = _conv_fwd(proj, 2048 // 256, small["conv_w"][i], small["conv_b"][i], taps=SSM_CONV, ct=256,
                        out_dtype=F32, name=f"ssm_conv_{i}")
    dtb, alog = _dt_cols(small["dt_bias"][i]), _dt_cols(small["a_log"][i])
    dskip = jnp.repeat(small["d_skip"][i], HEAD_DIM).reshape(1, SSM_WIDTH)
    dtc = _dt_fwd(proj, dtb, name=f"dt_{i}")
    ypre, hs, carried = _ssd_fwd(xbc_act, dtc, alog, dskip, name=f"ssd_{i}", rider=riders.get("ssd"))
    if carried is not None:
        arrived("ssd", carried)
    y_ssm = _gate_fwd(ypre.reshape(T, SSM_WIDTH), proj2, small["ssm_norm_w"][i], name=f"gate_{i}")
    sinkcol = jnp.repeat(small["attn_sink"][i], BLOCK).reshape(ATTN_HEADS * BLOCK, 1)
    y_attn, carried = _attn_fwd(proj, band_bias, sinkcol, name=f"attn_{i}", rider=riders.get("attn"))
    if carried is not None:
        arrived("attn", carried)
    y_attn = y_attn.reshape(T, D)
    w_out = wts["w_out"]
    x_mid = _mm(y_ssm, w_out[:SSM_WIDTH], add=x2, name=f"out_proj_a_{i}")
    x_mid = _mm(y_attn, w_out[SSM_WIDTH:], add=x_mid, name=f"out_proj_b_{i}")
    h2, gu2 = _norm_mm(x_mid, small["norm2_w"][i], wts["w_up"], name=f"up_proj_{i}", out_dtype=BF16, tn=1408)
    gu = gu2.reshape(B, L, 2 * D_FF)
    act = _conv_fwd(gu, 0, small["ffn_conv_w"][i], small["ffn_conv_b"][i], taps=FFN_CONV, ct=256, gate_blk0=D_FF // 256,
                    out_dtype=BF16, name=f"ffn_conv_{i}")
    x_out = _mm(act.reshape(T, D_FF), wts["w_down"], add=x_mid, name=f"down_proj_{i}", tk=1408)
    saved = dict(x2=x2, h=h, proj2=proj2, xbc_act=xbc_act, dtb=dtb, dtc=dtc, alog=alog, dskip=dskip, ypre=ypre, hs=hs, y_ssm=y_ssm,
                 sinkcol=sinkcol, y_attn=y_attn, x_mid=x_mid, h2=h2, gu=gu, act=act)
    return x_out.reshape(B, L, D), saved


def _layer_bwd(i, dx_out, dxb, sv, wts, small, band_bias, attn_rider=None, ssd_rider=None, arrived=None):
    T, D = dx_out.shape
    B, L = sv["gu"].shape[:2]
    g = {}
    dact = _mm(dxb, wts["w_down"], tb=True, out_dtype=BF16, name=f"d_act_{i}", tn=1408)
    g["w_down"] = _mm(sv["act"].reshape(T, D_FF), dxb, ta=True, name=f"dw_down_{i}", tm=1408)
    dg, du, dcw, dcb = _conv_bwd(sv["gu"], 0, small["ffn_conv_w"][i], small["ffn_conv_b"][i], dact.reshape(B, L, D_FF),
                                 taps=FFN_CONV, ct=256, gate_blk0=D_FF // 256, name=f"d_ffn_conv_{i}")
    g["ffn_conv_w"] = jnp.sum(dcw, axis=0)[:FFN_CONV]
    g["ffn_conv_b"] = jnp.sum(dcb, axis=(0, 1))
    dgu = jnp.concatenate([dg, du], axis=-1).reshape(T, 2 * D_FF)
    dx_mid, dmb, dw2 = _mm_dnorm(dgu, wts["w_up"], sv["x_mid"], small["norm2_w"][i], dx_out, name=f"d_h2_{i}", tk=1408)
    g["w_up"] = _mm(sv["h2"], dgu, ta=True, name=f"dw_up_{i}", tn=1408)
    g["norm2_w"] = dw2[0]
    w_out = wts["w_out"]
    dy_ssm = _mm(dmb, w_out[:SSM_WIDTH], tb=True, name=f"d_y_ssm_{i}")
    dy_attn = _mm(dmb, w_out[SSM_WIDTH:], tb=True, out_dtype=BF16, name=f"d_y_attn_{i}")
    g["w_out"] = jnp.concatenate([_mm(sv["y_ssm"], dmb, ta=True, name=f"dw_out_a_{i}"),
                                  _mm(sv["y_attn"], dmb, ta=True, name=f"dw_out_b_{i}")], axis=0)
    dypre, dz, dwn = _gate_bwd(sv["ypre"].reshape(T, SSM_WIDTH), sv["proj2"], small["ssm_norm_w"][i], dy_ssm, name=f"d_gate_{i}")
    g["ssm_norm_w"] = jnp.sum(dwn, axis=(0, 1))
    proj = sv["proj2"].reshape(B, L, P_COLS)
    dxs, dbm, dcm, ddt, dalog, ddsk, carried = _ssd_bwd(sv["xbc_act"], sv["dtc"], sv["alog"], sv["dskip"], sv["hs"],
                                                        dypre.reshape(B, L, SSM_WIDTH), name=f"d_ssd_{i}",
                                                        rider=ssd_rider(g) if ssd_rider is not None else None)
    if carried is not None:
        arrived("ssd", carried)
    ddt, ddtb = _dt_bwd(proj, sv["dtb"], ddt, name=f"d_dt_{i}")
    g["dt_bias"] = jnp.sum(ddtb, axis=(0, 1))[:2 * SSM_HEADS].reshape(2, SSM_HEADS)
    g["a_log"] = jnp.sum(dalog, axis=(0, 1, 2))[:2 * SSM_HEADS].reshape(2, SSM_HEADS)
    g["d_skip"] = jnp.sum(ddsk.reshape(B, SSM_HEADS, HEAD_DIM), axis=(0, 2))
    dxbc_act = jnp.concatenate([dxs, dbm, dcm], axis=-1)
    dxbc, dcw, dcb = _conv_bwd(proj, 2048 // 256, small["conv_w"][i], small["conv_b"][i], dxbc_act, taps=SSM_CONV, ct=256,
                               name=f"d_ssm_conv_{i}")
    g["conv_w"] = jnp.sum(dcw, axis=0)[:SSM_CONV]
    g["conv_b"] = jnp.sum(dcb, axis=(0, 1))
    dq, dk, dv, dbias, dsink, carried = _attn_bwd(proj, band_bias, sv["sinkcol"], dy_attn.reshape(B, L, D), name=f"d_attn_{i}",
                                                  rider=attn_rider)
    if carried is not None:
        arrived("attn", carried)
    g["attn_sink"] = jnp.sum(dsink.reshape(ATTN_HEADS, BLOCK), axis=1)
    dproj = jnp.concatenate([dq, dz.reshape(B, L, SSM_WIDTH), dxbc, dk.astype(BF16), dv.astype(BF16), ddt], axis=-1).reshape(T, P_COLS)
    dx_in, dx_in_b, dw1 = _mm_dnorm(dproj, wts["w_in"], sv["x2"], small["norm1_w"][i], dx_mid, name=f"d_h_{i}", tk=1408)
    g["w_in"] = _unperm_in_cols(_mm(sv["h"], dproj, ta=True, name=f"dw_in_{i}", tn=1408))
    g["norm1_w"] = dw1[0]
    return dx_in, dx_in_b, g, dbias


_BIG = ("w_in", "w_out", "w_up", "w_down")
_BIG_AXIS = {"w_in": 2, "w_out": 1, "w_up": 2, "w_down": 1}
_SMALL = ("rel_bias", "norm1_w", "conv_w", "conv_b", "dt_bias", "a_log", "d_skip", "ssm_norm_w", "attn_sink", "norm2_w",
          "ffn_conv_w", "ffn_conv_b", "final_norm_w")
_SMALL_SHARDED = ("conv_w", "ffn_conv_w")
_ORDER = ("rel_bias", "norm1_w", "w_in", "conv_w", "conv_b", "dt_bias", "a_log", "d_skip", "ssm_norm_w", "attn_sink", "w_out",
          "norm2_w", "w_up", "ffn_conv_w", "ffn_conv_b", "w_down", "final_norm_w")


def _local_step(x, target, small, wts=None, exchange=None):
    B, L, D = x.shape
    bucket = _bucket_table()
    band_bias = _bias_expand(small["rel_bias"], bucket, name="band_bias")

    def fetch(spec):
        return exchange["gather"](spec) if exchange is not None and spec else None

    def fetched(spec, carried):
        for (i, k), full in zip(spec, exchange["weights"](spec, carried)):
            wts[i][k] = full

    if exchange is not None:
        wts = [{} for _ in range(DEPTH)]
        fetched([(0, "w_in")], _run_rider(fetch([(0, "w_in")]), name="gather_w_in_0"))
    saved = []
    for i in range(DEPTH):
        nxt = i + 1 < DEPTH
        if i == 0:
            plan = {"ssd": [(0, "w_out"), (0, "w_up"), (0, "w_down")], "attn": [(1, "w_in"), (1, "w_out"), (1, "w_up")] if nxt else []}
        else:
            plan = {"ssd": [(i, "w_down")] + ([(i + 1, "w_in"), (i + 1, "w_out")] if nxt else []), "attn": [(i + 1, "w_up")] if nxt else []}
        x, sv = _layer_fwd(i, x, wts[i], small, band_bias, {c: fetch(s) for c, s in plan.items()}, lambda c, r: fetched(plan[c], r))
        saved.append(sv)
    loss, dx, dxb, dwf = _loss_head(x.reshape(B * L, D), small["final_norm_w"], target.reshape(B * L, D), name="loss_head")
    per_layer = []
    dbias = jnp.zeros((ATTN_HEADS, BLOCK, KEY_SPAN), F32)
    late = None
    for i in reversed(range(DEPTH)):
        own = [(i, "w_down"), (i, "w_up"), (i, "w_out")]
        plan = {"ssd": own, "attn": late[0] if late else []}
        attn_rider = exchange["scatter"](*late) if late else None
        ssd_rider = (lambda g: exchange["scatter"](own, [g[k] for _, k in own])) if exchange is not None else None
        dx, dxb, g, dbias_i = _layer_bwd(i, dx, dxb, saved[i], wts[i], small, band_bias, attn_rider, ssd_rider,
                                    lambda c, r: exchange["collect"](plan[c], r))
        if exchange is not None:
            late = ([(i, "w_in")], [g["w_in"]])
            for k in _BIG:
                g.pop(k)
        dbias = dbias + dbias_i
        per_layer.append(g)
    if exchange is not None:
        exchange["collect"](late[0], _run_rider(exchange["scatter"](*late), name="scatter_dw_in_0"))
    per_layer.reverse()
    grads = {k: jnp.stack([g[k] for g in per_layer]) for k in per_layer[0]}
    grads["rel_bias"] = _bias_reduce(dbias, bucket, name="d_rel_bias")
    grads["final_norm_w"] = dwf[0]
    return loss, dx.reshape(B, L, D), grads


def _split_by_chip(g, axis):
    shp = g.shape
    n = shp[axis] // N_CHIP
    g = g.reshape(shp[:axis] + (N_CHIP, n) + shp[axis + 1:])
    return jnp.moveaxis(g, axis, 0)


def _join_chips(a, axis):
    a = jnp.moveaxis(a, 0, axis)
    shp = a.shape
    return a.reshape(shp[:axis] + (shp[axis] * shp[axis + 1],) + shp[axis + 2:])


def kernel(x, rel_bias, norm1_w, w_in, conv_w, conv_b, dt_bias, a_log, d_skip, ssm_norm_w, attn_sink, w_out, norm2_w, w_up, ffn_conv_w, ffn_conv_b, w_down, final_norm_w, loss_target, m_rel_bias, m_norm1_w, m_w_in, m_conv_w, m_conv_b, m_dt_bias, m_a_log, m_d_skip, m_ssm_norm_w, m_attn_sink, m_w_out, m_norm2_w, m_w_up, m_ffn_conv_w, m_ffn_conv_b, m_w_down, m_final_norm_w, v_rel_bias, v_norm1_w, v_w_in, v_conv_w, v_conv_b, v_dt_bias, v_a_log, v_d_skip, v_ssm_norm_w, v_attn_sink, v_w_out, v_norm2_w, v_w_up, v_ffn_conv_w, v_ffn_conv_b, v_w_down, v_final_norm_w):
    w = dict(rel_bias=rel_bias, norm1_w=norm1_w, w_in=w_in, conv_w=conv_w, conv_b=conv_b, dt_bias=dt_bias, a_log=a_log,
             d_skip=d_skip, ssm_norm_w=ssm_norm_w, attn_sink=attn_sink, w_out=w_out, norm2_w=norm2_w, w_up=w_up,
             ffn_conv_w=ffn_conv_w, ffn_conv_b=ffn_conv_b, w_down=w_down, final_norm_w=final_norm_w)
    m = dict(rel_bias=m_rel_bias, norm1_w=m_norm1_w, w_in=m_w_in, conv_w=m_conv_w, conv_b=m_conv_b, dt_bias=m_dt_bias,
             a_log=m_a_log, d_skip=m_d_skip, ssm_norm_w=m_ssm_norm_w, attn_sink=m_attn_sink, w_out=m_w_out, norm2_w=m_norm2_w,
             w_up=m_w_up, ffn_conv_w=m_ffn_conv_w, ffn_conv_b=m_ffn_conv_b, w_down=m_w_down, final_norm_w=m_final_norm_w)
    v = dict(rel_bias=v_rel_bias, norm1_w=v_norm1_w, w_in=v_w_in, conv_w=v_conv_w, conv_b=v_conv_b, dt_bias=v_dt_bias,
             a_log=v_a_log, d_skip=v_d_skip, ssm_norm_w=v_ssm_norm_w, attn_sink=v_attn_sink, w_out=v_w_out, norm2_w=v_norm2_w,
             w_up=v_w_up, ffn_conv_w=v_ffn_conv_w, ffn_conv_b=v_ffn_conv_b, w_down=v_w_down, final_norm_w=v_final_norm_w)
    my_chip = 2 * lax.axis_index("x") + lax.axis_index("y")

    shards = {k: w[k].astype(BF16) for k in _BIG}
    received = {}

    def gather(spec):
        return _gather_rider([shards[k][i] for i, k in spec])

    def weights(spec, carried):
        out = []
        for (i, k), g_ in zip(spec, carried):
            full = _join_chips(lax.dynamic_update_index_in_dim(g_, shards[k][i], my_chip, 0), _BIG_AXIS[k] - 1)
            out.append(_perm_in_cols(full) if k == "w_in" else full)
        return out

    def scatter(spec, grads):
        layer = spec[0][0]
        bufs = [_split_by_chip(g_, _BIG_AXIS[k] - 1).astype(BF16) for (_, k), g_ in zip(spec, grads)]
        prev = [received[(layer % 2, k)] for _, k in spec] if layer + 2 < DEPTH else []
        return _scatter_rider(bufs, layer, prev)

    def collect(spec, carried):
        for (i, k), pieces in zip(spec, carried):
            received[(i % 2, k)] = pieces

    conv_shapes = [(DEPTH, SSM_CONV, CONV_CH), (DEPTH, FFN_CONV, D_FF)]
    placed = [lax.dynamic_update_slice_in_dim(jnp.zeros(s, F32), w[k], my_chip * w[k].shape[2], axis=2)
              for k, s in zip(_SMALL_SHARDED, conv_shapes)]
    lead = (lax.axis_index("c") == 0).astype(F32)
    conv_full = _unpack(_allreduce_small(_pack([p * lead for p in placed]), name="gather_conv_weights"), conv_shapes)
    small = {k: w[k] for k in _SMALL}
    small["conv_w"], small["ffn_conv_w"] = conv_full

    loss_part, grad_x, gp = _local_step(x, loss_target, small,
                                        exchange=dict(gather=gather, weights=weights, scatter=scatter, collect=collect))

    small_shapes = [small[k].shape for k in _SMALL] + [()]
    red = _unpack(_allreduce_small(_pack([gp[k] for k in _SMALL] + [loss_part[0, :1]]), name="reduce_small"), small_shapes)
    gsmall = dict(zip(_SMALL, red[:-1]))
    loss = red[-1]
    for k in _SMALL_SHARDED:
        n = w[k].shape[2]
        gsmall[k] = lax.dynamic_slice_in_dim(gsmall[k], my_chip * n, n, axis=2)

    core = lax.axis_index("c")
    half = DEPTH // 2
    halves = [jnp.stack([_sum_sources(received[(p, k)], name=f"sum_{k}_{p}") for p in range(half)]) for k in _BIG]
    others = _join_halves(halves, name="join_halves")
    gbig = {}
    for k, mine_, theirs_ in zip(_BIG, halves, others):
        full = jnp.zeros((DEPTH,) + mine_.shape[1:], F32)
        full = lax.dynamic_update_slice_in_dim(full, mine_, core * half, axis=0)
        gbig[k] = lax.dynamic_update_slice_in_dim(full, theirs_, (1 - core) * half, axis=0)

    grad, delta, new_m, new_v = {}, {}, {}, {}
    for k in _BIG:
        shp = w[k].shape
        two = lambda a: a.reshape(shp[0] * shp[1], shp[2])
        d_, m_, v_ = _adamw(two(w[k]), two(gbig[k]), two(m[k]), two(v[k]), name=f"adamw_{k}")
        grad[k], delta[k], new_m[k], new_v[k] = gbig[k], d_.reshape(shp), m_.reshape(shp), v_.reshape(shp)
    shapes = [w[k].shape for k in _SMALL]
    d_, m_, v_ = _adamw(_pack([w[k] for k in _SMALL]), _pack([gsmall[k] for k in _SMALL]), _pack([m[k] for k in _SMALL]),
                        _pack([v[k] for k in _SMALL]), name="adamw_small")
    for k, a, b_, c_ in zip(_SMALL, _unpack(d_, shapes), _unpack(m_, shapes), _unpack(v_, shapes)):
        grad[k], delta[k], new_m[k], new_v[k] = gsmall[k], a, b_, c_
    return (loss, grad_x, *[grad[k] for k in _ORDER], *[delta[k] for k in _ORDER], *[new_m[k] for k in _ORDER],
            *[new_v[k] for k in _ORDER])
```

```python
import functools
import math

import jax
import jax.numpy as jnp
import numpy as np
from jax import lax
from jax.experimental import pallas as pl
from jax.experimental.pallas import tpu as pltpu

F32 = jnp.float32
BF16 = jnp.bfloat16
BS = pl.BlockSpec
SDS = jax.ShapeDtypeStruct
MESH = pl.DeviceIdType.MESH

D_MODEL = 1024
DEPTH = 4
SSM_HEADS = 16
SSM_WIDTH = 1024
BC_WIDTH = 256
CONV_CH = 1536
SSM_CONV = 7
CHUNK = 128
ATTN_HEADS = 16
KV_HEADS = 4
HEAD_DIM = 64
WINDOW = 128
BLOCK = 128
KEY_SPAN = 384
REL_BUCKETS = 32
REL_MAX_DIST = 128
D_FF = 2816
FFN_CONV = 3
NORM_EPS = 1e-6
Z_END = 1024
XBC_END = 2560
DT_END = 2592
Q_END = 3616
K_END = 3872
IN_COLS = 4128
P_COLS = 4224
ADAM_LR, ADAM_B1, ADAM_B2, ADAM_EPS, ADAM_WD, ADAM_STEP = 0.001, 0.9, 0.999, 1e-08, 0.01, 10
NEG = -1e30
N_DEV = 8
N_CHIP = 4
LANE = 128
VMEM_LIMIT_BYTES = 48 * 1024 * 1024


def _pc(body, *, name, grid, in_specs, out_specs, out_shape, scratch_shapes=(), aliases=None):
    return pl.pallas_call(
        body, name=name, grid=grid, in_specs=in_specs, out_specs=out_specs, out_shape=out_shape,
        scratch_shapes=list(scratch_shapes), input_output_aliases=aliases or {},
        compiler_params=pltpu.CompilerParams(dimension_semantics=("arbitrary",) * len(grid),
                                             vmem_limit_bytes=VMEM_LIMIT_BYTES))


def _split_rider_refs(refs, n_in, n_out, rider):
    n_rin = len(rider["ins"]) + len(rider["prev"])
    n_rout = len(rider["out_shape"])
    core = refs[:n_in] + refs[n_in + n_rin:n_in + n_rin + n_out] + refs[n_in + n_rin + n_out + n_rout + len(rider["scratch"]):]
    rins = refs[n_in:n_in + len(rider["ins"])]
    routs = refs[n_in + n_rin + n_out:n_in + n_rin + n_out + n_rout]
    sems = refs[n_in + n_rin + n_out + n_rout:n_in + n_rin + n_out + n_rout + len(rider["scratch"])]
    return core, rins, routs, sems


def _pc_carry(body, args, *, name, grid, in_specs, out_specs, out_shape, scratch_shapes=(), rider=None, aliases=None):
    if rider is None:
        return _pc(body, name=name, grid=grid, in_specs=in_specs, out_specs=out_specs, out_shape=out_shape,
                   scratch_shapes=scratch_shapes, aliases=aliases)(*args), None
    n_in, n_out = len(in_specs), len(out_shape)
    any_spec = BS(memory_space=pl.ANY)

    def full(*refs):
        core, rins, routs, sems = _split_rider_refs(refs, n_in, n_out, rider)
        ids = [pl.program_id(d) for d in range(len(grid))]
        first = functools.reduce(jnp.logical_and, [i == 0 for i in ids])
        last = functools.reduce(jnp.logical_and, [i == g - 1 for i, g in zip(ids, grid)])

        @pl.when(first)
        def _():
            rider["start"](rins, routs, sems)

        body(*core)

        @pl.when(last)
        def _():
            rider["finish"](rins, routs, sems)

    n_rin = len(rider["ins"])
    outs = pl.pallas_call(
        full, name=name, grid=grid,
        in_specs=list(in_specs) + [any_spec] * (n_rin + len(rider["prev"])),
        out_specs=list(out_specs) + [any_spec] * len(rider["out_shape"]),
        out_shape=list(out_shape) + list(rider["out_shape"]),
        scratch_shapes=list(rider["scratch"]) + list(scratch_shapes),
        input_output_aliases={**(aliases or {}), **{n_in + n_rin + t: n_out + t for t in range(len(rider["prev"]))}},
        compiler_params=pltpu.CompilerParams(dimension_semantics=("arbitrary",) * len(grid), vmem_limit_bytes=VMEM_LIMIT_BYTES,
                                             has_side_effects=True))(*args, *rider["ins"], *rider["prev"])
    return outs[:n_out], outs[n_out:]


def _run_rider(rider, *, name):
    any_spec = BS(memory_space=pl.ANY)
    n_rin = len(rider["ins"])

    def body(*refs):
        _, rins, routs, sems = _split_rider_refs(refs, 0, 0, rider)
        rider["start"](rins, routs, sems)
        rider["finish"](rins, routs, sems)

    return pl.pallas_call(
        body, name=name, in_specs=[any_spec] * (n_rin + len(rider["prev"])), out_specs=[any_spec] * len(rider["out_shape"]),
        out_shape=list(rider["out_shape"]), scratch_shapes=list(rider["scratch"]),
        input_output_aliases={n_rin + t: t for t in range(len(rider["prev"]))},
        compiler_params=pltpu.CompilerParams(has_side_effects=True))(*rider["ins"], *rider["prev"])


def _div_tile(n, pref, mult):
    t = min(pref, n)
    t -= t % mult
    while t >= mult:
        if n % t == 0:
            return t
        t -= mult
    return n


def _mm(a, b, *, name, ta=False, tb=False, add=None, out_dtype=F32, tm=1024, tn=1024, tk=1024):
    if ta:
        K, M = a.shape
    else:
        M, K = a.shape
    N = b.shape[0] if tb else b.shape[1]
    tm, tn, tk = _div_tile(M, tm, LANE), _div_tile(N, tn, LANE), _div_tile(K, tk, LANE)
    nk = K // tk
    dims = (((0,) if ta else (1,), (1,) if tb else (0,)), ((), ()))

    def body_single(*refs):
        r = lax.dot_general(refs[0][...], refs[1][...], dims, preferred_element_type=F32)
        if add is not None:
            r = r + refs[2][...]
        refs[-1][...] = r.astype(out_dtype)

    def body(*refs):
        if add is None:
            a_ref, b_ref, o_ref, acc_ref = refs
        else:
            a_ref, b_ref, add_ref, o_ref, acc_ref = refs
        k = pl.program_id(2)

        @pl.when(k == 0)
        def _():
            acc_ref[...] = jnp.zeros_like(acc_ref)

        acc_ref[...] += lax.dot_general(a_ref[...], b_ref[...], dims, preferred_element_type=F32)

        @pl.when(k == nk - 1)
        def _():
            r = acc_ref[...]
            if add is not None:
                r = r + add_ref[...]
            o_ref[...] = r.astype(out_dtype)

    a_spec = BS((tk, tm), lambda i, j, k: (k, i)) if ta else BS((tm, tk), lambda i, j, k: (i, k))
    b_spec = BS((tn, tk), lambda i, j, k: (j, k)) if tb else BS((tk, tn), lambda i, j, k: (k, j))
    in_specs, args = [a_spec, b_spec], [a, b]
    if add is not None:
        in_specs.append(BS((tm, tn), lambda i, j, k: (i, j)))
        args.append(add)
    return _pc(body_single if nk == 1 else body, name=name, grid=(M // tm, N // tn, nk), in_specs=in_specs,
               out_specs=BS((tm, tn), lambda i, j, k: (i, j)), out_shape=SDS((M, N), out_dtype),
               scratch_shapes=[] if nk == 1 else [pltpu.VMEM((tm, tn), F32)])(*args)


def _dot(a, b, dims):
    return lax.dot_general(a.astype(BF16), b.astype(BF16), (dims, ((), ())), preferred_element_type=F32)


@jax.custom_vjp
def _nn(a, b):
    return _dot(a, b, ((1,), (0,)))


@jax.custom_vjp
def _nt(a, b):
    return _dot(a, b, ((1,), (1,)))


@jax.custom_vjp
def _tn(a, b):
    return _dot(a, b, ((0,), (0,)))


_nn.defvjp(lambda a, b: (_nn(a, b), (a, b)), lambda r, g: (_nt(g, r[1]), _tn(r[0], g)))
_nt.defvjp(lambda a, b: (_nt(a, b), (a, b)), lambda r, g: (_nn(g, r[1]), _tn(g, r[0])))
_tn.defvjp(lambda a, b: (_tn(a, b), (a, b)), lambda r, g: (_nt(r[1], g), _nn(r[0], g)))


def _hdot(m, x):
    hi = x.astype(BF16)
    r1 = x - hi.astype(F32)
    lo = r1.astype(BF16)
    lo2 = (r1 - lo.astype(F32)).astype(BF16)
    n = x.shape[1]
    out = lax.dot_general(m.astype(BF16), jnp.concatenate([hi, lo, lo2], axis=1), (((1,), (0,)), ((), ())),
                          preferred_element_type=F32)
    return out[:, :n] + out[:, n:2 * n] + out[:, 2 * n:]


@jax.custom_vjp
def _cumdot(m, mt, x):
    return _hdot(m, x)


_cumdot.defvjp(lambda m, mt, x: (_hdot(m, x), (m, mt)),
               lambda r, g: (jnp.zeros_like(r[0]), jnp.zeros_like(r[1]), _hdot(r[1], g)))


def _sigmoid(x):
    return 1.0 / (1.0 + jnp.exp(-x))


def _softplus(x):
    return jnp.maximum(x, 0.0) + jnp.log(1.0 + jnp.exp(-jnp.abs(x)))


def _rms(x, w):
    return x * lax.rsqrt(jnp.mean(x * x, axis=-1, keepdims=True) + NORM_EPS) * w


def _norm_mm(x2, nw, b, *, name, out_dtype=F32, tm=1024, tn=1024):
    T, D = x2.shape
    N = b.shape[1]
    tm, tn = _div_tile(T, tm, LANE), _div_tile(N, tn, LANE)

    def body(x_ref, w_ref, b_ref, h_ref, o_ref):
        @pl.when(pl.program_id(1) == 0)
        def _():
            h_ref[...] = _rms(x_ref[...], w_ref[...]).astype(BF16)

        o_ref[...] = lax.dot_general(h_ref[...], b_ref[...], (((1,), (0,)), ((), ())), preferred_element_type=F32).astype(out_dtype)

    return _pc(body, name=name, grid=(T // tm, N // tn),
               in_specs=[BS((tm, D), lambda i, j: (i, 0)), BS((1, D), lambda i, j: (0, 0)), BS((D, tn), lambda i, j: (0, j))],
               out_specs=[BS((tm, D), lambda i, j: (i, 0)), BS((tm, tn), lambda i, j: (i, j))],
               out_shape=[SDS((T, D), BF16), SDS((T, N), out_dtype)])(x2, nw.reshape(1, D), b)


def _mm_dnorm(a, b, x2, nw, resid, *, name, tk=1024):
    T, D = x2.shape
    dh = _mm(a, b, tb=True, name=name, tk=tk)
    tr = _div_tile(T, 512, 16)

    def body(x_ref, w_ref, dh_ref, r_ref, dx_ref, dxb_ref, dw_ref):
        _, vjp = jax.vjp(_rms, x_ref[...], w_ref[...])
        dx, dw = vjp(dh_ref[...])
        dx = dx + r_ref[...]
        dx_ref[...] = dx
        dxb_ref[...] = dx.astype(BF16)

        @pl.when(pl.program_id(0) == 0)
        def _():
            dw_ref[...] = jnp.zeros_like(dw_ref)

        dw_ref[...] += dw

    row = BS((tr, D), lambda i: (i, 0))
    one = BS((1, D), lambda i: (0, 0))
    return _pc(body, name=name + "_norm", grid=(T // tr,), in_specs=[row, one, row, row], out_specs=[row, row, one],
               out_shape=[SDS((T, D), F32), SDS((T, D), BF16), SDS((1, D), F32)])(x2, nw.reshape(1, D), dh, resid)


ROW_PAD = 8


def _pad_rows(x):
    return jnp.concatenate([x, jnp.zeros((ROW_PAD, x.shape[1]), x.dtype)], axis=0)


def _shift_rows(xp, s):
    n = xp.shape[0] - ROW_PAD
    return xp[:n] if s == 0 else pltpu.roll(xp, (-s) % xp.shape[0], 0)[:n]


def _conv_taps(x, taps):
    xp = _pad_rows(x)
    return [_shift_rows(xp, k - taps // 2) for k in range(taps)]


def _conv_pre(xs, w_ref, b_ref):
    c = b_ref[...] + w_ref[0:1, :] * xs[0]
    for k in range(1, len(xs)):
        c = c + w_ref[k:k + 1, :] * xs[k]
    return c


def _conv_fwd(x3, x_blk0, w, b, *, taps, ct, gate_blk0=None, out_dtype, name):
    B, L, _ = x3.shape
    C = w.shape[1]
    wp = jnp.zeros((8, C), F32).at[:taps].set(w)

    def body(*refs):
        if gate_blk0 is None:
            x_ref, w_ref, b_ref, o_ref = refs
        else:
            x_ref, u_ref, w_ref, b_ref, o_ref = refs
        c = _conv_pre(_conv_taps(x_ref[0].astype(F32), taps), w_ref, b_ref)
        y = c * _sigmoid(c)
        if gate_blk0 is not None:
            y = y * u_ref[0].astype(F32)
        o_ref[0] = y.astype(out_dtype)

    in_specs = [BS((1, L, ct), lambda bi, j: (bi, 0, x_blk0 + j))]
    args = [x3]
    if gate_blk0 is not None:
        in_specs.append(BS((1, L, ct), lambda bi, j: (bi, 0, gate_blk0 + j)))
        args.append(x3)
    in_specs += [BS((8, ct), lambda bi, j: (0, j)), BS((1, ct), lambda bi, j: (0, j))]
    args += [wp, b.reshape(1, C)]
    return _pc(body, name=name, grid=(B, C // ct), in_specs=in_specs,
               out_specs=BS((1, L, ct), lambda bi, j: (bi, 0, j)), out_shape=SDS((B, L, C), out_dtype))(*args)


def _conv_bwd(x3, x_blk0, w, b, dy3, *, taps, ct, gate_blk0=None, name):
    B, L, _ = x3.shape
    C = w.shape[1]
    wp = jnp.zeros((8, C), F32).at[:taps].set(w)
    gated = gate_blk0 is not None

    def body(*refs):
        if gated:
            x_ref, u_ref, w_ref, b_ref, dy_ref, dx_ref, du_ref, dw_ref, db_ref = refs
        else:
            x_ref, w_ref, b_ref, dy_ref, dx_ref, dw_ref, db_ref = refs
        xs = _conv_taps(x_ref[0].astype(F32), taps)
        dy = dy_ref[0].astype(F32)
        c = _conv_pre(xs, w_ref, b_ref)
        sg = _sigmoid(c)
        dsilu = sg * (1.0 + c * (1.0 - sg))
        if gated:
            du_ref[0] = (dy * (c * sg)).astype(BF16)
            dc = dy * u_ref[0].astype(F32) * dsilu
        else:
            dc = dy * dsilu
        dcp = _pad_rows(dc)
        dx = jnp.zeros_like(dc)
        dw_ref[0] = jnp.zeros((8, ct), F32)
        for k in range(taps):
            dx = dx + w_ref[k:k + 1, :] * _shift_rows(dcp, taps // 2 - k)
            dw_ref[0, k:k + 1, :] = jnp.sum(dc * xs[k], axis=0, keepdims=True)
        dx_ref[0] = dx.astype(BF16)
        db_ref[0] = jnp.sum(dc, axis=0, keepdims=True)

    xs = BS((1, L, ct), lambda bi, j: (bi, 0, x_blk0 + j))
    ys = BS((1, L, ct), lambda bi, j: (bi, 0, j))
    in_specs, args = [xs], [x3]
    if gated:
        in_specs.append(BS((1, L, ct), lambda bi, j: (bi, 0, gate_blk0 + j)))
        args.append(x3)
    in_specs += [BS((8, ct), lambda bi, j: (0, j)), BS((1, ct), lambda bi, j: (0, j)), ys]
    args += [wp, b.reshape(1, C), dy3]
    out_specs = [ys] + ([ys] if gated else []) + [BS((1, 8, ct), lambda bi, j: (bi, 0, j)), BS((1, 1, ct), lambda bi, j: (bi, 0, j))]
    out_shape = [SDS((B, L, C), BF16)] + ([SDS((B, L, C), BF16)] if gated else []) + [SDS((B, 8, C), F32), SDS((B, 1, C), F32)]
    return _pc(body, name=name, grid=(B, C // ct), in_specs=in_specs, out_specs=out_specs, out_shape=out_shape)(*args)


def _tri(reverse):
    r = lax.broadcasted_iota(jnp.int32, (CHUNK, CHUNK), 0)
    c = lax.broadcasted_iota(jnp.int32, (CHUNK, CHUNK), 1)
    return (c >= r) if reverse else (c <= r)


PAIRS = 2
QUADS = SSM_HEADS // (2 * PAIRS)
QW = PAIRS * LANE


def _ssd_chunk(h0, h1, x0, x1, bm, cm, dtc, alog, *, col0, reverse):
    mask = _tri(reverse)
    eye = lax.broadcasted_iota(jnp.int32, (CHUNK, CHUNK), 0) == lax.broadcasted_iota(jnp.int32, (CHUNK, CHUNK), 1)
    lane = lax.broadcasted_iota(jnp.int32, (1, LANE), 1)
    first = lane < HEAD_DIM
    adt = dtc * (-jnp.exp(alog))
    cumc = _cumdot(mask.astype(F32), _tri(not reverse).astype(F32), adt)
    totc = jnp.sum(adt, axis=0, keepdims=True)
    cb = _nt(cm, bm)

    def col(v, c):
        return jnp.sum(jnp.where(lane == c, v, 0.0), axis=1, keepdims=True)

    outs, states = [], []
    for p, (hprev, xs) in enumerate(((h0, x0), (h1, x1))):
        c0 = col0 + 2 * p
        cj = (col(cumc, c0), col(cumc, c0 + 1))
        cum = jnp.where(first, cj[0], cj[1])
        tot = jnp.where(first, col(totc, c0), col(totc, c0 + 1))
        xdt = xs * jnp.where(first, col(dtc, c0), col(dtc, c0 + 1))
        y = _nn(cm, hprev) * jnp.exp(cum)
        for j in range(2):
            rj = jnp.sum(jnp.where(eye, cj[j], 0.0), axis=0, keepdims=True)
            dec = jnp.exp(jnp.where(mask, cj[j] - rj, NEG))
            y = y + _nn(cb * dec, jnp.where(first if j == 0 else ~first, xdt, 0.0))
        outs.append(y)
        states.append(hprev * jnp.exp(tot) + _tn(bm, xdt * jnp.exp(tot - cum)))
    return outs[0], outs[1], states[0], states[1]


def _ssd_specs(B, L):
    def lanes(w, blk):
        return BS((1, L, w), blk)

    return [
        lanes(QW, lambda b, q: (b, 0, q)),
        lanes(LANE, lambda b, q: (b, 0, 8 + q // 2)),
        lanes(LANE, lambda b, q: (b, 0, 10 + q // 2)),
        lanes(LANE, lambda b, q: (b, 0, 0)),
        BS((1, LANE), lambda b, q: (0, 0)),
        BS((1, QW), lambda b, q: (0, q)),
    ]


def _ssd_slot(d, ci):
    return ci if d == 0 else ci + 1


def _ssd_fwd(xbc_act, dtc, alog, dskip, *, name, rider=None):
    B, L, _ = xbc_act.shape
    nc = L // CHUNK

    def body(xs_ref, b_ref, c_ref, dt_ref, alog_ref, dsk_ref, y_ref, hs_ref):
        q = pl.program_id(1)
        alog_v = alog_ref[...]
        y_ref[0] = dsk_ref[...] * xs_ref[0]
        hs_ref[0, 0, 0, 0] = jnp.zeros((LANE, QW), F32)
        hs_ref[0, 0, 1, nc] = jnp.zeros((LANE, QW), F32)

        def step(i, carry):
            cis = (i, nc - 1 - i)
            rows = [pl.ds(pl.multiple_of(ci * CHUNK, CHUNK), CHUNK) for ci in cis]
            res = []
            for d in range(2):
                cur = _ssd_slot(d, cis[d])
                res.append(_ssd_chunk(
                    hs_ref[0, 0, d, cur, :, :LANE], hs_ref[0, 0, d, cur, :, LANE:], xs_ref[0, rows[d], :LANE],
                    xs_ref[0, rows[d], LANE:], b_ref[0, rows[d], :], c_ref[0, rows[d], :], dt_ref[0, rows[d], :], alog_v,
                    col0=SSM_HEADS * d + 2 * PAIRS * q, reverse=d == 1))
            for d in range(2):
                y0, y1, n0, n1 = res[d]
                nxt = _ssd_slot(d, cis[d] + 1 if d == 0 else cis[d] - 1)
                hs_ref[0, 0, d, nxt, :, :LANE] = n0
                hs_ref[0, 0, d, nxt, :, LANE:] = n1
                y_ref[0, rows[d], :LANE] += y0
                y_ref[0, rows[d], LANE:] += y1
            return carry

        lax.fori_loop(0, nc, step, 0, unroll=2)

    (y, hs), carried = _pc_carry(
        body, (xbc_act, xbc_act, xbc_act, dtc, alog, dskip), name=name, grid=(B, QUADS), in_specs=_ssd_specs(B, L),
        out_specs=[BS((1, L, QW), lambda b, q: (b, 0, q)), BS((1, 1, 2, nc + 1, LANE, QW), lambda b, q: (b, q, 0, 0, 0, 0))],
        out_shape=[SDS((B, L, SSM_WIDTH), F32), SDS((B, QUADS, 2, nc + 1, LANE, QW), F32)], rider=rider)
    return y, hs, carried


def _ssd_bwd(xbc_act, dtc, alog, dskip, hs, dy, *, name, rider=None):
    B, L, _ = xbc_act.shape
    nc = L // CHUNK

    def body(xs_ref, b_ref, c_ref, dt_ref, alog_ref, dsk_ref, hs_ref, dy_ref,
             dxs_ref, db_ref, dc_ref, ddt_ref, dalog_ref, ddsk_ref, dh_ref):
        q = pl.program_id(1)
        alog_v = alog_ref[...]

        @pl.when(q % 2 == 0)
        def _():
            db_ref[...] = jnp.zeros_like(db_ref)
            dc_ref[...] = jnp.zeros_like(dc_ref)

        @pl.when(q == 0)
        def _():
            ddt_ref[...] = jnp.zeros_like(ddt_ref)

        dxs_ref[0] = dy_ref[0] * dsk_ref[...]
        ddsk_ref[0] = jnp.sum(dy_ref[0] * xs_ref[0], axis=0, keepdims=True)
        dh_ref[...] = jnp.zeros_like(dh_ref)

        def step(i, carry):
            g_alog = carry
            cis = (nc - 1 - i, i)
            rows = [pl.ds(pl.multiple_of(ci * CHUNK, CHUNK), CHUNK) for ci in cis]
            res = []
            for d in range(2):
                cur = _ssd_slot(d, cis[d])
                fn = functools.partial(_ssd_chunk, col0=SSM_HEADS * d + 2 * PAIRS * q, reverse=d == 1)
                _, vjp = jax.vjp(fn, hs_ref[0, 0, d, cur, :, :LANE], hs_ref[0, 0, d, cur, :, LANE:], xs_ref[0, rows[d], :LANE],
                                 xs_ref[0, rows[d], LANE:], b_ref[0, rows[d], :], c_ref[0, rows[d], :], dt_ref[0, rows[d], :],
                                 alog_v)
                res.append(vjp((dy_ref[0, rows[d], :LANE], dy_ref[0, rows[d], LANE:], dh_ref[d, :, :LANE], dh_ref[d, :, LANE:])))
            for d in range(2):
                g_h0, g_h1, g_x0, g_x1, g_b, g_c, g_dt, g_alog1 = res[d]
                dh_ref[d, :, :LANE] = g_h0
                dh_ref[d, :, LANE:] = g_h1
                dxs_ref[0, rows[d], :LANE] += g_x0
                dxs_ref[0, rows[d], LANE:] += g_x1
                db_ref[0, rows[d], :] += g_b
                dc_ref[0, rows[d], :] += g_c
                ddt_ref[0, rows[d], :] += g_dt
                g_alog = g_alog + g_alog1
            return g_alog

        dalog_ref[0, 0] = lax.fori_loop(0, nc, step, jnp.zeros((1, LANE), F32))

    lanes = lambda w, blk: BS((1, L, w), blk)
    in_specs = _ssd_specs(B, L) + [BS((1, 1, 2, nc + 1, LANE, QW), lambda b, q: (b, q, 0, 0, 0, 0)), lanes(QW, lambda b, q: (b, 0, q))]
    out_specs = [lanes(QW, lambda b, q: (b, 0, q)), lanes(LANE, lambda b, q: (b, 0, q // 2)), lanes(LANE, lambda b, q: (b, 0, q // 2)),
                 lanes(LANE, lambda b, q: (b, 0, 0)), BS((1, 1, 1, LANE), lambda b, q: (b, q, 0, 0)),
                 BS((1, 1, QW), lambda b, q: (b, 0, q))]
    out_shape = [SDS((B, L, SSM_WIDTH), F32), SDS((B, L, BC_WIDTH), F32), SDS((B, L, BC_WIDTH), F32), SDS((B, L, LANE), F32),
                 SDS((B, QUADS, 1, LANE), F32), SDS((B, 1, SSM_WIDTH), F32)]
    outs, carried = _pc_carry(body, (xbc_act, xbc_act, xbc_act, dtc, alog, dskip, hs, dy), name=name, grid=(B, QUADS),
                              in_specs=in_specs, out_specs=out_specs, out_shape=out_shape,
                              scratch_shapes=[pltpu.VMEM((2, LANE, QW), F32)], rider=rider)
    return (*outs, carried)


def _gate_norm(yp, z, w):
    v = yp * (z * _sigmoid(z))
    return v * lax.rsqrt(jnp.mean(v * v, axis=-1, keepdims=True) + NORM_EPS) * w


def _gate_fwd(ypre2, proj2, w, *, name):
    T = ypre2.shape[0]
    tr = _div_tile(T, 512, 8)
    G = 512

    def body(y_ref, z_ref, w_ref, o_ref):
        o_ref[...] = _gate_norm(y_ref[...], z_ref[...], w_ref[...]).astype(BF16)

    return _pc(body, name=name, grid=(T // tr, 2),
               in_specs=[BS((tr, G), lambda i, g: (i, g)), BS((tr, G), lambda i, g: (i, 2 + g)), BS((1, G), lambda i, g: (0, g))],
               out_specs=BS((tr, G), lambda i, g: (i, g)), out_shape=SDS((T, 2 * SSM_WIDTH), BF16))(ypre2, proj2, w.reshape(1, -1))


def _gate_bwd(ypre2, proj2, w, dy, *, name):
    T = ypre2.shape[0]
    tr = _div_tile(T, 512, 8)
    G = 512

    def body(y_ref, z_ref, w_ref, dy_ref, dyp_ref, dz_ref, dw_ref):
        _, vjp = jax.vjp(_gate_norm, y_ref[...], z_ref[...], w_ref[...])
        dyp, dz, dw = vjp(dy_ref[...])
        dyp_ref[...] = dyp
        dz_ref[...] = dz.astype(BF16)
        dw_ref[0] = dw

    tile = BS((tr, G), lambda i, g: (i, g))
    return _pc(body, name=name, grid=(T // tr, 2),
               in_specs=[tile, BS((tr, G), lambda i, g: (i, 2 + g)), BS((1, G), lambda i, g: (0, g)), tile],
               out_specs=[tile, tile, BS((1, 1, G), lambda i, g: (i, 0, g))],
               out_shape=[SDS((T, SSM_WIDTH), F32), SDS((T, SSM_WIDTH), BF16), SDS((T // tr, 1, SSM_WIDTH), F32)])(
        ypre2, proj2, w.reshape(1, -1), dy)


def _first_half():
    return lax.broadcasted_iota(jnp.int32, (1, LANE), 1) < HEAD_DIM


def _dup_kv_head(pair, odd):
    rolled = pltpu.roll(pair, HEAD_DIM, 1)
    return jnp.where(_first_half(), rolled, pair) if odd else jnp.where(_first_half(), pair, rolled)


def _stack_heads(quad):
    first = _first_half()
    lo, hi = quad[:, :LANE], quad[:, LANE:]
    return jnp.concatenate([jnp.where(first, lo, 0.0), jnp.where(first, 0.0, lo), jnp.where(first, hi, 0.0),
                            jnp.where(first, 0.0, hi)], axis=0)


def _unstack_heads(o):
    first = _first_half()
    return jnp.concatenate([jnp.where(first, o[:BLOCK], o[BLOCK:2 * BLOCK]), jnp.where(first, o[2 * BLOCK:3 * BLOCK], o[3 * BLOCK:])], axis=1)


def _fold_kv_head(d, odd):
    tot = d + pltpu.roll(d, HEAD_DIM, 1)
    return jnp.where(_first_half(), 0.0, tot) if odd else jnp.where(_first_half(), tot, 0.0)


def _attn_softmax(s, sink):
    m = jnp.maximum(jnp.max(s, axis=-1, keepdims=True), sink)
    p = jnp.exp(s - m)
    ps = jnp.exp(sink - m)
    inv = 1.0 / (jnp.sum(p, axis=-1, keepdims=True) + ps)
    return p * inv, ps * inv


def _attn_colneg(n, L):
    kpos = n * BLOCK - WINDOW + lax.broadcasted_iota(jnp.int32, (1, KEY_SPAN), 1)
    return jnp.where((kpos >= 0) & (kpos < L), 0.0, NEG)


def _attn_in_specs(L):
    nblk = L // BLOCK
    kv = lambda o, col: BS((1, BLOCK, 4 * HEAD_DIM), lambda b, n: (b, jnp.clip(n + o, 0, nblk - 1), col))
    kcol, vcol = 3584 // 256, 3840 // 256
    return [BS((1, BLOCK, ATTN_HEADS * HEAD_DIM), lambda b, n: (b, n, 0)), kv(-1, kcol), kv(0, kcol), kv(1, kcol),
            kv(-1, vcol), kv(0, vcol), kv(1, vcol),
            BS((ATTN_HEADS, BLOCK, KEY_SPAN), lambda b, n: (0, 0, 0)), BS((ATTN_HEADS * BLOCK, 1), lambda b, n: (0, 0))]


def _attn_fwd(proj, bias, sinkcol, mixed, *, name, rider=None):
    B, L, _ = proj.shape

    def body(q_ref, k0, k1, k2, v0, v1, v2, bias_ref, sink_ref, _, o_ref):
        colneg = _attn_colneg(pl.program_id(1), L)
        kcat = jnp.concatenate([k0[0], k1[0], k2[0]], axis=0)
        vcat = jnp.concatenate([v0[0], v1[0], v2[0]], axis=0)
        scores, probs = [], []
        for g in range(KV_HEADS):
            pair = slice(LANE * (g // 2), LANE * (g // 2) + LANE)
            quad = slice(4 * HEAD_DIM * g, 4 * HEAD_DIM * (g + 1))
            kd = _dup_kv_head(kcat[:, pair], g % 2).astype(BF16)
            qs = (_stack_heads(q_ref[0, :, quad]) * HEAD_DIM ** -0.5).astype(BF16)
            scores.append(lax.dot_general(qs, kd, (((1,), (1,)), ((), ())), preferred_element_type=F32))
        for g in range(KV_HEADS):
            pn, _ = _attn_softmax(scores[g] + bias_ref[4 * g:4 * g + 4].reshape(4 * BLOCK, KEY_SPAN) + colneg,
                                  sink_ref[4 * BLOCK * g:4 * BLOCK * (g + 1)])
            probs.append(pn.astype(BF16))
        for g in range(KV_HEADS):
            pair = slice(LANE * (g // 2), LANE * (g // 2) + LANE)
            quad = slice(4 * HEAD_DIM * g, 4 * HEAD_DIM * (g + 1))
            vd = _dup_kv_head(vcat[:, pair], g % 2).astype(BF16)
            o = lax.dot_general(probs[g], vd, (((1,), (0,)), ((), ())), preferred_element_type=F32)
            o_ref[0, :, quad] = _unstack_heads(o).astype(BF16)

    (out,), carried = _pc_carry(body, (proj, proj, proj, proj, proj, proj, proj, bias, sinkcol, mixed), name=name,
                                grid=(B, L // BLOCK), in_specs=_attn_in_specs(L) + [BS(memory_space=pl.ANY)],
                                out_specs=[BS((1, BLOCK, ATTN_HEADS * HEAD_DIM), lambda b, n: (b, n, 1))],
                                out_shape=[SDS(mixed.shape, BF16)], rider=rider, aliases={9: 0})
    return out, carried


def _attn_bwd(proj, bias, sinkcol, dout, *, name, rider=None):
    B, L, _ = proj.shape
    nblk = L // BLOCK
    nn, nt, tn = (((1,), (0,)), ((), ())), (((1,), (1,)), ((), ())), (((0,), (0,)), ((), ()))

    def body(q_ref, k0, k1, k2, v0, v1, v2, bias_ref, sink_ref, do_ref, dq_ref, dk_ref, dv_ref, dbias_ref, dsink_ref):
        b, n = pl.program_id(0), pl.program_id(1)

        @pl.when(n == 0)
        def _():
            dk_ref[...] = jnp.zeros_like(dk_ref)
            dv_ref[...] = jnp.zeros_like(dv_ref)

        @pl.when((n == 0) & (b == 0))
        def _():
            dbias_ref[...] = jnp.zeros_like(dbias_ref)
            dsink_ref[...] = jnp.zeros_like(dsink_ref)

        colneg = _attn_colneg(n, L)
        kcat = jnp.concatenate([k0[0], k1[0], k2[0]], axis=0)
        vcat = jnp.concatenate([v0[0], v1[0], v2[0]], axis=0)
        krows = [pl.ds(pl.multiple_of(jnp.clip(n + o, 0, nblk - 1) * BLOCK, BLOCK), BLOCK) for o in (-1, 0, 1)]
        ops, mids = [], []
        for g in range(KV_HEADS):
            pair = slice(LANE * (g // 2), LANE * (g // 2) + LANE)
            quad = slice(4 * HEAD_DIM * g, 4 * HEAD_DIM * (g + 1))
            kd = _dup_kv_head(kcat[:, pair], g % 2).astype(BF16)
            vd = _dup_kv_head(vcat[:, pair], g % 2).astype(BF16)
            qs = (_stack_heads(q_ref[0, :, quad]) * HEAD_DIM ** -0.5).astype(BF16)
            dos = _stack_heads(do_ref[0, :, quad].astype(F32)).astype(BF16)
            ops.append((kd, qs, dos, lax.dot_general(qs, kd, nt, preferred_element_type=F32),
                        lax.dot_general(dos, vd, nt, preferred_element_type=F32)))
        for g in range(KV_HEADS):
            rows = slice(4 * BLOCK * g, 4 * BLOCK * (g + 1))
            _, _, _, s, dpn = ops[g]
            pn, psink = _attn_softmax(s + bias_ref[4 * g:4 * g + 4].reshape(4 * BLOCK, KEY_SPAN) + colneg, sink_ref[rows])
            r = jnp.sum(dpn * pn, axis=-1, keepdims=True)
            ds = pn * (dpn - r)
            dbias_ref[4 * g:4 * g + 4] += ds.reshape(4, BLOCK, KEY_SPAN)
            dsink_ref[rows] += -psink * r
            mids.append((pn.astype(BF16), ds.astype(BF16)))
        for g in range(KV_HEADS):
            pair = slice(LANE * (g // 2), LANE * (g // 2) + LANE)
            quad = slice(4 * HEAD_DIM * g, 4 * HEAD_DIM * (g + 1))
            kd, qs, dos, _, _ = ops[g]
            pnb, dsb = mids[g]
            dvd = lax.dot_general(pnb, dos, tn, preferred_element_type=F32)
            dkd = lax.dot_general(dsb, qs, tn, preferred_element_type=F32)
            dqs = lax.dot_general(dsb, kd, nn, preferred_element_type=F32) * HEAD_DIM ** -0.5
            dq_ref[0, :, quad] = _unstack_heads(dqs).astype(BF16)
            dk_g, dv_g = _fold_kv_head(dkd, g % 2), _fold_kv_head(dvd, g % 2)
            for o in range(3):
                dk_ref[0, krows[o], pair] += dk_g[o * BLOCK:(o + 1) * BLOCK]
                dv_ref[0, krows[o], pair] += dv_g[o * BLOCK:(o + 1) * BLOCK]

    qspec = BS((1, BLOCK, ATTN_HEADS * HEAD_DIM), lambda b, n: (b, n, 0))
    kvout = BS((1, L, 4 * HEAD_DIM), lambda b, n: (b, 0, 0))
    outs, carried = _pc_carry(
        body, (proj, proj, proj, proj, proj, proj, proj, bias, sinkcol, dout), name=name, grid=(B, nblk),
        in_specs=_attn_in_specs(L) + [BS((1, BLOCK, ATTN_HEADS * HEAD_DIM), lambda b, n: (b, n, 1))],
        out_specs=[qspec, kvout, kvout, BS((ATTN_HEADS, BLOCK, KEY_SPAN), lambda b, n: (0, 0, 0)),
                   BS((ATTN_HEADS * BLOCK, 1), lambda b, n: (0, 0))],
        out_shape=[SDS((B, L, ATTN_HEADS * HEAD_DIM), BF16), SDS((B, L, 4 * HEAD_DIM), F32), SDS((B, L, 4 * HEAD_DIM), F32),
                   SDS((ATTN_HEADS, BLOCK, KEY_SPAN), F32), SDS((ATTN_HEADS * BLOCK, 1), F32)], rider=rider)
    return (*outs, carried)


def _t5_bucket(rel):
    half = REL_BUCKETS // 2
    max_exact = half // 2
    ret = jnp.where(rel > 0, half, 0)
    n = jnp.abs(rel)
    nf = jnp.maximum(n, 1).astype(F32)
    large = max_exact + (jnp.log(nf / max_exact) / math.log(REL_MAX_DIST / max_exact) * (half - max_exact)).astype(jnp.int32)
    large = jnp.minimum(large, half - 1)
    return ret + jnp.where(n < max_exact, n, large)


def _bucket_table():
    rel = jnp.arange(KEY_SPAN)[None, :] - WINDOW - jnp.arange(BLOCK)[:, None]
    return _t5_bucket(rel).astype(jnp.int32)


def _bias_expand(rel_bias, bucket, *, name):
    rbt = jnp.zeros((ATTN_HEADS, 1, LANE), F32).at[:, 0, :REL_BUCKETS].set(rel_bias.T)

    def body(rb_ref, bk_ref, o_ref):
        lane = lax.broadcasted_iota(jnp.int32, (1, LANE), 1)
        row = rb_ref[0]
        bk = bk_ref[...]
        acc = jnp.zeros((BLOCK, KEY_SPAN), F32)
        for r in range(REL_BUCKETS):
            val = jnp.sum(jnp.where(lane == r, row, 0.0), axis=1, keepdims=True)
            acc = jnp.where(bk == r, val, acc)
        rel = (lax.broadcasted_iota(jnp.int32, (BLOCK, KEY_SPAN), 1) - WINDOW
               - lax.broadcasted_iota(jnp.int32, (BLOCK, KEY_SPAN), 0))
        o_ref[0] = jnp.where(jnp.abs(rel) <= WINDOW, acc, NEG)

    return _pc(body, name=name, grid=(ATTN_HEADS,),
               in_specs=[BS((1, 1, LANE), lambda h: (h, 0, 0)), BS((BLOCK, KEY_SPAN), lambda h: (0, 0))],
               out_specs=BS((1, BLOCK, KEY_SPAN), lambda h: (h, 0, 0)), out_shape=SDS((ATTN_HEADS, BLOCK, KEY_SPAN), F32))(rbt, bucket)


def _bias_reduce(dbias, bucket, *, name):
    def body(db_ref, bk_ref, o_ref):
        lane = lax.broadcasted_iota(jnp.int32, (1, LANE), 1)
        x = db_ref[0]
        bk = bk_ref[...]
        acc = jnp.zeros((1, LANE), F32)
        for r in range(REL_BUCKETS):
            part = jnp.sum(jnp.where(bk == r, x, 0.0), axis=1, keepdims=True)
            acc = jnp.where(lane == r, jnp.sum(part, axis=0, keepdims=True), acc)
        o_ref[0] = acc

    out = _pc(body, name=name, grid=(ATTN_HEADS,),
              in_specs=[BS((1, BLOCK, KEY_SPAN), lambda h: (h, 0, 0)), BS((BLOCK, KEY_SPAN), lambda h: (0, 0))],
              out_specs=BS((1, 1, LANE), lambda h: (h, 0, 0)), out_shape=SDS((ATTN_HEADS, 1, LANE), F32))(dbias, bucket)
    return out[:, 0, :REL_BUCKETS].T


def _loss_head(x2, w, target, *, name):
    T, D = x2.shape
    tr = _div_tile(T, 512, 8)

    def tile_loss(x, w, t):
        err = _rms(x, w) - t
        return 0.5 * jnp.sum(jnp.mean(err * err, axis=-1, keepdims=True), axis=0, keepdims=True)

    def body(x_ref, w_ref, t_ref, loss_ref, dx_ref, dxb_ref, dw_ref):
        t = t_ref[...]
        l, vjp = jax.vjp(lambda x, w: tile_loss(x, w, t), x_ref[...], w_ref[...])
        dx, dw = vjp(jnp.ones((1, 1), F32))
        dx_ref[...] = dx
        dxb_ref[...] = dx.astype(BF16)

        @pl.when(pl.program_id(0) == 0)
        def _():
            dw_ref[...] = jnp.zeros_like(dw_ref)
            loss_ref[...] = jnp.zeros_like(loss_ref)

        dw_ref[...] += dw
        loss_ref[...] += l + jnp.zeros((1, LANE), F32)

    row = BS((tr, D), lambda i: (i, 0))
    one = BS((1, D), lambda i: (0, 0))
    return _pc(body, name=name, grid=(T // tr,), in_specs=[row, one, row],
               out_specs=[BS((1, LANE), lambda i: (0, 0)), row, row, one],
               out_shape=[SDS((1, LANE), F32), SDS((T, D), F32), SDS((T, D), BF16), SDS((1, D), F32)])(x2, w.reshape(1, D), target)


def _adamw(w2, g2, m2, v2, *, name):
    R, C = w2.shape
    tr = _div_tile(R, 256, 8)
    c1 = 1.0 - ADAM_B1 ** ADAM_STEP
    c2 = 1.0 - ADAM_B2 ** ADAM_STEP

    def body(w_ref, g_ref, m_ref, v_ref, d_ref, nm_ref, nv_ref):
        g = g_ref[...]
        m = ADAM_B1 * m_ref[...] + (1.0 - ADAM_B1) * g
        v = ADAM_B2 * v_ref[...] + (1.0 - ADAM_B2) * (g * g)
        d_ref[...] = -ADAM_LR * ((m / c1) / (jnp.sqrt(v / c2) + ADAM_EPS) + ADAM_WD * w_ref[...])
        nm_ref[...] = m
        nv_ref[...] = v

    t = BS((tr, C), lambda i: (i, 0))
    return _pc(body, name=name, grid=(R // tr,), in_specs=[t, t, t, t], out_specs=[t, t, t],
               out_shape=[SDS((R, C), F32)] * 3)(w2, g2, m2, v2)


def _place():
    return lax.axis_index("x"), lax.axis_index("y"), lax.axis_index("c")


def _gather_rider(shards):
    na = len(shards)

    def copies(ins, outs, sems):
        send_sems, recv_sems = sems
        x, y, c = _place()
        for a in range(na):
            for k, (px, py) in enumerate([(1 - x, y), (x, 1 - y), (1 - x, 1 - y)]):
                send = functools.partial(pltpu.make_async_remote_copy, ins[a], outs[a].at[2 * x + y], send_sems.at[a, k],
                                         recv_sems.at[a, k], device_id=(px, py, c), device_id_type=MESH)
                got = outs[a].at[2 * px + py]
                arrived = functools.partial(pltpu.make_async_remote_copy, got, got, send_sems.at[a, k], recv_sems.at[a, k],
                                            device_id=(px, py, c), device_id_type=MESH)
                yield send, arrived

    def start(ins, outs, sems):
        for send, _ in copies(ins, outs, sems):
            send().start()

    def finish(ins, outs, sems):
        both = list(copies(ins, outs, sems))
        for _, arrived in both:
            arrived().wait_recv()
        for send, _ in both:
            send().wait_send()

    return dict(ins=list(shards), prev=[], out_shape=[SDS((N_CHIP,) + s.shape, s.dtype) for s in shards],
                scratch=[pltpu.SemaphoreType.DMA((na, 3)), pltpu.SemaphoreType.DMA((na, 3))], start=start, finish=finish)


def _scatter_rider(bufs, layer, prev):
    na = len(bufs)
    h = layer // (DEPTH // 2)

    def copies(ins, outs, sems):
        send_sems, recv_sems, local_sems = sems
        x, y, c = _place()
        me = 4 * x + 2 * y + c
        for a in range(na):
            for j in range(N_CHIP):
                is_self = ((2 * x + y) == j) & (c == h)
                local = functools.partial(pltpu.make_async_copy, ins[a].at[j], outs[a].at[me], local_sems.at[a])
                remote = functools.partial(pltpu.make_async_remote_copy, ins[a].at[j], outs[a].at[me], send_sems.at[a, j],
                                           recv_sems.at[a, me], device_id=(j // 2, j % 2, h), device_id_type=MESH)
                yield is_self, local, remote

    def start(ins, outs, sems):
        for is_self, local, remote in copies(ins, outs, sems):
            pl.when(is_self)(lambda: local().start())
            pl.when(jnp.logical_not(is_self))(lambda: remote().start())

    def finish(ins, outs, sems):
        _, recv_sems, _ = sems
        x, y, c = _place()
        me = 4 * x + 2 * y + c
        for a in range(na):
            for s in range(N_DEV):
                got = outs[a].at[s]
                arrived = functools.partial(pltpu.make_async_remote_copy, got, got, recv_sems.at[a, s], recv_sems.at[a, s],
                                            device_id=(s // 4, (s // 2) % 2, s % 2), device_id_type=MESH)
                pl.when((c == h) & (me != s))(lambda: arrived().wait_recv())
        for is_self, local, remote in copies(ins, outs, sems):
            pl.when(is_self)(lambda: local().wait())
            pl.when(jnp.logical_not(is_self))(lambda: remote().wait_send())

    return dict(ins=list(bufs), prev=list(prev), out_shape=[SDS((N_DEV,) + b.shape[1:], b.dtype) for b in bufs],
                scratch=[pltpu.SemaphoreType.DMA((na, N_CHIP)), pltpu.SemaphoreType.DMA((na, N_DEV)), pltpu.SemaphoreType.DMA((na,))],
                start=start, finish=finish)


def _sum_sources(parts, *, name):
    _, R, C = parts.shape
    tr = _div_tile(R, 256, 16)

    def body(p_ref, o_ref):
        acc = p_ref[0].astype(F32)
        for s in range(1, N_DEV):
            acc = acc + p_ref[s].astype(F32)
        o_ref[...] = acc

    return _pc(body, name=name, grid=(R // tr,), in_specs=[BS((N_DEV, tr, C), lambda i: (0, i, 0))],
               out_specs=BS((tr, C), lambda i: (i, 0)), out_shape=SDS((R, C), F32))(parts)


def _join_halves(halves, *, name):
    na = len(halves)

    def body(*refs):
        ins, outs = refs[:na], refs[na:2 * na]
        send_sems, recv_sems = refs[2 * na:]
        x, y, c = _place()
        cps = []
        for a in range(na):
            cp = pltpu.make_async_remote_copy(ins[a], outs[a], send_sems.at[a], recv_sems.at[a],
                                              device_id=(x, y, 1 - c), device_id_type=MESH)
            cp.start()
            cps.append(cp)
        for cp in cps:
            cp.wait_recv()
        for cp in cps:
            cp.wait_send()

    any_spec = BS(memory_space=pl.ANY)
    return pl.pallas_call(
        body, name=name, in_specs=[any_spec] * na, out_specs=[any_spec] * na,
        out_shape=[SDS(h.shape, h.dtype) for h in halves],
        scratch_shapes=[pltpu.SemaphoreType.DMA((na,)), pltpu.SemaphoreType.DMA((na,))],
        compiler_params=pltpu.CompilerParams(has_side_effects=True))(*halves)


def _allreduce_small(vec, *, name):
    R = vec.shape[0]

    def body(v_ref, o_ref, all_ref, send_sems, recv_sems):
        x, y, c = _place()
        me = 4 * x + 2 * y + c
        all_ref[me] = v_ref[...]
        sends = []
        for r in range(1, N_DEV):
            tgt = (x ^ (r >> 2), y ^ ((r >> 1) & 1), c ^ (r & 1))
            cp = pltpu.make_async_remote_copy(v_ref, all_ref.at[me], send_sems.at[r - 1], recv_sems.at[r - 1],
                                              device_id=tgt, device_id_type=MESH)
            cp.start()
            sends.append(cp)
        for r in range(1, N_DEV):
            tx, ty, tc = x ^ (r >> 2), y ^ ((r >> 1) & 1), c ^ (r & 1)
            got = all_ref.at[4 * tx + 2 * ty + tc]
            pltpu.make_async_remote_copy(got, got, send_sems.at[r - 1], recv_sems.at[r - 1],
                                         device_id=(tx, ty, tc), device_id_type=MESH).wait_recv()
        for cp in sends:
            cp.wait_send()
        acc = all_ref[0]
        for s in range(1, N_DEV):
            acc = acc + all_ref[s]
        o_ref[...] = acc

    vm = BS(memory_space=pltpu.VMEM)
    return pl.pallas_call(
        body, name=name, in_specs=[vm], out_specs=vm, out_shape=SDS((R, LANE), F32),
        scratch_shapes=[pltpu.VMEM((N_DEV, R, LANE), F32), pltpu.SemaphoreType.DMA((N_DEV - 1,)), pltpu.SemaphoreType.DMA((N_DEV - 1,))],
        compiler_params=pltpu.CompilerParams(has_side_effects=True, vmem_limit_bytes=VMEM_LIMIT_BYTES))(vec)


def _pack(arrs):
    rows = []
    for a in arrs:
        f = a.reshape(-1).astype(F32)
        n = -(-f.shape[0] // LANE) * LANE
        rows.append(jnp.pad(f, (0, n - f.shape[0])).reshape(-1, LANE))
    v = jnp.concatenate(rows, axis=0)
    pad = -v.shape[0] % 8
    return jnp.pad(v, ((0, pad), (0, 0)))


def _unpack(v, shapes):
    out, r = [], 0
    for s in shapes:
        n = int(np.prod(s)) if len(s) else 1
        nr = -(-n // LANE)
        out.append(v[r:r + nr].reshape(-1)[:n].reshape(s))
        r += nr
    return out


def _perm_in_cols(w_full):
    z, xbc, dt, q, k, v = (w_full[..., :Z_END], w_full[..., Z_END:XBC_END], w_full[..., XBC_END:DT_END],
                           w_full[..., DT_END:Q_END], w_full[..., Q_END:K_END], w_full[..., K_END:])
    pad = jnp.zeros(dt.shape[:-1] + (LANE - dt.shape[-1],), dt.dtype)
    return jnp.concatenate([q, z, xbc, k, v, dt, pad], axis=-1)


def _unperm_in_cols(g):
    q, z, xbc, k, v, dt = (g[..., :1024], g[..., 1024:2048], g[..., 2048:3584], g[..., 3584:3840], g[..., 3840:4096],
                           g[..., 4096:4096 + 2 * SSM_HEADS])
    return jnp.concatenate([z, xbc, dt, q, k, v], axis=-1)


def _dt_cols(a):
    return jnp.pad(a.reshape(1, 2 * SSM_HEADS), ((0, 0), (0, LANE - 2 * SSM_HEADS)))


def _dt_fwd(proj, dtb, *, name):
    B, L, _ = proj.shape

    def body(p_ref, b_ref, o_ref):
        o_ref[0] = _softplus(p_ref[0] + b_ref[...])

    return _pc(body, name=name, grid=(B,),
               in_specs=[BS((1, L, LANE), lambda b: (b, 0, P_COLS // LANE - 1)), BS((1, LANE), lambda b: (0, 0))],
               out_specs=BS((1, L, LANE), lambda b: (b, 0, 0)), out_shape=SDS((B, L, LANE), F32))(proj, dtb)


def _dt_bwd(proj, dtb, ddt, *, name):
    B, L, _ = proj.shape

    def body(p_ref, b_ref, g_ref, o_ref, db_ref):
        g = g_ref[0] * _sigmoid(p_ref[0] + b_ref[...])
        o_ref[0] = g.astype(BF16)
        db_ref[0] = jnp.sum(g, axis=0, keepdims=True)

    row = BS((1, L, LANE), lambda b: (b, 0, 0))
    return _pc(body, name=name, grid=(B,),
               in_specs=[BS((1, L, LANE), lambda b: (b, 0, P_COLS // LANE - 1)), BS((1, LANE), lambda b: (0, 0)), row],
               out_specs=[row, BS((1, 1, LANE), lambda b: (b, 0, 0))],
               out_shape=[SDS((B, L, LANE), BF16), SDS((B, 1, LANE), F32)])(proj, dtb, ddt)


def _layer_fwd(i, x, wts, small, band_bias, riders=None, arrived=None):
    riders = riders or {}
    B, L, D = x.shape
    T = B * L
    x2 = x.reshape(T, D)
    h, proj2 = _norm_mm(x2, small["norm1_w"][i], wts["w_in"], name=f"in_proj_{i}", tn=1408)
    proj = proj2.reshape(B, L, P_COLS)
    xbc_act = _conv_fwd(proj, 2048 // 256, small["conv_w"][i], small["conv_b"][i], taps=SSM_CONV, ct=256,
                        out_dtype=F32, name=f"ssm_conv_{i}")
    dtb, alog = _dt_cols(small["dt_bias"][i]), _dt_cols(small["a_log"][i])
    dskip = jnp.repeat(small["d_skip"][i], HEAD_DIM).reshape(1, SSM_WIDTH)
    dtc = _dt_fwd(proj, dtb, name=f"dt_{i}")
    ypre, hs, carried = _ssd_fwd(xbc_act, dtc, alog, dskip, name=f"ssd_{i}", rider=riders.get("ssd"))
    if carried is not None:
        arrived("ssd", carried)
    mixed = _gate_fwd(ypre.reshape(T, SSM_WIDTH), proj2, small["ssm_norm_w"][i], name=f"gate_{i}")
    sinkcol = jnp.repeat(small["attn_sink"][i], BLOCK).reshape(ATTN_HEADS * BLOCK, 1)
    mixed, carried = _attn_fwd(proj, band_bias, sinkcol, mixed.reshape(B, L, 2 * D), name=f"attn_{i}", rider=riders.get("attn"))
    if carried is not None:
        arrived("attn", carried)
    mixed = mixed.reshape(T, 2 * D)
    x_mid = _mm(mixed, wts["w_out"], add=x2, name=f"out_proj_{i}")
    h2, gu2 = _norm_mm(x_mid, small["norm2_w"][i], wts["w_up"], name=f"up_proj_{i}", out_dtype=BF16, tn=1408)
    gu = gu2.reshape(B, L, 2 * D_FF)
    act = _conv_fwd(gu, 0, small["ffn_conv_w"][i], small["ffn_conv_b"][i], taps=FFN_CONV, ct=256, gate_blk0=D_FF // 256,
                    out_dtype=BF16, name=f"ffn_conv_{i}")
    x_out = _mm(act.reshape(T, D_FF), wts["w_down"], add=x_mid, name=f"down_proj_{i}", tk=1408)
    saved = dict(x2=x2, h=h, proj2=proj2, xbc_act=xbc_act, dtb=dtb, dtc=dtc, alog=alog, dskip=dskip, ypre=ypre, hs=hs, mixed=mixed,
                 sinkcol=sinkcol, x_mid=x_mid, h2=h2, gu=gu, act=act)
    return x_out.reshape(B, L, D), saved


def _layer_bwd(i, dx_out, dxb, sv, wts, small, band_bias, attn_rider=None, ssd_rider=None, arrived=None):
    T, D = dx_out.shape
    B, L = sv["gu"].shape[:2]
    g = {}
    dact = _mm(dxb, wts["w_down"], tb=True, out_dtype=BF16, name=f"d_act_{i}", tn=1408)
    g["w_down"] = _mm(sv["act"].reshape(T, D_FF), dxb, ta=True, name=f"dw_down_{i}", tm=1408)
    dg, du, dcw, dcb = _conv_bwd(sv["gu"], 0, small["ffn_conv_w"][i], small["ffn_conv_b"][i], dact.reshape(B, L, D_FF),
                                 taps=FFN_CONV, ct=256, gate_blk0=D_FF // 256, name=f"d_ffn_conv_{i}")
    g["ffn_conv_w"] = jnp.sum(dcw, axis=0)[:FFN_CONV]
    g["ffn_conv_b"] = jnp.sum(dcb, axis=(0, 1))
    dgu = jnp.concatenate([dg, du], axis=-1).reshape(T, 2 * D_FF)
    dx_mid, dmb, dw2 = _mm_dnorm(dgu, wts["w_up"], sv["x_mid"], small["norm2_w"][i], dx_out, name=f"d_h2_{i}", tk=1408)
    g["w_up"] = _mm(sv["h2"], dgu, ta=True, name=f"dw_up_{i}", tn=1408)
    g["norm2_w"] = dw2[0]
    dmixed = _mm(dmb, wts["w_out"], tb=True, name=f"d_mixed_{i}")
    g["w_out"] = _mm(sv["mixed"], dmb, ta=True, name=f"dw_out_{i}")
    dypre, dz, dwn = _gate_bwd(sv["ypre"].reshape(T, SSM_WIDTH), sv["proj2"], small["ssm_norm_w"][i], dmixed, name=f"d_gate_{i}")
    g["ssm_norm_w"] = jnp.sum(dwn, axis=(0, 1))
    proj = sv["proj2"].reshape(B, L, P_COLS)
    dxs, dbm, dcm, ddt, dalog, ddsk, carried = _ssd_bwd(sv["xbc_act"], sv["dtc"], sv["alog"], sv["dskip"], sv["hs"],
                                                        dypre.reshape(B, L, SSM_WIDTH), name=f"d_ssd_{i}",
                                                        rider=ssd_rider(g) if ssd_rider is not None else None)
    if carried is not None:
        arrived("ssd", carried)
    ddt, ddtb = _dt_bwd(proj, sv["dtb"], ddt, name=f"d_dt_{i}")
    g["dt_bias"] = jnp.sum(ddtb, axis=(0, 1))[:2 * SSM_HEADS].reshape(2, SSM_HEADS)
    g["a_log"] = jnp.sum(dalog, axis=(0, 1, 2))[:2 * SSM_HEADS].reshape(2, SSM_HEADS)
    g["d_skip"] = jnp.sum(ddsk.reshape(B, SSM_HEADS, HEAD_DIM), axis=(0, 2))
    dxbc_act = jnp.concatenate([dxs, dbm, dcm], axis=-1)
    dxbc, dcw, dcb = _conv_bwd(proj, 2048 // 256, small["conv_w"][i], small["conv_b"][i], dxbc_act, taps=SSM_CONV, ct=256,
                               name=f"d_ssm_conv_{i}")
    g["conv_w"] = jnp.sum(dcw, axis=0)[:SSM_CONV]
    g["conv_b"] = jnp.sum(dcb, axis=(0, 1))
    dq, dk, dv, dbias, dsink, carried = _attn_bwd(proj, band_bias, sv["sinkcol"], dmixed.reshape(B, L, 2 * D), name=f"d_attn_{i}",
                                                  rider=attn_rider)
    if carried is not None:
        arrived("attn", carried)
    g["attn_sink"] = jnp.sum(dsink.reshape(ATTN_HEADS, BLOCK), axis=1)
    dproj = jnp.concatenate([dq, dz.reshape(B, L, SSM_WIDTH), dxbc, dk.astype(BF16), dv.astype(BF16), ddt], axis=-1).reshape(T, P_COLS)
    dx_in, dx_in_b, dw1 = _mm_dnorm(dproj, wts["w_in"], sv["x2"], small["norm1_w"][i], dx_mid, name=f"d_h_{i}", tk=1408)
    g["w_in"] = _unperm_in_cols(_mm(sv["h"], dproj, ta=True, name=f"dw_in_{i}", tn=1408))
    g["norm1_w"] = dw1[0]
    return dx_in, dx_in_b, g, dbias


_BIG = ("w_in", "w_out", "w_up", "w_down")
_BIG_AXIS = {"w_in": 2, "w_out": 1, "w_up": 2, "w_down": 1}
_SMALL = ("rel_bias", "norm1_w", "conv_w", "conv_b", "dt_bias", "a_log", "d_skip", "ssm_norm_w", "attn_sink", "norm2_w",
          "ffn_conv_w", "ffn_conv_b", "final_norm_w")
_SMALL_SHARDED = ("conv_w", "ffn_conv_w")
_ORDER = ("rel_bias", "norm1_w", "w_in", "conv_w", "conv_b", "dt_bias", "a_log", "d_skip", "ssm_norm_w", "attn_sink", "w_out",
          "norm2_w", "w_up", "ffn_conv_w", "ffn_conv_b", "w_down", "final_norm_w")


def _local_step(x, target, small, wts=None, exchange=None):
    B, L, D = x.shape
    bucket = _bucket_table()
    band_bias = _bias_expand(small["rel_bias"], bucket, name="band_bias")

    def fetch(spec):
        return exchange["gather"](spec) if exchange is not None and spec else None

    def fetched(spec, carried):
        for (i, k), full in zip(spec, exchange["weights"](spec, carried)):
            wts[i][k] = full

    if exchange is not None:
        wts = [{} for _ in range(DEPTH)]
        fetched([(0, "w_in")], _run_rider(fetch([(0, "w_in")]), name="gather_w_in_0"))
    saved = []
    for i in range(DEPTH):
        nxt = i + 1 < DEPTH
        if i == 0:
            plan = {"ssd": [(0, "w_out"), (0, "w_up"), (0, "w_down")], "attn": [(1, "w_in"), (1, "w_out"), (1, "w_up")] if nxt else []}
        else:
            plan = {"ssd": [(i, "w_down")] + ([(i + 1, "w_in"), (i + 1, "w_out")] if nxt else []), "attn": [(i + 1, "w_up")] if nxt else []}
        x, sv = _layer_fwd(i, x, wts[i], small, band_bias, {c: fetch(s) for c, s in plan.items()}, lambda c, r: fetched(plan[c], r))
        saved.append(sv)
    loss, dx, dxb, dwf = _loss_head(x.reshape(B * L, D), small["final_norm_w"], target.reshape(B * L, D), name="loss_head")
    per_layer = []
    dbias = jnp.zeros((ATTN_HEADS, BLOCK, KEY_SPAN), F32)
    late = None
    for i in reversed(range(DEPTH)):
        own = [(i, "w_down"), (i, "w_up"), (i, "w_out")]
        plan = {"ssd": own, "attn": late[0] if late else []}
        attn_rider = exchange["scatter"](*late) if late else None
        ssd_rider = (lambda g: exchange["scatter"](own, [g[k] for _, k in own])) if exchange is not None else None
        dx, dxb, g, dbias_i = _layer_bwd(i, dx, dxb, saved[i], wts[i], small, band_bias, attn_rider, ssd_rider,
                                    lambda c, r: exchange["collect"](plan[c], r))
        if exchange is not None:
            late = ([(i, "w_in")], [g["w_in"]])
            for k in _BIG:
                g.pop(k)
        dbias = dbias + dbias_i
        per_layer.append(g)
    if exchange is not None:
        exchange["collect"](late[0], _run_rider(exchange["scatter"](*late), name="scatter_dw_in_0"))
    per_layer.reverse()
    grads = {k: jnp.stack([g[k] for g in per_layer]) for k in per_layer[0]}
    grads["rel_bias"] = _bias_reduce(dbias, bucket, name="d_rel_bias")
    grads["final_norm_w"] = dwf[0]
    return loss, dx.reshape(B, L, D), grads


def _split_by_chip(g, axis):
    shp = g.shape
    n = shp[axis] // N_CHIP
    g = g.reshape(shp[:axis] + (N_CHIP, n) + shp[axis + 1:])
    return jnp.moveaxis(g, axis, 0)


def _join_chips(a, axis):
    a = jnp.moveaxis(a, 0, axis)
    shp = a.shape
    return a.reshape(shp[:axis] + (shp[axis] * shp[axis + 1],) + shp[axis + 2:])


def kernel(x, rel_bias, norm1_w, w_in, conv_w, conv_b, dt_bias, a_log, d_skip, ssm_norm_w, attn_sink, w_out, norm2_w, w_up, ffn_conv_w, ffn_conv_b, w_down, final_norm_w, loss_target, m_rel_bias, m_norm1_w, m_w_in, m_conv_w, m_conv_b, m_dt_bias, m_a_log, m_d_skip, m_ssm_norm_w, m_attn_sink, m_w_out, m_norm2_w, m_w_up, m_ffn_conv_w, m_ffn_conv_b, m_w_down, m_final_norm_w, v_rel_bias, v_norm1_w, v_w_in, v_conv_w, v_conv_b, v_dt_bias, v_a_log, v_d_skip, v_ssm_norm_w, v_attn_sink, v_w_out, v_norm2_w, v_w_up, v_ffn_conv_w, v_ffn_conv_b, v_w_down, v_final_norm_w):
    w = dict(rel_bias=rel_bias, norm1_w=norm1_w, w_in=w_in, conv_w=conv_w, conv_b=conv_b, dt_bias=dt_bias, a_log=a_log,
             d_skip=d_skip, ssm_norm_w=ssm_norm_w, attn_sink=attn_sink, w_out=w_out, norm2_w=norm2_w, w_up=w_up,
             ffn_conv_w=ffn_conv_w, ffn_conv_b=ffn_conv_b, w_down=w_down, final_norm_w=final_norm_w)
    m = dict(rel_bias=m_rel_bias, norm1_w=m_norm1_w, w_in=m_w_in, conv_w=m_conv_w, conv_b=m_conv_b, dt_bias=m_dt_bias,
             a_log=m_a_log, d_skip=m_d_skip, ssm_norm_w=m_ssm_norm_w, attn_sink=m_attn_sink, w_out=m_w_out, norm2_w=m_norm2_w,
             w_up=m_w_up, ffn_conv_w=m_ffn_conv_w, ffn_conv_b=m_ffn_conv_b, w_down=m_w_down, final_norm_w=m_final_norm_w)
    v = dict(rel_bias=v_rel_bias, norm1_w=v_norm1_w, w_in=v_w_in, conv_w=v_conv_w, conv_b=v_conv_b, dt_bias=v_dt_bias,
             a_log=v_a_log, d_skip=v_d_skip, ssm_norm_w=v_ssm_norm_w, attn_sink=v_attn_sink, w_out=v_w_out, norm2_w=v_norm2_w,
             w_up=v_w_up, ffn_conv_w=v_ffn_conv_w, ffn_conv_b=v_ffn_conv_b, w_down=v_w_down, final_norm_w=v_final_norm_w)
    my_chip = 2 * lax.axis_index("x") + lax.axis_index("y")

    shards = {k: w[k].astype(BF16) for k in _BIG}
    received = {}

    def gather(spec):
        return _gather_rider([shards[k][i] for i, k in spec])

    def weights(spec, carried):
        out = []
        for (i, k), g_ in zip(spec, carried):
            full = _join_chips(lax.dynamic_update_index_in_dim(g_, shards[k][i], my_chip, 0), _BIG_AXIS[k] - 1)
            out.append(_perm_in_cols(full) if k == "w_in" else full)
        return out

    def scatter(spec, grads):
        layer = spec[0][0]
        bufs = [_split_by_chip(g_, _BIG_AXIS[k] - 1).astype(BF16) for (_, k), g_ in zip(spec, grads)]
        prev = [received[(layer % 2, k)] for _, k in spec] if layer + 2 < DEPTH else []
        return _scatter_rider(bufs, layer, prev)

    def collect(spec, carried):
        for (i, k), pieces in zip(spec, carried):
            received[(i % 2, k)] = pieces

    conv_shapes = [(DEPTH, SSM_CONV, CONV_CH), (DEPTH, FFN_CONV, D_FF)]
    placed = [lax.dynamic_update_slice_in_dim(jnp.zeros(s, F32), w[k], my_chip * w[k].shape[2], axis=2)
              for k, s in zip(_SMALL_SHARDED, conv_shapes)]
    lead = (lax.axis_index("c") == 0).astype(F32)
    conv_full = _unpack(_allreduce_small(_pack([p * lead for p in placed]), name="gather_conv_weights"), conv_shapes)
    small = {k: w[k] for k in _SMALL}
    small["conv_w"], small["ffn_conv_w"] = conv_full

    loss_part, grad_x, gp = _local_step(x, loss_target, small,
                                        exchange=dict(gather=gather, weights=weights, scatter=scatter, collect=collect))

    small_shapes = [small[k].shape for k in _SMALL] + [()]
    red = _unpack(_allreduce_small(_pack([gp[k] for k in _SMALL] + [loss_part[0, :1]]), name="reduce_small"), small_shapes)
    gsmall = dict(zip(_SMALL, red[:-1]))
    loss = red[-1]
    for k in _SMALL_SHARDED:
        n = w[k].shape[2]
        gsmall[k] = lax.dynamic_slice_in_dim(gsmall[k], my_chip * n, n, axis=2)

    core = lax.axis_index("c")
    half = DEPTH // 2
    halves = [jnp.stack([_sum_sources(received[(p, k)], name=f"sum_{k}_{p}") for p in range(half)]) for k in _BIG]
    others = _join_halves(halves, name="join_halves")
    gbig = {}
    for k, mine_, theirs_ in zip(_BIG, halves, others):
        full = jnp.zeros((DEPTH,) + mine_.shape[1:], F32)
        full = lax.dynamic_update_slice_in_dim(full, mine_, core * half, axis=0)
        gbig[k] = lax.dynamic_update_slice_in_dim(full, theirs_, (1 - core) * half, axis=0)

    grad, delta, new_m, new_v = {}, {}, {}, {}
    for k in _BIG:
        shp = w[k].shape
        two = lambda a: a.reshape(shp[0] * shp[1], shp[2])
        d_, m_, v_ = _adamw(two(w[k]), two(gbig[k]), two(m[k]), two(v[k]), name=f"adamw_{k}")
        grad[k], delta[k], new_m[k], new_v[k] = gbig[k], d_.reshape(shp), m_.reshape(shp), v_.reshape(shp)
    shapes = [w[k].shape for k in _SMALL]
    d_, m_, v_ = _adamw(_pack([w[k] for k in _SMALL]), _pack([gsmall[k] for k in _SMALL]), _pack([m[k] for k in _SMALL]),
                        _pack([v[k] for k in _SMALL]), name="adamw_small")
    for k, a, b_, c_ in zip(_SMALL, _unpack(d_, shapes), _unpack(m_, shapes), _unpack(v_, shapes)):
        grad[k], delta[k], new_m[k], new_v[k] = gsmall[k], a, b_, c_
    return (loss, grad_x, *[grad[k] for k in _ORDER], *[delta[k] for k in _ORDER], *[new_m[k] for k in _ORDER],
            *[new_v[k] for k in _ORDER])
```

```python
import functools
import math

import jax
import jax.numpy as jnp
import numpy as np
from jax import lax
from jax.experimental import pallas as pl
from jax.experimental.pallas import tpu as pltpu

F32 = jnp.float32
BF16 = jnp.bfloat16
BS = pl.BlockSpec
SDS = jax.ShapeDtypeStruct
MESH = pl.DeviceIdType.MESH

D_MODEL = 1024
DEPTH = 4
SSM_HEADS = 16
SSM_WIDTH = 1024
BC_WIDTH = 256
CONV_CH = 1536
SSM_CONV = 7
CHUNK = 128
ATTN_HEADS = 16
KV_HEADS = 4
HEAD_DIM = 64
WINDOW = 128
BLOCK = 128
KEY_SPAN = 384
REL_BUCKETS = 32
REL_MAX_DIST = 128
D_FF = 2816
FFN_CONV = 3
NORM_EPS = 1e-6
Z_END = 1024
XBC_END = 2560
DT_END = 2592
Q_END = 3616
K_END = 3872
IN_COLS = 4128
P_COLS = 4224
ADAM_LR, ADAM_B1, ADAM_B2, ADAM_EPS, ADAM_WD, ADAM_STEP = 0.001, 0.9, 0.999, 1e-08, 0.01, 10
NEG = -1e30
N_DEV = 8
N_CHIP = 4
LANE = 128
VMEM_LIMIT_BYTES = 48 * 1024 * 1024


def _pc(body, *, name, grid, in_specs, out_specs, out_shape, scratch_shapes=(), aliases=None):
    return pl.pallas_call(
        body, name=name, grid=grid, in_specs=in_specs, out_specs=out_specs, out_shape=out_shape,
        scratch_shapes=list(scratch_shapes), input_output_aliases=aliases or {},
        compiler_params=pltpu.CompilerParams(dimension_semantics=("arbitrary",) * len(grid),
                                             vmem_limit_bytes=VMEM_LIMIT_BYTES))


def _split_rider_refs(refs, n_in, n_out, rider):
    n_rin = len(rider["ins"]) + len(rider["prev"])
    n_rout = len(rider["out_shape"])
    core = refs[:n_in] + refs[n_in + n_rin:n_in + n_rin + n_out] + refs[n_in + n_rin + n_out + n_rout + len(rider["scratch"]):]
    rins = refs[n_in:n_in + len(rider["ins"])]
    routs = refs[n_in + n_rin + n_out:n_in + n_rin + n_out + n_rout]
    sems = refs[n_in + n_rin + n_out + n_rout:n_in + n_rin + n_out + n_rout + len(rider["scratch"])]
    return core, rins, routs, sems


def _pc_carry(body, args, *, name, grid, in_specs, out_specs, out_shape, scratch_shapes=(), rider=None, aliases=None):
    if rider is None:
        return _pc(body, name=name, grid=grid, in_specs=in_specs, out_specs=out_specs, out_shape=out_shape,
                   scratch_shapes=scratch_shapes, aliases=aliases)(*args), None
    n_in, n_out = len(in_specs), len(out_shape)
    any_spec = BS(memory_space=pl.ANY)

    def full(*refs):
        core, rins, routs, sems = _split_rider_refs(refs, n_in, n_out, rider)
        ids = [pl.program_id(d) for d in range(len(grid))]
        first = functools.reduce(jnp.logical_and, [i == 0 for i in ids])
        last = functools.reduce(jnp.logical_and, [i == g - 1 for i, g in zip(ids, grid)])

        @pl.when(first)
        def _():
            rider["start"](rins, routs, sems)

        body(*core)

        @pl.when(last)
        def _():
            rider["finish"](rins, routs, sems)

    n_rin = len(rider["ins"])
    outs = pl.pallas_call(
        full, name=name, grid=grid,
        in_specs=list(in_specs) + [any_spec] * (n_rin + len(rider["prev"])),
        out_specs=list(out_specs) + [any_spec] * len(rider["out_shape"]),
        out_shape=list(out_shape) + list(rider["out_shape"]),
        scratch_shapes=list(rider["scratch"]) + list(scratch_shapes),
        input_output_aliases={**(aliases or {}), **{n_in + n_rin + t: n_out + t for t in range(len(rider["prev"]))}},
        compiler_params=pltpu.CompilerParams(dimension_semantics=("arbitrary",) * len(grid), vmem_limit_bytes=VMEM_LIMIT_BYTES,
                                             has_side_effects=True))(*args, *rider["ins"], *rider["prev"])
    return outs[:n_out], outs[n_out:]


def _run_rider(rider, *, name):
    any_spec = BS(memory_space=pl.ANY)
    n_rin = len(rider["ins"])

    def body(*refs):
        _, rins, routs, sems = _split_rider_refs(refs, 0, 0, rider)
        rider["start"](rins, routs, sems)
        rider["finish"](rins, routs, sems)

    return pl.pallas_call(
        body, name=name, in_specs=[any_spec] * (n_rin + len(rider["prev"])), out_specs=[any_spec] * len(rider["out_shape"]),
        out_shape=list(rider["out_shape"]), scratch_shapes=list(rider["scratch"]),
        input_output_aliases={n_rin + t: t for t in range(len(rider["prev"]))},
        compiler_params=pltpu.CompilerParams(has_side_effects=True))(*rider["ins"], *rider["prev"])


def _div_tile(n, pref, mult):
    t = min(pref, n)
    t -= t % mult
    while t >= mult:
        if n % t == 0:
            return t
        t -= mult
    return n


def _mm(a, b, *, name, ta=False, tb=False, add=None, out_dtype=F32, tm=1024, tn=1024, tk=1024):
    if ta:
        K, M = a.shape
    else:
        M, K = a.shape
    N = b.shape[0] if tb else b.shape[1]
    tm, tn, tk = _div_tile(M, tm, LANE), _div_tile(N, tn, LANE), _div_tile(K, tk, LANE)
    nk = K // tk
    dims = (((0,) if ta else (1,), (1,) if tb else (0,)), ((), ()))

    def body_single(*refs):
        r = lax.dot_general(refs[0][...], refs[1][...], dims, preferred_element_type=F32)
        if add is not None:
            r = r + refs[2][...]
        refs[-1][...] = r.astype(out_dtype)

    def body(*refs):
        if add is None:
            a_ref, b_ref, o_ref, acc_ref = refs
        else:
            a_ref, b_ref, add_ref, o_ref, acc_ref = refs
        k = pl.program_id(2)

        @pl.when(k == 0)
        def _():
            acc_ref[...] = jnp.zeros_like(acc_ref)

        acc_ref[...] += lax.dot_general(a_ref[...], b_ref[...], dims, preferred_element_type=F32)

        @pl.when(k == nk - 1)
        def _():
            r = acc_ref[...]
            if add is not None:
                r = r + add_ref[...]
            o_ref[...] = r.astype(out_dtype)

    a_spec = BS((tk, tm), lambda i, j, k: (k, i)) if ta else BS((tm, tk), lambda i, j, k: (i, k))
    b_spec = BS((tn, tk), lambda i, j, k: (j, k)) if tb else BS((tk, tn), lambda i, j, k: (k, j))
    in_specs, args = [a_spec, b_spec], [a, b]
    if add is not None:
        in_specs.append(BS((tm, tn), lambda i, j, k: (i, j)))
        args.append(add)
    return _pc(body_single if nk == 1 else body, name=name, grid=(M // tm, N // tn, nk), in_specs=in_specs,
               out_specs=BS((tm, tn), lambda i, j, k: (i, j)), out_shape=SDS((M, N), out_dtype),
               scratch_shapes=[] if nk == 1 else [pltpu.VMEM((tm, tn), F32)])(*args)


def _dot(a, b, dims):
    return lax.dot_general(a.astype(BF16), b.astype(BF16), (dims, ((), ())), preferred_element_type=F32)


@jax.custom_vjp
def _nn(a, b):
    return _dot(a, b, ((1,), (0,)))


@jax.custom_vjp
def _nt(a, b):
    return _dot(a, b, ((1,), (1,)))


@jax.custom_vjp
def _tn(a, b):
    return _dot(a, b, ((0,), (0,)))


_nn.defvjp(lambda a, b: (_nn(a, b), (a, b)), lambda r, g: (_nt(g, r[1]), _tn(r[0], g)))
_nt.defvjp(lambda a, b: (_nt(a, b), (a, b)), lambda r, g: (_nn(g, r[1]), _tn(g, r[0])))
_tn.defvjp(lambda a, b: (_tn(a, b), (a, b)), lambda r, g: (_nt(r[1], g), _nn(r[0], g)))


def _hdot(m, x):
    hi = x.astype(BF16)
    r1 = x - hi.astype(F32)
    lo = r1.astype(BF16)
    lo2 = (r1 - lo.astype(F32)).astype(BF16)
    n = x.shape[1]
    out = lax.dot_general(m.astype(BF16), jnp.concatenate([hi, lo, lo2], axis=1), (((1,), (0,)), ((), ())),
                          preferred_element_type=F32)
    return out[:, :n] + out[:, n:2 * n] + out[:, 2 * n:]


@jax.custom_vjp
def _cumdot(m, mt, x):
    return _hdot(m, x)


_cumdot.defvjp(lambda m, mt, x: (_hdot(m, x), (m, mt)),
               lambda r, g: (jnp.zeros_like(r[0]), jnp.zeros_like(r[1]), _hdot(r[1], g)))


def _sigmoid(x):
    return 1.0 / (1.0 + jnp.exp(-x))


def _softplus(x):
    return jnp.maximum(x, 0.0) + jnp.log(1.0 + jnp.exp(-jnp.abs(x)))


def _rms(x, w):
    return x * lax.rsqrt(jnp.mean(x * x, axis=-1, keepdims=True) + NORM_EPS) * w


def _norm_mm(x2, nw, b, *, name, out_dtype=F32, tm=1024, tn=1024):
    T, D = x2.shape
    N = b.shape[1]
    tm, tn = _div_tile(T, tm, LANE), _div_tile(N, tn, LANE)

    def body(x_ref, w_ref, b_ref, h_ref, o_ref):
        @pl.when(pl.program_id(1) == 0)
        def _():
            h_ref[...] = _rms(x_ref[...], w_ref[...]).astype(BF16)

        o_ref[...] = lax.dot_general(h_ref[...], b_ref[...], (((1,), (0,)), ((), ())), preferred_element_type=F32).astype(out_dtype)

    return _pc(body, name=name, grid=(T // tm, N // tn),
               in_specs=[BS((tm, D), lambda i, j: (i, 0)), BS((1, D), lambda i, j: (0, 0)), BS((D, tn), lambda i, j: (0, j))],
               out_specs=[BS((tm, D), lambda i, j: (i, 0)), BS((tm, tn), lambda i, j: (i, j))],
               out_shape=[SDS((T, D), BF16), SDS((T, N), out_dtype)])(x2, nw.reshape(1, D), b)


def _dnorm(dh, x2, nw, resid, *, name):
    T, D = x2.shape
    tr = _div_tile(T, 512, 16)

    def body(x_ref, w_ref, dh_ref, r_ref, dx_ref, dxb_ref, dw_ref):
        _, vjp = jax.vjp(_rms, x_ref[...], w_ref[...])
        dx, dw = vjp(dh_ref[...])
        dx = dx + r_ref[...]
        dx_ref[...] = dx
        dxb_ref[...] = dx.astype(BF16)

        @pl.when(pl.program_id(0) == 0)
        def _():
            dw_ref[...] = jnp.zeros_like(dw_ref)

        dw_ref[...] += dw

    row = BS((tr, D), lambda i: (i, 0))
    one = BS((1, D), lambda i: (0, 0))
    return _pc(body, name=name, grid=(T // tr,), in_specs=[row, one, row, row], out_specs=[row, row, one],
               out_shape=[SDS((T, D), F32), SDS((T, D), BF16), SDS((1, D), F32)])(x2, nw.reshape(1, D), dh, resid)


ROW_PAD = 8


def _pad_rows(x):
    return jnp.concatenate([x, jnp.zeros((ROW_PAD, x.shape[1]), x.dtype)], axis=0)


def _shift_rows(xp, s):
    n = xp.shape[0] - ROW_PAD
    return xp[:n] if s == 0 else pltpu.roll(xp, (-s) % xp.shape[0], 0)[:n]


def _conv_taps(x, taps):
    xp = _pad_rows(x)
    return [_shift_rows(xp, k - taps // 2) for k in range(taps)]


def _conv_pre(xs, w_ref, b_ref):
    c = b_ref[...] + w_ref[0:1, :] * xs[0]
    for k in range(1, len(xs)):
        c = c + w_ref[k:k + 1, :] * xs[k]
    return c


def _conv_fwd(x3, x_blk0, w, b, *, taps, ct, gate_blk0=None, out_dtype, name):
    B, L, _ = x3.shape
    C = w.shape[1]
    wp = jnp.zeros((8, C), F32).at[:taps].set(w)

    def body(*refs):
        if gate_blk0 is None:
            x_ref, w_ref, b_ref, o_ref = refs
        else:
            x_ref, u_ref, w_ref, b_ref, o_ref = refs
        c = _conv_pre(_conv_taps(x_ref[0].astype(F32), taps), w_ref, b_ref)
        y = c * _sigmoid(c)
        if gate_blk0 is not None:
            y = y * u_ref[0].astype(F32)
        o_ref[0] = y.astype(out_dtype)

    in_specs = [BS((1, L, ct), lambda bi, j: (bi, 0, x_blk0 + j))]
    args = [x3]
    if gate_blk0 is not None:
        in_specs.append(BS((1, L, ct), lambda bi, j: (bi, 0, gate_blk0 + j)))
        args.append(x3)
    in_specs += [BS((8, ct), lambda bi, j: (0, j)), BS((1, ct), lambda bi, j: (0, j))]
    args += [wp, b.reshape(1, C)]
    return _pc(body, name=name, grid=(B, C // ct), in_specs=in_specs,
               out_specs=BS((1, L, ct), lambda bi, j: (bi, 0, j)), out_shape=SDS((B, L, C), out_dtype))(*args)


def _conv_bwd(x3, x_blk0, w, b, dy3, *, taps, ct, gate_blk0=None, name):
    B, L, _ = x3.shape
    C = w.shape[1]
    wp = jnp.zeros((8, C), F32).at[:taps].set(w)
    gated = gate_blk0 is not None

    def body(*refs):
        if gated:
            x_ref, u_ref, w_ref, b_ref, dy_ref, dx_ref, du_ref, dw_ref, db_ref = refs
        else:
            x_ref, w_ref, b_ref, dy_ref, dx_ref, dw_ref, db_ref = refs
        xs = _conv_taps(x_ref[0].astype(F32), taps)
        dy = dy_ref[0].astype(F32)
        c = _conv_pre(xs, w_ref, b_ref)
        sg = _sigmoid(c)
        dsilu = sg * (1.0 + c * (1.0 - sg))
        if gated:
            du_ref[0] = (dy * (c * sg)).astype(BF16)
            dc = dy * u_ref[0].astype(F32) * dsilu
        else:
            dc = dy * dsilu
        dcp = _pad_rows(dc)
        dx = jnp.zeros_like(dc)
        dw_ref[0] = jnp.zeros((8, ct), F32)
        for k in range(taps):
            dx = dx + w_ref[k:k + 1, :] * _shift_rows(dcp, taps // 2 - k)
            dw_ref[0, k:k + 1, :] = jnp.sum(dc * xs[k], axis=0, keepdims=True)
        dx_ref[0] = dx.astype(BF16)
        db_ref[0] = jnp.sum(dc, axis=0, keepdims=True)

    xs = BS((1, L, ct), lambda bi, j: (bi, 0, x_blk0 + j))
    ys = BS((1, L, ct), lambda bi, j: (bi, 0, j))
    in_specs, args = [xs], [x3]
    if gated:
        in_specs.append(BS((1, L, ct), lambda bi, j: (bi, 0, gate_blk0 + j)))
        args.append(x3)
    in_specs += [BS((8, ct), lambda bi, j: (0, j)), BS((1, ct), lambda bi, j: (0, j)), ys]
    args += [wp, b.reshape(1, C), dy3]
    out_specs = [ys] + ([ys] if gated else []) + [BS((1, 8, ct), lambda bi, j: (bi, 0, j)), BS((1, 1, ct), lambda bi, j: (bi, 0, j))]
    out_shape = [SDS((B, L, C), BF16)] + ([SDS((B, L, C), BF16)] if gated else []) + [SDS((B, 8, C), F32), SDS((B, 1, C), F32)]
    return _pc(body, name=name, grid=(B, C // ct), in_specs=in_specs, out_specs=out_specs, out_shape=out_shape)(*args)


def _tri(reverse):
    r = lax.broadcasted_iota(jnp.int32, (CHUNK, CHUNK), 0)
    c = lax.broadcasted_iota(jnp.int32, (CHUNK, CHUNK), 1)
    return (c >= r) if reverse else (c <= r)


PAIRS = 2
QUADS = SSM_HEADS // (2 * PAIRS)
QW = PAIRS * LANE


def _ssd_chunk(h0, h1, x0, x1, bm, cm, dtc, alog, *, col0, reverse):
    mask = _tri(reverse)
    eye = lax.broadcasted_iota(jnp.int32, (CHUNK, CHUNK), 0) == lax.broadcasted_iota(jnp.int32, (CHUNK, CHUNK), 1)
    lane = lax.broadcasted_iota(jnp.int32, (1, LANE), 1)
    first = lane < HEAD_DIM
    adt = dtc * (-jnp.exp(alog))
    cumc = _cumdot(mask.astype(F32), _tri(not reverse).astype(F32), adt)
    totc = jnp.sum(adt, axis=0, keepdims=True)
    cb = _nt(cm, bm)

    def col(v, c):
        return jnp.sum(jnp.where(lane == c, v, 0.0), axis=1, keepdims=True)

    outs, states = [], []
    for p, (hprev, xs) in enumerate(((h0, x0), (h1, x1))):
        c0 = col0 + 2 * p
        cj = (col(cumc, c0), col(cumc, c0 + 1))
        cum = jnp.where(first, cj[0], cj[1])
        tot = jnp.where(first, col(totc, c0), col(totc, c0 + 1))
        xdt = xs * jnp.where(first, col(dtc, c0), col(dtc, c0 + 1))
        y = _nn(cm, hprev) * jnp.exp(cum)
        for j in range(2):
            rj = jnp.sum(jnp.where(eye, cj[j], 0.0), axis=0, keepdims=True)
            dec = jnp.exp(jnp.where(mask, cj[j] - rj, NEG))
            y = y + _nn(cb * dec, jnp.where(first if j == 0 else ~first, xdt, 0.0))
        outs.append(y)
        states.append(hprev * jnp.exp(tot) + _tn(bm, xdt * jnp.exp(tot - cum)))
    return outs[0], outs[1], states[0], states[1]


def _ssd_specs(B, L):
    def lanes(w, blk):
        return BS((1, L, w), blk)

    return [
        lanes(QW, lambda b, q: (b, 0, q)),
        lanes(LANE, lambda b, q: (b, 0, 8 + q // 2)),
        lanes(LANE, lambda b, q: (b, 0, 10 + q // 2)),
        lanes(LANE, lambda b, q: (b, 0, 0)),
        BS((1, LANE), lambda b, q: (0, 0)),
        BS((1, QW), lambda b, q: (0, q)),
    ]


def _ssd_slot(d, ci):
    return ci if d == 0 else ci + 1


def _ssd_fwd(xbc_act, dtc, alog, dskip, *, name, rider=None):
    B, L, _ = xbc_act.shape
    nc = L // CHUNK

    def body(xs_ref, b_ref, c_ref, dt_ref, alog_ref, dsk_ref, y_ref, hs_ref):
        q = pl.program_id(1)
        alog_v = alog_ref[...]
        y_ref[0] = dsk_ref[...] * xs_ref[0]
        hs_ref[0, 0, 0, 0] = jnp.zeros((LANE, QW), F32)
        hs_ref[0, 0, 1, nc] = jnp.zeros((LANE, QW), F32)

        def step(i, carry):
            cis = (i, nc - 1 - i)
            rows = [pl.ds(pl.multiple_of(ci * CHUNK, CHUNK), CHUNK) for ci in cis]
            res = []
            for d in range(2):
                cur = _ssd_slot(d, cis[d])
                res.append(_ssd_chunk(
                    hs_ref[0, 0, d, cur, :, :LANE], hs_ref[0, 0, d, cur, :, LANE:], xs_ref[0, rows[d], :LANE],
                    xs_ref[0, rows[d], LANE:], b_ref[0, rows[d], :], c_ref[0, rows[d], :], dt_ref[0, rows[d], :], alog_v,
                    col0=SSM_HEADS * d + 2 * PAIRS * q, reverse=d == 1))
            for d in range(2):
                y0, y1, n0, n1 = res[d]
                nxt = _ssd_slot(d, cis[d] + 1 if d == 0 else cis[d] - 1)
                hs_ref[0, 0, d, nxt, :, :LANE] = n0
                hs_ref[0, 0, d, nxt, :, LANE:] = n1
                y_ref[0, rows[d], :LANE] += y0
                y_ref[0, rows[d], LANE:] += y1
            return carry

        lax.fori_loop(0, nc, step, 0, unroll=2)

    (y, hs), carried = _pc_carry(
        body, (xbc_act, xbc_act, xbc_act, dtc, alog, dskip), name=name, grid=(B, QUADS), in_specs=_ssd_specs(B, L),
        out_specs=[BS((1, L, QW), lambda b, q: (b, 0, q)), BS((1, 1, 2, nc + 1, LANE, QW), lambda b, q: (b, q, 0, 0, 0, 0))],
        out_shape=[SDS((B, L, SSM_WIDTH), F32), SDS((B, QUADS, 2, nc + 1, LANE, QW), F32)], rider=rider)
    return y, hs, carried


def _ssd_bwd(xbc_act, dtc, alog, dskip, hs, dy, *, name, rider=None):
    B, L, _ = xbc_act.shape
    nc = L // CHUNK

    def body(xs_ref, b_ref, c_ref, dt_ref, alog_ref, dsk_ref, hs_ref, dy_ref,
             dxs_ref, db_ref, dc_ref, ddt_ref, dalog_ref, ddsk_ref, dh_ref):
        q = pl.program_id(1)
        alog_v = alog_ref[...]

        @pl.when(q % 2 == 0)
        def _():
            db_ref[...] = jnp.zeros_like(db_ref)
            dc_ref[...] = jnp.zeros_like(dc_ref)

        @pl.when(q == 0)
        def _():
            ddt_ref[...] = jnp.zeros_like(ddt_ref)

        dxs_ref[0] = dy_ref[0] * dsk_ref[...]
        ddsk_ref[0] = jnp.sum(dy_ref[0] * xs_ref[0], axis=0, keepdims=True)
        dh_ref[...] = jnp.zeros_like(dh_ref)

        def step(i, carry):
            g_alog = carry
            cis = (nc - 1 - i, i)
            rows = [pl.ds(pl.multiple_of(ci * CHUNK, CHUNK), CHUNK) for ci in cis]
            res = []
            for d in range(2):
                cur = _ssd_slot(d, cis[d])
                fn = functools.partial(_ssd_chunk, col0=SSM_HEADS * d + 2 * PAIRS * q, reverse=d == 1)
                _, vjp = jax.vjp(fn, hs_ref[0, 0, d, cur, :, :LANE], hs_ref[0, 0, d, cur, :, LANE:], xs_ref[0, rows[d], :LANE],
                                 xs_ref[0, rows[d], LANE:], b_ref[0, rows[d], :], c_ref[0, rows[d], :], dt_ref[0, rows[d], :],
                                 alog_v)
                res.append(vjp((dy_ref[0, rows[d], :LANE], dy_ref[0, rows[d], LANE:], dh_ref[d, :, :LANE], dh_ref[d, :, LANE:])))
            for d in range(2):
                g_h0, g_h1, g_x0, g_x1, g_b, g_c, g_dt, g_alog1 = res[d]
                dh_ref[d, :, :LANE] = g_h0
                dh_ref[d, :, LANE:] = g_h1
                dxs_ref[0, rows[d], :LANE] += g_x0
                dxs_ref[0, rows[d], LANE:] += g_x1
                db_ref[0, rows[d], :] += g_b
                dc_ref[0, rows[d], :] += g_c
                ddt_ref[0, rows[d], :] += g_dt
                g_alog = g_alog + g_alog1
            return g_alog

        dalog_ref[0, 0] = lax.fori_loop(0, nc, step, jnp.zeros((1, LANE), F32))

    lanes = lambda w, blk: BS((1, L, w), blk)
    in_specs = _ssd_specs(B, L) + [BS((1, 1, 2, nc + 1, LANE, QW), lambda b, q: (b, q, 0, 0, 0, 0)), lanes(QW, lambda b, q: (b, 0, q))]
    out_specs = [lanes(QW, lambda b, q: (b, 0, q)), lanes(LANE, lambda b, q: (b, 0, q // 2)), lanes(LANE, lambda b, q: (b, 0, q // 2)),
                 lanes(LANE, lambda b, q: (b, 0, 0)), BS((1, 1, 1, LANE), lambda b, q: (b, q, 0, 0)),
                 BS((1, 1, QW), lambda b, q: (b, 0, q))]
    out_shape = [SDS((B, L, SSM_WIDTH), F32), SDS((B, L, BC_WIDTH), F32), SDS((B, L, BC_WIDTH), F32), SDS((B, L, LANE), F32),
                 SDS((B, QUADS, 1, LANE), F32), SDS((B, 1, SSM_WIDTH), F32)]
    outs, carried = _pc_carry(body, (xbc_act, xbc_act, xbc_act, dtc, alog, dskip, hs, dy), name=name, grid=(B, QUADS),
                              in_specs=in_specs, out_specs=out_specs, out_shape=out_shape,
                              scratch_shapes=[pltpu.VMEM((2, LANE, QW), F32)], rider=rider)
    return (*outs, carried)


def _gate_norm(yp, z, w):
    v = yp * (z * _sigmoid(z))
    return v * lax.rsqrt(jnp.mean(v * v, axis=-1, keepdims=True) + NORM_EPS) * w


def _gate_fwd(ypre2, proj2, w, *, name):
    T = ypre2.shape[0]
    tr = _div_tile(T, 512, 8)
    G = 512

    def body(y_ref, z_ref, w_ref, o_ref):
        o_ref[...] = _gate_norm(y_ref[...], z_ref[...], w_ref[...]).astype(BF16)

    return _pc(body, name=name, grid=(T // tr, 2),
               in_specs=[BS((tr, G), lambda i, g: (i, g)), BS((tr, G), lambda i, g: (i, 2 + g)), BS((1, G), lambda i, g: (0, g))],
               out_specs=BS((tr, G), lambda i, g: (i, g)), out_shape=SDS((T, 2 * SSM_WIDTH), BF16))(ypre2, proj2, w.reshape(1, -1))


def _gate_bwd(ypre2, proj2, w, dy, *, name):
    T = ypre2.shape[0]
    tr = _div_tile(T, 512, 8)
    G = 512

    def body(y_ref, z_ref, w_ref, dy_ref, dyp_ref, dz_ref, dw_ref):
        _, vjp = jax.vjp(_gate_norm, y_ref[...], z_ref[...], w_ref[...])
        dyp, dz, dw = vjp(dy_ref[...])
        dyp_ref[...] = dyp
        dz_ref[...] = dz.astype(BF16)
        dw_ref[0] = dw

    tile = BS((tr, G), lambda i, g: (i, g))
    return _pc(body, name=name, grid=(T // tr, 2),
               in_specs=[tile, BS((tr, G), lambda i, g: (i, 2 + g)), BS((1, G), lambda i, g: (0, g)), tile],
               out_specs=[tile, tile, BS((1, 1, G), lambda i, g: (i, 0, g))],
               out_shape=[SDS((T, SSM_WIDTH), F32), SDS((T, SSM_WIDTH), BF16), SDS((T // tr, 1, SSM_WIDTH), F32)])(
        ypre2, proj2, w.reshape(1, -1), dy)


def _first_half():
    return lax.broadcasted_iota(jnp.int32, (1, LANE), 1) < HEAD_DIM


def _dup_kv_head(pair, odd):
    rolled = pltpu.roll(pair, HEAD_DIM, 1)
    return jnp.where(_first_half(), rolled, pair) if odd else jnp.where(_first_half(), pair, rolled)


def _stack_heads(quad):
    first = _first_half()
    lo, hi = quad[:, :LANE], quad[:, LANE:]
    return jnp.concatenate([jnp.where(first, lo, 0.0), jnp.where(first, 0.0, lo), jnp.where(first, hi, 0.0),
                            jnp.where(first, 0.0, hi)], axis=0)


def _unstack_heads(o):
    first = _first_half()
    return jnp.concatenate([jnp.where(first, o[:BLOCK], o[BLOCK:2 * BLOCK]), jnp.where(first, o[2 * BLOCK:3 * BLOCK], o[3 * BLOCK:])], axis=1)


def _fold_kv_head(d, odd):
    tot = d + pltpu.roll(d, HEAD_DIM, 1)
    return jnp.where(_first_half(), 0.0, tot) if odd else jnp.where(_first_half(), tot, 0.0)


def _attn_softmax(s, sink):
    m = jnp.maximum(jnp.max(s, axis=-1, keepdims=True), sink)
    p = jnp.exp(s - m)
    ps = jnp.exp(sink - m)
    inv = 1.0 / (jnp.sum(p, axis=-1, keepdims=True) + ps)
    return p * inv, ps * inv


def _attn_colneg(n, L):
    kpos = n * BLOCK - WINDOW + lax.broadcasted_iota(jnp.int32, (1, KEY_SPAN), 1)
    return jnp.where((kpos >= 0) & (kpos < L), 0.0, NEG)


def _attn_in_specs(L):
    nblk = L // BLOCK
    kv = lambda o, col: BS((1, BLOCK, 4 * HEAD_DIM), lambda b, n: (b, jnp.clip(n + o, 0, nblk - 1), col))
    kcol, vcol = 3584 // 256, 3840 // 256
    return [BS((1, BLOCK, ATTN_HEADS * HEAD_DIM), lambda b, n: (b, n, 0)), kv(-1, kcol), kv(0, kcol), kv(1, kcol),
            kv(-1, vcol), kv(0, vcol), kv(1, vcol),
            BS((ATTN_HEADS, BLOCK, KEY_SPAN), lambda b, n: (0, 0, 0)), BS((ATTN_HEADS * BLOCK, 1), lambda b, n: (0, 0))]


def _attn_fwd(proj, bias, sinkcol, mixed, *, name, rider=None):
    B, L, _ = proj.shape

    def body(q_ref, k0, k1, k2, v0, v1, v2, bias_ref, sink_ref, _, o_ref):
        colneg = _attn_colneg(pl.program_id(1), L)
        kcat = jnp.concatenate([k0[0], k1[0], k2[0]], axis=0)
        vcat = jnp.concatenate([v0[0], v1[0], v2[0]], axis=0)
        scores, probs, scales = [], [], []
        for g in range(KV_HEADS):
            pair = slice(LANE * (g // 2), LANE * (g // 2) + LANE)
            quad = slice(4 * HEAD_DIM * g, 4 * HEAD_DIM * (g + 1))
            kd = _dup_kv_head(kcat[:, pair], g % 2).astype(BF16)
            qs = (_stack_heads(q_ref[0, :, quad]) * HEAD_DIM ** -0.5).astype(BF16)
            scores.append(lax.dot_general(qs, kd, (((1,), (1,)), ((), ())), preferred_element_type=F32))
        for g in range(KV_HEADS):
            s = scores[g] + bias_ref[4 * g:4 * g + 4].reshape(4 * BLOCK, KEY_SPAN) + colneg
            sink = sink_ref[4 * BLOCK * g:4 * BLOCK * (g + 1)]
            m = jnp.maximum(jnp.max(s, axis=-1, keepdims=True), sink)
            p = jnp.exp(s - m)
            scales.append(1.0 / (jnp.sum(p, axis=-1, keepdims=True) + jnp.exp(sink - m)))
            probs.append(p.astype(BF16))
        for g in range(KV_HEADS):
            pair = slice(LANE * (g // 2), LANE * (g // 2) + LANE)
            quad = slice(4 * HEAD_DIM * g, 4 * HEAD_DIM * (g + 1))
            vd = _dup_kv_head(vcat[:, pair], g % 2).astype(BF16)
            o = lax.dot_general(probs[g], vd, (((1,), (0,)), ((), ())), preferred_element_type=F32) * scales[g]
            o_ref[0, :, quad] = _unstack_heads(o).astype(BF16)

    (out,), carried = _pc_carry(body, (proj, proj, proj, proj, proj, proj, proj, bias, sinkcol, mixed), name=name,
                                grid=(B, L // BLOCK), in_specs=_attn_in_specs(L) + [BS(memory_space=pl.ANY)],
                                out_specs=[BS((1, BLOCK, ATTN_HEADS * HEAD_DIM), lambda b, n: (b, n, 1))],
                                out_shape=[SDS(mixed.shape, BF16)], rider=rider, aliases={9: 0})
    return out, carried


def _attn_bwd(proj, bias, sinkcol, dout, *, name, rider=None):
    B, L, _ = proj.shape
    nblk = L // BLOCK
    nn, nt, tn = (((1,), (0,)), ((), ())), (((1,), (1,)), ((), ())), (((0,), (0,)), ((), ()))

    def body(q_ref, k0, k1, k2, v0, v1, v2, bias_ref, sink_ref, do_ref, dq_ref, dk_ref, dv_ref, dbias_ref, dsink_ref):
        b, n = pl.program_id(0), pl.program_id(1)

        @pl.when(n == 0)
        def _():
            dk_ref[...] = jnp.zeros_like(dk_ref)
            dv_ref[...] = jnp.zeros_like(dv_ref)

        @pl.when((n == 0) & (b == 0))
        def _():
            dbias_ref[...] = jnp.zeros_like(dbias_ref)
            dsink_ref[...] = jnp.zeros_like(dsink_ref)

        colneg = _attn_colneg(n, L)
        kcat = jnp.concatenate([k0[0], k1[0], k2[0]], axis=0)
        vcat = jnp.concatenate([v0[0], v1[0], v2[0]], axis=0)
        krows = [pl.ds(pl.multiple_of(jnp.clip(n + o, 0, nblk - 1) * BLOCK, BLOCK), BLOCK) for o in (-1, 0, 1)]
        ops, mids = [], []
        for g in range(KV_HEADS):
            pair = slice(LANE * (g // 2), LANE * (g // 2) + LANE)
            quad = slice(4 * HEAD_DIM * g, 4 * HEAD_DIM * (g + 1))
            kd = _dup_kv_head(kcat[:, pair], g % 2).astype(BF16)
            vd = _dup_kv_head(vcat[:, pair], g % 2).astype(BF16)
            qs = (_stack_heads(q_ref[0, :, quad]) * HEAD_DIM ** -0.5).astype(BF16)
            dos = _stack_heads(do_ref[0, :, quad].astype(F32)).astype(BF16)
            ops.append((kd, qs, dos, lax.dot_general(qs, kd, nt, preferred_element_type=F32),
                        lax.dot_general(dos, vd, nt, preferred_element_type=F32)))
        for g in range(KV_HEADS):
            rows = slice(4 * BLOCK * g, 4 * BLOCK * (g + 1))
            _, _, _, s, dpn = ops[g]
            pn, psink = _attn_softmax(s + bias_ref[4 * g:4 * g + 4].reshape(4 * BLOCK, KEY_SPAN) + colneg, sink_ref[rows])
            r = jnp.sum(dpn * pn, axis=-1, keepdims=True)
            ds = pn * (dpn - r)
            dbias_ref[4 * g:4 * g + 4] += ds.reshape(4, BLOCK, KEY_SPAN)
            dsink_ref[rows] += -psink * r
            mids.append((pn.astype(BF16), ds.astype(BF16)))
        for g in range(KV_HEADS):
            pair = slice(LANE * (g // 2), LANE * (g // 2) + LANE)
            quad = slice(4 * HEAD_DIM * g, 4 * HEAD_DIM * (g + 1))
            kd, qs, dos, _, _ = ops[g]
            pnb, dsb = mids[g]
            dvd = lax.dot_general(pnb, dos, tn, preferred_element_type=F32)
            dkd = lax.dot_general(dsb, qs, tn, preferred_element_type=F32)
            dqs = lax.dot_general(dsb, kd, nn, preferred_element_type=F32) * HEAD_DIM ** -0.5
            dq_ref[0, :, quad] = _unstack_heads(dqs).astype(BF16)
            dk_g, dv_g = _fold_kv_head(dkd, g % 2), _fold_kv_head(dvd, g % 2)
            for o in range(3):
                dk_ref[0, krows[o], pair] += dk_g[o * BLOCK:(o + 1) * BLOCK]
                dv_ref[0, krows[o], pair] += dv_g[o * BLOCK:(o + 1) * BLOCK]

    qspec = BS((1, BLOCK, ATTN_HEADS * HEAD_DIM), lambda b, n: (b, n, 0))
    kvout = BS((1, L, 4 * HEAD_DIM), lambda b, n: (b, 0, 0))
    outs, carried = _pc_carry(
        body, (proj, proj, proj, proj, proj, proj, proj, bias, sinkcol, dout), name=name, grid=(B, nblk),
        in_specs=_attn_in_specs(L) + [BS((1, BLOCK, ATTN_HEADS * HEAD_DIM), lambda b, n: (b, n, 1))],
        out_specs=[qspec, kvout, kvout, BS((ATTN_HEADS, BLOCK, KEY_SPAN), lambda b, n: (0, 0, 0)),
                   BS((ATTN_HEADS * BLOCK, 1), lambda b, n: (0, 0))],
        out_shape=[SDS((B, L, ATTN_HEADS * HEAD_DIM), BF16), SDS((B, L, 4 * HEAD_DIM), F32), SDS((B, L, 4 * HEAD_DIM), F32),
                   SDS((ATTN_HEADS, BLOCK, KEY_SPAN), F32), SDS((ATTN_HEADS * BLOCK, 1), F32)], rider=rider)
    return (*outs, carried)


def _t5_bucket(rel):
    half = REL_BUCKETS // 2
    max_exact = half // 2
    ret = jnp.where(rel > 0, half, 0)
    n = jnp.abs(rel)
    nf = jnp.maximum(n, 1).astype(F32)
    large = max_exact + (jnp.log(nf / max_exact) / math.log(REL_MAX_DIST / max_exact) * (half - max_exact)).astype(jnp.int32)
    large = jnp.minimum(large, half - 1)
    return ret + jnp.where(n < max_exact, n, large)


def _bucket_table():
    rel = jnp.arange(KEY_SPAN)[None, :] - WINDOW - jnp.arange(BLOCK)[:, None]
    return _t5_bucket(rel).astype(jnp.int32)


def _bias_expand(rel_bias, bucket, *, name):
    rbt = jnp.zeros((ATTN_HEADS, 1, LANE), F32).at[:, 0, :REL_BUCKETS].set(rel_bias.T)

    def body(rb_ref, bk_ref, o_ref):
        lane = lax.broadcasted_iota(jnp.int32, (1, LANE), 1)
        row = rb_ref[0]
        bk = bk_ref[...]
        acc = jnp.zeros((BLOCK, KEY_SPAN), F32)
        for r in range(REL_BUCKETS):
            val = jnp.sum(jnp.where(lane == r, row, 0.0), axis=1, keepdims=True)
            acc = jnp.where(bk == r, val, acc)
        rel = (lax.broadcasted_iota(jnp.int32, (BLOCK, KEY_SPAN), 1) - WINDOW
               - lax.broadcasted_iota(jnp.int32, (BLOCK, KEY_SPAN), 0))
        o_ref[0] = jnp.where(jnp.abs(rel) <= WINDOW, acc, NEG)

    return _pc(body, name=name, grid=(ATTN_HEADS,),
               in_specs=[BS((1, 1, LANE), lambda h: (h, 0, 0)), BS((BLOCK, KEY_SPAN), lambda h: (0, 0))],
               out_specs=BS((1, BLOCK, KEY_SPAN), lambda h: (h, 0, 0)), out_shape=SDS((ATTN_HEADS, BLOCK, KEY_SPAN), F32))(rbt, bucket)


def _bias_reduce(dbias, bucket, *, name):
    def body(db_ref, bk_ref, o_ref):
        lane = lax.broadcasted_iota(jnp.int32, (1, LANE), 1)
        x = db_ref[0]
        bk = bk_ref[...]
        acc = jnp.zeros((1, LANE), F32)
        for r in range(REL_BUCKETS):
            part = jnp.sum(jnp.where(bk == r, x, 0.0), axis=1, keepdims=True)
            acc = jnp.where(lane == r, jnp.sum(part, axis=0, keepdims=True), acc)
        o_ref[0] = acc

    out = _pc(body, name=name, grid=(ATTN_HEADS,),
              in_specs=[BS((1, BLOCK, KEY_SPAN), lambda h: (h, 0, 0)), BS((BLOCK, KEY_SPAN), lambda h: (0, 0))],
              out_specs=BS((1, 1, LANE), lambda h: (h, 0, 0)), out_shape=SDS((ATTN_HEADS, 1, LANE), F32))(dbias, bucket)
    return out[:, 0, :REL_BUCKETS].T


def _loss_head(x2, w, target, *, name):
    T, D = x2.shape
    tr = _div_tile(T, 512, 8)

    def tile_loss(x, w, t):
        err = _rms(x, w) - t
        return 0.5 * jnp.sum(jnp.mean(err * err, axis=-1, keepdims=True), axis=0, keepdims=True)

    def body(x_ref, w_ref, t_ref, loss_ref, dx_ref, dxb_ref, dw_ref):
        t = t_ref[...]
        l, vjp = jax.vjp(lambda x, w: tile_loss(x, w, t), x_ref[...], w_ref[...])
        dx, dw = vjp(jnp.ones((1, 1), F32))
        dx_ref[...] = dx
        dxb_ref[...] = dx.astype(BF16)

        @pl.when(pl.program_id(0) == 0)
        def _():
            dw_ref[...] = jnp.zeros_like(dw_ref)
            loss_ref[...] = jnp.zeros_like(loss_ref)

        dw_ref[...] += dw
        loss_ref[...] += l + jnp.zeros((1, LANE), F32)

    row = BS((tr, D), lambda i: (i, 0))
    one = BS((1, D), lambda i: (0, 0))
    return _pc(body, name=name, grid=(T // tr,), in_specs=[row, one, row],
               out_specs=[BS((1, LANE), lambda i: (0, 0)), row, row, one],
               out_shape=[SDS((1, LANE), F32), SDS((T, D), F32), SDS((T, D), BF16), SDS((1, D), F32)])(x2, w.reshape(1, D), target)


def _adamw(w2, g2, m2, v2, *, name):
    R, C = w2.shape
    tr = _div_tile(R, 256, 8)
    c1 = 1.0 - ADAM_B1 ** ADAM_STEP
    c2 = 1.0 - ADAM_B2 ** ADAM_STEP

    def body(w_ref, g_ref, m_ref, v_ref, d_ref, nm_ref, nv_ref):
        g = g_ref[...]
        m = ADAM_B1 * m_ref[...] + (1.0 - ADAM_B1) * g
        v = ADAM_B2 * v_ref[...] + (1.0 - ADAM_B2) * (g * g)
        d_ref[...] = -ADAM_LR * ((m / c1) / (jnp.sqrt(v / c2) + ADAM_EPS) + ADAM_WD * w_ref[...])
        nm_ref[...] = m
        nv_ref[...] = v

    t = BS((tr, C), lambda i: (i, 0))
    return _pc(body, name=name, grid=(R // tr,), in_specs=[t, t, t, t], out_specs=[t, t, t],
               out_shape=[SDS((R, C), F32)] * 3)(w2, g2, m2, v2)


def _place():
    return lax.axis_index("x"), lax.axis_index("y"), lax.axis_index("c")


def _gather_rider(shards):
    na = len(shards)

    def copies(ins, outs, sems):
        send_sems, recv_sems = sems
        x, y, c = _place()
        for a in range(na):
            for k, (px, py) in enumerate([(1 - x, y), (x, 1 - y), (1 - x, 1 - y)]):
                send = functools.partial(pltpu.make_async_remote_copy, ins[a], outs[a].at[2 * x + y], send_sems.at[a, k],
                                         recv_sems.at[a, k], device_id=(px, py, c), device_id_type=MESH)
                got = outs[a].at[2 * px + py]
                arrived = functools.partial(pltpu.make_async_remote_copy, got, got, send_sems.at[a, k], recv_sems.at[a, k],
                                            device_id=(px, py, c), device_id_type=MESH)
                yield send, arrived

    def start(ins, outs, sems):
        for send, _ in copies(ins, outs, sems):
            send().start()

    def finish(ins, outs, sems):
        both = list(copies(ins, outs, sems))
        for _, arrived in both:
            arrived().wait_recv()
        for send, _ in both:
            send().wait_send()

    return dict(ins=list(shards), prev=[], out_shape=[SDS((N_CHIP,) + s.shape, s.dtype) for s in shards],
                scratch=[pltpu.SemaphoreType.DMA((na, 3)), pltpu.SemaphoreType.DMA((na, 3))], start=start, finish=finish)


def _scatter_rider(bufs, layer, prev):
    na = len(bufs)
    h = layer // (DEPTH // 2)

    def copies(ins, outs, sems):
        send_sems, recv_sems, local_sems = sems
        x, y, c = _place()
        me = 4 * x + 2 * y + c
        for a in range(na):
            for j in range(N_CHIP):
                is_self = ((2 * x + y) == j) & (c == h)
                local = functools.partial(pltpu.make_async_copy, ins[a].at[j], outs[a].at[me], local_sems.at[a])
                remote = functools.partial(pltpu.make_async_remote_copy, ins[a].at[j], outs[a].at[me], send_sems.at[a, j],
                                           recv_sems.at[a, me], device_id=(j // 2, j % 2, h), device_id_type=MESH)
                yield is_self, local, remote

    def start(ins, outs, sems):
        for is_self, local, remote in copies(ins, outs, sems):
            pl.when(is_self)(lambda: local().start())
            pl.when(jnp.logical_not(is_self))(lambda: remote().start())

    def finish(ins, outs, sems):
        _, recv_sems, _ = sems
        x, y, c = _place()
        me = 4 * x + 2 * y + c
        for a in range(na):
            for s in range(N_DEV):
                got = outs[a].at[s]
                arrived = functools.partial(pltpu.make_async_remote_copy, got, got, recv_sems.at[a, s], recv_sems.at[a, s],
                                            device_id=(s // 4, (s // 2) % 2, s % 2), device_id_type=MESH)
                pl.when((c == h) & (me != s))(lambda: arrived().wait_recv())
        for is_self, local, remote in copies(ins, outs, sems):
            pl.when(is_self)(lambda: local().wait())
            pl.when(jnp.logical_not(is_self))(lambda: remote().wait_send())

    return dict(ins=list(bufs), prev=list(prev), out_shape=[SDS((N_DEV,) + b.shape[1:], b.dtype) for b in bufs],
                scratch=[pltpu.SemaphoreType.DMA((na, N_CHIP)), pltpu.SemaphoreType.DMA((na, N_DEV)), pltpu.SemaphoreType.DMA((na,))],
                start=start, finish=finish)


def _sum_sources(parts, *, name):
    _, R, C = parts.shape
    tr = _div_tile(R, 256, 16)

    def body(p_ref, o_ref):
        acc = p_ref[0].astype(F32)
        for s in range(1, N_DEV):
            acc = acc + p_ref[s].astype(F32)
        o_ref[...] = acc

    return _pc(body, name=name, grid=(R // tr,), in_specs=[BS((N_DEV, tr, C), lambda i: (0, i, 0))],
               out_specs=BS((tr, C), lambda i: (i, 0)), out_shape=SDS((R, C), F32))(parts)


def _join_halves(halves, *, name):
    na = len(halves)

    def body(*refs):
        ins, outs = refs[:na], refs[na:2 * na]
        send_sems, recv_sems = refs[2 * na:]
        x, y, c = _place()
        cps = []
        for a in range(na):
            cp = pltpu.make_async_remote_copy(ins[a], outs[a], send_sems.at[a], recv_sems.at[a],
                                              device_id=(x, y, 1 - c), device_id_type=MESH)
            cp.start()
            cps.append(cp)
        for cp in cps:
            cp.wait_recv()
        for cp in cps:
            cp.wait_send()

    any_spec = BS(memory_space=pl.ANY)
    return pl.pallas_call(
        body, name=name, in_specs=[any_spec] * na, out_specs=[any_spec] * na,
        out_shape=[SDS(h.shape, h.dtype) for h in halves],
        scratch_shapes=[pltpu.SemaphoreType.DMA((na,)), pltpu.SemaphoreType.DMA((na,))],
        compiler_params=pltpu.CompilerParams(has_side_effects=True))(*halves)


def _allreduce_small(vec, *, name):
    R = vec.shape[0]

    def body(v_ref, o_ref, all_ref, send_sems, recv_sems):
        x, y, c = _place()
        me = 4 * x + 2 * y + c
        all_ref[me] = v_ref[...]
        sends = []
        for r in range(1, N_DEV):
            tgt = (x ^ (r >> 2), y ^ ((r >> 1) & 1), c ^ (r & 1))
            cp = pltpu.make_async_remote_copy(v_ref, all_ref.at[me], send_sems.at[r - 1], recv_sems.at[r - 1],
                                              device_id=tgt, device_id_type=MESH)
            cp.start()
            sends.append(cp)
        for r in range(1, N_DEV):
            tx, ty, tc = x ^ (r >> 2), y ^ ((r >> 1) & 1), c ^ (r & 1)
            got = all_ref.at[4 * tx + 2 * ty + tc]
            pltpu.make_async_remote_copy(got, got, send_sems.at[r - 1], recv_sems.at[r - 1],
                                         device_id=(tx, ty, tc), device_id_type=MESH).wait_recv()
        for cp in sends:
            cp.wait_send()
        acc = all_ref[0]
        for s in range(1, N_DEV):
            acc = acc + all_ref[s]
        o_ref[...] = acc

    vm = BS(memory_space=pltpu.VMEM)
    return pl.pallas_call(
        body, name=name, in_specs=[vm], out_specs=vm, out_shape=SDS((R, LANE), F32),
        scratch_shapes=[pltpu.VMEM((N_DEV, R, LANE), F32), pltpu.SemaphoreType.DMA((N_DEV - 1,)), pltpu.SemaphoreType.DMA((N_DEV - 1,))],
        compiler_params=pltpu.CompilerParams(has_side_effects=True, vmem_limit_bytes=VMEM_LIMIT_BYTES))(vec)


def _pack(arrs):
    rows = []
    for a in arrs:
        f = a.reshape(-1).astype(F32)
        n = -(-f.shape[0] // LANE) * LANE
        rows.append(jnp.pad(f, (0, n - f.shape[0])).reshape(-1, LANE))
    v = jnp.concatenate(rows, axis=0)
    pad = -v.shape[0] % 8
    return jnp.pad(v, ((0, pad), (0, 0)))


def _unpack(v, shapes):
    out, r = [], 0
    for s in shapes:
        n = int(np.prod(s)) if len(s) else 1
        nr = -(-n // LANE)
        out.append(v[r:r + nr].reshape(-1)[:n].reshape(s))
        r += nr
    return out


def _perm_in_cols(w_full):
    z, xbc, dt, q, k, v = (w_full[..., :Z_END], w_full[..., Z_END:XBC_END], w_full[..., XBC_END:DT_END],
                           w_full[..., DT_END:Q_END], w_full[..., Q_END:K_END], w_full[..., K_END:])
    pad = jnp.zeros(dt.shape[:-1] + (LANE - dt.shape[-1],), dt.dtype)
    return jnp.concatenate([q, z, xbc, k, v, dt, pad], axis=-1)


def _unperm_in_cols(g):
    q, z, xbc, k, v, dt = (g[..., :1024], g[..., 1024:2048], g[..., 2048:3584], g[..., 3584:3840], g[..., 3840:4096],
                           g[..., 4096:4096 + 2 * SSM_HEADS])
    return jnp.concatenate([z, xbc, dt, q, k, v], axis=-1)


def _dt_cols(a):
    return jnp.pad(a.reshape(1, 2 * SSM_HEADS), ((0, 0), (0, LANE - 2 * SSM_HEADS)))


def _dt_fwd(proj, dtb, *, name):
    B, L, _ = proj.shape

    def body(p_ref, b_ref, o_ref):
        o_ref[0] = _softplus(p_ref[0] + b_ref[...])

    return _pc(body, name=name, grid=(B,),
               in_specs=[BS((1, L, LANE), lambda b: (b, 0, P_COLS // LANE - 1)), BS((1, LANE), lambda b: (0, 0))],
               out_specs=BS((1, L, LANE), lambda b: (b, 0, 0)), out_shape=SDS((B, L, LANE), F32))(proj, dtb)


def _dt_bwd(proj, dtb, ddt, *, name):
    B, L, _ = proj.shape

    def body(p_ref, b_ref, g_ref, o_ref, db_ref):
        g = g_ref[0] * _sigmoid(p_ref[0] + b_ref[...])
        o_ref[0] = g.astype(BF16)
        db_ref[0] = jnp.sum(g, axis=0, keepdims=True)

    row = BS((1, L, LANE), lambda b: (b, 0, 0))
    return _pc(body, name=name, grid=(B,),
               in_specs=[BS((1, L, LANE), lambda b: (b, 0, P_COLS // LANE - 1)), BS((1, LANE), lambda b: (0, 0)), row],
               out_specs=[row, BS((1, 1, LANE), lambda b: (b, 0, 0))],
               out_shape=[SDS((B, L, LANE), BF16), SDS((B, 1, LANE), F32)])(proj, dtb, ddt)


def _layer_fwd(i, x, wts, small, band_bias, riders=None, arrived=None):
    riders = riders or {}
    B, L, D = x.shape
    T = B * L
    x2 = x.reshape(T, D)
    h, proj2 = _norm_mm(x2, small["norm1_w"][i], wts["w_in"], name=f"in_proj_{i}", tn=1408)
    proj = proj2.reshape(B, L, P_COLS)
    xbc_act = _conv_fwd(proj, 2048 // 256, small["conv_w"][i], small["conv_b"][i], taps=SSM_CONV, ct=256,
                        out_dtype=F32, name=f"ssm_conv_{i}")
    dtb, alog = _dt_cols(small["dt_bias"][i]), _dt_cols(small["a_log"][i])
    dskip = jnp.repeat(small["d_skip"][i], HEAD_DIM).reshape(1, SSM_WIDTH)
    dtc = _dt_fwd(proj, dtb, name=f"dt_{i}")
    ypre, hs, carried = _ssd_fwd(xbc_act, dtc, alog, dskip, name=f"ssd_{i}", rider=riders.get("ssd"))
    if carried is not None:
        arrived("ssd", carried)
    mixed = _gate_fwd(ypre.reshape(T, SSM_WIDTH), proj2, small["ssm_norm_w"][i], name=f"gate_{i}")
    sinkcol = jnp.repeat(small["attn_sink"][i], BLOCK).reshape(ATTN_HEADS * BLOCK, 1)
    mixed, carried = _attn_fwd(proj, band_bias, sinkcol, mixed.reshape(B, L, 2 * D), name=f"attn_{i}", rider=riders.get("attn"))
    if carried is not None:
        arrived("attn", carried)
    mixed = mixed.reshape(T, 2 * D)
    x_mid = _mm(mixed, wts["w_out"], add=x2, name=f"out_proj_{i}")
    h2, gu2 = _norm_mm(x_mid, small["norm2_w"][i], wts["w_up"], name=f"up_proj_{i}", out_dtype=BF16, tn=1408)
    gu = gu2.reshape(B, L, 2 * D_FF)
    act = _conv_fwd(gu, 0, small["ffn_conv_w"][i], small["ffn_conv_b"][i], taps=FFN_CONV, ct=256, gate_blk0=D_FF // 256,
                    out_dtype=BF16, name=f"ffn_conv_{i}")
    x_out = _mm(act.reshape(T, D_FF), wts["w_down"], add=x_mid, name=f"down_proj_{i}", tk=1408)
    saved = dict(x2=x2, h=h, proj2=proj2, xbc_act=xbc_act, dtb=dtb, dtc=dtc, alog=alog, dskip=dskip, ypre=ypre, hs=hs, mixed=mixed,
                 sinkcol=sinkcol, x_mid=x_mid, h2=h2, gu=gu, act=act)
    return x_out.reshape(B, L, D), saved


def _layer_bwd(i, dx_out, dxb, sv, wts, small, band_bias, attn_rider=None, ssd_rider=None, arrived=None, wgrad_dtype=F32):
    T, D = dx_out.shape
    B, L = sv["gu"].shape[:2]
    g = {}
    dact = _mm(dxb, wts["w_down"], tb=True, out_dtype=BF16, name=f"d_act_{i}", tn=1408)
    g["w_down"] = _mm(sv["act"].reshape(T, D_FF), dxb, ta=True, out_dtype=wgrad_dtype, name=f"dw_down_{i}", tm=1408)
    dg, du, dcw, dcb = _conv_bwd(sv["gu"], 0, small["ffn_conv_w"][i], small["ffn_conv_b"][i], dact.reshape(B, L, D_FF),
                                 taps=FFN_CONV, ct=256, gate_blk0=D_FF // 256, name=f"d_ffn_conv_{i}")
    g["ffn_conv_w"] = jnp.sum(dcw, axis=0)[:FFN_CONV]
    g["ffn_conv_b"] = jnp.sum(dcb, axis=(0, 1))
    dg, du, w_up = dg.reshape(T, D_FF), du.reshape(T, D_FF), wts["w_up"]
    dh2 = _mm(dg, w_up[:, :D_FF], tb=True, name=f"d_h2_g_{i}", tk=1408)
    dh2 = _mm(du, w_up[:, D_FF:], tb=True, add=dh2, name=f"d_h2_u_{i}", tk=1408)
    dx_mid, dmb, dw2 = _dnorm(dh2, sv["x_mid"], small["norm2_w"][i], dx_out, name=f"d_norm2_{i}")
    g["w_up"] = jnp.concatenate([_mm(sv["h2"], dg, ta=True, out_dtype=wgrad_dtype, name=f"dw_up_g_{i}", tn=1408),
                                 _mm(sv["h2"], du, ta=True, out_dtype=wgrad_dtype, name=f"dw_up_u_{i}", tn=1408)], axis=1)
    g["norm2_w"] = dw2[0]
    dmixed = _mm(dmb, wts["w_out"], tb=True, name=f"d_mixed_{i}")
    g["w_out"] = _mm(sv["mixed"], dmb, ta=True, out_dtype=wgrad_dtype, name=f"dw_out_{i}")
    dypre, dz, dwn = _gate_bwd(sv["ypre"].reshape(T, SSM_WIDTH), sv["proj2"], small["ssm_norm_w"][i], dmixed, name=f"d_gate_{i}")
    g["ssm_norm_w"] = jnp.sum(dwn, axis=(0, 1))
    proj = sv["proj2"].reshape(B, L, P_COLS)
    dxs, dbm, dcm, ddt, dalog, ddsk, carried = _ssd_bwd(sv["xbc_act"], sv["dtc"], sv["alog"], sv["dskip"], sv["hs"],
                                                        dypre.reshape(B, L, SSM_WIDTH), name=f"d_ssd_{i}",
                                                        rider=ssd_rider(g) if ssd_rider is not None else None)
    if carried is not None:
        arrived("ssd", carried)
    ddt, ddtb = _dt_bwd(proj, sv["dtb"], ddt, name=f"d_dt_{i}")
    g["dt_bias"] = jnp.sum(ddtb, axis=(0, 1))[:2 * SSM_HEADS].reshape(2, SSM_HEADS)
    g["a_log"] = jnp.sum(dalog, axis=(0, 1, 2))[:2 * SSM_HEADS].reshape(2, SSM_HEADS)
    g["d_skip"] = jnp.sum(ddsk.reshape(B, SSM_HEADS, HEAD_DIM), axis=(0, 2))
    dxbc_act = jnp.concatenate([dxs, dbm, dcm], axis=-1)
    dxbc, dcw, dcb = _conv_bwd(proj, 2048 // 256, small["conv_w"][i], small["conv_b"][i], dxbc_act, taps=SSM_CONV, ct=256,
                               name=f"d_ssm_conv_{i}")
    g["conv_w"] = jnp.sum(dcw, axis=0)[:SSM_CONV]
    g["conv_b"] = jnp.sum(dcb, axis=(0, 1))
    dq, dk, dv, dbias, dsink, carried = _attn_bwd(proj, band_bias, sv["sinkcol"], dmixed.reshape(B, L, 2 * D), name=f"d_attn_{i}",
                                                  rider=attn_rider)
    if carried is not None:
        arrived("attn", carried)
    g["attn_sink"] = jnp.sum(dsink.reshape(ATTN_HEADS, BLOCK), axis=1)
    dproj = jnp.concatenate([dq, dz.reshape(B, L, SSM_WIDTH), dxbc, dk.astype(BF16), dv.astype(BF16), ddt], axis=-1).reshape(T, P_COLS)
    dh = _mm(dproj, wts["w_in"], tb=True, name=f"d_h_{i}", tk=1408)
    dx_in, dx_in_b, dw1 = _dnorm(dh, sv["x2"], small["norm1_w"][i], dx_mid, name=f"d_norm1_{i}")
    g["w_in"] = _unperm_in_cols(_mm(sv["h"], dproj, ta=True, out_dtype=wgrad_dtype, name=f"dw_in_{i}", tn=1408))
    g["norm1_w"] = dw1[0]
    return dx_in, dx_in_b, g, dbias


_BIG = ("w_in", "w_out", "w_up", "w_down")
_BIG_AXIS = {"w_in": 2, "w_out": 1, "w_up": 2, "w_down": 1}
_SMALL = ("rel_bias", "norm1_w", "conv_w", "conv_b", "dt_bias", "a_log", "d_skip", "ssm_norm_w", "attn_sink", "norm2_w",
          "ffn_conv_w", "ffn_conv_b", "final_norm_w")
_SMALL_SHARDED = ("conv_w", "ffn_conv_w")
_ORDER = ("rel_bias", "norm1_w", "w_in", "conv_w", "conv_b", "dt_bias", "a_log", "d_skip", "ssm_norm_w", "attn_sink", "w_out",
          "norm2_w", "w_up", "ffn_conv_w", "ffn_conv_b", "w_down", "final_norm_w")


def _local_step(x, target, small, wts=None, exchange=None):
    B, L, D = x.shape
    bucket = _bucket_table()
    band_bias = _bias_expand(small["rel_bias"], bucket, name="band_bias")

    def fetch(spec):
        return exchange["gather"](spec) if exchange is not None and spec else None

    def fetched(spec, carried):
        for (i, k), full in zip(spec, exchange["weights"](spec, carried)):
            wts[i][k] = full

    if exchange is not None:
        wts = [{} for _ in range(DEPTH)]
        fetched([(0, "w_in")], _run_rider(fetch([(0, "w_in")]), name="gather_w_in_0"))
    saved = []
    for i in range(DEPTH):
        nxt = i + 1 < DEPTH
        if i == 0:
            plan = {"ssd": [(0, "w_out"), (0, "w_up"), (0, "w_down")], "attn": [(1, "w_in"), (1, "w_out"), (1, "w_up")] if nxt else []}
        else:
            plan = {"ssd": [(i, "w_down")] + ([(i + 1, "w_in"), (i + 1, "w_out")] if nxt else []), "attn": [(i + 1, "w_up")] if nxt else []}
        x, sv = _layer_fwd(i, x, wts[i], small, band_bias, {c: fetch(s) for c, s in plan.items()}, lambda c, r: fetched(plan[c], r))
        saved.append(sv)
    loss, dx, dxb, dwf = _loss_head(x.reshape(B * L, D), small["final_norm_w"], target.reshape(B * L, D), name="loss_head")
    per_layer = []
    dbias = jnp.zeros((ATTN_HEADS, BLOCK, KEY_SPAN), F32)
    late = None
    for i in reversed(range(DEPTH)):
        own = [(i, "w_down"), (i, "w_up"), (i, "w_out")]
        plan = {"ssd": own, "attn": late[0] if late else []}
        attn_rider = exchange["scatter"](*late) if late else None
        ssd_rider = (lambda g: exchange["scatter"](own, [g[k] for _, k in own])) if exchange is not None else None
        dx, dxb, g, dbias_i = _layer_bwd(i, dx, dxb, saved[i], wts[i], small, band_bias, attn_rider, ssd_rider,
                                    lambda c, r: exchange["collect"](plan[c], r), F32 if exchange is None else BF16)
        if exchange is not None:
            late = ([(i, "w_in")], [g["w_in"]])
            for k in _BIG:
                g.pop(k)
        dbias = dbias + dbias_i
        per_layer.append(g)
    if exchange is not None:
        exchange["collect"](late[0], _run_rider(exchange["scatter"](*late), name="scatter_dw_in_0"))
    per_layer.reverse()
    grads = {k: jnp.stack([g[k] for g in per_layer]) for k in per_layer[0]}
    grads["rel_bias"] = _bias_reduce(dbias, bucket, name="d_rel_bias")
    grads["final_norm_w"] = dwf[0]
    return loss, dx.reshape(B, L, D), grads


def _split_by_chip(g, axis):
    shp = g.shape
    n = shp[axis] // N_CHIP
    g = g.reshape(shp[:axis] + (N_CHIP, n) + shp[axis + 1:])
    return jnp.moveaxis(g, axis, 0)


def _join_chips(a, axis):
    a = jnp.moveaxis(a, 0, axis)
    shp = a.shape
    return a.reshape(shp[:axis] + (shp[axis] * shp[axis + 1],) + shp[axis + 2:])


def kernel(x, rel_bias, norm1_w, w_in, conv_w, conv_b, dt_bias, a_log, d_skip, ssm_norm_w, attn_sink, w_out, norm2_w, w_up, ffn_conv_w, ffn_conv_b, w_down, final_norm_w, loss_target, m_rel_bias, m_norm1_w, m_w_in, m_conv_w, m_conv_b, m_dt_bias, m_a_log, m_d_skip, m_ssm_norm_w, m_attn_sink, m_w_out, m_norm2_w, m_w_up, m_ffn_conv_w, m_ffn_conv_b, m_w_down, m_final_norm_w, v_rel_bias, v_norm1_w, v_w_in, v_conv_w, v_conv_b, v_dt_bias, v_a_log, v_d_skip, v_ssm_norm_w, v_attn_sink, v_w_out, v_norm2_w, v_w_up, v_ffn_conv_w, v_ffn_conv_b, v_w_down, v_final_norm_w):
    w = dict(rel_bias=rel_bias, norm1_w=norm1_w, w_in=w_in, conv_w=conv_w, conv_b=conv_b, dt_bias=dt_bias, a_log=a_log,
             d_skip=d_skip, ssm_norm_w=ssm_norm_w, attn_sink=attn_sink, w_out=w_out, norm2_w=norm2_w, w_up=w_up,
             ffn_conv_w=ffn_conv_w, ffn_conv_b=ffn_conv_b, w_down=w_down, final_norm_w=final_norm_w)
    m = dict(rel_bias=m_rel_bias, norm1_w=m_norm1_w, w_in=m_w_in, conv_w=m_conv_w, conv_b=m_conv_b, dt_bias=m_dt_bias,
             a_log=m_a_log, d_skip=m_d_skip, ssm_norm_w=m_ssm_norm_w, attn_sink=m_attn_sink, w_out=m_w_out, norm2_w=m_norm2_w,
             w_up=m_w_up, ffn_conv_w=m_ffn_conv_w, ffn_conv_b=m_ffn_conv_b, w_down=m_w_down, final_norm_w=m_final_norm_w)
    v = dict(rel_bias=v_rel_bias, norm1_w=v_norm1_w, w_in=v_w_in, conv_w=v_conv_w, conv_b=v_conv_b, dt_bias=v_dt_bias,
             a_log=v_a_log, d_skip=v_d_skip, ssm_norm_w=v_ssm_norm_w, attn_sink=v_attn_sink, w_out=v_w_out, norm2_w=v_norm2_w,
             w_up=v_w_up, ffn_conv_w=v_ffn_conv_w, ffn_conv_b=v_ffn_conv_b, w_down=v_w_down, final_norm_w=v_final_norm_w)
    my_chip = 2 * lax.axis_index("x") + lax.axis_index("y")

    shards = {k: w[k].astype(BF16) for k in _BIG}
    received = {}

    def gather(spec):
        return _gather_rider([shards[k][i] for i, k in spec])

    def weights(spec, carried):
        out = []
        for (i, k), g_ in zip(spec, carried):
            full = _join_chips(lax.dynamic_update_index_in_dim(g_, shards[k][i], my_chip, 0), _BIG_AXIS[k] - 1)
            out.append(_perm_in_cols(full) if k == "w_in" else full)
        return out

    def scatter(spec, grads):
        layer = spec[0][0]
        bufs = [_split_by_chip(g_, _BIG_AXIS[k] - 1).astype(BF16) for (_, k), g_ in zip(spec, grads)]
        prev = [received[(layer % 2, k)] for _, k in spec] if layer + 2 < DEPTH else []
        return _scatter_rider(bufs, layer, prev)

    def collect(spec, carried):
        for (i, k), pieces in zip(spec, carried):
            received[(i % 2, k)] = pieces

    conv_shapes = [(DEPTH, SSM_CONV, CONV_CH), (DEPTH, FFN_CONV, D_FF)]
    placed = [lax.dynamic_update_slice_in_dim(jnp.zeros(s, F32), w[k], my_chip * w[k].shape[2], axis=2)
              for k, s in zip(_SMALL_SHARDED, conv_shapes)]
    lead = (lax.axis_index("c") == 0).astype(F32)
    conv_full = _unpack(_allreduce_small(_pack([p * lead for p in placed]), name="gather_conv_weights"), conv_shapes)
    small = {k: w[k] for k in _SMALL}
    small["conv_w"], small["ffn_conv_w"] = conv_full

    loss_part, grad_x, gp = _local_step(x, loss_target, small,
                                        exchange=dict(gather=gather, weights=weights, scatter=scatter, collect=collect))

    small_shapes = [small[k].shape for k in _SMALL] + [()]
    red = _unpack(_allreduce_small(_pack([gp[k] for k in _SMALL] + [loss_part[0, :1]]), name="reduce_small"), small_shapes)
    gsmall = dict(zip(_SMALL, red[:-1]))
    loss = red[-1]
    for k in _SMALL_SHARDED:
        n = w[k].shape[2]
        gsmall[k] = lax.dynamic_slice_in_dim(gsmall[k], my_chip * n, n, axis=2)

    core = lax.axis_index("c")
    half = DEPTH // 2
    halves = [jnp.stack([_sum_sources(received[(p, k)], name=f"sum_{k}_{p}") for p in range(half)]) for k in _BIG]
    others = _join_halves(halves, name="join_halves")
    gbig = {}
    for k, mine_, theirs_ in zip(_BIG, halves, others):
        full = jnp.zeros((DEPTH,) + mine_.shape[1:], F32)
        full = lax.dynamic_update_slice_in_dim(full, mine_, core * half, axis=0)
        gbig[k] = lax.dynamic_update_slice_in_dim(full, theirs_, (1 - core) * half, axis=0)

    grad, delta, new_m, new_v = {}, {}, {}, {}
    for k in _BIG:
        shp = w[k].shape
        two = lambda a: a.reshape(shp[0] * shp[1], shp[2])
        d_, m_, v_ = _adamw(two(w[k]), two(gbig[k]), two(m[k]), two(v[k]), name=f"adamw_{k}")
        grad[k], delta[k], new_m[k], new_v[k] = gbig[k], d_.reshape(shp), m_.reshape(shp), v_.reshape(shp)
    shapes = [w[k].shape for k in _SMALL]
    d_, m_, v_ = _adamw(_pack([w[k] for k in _SMALL]), _pack([gsmall[k] for k in _SMALL]), _pack([m[k] for k in _SMALL]),
                        _pack([v[k] for k in _SMALL]), name="adamw_small")
    for k, a, b_, c_ in zip(_SMALL, _unpack(d_, shapes), _unpack(m_, shapes), _unpack(v_, shapes)):
        grad[k], delta[k], new_m[k], new_v[k] = gsmall[k], a, b_, c_
    return (loss, grad_x, *[grad[k] for k in _ORDER], *[delta[k] for k in _ORDER], *[new_m[k] for k in _ORDER],
            *[new_v[k] for k in _ORDER])
```

```python
import functools
import math

import jax
import jax.numpy as jnp
import numpy as np
from jax import lax
from jax.experimental import pallas as pl
from jax.experimental.pallas import tpu as pltpu

F32 = jnp.float32
BF16 = jnp.bfloat16
BS = pl.BlockSpec
SDS = jax.ShapeDtypeStruct
MESH = pl.DeviceIdType.MESH

D_MODEL = 1024
DEPTH = 4
SSM_HEADS = 16
SSM_WIDTH = 1024
BC_WIDTH = 256
CONV_CH = 1536
SSM_CONV = 7
CHUNK = 128
ATTN_HEADS = 16
KV_HEADS = 4
HEAD_DIM = 64
WINDOW = 128
BLOCK = 128
KEY_SPAN = 384
REL_BUCKETS = 32
REL_MAX_DIST = 128
D_FF = 2816
FFN_CONV = 3
NORM_EPS = 1e-6
Z_END = 1024
XBC_END = 2560
DT_END = 2592
Q_END = 3616
K_END = 3872
IN_COLS = 4128
P_COLS = 4224
ADAM_LR, ADAM_B1, ADAM_B2, ADAM_EPS, ADAM_WD, ADAM_STEP = 0.001, 0.9, 0.999, 1e-08, 0.01, 10
NEG = -1e30
N_DEV = 8
N_CHIP = 4
LANE = 128
VMEM_LIMIT_BYTES = 48 * 1024 * 1024


def _pc(body, *, name, grid, in_specs, out_specs, out_shape, scratch_shapes=(), aliases=None):
    return pl.pallas_call(
        body, name=name, grid=grid, in_specs=in_specs, out_specs=out_specs, out_shape=out_shape,
        scratch_shapes=list(scratch_shapes), input_output_aliases=aliases or {},
        compiler_params=pltpu.CompilerParams(dimension_semantics=("arbitrary",) * len(grid),
                                             vmem_limit_bytes=VMEM_LIMIT_BYTES))


def _split_rider_refs(refs, n_in, n_out, rider):
    n_rin = len(rider["ins"]) + len(rider["prev"])
    n_rout = len(rider["out_shape"])
    core = refs[:n_in] + refs[n_in + n_rin:n_in + n_rin + n_out] + refs[n_in + n_rin + n_out + n_rout + len(rider["scratch"]):]
    rins = refs[n_in:n_in + len(rider["ins"])]
    routs = refs[n_in + n_rin + n_out:n_in + n_rin + n_out + n_rout]
    sems = refs[n_in + n_rin + n_out + n_rout:n_in + n_rin + n_out + n_rout + len(rider["scratch"])]
    return core, rins, routs, sems


def _pc_carry(body, args, *, name, grid, in_specs, out_specs, out_shape, scratch_shapes=(), rider=None, aliases=None):
    if rider is None:
        return _pc(body, name=name, grid=grid, in_specs=in_specs, out_specs=out_specs, out_shape=out_shape,
                   scratch_shapes=scratch_shapes, aliases=aliases)(*args), None
    n_in, n_out = len(in_specs), len(out_shape)
    any_spec = BS(memory_space=pl.ANY)

    def full(*refs):
        core, rins, routs, sems = _split_rider_refs(refs, n_in, n_out, rider)
        ids = [pl.program_id(d) for d in range(len(grid))]
        first = functools.reduce(jnp.logical_and, [i == 0 for i in ids])
        last = functools.reduce(jnp.logical_and, [i == g - 1 for i, g in zip(ids, grid)])

        @pl.when(first)
        def _():
            rider["start"](rins, routs, sems)

        body(*core)

        @pl.when(last)
        def _():
            rider["finish"](rins, routs, sems)

    n_rin = len(rider["ins"])
    outs = pl.pallas_call(
        full, name=name, grid=grid,
        in_specs=list(in_specs) + [any_spec] * (n_rin + len(rider["prev"])),
        out_specs=list(out_specs) + [any_spec] * len(rider["out_shape"]),
        out_shape=list(out_shape) + list(rider["out_shape"]),
        scratch_shapes=list(rider["scratch"]) + list(scratch_shapes),
        input_output_aliases={**(aliases or {}), **{n_in + n_rin + t: n_out + t for t in range(len(rider["prev"]))}},
        compiler_params=pltpu.CompilerParams(dimension_semantics=("arbitrary",) * len(grid), vmem_limit_bytes=VMEM_LIMIT_BYTES,
                                             has_side_effects=True))(*args, *rider["ins"], *rider["prev"])
    return outs[:n_out], outs[n_out:]


def _run_rider(rider, *, name):
    any_spec = BS(memory_space=pl.ANY)
    n_rin = len(rider["ins"])

    def body(*refs):
        _, rins, routs, sems = _split_rider_refs(refs, 0, 0, rider)
        rider["start"](rins, routs, sems)
        rider["finish"](rins, routs, sems)

    return pl.pallas_call(
        body, name=name, in_specs=[any_spec] * (n_rin + len(rider["prev"])), out_specs=[any_spec] * len(rider["out_shape"]),
        out_shape=list(rider["out_shape"]), scratch_shapes=list(rider["scratch"]),
        input_output_aliases={n_rin + t: t for t in range(len(rider["prev"]))},
        compiler_params=pltpu.CompilerParams(has_side_effects=True))(*rider["ins"], *rider["prev"])


def _div_tile(n, pref, mult):
    t = min(pref, n)
    t -= t % mult
    while t >= mult:
        if n % t == 0:
            return t
        t -= mult
    return n


def _mm(a, b, *, name, ta=False, tb=False, add=None, out_dtype=F32, tm=1024, tn=1024, tk=1024):
    if ta:
        K, M = a.shape
    else:
        M, K = a.shape
    N = b.shape[0] if tb else b.shape[1]
    tm, tn, tk = _div_tile(M, tm, LANE), _div_tile(N, tn, LANE), _div_tile(K, tk, LANE)
    nk = K // tk
    dims = (((0,) if ta else (1,), (1,) if tb else (0,)), ((), ()))

    def body_single(*refs):
        r = lax.dot_general(refs[0][...], refs[1][...], dims, preferred_element_type=F32)
        if add is not None:
            r = r + refs[2][...]
        refs[-1][...] = r.astype(out_dtype)

    def body(*refs):
        if add is None:
            a_ref, b_ref, o_ref, acc_ref = refs
        else:
            a_ref, b_ref, add_ref, o_ref, acc_ref = refs
        k = pl.program_id(2)

        @pl.when(k == 0)
        def _():
            acc_ref[...] = jnp.zeros_like(acc_ref)

        acc_ref[...] += lax.dot_general(a_ref[...], b_ref[...], dims, preferred_element_type=F32)

        @pl.when(k == nk - 1)
        def _():
            r = acc_ref[...]
            if add is not None:
                r = r + add_ref[...]
            o_ref[...] = r.astype(out_dtype)

    a_spec = BS((tk, tm), lambda i, j, k: (k, i)) if ta else BS((tm, tk), lambda i, j, k: (i, k))
    b_spec = BS((tn, tk), lambda i, j, k: (j, k)) if tb else BS((tk, tn), lambda i, j, k: (k, j))
    in_specs, args = [a_spec, b_spec], [a, b]
    if add is not None:
        in_specs.append(BS((tm, tn), lambda i, j, k: (i, j)))
        args.append(add)
    return _pc(body_single if nk == 1 else body, name=name, grid=(M // tm, N // tn, nk), in_specs=in_specs,
               out_specs=BS((tm, tn), lambda i, j, k: (i, j)), out_shape=SDS((M, N), out_dtype),
               scratch_shapes=[] if nk == 1 else [pltpu.VMEM((tm, tn), F32)])(*args)


def _dot(a, b, dims):
    return lax.dot_general(a.astype(BF16), b.astype(BF16), (dims, ((), ())), preferred_element_type=F32)


@jax.custom_vjp
def _nn(a, b):
    return _dot(a, b, ((1,), (0,)))


@jax.custom_vjp
def _nt(a, b):
    return _dot(a, b, ((1,), (1,)))


@jax.custom_vjp
def _tn(a, b):
    return _dot(a, b, ((0,), (0,)))


_nn.defvjp(lambda a, b: (_nn(a, b), (a, b)), lambda r, g: (_nt(g, r[1]), _tn(r[0], g)))
_nt.defvjp(lambda a, b: (_nt(a, b), (a, b)), lambda r, g: (_nn(g, r[1]), _tn(g, r[0])))
_tn.defvjp(lambda a, b: (_tn(a, b), (a, b)), lambda r, g: (_nt(r[1], g), _nn(r[0], g)))


def _hdot(m, x):
    hi = x.astype(BF16)
    r1 = x - hi.astype(F32)
    lo = r1.astype(BF16)
    lo2 = (r1 - lo.astype(F32)).astype(BF16)
    n = x.shape[1]
    out = lax.dot_general(m.astype(BF16), jnp.concatenate([hi, lo, lo2], axis=1), (((1,), (0,)), ((), ())),
                          preferred_element_type=F32)
    return out[:, :n] + out[:, n:2 * n] + out[:, 2 * n:]


@jax.custom_vjp
def _cumdot(m, mt, x):
    return _hdot(m, x)


_cumdot.defvjp(lambda m, mt, x: (_hdot(m, x), (m, mt)),
               lambda r, g: (jnp.zeros_like(r[0]), jnp.zeros_like(r[1]), _hdot(r[1], g)))


def _sigmoid(x):
    return 1.0 / (1.0 + jnp.exp(-x))


def _softplus(x):
    return jnp.maximum(x, 0.0) + jnp.log(1.0 + jnp.exp(-jnp.abs(x)))


def _rms(x, w):
    return x * lax.rsqrt(jnp.mean(x * x, axis=-1, keepdims=True) + NORM_EPS) * w


def _norm_mm(x2, nw, b, *, name, out_dtype=F32, tm=1024, tn=1024):
    T, D = x2.shape
    N = b.shape[1]
    tm, tn = _div_tile(T, tm, LANE), _div_tile(N, tn, LANE)

    def body(x_ref, w_ref, b_ref, h_ref, o_ref):
        @pl.when(pl.program_id(1) == 0)
        def _():
            h_ref[...] = _rms(x_ref[...], w_ref[...]).astype(BF16)

        o_ref[...] = lax.dot_general(h_ref[...], b_ref[...], (((1,), (0,)), ((), ())), preferred_element_type=F32).astype(out_dtype)

    return _pc(body, name=name, grid=(T // tm, N // tn),
               in_specs=[BS((tm, D), lambda i, j: (i, 0)), BS((1, D), lambda i, j: (0, 0)), BS((D, tn), lambda i, j: (0, j))],
               out_specs=[BS((tm, D), lambda i, j: (i, 0)), BS((tm, tn), lambda i, j: (i, j))],
               out_shape=[SDS((T, D), BF16), SDS((T, N), out_dtype)])(x2, nw.reshape(1, D), b)


def _dnorm(dh, x2, nw, resid, *, name):
    T, D = x2.shape
    tr = _div_tile(T, 512, 16)

    def body(x_ref, w_ref, dh_ref, r_ref, dx_ref, dxb_ref, dw_ref):
        _, vjp = jax.vjp(_rms, x_ref[...], w_ref[...])
        dx, dw = vjp(dh_ref[...])
        dx = dx + r_ref[...]
        dx_ref[...] = dx
        dxb_ref[...] = dx.astype(BF16)

        @pl.when(pl.program_id(0) == 0)
        def _():
            dw_ref[...] = jnp.zeros_like(dw_ref)

        dw_ref[...] += dw

    row = BS((tr, D), lambda i: (i, 0))
    one = BS((1, D), lambda i: (0, 0))
    return _pc(body, name=name, grid=(T // tr,), in_specs=[row, one, row, row], out_specs=[row, row, one],
               out_shape=[SDS((T, D), F32), SDS((T, D), BF16), SDS((1, D), F32)])(x2, nw.reshape(1, D), dh, resid)


ROW_PAD = 8


def _pad_rows(x):
    return jnp.concatenate([x, jnp.zeros((ROW_PAD, x.shape[1]), x.dtype)], axis=0)


def _shift_rows(xp, s):
    n = xp.shape[0] - ROW_PAD
    return xp[:n] if s == 0 else pltpu.roll(xp, (-s) % xp.shape[0], 0)[:n]


def _conv_taps(x, taps):
    xp = _pad_rows(x)
    return [_shift_rows(xp, k - taps // 2) for k in range(taps)]


def _conv_pre(xs, w_ref, b_ref):
    c = b_ref[...] + w_ref[0:1, :] * xs[0]
    for k in range(1, len(xs)):
        c = c + w_ref[k:k + 1, :] * xs[k]
    return c


def _conv_fwd(x3, x_blk0, w, b, *, taps, ct, gate_blk0=None, out_dtype, name):
    B, L, _ = x3.shape
    C = w.shape[1]
    wp = jnp.zeros((8, C), F32).at[:taps].set(w)

    def body(*refs):
        if gate_blk0 is None:
            x_ref, w_ref, b_ref, o_ref = refs
        else:
            x_ref, u_ref, w_ref, b_ref, o_ref = refs
        c = _conv_pre(_conv_taps(x_ref[0].astype(F32), taps), w_ref, b_ref)
        y = c * _sigmoid(c)
        if gate_blk0 is not None:
            y = y * u_ref[0].astype(F32)
        o_ref[0] = y.astype(out_dtype)

    in_specs = [BS((1, L, ct), lambda bi, j: (bi, 0, x_blk0 + j))]
    args = [x3]
    if gate_blk0 is not None:
        in_specs.append(BS((1, L, ct), lambda bi, j: (bi, 0, gate_blk0 + j)))
        args.append(x3)
    in_specs += [BS((8, ct), lambda bi, j: (0, j)), BS((1, ct), lambda bi, j: (0, j))]
    args += [wp, b.reshape(1, C)]
    return _pc(body, name=name, grid=(B, C // ct), in_specs=in_specs,
               out_specs=BS((1, L, ct), lambda bi, j: (bi, 0, j)), out_shape=SDS((B, L, C), out_dtype))(*args)


def _conv_bwd(x3, x_blk0, w, b, dy3, *, taps, ct, gate_blk0=None, name):
    B, L, _ = x3.shape
    C = w.shape[1]
    wp = jnp.zeros((8, C), F32).at[:taps].set(w)
    gated = gate_blk0 is not None

    def body(*refs):
        if gated:
            x_ref, u_ref, w_ref, b_ref, dy_ref, dx_ref, du_ref, dw_ref, db_ref = refs
        else:
            x_ref, w_ref, b_ref, dy_ref, dx_ref, dw_ref, db_ref = refs
        xs = _conv_taps(x_ref[0].astype(F32), taps)
        dy = dy_ref[0].astype(F32)
        c = _conv_pre(xs, w_ref, b_ref)
        sg = _sigmoid(c)
        dsilu = sg * (1.0 + c * (1.0 - sg))
        if gated:
            du_ref[0] = (dy * (c * sg)).astype(BF16)
            dc = dy * u_ref[0].astype(F32) * dsilu
        else:
            dc = dy * dsilu
        dcp = _pad_rows(dc)
        dx = jnp.zeros_like(dc)
        dw_ref[0] = jnp.zeros((8, ct), F32)
        for k in range(taps):
            dx = dx + w_ref[k:k + 1, :] * _shift_rows(dcp, taps // 2 - k)
            dw_ref[0, k:k + 1, :] = jnp.sum(dc * xs[k], axis=0, keepdims=True)
        dx_ref[0] = dx.astype(BF16)
        db_ref[0] = jnp.sum(dc, axis=0, keepdims=True)

    xs = BS((1, L, ct), lambda bi, j: (bi, 0, x_blk0 + j))
    ys = BS((1, L, ct), lambda bi, j: (bi, 0, j))
    in_specs, args = [xs], [x3]
    if gated:
        in_specs.append(BS((1, L, ct), lambda bi, j: (bi, 0, gate_blk0 + j)))
        args.append(x3)
    in_specs += [BS((8, ct), lambda bi, j: (0, j)), BS((1, ct), lambda bi, j: (0, j)), ys]
    args += [wp, b.reshape(1, C), dy3]
    out_specs = [ys] + ([ys] if gated else []) + [BS((1, 8, ct), lambda bi, j: (bi, 0, j)), BS((1, 1, ct), lambda bi, j: (bi, 0, j))]
    out_shape = [SDS((B, L, C), BF16)] + ([SDS((B, L, C), BF16)] if gated else []) + [SDS((B, 8, C), F32), SDS((B, 1, C), F32)]
    return _pc(body, name=name, grid=(B, C // ct), in_specs=in_specs, out_specs=out_specs, out_shape=out_shape)(*args)


def _tri(reverse):
    r = lax.broadcasted_iota(jnp.int32, (CHUNK, CHUNK), 0)
    c = lax.broadcasted_iota(jnp.int32, (CHUNK, CHUNK), 1)
    return (c >= r) if reverse else (c <= r)


PAIRS = 2
QUADS = SSM_HEADS // (2 * PAIRS)
QW = PAIRS * LANE


def _ssd_chunk(h0, h1, x0, x1, bm, cm, dtc, alog, *, col0, reverse):
    mask = _tri(reverse)
    eye = lax.broadcasted_iota(jnp.int32, (CHUNK, CHUNK), 0) == lax.broadcasted_iota(jnp.int32, (CHUNK, CHUNK), 1)
    lane = lax.broadcasted_iota(jnp.int32, (1, LANE), 1)
    first = lane < HEAD_DIM
    adt = dtc * (-jnp.exp(alog))
    cumc = _cumdot(mask.astype(F32), _tri(not reverse).astype(F32), adt)
    totc = jnp.sum(adt, axis=0, keepdims=True)
    cb = _nt(cm, bm)

    def col(v, c):
        return jnp.sum(jnp.where(lane == c, v, 0.0), axis=1, keepdims=True)

    outs, states = [], []
    for p, (hprev, xs) in enumerate(((h0, x0), (h1, x1))):
        c0 = col0 + 2 * p
        cj = (col(cumc, c0), col(cumc, c0 + 1))
        cum = jnp.where(first, cj[0], cj[1])
        tot = jnp.where(first, col(totc, c0), col(totc, c0 + 1))
        xdt = xs * jnp.where(first, col(dtc, c0), col(dtc, c0 + 1))
        y = _nn(cm, hprev) * jnp.exp(cum)
        for j in range(2):
            rj = jnp.sum(jnp.where(eye, cj[j], 0.0), axis=0, keepdims=True)
            dec = jnp.exp(jnp.where(mask, cj[j] - rj, NEG))
            y = y + _nn(cb * dec, jnp.where(first if j == 0 else ~first, xdt, 0.0))
        outs.append(y)
        states.append(hprev * jnp.exp(tot) + _tn(bm, xdt * jnp.exp(tot - cum)))
    return outs[0], outs[1], states[0], states[1]


def _ssd_specs(B, L):
    def lanes(w, blk):
        return BS((1, L, w), blk)

    return [
        lanes(QW, lambda b, q: (b, 0, q)),
        lanes(LANE, lambda b, q: (b, 0, 8 + q // 2)),
        lanes(LANE, lambda b, q: (b, 0, 10 + q // 2)),
        lanes(LANE, lambda b, q: (b, 0, 0)),
        BS((1, LANE), lambda b, q: (0, 0)),
        BS((1, QW), lambda b, q: (0, q)),
    ]


def _ssd_slot(d, ci):
    return ci if d == 0 else ci + 1


def _ssd_fwd(xbc_act, dtc, alog, dskip, *, name, rider=None):
    B, L, _ = xbc_act.shape
    nc = L // CHUNK

    def body(xs_ref, b_ref, c_ref, dt_ref, alog_ref, dsk_ref, y_ref, hs_ref):
        q = pl.program_id(1)
        alog_v = alog_ref[...]
        y_ref[0] = dsk_ref[...] * xs_ref[0]
        hs_ref[0, 0, 0, 0] = jnp.zeros((LANE, QW), F32)
        hs_ref[0, 0, 1, nc] = jnp.zeros((LANE, QW), F32)

        def step(i, carry):
            cis = (i, nc - 1 - i)
            rows = [pl.ds(pl.multiple_of(ci * CHUNK, CHUNK), CHUNK) for ci in cis]
            res = []
            for d in range(2):
                cur = _ssd_slot(d, cis[d])
                res.append(_ssd_chunk(
                    hs_ref[0, 0, d, cur, :, :LANE], hs_ref[0, 0, d, cur, :, LANE:], xs_ref[0, rows[d], :LANE],
                    xs_ref[0, rows[d], LANE:], b_ref[0, rows[d], :], c_ref[0, rows[d], :], dt_ref[0, rows[d], :], alog_v,
                    col0=SSM_HEADS * d + 2 * PAIRS * q, reverse=d == 1))
            for d in range(2):
                y0, y1, n0, n1 = res[d]
                nxt = _ssd_slot(d, cis[d] + 1 if d == 0 else cis[d] - 1)
                hs_ref[0, 0, d, nxt, :, :LANE] = n0
                hs_ref[0, 0, d, nxt, :, LANE:] = n1
                y_ref[0, rows[d], :LANE] += y0
                y_ref[0, rows[d], LANE:] += y1
            return carry

        lax.fori_loop(0, nc, step, 0, unroll=2)

    (y, hs), carried = _pc_carry(
        body, (xbc_act, xbc_act, xbc_act, dtc, alog, dskip), name=name, grid=(B, QUADS), in_specs=_ssd_specs(B, L),
        out_specs=[BS((1, L, QW), lambda b, q: (b, 0, q)), BS((1, 1, 2, nc + 1, LANE, QW), lambda b, q: (b, q, 0, 0, 0, 0))],
        out_shape=[SDS((B, L, SSM_WIDTH), F32), SDS((B, QUADS, 2, nc + 1, LANE, QW), F32)], rider=rider)
    return y, hs, carried


def _ssd_bwd(xbc_act, dtc, alog, dskip, hs, dy, *, name, rider=None):
    B, L, _ = xbc_act.shape
    nc = L // CHUNK

    def body(xs_ref, b_ref, c_ref, dt_ref, alog_ref, dsk_ref, hs_ref, dy_ref,
             dxs_ref, db_ref, dc_ref, ddt_ref, dalog_ref, ddsk_ref, dh_ref):
        q = pl.program_id(1)
        alog_v = alog_ref[...]

        @pl.when(q % 2 == 0)
        def _():
            db_ref[...] = jnp.zeros_like(db_ref)
            dc_ref[...] = jnp.zeros_like(dc_ref)

        @pl.when(q == 0)
        def _():
            ddt_ref[...] = jnp.zeros_like(ddt_ref)

        dxs_ref[0] = dy_ref[0] * dsk_ref[...]
        ddsk_ref[0] = jnp.sum(dy_ref[0] * xs_ref[0], axis=0, keepdims=True)
        dh_ref[...] = jnp.zeros_like(dh_ref)

        def step(i, carry):
            g_alog = carry
            cis = (nc - 1 - i, i)
            rows = [pl.ds(pl.multiple_of(ci * CHUNK, CHUNK), CHUNK) for ci in cis]
            res = []
            for d in range(2):
                cur = _ssd_slot(d, cis[d])
                fn = functools.partial(_ssd_chunk, col0=SSM_HEADS * d + 2 * PAIRS * q, reverse=d == 1)
                _, vjp = jax.vjp(fn, hs_ref[0, 0, d, cur, :, :LANE], hs_ref[0, 0, d, cur, :, LANE:], xs_ref[0, rows[d], :LANE],
                                 xs_ref[0, rows[d], LANE:], b_ref[0, rows[d], :], c_ref[0, rows[d], :], dt_ref[0, rows[d], :],
                                 alog_v)
                res.append(vjp((dy_ref[0, rows[d], :LANE], dy_ref[0, rows[d], LANE:], dh_ref[d, :, :LANE], dh_ref[d, :, LANE:])))
            for d in range(2):
                g_h0, g_h1, g_x0, g_x1, g_b, g_c, g_dt, g_alog1 = res[d]
                dh_ref[d, :, :LANE] = g_h0
                dh_ref[d, :, LANE:] = g_h1
                dxs_ref[0, rows[d], :LANE] += g_x0
                dxs_ref[0, rows[d], LANE:] += g_x1
                db_ref[0, rows[d], :] += g_b
                dc_ref[0, rows[d], :] += g_c
                ddt_ref[0, rows[d], :] += g_dt
                g_alog = g_alog + g_alog1
            return g_alog

        dalog_ref[0, 0] = lax.fori_loop(0, nc, step, jnp.zeros((1, LANE), F32))

    lanes = lambda w, blk: BS((1, L, w), blk)
    in_specs = _ssd_specs(B, L) + [BS((1, 1, 2, nc + 1, LANE, QW), lambda b, q: (b, q, 0, 0, 0, 0)), lanes(QW, lambda b, q: (b, 0, q))]
    out_specs = [lanes(QW, lambda b, q: (b, 0, q)), lanes(LANE, lambda b, q: (b, 0, q // 2)), lanes(LANE, lambda b, q: (b, 0, q // 2)),
                 lanes(LANE, lambda b, q: (b, 0, 0)), BS((1, 1, 1, LANE), lambda b, q: (b, q, 0, 0)),
                 BS((1, 1, QW), lambda b, q: (b, 0, q))]
    out_shape = [SDS((B, L, SSM_WIDTH), F32), SDS((B, L, BC_WIDTH), F32), SDS((B, L, BC_WIDTH), F32), SDS((B, L, LANE), F32),
                 SDS((B, QUADS, 1, LANE), F32), SDS((B, 1, SSM_WIDTH), F32)]
    outs, carried = _pc_carry(body, (xbc_act, xbc_act, xbc_act, dtc, alog, dskip, hs, dy), name=name, grid=(B, QUADS),
                              in_specs=in_specs, out_specs=out_specs, out_shape=out_shape,
                              scratch_shapes=[pltpu.VMEM((2, LANE, QW), F32)], rider=rider)
    return (*outs, carried)


def _gate_norm(yp, z, w):
    v = yp * (z * _sigmoid(z))
    return v * lax.rsqrt(jnp.mean(v * v, axis=-1, keepdims=True) + NORM_EPS) * w


def _gate_fwd(ypre2, proj2, w, *, name):
    T = ypre2.shape[0]
    tr = _div_tile(T, 512, 8)
    G = 512

    def body(y_ref, z_ref, w_ref, o_ref):
        o_ref[...] = _gate_norm(y_ref[...], z_ref[...], w_ref[...]).astype(BF16)

    return _pc(body, name=name, grid=(T // tr, 2),
               in_specs=[BS((tr, G), lambda i, g: (i, g)), BS((tr, G), lambda i, g: (i, 2 + g)), BS((1, G), lambda i, g: (0, g))],
               out_specs=BS((tr, G), lambda i, g: (i, g)), out_shape=SDS((T, 2 * SSM_WIDTH), BF16))(ypre2, proj2, w.reshape(1, -1))


def _gate_bwd(ypre2, proj2, w, dy, *, name):
    T = ypre2.shape[0]
    tr = _div_tile(T, 512, 8)
    G = 512

    def body(y_ref, z_ref, w_ref, dy_ref, dyp_ref, dz_ref, dw_ref):
        _, vjp = jax.vjp(_gate_norm, y_ref[...], z_ref[...], w_ref[...])
        dyp, dz, dw = vjp(dy_ref[...])
        dyp_ref[...] = dyp
        dz_ref[...] = dz.astype(BF16)
        dw_ref[0] = dw

    tile = BS((tr, G), lambda i, g: (i, g))
    return _pc(body, name=name, grid=(T // tr, 2),
               in_specs=[tile, BS((tr, G), lambda i, g: (i, 2 + g)), BS((1, G), lambda i, g: (0, g)), tile],
               out_specs=[tile, tile, BS((1, 1, G), lambda i, g: (i, 0, g))],
               out_shape=[SDS((T, SSM_WIDTH), F32), SDS((T, SSM_WIDTH), BF16), SDS((T // tr, 1, SSM_WIDTH), F32)])(
        ypre2, proj2, w.reshape(1, -1), dy)


def _first_half():
    return lax.broadcasted_iota(jnp.int32, (1, LANE), 1) < HEAD_DIM


def _dup_kv_head(pair, odd):
    rolled = pltpu.roll(pair, HEAD_DIM, 1)
    return jnp.where(_first_half(), rolled, pair) if odd else jnp.where(_first_half(), pair, rolled)


def _stack_heads(quad):
    first = _first_half()
    lo, hi = quad[:, :LANE], quad[:, LANE:]
    return jnp.concatenate([jnp.where(first, lo, 0.0), jnp.where(first, 0.0, lo), jnp.where(first, hi, 0.0),
                            jnp.where(first, 0.0, hi)], axis=0)


def _unstack_heads(o):
    first = _first_half()
    return jnp.concatenate([jnp.where(first, o[:BLOCK], o[BLOCK:2 * BLOCK]), jnp.where(first, o[2 * BLOCK:3 * BLOCK], o[3 * BLOCK:])], axis=1)


def _fold_kv_head(d, odd):
    tot = d + pltpu.roll(d, HEAD_DIM, 1)
    return jnp.where(_first_half(), 0.0, tot) if odd else jnp.where(_first_half(), tot, 0.0)


def _attn_softmax(s, sink):
    m = jnp.maximum(jnp.max(s, axis=-1, keepdims=True), sink)
    p = jnp.exp(s - m)
    ps = jnp.exp(sink - m)
    inv = 1.0 / (jnp.sum(p, axis=-1, keepdims=True) + ps)
    return p * inv, ps * inv


def _attn_colneg(n, L):
    kpos = n * BLOCK - WINDOW + lax.broadcasted_iota(jnp.int32, (1, KEY_SPAN), 1)
    return jnp.where((kpos >= 0) & (kpos < L), 0.0, NEG)


def _attn_in_specs(L):
    nblk = L // BLOCK
    kv = lambda o, col: BS((1, BLOCK, 4 * HEAD_DIM), lambda b, n: (b, jnp.clip(n + o, 0, nblk - 1), col))
    kcol, vcol = 3584 // 256, 3840 // 256
    return [BS((1, BLOCK, ATTN_HEADS * HEAD_DIM), lambda b, n: (b, n, 0)), kv(-1, kcol), kv(0, kcol), kv(1, kcol),
            kv(-1, vcol), kv(0, vcol), kv(1, vcol),
            BS((ATTN_HEADS, BLOCK, KEY_SPAN), lambda b, n: (0, 0, 0)), BS((ATTN_HEADS * BLOCK, 1), lambda b, n: (0, 0))]


def _attn_fwd(proj, bias, sinkcol, mixed, *, name, rider=None):
    B, L, _ = proj.shape

    def body(q_ref, k0, k1, k2, v0, v1, v2, bias_ref, sink_ref, _, o_ref):
        colneg = _attn_colneg(pl.program_id(1), L)
        kcat = jnp.concatenate([k0[0], k1[0], k2[0]], axis=0)
        vcat = jnp.concatenate([v0[0], v1[0], v2[0]], axis=0)
        scores, probs, scales = [], [], []
        for g in range(KV_HEADS):
            pair = slice(LANE * (g // 2), LANE * (g // 2) + LANE)
            quad = slice(4 * HEAD_DIM * g, 4 * HEAD_DIM * (g + 1))
            kd = _dup_kv_head(kcat[:, pair], g % 2).astype(BF16)
            qs = (_stack_heads(q_ref[0, :, quad]) * HEAD_DIM ** -0.5).astype(BF16)
            scores.append(lax.dot_general(qs, kd, (((1,), (1,)), ((), ())), preferred_element_type=F32))
        for g in range(KV_HEADS):
            s = scores[g] + bias_ref[4 * g:4 * g + 4].reshape(4 * BLOCK, KEY_SPAN) + colneg
            sink = sink_ref[4 * BLOCK * g:4 * BLOCK * (g + 1)]
            m = jnp.maximum(jnp.max(s, axis=-1, keepdims=True), sink)
            p = jnp.exp(s - m)
            scales.append(1.0 / (jnp.sum(p, axis=-1, keepdims=True) + jnp.exp(sink - m)))
            probs.append(p.astype(BF16))
        for g in range(KV_HEADS):
            pair = slice(LANE * (g // 2), LANE * (g // 2) + LANE)
            quad = slice(4 * HEAD_DIM * g, 4 * HEAD_DIM * (g + 1))
            vd = _dup_kv_head(vcat[:, pair], g % 2).astype(BF16)
            o = lax.dot_general(probs[g], vd, (((1,), (0,)), ((), ())), preferred_element_type=F32) * scales[g]
            o_ref[0, :, quad] = _unstack_heads(o).astype(BF16)

    (out,), carried = _pc_carry(body, (proj, proj, proj, proj, proj, proj, proj, bias, sinkcol, mixed), name=name,
                                grid=(B, L // BLOCK), in_specs=_attn_in_specs(L) + [BS(memory_space=pl.ANY)],
                                out_specs=[BS((1, BLOCK, ATTN_HEADS * HEAD_DIM), lambda b, n: (b, n, 1))],
                                out_shape=[SDS(mixed.shape, BF16)], rider=rider, aliases={9: 0})
    return out, carried


def _attn_bwd(proj, bias, sinkcol, dout, *, name, rider=None):
    B, L, _ = proj.shape
    nblk = L // BLOCK
    nn, nt, tn = (((1,), (0,)), ((), ())), (((1,), (1,)), ((), ())), (((0,), (0,)), ((), ()))

    def body(q_ref, k0, k1, k2, v0, v1, v2, bias_ref, sink_ref, do_ref, dq_ref, dk_ref, dv_ref, dbias_ref, dsink_ref):
        b, n = pl.program_id(0), pl.program_id(1)

        @pl.when(n == 0)
        def _():
            dk_ref[...] = jnp.zeros_like(dk_ref)
            dv_ref[...] = jnp.zeros_like(dv_ref)

        @pl.when((n == 0) & (b == 0))
        def _():
            dbias_ref[...] = jnp.zeros_like(dbias_ref)
            dsink_ref[...] = jnp.zeros_like(dsink_ref)

        colneg = _attn_colneg(n, L)
        kcat = jnp.concatenate([k0[0], k1[0], k2[0]], axis=0)
        vcat = jnp.concatenate([v0[0], v1[0], v2[0]], axis=0)
        krows = [pl.ds(pl.multiple_of(jnp.clip(n + o, 0, nblk - 1) * BLOCK, BLOCK), BLOCK) for o in (-1, 0, 1)]
        ops, mids = [], []
        for g in range(KV_HEADS):
            pair = slice(LANE * (g // 2), LANE * (g // 2) + LANE)
            quad = slice(4 * HEAD_DIM * g, 4 * HEAD_DIM * (g + 1))
            kd = _dup_kv_head(kcat[:, pair], g % 2).astype(BF16)
            vd = _dup_kv_head(vcat[:, pair], g % 2).astype(BF16)
            qs = (_stack_heads(q_ref[0, :, quad]) * HEAD_DIM ** -0.5).astype(BF16)
            dos = _stack_heads(do_ref[0, :, quad].astype(F32)).astype(BF16)
            ops.append((kd, qs, dos, lax.dot_general(qs, kd, nt, preferred_element_type=F32),
                        lax.dot_general(dos, vd, nt, preferred_element_type=F32)))
        for g in range(KV_HEADS):
            rows = slice(4 * BLOCK * g, 4 * BLOCK * (g + 1))
            _, _, _, s, dpn = ops[g]
            pn, psink = _attn_softmax(s + bias_ref[4 * g:4 * g + 4].reshape(4 * BLOCK, KEY_SPAN) + colneg, sink_ref[rows])
            r = jnp.sum(dpn * pn, axis=-1, keepdims=True)
            ds = pn * (dpn - r)
            dbias_ref[4 * g:4 * g + 4] += ds.reshape(4, BLOCK, KEY_SPAN)
            dsink_ref[rows] += -psink * r
            mids.append((pn.astype(BF16), ds.astype(BF16)))
        for g in range(KV_HEADS):
            pair = slice(LANE * (g // 2), LANE * (g // 2) + LANE)
            quad = slice(4 * HEAD_DIM * g, 4 * HEAD_DIM * (g + 1))
            kd, qs, dos, _, _ = ops[g]
            pnb, dsb = mids[g]
            dvd = lax.dot_general(pnb, dos, tn, preferred_element_type=F32)
            dkd = lax.dot_general(dsb, qs, tn, preferred_element_type=F32)
            dqs = lax.dot_general(dsb, kd, nn, preferred_element_type=F32) * HEAD_DIM ** -0.5
            dq_ref[0, :, quad] = _unstack_heads(dqs).astype(BF16)
            dk_g, dv_g = _fold_kv_head(dkd, g % 2), _fold_kv_head(dvd, g % 2)
            for o in range(3):
                dk_ref[0, krows[o], pair] += dk_g[o * BLOCK:(o + 1) * BLOCK]
                dv_ref[0, krows[o], pair] += dv_g[o * BLOCK:(o + 1) * BLOCK]

    qspec = BS((1, BLOCK, ATTN_HEADS * HEAD_DIM), lambda b, n: (b, n, 0))
    kvout = BS((1, L, 4 * HEAD_DIM), lambda b, n: (b, 0, 0))
    outs, carried = _pc_carry(
        body, (proj, proj, proj, proj, proj, proj, proj, bias, sinkcol, dout), name=name, grid=(B, nblk),
        in_specs=_attn_in_specs(L) + [BS((1, BLOCK, ATTN_HEADS * HEAD_DIM), lambda b, n: (b, n, 1))],
        out_specs=[qspec, kvout, kvout, BS((ATTN_HEADS, BLOCK, KEY_SPAN), lambda b, n: (0, 0, 0)),
                   BS((ATTN_HEADS * BLOCK, 1), lambda b, n: (0, 0))],
        out_shape=[SDS((B, L, ATTN_HEADS * HEAD_DIM), BF16), SDS((B, L, 4 * HEAD_DIM), F32), SDS((B, L, 4 * HEAD_DIM), F32),
                   SDS((ATTN_HEADS, BLOCK, KEY_SPAN), F32), SDS((ATTN_HEADS * BLOCK, 1), F32)], rider=rider)
    return (*outs, carried)


def _t5_bucket(rel):
    half = REL_BUCKETS // 2
    max_exact = half // 2
    ret = jnp.where(rel > 0, half, 0)
    n = jnp.abs(rel)
    nf = jnp.maximum(n, 1).astype(F32)
    large = max_exact + (jnp.log(nf / max_exact) / math.log(REL_MAX_DIST / max_exact) * (half - max_exact)).astype(jnp.int32)
    large = jnp.minimum(large, half - 1)
    return ret + jnp.where(n < max_exact, n, large)


def _bucket_table():
    rel = jnp.arange(KEY_SPAN)[None, :] - WINDOW - jnp.arange(BLOCK)[:, None]
    return _t5_bucket(rel).astype(jnp.int32)


def _bias_expand(rel_bias, bucket, *, name):
    rbt = jnp.zeros((ATTN_HEADS, 1, LANE), F32).at[:, 0, :REL_BUCKETS].set(rel_bias.T)

    def body(rb_ref, bk_ref, o_ref):
        lane = lax.broadcasted_iota(jnp.int32, (1, LANE), 1)
        row = rb_ref[0]
        bk = bk_ref[...]
        acc = jnp.zeros((BLOCK, KEY_SPAN), F32)
        for r in range(REL_BUCKETS):
            val = jnp.sum(jnp.where(lane == r, row, 0.0), axis=1, keepdims=True)
            acc = jnp.where(bk == r, val, acc)
        rel = (lax.broadcasted_iota(jnp.int32, (BLOCK, KEY_SPAN), 1) - WINDOW
               - lax.broadcasted_iota(jnp.int32, (BLOCK, KEY_SPAN), 0))
        o_ref[0] = jnp.where(jnp.abs(rel) <= WINDOW, acc, NEG)

    return _pc(body, name=name, grid=(ATTN_HEADS,),
               in_specs=[BS((1, 1, LANE), lambda h: (h, 0, 0)), BS((BLOCK, KEY_SPAN), lambda h: (0, 0))],
               out_specs=BS((1, BLOCK, KEY_SPAN), lambda h: (h, 0, 0)), out_shape=SDS((ATTN_HEADS, BLOCK, KEY_SPAN), F32))(rbt, bucket)


def _bias_reduce(dbias, bucket, *, name):
    def body(db_ref, bk_ref, o_ref):
        lane = lax.broadcasted_iota(jnp.int32, (1, LANE), 1)
        x = db_ref[0]
        bk = bk_ref[...]
        acc = jnp.zeros((1, LANE), F32)
        for r in range(REL_BUCKETS):
            part = jnp.sum(jnp.where(bk == r, x, 0.0), axis=1, keepdims=True)
            acc = jnp.where(lane == r, jnp.sum(part, axis=0, keepdims=True), acc)
        o_ref[0] = acc

    out = _pc(body, name=name, grid=(ATTN_HEADS,),
              in_specs=[BS((1, BLOCK, KEY_SPAN), lambda h: (h, 0, 0)), BS((BLOCK, KEY_SPAN), lambda h: (0, 0))],
              out_specs=BS((1, 1, LANE), lambda h: (h, 0, 0)), out_shape=SDS((ATTN_HEADS, 1, LANE), F32))(dbias, bucket)
    return out[:, 0, :REL_BUCKETS].T


def _loss_head(x2, w, target, *, name):
    T, D = x2.shape
    tr = _div_tile(T, 512, 8)

    def tile_loss(x, w, t):
        err = _rms(x, w) - t
        return 0.5 * jnp.sum(jnp.mean(err * err, axis=-1, keepdims=True), axis=0, keepdims=True)

    def body(x_ref, w_ref, t_ref, loss_ref, dx_ref, dxb_ref, dw_ref):
        t = t_ref[...]
        l, vjp = jax.vjp(lambda x, w: tile_loss(x, w, t), x_ref[...], w_ref[...])
        dx, dw = vjp(jnp.ones((1, 1), F32))
        dx_ref[...] = dx
        dxb_ref[...] = dx.astype(BF16)

        @pl.when(pl.program_id(0) == 0)
        def _():
            dw_ref[...] = jnp.zeros_like(dw_ref)
            loss_ref[...] = jnp.zeros_like(loss_ref)

        dw_ref[...] += dw
        loss_ref[...] += l + jnp.zeros((1, LANE), F32)

    row = BS((tr, D), lambda i: (i, 0))
    one = BS((1, D), lambda i: (0, 0))
    return _pc(body, name=name, grid=(T // tr,), in_specs=[row, one, row],
               out_specs=[BS((1, LANE), lambda i: (0, 0)), row, row, one],
               out_shape=[SDS((1, LANE), F32), SDS((T, D), F32), SDS((T, D), BF16), SDS((1, D), F32)])(x2, w.reshape(1, D), target)


def _adamw(w2, g2, m2, v2, *, name):
    R, C = w2.shape
    tr = _div_tile(R, 256, 8)
    c1 = 1.0 - ADAM_B1 ** ADAM_STEP
    c2 = 1.0 - ADAM_B2 ** ADAM_STEP

    def body(w_ref, g_ref, m_ref, v_ref, d_ref, nm_ref, nv_ref):
        g = g_ref[...]
        m = ADAM_B1 * m_ref[...] + (1.0 - ADAM_B1) * g
        v = ADAM_B2 * v_ref[...] + (1.0 - ADAM_B2) * (g * g)
        d_ref[...] = -ADAM_LR * ((m / c1) / (jnp.sqrt(v / c2) + ADAM_EPS) + ADAM_WD * w_ref[...])
        nm_ref[...] = m
        nv_ref[...] = v

    t = BS((tr, C), lambda i: (i, 0))
    return _pc(body, name=name, grid=(R // tr,), in_specs=[t, t, t, t], out_specs=[t, t, t],
               out_shape=[SDS((R, C), F32)] * 3)(w2, g2, m2, v2)


def _place():
    return lax.axis_index("x"), lax.axis_index("y"), lax.axis_index("c")


def _gather_rider(shards):
    na = len(shards)

    def copies(ins, outs, sems):
        send_sems, recv_sems = sems
        x, y, c = _place()
        for a in range(na):
            for k, peer in enumerate([(1 - x, y, c), (x, 1 - y, c), (1 - x, 1 - y, c), (x, y, 1 - c)]):
                send = functools.partial(pltpu.make_async_remote_copy, ins[a], outs[a].at[2 * x + y], send_sems.at[a, k],
                                         recv_sems.at[a, k], device_id=peer, device_id_type=MESH)
                got = outs[a].at[2 * peer[0] + peer[1]]
                arrived = functools.partial(pltpu.make_async_remote_copy, got, got, send_sems.at[a, k], recv_sems.at[a, k],
                                            device_id=peer, device_id_type=MESH)
                yield send, arrived

    def start(ins, outs, sems):
        for send, _ in copies(ins, outs, sems):
            send().start()

    def finish(ins, outs, sems):
        both = list(copies(ins, outs, sems))
        for _, arrived in both:
            arrived().wait_recv()
        for send, _ in both:
            send().wait_send()

    return dict(ins=list(shards), prev=[], out_shape=[SDS((N_CHIP,) + s.shape, s.dtype) for s in shards],
                scratch=[pltpu.SemaphoreType.DMA((na, 4)), pltpu.SemaphoreType.DMA((na, 4))], start=start, finish=finish)


def _scatter_rider(bufs, layer, prev):
    na = len(bufs)
    h = layer // (DEPTH // 2)

    def copies(ins, outs, sems):
        send_sems, recv_sems, local_sems = sems
        x, y, c = _place()
        me = 4 * x + 2 * y + c
        for a in range(na):
            for j in range(N_CHIP):
                is_self = ((2 * x + y) == j) & (c == h)
                local = functools.partial(pltpu.make_async_copy, ins[a].at[j], outs[a].at[me], local_sems.at[a])
                remote = functools.partial(pltpu.make_async_remote_copy, ins[a].at[j], outs[a].at[me], send_sems.at[a, j],
                                           recv_sems.at[a, me], device_id=(j // 2, j % 2, h), device_id_type=MESH)
                yield is_self, local, remote

    def start(ins, outs, sems):
        for is_self, local, remote in copies(ins, outs, sems):
            pl.when(is_self)(lambda: local().start())
            pl.when(jnp.logical_not(is_self))(lambda: remote().start())

    def finish(ins, outs, sems):
        _, recv_sems, _ = sems
        x, y, c = _place()
        me = 4 * x + 2 * y + c
        for a in range(na):
            for s in range(N_DEV):
                got = outs[a].at[s]
                arrived = functools.partial(pltpu.make_async_remote_copy, got, got, recv_sems.at[a, s], recv_sems.at[a, s],
                                            device_id=(s // 4, (s // 2) % 2, s % 2), device_id_type=MESH)
                pl.when((c == h) & (me != s))(lambda: arrived().wait_recv())
        for is_self, local, remote in copies(ins, outs, sems):
            pl.when(is_self)(lambda: local().wait())
            pl.when(jnp.logical_not(is_self))(lambda: remote().wait_send())

    return dict(ins=list(bufs), prev=list(prev), out_shape=[SDS((N_DEV,) + b.shape[1:], b.dtype) for b in bufs],
                scratch=[pltpu.SemaphoreType.DMA((na, N_CHIP)), pltpu.SemaphoreType.DMA((na, N_DEV)), pltpu.SemaphoreType.DMA((na,))],
                start=start, finish=finish)


def _sum_sources(parts, parity, into, *, name):
    _, R, C = parts.shape
    tr = _div_tile(R, 256, 16)

    def body(p_ref, *rest):
        acc = p_ref[0].astype(F32)
        for s in range(1, N_DEV):
            acc = acc + p_ref[s].astype(F32)
        rest[-1][0] = acc

    prev = [] if into is None else [into]
    return _pc(body, name=name, grid=(R // tr,),
               in_specs=[BS((N_DEV, tr, C), lambda i: (0, i, 0))] + [BS(memory_space=pl.ANY)] * len(prev),
               out_specs=BS((1, tr, C), lambda i: ((DEPTH // 2) * lax.axis_index("c") + parity, i, 0)),
               out_shape=SDS((DEPTH, R, C), F32), aliases={1: 0} if prev else None)(parts, *prev)


def _join_halves(fulls, *, name):
    na = len(fulls)
    half = DEPTH // 2

    def body(*refs):
        outs = refs[na:2 * na]
        send_sems, recv_sems = refs[2 * na:]
        x, y, c = _place()
        cps = []
        for a in range(na):
            mine = outs[a].at[pl.ds(c * half, half)]
            cp = pltpu.make_async_remote_copy(mine, mine, send_sems.at[a], recv_sems.at[a],
                                              device_id=(x, y, 1 - c), device_id_type=MESH)
            cp.start()
            cps.append(cp)
        for a in range(na):
            theirs = outs[a].at[pl.ds((1 - c) * half, half)]
            pltpu.make_async_remote_copy(theirs, theirs, send_sems.at[a], recv_sems.at[a],
                                         device_id=(x, y, 1 - c), device_id_type=MESH).wait_recv()
        for cp in cps:
            cp.wait_send()

    any_spec = BS(memory_space=pl.ANY)
    return pl.pallas_call(
        body, name=name, in_specs=[any_spec] * na, out_specs=[any_spec] * na,
        out_shape=[SDS(f.shape, f.dtype) for f in fulls], input_output_aliases={a: a for a in range(na)},
        scratch_shapes=[pltpu.SemaphoreType.DMA((na,)), pltpu.SemaphoreType.DMA((na,))],
        compiler_params=pltpu.CompilerParams(has_side_effects=True))(*fulls)


def _allreduce_small(vec, *, name):
    R = vec.shape[0]

    def body(v_ref, o_ref, all_ref, send_sems, recv_sems):
        x, y, c = _place()
        me = 4 * x + 2 * y + c
        all_ref[me] = v_ref[...]
        sends = []
        for r in range(1, N_DEV):
            tgt = (x ^ (r >> 2), y ^ ((r >> 1) & 1), c ^ (r & 1))
            cp = pltpu.make_async_remote_copy(v_ref, all_ref.at[me], send_sems.at[r - 1], recv_sems.at[r - 1],
                                              device_id=tgt, device_id_type=MESH)
            cp.start()
            sends.append(cp)
        for r in range(1, N_DEV):
            tx, ty, tc = x ^ (r >> 2), y ^ ((r >> 1) & 1), c ^ (r & 1)
            got = all_ref.at[4 * tx + 2 * ty + tc]
            pltpu.make_async_remote_copy(got, got, send_sems.at[r - 1], recv_sems.at[r - 1],
                                         device_id=(tx, ty, tc), device_id_type=MESH).wait_recv()
        for cp in sends:
            cp.wait_send()
        acc = all_ref[0]
        for s in range(1, N_DEV):
            acc = acc + all_ref[s]
        o_ref[...] = acc

    vm = BS(memory_space=pltpu.VMEM)
    return pl.pallas_call(
        body, name=name, in_specs=[vm], out_specs=vm, out_shape=SDS((R, LANE), F32),
        scratch_shapes=[pltpu.VMEM((N_DEV, R, LANE), F32), pltpu.SemaphoreType.DMA((N_DEV - 1,)), pltpu.SemaphoreType.DMA((N_DEV - 1,))],
        compiler_params=pltpu.CompilerParams(has_side_effects=True, vmem_limit_bytes=VMEM_LIMIT_BYTES))(vec)


def _pack(arrs):
    rows = []
    for a in arrs:
        f = a.reshape(-1).astype(F32)
        n = -(-f.shape[0] // LANE) * LANE
        rows.append(jnp.pad(f, (0, n - f.shape[0])).reshape(-1, LANE))
    v = jnp.concatenate(rows, axis=0)
    pad = -v.shape[0] % 8
    return jnp.pad(v, ((0, pad), (0, 0)))


def _unpack(v, shapes):
    out, r = [], 0
    for s in shapes:
        n = int(np.prod(s)) if len(s) else 1
        nr = -(-n // LANE)
        out.append(v[r:r + nr].reshape(-1)[:n].reshape(s))
        r += nr
    return out


def _perm_in_cols(w_full):
    z, xbc, dt, q, k, v = (w_full[..., :Z_END], w_full[..., Z_END:XBC_END], w_full[..., XBC_END:DT_END],
                           w_full[..., DT_END:Q_END], w_full[..., Q_END:K_END], w_full[..., K_END:])
    pad = jnp.zeros(dt.shape[:-1] + (LANE - dt.shape[-1],), dt.dtype)
    return jnp.concatenate([q, z, xbc, k, v, dt, pad], axis=-1)


def _unperm_in_cols(g):
    q, z, xbc, k, v, dt = (g[..., :1024], g[..., 1024:2048], g[..., 2048:3584], g[..., 3584:3840], g[..., 3840:4096],
                           g[..., 4096:4096 + 2 * SSM_HEADS])
    return jnp.concatenate([z, xbc, dt, q, k, v], axis=-1)


def _dt_cols(a):
    return jnp.pad(a.reshape(1, 2 * SSM_HEADS), ((0, 0), (0, LANE - 2 * SSM_HEADS)))


def _dt_fwd(proj, dtb, *, name):
    B, L, _ = proj.shape

    def body(p_ref, b_ref, o_ref):
        o_ref[0] = _softplus(p_ref[0] + b_ref[...])

    return _pc(body, name=name, grid=(B,),
               in_specs=[BS((1, L, LANE), lambda b: (b, 0, P_COLS // LANE - 1)), BS((1, LANE), lambda b: (0, 0))],
               out_specs=BS((1, L, LANE), lambda b: (b, 0, 0)), out_shape=SDS((B, L, LANE), F32))(proj, dtb)


def _dt_bwd(proj, dtb, ddt, *, name):
    B, L, _ = proj.shape

    def body(p_ref, b_ref, g_ref, o_ref, db_ref):
        g = g_ref[0] * _sigmoid(p_ref[0] + b_ref[...])
        o_ref[0] = g.astype(BF16)
        db_ref[0] = jnp.sum(g, axis=0, keepdims=True)

    row = BS((1, L, LANE), lambda b: (b, 0, 0))
    return _pc(body, name=name, grid=(B,),
               in_specs=[BS((1, L, LANE), lambda b: (b, 0, P_COLS // LANE - 1)), BS((1, LANE), lambda b: (0, 0)), row],
               out_specs=[row, BS((1, 1, LANE), lambda b: (b, 0, 0))],
               out_shape=[SDS((B, L, LANE), BF16), SDS((B, 1, LANE), F32)])(proj, dtb, ddt)


def _layer_fwd(i, x, wts, small, band_bias, riders=None, arrived=None):
    riders = riders or {}
    B, L, D = x.shape
    T = B * L
    x2 = x.reshape(T, D)
    h, proj2 = _norm_mm(x2, small["norm1_w"][i], wts["w_in"], name=f"in_proj_{i}", tn=1408)
    proj = proj2.reshape(B, L, P_COLS)
    xbc_act = _conv_fwd(proj, 2048 // 256, small["conv_w"][i], small["conv_b"][i], taps=SSM_CONV, ct=256,
                        out_dtype=F32, name=f"ssm_conv_{i}")
    dtb, alog = _dt_cols(small["dt_bias"][i]), _dt_cols(small["a_log"][i])
    dskip = jnp.repeat(small["d_skip"][i], HEAD_DIM).reshape(1, SSM_WIDTH)
    dtc = _dt_fwd(proj, dtb, name=f"dt_{i}")
    ypre, hs, carried = _ssd_fwd(xbc_act, dtc, alog, dskip, name=f"ssd_{i}", rider=riders.get("ssd"))
    if carried is not None:
        arrived("ssd", carried)
    mixed = _gate_fwd(ypre.reshape(T, SSM_WIDTH), proj2, small["ssm_norm_w"][i], name=f"gate_{i}")
    sinkcol = jnp.repeat(small["attn_sink"][i], BLOCK).reshape(ATTN_HEADS * BLOCK, 1)
    mixed, carried = _attn_fwd(proj, band_bias, sinkcol, mixed.reshape(B, L, 2 * D), name=f"attn_{i}", rider=riders.get("attn"))
    if carried is not None:
        arrived("attn", carried)
    mixed = mixed.reshape(T, 2 * D)
    x_mid = _mm(mixed, wts["w_out"], add=x2, name=f"out_proj_{i}")
    h2, gu2 = _norm_mm(x_mid, small["norm2_w"][i], wts["w_up"], name=f"up_proj_{i}", out_dtype=BF16, tn=1408)
    gu = gu2.reshape(B, L, 2 * D_FF)
    act = _conv_fwd(gu, 0, small["ffn_conv_w"][i], small["ffn_conv_b"][i], taps=FFN_CONV, ct=256, gate_blk0=D_FF // 256,
                    out_dtype=BF16, name=f"ffn_conv_{i}")
    x_out = _mm(act.reshape(T, D_FF), wts["w_down"], add=x_mid, name=f"down_proj_{i}", tk=1408)
    saved = dict(x2=x2, h=h, proj2=proj2, xbc_act=xbc_act, dtb=dtb, dtc=dtc, alog=alog, dskip=dskip, ypre=ypre, hs=hs, mixed=mixed,
                 sinkcol=sinkcol, x_mid=x_mid, h2=h2, gu=gu, act=act)
    return x_out.reshape(B, L, D), saved


def _layer_bwd(i, dx_out, dxb, sv, wts, small, band_bias, attn_rider=None, ssd_rider=None, arrived=None, wgrad_dtype=F32):
    T, D = dx_out.shape
    B, L = sv["gu"].shape[:2]
    g = {}
    dact = _mm(dxb, wts["w_down"], tb=True, out_dtype=BF16, name=f"d_act_{i}", tn=1408)
    g["w_down"] = _mm(sv["act"].reshape(T, D_FF), dxb, ta=True, out_dtype=wgrad_dtype, name=f"dw_down_{i}", tm=1408)
    dg, du, dcw, dcb = _conv_bwd(sv["gu"], 0, small["ffn_conv_w"][i], small["ffn_conv_b"][i], dact.reshape(B, L, D_FF),
                                 taps=FFN_CONV, ct=256, gate_blk0=D_FF // 256, name=f"d_ffn_conv_{i}")
    g["ffn_conv_w"] = jnp.sum(dcw, axis=0)[:FFN_CONV]
    g["ffn_conv_b"] = jnp.sum(dcb, axis=(0, 1))
    dg, du, w_up = dg.reshape(T, D_FF), du.reshape(T, D_FF), wts["w_up"]
    dh2 = _mm(dg, w_up[:, :D_FF], tb=True, name=f"d_h2_g_{i}", tk=1408)
    dh2 = _mm(du, w_up[:, D_FF:], tb=True, add=dh2, name=f"d_h2_u_{i}", tk=1408)
    dx_mid, dmb, dw2 = _dnorm(dh2, sv["x_mid"], small["norm2_w"][i], dx_out, name=f"d_norm2_{i}")
    g["w_up"] = jnp.concatenate([_mm(sv["h2"], dg, ta=True, out_dtype=wgrad_dtype, name=f"dw_up_g_{i}", tn=1408),
                                 _mm(sv["h2"], du, ta=True, out_dtype=wgrad_dtype, name=f"dw_up_u_{i}", tn=1408)], axis=1)
    g["norm2_w"] = dw2[0]
    dmixed = _mm(dmb, wts["w_out"], tb=True, name=f"d_mixed_{i}")
    g["w_out"] = _mm(sv["mixed"], dmb, ta=True, out_dtype=wgrad_dtype, name=f"dw_out_{i}")
    dypre, dz, dwn = _gate_bwd(sv["ypre"].reshape(T, SSM_WIDTH), sv["proj2"], small["ssm_norm_w"][i], dmixed, name=f"d_gate_{i}")
    g["ssm_norm_w"] = jnp.sum(dwn, axis=(0, 1))
    proj = sv["proj2"].reshape(B, L, P_COLS)
    dxs, dbm, dcm, ddt, dalog, ddsk, carried = _ssd_bwd(sv["xbc_act"], sv["dtc"], sv["alog"], sv["dskip"], sv["hs"],
                                                        dypre.reshape(B, L, SSM_WIDTH), name=f"d_ssd_{i}",
                                                        rider=ssd_rider(g) if ssd_rider is not None else None)
    if carried is not None:
        arrived("ssd", carried)
    ddt, ddtb = _dt_bwd(proj, sv["dtb"], ddt, name=f"d_dt_{i}")
    g["dt_bias"] = jnp.sum(ddtb, axis=(0, 1))[:2 * SSM_HEADS].reshape(2, SSM_HEADS)
    g["a_log"] = jnp.sum(dalog, axis=(0, 1, 2))[:2 * SSM_HEADS].reshape(2, SSM_HEADS)
    g["d_skip"] = jnp.sum(ddsk.reshape(B, SSM_HEADS, HEAD_DIM), axis=(0, 2))
    dxbc_act = jnp.concatenate([dxs, dbm, dcm], axis=-1)
    dxbc, dcw, dcb = _conv_bwd(proj, 2048 // 256, small["conv_w"][i], small["conv_b"][i], dxbc_act, taps=SSM_CONV, ct=256,
                               name=f"d_ssm_conv_{i}")
    g["conv_w"] = jnp.sum(dcw, axis=0)[:SSM_CONV]
    g["conv_b"] = jnp.sum(dcb, axis=(0, 1))
    dq, dk, dv, dbias, dsink, carried = _attn_bwd(proj, band_bias, sv["sinkcol"], dmixed.reshape(B, L, 2 * D), name=f"d_attn_{i}",
                                                  rider=attn_rider)
    if carried is not None:
        arrived("attn", carried)
    g["attn_sink"] = jnp.sum(dsink.reshape(ATTN_HEADS, BLOCK), axis=1)
    dproj = jnp.concatenate([dq, dz.reshape(B, L, SSM_WIDTH), dxbc, dk.astype(BF16), dv.astype(BF16), ddt], axis=-1).reshape(T, P_COLS)
    dh = _mm(dproj, wts["w_in"], tb=True, name=f"d_h_{i}", tk=1408)
    dx_in, dx_in_b, dw1 = _dnorm(dh, sv["x2"], small["norm1_w"][i], dx_mid, name=f"d_norm1_{i}")
    g["w_in"] = _unperm_in_cols(_mm(sv["h"], dproj, ta=True, out_dtype=wgrad_dtype, name=f"dw_in_{i}", tn=1408))
    g["norm1_w"] = dw1[0]
    return dx_in, dx_in_b, g, dbias


_BIG = ("w_in", "w_out", "w_up", "w_down")
_BIG_AXIS = {"w_in": 2, "w_out": 1, "w_up": 2, "w_down": 1}
_SMALL = ("rel_bias", "norm1_w", "conv_w", "conv_b", "dt_bias", "a_log", "d_skip", "ssm_norm_w", "attn_sink", "norm2_w",
          "ffn_conv_w", "ffn_conv_b", "final_norm_w")
_SMALL_SHARDED = ("conv_w", "ffn_conv_w")
_ORDER = ("rel_bias", "norm1_w", "w_in", "conv_w", "conv_b", "dt_bias", "a_log", "d_skip", "ssm_norm_w", "attn_sink", "w_out",
          "norm2_w", "w_up", "ffn_conv_w", "ffn_conv_b", "w_down", "final_norm_w")


def _local_step(x, target, small, wts=None, exchange=None):
    B, L, D = x.shape
    bucket = _bucket_table()
    band_bias = _bias_expand(small["rel_bias"], bucket, name="band_bias")

    def fetch(spec):
        return exchange["gather"](spec) if exchange is not None and spec else None

    def fetched(spec, carried):
        for (i, k), full in zip(spec, exchange["weights"](spec, carried)):
            wts[i][k] = full

    if exchange is not None:
        wts = [{} for _ in range(DEPTH)]
        fetched([(0, "w_in")], _run_rider(fetch([(0, "w_in")]), name="gather_w_in_0"))
    saved = []
    for i in range(DEPTH):
        nxt = i + 1 < DEPTH
        if i == 0:
            plan = {"ssd": [(0, "w_out"), (0, "w_up"), (0, "w_down")], "attn": [(1, "w_in"), (1, "w_out"), (1, "w_up")] if nxt else []}
        else:
            plan = {"ssd": [(i, "w_down")] + ([(i + 1, "w_in"), (i + 1, "w_out")] if nxt else []), "attn": [(i + 1, "w_up")] if nxt else []}
        x, sv = _layer_fwd(i, x, wts[i], small, band_bias, {c: fetch(s) for c, s in plan.items()}, lambda c, r: fetched(plan[c], r))
        saved.append(sv)
    loss, dx, dxb, dwf = _loss_head(x.reshape(B * L, D), small["final_norm_w"], target.reshape(B * L, D), name="loss_head")
    per_layer = []
    dbias = jnp.zeros((ATTN_HEADS, BLOCK, KEY_SPAN), F32)
    late = None
    for i in reversed(range(DEPTH)):
        own = [(i, "w_down"), (i, "w_up"), (i, "w_out")]
        plan = {"ssd": own, "attn": late[0] if late else []}
        attn_rider = exchange["scatter"](*late) if late else None
        ssd_rider = (lambda g: exchange["scatter"](own, [g[k] for _, k in own])) if exchange is not None else None
        dx, dxb, g, dbias_i = _layer_bwd(i, dx, dxb, saved[i], wts[i], small, band_bias, attn_rider, ssd_rider,
                                    lambda c, r: exchange["collect"](plan[c], r), F32 if exchange is None else BF16)
        if exchange is not None:
            late = ([(i, "w_in")], [g["w_in"]])
            for k in _BIG:
                g.pop(k)
        dbias = dbias + dbias_i
        per_layer.append(g)
    if exchange is not None:
        exchange["collect"](late[0], _run_rider(exchange["scatter"](*late), name="scatter_dw_in_0"))
    per_layer.reverse()
    grads = {k: jnp.stack([g[k] for g in per_layer]) for k in per_layer[0]}
    grads["rel_bias"] = _bias_reduce(dbias, bucket, name="d_rel_bias")
    grads["final_norm_w"] = dwf[0]
    return loss, dx.reshape(B, L, D), grads


def _split_by_chip(g, axis):
    shp = g.shape
    n = shp[axis] // N_CHIP
    g = g.reshape(shp[:axis] + (N_CHIP, n) + shp[axis + 1:])
    return jnp.moveaxis(g, axis, 0)


def _join_chips(a, axis):
    a = jnp.moveaxis(a, 0, axis)
    shp = a.shape
    return a.reshape(shp[:axis] + (shp[axis] * shp[axis + 1],) + shp[axis + 2:])


def kernel(x, rel_bias, norm1_w, w_in, conv_w, conv_b, dt_bias, a_log, d_skip, ssm_norm_w, attn_sink, w_out, norm2_w, w_up, ffn_conv_w, ffn_conv_b, w_down, final_norm_w, loss_target, m_rel_bias, m_norm1_w, m_w_in, m_conv_w, m_conv_b, m_dt_bias, m_a_log, m_d_skip, m_ssm_norm_w, m_attn_sink, m_w_out, m_norm2_w, m_w_up, m_ffn_conv_w, m_ffn_conv_b, m_w_down, m_final_norm_w, v_rel_bias, v_norm1_w, v_w_in, v_conv_w, v_conv_b, v_dt_bias, v_a_log, v_d_skip, v_ssm_norm_w, v_attn_sink, v_w_out, v_norm2_w, v_w_up, v_ffn_conv_w, v_ffn_conv_b, v_w_down, v_final_norm_w):
    w = dict(rel_bias=rel_bias, norm1_w=norm1_w, w_in=w_in, conv_w=conv_w, conv_b=conv_b, dt_bias=dt_bias, a_log=a_log,
             d_skip=d_skip, ssm_norm_w=ssm_norm_w, attn_sink=attn_sink, w_out=w_out, norm2_w=norm2_w, w_up=w_up,
             ffn_conv_w=ffn_conv_w, ffn_conv_b=ffn_conv_b, w_down=w_down, final_norm_w=final_norm_w)
    m = dict(rel_bias=m_rel_bias, norm1_w=m_norm1_w, w_in=m_w_in, conv_w=m_conv_w, conv_b=m_conv_b, dt_bias=m_dt_bias,
             a_log=m_a_log, d_skip=m_d_skip, ssm_norm_w=m_ssm_norm_w, attn_sink=m_attn_sink, w_out=m_w_out, norm2_w=m_norm2_w,
             w_up=m_w_up, ffn_conv_w=m_ffn_conv_w, ffn_conv_b=m_ffn_conv_b, w_down=m_w_down, final_norm_w=m_final_norm_w)
    v = dict(rel_bias=v_rel_bias, norm1_w=v_norm1_w, w_in=v_w_in, conv_w=v_conv_w, conv_b=v_conv_b, dt_bias=v_dt_bias,
             a_log=v_a_log, d_skip=v_d_skip, ssm_norm_w=v_ssm_norm_w, attn_sink=v_attn_sink, w_out=v_w_out, norm2_w=v_norm2_w,
             w_up=v_w_up, ffn_conv_w=v_ffn_conv_w, ffn_conv_b=v_ffn_conv_b, w_down=v_w_down, final_norm_w=v_final_norm_w)
    my_chip = 2 * lax.axis_index("x") + lax.axis_index("y")

    shards = {k: w[k].astype(BF16) for k in _BIG}
    received = {}

    def gather(spec):
        return _gather_rider([shards[k][i] for i, k in spec])

    def weights(spec, carried):
        out = []
        for (i, k), g_ in zip(spec, carried):
            full = _join_chips(g_, _BIG_AXIS[k] - 1)
            out.append(_perm_in_cols(full) if k == "w_in" else full)
        return out

    def scatter(spec, grads):
        layer = spec[0][0]
        bufs = [_split_by_chip(g_, _BIG_AXIS[k] - 1).astype(BF16) for (_, k), g_ in zip(spec, grads)]
        prev = [received[(layer % 2, k)] for _, k in spec] if layer + 2 < DEPTH else []
        return _scatter_rider(bufs, layer, prev)

    def collect(spec, carried):
        for (i, k), pieces in zip(spec, carried):
            received[(i % 2, k)] = pieces

    conv_shapes = [(DEPTH, SSM_CONV, CONV_CH), (DEPTH, FFN_CONV, D_FF)]
    placed = [lax.dynamic_update_slice_in_dim(jnp.zeros(s, F32), w[k], my_chip * w[k].shape[2], axis=2)
              for k, s in zip(_SMALL_SHARDED, conv_shapes)]
    lead = (lax.axis_index("c") == 0).astype(F32)
    conv_full = _unpack(_allreduce_small(_pack([p * lead for p in placed]), name="gather_conv_weights"), conv_shapes)
    small = {k: w[k] for k in _SMALL}
    small["conv_w"], small["ffn_conv_w"] = conv_full

    loss_part, grad_x, gp = _local_step(x, loss_target, small,
                                        exchange=dict(gather=gather, weights=weights, scatter=scatter, collect=collect))

    small_shapes = [small[k].shape for k in _SMALL] + [()]
    red = _unpack(_allreduce_small(_pack([gp[k] for k in _SMALL] + [loss_part[0, :1]]), name="reduce_small"), small_shapes)
    gsmall = dict(zip(_SMALL, red[:-1]))
    loss = red[-1]
    for k in _SMALL_SHARDED:
        n = w[k].shape[2]
        gsmall[k] = lax.dynamic_slice_in_dim(gsmall[k], my_chip * n, n, axis=2)

    fulls = []
    for k in _BIG:
        full = None
        for p in range(DEPTH // 2):
            full = _sum_sources(received[(p, k)], p, full, name=f"sum_{k}_{p}")
        fulls.append(full)
    gbig = dict(zip(_BIG, _join_halves(fulls, name="join_halves")))

    grad, delta, new_m, new_v = {}, {}, {}, {}
    for k in _BIG:
        shp = w[k].shape
        two = lambda a: a.reshape(shp[0] * shp[1], shp[2])
        d_, m_, v_ = _adamw(two(w[k]), two(gbig[k]), two(m[k]), two(v[k]), name=f"adamw_{k}")
        grad[k], delta[k], new_m[k], new_v[k] = gbig[k], d_.reshape(shp), m_.reshape(shp), v_.reshape(shp)
    shapes = [w[k].shape for k in _SMALL]
    d_, m_, v_ = _adamw(_pack([w[k] for k in _SMALL]), _pack([gsmall[k] for k in _SMALL]), _pack([m[k] for k in _SMALL]),
                        _pack([v[k] for k in _SMALL]), name="adamw_small")
    for k, a, b_, c_ in zip(_SMALL, _unpack(d_, shapes), _unpack(m_, shapes), _unpack(v_, shapes)):
        grad[k], delta[k], new_m[k], new_v[k] = gsmall[k], a, b_, c_
    return (loss, grad_x, *[grad[k] for k in _ORDER], *[delta[k] for k in _ORDER], *[new_m[k] for k in _ORDER],
            *[new_v[k] for k in _ORDER])
```

```python
import functools
import math

import jax
import jax.numpy as jnp
import numpy as np
from jax import lax
from jax.experimental import pallas as pl
from jax.experimental.pallas import tpu as pltpu

F32 = jnp.float32
BF16 = jnp.bfloat16
BS = pl.BlockSpec
SDS = jax.ShapeDtypeStruct
MESH = pl.DeviceIdType.MESH

D_MODEL = 1024
DEPTH = 4
SSM_HEADS = 16
SSM_WIDTH = 1024
BC_WIDTH = 256
CONV_CH = 1536
SSM_CONV = 7
CHUNK = 128
ATTN_HEADS = 16
KV_HEADS = 4
HEAD_DIM = 64
WINDOW = 128
BLOCK = 128
KEY_SPAN = 384
REL_BUCKETS = 32
REL_MAX_DIST = 128
D_FF = 2816
FFN_CONV = 3
NORM_EPS = 1e-6
Z_END = 1024
XBC_END = 2560
DT_END = 2592
Q_END = 3616
K_END = 3872
IN_COLS = 4128
P_COLS = 4224
ADAM_LR, ADAM_B1, ADAM_B2, ADAM_EPS, ADAM_WD, ADAM_STEP = 0.001, 0.9, 0.999, 1e-08, 0.01, 10
NEG = -1e30
N_DEV = 8
N_CHIP = 4
LANE = 128
VMEM_LIMIT_BYTES = 48 * 1024 * 1024


def _pc(body, *, name, grid, in_specs, out_specs, out_shape, scratch_shapes=(), aliases=None):
    return pl.pallas_call(
        body, name=name, grid=grid, in_specs=in_specs, out_specs=out_specs, out_shape=out_shape,
        scratch_shapes=list(scratch_shapes), input_output_aliases=aliases or {},
        compiler_params=pltpu.CompilerParams(dimension_semantics=("arbitrary",) * len(grid),
                                             vmem_limit_bytes=VMEM_LIMIT_BYTES))


def _split_rider_refs(refs, n_in, n_out, rider):
    n_rin = len(rider["ins"]) + len(rider["prev"])
    n_rout = len(rider["out_shape"])
    core = refs[:n_in] + refs[n_in + n_rin:n_in + n_rin + n_out] + refs[n_in + n_rin + n_out + n_rout + len(rider["scratch"]):]
    rins = refs[n_in:n_in + len(rider["ins"])]
    routs = refs[n_in + n_rin + n_out:n_in + n_rin + n_out + n_rout]
    sems = refs[n_in + n_rin + n_out + n_rout:n_in + n_rin + n_out + n_rout + len(rider["scratch"])]
    return core, rins, routs, sems


def _pc_carry(body, args, *, name, grid, in_specs, out_specs, out_shape, scratch_shapes=(), rider=None, aliases=None):
    if rider is None:
        return _pc(body, name=name, grid=grid, in_specs=in_specs, out_specs=out_specs, out_shape=out_shape,
                   scratch_shapes=scratch_shapes, aliases=aliases)(*args), None
    n_in, n_out = len(in_specs), len(out_shape)
    any_spec = BS(memory_space=pl.ANY)

    def full(*refs):
        core, rins, routs, sems = _split_rider_refs(refs, n_in, n_out, rider)
        ids = [pl.program_id(d) for d in range(len(grid))]
        first = functools.reduce(jnp.logical_and, [i == 0 for i in ids])
        last = functools.reduce(jnp.logical_and, [i == g - 1 for i, g in zip(ids, grid)])

        @pl.when(first)
        def _():
            rider["start"](rins, routs, sems)

        body(*core)

        @pl.when(last)
        def _():
            rider["finish"](rins, routs, sems)

    n_rin = len(rider["ins"])
    outs = pl.pallas_call(
        full, name=name, grid=grid,
        in_specs=list(in_specs) + [any_spec] * (n_rin + len(rider["prev"])),
        out_specs=list(out_specs) + [any_spec] * len(rider["out_shape"]),
        out_shape=list(out_shape) + list(rider["out_shape"]),
        scratch_shapes=list(rider["scratch"]) + list(scratch_shapes),
        input_output_aliases={**(aliases or {}), **{n_in + n_rin + t: n_out + t for t in range(len(rider["prev"]))}},
        compiler_params=pltpu.CompilerParams(dimension_semantics=("arbitrary",) * len(grid), vmem_limit_bytes=VMEM_LIMIT_BYTES,
                                             has_side_effects=True))(*args, *rider["ins"], *rider["prev"])
    return outs[:n_out], outs[n_out:]


def _run_rider(rider, *, name):
    any_spec = BS(memory_space=pl.ANY)
    n_rin = len(rider["ins"])

    def body(*refs):
        _, rins, routs, sems = _split_rider_refs(refs, 0, 0, rider)
        rider["start"](rins, routs, sems)
        rider["finish"](rins, routs, sems)

    return pl.pallas_call(
        body, name=name, in_specs=[any_spec] * (n_rin + len(rider["prev"])), out_specs=[any_spec] * len(rider["out_shape"]),
        out_shape=list(rider["out_shape"]), scratch_shapes=list(rider["scratch"]),
        input_output_aliases={n_rin + t: t for t in range(len(rider["prev"]))},
        compiler_params=pltpu.CompilerParams(has_side_effects=True))(*rider["ins"], *rider["prev"])


def _div_tile(n, pref, mult):
    t = min(pref, n)
    t -= t % mult
    while t >= mult:
        if n % t == 0:
            return t
        t -= mult
    return n


def _mm(a, b, *, name, ta=False, tb=False, add=None, out_dtype=F32, tm=1024, tn=1024, tk=1024):
    if ta:
        K, M = a.shape
    else:
        M, K = a.shape
    N = b.shape[0] if tb else b.shape[1]
    tm, tn, tk = _div_tile(M, tm, LANE), _div_tile(N, tn, LANE), _div_tile(K, tk, LANE)
    nk = K // tk
    dims = (((0,) if ta else (1,), (1,) if tb else (0,)), ((), ()))

    def body_single(*refs):
        r = lax.dot_general(refs[0][...], refs[1][...], dims, preferred_element_type=F32)
        if add is not None:
            r = r + refs[2][...]
        refs[-1][...] = r.astype(out_dtype)

    def body(*refs):
        if add is None:
            a_ref, b_ref, o_ref, acc_ref = refs
        else:
            a_ref, b_ref, add_ref, o_ref, acc_ref = refs
        k = pl.program_id(2)

        @pl.when(k == 0)
        def _():
            acc_ref[...] = jnp.zeros_like(acc_ref)

        acc_ref[...] += lax.dot_general(a_ref[...], b_ref[...], dims, preferred_element_type=F32)

        @pl.when(k == nk - 1)
        def _():
            r = acc_ref[...]
            if add is not None:
                r = r + add_ref[...]
            o_ref[...] = r.astype(out_dtype)

    a_spec = BS((tk, tm), lambda i, j, k: (k, i)) if ta else BS((tm, tk), lambda i, j, k: (i, k))
    b_spec = BS((tn, tk), lambda i, j, k: (j, k)) if tb else BS((tk, tn), lambda i, j, k: (k, j))
    in_specs, args = [a_spec, b_spec], [a, b]
    if add is not None:
        in_specs.append(BS((tm, tn), lambda i, j, k: (i, j)))
        args.append(add)
    return _pc(body_single if nk == 1 else body, name=name, grid=(M // tm, N // tn, nk), in_specs=in_specs,
               out_specs=BS((tm, tn), lambda i, j, k: (i, j)), out_shape=SDS((M, N), out_dtype),
               scratch_shapes=[] if nk == 1 else [pltpu.VMEM((tm, tn), F32)])(*args)


def _dot(a, b, dims):
    return lax.dot_general(a.astype(BF16), b.astype(BF16), (dims, ((), ())), preferred_element_type=F32)


@jax.custom_vjp
def _nn(a, b):
    return _dot(a, b, ((1,), (0,)))


@jax.custom_vjp
def _nt(a, b):
    return _dot(a, b, ((1,), (1,)))


@jax.custom_vjp
def _tn(a, b):
    return _dot(a, b, ((0,), (0,)))


_nn.defvjp(lambda a, b: (_nn(a, b), (a, b)), lambda r, g: (_nt(g, r[1]), _tn(r[0], g)))
_nt.defvjp(lambda a, b: (_nt(a, b), (a, b)), lambda r, g: (_nn(g, r[1]), _tn(g, r[0])))
_tn.defvjp(lambda a, b: (_tn(a, b), (a, b)), lambda r, g: (_nt(r[1], g), _nn(r[0], g)))


def _hdot(m, x):
    hi = x.astype(BF16)
    r1 = x - hi.astype(F32)
    lo = r1.astype(BF16)
    lo2 = (r1 - lo.astype(F32)).astype(BF16)
    n = x.shape[1]
    out = lax.dot_general(m.astype(BF16), jnp.concatenate([hi, lo, lo2], axis=1), (((1,), (0,)), ((), ())),
                          preferred_element_type=F32)
    return out[:, :n] + out[:, n:2 * n] + out[:, 2 * n:]


@jax.custom_vjp
def _cumdot(m, mt, x):
    return _hdot(m, x)


_cumdot.defvjp(lambda m, mt, x: (_hdot(m, x), (m, mt)),
               lambda r, g: (jnp.zeros_like(r[0]), jnp.zeros_like(r[1]), _hdot(r[1], g)))


def _sigmoid(x):
    return 1.0 / (1.0 + jnp.exp(-x))


def _softplus(x):
    return jnp.maximum(x, 0.0) + jnp.log(1.0 + jnp.exp(-jnp.abs(x)))


def _rms(x, w):
    return x * lax.rsqrt(jnp.mean(x * x, axis=-1, keepdims=True) + NORM_EPS) * w


def _norm_mm(x2, nw, b, *, name, out_dtype=F32, tm=1024, tn=1024):
    T, D = x2.shape
    N = b.shape[1]
    tm, tn = _div_tile(T, tm, LANE), _div_tile(N, tn, LANE)

    def body(x_ref, w_ref, b_ref, h_ref, o_ref):
        @pl.when(pl.program_id(1) == 0)
        def _():
            h_ref[...] = _rms(x_ref[...], w_ref[...]).astype(BF16)

        o_ref[...] = lax.dot_general(h_ref[...], b_ref[...], (((1,), (0,)), ((), ())), preferred_element_type=F32).astype(out_dtype)

    return _pc(body, name=name, grid=(T // tm, N // tn),
               in_specs=[BS((tm, D), lambda i, j: (i, 0)), BS((1, D), lambda i, j: (0, 0)), BS((D, tn), lambda i, j: (0, j))],
               out_specs=[BS((tm, D), lambda i, j: (i, 0)), BS((tm, tn), lambda i, j: (i, j))],
               out_shape=[SDS((T, D), BF16), SDS((T, N), out_dtype)])(x2, nw.reshape(1, D), b)


def _dnorm(dh, x2, nw, resid, *, name):
    T, D = x2.shape
    tr = _div_tile(T, 512, 16)

    def body(x_ref, w_ref, dh_ref, r_ref, dx_ref, dxb_ref, dw_ref):
        _, vjp = jax.vjp(_rms, x_ref[...], w_ref[...])
        dx, dw = vjp(dh_ref[...])
        dx = dx + r_ref[...]
        dx_ref[...] = dx
        dxb_ref[...] = dx.astype(BF16)

        @pl.when(pl.program_id(0) == 0)
        def _():
            dw_ref[...] = jnp.zeros_like(dw_ref)

        dw_ref[...] += dw

    row = BS((tr, D), lambda i: (i, 0))
    one = BS((1, D), lambda i: (0, 0))
    return _pc(body, name=name, grid=(T // tr,), in_specs=[row, one, row, row], out_specs=[row, row, one],
               out_shape=[SDS((T, D), F32), SDS((T, D), BF16), SDS((1, D), F32)])(x2, nw.reshape(1, D), dh, resid)


ROW_PAD = 8


def _pad_rows(x):
    return jnp.concatenate([x, jnp.zeros((ROW_PAD, x.shape[1]), x.dtype)], axis=0)


def _shift_rows(xp, s):
    n = xp.shape[0] - ROW_PAD
    return xp[:n] if s == 0 else pltpu.roll(xp, (-s) % xp.shape[0], 0)[:n]


def _conv_taps(x, taps):
    xp = _pad_rows(x)
    return [_shift_rows(xp, k - taps // 2) for k in range(taps)]


def _conv_pre(xs, w_ref, b_ref):
    c = b_ref[...] + w_ref[0:1, :] * xs[0]
    for k in range(1, len(xs)):
        c = c + w_ref[k:k + 1, :] * xs[k]
    return c


def _conv_fwd(x3, x_blk0, w, b, *, taps, ct, gate_blk0=None, out_dtype, name):
    B, L, _ = x3.shape
    C = w.shape[1]
    wp = jnp.zeros((8, C), F32).at[:taps].set(w)

    def body(*refs):
        if gate_blk0 is None:
            x_ref, w_ref, b_ref, o_ref = refs
        else:
            x_ref, u_ref, w_ref, b_ref, o_ref = refs
        c = _conv_pre(_conv_taps(x_ref[0].astype(F32), taps), w_ref, b_ref)
        y = c * _sigmoid(c)
        if gate_blk0 is not None:
            y = y * u_ref[0].astype(F32)
        o_ref[0] = y.astype(out_dtype)

    in_specs = [BS((1, L, ct), lambda bi, j: (bi, 0, x_blk0 + j))]
    args = [x3]
    if gate_blk0 is not None:
        in_specs.append(BS((1, L, ct), lambda bi, j: (bi, 0, gate_blk0 + j)))
        args.append(x3)
    in_specs += [BS((8, ct), lambda bi, j: (0, j)), BS((1, ct), lambda bi, j: (0, j))]
    args += [wp, b.reshape(1, C)]
    return _pc(body, name=name, grid=(B, C // ct), in_specs=in_specs,
               out_specs=BS((1, L, ct), lambda bi, j: (bi, 0, j)), out_shape=SDS((B, L, C), out_dtype))(*args)


def _conv_bwd(x3, x_blk0, w, b, dy3, *, taps, ct, gate_blk0=None, name):
    B, L, _ = x3.shape
    C = w.shape[1]
    wp = jnp.zeros((8, C), F32).at[:taps].set(w)
    gated = gate_blk0 is not None

    def body(*refs):
        if gated:
            x_ref, u_ref, w_ref, b_ref, dy_ref, dx_ref, du_ref, dw_ref, db_ref = refs
        else:
            x_ref, w_ref, b_ref, dy_ref, dx_ref, dw_ref, db_ref = refs
        xs = _conv_taps(x_ref[0].astype(F32), taps)
        dy = dy_ref[0].astype(F32)
        c = _conv_pre(xs, w_ref, b_ref)
        sg = _sigmoid(c)
        dsilu = sg * (1.0 + c * (1.0 - sg))
        if gated:
            du_ref[0] = (dy * (c * sg)).astype(BF16)
            dc = dy * u_ref[0].astype(F32) * dsilu
        else:
            dc = dy * dsilu
        dcp = _pad_rows(dc)
        dx = jnp.zeros_like(dc)
        dw_ref[0] = jnp.zeros((8, ct), F32)
        for k in range(taps):
            dx = dx + w_ref[k:k + 1, :] * _shift_rows(dcp, taps // 2 - k)
            dw_ref[0, k:k + 1, :] = jnp.sum(dc * xs[k], axis=0, keepdims=True)
        dx_ref[0] = dx.astype(BF16)
        db_ref[0] = jnp.sum(dc, axis=0, keepdims=True)

    xs = BS((1, L, ct), lambda bi, j: (bi, 0, x_blk0 + j))
    ys = BS((1, L, ct), lambda bi, j: (bi, 0, j))
    in_specs, args = [xs], [x3]
    if gated:
        in_specs.append(BS((1, L, ct), lambda bi, j: (bi, 0, gate_blk0 + j)))
        args.append(x3)
    in_specs += [BS((8, ct), lambda bi, j: (0, j)), BS((1, ct), lambda bi, j: (0, j)), ys]
    args += [wp, b.reshape(1, C), dy3]
    out_specs = [ys] + ([ys] if gated else []) + [BS((1, 8, ct), lambda bi, j: (bi, 0, j)), BS((1, 1, ct), lambda bi, j: (bi, 0, j))]
    out_shape = [SDS((B, L, C), BF16)] + ([SDS((B, L, C), BF16)] if gated else []) + [SDS((B, 8, C), F32), SDS((B, 1, C), F32)]
    return _pc(body, name=name, grid=(B, C // ct), in_specs=in_specs, out_specs=out_specs, out_shape=out_shape)(*args)


def _tri(reverse):
    r = lax.broadcasted_iota(jnp.int32, (CHUNK, CHUNK), 0)
    c = lax.broadcasted_iota(jnp.int32, (CHUNK, CHUNK), 1)
    return (c >= r) if reverse else (c <= r)


PAIRS = 2
QUADS = SSM_HEADS // (2 * PAIRS)
QW = PAIRS * LANE


def _ssd_chunk(h0, h1, x0, x1, bm, cm, dtc, alog, *, col0, reverse):
    mask = _tri(reverse)
    eye = lax.broadcasted_iota(jnp.int32, (CHUNK, CHUNK), 0) == lax.broadcasted_iota(jnp.int32, (CHUNK, CHUNK), 1)
    lane = lax.broadcasted_iota(jnp.int32, (1, LANE), 1)
    first = lane < HEAD_DIM
    adt = dtc * (-jnp.exp(alog))
    cumc = _cumdot(mask.astype(F32), _tri(not reverse).astype(F32), adt)
    totc = jnp.sum(adt, axis=0, keepdims=True)
    cb = _nt(cm, bm)

    def col(v, c):
        return jnp.sum(jnp.where(lane == c, v, 0.0), axis=1, keepdims=True)

    outs, states = [], []
    for p, (hprev, xs) in enumerate(((h0, x0), (h1, x1))):
        c0 = col0 + 2 * p
        cj = (col(cumc, c0), col(cumc, c0 + 1))
        cum = jnp.where(first, cj[0], cj[1])
        tot = jnp.where(first, col(totc, c0), col(totc, c0 + 1))
        xdt = xs * jnp.where(first, col(dtc, c0), col(dtc, c0 + 1))
        y = _nn(cm, hprev) * jnp.exp(cum)
        for j in range(2):
            rj = jnp.sum(jnp.where(eye, cj[j], 0.0), axis=0, keepdims=True)
            dec = jnp.exp(jnp.where(mask, cj[j] - rj, NEG))
            y = y + _nn(cb * dec, jnp.where(first if j == 0 else ~first, xdt, 0.0))
        outs.append(y)
        states.append(hprev * jnp.exp(tot) + _tn(bm, xdt * jnp.exp(tot - cum)))
    return outs[0], outs[1], states[0], states[1]


def _ssd_specs(B, L):
    def lanes(w, blk):
        return BS((1, L, w), blk)

    return [
        lanes(QW, lambda b, q: (b, 0, q)),
        lanes(LANE, lambda b, q: (b, 0, 8 + q // 2)),
        lanes(LANE, lambda b, q: (b, 0, 10 + q // 2)),
        lanes(LANE, lambda b, q: (b, 0, 0)),
        BS((1, LANE), lambda b, q: (0, 0)),
        BS((1, QW), lambda b, q: (0, q)),
    ]


def _ssd_slot(d, ci):
    return ci if d == 0 else ci + 1


def _ssd_fwd(xbc_act, dtc, alog, dskip, *, name, rider=None):
    B, L, _ = xbc_act.shape
    nc = L // CHUNK

    def body(xs_ref, b_ref, c_ref, dt_ref, alog_ref, dsk_ref, y_ref, hs_ref):
        q = pl.program_id(1)
        alog_v = alog_ref[...]
        y_ref[0] = dsk_ref[...] * xs_ref[0]
        hs_ref[0, 0, 0, 0] = jnp.zeros((LANE, QW), F32)
        hs_ref[0, 0, 1, nc] = jnp.zeros((LANE, QW), F32)

        def step(i, carry):
            cis = (i, nc - 1 - i)
            rows = [pl.ds(pl.multiple_of(ci * CHUNK, CHUNK), CHUNK) for ci in cis]
            res = []
            for d in range(2):
                cur = _ssd_slot(d, cis[d])
                res.append(_ssd_chunk(
                    hs_ref[0, 0, d, cur, :, :LANE], hs_ref[0, 0, d, cur, :, LANE:], xs_ref[0, rows[d], :LANE],
                    xs_ref[0, rows[d], LANE:], b_ref[0, rows[d], :], c_ref[0, rows[d], :], dt_ref[0, rows[d], :], alog_v,
                    col0=SSM_HEADS * d + 2 * PAIRS * q, reverse=d == 1))
            for d in range(2):
                y0, y1, n0, n1 = res[d]
                nxt = _ssd_slot(d, cis[d] + 1 if d == 0 else cis[d] - 1)
                hs_ref[0, 0, d, nxt, :, :LANE] = n0
                hs_ref[0, 0, d, nxt, :, LANE:] = n1
                y_ref[0, rows[d], :LANE] += y0
                y_ref[0, rows[d], LANE:] += y1
            return carry

        lax.fori_loop(0, nc, step, 0, unroll=2)

    (y, hs), carried = _pc_carry(
        body, (xbc_act, xbc_act, xbc_act, dtc, alog, dskip), name=name, grid=(B, QUADS), in_specs=_ssd_specs(B, L),
        out_specs=[BS((1, L, QW), lambda b, q: (b, 0, q)), BS((1, 1, 2, nc + 1, LANE, QW), lambda b, q: (b, q, 0, 0, 0, 0))],
        out_shape=[SDS((B, L, SSM_WIDTH), F32), SDS((B, QUADS, 2, nc + 1, LANE, QW), F32)], rider=rider)
    return y, hs, carried


def _ssd_bwd(xbc_act, dtc, alog, dskip, hs, dy, *, name, rider=None):
    B, L, _ = xbc_act.shape
    nc = L // CHUNK

    def body(xs_ref, b_ref, c_ref, dt_ref, alog_ref, dsk_ref, hs_ref, dy_ref,
             dxs_ref, db_ref, dc_ref, ddt_ref, dalog_ref, ddsk_ref, dh_ref):
        q = pl.program_id(1)
        alog_v = alog_ref[...]

        @pl.when(q % 2 == 0)
        def _():
            db_ref[...] = jnp.zeros_like(db_ref)
            dc_ref[...] = jnp.zeros_like(dc_ref)

        @pl.when(q == 0)
        def _():
            ddt_ref[...] = jnp.zeros_like(ddt_ref)

        dxs_ref[0] = dy_ref[0] * dsk_ref[...]
        ddsk_ref[0] = jnp.sum(dy_ref[0] * xs_ref[0], axis=0, keepdims=True)
        dh_ref[...] = jnp.zeros_like(dh_ref)

        def step(i, carry):
            g_alog = carry
            cis = (nc - 1 - i, i)
            rows = [pl.ds(pl.multiple_of(ci * CHUNK, CHUNK), CHUNK) for ci in cis]
            res = []
            for d in range(2):
                cur = _ssd_slot(d, cis[d])
                fn = functools.partial(_ssd_chunk, col0=SSM_HEADS * d + 2 * PAIRS * q, reverse=d == 1)
                _, vjp = jax.vjp(fn, hs_ref[0, 0, d, cur, :, :LANE], hs_ref[0, 0, d, cur, :, LANE:], xs_ref[0, rows[d], :LANE],
                                 xs_ref[0, rows[d], LANE:], b_ref[0, rows[d], :], c_ref[0, rows[d], :], dt_ref[0, rows[d], :],
                                 alog_v)
                res.append(vjp((dy_ref[0, rows[d], :LANE], dy_ref[0, rows[d], LANE:], dh_ref[d, :, :LANE], dh_ref[d, :, LANE:])))
            for d in range(2):
                g_h0, g_h1, g_x0, g_x1, g_b, g_c, g_dt, g_alog1 = res[d]
                dh_ref[d, :, :LANE] = g_h0
                dh_ref[d, :, LANE:] = g_h1
                dxs_ref[0, rows[d], :LANE] += g_x0
                dxs_ref[0, rows[d], LANE:] += g_x1
                db_ref[0, rows[d], :] += g_b
                dc_ref[0, rows[d], :] += g_c
                ddt_ref[0, rows[d], :] += g_dt
                g_alog = g_alog + g_alog1
            return g_alog

        dalog_ref[0, 0] = lax.fori_loop(0, nc, step, jnp.zeros((1, LANE), F32))

    lanes = lambda w, blk: BS((1, L, w), blk)
    in_specs = _ssd_specs(B, L) + [BS((1, 1, 2, nc + 1, LANE, QW), lambda b, q: (b, q, 0, 0, 0, 0)), lanes(QW, lambda b, q: (b, 0, q))]
    out_specs = [lanes(QW, lambda b, q: (b, 0, q)), lanes(LANE, lambda b, q: (b, 0, q // 2)), lanes(LANE, lambda b, q: (b, 0, q // 2)),
                 lanes(LANE, lambda b, q: (b, 0, 0)), BS((1, 1, 1, LANE), lambda b, q: (b, q, 0, 0)),
                 BS((1, 1, QW), lambda b, q: (b, 0, q))]
    out_shape = [SDS((B, L, CONV_CH), F32), SDS((B, L, BC_WIDTH), F32), SDS((B, L, BC_WIDTH), F32), SDS((B, L, LANE), F32),
                 SDS((B, QUADS, 1, LANE), F32), SDS((B, 1, SSM_WIDTH), F32)]
    outs, carried = _pc_carry(body, (xbc_act, xbc_act, xbc_act, dtc, alog, dskip, hs, dy), name=name, grid=(B, QUADS),
                              in_specs=in_specs, out_specs=out_specs, out_shape=out_shape,
                              scratch_shapes=[pltpu.VMEM((2, LANE, QW), F32)], rider=rider)
    return (*outs, carried)


def _gate_norm(yp, z, w):
    v = yp * (z * _sigmoid(z))
    return v * lax.rsqrt(jnp.mean(v * v, axis=-1, keepdims=True) + NORM_EPS) * w


def _gate_fwd(ypre2, proj2, w, *, name):
    T = ypre2.shape[0]
    tr = _div_tile(T, 512, 8)
    G = 512

    def body(y_ref, z_ref, w_ref, o_ref):
        o_ref[...] = _gate_norm(y_ref[...], z_ref[...], w_ref[...]).astype(BF16)

    return _pc(body, name=name, grid=(T // tr, 2),
               in_specs=[BS((tr, G), lambda i, g: (i, g)), BS((tr, G), lambda i, g: (i, 2 + g)), BS((1, G), lambda i, g: (0, g))],
               out_specs=BS((tr, G), lambda i, g: (i, g)), out_shape=SDS((T, 2 * SSM_WIDTH), BF16))(ypre2, proj2, w.reshape(1, -1))


def _gate_bwd(ypre2, proj2, w, dy, *, name):
    T = ypre2.shape[0]
    tr = _div_tile(T, 512, 8)
    G = 512

    def body(y_ref, z_ref, w_ref, dy_ref, dyp_ref, dz_ref, dw_ref):
        _, vjp = jax.vjp(_gate_norm, y_ref[...], z_ref[...], w_ref[...])
        dyp, dz, dw = vjp(dy_ref[...])
        dyp_ref[...] = dyp
        dz_ref[...] = dz.astype(BF16)
        dw_ref[0] = dw

    tile = BS((tr, G), lambda i, g: (i, g))
    return _pc(body, name=name, grid=(T // tr, 2),
               in_specs=[tile, BS((tr, G), lambda i, g: (i, 2 + g)), BS((1, G), lambda i, g: (0, g)), tile],
               out_specs=[tile, tile, BS((1, 1, G), lambda i, g: (i, 0, g))],
               out_shape=[SDS((T, SSM_WIDTH), F32), SDS((T, SSM_WIDTH), BF16), SDS((T // tr, 1, SSM_WIDTH), F32)])(
        ypre2, proj2, w.reshape(1, -1), dy)


def _first_half():
    return lax.broadcasted_iota(jnp.int32, (1, LANE), 1) < HEAD_DIM


def _dup_kv_head(pair, odd):
    rolled = pltpu.roll(pair, HEAD_DIM, 1)
    return jnp.where(_first_half(), rolled, pair) if odd else jnp.where(_first_half(), pair, rolled)


def _stack_heads(quad):
    first = _first_half()
    lo, hi = quad[:, :LANE], quad[:, LANE:]
    return jnp.concatenate([jnp.where(first, lo, 0.0), jnp.where(first, 0.0, lo), jnp.where(first, hi, 0.0),
                            jnp.where(first, 0.0, hi)], axis=0)


def _unstack_heads(o):
    first = _first_half()
    return jnp.concatenate([jnp.where(first, o[:BLOCK], o[BLOCK:2 * BLOCK]), jnp.where(first, o[2 * BLOCK:3 * BLOCK], o[3 * BLOCK:])], axis=1)


def _fold_kv_head(d, odd):
    tot = d + pltpu.roll(d, HEAD_DIM, 1)
    return jnp.where(_first_half(), 0.0, tot) if odd else jnp.where(_first_half(), tot, 0.0)


def _attn_softmax(s, sink):
    m = jnp.maximum(jnp.max(s, axis=-1, keepdims=True), sink)
    p = jnp.exp(s - m)
    ps = jnp.exp(sink - m)
    inv = 1.0 / (jnp.sum(p, axis=-1, keepdims=True) + ps)
    return p * inv, ps * inv


def _attn_colneg(n, L):
    kpos = n * BLOCK - WINDOW + lax.broadcasted_iota(jnp.int32, (1, KEY_SPAN), 1)
    return jnp.where((kpos >= 0) & (kpos < L), 0.0, NEG)


def _attn_in_specs(L):
    nblk = L // BLOCK
    kv = lambda o, col: BS((1, BLOCK, 4 * HEAD_DIM), lambda b, n: (b, jnp.clip(n + o, 0, nblk - 1), col))
    kcol, vcol = 3584 // 256, 3840 // 256
    return [BS((1, BLOCK, ATTN_HEADS * HEAD_DIM), lambda b, n: (b, n, 0)), kv(-1, kcol), kv(0, kcol), kv(1, kcol),
            kv(-1, vcol), kv(0, vcol), kv(1, vcol),
            BS((ATTN_HEADS, BLOCK, KEY_SPAN), lambda b, n: (0, 0, 0)), BS((ATTN_HEADS * BLOCK, 1), lambda b, n: (0, 0))]


def _attn_fwd(proj, bias, sinkcol, mixed, *, name, rider=None):
    B, L, _ = proj.shape

    def body(q_ref, k0, k1, k2, v0, v1, v2, bias_ref, sink_ref, _, o_ref):
        colneg = _attn_colneg(pl.program_id(1), L)
        kcat = jnp.concatenate([k0[0], k1[0], k2[0]], axis=0)
        vcat = jnp.concatenate([v0[0], v1[0], v2[0]], axis=0)
        scores, probs, scales = [], [], []
        for g in range(KV_HEADS):
            pair = slice(LANE * (g // 2), LANE * (g // 2) + LANE)
            quad = slice(4 * HEAD_DIM * g, 4 * HEAD_DIM * (g + 1))
            kd = _dup_kv_head(kcat[:, pair], g % 2).astype(BF16)
            qs = (_stack_heads(q_ref[0, :, quad]) * HEAD_DIM ** -0.5).astype(BF16)
            scores.append(lax.dot_general(qs, kd, (((1,), (1,)), ((), ())), preferred_element_type=F32))
        for g in range(KV_HEADS):
            s = scores[g] + bias_ref[4 * g:4 * g + 4].reshape(4 * BLOCK, KEY_SPAN) + colneg
            sink = sink_ref[4 * BLOCK * g:4 * BLOCK * (g + 1)]
            m = jnp.maximum(jnp.max(s, axis=-1, keepdims=True), sink)
            p = jnp.exp(s - m)
            scales.append(1.0 / (jnp.sum(p, axis=-1, keepdims=True) + jnp.exp(sink - m)))
            probs.append(p.astype(BF16))
        for g in range(KV_HEADS):
            pair = slice(LANE * (g // 2), LANE * (g // 2) + LANE)
            quad = slice(4 * HEAD_DIM * g, 4 * HEAD_DIM * (g + 1))
            vd = _dup_kv_head(vcat[:, pair], g % 2).astype(BF16)
            o = lax.dot_general(probs[g], vd, (((1,), (0,)), ((), ())), preferred_element_type=F32) * scales[g]
            o_ref[0, :, quad] = _unstack_heads(o).astype(BF16)

    (out,), carried = _pc_carry(body, (proj, proj, proj, proj, proj, proj, proj, bias, sinkcol, mixed), name=name,
                                grid=(B, L // BLOCK), in_specs=_attn_in_specs(L) + [BS(memory_space=pl.ANY)],
                                out_specs=[BS((1, BLOCK, ATTN_HEADS * HEAD_DIM), lambda b, n: (b, n, 1))],
                                out_shape=[SDS(mixed.shape, BF16)], rider=rider, aliases={9: 0})
    return out, carried


def _attn_bwd(proj, bias, sinkcol, dout, *, name, rider=None):
    B, L, _ = proj.shape
    nblk = L // BLOCK
    nn, nt, tn = (((1,), (0,)), ((), ())), (((1,), (1,)), ((), ())), (((0,), (0,)), ((), ()))

    def body(q_ref, k0, k1, k2, v0, v1, v2, bias_ref, sink_ref, do_ref, dq_ref, dk_ref, dv_ref, dbias_ref, dsink_ref):
        b, n = pl.program_id(0), pl.program_id(1)

        @pl.when(n == 0)
        def _():
            dk_ref[...] = jnp.zeros_like(dk_ref)
            dv_ref[...] = jnp.zeros_like(dv_ref)

        @pl.when((n == 0) & (b == 0))
        def _():
            dbias_ref[...] = jnp.zeros_like(dbias_ref)
            dsink_ref[...] = jnp.zeros_like(dsink_ref)

        colneg = _attn_colneg(n, L)
        kcat = jnp.concatenate([k0[0], k1[0], k2[0]], axis=0)
        vcat = jnp.concatenate([v0[0], v1[0], v2[0]], axis=0)
        krows = [pl.ds(pl.multiple_of(jnp.clip(n + o, 0, nblk - 1) * BLOCK, BLOCK), BLOCK) for o in (-1, 0, 1)]
        ops, mids = [], []
        for g in range(KV_HEADS):
            pair = slice(LANE * (g // 2), LANE * (g // 2) + LANE)
            quad = slice(4 * HEAD_DIM * g, 4 * HEAD_DIM * (g + 1))
            kd = _dup_kv_head(kcat[:, pair], g % 2).astype(BF16)
            vd = _dup_kv_head(vcat[:, pair], g % 2).astype(BF16)
            qs = (_stack_heads(q_ref[0, :, quad]) * HEAD_DIM ** -0.5).astype(BF16)
            dos = _stack_heads(do_ref[0, :, quad].astype(F32)).astype(BF16)
            ops.append((kd, qs, dos, lax.dot_general(qs, kd, nt, preferred_element_type=F32),
                        lax.dot_general(dos, vd, nt, preferred_element_type=F32)))
        for g in range(KV_HEADS):
            rows = slice(4 * BLOCK * g, 4 * BLOCK * (g + 1))
            _, _, _, s, dpn = ops[g]
            pn, psink = _attn_softmax(s + bias_ref[4 * g:4 * g + 4].reshape(4 * BLOCK, KEY_SPAN) + colneg, sink_ref[rows])
            r = jnp.sum(dpn * pn, axis=-1, keepdims=True)
            ds = pn * (dpn - r)
            dbias_ref[4 * g:4 * g + 4] += ds.reshape(4, BLOCK, KEY_SPAN)
            dsink_ref[rows] += -psink * r
            mids.append((pn.astype(BF16), ds.astype(BF16)))
        for g in range(KV_HEADS):
            pair = slice(LANE * (g // 2), LANE * (g // 2) + LANE)
            quad = slice(4 * HEAD_DIM * g, 4 * HEAD_DIM * (g + 1))
            kd, qs, dos, _, _ = ops[g]
            pnb, dsb = mids[g]
            dvd = lax.dot_general(pnb, dos, tn, preferred_element_type=F32)
            dkd = lax.dot_general(dsb, qs, tn, preferred_element_type=F32)
            dqs = lax.dot_general(dsb, kd, nn, preferred_element_type=F32) * HEAD_DIM ** -0.5
            dq_ref[0, :, quad] = _unstack_heads(dqs).astype(BF16)
            dk_g, dv_g = _fold_kv_head(dkd, g % 2), _fold_kv_head(dvd, g % 2)
            for o in range(3):
                dk_ref[0, krows[o], pair] += dk_g[o * BLOCK:(o + 1) * BLOCK]
                dv_ref[0, krows[o], pair] += dv_g[o * BLOCK:(o + 1) * BLOCK]

    qspec = BS((1, BLOCK, ATTN_HEADS * HEAD_DIM), lambda b, n: (b, n, 0))
    kvout = BS((1, L, 4 * HEAD_DIM), lambda b, n: (b, 0, 0))
    outs, carried = _pc_carry(
        body, (proj, proj, proj, proj, proj, proj, proj, bias, sinkcol, dout), name=name, grid=(B, nblk),
        in_specs=_attn_in_specs(L) + [BS((1, BLOCK, ATTN_HEADS * HEAD_DIM), lambda b, n: (b, n, 1))],
        out_specs=[qspec, kvout, kvout, BS((ATTN_HEADS, BLOCK, KEY_SPAN), lambda b, n: (0, 0, 0)),
                   BS((ATTN_HEADS * BLOCK, 1), lambda b, n: (0, 0))],
        out_shape=[SDS((B, L, ATTN_HEADS * HEAD_DIM), BF16), SDS((B, L, 4 * HEAD_DIM), F32), SDS((B, L, 4 * HEAD_DIM), F32),
                   SDS((ATTN_HEADS, BLOCK, KEY_SPAN), F32), SDS((ATTN_HEADS * BLOCK, 1), F32)], rider=rider)
    return (*outs, carried)


def _t5_bucket(rel):
    half = REL_BUCKETS // 2
    max_exact = half // 2
    ret = jnp.where(rel > 0, half, 0)
    n = jnp.abs(rel)
    nf = jnp.maximum(n, 1).astype(F32)
    large = max_exact + (jnp.log(nf / max_exact) / math.log(REL_MAX_DIST / max_exact) * (half - max_exact)).astype(jnp.int32)
    large = jnp.minimum(large, half - 1)
    return ret + jnp.where(n < max_exact, n, large)


def _bucket_table():
    rel = jnp.arange(KEY_SPAN)[None, :] - WINDOW - jnp.arange(BLOCK)[:, None]
    return _t5_bucket(rel).astype(jnp.int32)


def _bias_expand(rel_bias, bucket, *, name):
    rbt = jnp.zeros((ATTN_HEADS, 1, LANE), F32).at[:, 0, :REL_BUCKETS].set(rel_bias.T)

    def body(rb_ref, bk_ref, o_ref):
        lane = lax.broadcasted_iota(jnp.int32, (1, LANE), 1)
        row = rb_ref[0]
        bk = bk_ref[...]
        acc = jnp.zeros((BLOCK, KEY_SPAN), F32)
        for r in range(REL_BUCKETS):
            val = jnp.sum(jnp.where(lane == r, row, 0.0), axis=1, keepdims=True)
            acc = jnp.where(bk == r, val, acc)
        rel = (lax.broadcasted_iota(jnp.int32, (BLOCK, KEY_SPAN), 1) - WINDOW
               - lax.broadcasted_iota(jnp.int32, (BLOCK, KEY_SPAN), 0))
        o_ref[0] = jnp.where(jnp.abs(rel) <= WINDOW, acc, NEG)

    return _pc(body, name=name, grid=(ATTN_HEADS,),
               in_specs=[BS((1, 1, LANE), lambda h: (h, 0, 0)), BS((BLOCK, KEY_SPAN), lambda h: (0, 0))],
               out_specs=BS((1, BLOCK, KEY_SPAN), lambda h: (h, 0, 0)), out_shape=SDS((ATTN_HEADS, BLOCK, KEY_SPAN), F32))(rbt, bucket)


def _bias_reduce(dbias, bucket, *, name):
    def body(db_ref, bk_ref, o_ref):
        lane = lax.broadcasted_iota(jnp.int32, (1, LANE), 1)
        x = db_ref[0]
        bk = bk_ref[...]
        acc = jnp.zeros((1, LANE), F32)
        for r in range(REL_BUCKETS):
            part = jnp.sum(jnp.where(bk == r, x, 0.0), axis=1, keepdims=True)
            acc = jnp.where(lane == r, jnp.sum(part, axis=0, keepdims=True), acc)
        o_ref[0] = acc

    out = _pc(body, name=name, grid=(ATTN_HEADS,),
              in_specs=[BS((1, BLOCK, KEY_SPAN), lambda h: (h, 0, 0)), BS((BLOCK, KEY_SPAN), lambda h: (0, 0))],
              out_specs=BS((1, 1, LANE), lambda h: (h, 0, 0)), out_shape=SDS((ATTN_HEADS, 1, LANE), F32))(dbias, bucket)
    return out[:, 0, :REL_BUCKETS].T


def _loss_head(x2, w, target, *, name):
    T, D = x2.shape
    tr = _div_tile(T, 512, 8)

    def tile_loss(x, w, t):
        err = _rms(x, w) - t
        return 0.5 * jnp.sum(jnp.mean(err * err, axis=-1, keepdims=True), axis=0, keepdims=True)

    def body(x_ref, w_ref, t_ref, loss_ref, dx_ref, dxb_ref, dw_ref):
        t = t_ref[...]
        l, vjp = jax.vjp(lambda x, w: tile_loss(x, w, t), x_ref[...], w_ref[...])
        dx, dw = vjp(jnp.ones((1, 1), F32))
        dx_ref[...] = dx
        dxb_ref[...] = dx.astype(BF16)

        @pl.when(pl.program_id(0) == 0)
        def _():
            dw_ref[...] = jnp.zeros_like(dw_ref)
            loss_ref[...] = jnp.zeros_like(loss_ref)

        dw_ref[...] += dw
        loss_ref[...] += l + jnp.zeros((1, LANE), F32)

    row = BS((tr, D), lambda i: (i, 0))
    one = BS((1, D), lambda i: (0, 0))
    return _pc(body, name=name, grid=(T // tr,), in_specs=[row, one, row],
               out_specs=[BS((1, LANE), lambda i: (0, 0)), row, row, one],
               out_shape=[SDS((1, LANE), F32), SDS((T, D), F32), SDS((T, D), BF16), SDS((1, D), F32)])(x2, w.reshape(1, D), target)


def _adamw(w2, g2, m2, v2, *, name):
    R, C = w2.shape
    tr = _div_tile(R, 256, 8)
    c1 = 1.0 - ADAM_B1 ** ADAM_STEP
    c2 = 1.0 - ADAM_B2 ** ADAM_STEP

    def body(w_ref, g_ref, m_ref, v_ref, d_ref, nm_ref, nv_ref):
        g = g_ref[...]
        m = ADAM_B1 * m_ref[...] + (1.0 - ADAM_B1) * g
        v = ADAM_B2 * v_ref[...] + (1.0 - ADAM_B2) * (g * g)
        d_ref[...] = -ADAM_LR * ((m / c1) / (jnp.sqrt(v / c2) + ADAM_EPS) + ADAM_WD * w_ref[...])
        nm_ref[...] = m
        nv_ref[...] = v

    t = BS((tr, C), lambda i: (i, 0))
    return _pc(body, name=name, grid=(R // tr,), in_specs=[t, t, t, t], out_specs=[t, t, t],
               out_shape=[SDS((R, C), F32)] * 3)(w2, g2, m2, v2)


def _place():
    return lax.axis_index("x"), lax.axis_index("y"), lax.axis_index("c")


def _gather_rider(shards):
    na = len(shards)

    def copies(ins, outs, sems):
        send_sems, recv_sems = sems
        x, y, c = _place()
        for a in range(na):
            for k, peer in enumerate([(1 - x, y, c), (x, 1 - y, c), (1 - x, 1 - y, c), (x, y, 1 - c)]):
                send = functools.partial(pltpu.make_async_remote_copy, ins[a], outs[a].at[2 * x + y], send_sems.at[a, k],
                                         recv_sems.at[a, k], device_id=peer, device_id_type=MESH)
                got = outs[a].at[2 * peer[0] + peer[1]]
                arrived = functools.partial(pltpu.make_async_remote_copy, got, got, send_sems.at[a, k], recv_sems.at[a, k],
                                            device_id=peer, device_id_type=MESH)
                yield send, arrived

    def start(ins, outs, sems):
        for send, _ in copies(ins, outs, sems):
            send().start()

    def finish(ins, outs, sems):
        both = list(copies(ins, outs, sems))
        for _, arrived in both:
            arrived().wait_recv()
        for send, _ in both:
            send().wait_send()

    return dict(ins=list(shards), prev=[], out_shape=[SDS((N_CHIP,) + s.shape, s.dtype) for s in shards],
                scratch=[pltpu.SemaphoreType.DMA((na, 4)), pltpu.SemaphoreType.DMA((na, 4))], start=start, finish=finish)


def _scatter_rider(bufs, layer, prev):
    na = len(bufs)
    h = layer // (DEPTH // 2)

    def copies(ins, outs, sems):
        send_sems, recv_sems, local_sems = sems
        x, y, c = _place()
        me = 4 * x + 2 * y + c
        for a in range(na):
            for j in range(N_CHIP):
                is_self = ((2 * x + y) == j) & (c == h)
                local = functools.partial(pltpu.make_async_copy, ins[a].at[j], outs[a].at[me], local_sems.at[a])
                remote = functools.partial(pltpu.make_async_remote_copy, ins[a].at[j], outs[a].at[me], send_sems.at[a, j],
                                           recv_sems.at[a, me], device_id=(j // 2, j % 2, h), device_id_type=MESH)
                yield is_self, local, remote

    def start(ins, outs, sems):
        for is_self, local, remote in copies(ins, outs, sems):
            pl.when(is_self)(lambda: local().start())
            pl.when(jnp.logical_not(is_self))(lambda: remote().start())

    def finish(ins, outs, sems):
        _, recv_sems, _ = sems
        x, y, c = _place()
        me = 4 * x + 2 * y + c
        for a in range(na):
            for s in range(N_DEV):
                got = outs[a].at[s]
                arrived = functools.partial(pltpu.make_async_remote_copy, got, got, recv_sems.at[a, s], recv_sems.at[a, s],
                                            device_id=(s // 4, (s // 2) % 2, s % 2), device_id_type=MESH)
                pl.when((c == h) & (me != s))(lambda: arrived().wait_recv())
        for is_self, local, remote in copies(ins, outs, sems):
            pl.when(is_self)(lambda: local().wait())
            pl.when(jnp.logical_not(is_self))(lambda: remote().wait_send())

    return dict(ins=list(bufs), prev=list(prev), out_shape=[SDS((N_DEV,) + b.shape[1:], b.dtype) for b in bufs],
                scratch=[pltpu.SemaphoreType.DMA((na, N_CHIP)), pltpu.SemaphoreType.DMA((na, N_DEV)), pltpu.SemaphoreType.DMA((na,))],
                start=start, finish=finish)


def _sum_sources(parts, parity, into, *, name):
    _, R, C = parts.shape
    tr = _div_tile(R, 256, 16)

    def body(p_ref, *rest):
        acc = p_ref[0].astype(F32)
        for s in range(1, N_DEV):
            acc = acc + p_ref[s].astype(F32)
        rest[-1][0] = acc

    prev = [] if into is None else [into]
    return _pc(body, name=name, grid=(R // tr,),
               in_specs=[BS((N_DEV, tr, C), lambda i: (0, i, 0))] + [BS(memory_space=pl.ANY)] * len(prev),
               out_specs=BS((1, tr, C), lambda i: ((DEPTH // 2) * lax.axis_index("c") + parity, i, 0)),
               out_shape=SDS((DEPTH, R, C), F32), aliases={1: 0} if prev else None)(parts, *prev)


def _join_halves(fulls, *, name):
    na = len(fulls)
    half = DEPTH // 2

    def body(*refs):
        outs = refs[na:2 * na]
        send_sems, recv_sems = refs[2 * na:]
        x, y, c = _place()
        cps = []
        for a in range(na):
            mine = outs[a].at[pl.ds(c * half, half)]
            cp = pltpu.make_async_remote_copy(mine, mine, send_sems.at[a], recv_sems.at[a],
                                              device_id=(x, y, 1 - c), device_id_type=MESH)
            cp.start()
            cps.append(cp)
        for a in range(na):
            theirs = outs[a].at[pl.ds((1 - c) * half, half)]
            pltpu.make_async_remote_copy(theirs, theirs, send_sems.at[a], recv_sems.at[a],
                                         device_id=(x, y, 1 - c), device_id_type=MESH).wait_recv()
        for cp in cps:
            cp.wait_send()

    any_spec = BS(memory_space=pl.ANY)
    return pl.pallas_call(
        body, name=name, in_specs=[any_spec] * na, out_specs=[any_spec] * na,
        out_shape=[SDS(f.shape, f.dtype) for f in fulls], input_output_aliases={a: a for a in range(na)},
        scratch_shapes=[pltpu.SemaphoreType.DMA((na,)), pltpu.SemaphoreType.DMA((na,))],
        compiler_params=pltpu.CompilerParams(has_side_effects=True))(*fulls)


def _allreduce_small(vec, *, name):
    R = vec.shape[0]

    def body(v_ref, o_ref, all_ref, send_sems, recv_sems):
        x, y, c = _place()
        me = 4 * x + 2 * y + c
        all_ref[me] = v_ref[...]
        sends = []
        for r in range(1, N_DEV):
            tgt = (x ^ (r >> 2), y ^ ((r >> 1) & 1), c ^ (r & 1))
            cp = pltpu.make_async_remote_copy(v_ref, all_ref.at[me], send_sems.at[r - 1], recv_sems.at[r - 1],
                                              device_id=tgt, device_id_type=MESH)
            cp.start()
            sends.append(cp)
        for r in range(1, N_DEV):
            tx, ty, tc = x ^ (r >> 2), y ^ ((r >> 1) & 1), c ^ (r & 1)
            got = all_ref.at[4 * tx + 2 * ty + tc]
            pltpu.make_async_remote_copy(got, got, send_sems.at[r - 1], recv_sems.at[r - 1],
                                         device_id=(tx, ty, tc), device_id_type=MESH).wait_recv()
        for cp in sends:
            cp.wait_send()
        acc = all_ref[0]
        for s in range(1, N_DEV):
            acc = acc + all_ref[s]
        o_ref[...] = acc

    vm = BS(memory_space=pltpu.VMEM)
    return pl.pallas_call(
        body, name=name, in_specs=[vm], out_specs=vm, out_shape=SDS((R, LANE), F32),
        scratch_shapes=[pltpu.VMEM((N_DEV, R, LANE), F32), pltpu.SemaphoreType.DMA((N_DEV - 1,)), pltpu.SemaphoreType.DMA((N_DEV - 1,))],
        compiler_params=pltpu.CompilerParams(has_side_effects=True, vmem_limit_bytes=VMEM_LIMIT_BYTES))(vec)


def _pack(arrs):
    rows = []
    for a in arrs:
        f = a.reshape(-1).astype(F32)
        n = -(-f.shape[0] // LANE) * LANE
        rows.append(jnp.pad(f, (0, n - f.shape[0])).reshape(-1, LANE))
    v = jnp.concatenate(rows, axis=0)
    pad = -v.shape[0] % 8
    return jnp.pad(v, ((0, pad), (0, 0)))


def _unpack(v, shapes):
    out, r = [], 0
    for s in shapes:
        n = int(np.prod(s)) if len(s) else 1
        nr = -(-n // LANE)
        out.append(v[r:r + nr].reshape(-1)[:n].reshape(s))
        r += nr
    return out


def _perm_in_cols(w_full):
    z, xbc, dt, q, k, v = (w_full[..., :Z_END], w_full[..., Z_END:XBC_END], w_full[..., XBC_END:DT_END],
                           w_full[..., DT_END:Q_END], w_full[..., Q_END:K_END], w_full[..., K_END:])
    pad = jnp.zeros(dt.shape[:-1] + (LANE - dt.shape[-1],), dt.dtype)
    return jnp.concatenate([q, z, xbc, k, v, dt, pad], axis=-1)


def _unperm_in_cols(g):
    q, z, xbc, k, v, dt = (g[..., :1024], g[..., 1024:2048], g[..., 2048:3584], g[..., 3584:3840], g[..., 3840:4096],
                           g[..., 4096:4096 + 2 * SSM_HEADS])
    return jnp.concatenate([z, xbc, dt, q, k, v], axis=-1)


def _dt_cols(a):
    return jnp.pad(a.reshape(1, 2 * SSM_HEADS), ((0, 0), (0, LANE - 2 * SSM_HEADS)))


def _dt_fwd(proj, dtb, *, name):
    B, L, _ = proj.shape

    def body(p_ref, b_ref, o_ref):
        o_ref[0] = _softplus(p_ref[0] + b_ref[...])

    return _pc(body, name=name, grid=(B,),
               in_specs=[BS((1, L, LANE), lambda b: (b, 0, P_COLS // LANE - 1)), BS((1, LANE), lambda b: (0, 0))],
               out_specs=BS((1, L, LANE), lambda b: (b, 0, 0)), out_shape=SDS((B, L, LANE), F32))(proj, dtb)


def _dt_bwd(proj, dtb, ddt, *, name):
    B, L, _ = proj.shape

    def body(p_ref, b_ref, g_ref, o_ref, db_ref):
        g = g_ref[0] * _sigmoid(p_ref[0] + b_ref[...])
        o_ref[0] = g.astype(BF16)
        db_ref[0] = jnp.sum(g, axis=0, keepdims=True)

    row = BS((1, L, LANE), lambda b: (b, 0, 0))
    return _pc(body, name=name, grid=(B,),
               in_specs=[BS((1, L, LANE), lambda b: (b, 0, P_COLS // LANE - 1)), BS((1, LANE), lambda b: (0, 0)), row],
               out_specs=[row, BS((1, 1, LANE), lambda b: (b, 0, 0))],
               out_shape=[SDS((B, L, LANE), BF16), SDS((B, 1, LANE), F32)])(proj, dtb, ddt)


def _layer_fwd(i, x, wts, small, band_bias, riders=None, arrived=None):
    riders = riders or {}
    B, L, D = x.shape
    T = B * L
    x2 = x.reshape(T, D)
    h, proj2 = _norm_mm(x2, small["norm1_w"][i], wts["w_in"], name=f"in_proj_{i}", tn=1408)
    proj = proj2.reshape(B, L, P_COLS)
    xbc_act = _conv_fwd(proj, 2048 // 256, small["conv_w"][i], small["conv_b"][i], taps=SSM_CONV, ct=256,
                        out_dtype=F32, name=f"ssm_conv_{i}")
    dtb, alog = _dt_cols(small["dt_bias"][i]), _dt_cols(small["a_log"][i])
    dskip = jnp.repeat(small["d_skip"][i], HEAD_DIM).reshape(1, SSM_WIDTH)
    dtc = _dt_fwd(proj, dtb, name=f"dt_{i}")
    ypre, hs, carried = _ssd_fwd(xbc_act, dtc, alog, dskip, name=f"ssd_{i}", rider=riders.get("ssd"))
    if carried is not None:
        arrived("ssd", carried)
    mixed = _gate_fwd(ypre.reshape(T, SSM_WIDTH), proj2, small["ssm_norm_w"][i], name=f"gate_{i}")
    sinkcol = jnp.repeat(small["attn_sink"][i], BLOCK).reshape(ATTN_HEADS * BLOCK, 1)
    mixed, carried = _attn_fwd(proj, band_bias, sinkcol, mixed.reshape(B, L, 2 * D), name=f"attn_{i}", rider=riders.get("attn"))
    if carried is not None:
        arrived("attn", carried)
    mixed = mixed.reshape(T, 2 * D)
    x_mid = _mm(mixed, wts["w_out"], add=x2, name=f"out_proj_{i}")
    h2, gu2 = _norm_mm(x_mid, small["norm2_w"][i], wts["w_up"], name=f"up_proj_{i}", out_dtype=BF16, tn=1408)
    gu = gu2.reshape(B, L, 2 * D_FF)
    act = _conv_fwd(gu, 0, small["ffn_conv_w"][i], small["ffn_conv_b"][i], taps=FFN_CONV, ct=256, gate_blk0=D_FF // 256,
                    out_dtype=BF16, name=f"ffn_conv_{i}")
    x_out = _mm(act.reshape(T, D_FF), wts["w_down"], add=x_mid, name=f"down_proj_{i}", tk=1408)
    saved = dict(x2=x2, h=h, proj2=proj2, xbc_act=xbc_act, dtb=dtb, dtc=dtc, alog=alog, dskip=dskip, ypre=ypre, hs=hs, mixed=mixed,
                 sinkcol=sinkcol, x_mid=x_mid, h2=h2, gu=gu, act=act)
    return x_out.reshape(B, L, D), saved


def _layer_bwd(i, dx_out, dxb, sv, wts, small, band_bias, attn_rider=None, ssd_rider=None, arrived=None, wgrad_dtype=F32):
    T, D = dx_out.shape
    B, L = sv["gu"].shape[:2]
    g = {}
    dact = _mm(dxb, wts["w_down"], tb=True, out_dtype=BF16, name=f"d_act_{i}", tn=1408)
    g["w_down"] = _mm(sv["act"].reshape(T, D_FF), dxb, ta=True, out_dtype=wgrad_dtype, name=f"dw_down_{i}", tm=1408)
    dg, du, dcw, dcb = _conv_bwd(sv["gu"], 0, small["ffn_conv_w"][i], small["ffn_conv_b"][i], dact.reshape(B, L, D_FF),
                                 taps=FFN_CONV, ct=256, gate_blk0=D_FF // 256, name=f"d_ffn_conv_{i}")
    g["ffn_conv_w"] = jnp.sum(dcw, axis=0)[:FFN_CONV]
    g["ffn_conv_b"] = jnp.sum(dcb, axis=(0, 1))
    dg, du, w_up = dg.reshape(T, D_FF), du.reshape(T, D_FF), wts["w_up"]
    dh2 = _mm(dg, w_up[:, :D_FF], tb=True, name=f"d_h2_g_{i}", tk=1408)
    dh2 = _mm(du, w_up[:, D_FF:], tb=True, add=dh2, name=f"d_h2_u_{i}", tk=1408)
    dx_mid, dmb, dw2 = _dnorm(dh2, sv["x_mid"], small["norm2_w"][i], dx_out, name=f"d_norm2_{i}")
    g["w_up"] = jnp.concatenate([_mm(sv["h2"], dg, ta=True, out_dtype=wgrad_dtype, name=f"dw_up_g_{i}", tn=1408),
                                 _mm(sv["h2"], du, ta=True, out_dtype=wgrad_dtype, name=f"dw_up_u_{i}", tn=1408)], axis=1)
    g["norm2_w"] = dw2[0]
    dmixed = _mm(dmb, wts["w_out"], tb=True, name=f"d_mixed_{i}")
    g["w_out"] = _mm(sv["mixed"], dmb, ta=True, out_dtype=wgrad_dtype, name=f"dw_out_{i}")
    dypre, dz, dwn = _gate_bwd(sv["ypre"].reshape(T, SSM_WIDTH), sv["proj2"], small["ssm_norm_w"][i], dmixed, name=f"d_gate_{i}")
    g["ssm_norm_w"] = jnp.sum(dwn, axis=(0, 1))
    proj = sv["proj2"].reshape(B, L, P_COLS)
    dxs, dbm, dcm, ddt, dalog, ddsk, carried = _ssd_bwd(sv["xbc_act"], sv["dtc"], sv["alog"], sv["dskip"], sv["hs"],
                                                        dypre.reshape(B, L, SSM_WIDTH), name=f"d_ssd_{i}",
                                                        rider=ssd_rider(g) if ssd_rider is not None else None)
    if carried is not None:
        arrived("ssd", carried)
    ddt, ddtb = _dt_bwd(proj, sv["dtb"], ddt, name=f"d_dt_{i}")
    g["dt_bias"] = jnp.sum(ddtb, axis=(0, 1))[:2 * SSM_HEADS].reshape(2, SSM_HEADS)
    g["a_log"] = jnp.sum(dalog, axis=(0, 1, 2))[:2 * SSM_HEADS].reshape(2, SSM_HEADS)
    g["d_skip"] = jnp.sum(ddsk.reshape(B, SSM_HEADS, HEAD_DIM), axis=(0, 2))
    dxbc_act = dxs.at[:, :, SSM_WIDTH:SSM_WIDTH + BC_WIDTH].set(dbm).at[:, :, SSM_WIDTH + BC_WIDTH:].set(dcm)
    dxbc, dcw, dcb = _conv_bwd(proj, 2048 // 256, small["conv_w"][i], small["conv_b"][i], dxbc_act, taps=SSM_CONV, ct=256,
                               name=f"d_ssm_conv_{i}")
    g["conv_w"] = jnp.sum(dcw, axis=0)[:SSM_CONV]
    g["conv_b"] = jnp.sum(dcb, axis=(0, 1))
    dq, dk, dv, dbias, dsink, carried = _attn_bwd(proj, band_bias, sv["sinkcol"], dmixed.reshape(B, L, 2 * D), name=f"d_attn_{i}",
                                                  rider=attn_rider)
    if carried is not None:
        arrived("attn", carried)
    g["attn_sink"] = jnp.sum(dsink.reshape(ATTN_HEADS, BLOCK), axis=1)
    dproj = jnp.concatenate([dq, dz.reshape(B, L, SSM_WIDTH), dxbc, dk.astype(BF16), dv.astype(BF16), ddt], axis=-1).reshape(T, P_COLS)
    dh = _mm(dproj, wts["w_in"], tb=True, name=f"d_h_{i}", tk=1408)
    dx_in, dx_in_b, dw1 = _dnorm(dh, sv["x2"], small["norm1_w"][i], dx_mid, name=f"d_norm1_{i}")
    g["w_in"] = _unperm_in_cols(_mm(sv["h"], dproj, ta=True, out_dtype=wgrad_dtype, name=f"dw_in_{i}", tn=1408))
    g["norm1_w"] = dw1[0]
    return dx_in, dx_in_b, g, dbias


_BIG = ("w_in", "w_out", "w_up", "w_down")
_BIG_AXIS = {"w_in": 2, "w_out": 1, "w_up": 2, "w_down": 1}
_SMALL = ("rel_bias", "norm1_w", "conv_w", "conv_b", "dt_bias", "a_log", "d_skip", "ssm_norm_w", "attn_sink", "norm2_w",
          "ffn_conv_w", "ffn_conv_b", "final_norm_w")
_SMALL_SHARDED = ("conv_w", "ffn_conv_w")
_ORDER = ("rel_bias", "norm1_w", "w_in", "conv_w", "conv_b", "dt_bias", "a_log", "d_skip", "ssm_norm_w", "attn_sink", "w_out",
          "norm2_w", "w_up", "ffn_conv_w", "ffn_conv_b", "w_down", "final_norm_w")


def _local_step(x, target, small, wts=None, exchange=None):
    B, L, D = x.shape
    bucket = _bucket_table()
    band_bias = _bias_expand(small["rel_bias"], bucket, name="band_bias")

    def fetch(spec):
        return exchange["gather"](spec) if exchange is not None and spec else None

    def fetched(spec, carried):
        for (i, k), full in zip(spec, exchange["weights"](spec, carried)):
            wts[i][k] = full

    if exchange is not None:
        wts = [{} for _ in range(DEPTH)]
        fetched([(0, "w_in")], _run_rider(fetch([(0, "w_in")]), name="gather_w_in_0"))
    saved = []
    for i in range(DEPTH):
        nxt = i + 1 < DEPTH
        if i == 0:
            plan = {"ssd": [(0, "w_out"), (0, "w_up"), (0, "w_down")], "attn": [(1, "w_in"), (1, "w_out"), (1, "w_up")] if nxt else []}
        else:
            plan = {"ssd": [(i, "w_down")] + ([(i + 1, "w_in"), (i + 1, "w_out")] if nxt else []), "attn": [(i + 1, "w_up")] if nxt else []}
        x, sv = _layer_fwd(i, x, wts[i], small, band_bias, {c: fetch(s) for c, s in plan.items()}, lambda c, r: fetched(plan[c], r))
        saved.append(sv)
    loss, dx, dxb, dwf = _loss_head(x.reshape(B * L, D), small["final_norm_w"], target.reshape(B * L, D), name="loss_head")
    per_layer = []
    dbias = jnp.zeros((ATTN_HEADS, BLOCK, KEY_SPAN), F32)
    late = None
    for i in reversed(range(DEPTH)):
        own = [(i, "w_down"), (i, "w_up"), (i, "w_out")]
        plan = {"ssd": own, "attn": late[0] if late else []}
        attn_rider = exchange["scatter"](*late) if late else None
        ssd_rider = (lambda g: exchange["scatter"](own, [g[k] for _, k in own])) if exchange is not None else None
        dx, dxb, g, dbias_i = _layer_bwd(i, dx, dxb, saved[i], wts[i], small, band_bias, attn_rider, ssd_rider,
                                    lambda c, r: exchange["collect"](plan[c], r), F32 if exchange is None else BF16)
        if exchange is not None:
            late = ([(i, "w_in")], [g["w_in"]])
            for k in _BIG:
                g.pop(k)
        dbias = dbias + dbias_i
        per_layer.append(g)
    if exchange is not None:
        exchange["collect"](late[0], _run_rider(exchange["scatter"](*late), name="scatter_dw_in_0"))
    per_layer.reverse()
    grads = {k: jnp.stack([g[k] for g in per_layer]) for k in per_layer[0]}
    grads["rel_bias"] = _bias_reduce(dbias, bucket, name="d_rel_bias")
    grads["final_norm_w"] = dwf[0]
    return loss, dx.reshape(B, L, D), grads


def _split_by_chip(g, axis):
    shp = g.shape
    n = shp[axis] // N_CHIP
    g = g.reshape(shp[:axis] + (N_CHIP, n) + shp[axis + 1:])
    return jnp.moveaxis(g, axis, 0)


def _join_chips(a, axis):
    a = jnp.moveaxis(a, 0, axis)
    shp = a.shape
    return a.reshape(shp[:axis] + (shp[axis] * shp[axis + 1],) + shp[axis + 2:])


def kernel(x, rel_bias, norm1_w, w_in, conv_w, conv_b, dt_bias, a_log, d_skip, ssm_norm_w, attn_sink, w_out, norm2_w, w_up, ffn_conv_w, ffn_conv_b, w_down, final_norm_w, loss_target, m_rel_bias, m_norm1_w, m_w_in, m_conv_w, m_conv_b, m_dt_bias, m_a_log, m_d_skip, m_ssm_norm_w, m_attn_sink, m_w_out, m_norm2_w, m_w_up, m_ffn_conv_w, m_ffn_conv_b, m_w_down, m_final_norm_w, v_rel_bias, v_norm1_w, v_w_in, v_conv_w, v_conv_b, v_dt_bias, v_a_log, v_d_skip, v_ssm_norm_w, v_attn_sink, v_w_out, v_norm2_w, v_w_up, v_ffn_conv_w, v_ffn_conv_b, v_w_down, v_final_norm_w):
    w = dict(rel_bias=rel_bias, norm1_w=norm1_w, w_in=w_in, conv_w=conv_w, conv_b=conv_b, dt_bias=dt_bias, a_log=a_log,
             d_skip=d_skip, ssm_norm_w=ssm_norm_w, attn_sink=attn_sink, w_out=w_out, norm2_w=norm2_w, w_up=w_up,
             ffn_conv_w=ffn_conv_w, ffn_conv_b=ffn_conv_b, w_down=w_down, final_norm_w=final_norm_w)
    m = dict(rel_bias=m_rel_bias, norm1_w=m_norm1_w, w_in=m_w_in, conv_w=m_conv_w, conv_b=m_conv_b, dt_bias=m_dt_bias,
             a_log=m_a_log, d_skip=m_d_skip, ssm_norm_w=m_ssm_norm_w, attn_sink=m_attn_sink, w_out=m_w_out, norm2_w=m_norm2_w,
             w_up=m_w_up, ffn_conv_w=m_ffn_conv_w, ffn_conv_b=m_ffn_conv_b, w_down=m_w_down, final_norm_w=m_final_norm_w)
    v = dict(rel_bias=v_rel_bias, norm1_w=v_norm1_w, w_in=v_w_in, conv_w=v_conv_w, conv_b=v_conv_b, dt_bias=v_dt_bias,
             a_log=v_a_log, d_skip=v_d_skip, ssm_norm_w=v_ssm_norm_w, attn_sink=v_attn_sink, w_out=v_w_out, norm2_w=v_norm2_w,
             w_up=v_w_up, ffn_conv_w=v_ffn_conv_w, ffn_conv_b=v_ffn_conv_b, w_down=v_w_down, final_norm_w=v_final_norm_w)
    my_chip = 2 * lax.axis_index("x") + lax.axis_index("y")

    shards = {k: w[k].astype(BF16) for k in _BIG}
    received = {}

    def gather(spec):
        return _gather_rider([shards[k][i] for i, k in spec])

    def weights(spec, carried):
        out = []
        for (i, k), g_ in zip(spec, carried):
            full = _join_chips(g_, _BIG_AXIS[k] - 1)
            out.append(_perm_in_cols(full) if k == "w_in" else full)
        return out

    def scatter(spec, grads):
        layer = spec[0][0]
        bufs = [_split_by_chip(g_, _BIG_AXIS[k] - 1).astype(BF16) for (_, k), g_ in zip(spec, grads)]
        prev = [received[(layer % 2, k)] for _, k in spec] if layer + 2 < DEPTH else []
        return _scatter_rider(bufs, layer, prev)

    def collect(spec, carried):
        for (i, k), pieces in zip(spec, carried):
            received[(i % 2, k)] = pieces

    conv_shapes = [(DEPTH, SSM_CONV, CONV_CH), (DEPTH, FFN_CONV, D_FF)]
    placed = [lax.dynamic_update_slice_in_dim(jnp.zeros(s, F32), w[k], my_chip * w[k].shape[2], axis=2)
              for k, s in zip(_SMALL_SHARDED, conv_shapes)]
    lead = (lax.axis_index("c") == 0).astype(F32)
    conv_full = _unpack(_allreduce_small(_pack([p * lead for p in placed]), name="gather_conv_weights"), conv_shapes)
    small = {k: w[k] for k in _SMALL}
    small["conv_w"], small["ffn_conv_w"] = conv_full

    loss_part, grad_x, gp = _local_step(x, loss_target, small,
                                        exchange=dict(gather=gather, weights=weights, scatter=scatter, collect=collect))

    small_shapes = [small[k].shape for k in _SMALL] + [()]
    red = _unpack(_allreduce_small(_pack([gp[k] for k in _SMALL] + [loss_part[0, :1]]), name="reduce_small"), small_shapes)
    gsmall = dict(zip(_SMALL, red[:-1]))
    loss = red[-1]
    for k in _SMALL_SHARDED:
        n = w[k].shape[2]
        gsmall[k] = lax.dynamic_slice_in_dim(gsmall[k], my_chip * n, n, axis=2)

    fulls = []
    for k in _BIG:
        full = None
        for p in range(DEPTH // 2):
            full = _sum_sources(received[(p, k)], p, full, name=f"sum_{k}_{p}")
        fulls.append(full)
    gbig = dict(zip(_BIG, _join_halves(fulls, name="join_halves")))

    grad, delta, new_m, new_v = {}, {}, {}, {}
    for k in _BIG:
        shp = w[k].shape
        two = lambda a: a.reshape(shp[0] * shp[1], shp[2])
        d_, m_, v_ = _adamw(two(w[k]), two(gbig[k]), two(m[k]), two(v[k]), name=f"adamw_{k}")
        grad[k], delta[k], new_m[k], new_v[k] = gbig[k], d_.reshape(shp), m_.reshape(shp), v_.reshape(shp)
    shapes = [w[k].shape for k in _SMALL]
    d_, m_, v_ = _adamw(_pack([w[k] for k in _SMALL]), _pack([gsmall[k] for k in _SMALL]), _pack([m[k] for k in _SMALL]),
                        _pack([v[k] for k in _SMALL]), name="adamw_small")
    for k, a, b_, c_ in zip(_SMALL, _unpack(d_, shapes), _unpack(m_, shapes), _unpack(v_, shapes)):
        grad[k], delta[k], new_m[k], new_v[k] = gsmall[k], a, b_, c_
    return (loss, grad_x, *[grad[k] for k in _ORDER], *[delta[k] for k in _ORDER], *[new_m[k] for k in _ORDER],
            *[new_v[k] for k in _ORDER])
```

```python
import functools
import math

import jax
import jax.numpy as jnp
import numpy as np
from jax import lax
from jax.experimental import pallas as pl
from jax.experimental.pallas import tpu as pltpu

F32 = jnp.float32
BF16 = jnp.bfloat16
BS = pl.BlockSpec
SDS = jax.ShapeDtypeStruct
MESH = pl.DeviceIdType.MESH

D_MODEL = 1024
DEPTH = 4
SSM_HEADS = 16
SSM_WIDTH = 1024
BC_WIDTH = 256
CONV_CH = 1536
SSM_CONV = 7
CHUNK = 128
ATTN_HEADS = 16
KV_HEADS = 4
HEAD_DIM = 64
WINDOW = 128
BLOCK = 128
KEY_SPAN = 384
REL_BUCKETS = 32
REL_MAX_DIST = 128
D_FF = 2816
FFN_CONV = 3
NORM_EPS = 1e-6
Z_END = 1024
XBC_END = 2560
DT_END = 2592
Q_END = 3616
K_END = 3872
IN_COLS = 4128
P_COLS = 4224
ADAM_LR, ADAM_B1, ADAM_B2, ADAM_EPS, ADAM_WD, ADAM_STEP = 0.001, 0.9, 0.999, 1e-08, 0.01, 10
NEG = -1e30
N_DEV = 8
N_CHIP = 4
LANE = 128
VMEM_LIMIT_BYTES = 48 * 1024 * 1024


def _pc(body, *, name, grid, in_specs, out_specs, out_shape, scratch_shapes=(), aliases=None):
    return pl.pallas_call(
        body, name=name, grid=grid, in_specs=in_specs, out_specs=out_specs, out_shape=out_shape,
        scratch_shapes=list(scratch_shapes), input_output_aliases=aliases or {},
        compiler_params=pltpu.CompilerParams(dimension_semantics=("arbitrary",) * len(grid),
                                             vmem_limit_bytes=VMEM_LIMIT_BYTES))


def _split_rider_refs(refs, n_in, n_out, rider):
    n_rin = len(rider["ins"]) + len(rider["prev"])
    n_rout = len(rider["out_shape"])
    core = refs[:n_in] + refs[n_in + n_rin:n_in + n_rin + n_out] + refs[n_in + n_rin + n_out + n_rout + len(rider["scratch"]):]
    rins = refs[n_in:n_in + len(rider["ins"])]
    routs = refs[n_in + n_rin + n_out:n_in + n_rin + n_out + n_rout]
    sems = refs[n_in + n_rin + n_out + n_rout:n_in + n_rin + n_out + n_rout + len(rider["scratch"])]
    return core, rins, routs, sems


def _pc_carry(body, args, *, name, grid, in_specs, out_specs, out_shape, scratch_shapes=(), rider=None, aliases=None):
    if rider is None:
        return _pc(body, name=name, grid=grid, in_specs=in_specs, out_specs=out_specs, out_shape=out_shape,
                   scratch_shapes=scratch_shapes, aliases=aliases)(*args), None
    n_in, n_out = len(in_specs), len(out_shape)
    any_spec = BS(memory_space=pl.ANY)

    def full(*refs):
        core, rins, routs, sems = _split_rider_refs(refs, n_in, n_out, rider)
        ids = [pl.program_id(d) for d in range(len(grid))]
        first = functools.reduce(jnp.logical_and, [i == 0 for i in ids])
        last = functools.reduce(jnp.logical_and, [i == g - 1 for i, g in zip(ids, grid)])

        @pl.when(first)
        def _():
            rider["start"](rins, routs, sems)

        body(*core)

        @pl.when(last)
        def _():
            rider["finish"](rins, routs, sems)

    n_rin = len(rider["ins"])
    outs = pl.pallas_call(
        full, name=name, grid=grid,
        in_specs=list(in_specs) + [any_spec] * (n_rin + len(rider["prev"])),
        out_specs=list(out_specs) + [any_spec] * len(rider["out_shape"]),
        out_shape=list(out_shape) + list(rider["out_shape"]),
        scratch_shapes=list(rider["scratch"]) + list(scratch_shapes),
        input_output_aliases={**(aliases or {}), **{n_in + n_rin + t: n_out + t for t in range(len(rider["prev"]))}},
        compiler_params=pltpu.CompilerParams(dimension_semantics=("arbitrary",) * len(grid), vmem_limit_bytes=VMEM_LIMIT_BYTES,
                                             has_side_effects=True))(*args, *rider["ins"], *rider["prev"])
    return outs[:n_out], outs[n_out:]


def _run_rider(rider, *, name):
    any_spec = BS(memory_space=pl.ANY)
    n_rin = len(rider["ins"])

    def body(*refs):
        _, rins, routs, sems = _split_rider_refs(refs, 0, 0, rider)
        rider["start"](rins, routs, sems)
        rider["finish"](rins, routs, sems)

    return pl.pallas_call(
        body, name=name, in_specs=[any_spec] * (n_rin + len(rider["prev"])), out_specs=[any_spec] * len(rider["out_shape"]),
        out_shape=list(rider["out_shape"]), scratch_shapes=list(rider["scratch"]),
        input_output_aliases={n_rin + t: t for t in range(len(rider["prev"]))},
        compiler_params=pltpu.CompilerParams(has_side_effects=True))(*rider["ins"], *rider["prev"])


def _div_tile(n, pref, mult):
    t = min(pref, n)
    t -= t % mult
    while t >= mult:
        if n % t == 0:
            return t
        t -= mult
    return n


def _mm(a, b, *, name, ta=False, tb=False, add=None, out_dtype=F32, tm=1024, tn=1024, tk=1024):
    if ta:
        K, M = a.shape
    else:
        M, K = a.shape
    N = b.shape[0] if tb else b.shape[1]
    tm, tn, tk = _div_tile(M, tm, LANE), _div_tile(N, tn, LANE), _div_tile(K, tk, LANE)
    nk = K // tk
    dims = (((0,) if ta else (1,), (1,) if tb else (0,)), ((), ()))

    def body_single(*refs):
        r = lax.dot_general(refs[0][...], refs[1][...], dims, preferred_element_type=F32)
        if add is not None:
            r = r + refs[2][...]
        refs[-1][...] = r.astype(out_dtype)

    def body(*refs):
        if add is None:
            a_ref, b_ref, o_ref, acc_ref = refs
        else:
            a_ref, b_ref, add_ref, o_ref, acc_ref = refs
        k = pl.program_id(2)

        @pl.when(k == 0)
        def _():
            acc_ref[...] = jnp.zeros_like(acc_ref)

        acc_ref[...] += lax.dot_general(a_ref[...], b_ref[...], dims, preferred_element_type=F32)

        @pl.when(k == nk - 1)
        def _():
            r = acc_ref[...]
            if add is not None:
                r = r + add_ref[...]
            o_ref[...] = r.astype(out_dtype)

    a_spec = BS((tk, tm), lambda i, j, k: (k, i)) if ta else BS((tm, tk), lambda i, j, k: (i, k))
    b_spec = BS((tn, tk), lambda i, j, k: (j, k)) if tb else BS((tk, tn), lambda i, j, k: (k, j))
    in_specs, args = [a_spec, b_spec], [a, b]
    if add is not None:
        in_specs.append(BS((tm, tn), lambda i, j, k: (i, j)))
        args.append(add)
    return _pc(body_single if nk == 1 else body, name=name, grid=(M // tm, N // tn, nk), in_specs=in_specs,
               out_specs=BS((tm, tn), lambda i, j, k: (i, j)), out_shape=SDS((M, N), out_dtype),
               scratch_shapes=[] if nk == 1 else [pltpu.VMEM((tm, tn), F32)])(*args)


def _dot(a, b, dims):
    return lax.dot_general(a.astype(BF16), b.astype(BF16), (dims, ((), ())), preferred_element_type=F32)


@jax.custom_vjp
def _nn(a, b):
    return _dot(a, b, ((1,), (0,)))


@jax.custom_vjp
def _nt(a, b):
    return _dot(a, b, ((1,), (1,)))


@jax.custom_vjp
def _tn(a, b):
    return _dot(a, b, ((0,), (0,)))


_nn.defvjp(lambda a, b: (_nn(a, b), (a, b)), lambda r, g: (_nt(g, r[1]), _tn(r[0], g)))
_nt.defvjp(lambda a, b: (_nt(a, b), (a, b)), lambda r, g: (_nn(g, r[1]), _tn(g, r[0])))
_tn.defvjp(lambda a, b: (_tn(a, b), (a, b)), lambda r, g: (_nt(r[1], g), _nn(r[0], g)))


def _hdot(m, x):
    hi = x.astype(BF16)
    r1 = x - hi.astype(F32)
    lo = r1.astype(BF16)
    lo2 = (r1 - lo.astype(F32)).astype(BF16)
    n = x.shape[1]
    out = lax.dot_general(m.astype(BF16), jnp.concatenate([hi, lo, lo2], axis=1), (((1,), (0,)), ((), ())),
                          preferred_element_type=F32)
    return out[:, :n] + out[:, n:2 * n] + out[:, 2 * n:]


@jax.custom_vjp
def _cumdot(m, mt, x):
    return _hdot(m, x)


_cumdot.defvjp(lambda m, mt, x: (_hdot(m, x), (m, mt)),
               lambda r, g: (jnp.zeros_like(r[0]), jnp.zeros_like(r[1]), _hdot(r[1], g)))


def _sigmoid(x):
    return 1.0 / (1.0 + jnp.exp(-x))


def _softplus(x):
    return jnp.maximum(x, 0.0) + jnp.log(1.0 + jnp.exp(-jnp.abs(x)))


def _rms(x, w):
    return x * lax.rsqrt(jnp.mean(x * x, axis=-1, keepdims=True) + NORM_EPS) * w


def _norm_mm(x2, nw, b, *, name, out_dtype=F32, tm=1024, tn=1024):
    T, D = x2.shape
    N = b.shape[1]
    tm, tn = _div_tile(T, tm, LANE), _div_tile(N, tn, LANE)

    def body(x_ref, w_ref, b_ref, h_ref, o_ref):
        @pl.when(pl.program_id(1) == 0)
        def _():
            h_ref[...] = _rms(x_ref[...], w_ref[...]).astype(BF16)

        o_ref[...] = lax.dot_general(h_ref[...], b_ref[...], (((1,), (0,)), ((), ())), preferred_element_type=F32).astype(out_dtype)

    return _pc(body, name=name, grid=(T // tm, N // tn),
               in_specs=[BS((tm, D), lambda i, j: (i, 0)), BS((1, D), lambda i, j: (0, 0)), BS((D, tn), lambda i, j: (0, j))],
               out_specs=[BS((tm, D), lambda i, j: (i, 0)), BS((tm, tn), lambda i, j: (i, j))],
               out_shape=[SDS((T, D), BF16), SDS((T, N), out_dtype)])(x2, nw.reshape(1, D), b)


def _dnorm(dh, x2, nw, resid, *, name):
    T, D = x2.shape
    tr = _div_tile(T, 512, 16)

    def body(x_ref, w_ref, dh_ref, r_ref, dx_ref, dxb_ref, dw_ref):
        _, vjp = jax.vjp(_rms, x_ref[...], w_ref[...])
        dx, dw = vjp(dh_ref[...])
        dx = dx + r_ref[...]
        dx_ref[...] = dx
        dxb_ref[...] = dx.astype(BF16)

        @pl.when(pl.program_id(0) == 0)
        def _():
            dw_ref[...] = jnp.zeros_like(dw_ref)

        dw_ref[...] += dw

    row = BS((tr, D), lambda i: (i, 0))
    one = BS((1, D), lambda i: (0, 0))
    return _pc(body, name=name, grid=(T // tr,), in_specs=[row, one, row, row], out_specs=[row, row, one],
               out_shape=[SDS((T, D), F32), SDS((T, D), BF16), SDS((1, D), F32)])(x2, nw.reshape(1, D), dh, resid)


ROW_PAD = 8


def _pad_rows(x):
    return jnp.concatenate([x, jnp.zeros((ROW_PAD, x.shape[1]), x.dtype)], axis=0)


def _shift_rows(xp, s):
    n = xp.shape[0] - ROW_PAD
    return xp[:n] if s == 0 else pltpu.roll(xp, (-s) % xp.shape[0], 0)[:n]


def _conv_taps(x, taps):
    xp = _pad_rows(x)
    return [_shift_rows(xp, k - taps // 2) for k in range(taps)]


def _conv_pre(xs, w_ref, b_ref):
    c = b_ref[...] + w_ref[0:1, :] * xs[0]
    for k in range(1, len(xs)):
        c = c + w_ref[k:k + 1, :] * xs[k]
    return c


def _conv_fwd(x3, x_blk0, w, b, *, taps, ct, gate_blk0=None, out_dtype, name):
    B, L, _ = x3.shape
    C = w.shape[1]
    wp = jnp.zeros((8, C), F32).at[:taps].set(w)

    def body(*refs):
        if gate_blk0 is None:
            x_ref, w_ref, b_ref, o_ref = refs
        else:
            x_ref, u_ref, w_ref, b_ref, o_ref = refs
        c = _conv_pre(_conv_taps(x_ref[0].astype(F32), taps), w_ref, b_ref)
        y = c * _sigmoid(c)
        if gate_blk0 is not None:
            y = y * u_ref[0].astype(F32)
        o_ref[0] = y.astype(out_dtype)

    in_specs = [BS((1, L, ct), lambda bi, j: (bi, 0, x_blk0 + j))]
    args = [x3]
    if gate_blk0 is not None:
        in_specs.append(BS((1, L, ct), lambda bi, j: (bi, 0, gate_blk0 + j)))
        args.append(x3)
    in_specs += [BS((8, ct), lambda bi, j: (0, j)), BS((1, ct), lambda bi, j: (0, j))]
    args += [wp, b.reshape(1, C)]
    return _pc(body, name=name, grid=(B, C // ct), in_specs=in_specs,
               out_specs=BS((1, L, ct), lambda bi, j: (bi, 0, j)), out_shape=SDS((B, L, C), out_dtype))(*args)


def _conv_bwd(x3, x_blk0, w, b, dy3, *, taps, ct, gate_blk0=None, name):
    B, L, _ = x3.shape
    C = w.shape[1]
    wp = jnp.zeros((8, C), F32).at[:taps].set(w)
    gated = gate_blk0 is not None

    def body(*refs):
        if gated:
            x_ref, u_ref, w_ref, b_ref, dy_ref, dx_ref, du_ref, dw_ref, db_ref = refs
        else:
            x_ref, w_ref, b_ref, dy_ref, dx_ref, dw_ref, db_ref = refs
        xs = _conv_taps(x_ref[0].astype(F32), taps)
        dy = dy_ref[0].astype(F32)
        c = _conv_pre(xs, w_ref, b_ref)
        sg = _sigmoid(c)
        dsilu = sg * (1.0 + c * (1.0 - sg))
        if gated:
            du_ref[0] = (dy * (c * sg)).astype(BF16)
            dc = dy * u_ref[0].astype(F32) * dsilu
        else:
            dc = dy * dsilu
        dcp = _pad_rows(dc)
        dx = jnp.zeros_like(dc)
        dw_ref[0] = jnp.zeros((8, ct), F32)
        for k in range(taps):
            dx = dx + w_ref[k:k + 1, :] * _shift_rows(dcp, taps // 2 - k)
            dw_ref[0, k:k + 1, :] = jnp.sum(dc * xs[k], axis=0, keepdims=True)
        dx_ref[0] = dx.astype(BF16)
        db_ref[0] = jnp.sum(dc, axis=0, keepdims=True)

    xs = BS((1, L, ct), lambda bi, j: (bi, 0, x_blk0 + j))
    ys = BS((1, L, ct), lambda bi, j: (bi, 0, j))
    in_specs, args = [xs], [x3]
    if gated:
        in_specs.append(BS((1, L, ct), lambda bi, j: (bi, 0, gate_blk0 + j)))
        args.append(x3)
    in_specs += [BS((8, ct), lambda bi, j: (0, j)), BS((1, ct), lambda bi, j: (0, j)), ys]
    args += [wp, b.reshape(1, C), dy3]
    out_specs = [ys] + ([ys] if gated else []) + [BS((1, 8, ct), lambda bi, j: (bi, 0, j)), BS((1, 1, ct), lambda bi, j: (bi, 0, j))]
    out_shape = [SDS((B, L, C), BF16)] + ([SDS((B, L, C), BF16)] if gated else []) + [SDS((B, 8, C), F32), SDS((B, 1, C), F32)]
    return _pc(body, name=name, grid=(B, C // ct), in_specs=in_specs, out_specs=out_specs, out_shape=out_shape)(*args)


def _tri(reverse):
    r = lax.broadcasted_iota(jnp.int32, (CHUNK, CHUNK), 0)
    c = lax.broadcasted_iota(jnp.int32, (CHUNK, CHUNK), 1)
    return (c >= r) if reverse else (c <= r)


PAIRS = 2
QUADS = SSM_HEADS // (2 * PAIRS)
QW = PAIRS * LANE


def _ssd_chunk(h0, h1, x0, x1, bm, cm, dtc, alog, *, col0, reverse):
    mask = _tri(reverse)
    eye = lax.broadcasted_iota(jnp.int32, (CHUNK, CHUNK), 0) == lax.broadcasted_iota(jnp.int32, (CHUNK, CHUNK), 1)
    lane = lax.broadcasted_iota(jnp.int32, (1, LANE), 1)
    first = lane < HEAD_DIM
    adt = dtc * (-jnp.exp(alog))
    cumc = _cumdot(mask.astype(F32), _tri(not reverse).astype(F32), adt)
    totc = jnp.sum(adt, axis=0, keepdims=True)
    cb = _nt(cm, bm)

    def col(v, c):
        return jnp.sum(jnp.where(lane == c, v, 0.0), axis=1, keepdims=True)

    outs, states = [], []
    for p, (hprev, xs) in enumerate(((h0, x0), (h1, x1))):
        c0 = col0 + 2 * p
        cj = (col(cumc, c0), col(cumc, c0 + 1))
        cum = jnp.where(first, cj[0], cj[1])
        tot = jnp.where(first, col(totc, c0), col(totc, c0 + 1))
        xdt = xs * jnp.where(first, col(dtc, c0), col(dtc, c0 + 1))
        y = _nn(cm, hprev) * jnp.exp(cum)
        for j in range(2):
            rj = jnp.sum(jnp.where(eye, cj[j], 0.0), axis=0, keepdims=True)
            dec = jnp.exp(jnp.where(mask, cj[j] - rj, NEG))
            y = y + _nn(cb * dec, jnp.where(first if j == 0 else ~first, xdt, 0.0))
        outs.append(y)
        states.append(hprev * jnp.exp(tot) + _tn(bm, xdt * jnp.exp(tot - cum)))
    return outs[0], outs[1], states[0], states[1]


def _ssd_specs(B, L):
    def lanes(w, blk):
        return BS((1, L, w), blk)

    return [
        lanes(QW, lambda b, q: (b, 0, q)),
        lanes(LANE, lambda b, q: (b, 0, 8 + q // 2)),
        lanes(LANE, lambda b, q: (b, 0, 10 + q // 2)),
        lanes(LANE, lambda b, q: (b, 0, 0)),
        BS((1, LANE), lambda b, q: (0, 0)),
        BS((1, QW), lambda b, q: (0, q)),
    ]


def _ssd_slot(d, ci):
    return ci if d == 0 else ci + 1


def _ssd_fwd(xbc_act, dtc, alog, dskip, *, name, rider=None):
    B, L, _ = xbc_act.shape
    nc = L // CHUNK

    def body(xs_ref, b_ref, c_ref, dt_ref, alog_ref, dsk_ref, y_ref, hs_ref):
        q = pl.program_id(1)
        alog_v = alog_ref[...]
        y_ref[0] = dsk_ref[...] * xs_ref[0]
        hs_ref[0, 0, 0, 0] = jnp.zeros((LANE, QW), F32)
        hs_ref[0, 0, 1, nc] = jnp.zeros((LANE, QW), F32)

        def step(i, carry):
            cis = (i, nc - 1 - i)
            rows = [pl.ds(pl.multiple_of(ci * CHUNK, CHUNK), CHUNK) for ci in cis]
            res = []
            for d in range(2):
                cur = _ssd_slot(d, cis[d])
                res.append(_ssd_chunk(
                    hs_ref[0, 0, d, cur, :, :LANE], hs_ref[0, 0, d, cur, :, LANE:], xs_ref[0, rows[d], :LANE],
                    xs_ref[0, rows[d], LANE:], b_ref[0, rows[d], :], c_ref[0, rows[d], :], dt_ref[0, rows[d], :], alog_v,
                    col0=SSM_HEADS * d + 2 * PAIRS * q, reverse=d == 1))
            for d in range(2):
                y0, y1, n0, n1 = res[d]
                nxt = _ssd_slot(d, cis[d] + 1 if d == 0 else cis[d] - 1)
                hs_ref[0, 0, d, nxt, :, :LANE] = n0
                hs_ref[0, 0, d, nxt, :, LANE:] = n1
                y_ref[0, rows[d], :LANE] += y0
                y_ref[0, rows[d], LANE:] += y1
            return carry

        lax.fori_loop(0, nc, step, 0, unroll=2)

    (y, hs), carried = _pc_carry(
        body, (xbc_act, xbc_act, xbc_act, dtc, alog, dskip), name=name, grid=(B, QUADS), in_specs=_ssd_specs(B, L),
        out_specs=[BS((1, L, QW), lambda b, q: (b, 0, q)), BS((1, 1, 2, nc + 1, LANE, QW), lambda b, q: (b, q, 0, 0, 0, 0))],
        out_shape=[SDS((B, L, SSM_WIDTH), F32), SDS((B, QUADS, 2, nc + 1, LANE, QW), F32)], rider=rider)
    return y, hs, carried


def _ssd_bwd(xbc_act, dtc, alog, dskip, hs, dy, *, name, rider=None):
    B, L, _ = xbc_act.shape
    nc = L // CHUNK

    def body(xs_ref, b_ref, c_ref, dt_ref, alog_ref, dsk_ref, hs_ref, dy_ref,
             dxs_ref, db_ref, dc_ref, ddt_ref, dalog_ref, ddsk_ref, dh_ref):
        q = pl.program_id(1)
        alog_v = alog_ref[...]

        @pl.when(q % 2 == 0)
        def _():
            db_ref[...] = jnp.zeros_like(db_ref)
            dc_ref[...] = jnp.zeros_like(dc_ref)

        @pl.when(q == 0)
        def _():
            ddt_ref[...] = jnp.zeros_like(ddt_ref)

        dxs_ref[0] = dy_ref[0] * dsk_ref[...]
        ddsk_ref[0] = jnp.sum(dy_ref[0] * xs_ref[0], axis=0, keepdims=True)
        dh_ref[...] = jnp.zeros_like(dh_ref)

        def step(i, carry):
            g_alog = carry
            cis = (nc - 1 - i, i)
            rows = [pl.ds(pl.multiple_of(ci * CHUNK, CHUNK), CHUNK) for ci in cis]
            res = []
            for d in range(2):
                cur = _ssd_slot(d, cis[d])
                fn = functools.partial(_ssd_chunk, col0=SSM_HEADS * d + 2 * PAIRS * q, reverse=d == 1)
                _, vjp = jax.vjp(fn, hs_ref[0, 0, d, cur, :, :LANE], hs_ref[0, 0, d, cur, :, LANE:], xs_ref[0, rows[d], :LANE],
                                 xs_ref[0, rows[d], LANE:], b_ref[0, rows[d], :], c_ref[0, rows[d], :], dt_ref[0, rows[d], :],
                                 alog_v)
                res.append(vjp((dy_ref[0, rows[d], :LANE], dy_ref[0, rows[d], LANE:], dh_ref[d, :, :LANE], dh_ref[d, :, LANE:])))
            for d in range(2):
                g_h0, g_h1, g_x0, g_x1, g_b, g_c, g_dt, g_alog1 = res[d]
                dh_ref[d, :, :LANE] = g_h0
                dh_ref[d, :, LANE:] = g_h1
                dxs_ref[0, rows[d], :LANE] += g_x0
                dxs_ref[0, rows[d], LANE:] += g_x1
                db_ref[0, rows[d], :] += g_b
                dc_ref[0, rows[d], :] += g_c
                ddt_ref[0, rows[d], :] += g_dt
                g_alog = g_alog + g_alog1
            return g_alog

        dalog_ref[0, 0] = lax.fori_loop(0, nc, step, jnp.zeros((1, LANE), F32))

    lanes = lambda w, blk: BS((1, L, w), blk)
    in_specs = _ssd_specs(B, L) + [BS((1, 1, 2, nc + 1, LANE, QW), lambda b, q: (b, q, 0, 0, 0, 0)), lanes(QW, lambda b, q: (b, 0, q))]
    out_specs = [lanes(QW, lambda b, q: (b, 0, q)), lanes(LANE, lambda b, q: (b, 0, q // 2)), lanes(LANE, lambda b, q: (b, 0, q // 2)),
                 lanes(LANE, lambda b, q: (b, 0, 0)), BS((1, 1, 1, LANE), lambda b, q: (b, q, 0, 0)),
                 BS((1, 1, QW), lambda b, q: (b, 0, q))]
    out_shape = [SDS((B, L, CONV_CH), F32), SDS((B, L, BC_WIDTH), F32), SDS((B, L, BC_WIDTH), F32), SDS((B, L, LANE), F32),
                 SDS((B, QUADS, 1, LANE), F32), SDS((B, 1, SSM_WIDTH), F32)]
    outs, carried = _pc_carry(body, (xbc_act, xbc_act, xbc_act, dtc, alog, dskip, hs, dy), name=name, grid=(B, QUADS),
                              in_specs=in_specs, out_specs=out_specs, out_shape=out_shape,
                              scratch_shapes=[pltpu.VMEM((2, LANE, QW), F32)], rider=rider)
    return (*outs, carried)


def _gate_norm(yp, z, w):
    v = yp * (z * _sigmoid(z))
    return v * lax.rsqrt(jnp.mean(v * v, axis=-1, keepdims=True) + NORM_EPS) * w


def _gate_fwd(ypre2, proj2, w, *, name):
    T = ypre2.shape[0]
    tr = _div_tile(T, 512, 8)
    G = 512

    def body(y_ref, z_ref, w_ref, o_ref):
        o_ref[...] = _gate_norm(y_ref[...], z_ref[...], w_ref[...]).astype(BF16)

    return _pc(body, name=name, grid=(T // tr, 2),
               in_specs=[BS((tr, G), lambda i, g: (i, g)), BS((tr, G), lambda i, g: (i, 2 + g)), BS((1, G), lambda i, g: (0, g))],
               out_specs=BS((tr, G), lambda i, g: (i, g)), out_shape=SDS((T, 2 * SSM_WIDTH), BF16))(ypre2, proj2, w.reshape(1, -1))


def _gate_bwd(ypre2, proj2, w, dy, *, name):
    T = ypre2.shape[0]
    tr = _div_tile(T, 512, 8)
    G = 512

    def body(y_ref, z_ref, w_ref, dy_ref, dyp_ref, dz_ref, dw_ref):
        _, vjp = jax.vjp(_gate_norm, y_ref[...], z_ref[...], w_ref[...])
        dyp, dz, dw = vjp(dy_ref[...].astype(F32))
        dyp_ref[...] = dyp
        dz_ref[...] = dz.astype(BF16)
        dw_ref[0] = dw

    tile = BS((tr, G), lambda i, g: (i, g))
    return _pc(body, name=name, grid=(T // tr, 2),
               in_specs=[tile, BS((tr, G), lambda i, g: (i, 2 + g)), BS((1, G), lambda i, g: (0, g)), tile],
               out_specs=[tile, tile, BS((1, 1, G), lambda i, g: (i, 0, g))],
               out_shape=[SDS((T, SSM_WIDTH), F32), SDS((T, SSM_WIDTH), BF16), SDS((T // tr, 1, SSM_WIDTH), F32)])(
        ypre2, proj2, w.reshape(1, -1), dy)


def _first_half():
    return lax.broadcasted_iota(jnp.int32, (1, LANE), 1) < HEAD_DIM


def _dup_kv_head(pair, odd):
    rolled = pltpu.roll(pair, HEAD_DIM, 1)
    return jnp.where(_first_half(), rolled, pair) if odd else jnp.where(_first_half(), pair, rolled)


def _stack_heads(quad):
    first = _first_half()
    lo, hi = quad[:, :LANE], quad[:, LANE:]
    return jnp.concatenate([jnp.where(first, lo, 0.0), jnp.where(first, 0.0, lo), jnp.where(first, hi, 0.0),
                            jnp.where(first, 0.0, hi)], axis=0)


def _unstack_heads(o):
    first = _first_half()
    return jnp.concatenate([jnp.where(first, o[:BLOCK], o[BLOCK:2 * BLOCK]), jnp.where(first, o[2 * BLOCK:3 * BLOCK], o[3 * BLOCK:])], axis=1)


def _fold_kv_head(d, odd):
    tot = d + pltpu.roll(d, HEAD_DIM, 1)
    return jnp.where(_first_half(), 0.0, tot) if odd else jnp.where(_first_half(), tot, 0.0)


def _attn_softmax(s, sink):
    m = jnp.maximum(jnp.max(s, axis=-1, keepdims=True), sink)
    p = jnp.exp(s - m)
    ps = jnp.exp(sink - m)
    inv = 1.0 / (jnp.sum(p, axis=-1, keepdims=True) + ps)
    return p * inv, ps * inv


def _attn_colneg(n, L):
    kpos = n * BLOCK - WINDOW + lax.broadcasted_iota(jnp.int32, (1, KEY_SPAN), 1)
    return jnp.where((kpos >= 0) & (kpos < L), 0.0, NEG)


def _attn_in_specs(L):
    nblk = L // BLOCK
    kv = lambda o, col: BS((1, BLOCK, 4 * HEAD_DIM), lambda b, n: (b, jnp.clip(n + o, 0, nblk - 1), col))
    kcol, vcol = 3584 // 256, 3840 // 256
    return [BS((1, BLOCK, ATTN_HEADS * HEAD_DIM), lambda b, n: (b, n, 0)), kv(-1, kcol), kv(0, kcol), kv(1, kcol),
            kv(-1, vcol), kv(0, vcol), kv(1, vcol),
            BS((ATTN_HEADS, BLOCK, KEY_SPAN), lambda b, n: (0, 0, 0)), BS((ATTN_HEADS * BLOCK, 1), lambda b, n: (0, 0))]


def _attn_fwd(proj, bias, sinkcol, mixed, *, name, rider=None):
    B, L, _ = proj.shape

    def body(q_ref, k0, k1, k2, v0, v1, v2, bias_ref, sink_ref, _, o_ref):
        colneg = _attn_colneg(pl.program_id(1), L)
        kcat = jnp.concatenate([k0[0], k1[0], k2[0]], axis=0)
        vcat = jnp.concatenate([v0[0], v1[0], v2[0]], axis=0)
        scores, probs, scales = [], [], []
        for g in range(KV_HEADS):
            pair = slice(LANE * (g // 2), LANE * (g // 2) + LANE)
            quad = slice(4 * HEAD_DIM * g, 4 * HEAD_DIM * (g + 1))
            kd = _dup_kv_head(kcat[:, pair], g % 2).astype(BF16)
            qs = (_stack_heads(q_ref[0, :, quad]) * HEAD_DIM ** -0.5).astype(BF16)
            scores.append(lax.dot_general(qs, kd, (((1,), (1,)), ((), ())), preferred_element_type=F32))
        for g in range(KV_HEADS):
            s = scores[g] + bias_ref[4 * g:4 * g + 4].reshape(4 * BLOCK, KEY_SPAN) + colneg
            sink = sink_ref[4 * BLOCK * g:4 * BLOCK * (g + 1)]
            m = jnp.maximum(jnp.max(s, axis=-1, keepdims=True), sink)
            p = jnp.exp(s - m)
            scales.append(1.0 / (jnp.sum(p, axis=-1, keepdims=True) + jnp.exp(sink - m)))
            probs.append(p.astype(BF16))
        for g in range(KV_HEADS):
            pair = slice(LANE * (g // 2), LANE * (g // 2) + LANE)
            quad = slice(4 * HEAD_DIM * g, 4 * HEAD_DIM * (g + 1))
            vd = _dup_kv_head(vcat[:, pair], g % 2).astype(BF16)
            o = lax.dot_general(probs[g], vd, (((1,), (0,)), ((), ())), preferred_element_type=F32) * scales[g]
            o_ref[0, :, quad] = _unstack_heads(o).astype(BF16)

    (out,), carried = _pc_carry(body, (proj, proj, proj, proj, proj, proj, proj, bias, sinkcol, mixed), name=name,
                                grid=(B, L // BLOCK), in_specs=_attn_in_specs(L) + [BS(memory_space=pl.ANY)],
                                out_specs=[BS((1, BLOCK, ATTN_HEADS * HEAD_DIM), lambda b, n: (b, n, 1))],
                                out_shape=[SDS(mixed.shape, BF16)], rider=rider, aliases={9: 0})
    return out, carried


def _attn_bwd(proj, bias, sinkcol, dout, *, name, rider=None):
    B, L, _ = proj.shape
    nblk = L // BLOCK
    nn, nt, tn = (((1,), (0,)), ((), ())), (((1,), (1,)), ((), ())), (((0,), (0,)), ((), ()))

    def body(q_ref, k0, k1, k2, v0, v1, v2, bias_ref, sink_ref, do_ref, dq_ref, dk_ref, dv_ref, dbias_ref, dsink_ref):
        b, n = pl.program_id(0), pl.program_id(1)

        @pl.when(n == 0)
        def _():
            dk_ref[...] = jnp.zeros_like(dk_ref)
            dv_ref[...] = jnp.zeros_like(dv_ref)

        @pl.when((n == 0) & (b == 0))
        def _():
            dbias_ref[...] = jnp.zeros_like(dbias_ref)
            dsink_ref[...] = jnp.zeros_like(dsink_ref)

        colneg = _attn_colneg(n, L)
        kcat = jnp.concatenate([k0[0], k1[0], k2[0]], axis=0)
        vcat = jnp.concatenate([v0[0], v1[0], v2[0]], axis=0)
        krows = [pl.ds(pl.multiple_of(jnp.clip(n + o, 0, nblk - 1) * BLOCK, BLOCK), BLOCK) for o in (-1, 0, 1)]
        ops, mids = [], []
        for g in range(KV_HEADS):
            pair = slice(LANE * (g // 2), LANE * (g // 2) + LANE)
            quad = slice(4 * HEAD_DIM * g, 4 * HEAD_DIM * (g + 1))
            kd = _dup_kv_head(kcat[:, pair], g % 2).astype(BF16)
            vd = _dup_kv_head(vcat[:, pair], g % 2).astype(BF16)
            qs = (_stack_heads(q_ref[0, :, quad]) * HEAD_DIM ** -0.5).astype(BF16)
            dos = _stack_heads(do_ref[0, :, quad].astype(F32)).astype(BF16)
            ops.append((kd, qs, dos, lax.dot_general(qs, kd, nt, preferred_element_type=F32),
                        lax.dot_general(dos, vd, nt, preferred_element_type=F32)))
        for g in range(KV_HEADS):
            rows = slice(4 * BLOCK * g, 4 * BLOCK * (g + 1))
            _, _, _, s, dpn = ops[g]
            pn, psink = _attn_softmax(s + bias_ref[4 * g:4 * g + 4].reshape(4 * BLOCK, KEY_SPAN) + colneg, sink_ref[rows])
            r = jnp.sum(dpn * pn, axis=-1, keepdims=True)
            ds = pn * (dpn - r)
            dbias_ref[4 * g:4 * g + 4] += ds.reshape(4, BLOCK, KEY_SPAN)
            dsink_ref[rows] += -psink * r
            mids.append((pn.astype(BF16), ds.astype(BF16)))
        for g in range(KV_HEADS):
            pair = slice(LANE * (g // 2), LANE * (g // 2) + LANE)
            quad = slice(4 * HEAD_DIM * g, 4 * HEAD_DIM * (g + 1))
            kd, qs, dos, _, _ = ops[g]
            pnb, dsb = mids[g]
            dvd = lax.dot_general(pnb, dos, tn, preferred_element_type=F32)
            dkd = lax.dot_general(dsb, qs, tn, preferred_element_type=F32)
            dqs = lax.dot_general(dsb, kd, nn, preferred_element_type=F32) * HEAD_DIM ** -0.5
            dq_ref[0, :, quad] = _unstack_heads(dqs).astype(BF16)
            dk_g, dv_g = _fold_kv_head(dkd, g % 2), _fold_kv_head(dvd, g % 2)
            for o in range(3):
                dk_ref[0, krows[o], pair] += dk_g[o * BLOCK:(o + 1) * BLOCK]
                dv_ref[0, krows[o], pair] += dv_g[o * BLOCK:(o + 1) * BLOCK]

    qspec = BS((1, BLOCK, ATTN_HEADS * HEAD_DIM), lambda b, n: (b, n, 0))
    kvout = BS((1, L, 4 * HEAD_DIM), lambda b, n: (b, 0, 0))
    outs, carried = _pc_carry(
        body, (proj, proj, proj, proj, proj, proj, proj, bias, sinkcol, dout), name=name, grid=(B, nblk),
        in_specs=_attn_in_specs(L) + [BS((1, BLOCK, ATTN_HEADS * HEAD_DIM), lambda b, n: (b, n, 1))],
        out_specs=[qspec, kvout, kvout, BS((ATTN_HEADS, BLOCK, KEY_SPAN), lambda b, n: (0, 0, 0)),
                   BS((ATTN_HEADS * BLOCK, 1), lambda b, n: (0, 0))],
        out_shape=[SDS((B, L, ATTN_HEADS * HEAD_DIM), BF16), SDS((B, L, 4 * HEAD_DIM), F32), SDS((B, L, 4 * HEAD_DIM), F32),
                   SDS((ATTN_HEADS, BLOCK, KEY_SPAN), F32), SDS((ATTN_HEADS * BLOCK, 1), F32)], rider=rider)
    return (*outs, carried)


def _t5_bucket(rel):
    half = REL_BUCKETS // 2
    max_exact = half // 2
    ret = jnp.where(rel > 0, half, 0)
    n = jnp.abs(rel)
    nf = jnp.maximum(n, 1).astype(F32)
    large = max_exact + (jnp.log(nf / max_exact) / math.log(REL_MAX_DIST / max_exact) * (half - max_exact)).astype(jnp.int32)
    large = jnp.minimum(large, half - 1)
    return ret + jnp.where(n < max_exact, n, large)


def _bucket_table():
    rel = jnp.arange(KEY_SPAN)[None, :] - WINDOW - jnp.arange(BLOCK)[:, None]
    return _t5_bucket(rel).astype(jnp.int32)


def _bias_expand(rel_bias, bucket, *, name):
    rbt = jnp.zeros((ATTN_HEADS, 1, LANE), F32).at[:, 0, :REL_BUCKETS].set(rel_bias.T)

    def body(rb_ref, bk_ref, o_ref):
        lane = lax.broadcasted_iota(jnp.int32, (1, LANE), 1)
        row = rb_ref[0]
        bk = bk_ref[...]
        acc = jnp.zeros((BLOCK, KEY_SPAN), F32)
        for r in range(REL_BUCKETS):
            val = jnp.sum(jnp.where(lane == r, row, 0.0), axis=1, keepdims=True)
            acc = jnp.where(bk == r, val, acc)
        rel = (lax.broadcasted_iota(jnp.int32, (BLOCK, KEY_SPAN), 1) - WINDOW
               - lax.broadcasted_iota(jnp.int32, (BLOCK, KEY_SPAN), 0))
        o_ref[0] = jnp.where(jnp.abs(rel) <= WINDOW, acc, NEG)

    return _pc(body, name=name, grid=(ATTN_HEADS,),
               in_specs=[BS((1, 1, LANE), lambda h: (h, 0, 0)), BS((BLOCK, KEY_SPAN), lambda h: (0, 0))],
               out_specs=BS((1, BLOCK, KEY_SPAN), lambda h: (h, 0, 0)), out_shape=SDS((ATTN_HEADS, BLOCK, KEY_SPAN), F32))(rbt, bucket)


def _bias_reduce(dbias, bucket, *, name):
    def body(db_ref, bk_ref, o_ref):
        lane = lax.broadcasted_iota(jnp.int32, (1, LANE), 1)
        x = db_ref[0]
        bk = bk_ref[...]
        acc = jnp.zeros((1, LANE), F32)
        for r in range(REL_BUCKETS):
            part = jnp.sum(jnp.where(bk == r, x, 0.0), axis=1, keepdims=True)
            acc = jnp.where(lane == r, jnp.sum(part, axis=0, keepdims=True), acc)
        o_ref[0] = acc

    out = _pc(body, name=name, grid=(ATTN_HEADS,),
              in_specs=[BS((1, BLOCK, KEY_SPAN), lambda h: (h, 0, 0)), BS((BLOCK, KEY_SPAN), lambda h: (0, 0))],
              out_specs=BS((1, 1, LANE), lambda h: (h, 0, 0)), out_shape=SDS((ATTN_HEADS, 1, LANE), F32))(dbias, bucket)
    return out[:, 0, :REL_BUCKETS].T


def _loss_head(x2, w, target, *, name):
    T, D = x2.shape
    tr = _div_tile(T, 512, 8)

    def tile_loss(x, w, t):
        err = _rms(x, w) - t
        return 0.5 * jnp.sum(jnp.mean(err * err, axis=-1, keepdims=True), axis=0, keepdims=True)

    def body(x_ref, w_ref, t_ref, loss_ref, dx_ref, dxb_ref, dw_ref):
        t = t_ref[...]
        l, vjp = jax.vjp(lambda x, w: tile_loss(x, w, t), x_ref[...], w_ref[...])
        dx, dw = vjp(jnp.ones((1, 1), F32))
        dx_ref[...] = dx
        dxb_ref[...] = dx.astype(BF16)

        @pl.when(pl.program_id(0) == 0)
        def _():
            dw_ref[...] = jnp.zeros_like(dw_ref)
            loss_ref[...] = jnp.zeros_like(loss_ref)

        dw_ref[...] += dw
        loss_ref[...] += l + jnp.zeros((1, LANE), F32)

    row = BS((tr, D), lambda i: (i, 0))
    one = BS((1, D), lambda i: (0, 0))
    return _pc(body, name=name, grid=(T // tr,), in_specs=[row, one, row],
               out_specs=[BS((1, LANE), lambda i: (0, 0)), row, row, one],
               out_shape=[SDS((1, LANE), F32), SDS((T, D), F32), SDS((T, D), BF16), SDS((1, D), F32)])(x2, w.reshape(1, D), target)


def _adamw(w2, g2, m2, v2, *, name):
    R, C = w2.shape
    tr = _div_tile(R, 256, 8)
    c1 = 1.0 - ADAM_B1 ** ADAM_STEP
    c2 = 1.0 - ADAM_B2 ** ADAM_STEP

    def body(w_ref, g_ref, m_ref, v_ref, d_ref, nm_ref, nv_ref):
        g = g_ref[...]
        m = ADAM_B1 * m_ref[...] + (1.0 - ADAM_B1) * g
        v = ADAM_B2 * v_ref[...] + (1.0 - ADAM_B2) * (g * g)
        d_ref[...] = -ADAM_LR * ((m / c1) / (jnp.sqrt(v / c2) + ADAM_EPS) + ADAM_WD * w_ref[...])
        nm_ref[...] = m
        nv_ref[...] = v

    t = BS((tr, C), lambda i: (i, 0))
    return _pc(body, name=name, grid=(R // tr,), in_specs=[t, t, t, t], out_specs=[t, t, t],
               out_shape=[SDS((R, C), F32)] * 3)(w2, g2, m2, v2)


def _place():
    return lax.axis_index("x"), lax.axis_index("y"), lax.axis_index("c")


def _gather_rider(shards):
    na = len(shards)

    def copies(ins, outs, sems):
        send_sems, recv_sems = sems
        x, y, c = _place()
        for a in range(na):
            for k, peer in enumerate([(1 - x, y, c), (x, 1 - y, c), (1 - x, 1 - y, c), (x, y, 1 - c)]):
                send = functools.partial(pltpu.make_async_remote_copy, ins[a], outs[a].at[2 * x + y], send_sems.at[a, k],
                                         recv_sems.at[a, k], device_id=peer, device_id_type=MESH)
                got = outs[a].at[2 * peer[0] + peer[1]]
                arrived = functools.partial(pltpu.make_async_remote_copy, got, got, send_sems.at[a, k], recv_sems.at[a, k],
                                            device_id=peer, device_id_type=MESH)
                yield send, arrived

    def start(ins, outs, sems):
        for send, _ in copies(ins, outs, sems):
            send().start()

    def finish(ins, outs, sems):
        both = list(copies(ins, outs, sems))
        for _, arrived in both:
            arrived().wait_recv()
        for send, _ in both:
            send().wait_send()

    return dict(ins=list(shards), prev=[], out_shape=[SDS((N_CHIP,) + s.shape, s.dtype) for s in shards],
                scratch=[pltpu.SemaphoreType.DMA((na, 4)), pltpu.SemaphoreType.DMA((na, 4))], start=start, finish=finish)


def _scatter_rider(bufs, layer, prev):
    na = len(bufs)
    h = layer // (DEPTH // 2)

    def copies(ins, outs, sems):
        send_sems, recv_sems, local_sems = sems
        x, y, c = _place()
        me = 4 * x + 2 * y + c
        for a in range(na):
            for j in range(N_CHIP):
                is_self = ((2 * x + y) == j) & (c == h)
                local = functools.partial(pltpu.make_async_copy, ins[a].at[j], outs[a].at[me], local_sems.at[a])
                remote = functools.partial(pltpu.make_async_remote_copy, ins[a].at[j], outs[a].at[me], send_sems.at[a, j],
                                           recv_sems.at[a, me], device_id=(j // 2, j % 2, h), device_id_type=MESH)
                yield is_self, local, remote

    def start(ins, outs, sems):
        for is_self, local, remote in copies(ins, outs, sems):
            pl.when(is_self)(lambda: local().start())
            pl.when(jnp.logical_not(is_self))(lambda: remote().start())

    def finish(ins, outs, sems):
        _, recv_sems, _ = sems
        x, y, c = _place()
        me = 4 * x + 2 * y + c
        for a in range(na):
            for s in range(N_DEV):
                got = outs[a].at[s]
                arrived = functools.partial(pltpu.make_async_remote_copy, got, got, recv_sems.at[a, s], recv_sems.at[a, s],
                                            device_id=(s // 4, (s // 2) % 2, s % 2), device_id_type=MESH)
                pl.when((c == h) & (me != s))(lambda: arrived().wait_recv())
        for is_self, local, remote in copies(ins, outs, sems):
            pl.when(is_self)(lambda: local().wait())
            pl.when(jnp.logical_not(is_self))(lambda: remote().wait_send())

    return dict(ins=list(bufs), prev=list(prev), out_shape=[SDS((N_DEV,) + b.shape[1:], b.dtype) for b in bufs],
                scratch=[pltpu.SemaphoreType.DMA((na, N_CHIP)), pltpu.SemaphoreType.DMA((na, N_DEV)), pltpu.SemaphoreType.DMA((na,))],
                start=start, finish=finish)


def _sum_sources(parts, parity, into, *, name):
    _, R, C = parts.shape
    tr = _div_tile(R, 256, 16)

    def body(p_ref, *rest):
        acc = p_ref[0].astype(F32)
        for s in range(1, N_DEV):
            acc = acc + p_ref[s].astype(F32)
        rest[-1][0] = acc

    prev = [] if into is None else [into]
    return _pc(body, name=name, grid=(R // tr,),
               in_specs=[BS((N_DEV, tr, C), lambda i: (0, i, 0))] + [BS(memory_space=pl.ANY)] * len(prev),
               out_specs=BS((1, tr, C), lambda i: ((DEPTH // 2) * lax.axis_index("c") + parity, i, 0)),
               out_shape=SDS((DEPTH, R, C), F32), aliases={1: 0} if prev else None)(parts, *prev)


def _join_halves(fulls, *, name):
    na = len(fulls)
    half = DEPTH // 2

    def body(*refs):
        outs = refs[na:2 * na]
        send_sems, recv_sems = refs[2 * na:]
        x, y, c = _place()
        cps = []
        for a in range(na):
            mine = outs[a].at[pl.ds(c * half, half)]
            cp = pltpu.make_async_remote_copy(mine, mine, send_sems.at[a], recv_sems.at[a],
                                              device_id=(x, y, 1 - c), device_id_type=MESH)
            cp.start()
            cps.append(cp)
        for a in range(na):
            theirs = outs[a].at[pl.ds((1 - c) * half, half)]
            pltpu.make_async_remote_copy(theirs, theirs, send_sems.at[a], recv_sems.at[a],
                                         device_id=(x, y, 1 - c), device_id_type=MESH).wait_recv()
        for cp in cps:
            cp.wait_send()

    any_spec = BS(memory_space=pl.ANY)
    return pl.pallas_call(
        body, name=name, in_specs=[any_spec] * na, out_specs=[any_spec] * na,
        out_shape=[SDS(f.shape, f.dtype) for f in fulls], input_output_aliases={a: a for a in range(na)},
        scratch_shapes=[pltpu.SemaphoreType.DMA((na,)), pltpu.SemaphoreType.DMA((na,))],
        compiler_params=pltpu.CompilerParams(has_side_effects=True))(*fulls)


def _allreduce_small(vec, *, name):
    R = vec.shape[0]

    def body(v_ref, o_ref, all_ref, send_sems, recv_sems):
        x, y, c = _place()
        me = 4 * x + 2 * y + c
        all_ref[me] = v_ref[...]
        sends = []
        for r in range(1, N_DEV):
            tgt = (x ^ (r >> 2), y ^ ((r >> 1) & 1), c ^ (r & 1))
            cp = pltpu.make_async_remote_copy(v_ref, all_ref.at[me], send_sems.at[r - 1], recv_sems.at[r - 1],
                                              device_id=tgt, device_id_type=MESH)
            cp.start()
            sends.append(cp)
        for r in range(1, N_DEV):
            tx, ty, tc = x ^ (r >> 2), y ^ ((r >> 1) & 1), c ^ (r & 1)
            got = all_ref.at[4 * tx + 2 * ty + tc]
            pltpu.make_async_remote_copy(got, got, send_sems.at[r - 1], recv_sems.at[r - 1],
                                         device_id=(tx, ty, tc), device_id_type=MESH).wait_recv()
        for cp in sends:
            cp.wait_send()
        acc = all_ref[0]
        for s in range(1, N_DEV):
            acc = acc + all_ref[s]
        o_ref[...] = acc

    vm = BS(memory_space=pltpu.VMEM)
    return pl.pallas_call(
        body, name=name, in_specs=[vm], out_specs=vm, out_shape=SDS((R, LANE), F32),
        scratch_shapes=[pltpu.VMEM((N_DEV, R, LANE), F32), pltpu.SemaphoreType.DMA((N_DEV - 1,)), pltpu.SemaphoreType.DMA((N_DEV - 1,))],
        compiler_params=pltpu.CompilerParams(has_side_effects=True, vmem_limit_bytes=VMEM_LIMIT_BYTES))(vec)


def _pack(arrs):
    rows = []
    for a in arrs:
        f = a.reshape(-1).astype(F32)
        n = -(-f.shape[0] // LANE) * LANE
        rows.append(jnp.pad(f, (0, n - f.shape[0])).reshape(-1, LANE))
    v = jnp.concatenate(rows, axis=0)
    pad = -v.shape[0] % 8
    return jnp.pad(v, ((0, pad), (0, 0)))


def _unpack(v, shapes):
    out, r = [], 0
    for s in shapes:
        n = int(np.prod(s)) if len(s) else 1
        nr = -(-n // LANE)
        out.append(v[r:r + nr].reshape(-1)[:n].reshape(s))
        r += nr
    return out


def _perm_in_cols(w_full):
    z, xbc, dt, q, k, v = (w_full[..., :Z_END], w_full[..., Z_END:XBC_END], w_full[..., XBC_END:DT_END],
                           w_full[..., DT_END:Q_END], w_full[..., Q_END:K_END], w_full[..., K_END:])
    pad = jnp.zeros(dt.shape[:-1] + (LANE - dt.shape[-1],), dt.dtype)
    return jnp.concatenate([q, z, xbc, k, v, dt, pad], axis=-1)


def _unperm_in_cols(g):
    q, z, xbc, k, v, dt = (g[..., :1024], g[..., 1024:2048], g[..., 2048:3584], g[..., 3584:3840], g[..., 3840:4096],
                           g[..., 4096:4096 + 2 * SSM_HEADS])
    return jnp.concatenate([z, xbc, dt, q, k, v], axis=-1)


def _dt_cols(a):
    return jnp.pad(a.reshape(1, 2 * SSM_HEADS), ((0, 0), (0, LANE - 2 * SSM_HEADS)))


def _dt_fwd(proj, dtb, *, name):
    B, L, _ = proj.shape

    def body(p_ref, b_ref, o_ref):
        o_ref[0] = _softplus(p_ref[0] + b_ref[...])

    return _pc(body, name=name, grid=(B,),
               in_specs=[BS((1, L, LANE), lambda b: (b, 0, P_COLS // LANE - 1)), BS((1, LANE), lambda b: (0, 0))],
               out_specs=BS((1, L, LANE), lambda b: (b, 0, 0)), out_shape=SDS((B, L, LANE), F32))(proj, dtb)


def _dt_bwd(proj, dtb, ddt, *, name):
    B, L, _ = proj.shape

    def body(p_ref, b_ref, g_ref, o_ref, db_ref):
        g = g_ref[0] * _sigmoid(p_ref[0] + b_ref[...])
        o_ref[0] = g.astype(BF16)
        db_ref[0] = jnp.sum(g, axis=0, keepdims=True)

    row = BS((1, L, LANE), lambda b: (b, 0, 0))
    return _pc(body, name=name, grid=(B,),
               in_specs=[BS((1, L, LANE), lambda b: (b, 0, P_COLS // LANE - 1)), BS((1, LANE), lambda b: (0, 0)), row],
               out_specs=[row, BS((1, 1, LANE), lambda b: (b, 0, 0))],
               out_shape=[SDS((B, L, LANE), BF16), SDS((B, 1, LANE), F32)])(proj, dtb, ddt)


def _layer_fwd(i, x, wts, small, band_bias, riders=None, arrived=None):
    riders = riders or {}
    B, L, D = x.shape
    T = B * L
    x2 = x.reshape(T, D)
    h, proj2 = _norm_mm(x2, small["norm1_w"][i], wts["w_in"], name=f"in_proj_{i}", tn=1408)
    proj = proj2.reshape(B, L, P_COLS)
    xbc_act = _conv_fwd(proj, 2048 // 256, small["conv_w"][i], small["conv_b"][i], taps=SSM_CONV, ct=256,
                        out_dtype=F32, name=f"ssm_conv_{i}")
    dtb, alog = _dt_cols(small["dt_bias"][i]), _dt_cols(small["a_log"][i])
    dskip = jnp.repeat(small["d_skip"][i], HEAD_DIM).reshape(1, SSM_WIDTH)
    dtc = _dt_fwd(proj, dtb, name=f"dt_{i}")
    ypre, hs, carried = _ssd_fwd(xbc_act, dtc, alog, dskip, name=f"ssd_{i}", rider=riders.get("ssd"))
    if carried is not None:
        arrived("ssd", carried)
    mixed = _gate_fwd(ypre.reshape(T, SSM_WIDTH), proj2, small["ssm_norm_w"][i], name=f"gate_{i}")
    sinkcol = jnp.repeat(small["attn_sink"][i], BLOCK).reshape(ATTN_HEADS * BLOCK, 1)
    mixed, carried = _attn_fwd(proj, band_bias, sinkcol, mixed.reshape(B, L, 2 * D), name=f"attn_{i}", rider=riders.get("attn"))
    if carried is not None:
        arrived("attn", carried)
    mixed = mixed.reshape(T, 2 * D)
    x_mid = _mm(mixed, wts["w_out"], add=x2, name=f"out_proj_{i}")
    h2, gu2 = _norm_mm(x_mid, small["norm2_w"][i], wts["w_up"], name=f"up_proj_{i}", out_dtype=BF16, tn=1408)
    gu = gu2.reshape(B, L, 2 * D_FF)
    act = _conv_fwd(gu, 0, small["ffn_conv_w"][i], small["ffn_conv_b"][i], taps=FFN_CONV, ct=256, gate_blk0=D_FF // 256,
                    out_dtype=BF16, name=f"ffn_conv_{i}")
    x_out = _mm(act.reshape(T, D_FF), wts["w_down"], add=x_mid, name=f"down_proj_{i}", tk=1408)
    saved = dict(x2=x2, h=h, proj2=proj2, xbc_act=xbc_act, dtb=dtb, dtc=dtc, alog=alog, dskip=dskip, ypre=ypre, hs=hs, mixed=mixed,
                 sinkcol=sinkcol, x_mid=x_mid, h2=h2, gu=gu, act=act)
    return x_out.reshape(B, L, D), saved


def _layer_bwd(i, dx_out, dxb, sv, wts, small, band_bias, attn_rider=None, ssd_rider=None, arrived=None, wgrad_dtype=F32):
    T, D = dx_out.shape
    B, L = sv["gu"].shape[:2]
    g = {}
    dact = _mm(dxb, wts["w_down"], tb=True, out_dtype=BF16, name=f"d_act_{i}", tn=1408)
    g["w_down"] = _mm(sv["act"].reshape(T, D_FF), dxb, ta=True, out_dtype=wgrad_dtype, name=f"dw_down_{i}", tm=1408)
    dg, du, dcw, dcb = _conv_bwd(sv["gu"], 0, small["ffn_conv_w"][i], small["ffn_conv_b"][i], dact.reshape(B, L, D_FF),
                                 taps=FFN_CONV, ct=256, gate_blk0=D_FF // 256, name=f"d_ffn_conv_{i}")
    g["ffn_conv_w"] = jnp.sum(dcw, axis=0)[:FFN_CONV]
    g["ffn_conv_b"] = jnp.sum(dcb, axis=(0, 1))
    dg, du, w_up = dg.reshape(T, D_FF), du.reshape(T, D_FF), wts["w_up"]
    dh2 = _mm(dg, w_up[:, :D_FF], tb=True, name=f"d_h2_g_{i}", tk=1408)
    dh2 = _mm(du, w_up[:, D_FF:], tb=True, add=dh2, name=f"d_h2_u_{i}", tk=1408)
    dx_mid, dmb, dw2 = _dnorm(dh2, sv["x_mid"], small["norm2_w"][i], dx_out, name=f"d_norm2_{i}")
    g["w_up"] = jnp.concatenate([_mm(sv["h2"], dg, ta=True, out_dtype=wgrad_dtype, name=f"dw_up_g_{i}", tn=1408),
                                 _mm(sv["h2"], du, ta=True, out_dtype=wgrad_dtype, name=f"dw_up_u_{i}", tn=1408)], axis=1)
    g["norm2_w"] = dw2[0]
    dmixed = _mm(dmb, wts["w_out"], tb=True, out_dtype=BF16, name=f"d_mixed_{i}")
    g["w_out"] = _mm(sv["mixed"], dmb, ta=True, out_dtype=wgrad_dtype, name=f"dw_out_{i}")
    dypre, dz, dwn = _gate_bwd(sv["ypre"].reshape(T, SSM_WIDTH), sv["proj2"], small["ssm_norm_w"][i], dmixed, name=f"d_gate_{i}")
    g["ssm_norm_w"] = jnp.sum(dwn, axis=(0, 1))
    proj = sv["proj2"].reshape(B, L, P_COLS)
    dxs, dbm, dcm, ddt, dalog, ddsk, carried = _ssd_bwd(sv["xbc_act"], sv["dtc"], sv["alog"], sv["dskip"], sv["hs"],
                                                        dypre.reshape(B, L, SSM_WIDTH), name=f"d_ssd_{i}",
                                                        rider=ssd_rider(g) if ssd_rider is not None else None)
    if carried is not None:
        arrived("ssd", carried)
    ddt, ddtb = _dt_bwd(proj, sv["dtb"], ddt, name=f"d_dt_{i}")
    g["dt_bias"] = jnp.sum(ddtb, axis=(0, 1))[:2 * SSM_HEADS].reshape(2, SSM_HEADS)
    g["a_log"] = jnp.sum(dalog, axis=(0, 1, 2))[:2 * SSM_HEADS].reshape(2, SSM_HEADS)
    g["d_skip"] = jnp.sum(ddsk.reshape(B, SSM_HEADS, HEAD_DIM), axis=(0, 2))
    dxbc_act = dxs.at[:, :, SSM_WIDTH:SSM_WIDTH + BC_WIDTH].set(dbm).at[:, :, SSM_WIDTH + BC_WIDTH:].set(dcm)
    dxbc, dcw, dcb = _conv_bwd(proj, 2048 // 256, small["conv_w"][i], small["conv_b"][i], dxbc_act, taps=SSM_CONV, ct=256,
                               name=f"d_ssm_conv_{i}")
    g["conv_w"] = jnp.sum(dcw, axis=0)[:SSM_CONV]
    g["conv_b"] = jnp.sum(dcb, axis=(0, 1))
    dq, dk, dv, dbias, dsink, carried = _attn_bwd(proj, band_bias, sv["sinkcol"], dmixed.reshape(B, L, 2 * D), name=f"d_attn_{i}",
                                                  rider=attn_rider)
    if carried is not None:
        arrived("attn", carried)
    g["attn_sink"] = jnp.sum(dsink.reshape(ATTN_HEADS, BLOCK), axis=1)
    dproj = jnp.concatenate([dq, dz.reshape(B, L, SSM_WIDTH), dxbc, dk.astype(BF16), dv.astype(BF16), ddt], axis=-1).reshape(T, P_COLS)
    dh = _mm(dproj, wts["w_in"], tb=True, name=f"d_h_{i}", tk=1408)
    dx_in, dx_in_b, dw1 = _dnorm(dh, sv["x2"], small["norm1_w"][i], dx_mid, name=f"d_norm1_{i}")
    g["w_in"] = _unperm_in_cols(_mm(sv["h"], dproj, ta=True, out_dtype=wgrad_dtype, name=f"dw_in_{i}", tn=1408))
    g["norm1_w"] = dw1[0]
    return dx_in, dx_in_b, g, dbias


_BIG = ("w_in", "w_out", "w_up", "w_down")
_BIG_AXIS = {"w_in": 2, "w_out": 1, "w_up": 2, "w_down": 1}
_SMALL = ("rel_bias", "norm1_w", "conv_w", "conv_b", "dt_bias", "a_log", "d_skip", "ssm_norm_w", "attn_sink", "norm2_w",
          "ffn_conv_w", "ffn_conv_b", "final_norm_w")
_SMALL_SHARDED = ("conv_w", "ffn_conv_w")
_ORDER = ("rel_bias", "norm1_w", "w_in", "conv_w", "conv_b", "dt_bias", "a_log", "d_skip", "ssm_norm_w", "attn_sink", "w_out",
          "norm2_w", "w_up", "ffn_conv_w", "ffn_conv_b", "w_down", "final_norm_w")


def _local_step(x, target, small, wts=None, exchange=None):
    B, L, D = x.shape
    bucket = _bucket_table()
    band_bias = _bias_expand(small["rel_bias"], bucket, name="band_bias")

    def fetch(spec):
        return exchange["gather"](spec) if exchange is not None and spec else None

    def fetched(spec, carried):
        for (i, k), full in zip(spec, exchange["weights"](spec, carried)):
            wts[i][k] = full

    if exchange is not None:
        wts = [{} for _ in range(DEPTH)]
        fetched([(0, "w_in")], _run_rider(fetch([(0, "w_in")]), name="gather_w_in_0"))
    saved = []
    for i in range(DEPTH):
        nxt = i + 1 < DEPTH
        if i == 0:
            plan = {"ssd": [(0, "w_out"), (0, "w_up"), (0, "w_down")], "attn": [(1, "w_in"), (1, "w_out"), (1, "w_up")] if nxt else []}
        else:
            plan = {"ssd": [(i, "w_down")] + ([(i + 1, "w_in"), (i + 1, "w_out")] if nxt else []), "attn": [(i + 1, "w_up")] if nxt else []}
        x, sv = _layer_fwd(i, x, wts[i], small, band_bias, {c: fetch(s) for c, s in plan.items()}, lambda c, r: fetched(plan[c], r))
        saved.append(sv)
    loss, dx, dxb, dwf = _loss_head(x.reshape(B * L, D), small["final_norm_w"], target.reshape(B * L, D), name="loss_head")
    per_layer = []
    dbias = jnp.zeros((ATTN_HEADS, BLOCK, KEY_SPAN), F32)
    late = None
    for i in reversed(range(DEPTH)):
        own = [(i, "w_down"), (i, "w_up"), (i, "w_out")]
        plan = {"ssd": own, "attn": late[0] if late else []}
        attn_rider = exchange["scatter"](*late) if late else None
        ssd_rider = (lambda g: exchange["scatter"](own, [g[k] for _, k in own])) if exchange is not None else None
        dx, dxb, g, dbias_i = _layer_bwd(i, dx, dxb, saved[i], wts[i], small, band_bias, attn_rider, ssd_rider,
                                    lambda c, r: exchange["collect"](plan[c], r), F32 if exchange is None else BF16)
        if exchange is not None:
            late = ([(i, "w_in")], [g["w_in"]])
            for k in _BIG:
                g.pop(k)
        dbias = dbias + dbias_i
        per_layer.append(g)
    if exchange is not None:
        exchange["collect"](late[0], _run_rider(exchange["scatter"](*late), name="scatter_dw_in_0"))
    per_layer.reverse()
    grads = {k: jnp.stack([g[k] for g in per_layer]) for k in per_layer[0]}
    grads["rel_bias"] = _bias_reduce(dbias, bucket, name="d_rel_bias")
    grads["final_norm_w"] = dwf[0]
    return loss, dx.reshape(B, L, D), grads


def _split_by_chip(g, axis):
    shp = g.shape
    n = shp[axis] // N_CHIP
    g = g.reshape(shp[:axis] + (N_CHIP, n) + shp[axis + 1:])
    return jnp.moveaxis(g, axis, 0)


def _join_chips(a, axis):
    a = jnp.moveaxis(a, 0, axis)
    shp = a.shape
    return a.reshape(shp[:axis] + (shp[axis] * shp[axis + 1],) + shp[axis + 2:])


def kernel(x, rel_bias, norm1_w, w_in, conv_w, conv_b, dt_bias, a_log, d_skip, ssm_norm_w, attn_sink, w_out, norm2_w, w_up, ffn_conv_w, ffn_conv_b, w_down, final_norm_w, loss_target, m_rel_bias, m_norm1_w, m_w_in, m_conv_w, m_conv_b, m_dt_bias, m_a_log, m_d_skip, m_ssm_norm_w, m_attn_sink, m_w_out, m_norm2_w, m_w_up, m_ffn_conv_w, m_ffn_conv_b, m_w_down, m_final_norm_w, v_rel_bias, v_norm1_w, v_w_in, v_conv_w, v_conv_b, v_dt_bias, v_a_log, v_d_skip, v_ssm_norm_w, v_attn_sink, v_w_out, v_norm2_w, v_w_up, v_ffn_conv_w, v_ffn_conv_b, v_w_down, v_final_norm_w):
    w = dict(rel_bias=rel_bias, norm1_w=norm1_w, w_in=w_in, conv_w=conv_w, conv_b=conv_b, dt_bias=dt_bias, a_log=a_log,
             d_skip=d_skip, ssm_norm_w=ssm_norm_w, attn_sink=attn_sink, w_out=w_out, norm2_w=norm2_w, w_up=w_up,
             ffn_conv_w=ffn_conv_w, ffn_conv_b=ffn_conv_b, w_down=w_down, final_norm_w=final_norm_w)
    m = dict(rel_bias=m_rel_bias, norm1_w=m_norm1_w, w_in=m_w_in, conv_w=m_conv_w, conv_b=m_conv_b, dt_bias=m_dt_bias,
             a_log=m_a_log, d_skip=m_d_skip, ssm_norm_w=m_ssm_norm_w, attn_sink=m_attn_sink, w_out=m_w_out, norm2_w=m_norm2_w,
             w_up=m_w_up, ffn_conv_w=m_ffn_conv_w, ffn_conv_b=m_ffn_conv_b, w_down=m_w_down, final_norm_w=m_final_norm_w)
    v = dict(rel_bias=v_rel_bias, norm1_w=v_norm1_w, w_in=v_w_in, conv_w=v_conv_w, conv_b=v_conv_b, dt_bias=v_dt_bias,
             a_log=v_a_log, d_skip=v_d_skip, ssm_norm_w=v_ssm_norm_w, attn_sink=v_attn_sink, w_out=v_w_out, norm2_w=v_norm2_w,
             w_up=v_w_up, ffn_conv_w=v_ffn_conv_w, ffn_conv_b=v_ffn_conv_b, w_down=v_w_down, final_norm_w=v_final_norm_w)
    my_chip = 2 * lax.axis_index("x") + lax.axis_index("y")

    shards = {k: w[k].astype(BF16) for k in _BIG}
    received = {}

    def gather(spec):
        return _gather_rider([shards[k][i] for i, k in spec])

    def weights(spec, carried):
        out = []
        for (i, k), g_ in zip(spec, carried):
            full = _join_chips(g_, _BIG_AXIS[k] - 1)
            out.append(_perm_in_cols(full) if k == "w_in" else full)
        return out

    def scatter(spec, grads):
        layer = spec[0][0]
        bufs = [_split_by_chip(g_, _BIG_AXIS[k] - 1).astype(BF16) for (_, k), g_ in zip(spec, grads)]
        prev = [received[(layer % 2, k)] for _, k in spec] if layer + 2 < DEPTH else []
        return _scatter_rider(bufs, layer, prev)

    def collect(spec, carried):
        for (i, k), pieces in zip(spec, carried):
            received[(i % 2, k)] = pieces

    conv_shapes = [(DEPTH, SSM_CONV, CONV_CH), (DEPTH, FFN_CONV, D_FF)]
    placed = [lax.dynamic_update_slice_in_dim(jnp.zeros(s, F32), w[k], my_chip * w[k].shape[2], axis=2)
              for k, s in zip(_SMALL_SHARDED, conv_shapes)]
    lead = (lax.axis_index("c") == 0).astype(F32)
    conv_full = _unpack(_allreduce_small(_pack([p * lead for p in placed]), name="gather_conv_weights"), conv_shapes)
    small = {k: w[k] for k in _SMALL}
    small["conv_w"], small["ffn_conv_w"] = conv_full

    loss_part, grad_x, gp = _local_step(x, loss_target, small,
                                        exchange=dict(gather=gather, weights=weights, scatter=scatter, collect=collect))

    small_shapes = [small[k].shape for k in _SMALL] + [()]
    red = _unpack(_allreduce_small(_pack([gp[k] for k in _SMALL] + [loss_part[0, :1]]), name="reduce_small"), small_shapes)
    gsmall = dict(zip(_SMALL, red[:-1]))
    loss = red[-1]
    for k in _SMALL_SHARDED:
        n = w[k].shape[2]
        gsmall[k] = lax.dynamic_slice_in_dim(gsmall[k], my_chip * n, n, axis=2)

    fulls = []
    for k in _BIG:
        full = None
        for p in range(DEPTH // 2):
            full = _sum_sources(received[(p, k)], p, full, name=f"sum_{k}_{p}")
        fulls.append(full)
    gbig = dict(zip(_BIG, _join_halves(fulls, name="join_halves")))

    grad, delta, new_m, new_v = {}, {}, {}, {}
    for k in _BIG:
        shp = w[k].shape
        two = lambda a: a.reshape(shp[0] * shp[1], shp[2])
        d_, m_, v_ = _adamw(two(w[k]), two(gbig[k]), two(m[k]), two(v[k]), name=f"adamw_{k}")
        grad[k], delta[k], new_m[k], new_v[k] = gbig[k], d_.reshape(shp), m_.reshape(shp), v_.reshape(shp)
    shapes = [w[k].shape for k in _SMALL]
    d_, m_, v_ = _adamw(_pack([w[k] for k in _SMALL]), _pack([gsmall[k] for k in _SMALL]), _pack([m[k] for k in _SMALL]),
                        _pack([v[k] for k in _SMALL]), name="adamw_small")
    for k, a, b_, c_ in zip(_SMALL, _unpack(d_, shapes), _unpack(m_, shapes), _unpack(v_, shapes)):
        grad[k], delta[k], new_m[k], new_v[k] = gsmall[k], a, b_, c_
    return (loss, grad_x, *[grad[k] for k in _ORDER], *[delta[k] for k in _ORDER], *[new_m[k] for k in _ORDER],
            *[new_v[k] for k in _ORDER])
```

```python
import functools
import math

import jax
import jax.numpy as jnp
import numpy as np
from jax import lax
from jax.experimental import pallas as pl
from jax.experimental.pallas import tpu as pltpu

F32 = jnp.float32
BF16 = jnp.bfloat16
BS = pl.BlockSpec
SDS = jax.ShapeDtypeStruct
MESH = pl.DeviceIdType.MESH

D_MODEL = 1024
DEPTH = 4
SSM_HEADS = 16
SSM_WIDTH = 1024
BC_WIDTH = 256
CONV_CH = 1536
SSM_CONV = 7
CHUNK = 128
ATTN_HEADS = 16
KV_HEADS = 4
HEAD_DIM = 64
WINDOW = 128
BLOCK = 128
KEY_SPAN = 384
REL_BUCKETS = 32
REL_MAX_DIST = 128
D_FF = 2816
FFN_CONV = 3
NORM_EPS = 1e-6
Z_END = 1024
XBC_END = 2560
DT_END = 2592
Q_END = 3616
K_END = 3872
IN_COLS = 4128
P_COLS = 4224
ADAM_LR, ADAM_B1, ADAM_B2, ADAM_EPS, ADAM_WD, ADAM_STEP = 0.001, 0.9, 0.999, 1e-08, 0.01, 10
NEG = -1e30
N_DEV = 8
N_CHIP = 4
LANE = 128
VMEM_LIMIT_BYTES = 48 * 1024 * 1024


def _pc(body, *, name, grid, in_specs, out_specs, out_shape, scratch_shapes=(), aliases=None):
    return pl.pallas_call(
        body, name=name, grid=grid, in_specs=in_specs, out_specs=out_specs, out_shape=out_shape,
        scratch_shapes=list(scratch_shapes), input_output_aliases=aliases or {},
        compiler_params=pltpu.CompilerParams(dimension_semantics=("arbitrary",) * len(grid),
                                             vmem_limit_bytes=VMEM_LIMIT_BYTES))


def _split_rider_refs(refs, n_in, n_out, rider):
    n_rin = len(rider["ins"]) + len(rider["prev"])
    n_rout = len(rider["out_shape"])
    core = refs[:n_in] + refs[n_in + n_rin:n_in + n_rin + n_out] + refs[n_in + n_rin + n_out + n_rout + len(rider["scratch"]):]
    rins = refs[n_in:n_in + len(rider["ins"])]
    routs = refs[n_in + n_rin + n_out:n_in + n_rin + n_out + n_rout]
    sems = refs[n_in + n_rin + n_out + n_rout:n_in + n_rin + n_out + n_rout + len(rider["scratch"])]
    return core, rins, routs, sems


def _pc_carry(body, args, *, name, grid, in_specs, out_specs, out_shape, scratch_shapes=(), rider=None, aliases=None):
    if rider is None:
        return _pc(body, name=name, grid=grid, in_specs=in_specs, out_specs=out_specs, out_shape=out_shape,
                   scratch_shapes=scratch_shapes, aliases=aliases)(*args), None
    n_in, n_out = len(in_specs), len(out_shape)
    any_spec = BS(memory_space=pl.ANY)

    def full(*refs):
        core, rins, routs, sems = _split_rider_refs(refs, n_in, n_out, rider)
        ids = [pl.program_id(d) for d in range(len(grid))]
        first = functools.reduce(jnp.logical_and, [i == 0 for i in ids])
        last = functools.reduce(jnp.logical_and, [i == g - 1 for i, g in zip(ids, grid)])

        @pl.when(first)
        def _():
            rider["start"](rins, routs, sems)

        body(*core)

        @pl.when(last)
        def _():
            rider["finish"](rins, routs, sems)

    n_rin = len(rider["ins"])
    outs = pl.pallas_call(
        full, name=name, grid=grid,
        in_specs=list(in_specs) + [any_spec] * (n_rin + len(rider["prev"])),
        out_specs=list(out_specs) + [any_spec] * len(rider["out_shape"]),
        out_shape=list(out_shape) + list(rider["out_shape"]),
        scratch_shapes=list(rider["scratch"]) + list(scratch_shapes),
        input_output_aliases={**(aliases or {}), **{n_in + n_rin + t: n_out + t for t in range(len(rider["prev"]))}},
        compiler_params=pltpu.CompilerParams(dimension_semantics=("arbitrary",) * len(grid), vmem_limit_bytes=VMEM_LIMIT_BYTES,
                                             has_side_effects=True))(*args, *rider["ins"], *rider["prev"])
    return outs[:n_out], outs[n_out:]


def _run_rider(rider, *, name):
    any_spec = BS(memory_space=pl.ANY)
    n_rin = len(rider["ins"])

    def body(*refs):
        _, rins, routs, sems = _split_rider_refs(refs, 0, 0, rider)
        rider["start"](rins, routs, sems)
        rider["finish"](rins, routs, sems)

    return pl.pallas_call(
        body, name=name, in_specs=[any_spec] * (n_rin + len(rider["prev"])), out_specs=[any_spec] * len(rider["out_shape"]),
        out_shape=list(rider["out_shape"]), scratch_shapes=list(rider["scratch"]),
        input_output_aliases={n_rin + t: t for t in range(len(rider["prev"]))},
        compiler_params=pltpu.CompilerParams(has_side_effects=True))(*rider["ins"], *rider["prev"])


def _div_tile(n, pref, mult):
    t = min(pref, n)
    t -= t % mult
    while t >= mult:
        if n % t == 0:
            return t
        t -= mult
    return n


def _mm(a, b, *, name, ta=False, tb=False, add=None, out_dtype=F32, tm=1024, tn=1024, tk=1024):
    if ta:
        K, M = a.shape
    else:
        M, K = a.shape
    N = b.shape[0] if tb else b.shape[1]
    tm, tn, tk = _div_tile(M, tm, LANE), _div_tile(N, tn, LANE), _div_tile(K, tk, LANE)
    nk = K // tk
    dims = (((0,) if ta else (1,), (1,) if tb else (0,)), ((), ()))

    def body_single(*refs):
        r = lax.dot_general(refs[0][...], refs[1][...], dims, preferred_element_type=F32)
        if add is not None:
            r = r + refs[2][...]
        refs[-1][...] = r.astype(out_dtype)

    def body(*refs):
        if add is None:
            a_ref, b_ref, o_ref, acc_ref = refs
        else:
            a_ref, b_ref, add_ref, o_ref, acc_ref = refs
        k = pl.program_id(2)

        @pl.when(k == 0)
        def _():
            acc_ref[...] = jnp.zeros_like(acc_ref)

        acc_ref[...] += lax.dot_general(a_ref[...], b_ref[...], dims, preferred_element_type=F32)

        @pl.when(k == nk - 1)
        def _():
            r = acc_ref[...]
            if add is not None:
                r = r + add_ref[...]
            o_ref[...] = r.astype(out_dtype)

    a_spec = BS((tk, tm), lambda i, j, k: (k, i)) if ta else BS((tm, tk), lambda i, j, k: (i, k))
    b_spec = BS((tn, tk), lambda i, j, k: (j, k)) if tb else BS((tk, tn), lambda i, j, k: (k, j))
    in_specs, args = [a_spec, b_spec], [a, b]
    if add is not None:
        in_specs.append(BS((tm, tn), lambda i, j, k: (i, j)))
        args.append(add)
    return _pc(body_single if nk == 1 else body, name=name, grid=(M // tm, N // tn, nk), in_specs=in_specs,
               out_specs=BS((tm, tn), lambda i, j, k: (i, j)), out_shape=SDS((M, N), out_dtype),
               scratch_shapes=[] if nk == 1 else [pltpu.VMEM((tm, tn), F32)])(*args)


def _dot(a, b, dims):
    return lax.dot_general(a.astype(BF16), b.astype(BF16), (dims, ((), ())), preferred_element_type=F32)


@jax.custom_vjp
def _nn(a, b):
    return _dot(a, b, ((1,), (0,)))


@jax.custom_vjp
def _nt(a, b):
    return _dot(a, b, ((1,), (1,)))


@jax.custom_vjp
def _tn(a, b):
    return _dot(a, b, ((0,), (0,)))


_nn.defvjp(lambda a, b: (_nn(a, b), (a, b)), lambda r, g: (_nt(g, r[1]), _tn(r[0], g)))
_nt.defvjp(lambda a, b: (_nt(a, b), (a, b)), lambda r, g: (_nn(g, r[1]), _tn(g, r[0])))
_tn.defvjp(lambda a, b: (_tn(a, b), (a, b)), lambda r, g: (_nt(r[1], g), _nn(r[0], g)))


def _hdot(m, x):
    hi = x.astype(BF16)
    r1 = x - hi.astype(F32)
    lo = r1.astype(BF16)
    lo2 = (r1 - lo.astype(F32)).astype(BF16)
    n = x.shape[1]
    out = lax.dot_general(m.astype(BF16), jnp.concatenate([hi, lo, lo2], axis=1), (((1,), (0,)), ((), ())),
                          preferred_element_type=F32)
    return out[:, :n] + out[:, n:2 * n] + out[:, 2 * n:]


@jax.custom_vjp
def _cumdot(m, mt, x):
    return _hdot(m, x)


_cumdot.defvjp(lambda m, mt, x: (_hdot(m, x), (m, mt)),
               lambda r, g: (jnp.zeros_like(r[0]), jnp.zeros_like(r[1]), _hdot(r[1], g)))


def _sigmoid(x):
    return 1.0 / (1.0 + jnp.exp(-x))


def _softplus(x):
    return jnp.maximum(x, 0.0) + jnp.log(1.0 + jnp.exp(-jnp.abs(x)))


def _rms(x, w):
    return x * lax.rsqrt(jnp.mean(x * x, axis=-1, keepdims=True) + NORM_EPS) * w


def _norm_mm(x2, nw, b, *, name, out_dtype=F32, tm=1024, tn=1024):
    T, D = x2.shape
    N = b.shape[1]
    tm, tn = _div_tile(T, tm, LANE), _div_tile(N, tn, LANE)

    def body(x_ref, w_ref, b_ref, h_ref, o_ref):
        @pl.when(pl.program_id(1) == 0)
        def _():
            h_ref[...] = _rms(x_ref[...], w_ref[...]).astype(BF16)

        o_ref[...] = lax.dot_general(h_ref[...], b_ref[...], (((1,), (0,)), ((), ())), preferred_element_type=F32).astype(out_dtype)

    return _pc(body, name=name, grid=(T // tm, N // tn),
               in_specs=[BS((tm, D), lambda i, j: (i, 0)), BS((1, D), lambda i, j: (0, 0)), BS((D, tn), lambda i, j: (0, j))],
               out_specs=[BS((tm, D), lambda i, j: (i, 0)), BS((tm, tn), lambda i, j: (i, j))],
               out_shape=[SDS((T, D), BF16), SDS((T, N), out_dtype)])(x2, nw.reshape(1, D), b)


def _dnorm(dh, x2, nw, resid, *, name):
    T, D = x2.shape
    tr = _div_tile(T, 512, 16)

    def body(x_ref, w_ref, dh_ref, r_ref, dx_ref, dxb_ref, dw_ref):
        _, vjp = jax.vjp(_rms, x_ref[...], w_ref[...])
        dx, dw = vjp(dh_ref[...])
        dx = dx + r_ref[...]
        dx_ref[...] = dx
        dxb_ref[...] = dx.astype(BF16)

        @pl.when(pl.program_id(0) == 0)
        def _():
            dw_ref[...] = jnp.zeros_like(dw_ref)

        dw_ref[...] += dw

    row = BS((tr, D), lambda i: (i, 0))
    one = BS((1, D), lambda i: (0, 0))
    return _pc(body, name=name, grid=(T // tr,), in_specs=[row, one, row, row], out_specs=[row, row, one],
               out_shape=[SDS((T, D), F32), SDS((T, D), BF16), SDS((1, D), F32)])(x2, nw.reshape(1, D), dh, resid)


ROW_PAD = 8


def _pad_rows(x):
    return jnp.concatenate([x, jnp.zeros((ROW_PAD, x.shape[1]), x.dtype)], axis=0)


def _shift_rows(xp, s):
    n = xp.shape[0] - ROW_PAD
    return xp[:n] if s == 0 else pltpu.roll(xp, (-s) % xp.shape[0], 0)[:n]


def _conv_taps(x, taps):
    xp = _pad_rows(x)
    return [_shift_rows(xp, k - taps // 2) for k in range(taps)]


def _conv_pre(xs, w_ref, b_ref):
    c = b_ref[...] + w_ref[0:1, :] * xs[0]
    for k in range(1, len(xs)):
        c = c + w_ref[k:k + 1, :] * xs[k]
    return c


def _conv_fwd(x3, x_blk0, w, b, *, taps, ct, gate_blk0=None, out_dtype, name):
    B, L, _ = x3.shape
    C = w.shape[1]
    wp = jnp.zeros((8, C), F32).at[:taps].set(w)

    def body(*refs):
        if gate_blk0 is None:
            x_ref, w_ref, b_ref, o_ref = refs
        else:
            x_ref, u_ref, w_ref, b_ref, o_ref = refs
        c = _conv_pre(_conv_taps(x_ref[0].astype(F32), taps), w_ref, b_ref)
        y = c * _sigmoid(c)
        if gate_blk0 is not None:
            y = y * u_ref[0].astype(F32)
        o_ref[0] = y.astype(out_dtype)

    in_specs = [BS((1, L, ct), lambda bi, j: (bi, 0, x_blk0 + j))]
    args = [x3]
    if gate_blk0 is not None:
        in_specs.append(BS((1, L, ct), lambda bi, j: (bi, 0, gate_blk0 + j)))
        args.append(x3)
    in_specs += [BS((8, ct), lambda bi, j: (0, j)), BS((1, ct), lambda bi, j: (0, j))]
    args += [wp, b.reshape(1, C)]
    return _pc(body, name=name, grid=(B, C // ct), in_specs=in_specs,
               out_specs=BS((1, L, ct), lambda bi, j: (bi, 0, j)), out_shape=SDS((B, L, C), out_dtype))(*args)


def _conv_bwd(x3, x_blk0, w, b, dy3, *, taps, ct, gate_blk0=None, name):
    B, L, _ = x3.shape
    C = w.shape[1]
    wp = jnp.zeros((8, C), F32).at[:taps].set(w)
    gated = gate_blk0 is not None

    def body(*refs):
        if gated:
            x_ref, u_ref, w_ref, b_ref, dy_ref, dx_ref, du_ref, dw_ref, db_ref = refs
        else:
            x_ref, w_ref, b_ref, dy_ref, dx_ref, dw_ref, db_ref = refs
        xs = _conv_taps(x_ref[0].astype(F32), taps)
        dy = dy_ref[0].astype(F32)
        c = _conv_pre(xs, w_ref, b_ref)
        sg = _sigmoid(c)
        dsilu = sg * (1.0 + c * (1.0 - sg))
        if gated:
            du_ref[0] = (dy * (c * sg)).astype(BF16)
            dc = dy * u_ref[0].astype(F32) * dsilu
        else:
            dc = dy * dsilu
        dcp = _pad_rows(dc)
        dx = jnp.zeros_like(dc)
        dw_ref[0] = jnp.zeros((8, ct), F32)
        for k in range(taps):
            dx = dx + w_ref[k:k + 1, :] * _shift_rows(dcp, taps // 2 - k)
            dw_ref[0, k:k + 1, :] = jnp.sum(dc * xs[k], axis=0, keepdims=True)
        dx_ref[0] = dx.astype(BF16)
        db_ref[0] = jnp.sum(dc, axis=0, keepdims=True)

    xs = BS((1, L, ct), lambda bi, j: (bi, 0, x_blk0 + j))
    ys = BS((1, L, ct), lambda bi, j: (bi, 0, j))
    in_specs, args = [xs], [x3]
    if gated:
        in_specs.append(BS((1, L, ct), lambda bi, j: (bi, 0, gate_blk0 + j)))
        args.append(x3)
    in_specs += [BS((8, ct), lambda bi, j: (0, j)), BS((1, ct), lambda bi, j: (0, j)), ys]
    args += [wp, b.reshape(1, C), dy3]
    out_specs = [ys] + ([ys] if gated else []) + [BS((1, 8, ct), lambda bi, j: (bi, 0, j)), BS((1, 1, ct), lambda bi, j: (bi, 0, j))]
    out_shape = [SDS((B, L, C), BF16)] + ([SDS((B, L, C), BF16)] if gated else []) + [SDS((B, 8, C), F32), SDS((B, 1, C), F32)]
    return _pc(body, name=name, grid=(B, C // ct), in_specs=in_specs, out_specs=out_specs, out_shape=out_shape)(*args)


def _tri(reverse):
    r = lax.broadcasted_iota(jnp.int32, (CHUNK, CHUNK), 0)
    c = lax.broadcasted_iota(jnp.int32, (CHUNK, CHUNK), 1)
    return (c >= r) if reverse else (c <= r)


PAIRS = 2
QUADS = SSM_HEADS // (2 * PAIRS)
QW = PAIRS * LANE


def _ssd_chunk(h0, h1, x0, x1, bm, cm, dtc, alog, *, col0, reverse):
    mask = _tri(reverse)
    eye = lax.broadcasted_iota(jnp.int32, (CHUNK, CHUNK), 0) == lax.broadcasted_iota(jnp.int32, (CHUNK, CHUNK), 1)
    lane = lax.broadcasted_iota(jnp.int32, (1, LANE), 1)
    first = lane < HEAD_DIM
    adt = dtc * (-jnp.exp(alog))
    cumc = _cumdot(mask.astype(F32), _tri(not reverse).astype(F32), adt)
    totc = jnp.sum(adt, axis=0, keepdims=True)
    cb = _nt(cm, bm)

    def col(v, c):
        return jnp.sum(jnp.where(lane == c, v, 0.0), axis=1, keepdims=True)

    outs, states = [], []
    for p, (hprev, xs) in enumerate(((h0, x0), (h1, x1))):
        c0 = col0 + 2 * p
        cj = (col(cumc, c0), col(cumc, c0 + 1))
        cum = jnp.where(first, cj[0], cj[1])
        tot = jnp.where(first, col(totc, c0), col(totc, c0 + 1))
        xdt = xs * jnp.where(first, col(dtc, c0), col(dtc, c0 + 1))
        y = _nn(cm, hprev) * jnp.exp(cum)
        for j in range(2):
            rj = jnp.sum(jnp.where(eye, cj[j], 0.0), axis=0, keepdims=True)
            dec = jnp.exp(jnp.where(mask, cj[j] - rj, NEG))
            y = y + _nn(cb * dec, jnp.where(first if j == 0 else ~first, xdt, 0.0))
        outs.append(y)
        states.append(hprev * jnp.exp(tot) + _tn(bm, xdt * jnp.exp(tot - cum)))
    return outs[0], outs[1], states[0], states[1]


def _ssd_specs(B, L):
    def lanes(w, blk):
        return BS((1, L, w), blk)

    return [
        lanes(QW, lambda b, q: (b, 0, q)),
        lanes(LANE, lambda b, q: (b, 0, 8 + q // 2)),
        lanes(LANE, lambda b, q: (b, 0, 10 + q // 2)),
        lanes(LANE, lambda b, q: (b, 0, 0)),
        BS((1, LANE), lambda b, q: (0, 0)),
        BS((1, QW), lambda b, q: (0, q)),
    ]


def _ssd_slot(d, ci):
    return ci if d == 0 else ci + 1


def _ssd_fwd(xbc_act, dtc, alog, dskip, *, name, rider=None):
    B, L, _ = xbc_act.shape
    nc = L // CHUNK

    def body(xs_ref, b_ref, c_ref, dt_ref, alog_ref, dsk_ref, y_ref, hs_ref):
        q = pl.program_id(1)
        alog_v = alog_ref[...]
        y_ref[0] = dsk_ref[...] * xs_ref[0]
        hs_ref[0, 0, 0, 0] = jnp.zeros((LANE, QW), F32)
        hs_ref[0, 0, 1, nc] = jnp.zeros((LANE, QW), F32)

        def step(i, carry):
            cis = (i, nc - 1 - i)
            rows = [pl.ds(pl.multiple_of(ci * CHUNK, CHUNK), CHUNK) for ci in cis]
            res = []
            for d in range(2):
                cur = _ssd_slot(d, cis[d])
                res.append(_ssd_chunk(
                    hs_ref[0, 0, d, cur, :, :LANE], hs_ref[0, 0, d, cur, :, LANE:], xs_ref[0, rows[d], :LANE],
                    xs_ref[0, rows[d], LANE:], b_ref[0, rows[d], :], c_ref[0, rows[d], :], dt_ref[0, rows[d], :], alog_v,
                    col0=SSM_HEADS * d + 2 * PAIRS * q, reverse=d == 1))
            for d in range(2):
                y0, y1, n0, n1 = res[d]
                nxt = _ssd_slot(d, cis[d] + 1 if d == 0 else cis[d] - 1)
                hs_ref[0, 0, d, nxt, :, :LANE] = n0
                hs_ref[0, 0, d, nxt, :, LANE:] = n1
                y_ref[0, rows[d], :LANE] += y0
                y_ref[0, rows[d], LANE:] += y1
            return carry

        lax.fori_loop(0, nc, step, 0, unroll=2)

    (y, hs), carried = _pc_carry(
        body, (xbc_act, xbc_act, xbc_act, dtc, alog, dskip), name=name, grid=(B, QUADS), in_specs=_ssd_specs(B, L),
        out_specs=[BS((1, L, QW), lambda b, q: (b, 0, q)), BS((1, 1, 2, nc + 1, LANE, QW), lambda b, q: (b, q, 0, 0, 0, 0))],
        out_shape=[SDS((B, L, SSM_WIDTH), F32), SDS((B, QUADS, 2, nc + 1, LANE, QW), F32)], rider=rider)
    return y, hs, carried


def _ssd_bwd(xbc_act, dtc, alog, dskip, hs, dy, *, name, rider=None):
    B, L, _ = xbc_act.shape
    nc = L // CHUNK

    def body(xs_ref, b_ref, c_ref, dt_ref, alog_ref, dsk_ref, hs_ref, dy_ref,
             dxs_ref, db_ref, dc_ref, ddt_ref, dalog_ref, ddsk_ref, dh_ref):
        q = pl.program_id(1)
        alog_v = alog_ref[...]

        @pl.when(q % 2 == 0)
        def _():
            db_ref[...] = jnp.zeros_like(db_ref)
            dc_ref[...] = jnp.zeros_like(dc_ref)

        @pl.when(q == 0)
        def _():
            ddt_ref[...] = jnp.zeros_like(ddt_ref)

        dxs_ref[0] = dy_ref[0] * dsk_ref[...]
        ddsk_ref[0] = jnp.sum(dy_ref[0] * xs_ref[0], axis=0, keepdims=True)
        dh_ref[...] = jnp.zeros_like(dh_ref)

        def step(i, carry):
            g_alog = carry
            cis = (nc - 1 - i, i)
            rows = [pl.ds(pl.multiple_of(ci * CHUNK, CHUNK), CHUNK) for ci in cis]
            res = []
            for d in range(2):
                cur = _ssd_slot(d, cis[d])
                fn = functools.partial(_ssd_chunk, col0=SSM_HEADS * d + 2 * PAIRS * q, reverse=d == 1)
                _, vjp = jax.vjp(fn, hs_ref[0, 0, d, cur, :, :LANE], hs_ref[0, 0, d, cur, :, LANE:], xs_ref[0, rows[d], :LANE],
                                 xs_ref[0, rows[d], LANE:], b_ref[0, rows[d], :], c_ref[0, rows[d], :], dt_ref[0, rows[d], :],
                                 alog_v)
                res.append(vjp((dy_ref[0, rows[d], :LANE], dy_ref[0, rows[d], LANE:], dh_ref[d, :, :LANE], dh_ref[d, :, LANE:])))
            for d in range(2):
                g_h0, g_h1, g_x0, g_x1, g_b, g_c, g_dt, g_alog1 = res[d]
                dh_ref[d, :, :LANE] = g_h0
                dh_ref[d, :, LANE:] = g_h1
                dxs_ref[0, rows[d], :LANE] += g_x0
                dxs_ref[0, rows[d], LANE:] += g_x1
                db_ref[0, rows[d], :] += g_b
                dc_ref[0, rows[d], :] += g_c
                ddt_ref[0, rows[d], :] += g_dt
                g_alog = g_alog + g_alog1
            return g_alog

        dalog_ref[0, 0] = lax.fori_loop(0, nc, step, jnp.zeros((1, LANE), F32))

    lanes = lambda w, blk: BS((1, L, w), blk)
    in_specs = _ssd_specs(B, L) + [BS((1, 1, 2, nc + 1, LANE, QW), lambda b, q: (b, q, 0, 0, 0, 0)), lanes(QW, lambda b, q: (b, 0, q))]
    out_specs = [lanes(QW, lambda b, q: (b, 0, q)), lanes(LANE, lambda b, q: (b, 0, q // 2)), lanes(LANE, lambda b, q: (b, 0, q // 2)),
                 lanes(LANE, lambda b, q: (b, 0, 0)), BS((1, 1, 1, LANE), lambda b, q: (b, q, 0, 0)),
                 BS((1, 1, QW), lambda b, q: (b, 0, q))]
    out_shape = [SDS((B, L, CONV_CH), F32), SDS((B, L, BC_WIDTH), F32), SDS((B, L, BC_WIDTH), F32), SDS((B, L, LANE), F32),
                 SDS((B, QUADS, 1, LANE), F32), SDS((B, 1, SSM_WIDTH), F32)]
    outs, carried = _pc_carry(body, (xbc_act, xbc_act, xbc_act, dtc, alog, dskip, hs, dy), name=name, grid=(B, QUADS),
                              in_specs=in_specs, out_specs=out_specs, out_shape=out_shape,
                              scratch_shapes=[pltpu.VMEM((2, LANE, QW), F32)], rider=rider)
    return (*outs, carried)


def _gate_norm(yp, z, w):
    v = yp * (z * _sigmoid(z))
    return v * lax.rsqrt(jnp.mean(v * v, axis=-1, keepdims=True) + NORM_EPS) * w


def _gate_fwd(ypre2, proj2, w, *, name):
    T = ypre2.shape[0]
    tr = _div_tile(T, 512, 8)
    G = 512

    def body(y_ref, z_ref, w_ref, o_ref):
        o_ref[...] = _gate_norm(y_ref[...], z_ref[...], w_ref[...]).astype(BF16)

    return _pc(body, name=name, grid=(T // tr, 2),
               in_specs=[BS((tr, G), lambda i, g: (i, g)), BS((tr, G), lambda i, g: (i, 2 + g)), BS((1, G), lambda i, g: (0, g))],
               out_specs=BS((tr, G), lambda i, g: (i, g)), out_shape=SDS((T, 2 * SSM_WIDTH), BF16))(ypre2, proj2, w.reshape(1, -1))


def _gate_bwd(ypre2, proj2, w, dy, *, name):
    T = ypre2.shape[0]
    tr = _div_tile(T, 512, 8)
    G = 512

    def body(y_ref, z_ref, w_ref, dy_ref, dyp_ref, dz_ref, dw_ref):
        _, vjp = jax.vjp(_gate_norm, y_ref[...], z_ref[...], w_ref[...])
        dyp, dz, dw = vjp(dy_ref[...].astype(F32))
        dyp_ref[...] = dyp
        dz_ref[...] = dz.astype(BF16)
        dw_ref[0] = dw

    tile = BS((tr, G), lambda i, g: (i, g))
    return _pc(body, name=name, grid=(T // tr, 2),
               in_specs=[tile, BS((tr, G), lambda i, g: (i, 2 + g)), BS((1, G), lambda i, g: (0, g)), tile],
               out_specs=[tile, tile, BS((1, 1, G), lambda i, g: (i, 0, g))],
               out_shape=[SDS((T, SSM_WIDTH), F32), SDS((T, SSM_WIDTH), BF16), SDS((T // tr, 1, SSM_WIDTH), F32)])(
        ypre2, proj2, w.reshape(1, -1), dy)


def _first_half():
    return lax.broadcasted_iota(jnp.int32, (1, LANE), 1) < HEAD_DIM


def _dup_kv_head(pair, odd):
    rolled = pltpu.roll(pair, HEAD_DIM, 1)
    return jnp.where(_first_half(), rolled, pair) if odd else jnp.where(_first_half(), pair, rolled)


def _stack_heads(quad):
    first = _first_half()
    lo, hi = quad[:, :LANE], quad[:, LANE:]
    return jnp.concatenate([jnp.where(first, lo, 0.0), jnp.where(first, 0.0, lo), jnp.where(first, hi, 0.0),
                            jnp.where(first, 0.0, hi)], axis=0)


def _unstack_heads(o):
    first = _first_half()
    return jnp.concatenate([jnp.where(first, o[:BLOCK], o[BLOCK:2 * BLOCK]), jnp.where(first, o[2 * BLOCK:3 * BLOCK], o[3 * BLOCK:])], axis=1)


def _fold_kv_head(d, odd):
    tot = d + pltpu.roll(d, HEAD_DIM, 1)
    return jnp.where(_first_half(), 0.0, tot) if odd else jnp.where(_first_half(), tot, 0.0)


def _attn_softmax(s, sink):
    m = jnp.maximum(jnp.max(s, axis=-1, keepdims=True), sink)
    p = jnp.exp(s - m)
    ps = jnp.exp(sink - m)
    inv = 1.0 / (jnp.sum(p, axis=-1, keepdims=True) + ps)
    return p * inv, ps * inv


def _attn_colneg(n, L):
    kpos = n * BLOCK - WINDOW + lax.broadcasted_iota(jnp.int32, (1, KEY_SPAN), 1)
    return jnp.where((kpos >= 0) & (kpos < L), 0.0, NEG)


def _attn_in_specs(L):
    nblk = L // BLOCK
    kv = lambda o, col: BS((1, BLOCK, 4 * HEAD_DIM), lambda b, n: (b, jnp.clip(n + o, 0, nblk - 1), col))
    kcol, vcol = 3584 // 256, 3840 // 256
    return [BS((1, BLOCK, ATTN_HEADS * HEAD_DIM), lambda b, n: (b, n, 0)), kv(-1, kcol), kv(0, kcol), kv(1, kcol),
            kv(-1, vcol), kv(0, vcol), kv(1, vcol),
            BS((ATTN_HEADS, BLOCK, KEY_SPAN), lambda b, n: (0, 0, 0)), BS((ATTN_HEADS * BLOCK, 1), lambda b, n: (0, 0))]


def _attn_fwd(proj, bias, sinkcol, mixed, *, name, rider=None):
    B, L, _ = proj.shape

    def body(q_ref, k0, k1, k2, v0, v1, v2, bias_ref, sink_ref, _, o_ref):
        colneg = _attn_colneg(pl.program_id(1), L)
        kcat = jnp.concatenate([k0[0], k1[0], k2[0]], axis=0)
        vcat = jnp.concatenate([v0[0], v1[0], v2[0]], axis=0)
        scores, probs, scales = [], [], []
        for g in range(KV_HEADS):
            pair = slice(LANE * (g // 2), LANE * (g // 2) + LANE)
            quad = slice(4 * HEAD_DIM * g, 4 * HEAD_DIM * (g + 1))
            kd = _dup_kv_head(kcat[:, pair], g % 2).astype(BF16)
            qs = (_stack_heads(q_ref[0, :, quad]) * HEAD_DIM ** -0.5).astype(BF16)
            scores.append(lax.dot_general(qs, kd, (((1,), (1,)), ((), ())), preferred_element_type=F32))
        for g in range(KV_HEADS):
            s = scores[g] + bias_ref[4 * g:4 * g + 4].reshape(4 * BLOCK, KEY_SPAN) + colneg
            sink = sink_ref[4 * BLOCK * g:4 * BLOCK * (g + 1)]
            m = jnp.maximum(jnp.max(s, axis=-1, keepdims=True), sink)
            p = jnp.exp(s - m)
            scales.append(1.0 / (jnp.sum(p, axis=-1, keepdims=True) + jnp.exp(sink - m)))
            probs.append(p.astype(BF16))
        for g in range(KV_HEADS):
            pair = slice(LANE * (g // 2), LANE * (g // 2) + LANE)
            quad = slice(4 * HEAD_DIM * g, 4 * HEAD_DIM * (g + 1))
            vd = _dup_kv_head(vcat[:, pair], g % 2).astype(BF16)
            o = lax.dot_general(probs[g], vd, (((1,), (0,)), ((), ())), preferred_element_type=F32) * scales[g]
            o_ref[0, :, quad] = _unstack_heads(o).astype(BF16)

    (out,), carried = _pc_carry(body, (proj, proj, proj, proj, proj, proj, proj, bias, sinkcol, mixed), name=name,
                                grid=(B, L // BLOCK), in_specs=_attn_in_specs(L) + [BS(memory_space=pl.ANY)],
                                out_specs=[BS((1, BLOCK, ATTN_HEADS * HEAD_DIM), lambda b, n: (b, n, 1))],
                                out_shape=[SDS(mixed.shape, BF16)], rider=rider, aliases={9: 0})
    return out, carried


def _attn_bwd(proj, bias, sinkcol, dout, *, name, rider=None):
    B, L, _ = proj.shape
    nblk = L // BLOCK
    nn, nt, tn = (((1,), (0,)), ((), ())), (((1,), (1,)), ((), ())), (((0,), (0,)), ((), ()))

    def body(q_ref, k0, k1, k2, v0, v1, v2, bias_ref, sink_ref, do_ref, dq_ref, dk_ref, dv_ref, dbias_ref, dsink_ref):
        b, n = pl.program_id(0), pl.program_id(1)

        @pl.when(n == 0)
        def _():
            dk_ref[...] = jnp.zeros_like(dk_ref)
            dv_ref[...] = jnp.zeros_like(dv_ref)

        @pl.when((n == 0) & (b == 0))
        def _():
            dbias_ref[...] = jnp.zeros_like(dbias_ref)
            dsink_ref[...] = jnp.zeros_like(dsink_ref)

        colneg = _attn_colneg(n, L)
        kcat = jnp.concatenate([k0[0], k1[0], k2[0]], axis=0)
        vcat = jnp.concatenate([v0[0], v1[0], v2[0]], axis=0)
        krows = [pl.ds(pl.multiple_of(jnp.clip(n + o, 0, nblk - 1) * BLOCK, BLOCK), BLOCK) for o in (-1, 0, 1)]
        ops, mids = [], []
        for g in range(KV_HEADS):
            pair = slice(LANE * (g // 2), LANE * (g // 2) + LANE)
            quad = slice(4 * HEAD_DIM * g, 4 * HEAD_DIM * (g + 1))
            kd = _dup_kv_head(kcat[:, pair], g % 2).astype(BF16)
            vd = _dup_kv_head(vcat[:, pair], g % 2).astype(BF16)
            qs = (_stack_heads(q_ref[0, :, quad]) * HEAD_DIM ** -0.5).astype(BF16)
            dos = _stack_heads(do_ref[0, :, quad].astype(F32)).astype(BF16)
            ops.append((kd, qs, dos, lax.dot_general(qs, kd, nt, preferred_element_type=F32),
                        lax.dot_general(dos, vd, nt, preferred_element_type=F32)))
        for g in range(KV_HEADS):
            rows = slice(4 * BLOCK * g, 4 * BLOCK * (g + 1))
            _, _, _, s, dpn = ops[g]
            pn, psink = _attn_softmax(s + bias_ref[4 * g:4 * g + 4].reshape(4 * BLOCK, KEY_SPAN) + colneg, sink_ref[rows])
            r = jnp.sum(dpn * pn, axis=-1, keepdims=True)
            ds = pn * (dpn - r)
            dbias_ref[4 * g:4 * g + 4] += ds.reshape(4, BLOCK, KEY_SPAN)
            dsink_ref[rows] += -psink * r
            mids.append((pn.astype(BF16), ds.astype(BF16)))
        for g in range(KV_HEADS):
            pair = slice(LANE * (g // 2), LANE * (g // 2) + LANE)
            quad = slice(4 * HEAD_DIM * g, 4 * HEAD_DIM * (g + 1))
            kd, qs, dos, _, _ = ops[g]
            pnb, dsb = mids[g]
            dvd = lax.dot_general(pnb, dos, tn, preferred_element_type=F32)
            dkd = lax.dot_general(dsb, qs, tn, preferred_element_type=F32)
            dqs = lax.dot_general(dsb, kd, nn, preferred_element_type=F32) * HEAD_DIM ** -0.5
            dq_ref[0, :, quad] = _unstack_heads(dqs).astype(BF16)
            dk_g, dv_g = _fold_kv_head(dkd, g % 2), _fold_kv_head(dvd, g % 2)
            for o in range(3):
                dk_ref[0, krows[o], pair] += dk_g[o * BLOCK:(o + 1) * BLOCK]
                dv_ref[0, krows[o], pair] += dv_g[o * BLOCK:(o + 1) * BLOCK]

    qspec = BS((1, BLOCK, ATTN_HEADS * HEAD_DIM), lambda b, n: (b, n, 0))
    kvout = BS((1, L, 4 * HEAD_DIM), lambda b, n: (b, 0, 0))
    outs, carried = _pc_carry(
        body, (proj, proj, proj, proj, proj, proj, proj, bias, sinkcol, dout), name=name, grid=(B, nblk),
        in_specs=_attn_in_specs(L) + [BS((1, BLOCK, ATTN_HEADS * HEAD_DIM), lambda b, n: (b, n, 1))],
        out_specs=[qspec, kvout, kvout, BS((ATTN_HEADS, BLOCK, KEY_SPAN), lambda b, n: (0, 0, 0)),
                   BS((ATTN_HEADS * BLOCK, 1), lambda b, n: (0, 0))],
        out_shape=[SDS((B, L, ATTN_HEADS * HEAD_DIM), BF16), SDS((B, L, 4 * HEAD_DIM), F32), SDS((B, L, 4 * HEAD_DIM), F32),
                   SDS((ATTN_HEADS, BLOCK, KEY_SPAN), F32), SDS((ATTN_HEADS * BLOCK, 1), F32)], rider=rider)
    return (*outs, carried)


def _t5_bucket(rel):
    half = REL_BUCKETS // 2
    max_exact = half // 2
    ret = jnp.where(rel > 0, half, 0)
    n = jnp.abs(rel)
    nf = jnp.maximum(n, 1).astype(F32)
    large = max_exact + (jnp.log(nf / max_exact) / math.log(REL_MAX_DIST / max_exact) * (half - max_exact)).astype(jnp.int32)
    large = jnp.minimum(large, half - 1)
    return ret + jnp.where(n < max_exact, n, large)


def _bucket_table():
    rel = jnp.arange(KEY_SPAN)[None, :] - WINDOW - jnp.arange(BLOCK)[:, None]
    return _t5_bucket(rel).astype(jnp.int32)


def _bias_expand(rel_bias, bucket, *, name):
    rbt = jnp.zeros((ATTN_HEADS, 1, LANE), F32).at[:, 0, :REL_BUCKETS].set(rel_bias.T)

    def body(rb_ref, bk_ref, o_ref):
        lane = lax.broadcasted_iota(jnp.int32, (1, LANE), 1)
        row = rb_ref[0]
        bk = bk_ref[...]
        acc = jnp.zeros((BLOCK, KEY_SPAN), F32)
        for r in range(REL_BUCKETS):
            val = jnp.sum(jnp.where(lane == r, row, 0.0), axis=1, keepdims=True)
            acc = jnp.where(bk == r, val, acc)
        rel = (lax.broadcasted_iota(jnp.int32, (BLOCK, KEY_SPAN), 1) - WINDOW
               - lax.broadcasted_iota(jnp.int32, (BLOCK, KEY_SPAN), 0))
        o_ref[0] = jnp.where(jnp.abs(rel) <= WINDOW, acc, NEG)

    return _pc(body, name=name, grid=(ATTN_HEADS,),
               in_specs=[BS((1, 1, LANE), lambda h: (h, 0, 0)), BS((BLOCK, KEY_SPAN), lambda h: (0, 0))],
               out_specs=BS((1, BLOCK, KEY_SPAN), lambda h: (h, 0, 0)), out_shape=SDS((ATTN_HEADS, BLOCK, KEY_SPAN), F32))(rbt, bucket)


def _bias_reduce(dbias, bucket, *, name):
    def body(db_ref, bk_ref, o_ref):
        lane = lax.broadcasted_iota(jnp.int32, (1, LANE), 1)
        x = db_ref[0]
        bk = bk_ref[...]
        acc = jnp.zeros((1, LANE), F32)
        for r in range(REL_BUCKETS):
            part = jnp.sum(jnp.where(bk == r, x, 0.0), axis=1, keepdims=True)
            acc = jnp.where(lane == r, jnp.sum(part, axis=0, keepdims=True), acc)
        o_ref[0] = acc

    out = _pc(body, name=name, grid=(ATTN_HEADS,),
              in_specs=[BS((1, BLOCK, KEY_SPAN), lambda h: (h, 0, 0)), BS((BLOCK, KEY_SPAN), lambda h: (0, 0))],
              out_specs=BS((1, 1, LANE), lambda h: (h, 0, 0)), out_shape=SDS((ATTN_HEADS, 1, LANE), F32))(dbias, bucket)
    return out[:, 0, :REL_BUCKETS].T


def _loss_head(x2, w, target, *, name):
    T, D = x2.shape
    tr = _div_tile(T, 512, 8)

    def tile_loss(x, w, t):
        err = _rms(x, w) - t
        return 0.5 * jnp.sum(jnp.mean(err * err, axis=-1, keepdims=True), axis=0, keepdims=True)

    def body(x_ref, w_ref, t_ref, loss_ref, dx_ref, dxb_ref, dw_ref):
        t = t_ref[...]
        l, vjp = jax.vjp(lambda x, w: tile_loss(x, w, t), x_ref[...], w_ref[...])
        dx, dw = vjp(jnp.ones((1, 1), F32))
        dx_ref[...] = dx
        dxb_ref[...] = dx.astype(BF16)

        @pl.when(pl.program_id(0) == 0)
        def _():
            dw_ref[...] = jnp.zeros_like(dw_ref)
            loss_ref[...] = jnp.zeros_like(loss_ref)

        dw_ref[...] += dw
        loss_ref[...] += l + jnp.zeros((1, LANE), F32)

    row = BS((tr, D), lambda i: (i, 0))
    one = BS((1, D), lambda i: (0, 0))
    return _pc(body, name=name, grid=(T // tr,), in_specs=[row, one, row],
               out_specs=[BS((1, LANE), lambda i: (0, 0)), row, row, one],
               out_shape=[SDS((1, LANE), F32), SDS((T, D), F32), SDS((T, D), BF16), SDS((1, D), F32)])(x2, w.reshape(1, D), target)


def _adamw(w2, g2, m2, v2, *, name):
    R, C = w2.shape
    tr = _div_tile(R, 256, 8)
    c1 = 1.0 - ADAM_B1 ** ADAM_STEP
    c2 = 1.0 - ADAM_B2 ** ADAM_STEP

    def body(w_ref, g_ref, m_ref, v_ref, d_ref, nm_ref, nv_ref):
        g = g_ref[...]
        m = ADAM_B1 * m_ref[...] + (1.0 - ADAM_B1) * g
        v = ADAM_B2 * v_ref[...] + (1.0 - ADAM_B2) * (g * g)
        d_ref[...] = -ADAM_LR * ((m / c1) / (jnp.sqrt(v / c2) + ADAM_EPS) + ADAM_WD * w_ref[...])
        nm_ref[...] = m
        nv_ref[...] = v

    t = BS((tr, C), lambda i: (i, 0))
    return _pc(body, name=name, grid=(R // tr,), in_specs=[t, t, t, t], out_specs=[t, t, t],
               out_shape=[SDS((R, C), F32)] * 3)(w2, g2, m2, v2)


def _place():
    return lax.axis_index("x"), lax.axis_index("y"), lax.axis_index("c")


def _gather_rider(shards):
    na = len(shards)

    def copies(ins, outs, sems):
        send_sems, recv_sems = sems
        x, y, c = _place()
        for a in range(na):
            for k, peer in enumerate([(1 - x, y, c), (x, 1 - y, c), (1 - x, 1 - y, c), (x, y, 1 - c)]):
                send = functools.partial(pltpu.make_async_remote_copy, ins[a], outs[a].at[2 * x + y], send_sems.at[a, k],
                                         recv_sems.at[a, k], device_id=peer, device_id_type=MESH)
                got = outs[a].at[2 * peer[0] + peer[1]]
                arrived = functools.partial(pltpu.make_async_remote_copy, got, got, send_sems.at[a, k], recv_sems.at[a, k],
                                            device_id=peer, device_id_type=MESH)
                yield send, arrived

    def start(ins, outs, sems):
        for send, _ in copies(ins, outs, sems):
            send().start()

    def finish(ins, outs, sems):
        both = list(copies(ins, outs, sems))
        for _, arrived in both:
            arrived().wait_recv()
        for send, _ in both:
            send().wait_send()

    return dict(ins=list(shards), prev=[], out_shape=[SDS((N_CHIP,) + s.shape, s.dtype) for s in shards],
                scratch=[pltpu.SemaphoreType.DMA((na, 4)), pltpu.SemaphoreType.DMA((na, 4))], start=start, finish=finish)


def _scatter_rider(bufs, layer, prev):
    na = len(bufs)
    h = layer // (DEPTH // 2)

    def copies(ins, outs, sems):
        send_sems, recv_sems, local_sems = sems
        x, y, c = _place()
        me = 4 * x + 2 * y + c
        for a in range(na):
            for j in range(N_CHIP):
                is_self = ((2 * x + y) == j) & (c == h)
                local = functools.partial(pltpu.make_async_copy, ins[a].at[j], outs[a].at[me], local_sems.at[a])
                remote = functools.partial(pltpu.make_async_remote_copy, ins[a].at[j], outs[a].at[me], send_sems.at[a, j],
                                           recv_sems.at[a, me], device_id=(j // 2, j % 2, h), device_id_type=MESH)
                yield is_self, local, remote

    def start(ins, outs, sems):
        for is_self, local, remote in copies(ins, outs, sems):
            pl.when(is_self)(lambda: local().start())
            pl.when(jnp.logical_not(is_self))(lambda: remote().start())

    def finish(ins, outs, sems):
        _, recv_sems, _ = sems
        x, y, c = _place()
        me = 4 * x + 2 * y + c
        for a in range(na):
            for s in range(N_DEV):
                got = outs[a].at[s]
                arrived = functools.partial(pltpu.make_async_remote_copy, got, got, recv_sems.at[a, s], recv_sems.at[a, s],
                                            device_id=(s // 4, (s // 2) % 2, s % 2), device_id_type=MESH)
                pl.when((c == h) & (me != s))(lambda: arrived().wait_recv())
        for is_self, local, remote in copies(ins, outs, sems):
            pl.when(is_self)(lambda: local().wait())
            pl.when(jnp.logical_not(is_self))(lambda: remote().wait_send())

    return dict(ins=list(bufs), prev=list(prev), out_shape=[SDS((N_DEV,) + b.shape[1:], b.dtype) for b in bufs],
                scratch=[pltpu.SemaphoreType.DMA((na, N_CHIP)), pltpu.SemaphoreType.DMA((na, N_DEV)), pltpu.SemaphoreType.DMA((na,))],
                start=start, finish=finish)


def _sum_sources(parts, parity, into, *, name):
    _, R, C = parts.shape
    tr = _div_tile(R, 256, 16)

    def body(p_ref, *rest):
        acc = p_ref[0].astype(F32)
        for s in range(1, N_DEV):
            acc = acc + p_ref[s].astype(F32)
        rest[-1][0] = acc

    prev = [] if into is None else [into]
    return _pc(body, name=name, grid=(R // tr,),
               in_specs=[BS((N_DEV, tr, C), lambda i: (0, i, 0))] + [BS(memory_space=pl.ANY)] * len(prev),
               out_specs=BS((1, tr, C), lambda i: ((DEPTH // 2) * lax.axis_index("c") + parity, i, 0)),
               out_shape=SDS((DEPTH, R, C), F32), aliases={1: 0} if prev else None)(parts, *prev)


def _join_halves(fulls, *, name):
    na = len(fulls)
    half = DEPTH // 2

    def body(*refs):
        outs = refs[na:2 * na]
        send_sems, recv_sems = refs[2 * na:]
        x, y, c = _place()
        cps = []
        for a in range(na):
            mine = outs[a].at[pl.ds(c * half, half)]
            cp = pltpu.make_async_remote_copy(mine, mine, send_sems.at[a], recv_sems.at[a],
                                              device_id=(x, y, 1 - c), device_id_type=MESH)
            cp.start()
            cps.append(cp)
        for a in range(na):
            theirs = outs[a].at[pl.ds((1 - c) * half, half)]
            pltpu.make_async_remote_copy(theirs, theirs, send_sems.at[a], recv_sems.at[a],
                                         device_id=(x, y, 1 - c), device_id_type=MESH).wait_recv()
        for cp in cps:
            cp.wait_send()

    any_spec = BS(memory_space=pl.ANY)
    return pl.pallas_call(
        body, name=name, in_specs=[any_spec] * na, out_specs=[any_spec] * na,
        out_shape=[SDS(f.shape, f.dtype) for f in fulls], input_output_aliases={a: a for a in range(na)},
        scratch_shapes=[pltpu.SemaphoreType.DMA((na,)), pltpu.SemaphoreType.DMA((na,))],
        compiler_params=pltpu.CompilerParams(has_side_effects=True))(*fulls)


def _allreduce_small(vec, *, name):
    R = vec.shape[0]

    def body(v_ref, o_ref, all_ref, send_sems, recv_sems):
        x, y, c = _place()
        me = 4 * x + 2 * y + c
        all_ref[me] = v_ref[...]
        sends = []
        for r in range(1, N_DEV):
            tgt = (x ^ (r >> 2), y ^ ((r >> 1) & 1), c ^ (r & 1))
            cp = pltpu.make_async_remote_copy(v_ref, all_ref.at[me], send_sems.at[r - 1], recv_sems.at[r - 1],
                                              device_id=tgt, device_id_type=MESH)
            cp.start()
            sends.append(cp)
        for r in range(1, N_DEV):
            tx, ty, tc = x ^ (r >> 2), y ^ ((r >> 1) & 1), c ^ (r & 1)
            got = all_ref.at[4 * tx + 2 * ty + tc]
            pltpu.make_async_remote_copy(got, got, send_sems.at[r - 1], recv_sems.at[r - 1],
                                         device_id=(tx, ty, tc), device_id_type=MESH).wait_recv()
        for cp in sends:
            cp.wait_send()
        acc = all_ref[0]
        for s in range(1, N_DEV):
            acc = acc + all_ref[s]
        o_ref[...] = acc

    vm = BS(memory_space=pltpu.VMEM)
    return pl.pallas_call(
        body, name=name, in_specs=[vm], out_specs=vm, out_shape=SDS((R, LANE), F32),
        scratch_shapes=[pltpu.VMEM((N_DEV, R, LANE), F32), pltpu.SemaphoreType.DMA((N_DEV - 1,)), pltpu.SemaphoreType.DMA((N_DEV - 1,))],
        compiler_params=pltpu.CompilerParams(has_side_effects=True, vmem_limit_bytes=VMEM_LIMIT_BYTES))(vec)


def _pack(arrs):
    rows = []
    for a in arrs:
        f = a.reshape(-1).astype(F32)
        n = -(-f.shape[0] // LANE) * LANE
        rows.append(jnp.pad(f, (0, n - f.shape[0])).reshape(-1, LANE))
    v = jnp.concatenate(rows, axis=0)
    pad = -v.shape[0] % 8
    return jnp.pad(v, ((0, pad), (0, 0)))


def _unpack(v, shapes):
    out, r = [], 0
    for s in shapes:
        n = int(np.prod(s)) if len(s) else 1
        nr = -(-n // LANE)
        out.append(v[r:r + nr].reshape(-1)[:n].reshape(s))
        r += nr
    return out


def _perm_in_cols(w_full):
    z, xbc, dt, q, k, v = (w_full[..., :Z_END], w_full[..., Z_END:XBC_END], w_full[..., XBC_END:DT_END],
                           w_full[..., DT_END:Q_END], w_full[..., Q_END:K_END], w_full[..., K_END:])
    pad = jnp.zeros(dt.shape[:-1] + (LANE - dt.shape[-1],), dt.dtype)
    return jnp.concatenate([q, z, xbc, k, v, dt, pad], axis=-1)


def _unperm_in_cols(g):
    q, z, xbc, k, v, dt = (g[..., :1024], g[..., 1024:2048], g[..., 2048:3584], g[..., 3584:3840], g[..., 3840:4096],
                           g[..., 4096:4096 + 2 * SSM_HEADS])
    return jnp.concatenate([z, xbc, dt, q, k, v], axis=-1)


def _dt_cols(a):
    return jnp.pad(a.reshape(1, 2 * SSM_HEADS), ((0, 0), (0, LANE - 2 * SSM_HEADS)))


def _dt_fwd(proj, dtb, *, name):
    B, L, _ = proj.shape

    def body(p_ref, b_ref, o_ref):
        o_ref[0] = _softplus(p_ref[0] + b_ref[...])

    return _pc(body, name=name, grid=(B,),
               in_specs=[BS((1, L, LANE), lambda b: (b, 0, P_COLS // LANE - 1)), BS((1, LANE), lambda b: (0, 0))],
               out_specs=BS((1, L, LANE), lambda b: (b, 0, 0)), out_shape=SDS((B, L, LANE), F32))(proj, dtb)


def _dt_bwd(proj, dtb, ddt, *, name):
    B, L, _ = proj.shape

    def body(p_ref, b_ref, g_ref, o_ref, db_ref):
        g = g_ref[0] * _sigmoid(p_ref[0] + b_ref[...])
        o_ref[0] = g.astype(BF16)
        db_ref[0] = jnp.sum(g, axis=0, keepdims=True)

    row = BS((1, L, LANE), lambda b: (b, 0, 0))
    return _pc(body, name=name, grid=(B,),
               in_specs=[BS((1, L, LANE), lambda b: (b, 0, P_COLS // LANE - 1)), BS((1, LANE), lambda b: (0, 0)), row],
               out_specs=[row, BS((1, 1, LANE), lambda b: (b, 0, 0))],
               out_shape=[SDS((B, L, LANE), BF16), SDS((B, 1, LANE), F32)])(proj, dtb, ddt)


def _layer_fwd(i, x, wts, small, band_bias, riders=None, arrived=None):
    riders = riders or {}
    B, L, D = x.shape
    T = B * L
    x2 = x.reshape(T, D)
    h, proj2 = _norm_mm(x2, small["norm1_w"][i], wts["w_in"], name=f"in_proj_{i}", tn=1408)
    proj = proj2.reshape(B, L, P_COLS)
    xbc_act = _conv_fwd(proj, 2048 // 256, small["conv_w"][i], small["conv_b"][i], taps=SSM_CONV, ct=256,
                        out_dtype=F32, name=f"ssm_conv_{i}")
    dtb, alog = _dt_cols(small["dt_bias"][i]), _dt_cols(small["a_log"][i])
    dskip = jnp.repeat(small["d_skip"][i], HEAD_DIM).reshape(1, SSM_WIDTH)
    dtc = _dt_fwd(proj, dtb, name=f"dt_{i}")
    ypre, hs, carried = _ssd_fwd(xbc_act, dtc, alog, dskip, name=f"ssd_{i}", rider=riders.get("ssd"))
    if carried is not None:
        arrived("ssd", carried)
    mixed = _gate_fwd(ypre.reshape(T, SSM_WIDTH), proj2, small["ssm_norm_w"][i], name=f"gate_{i}")
    sinkcol = jnp.repeat(small["attn_sink"][i], BLOCK).reshape(ATTN_HEADS * BLOCK, 1)
    mixed, carried = _attn_fwd(proj, band_bias, sinkcol, mixed.reshape(B, L, 2 * D), name=f"attn_{i}", rider=riders.get("attn"))
    if carried is not None:
        arrived("attn", carried)
    mixed = mixed.reshape(T, 2 * D)
    x_mid = _mm(mixed, wts["w_out"], add=x2, name=f"out_proj_{i}")
    h2, gu2 = _norm_mm(x_mid, small["norm2_w"][i], wts["w_up"], name=f"up_proj_{i}", out_dtype=BF16, tn=1408)
    gu = gu2.reshape(B, L, 2 * D_FF)
    act = _conv_fwd(gu, 0, small["ffn_conv_w"][i], small["ffn_conv_b"][i], taps=FFN_CONV, ct=256, gate_blk0=D_FF // 256,
                    out_dtype=BF16, name=f"ffn_conv_{i}")
    x_out = _mm(act.reshape(T, D_FF), wts["w_down"], add=x_mid, name=f"down_proj_{i}", tk=1408)
    saved = dict(x2=x2, h=h, proj2=proj2, xbc_act=xbc_act, dtb=dtb, dtc=dtc, alog=alog, dskip=dskip, ypre=ypre, hs=hs, mixed=mixed,
                 sinkcol=sinkcol, x_mid=x_mid, h2=h2, gu=gu, act=act)
    return x_out.reshape(B, L, D), saved


def _layer_bwd(i, dx_out, dxb, sv, wts, small, band_bias, attn_rider=None, ssd_rider=None, arrived=None, wgrad_dtype=F32):
    T, D = dx_out.shape
    B, L = sv["gu"].shape[:2]
    g = {}
    dact = _mm(dxb, wts["w_down"], tb=True, out_dtype=BF16, name=f"d_act_{i}", tn=1408)
    g["w_down"] = _mm(sv["act"].reshape(T, D_FF), dxb, ta=True, out_dtype=wgrad_dtype, name=f"dw_down_{i}", tm=1408)
    dg, du, dcw, dcb = _conv_bwd(sv["gu"], 0, small["ffn_conv_w"][i], small["ffn_conv_b"][i], dact.reshape(B, L, D_FF),
                                 taps=FFN_CONV, ct=256, gate_blk0=D_FF // 256, name=f"d_ffn_conv_{i}")
    g["ffn_conv_w"] = jnp.sum(dcw, axis=0)[:FFN_CONV]
    g["ffn_conv_b"] = jnp.sum(dcb, axis=(0, 1))
    dg, du, w_up = dg.reshape(T, D_FF), du.reshape(T, D_FF), wts["w_up"]
    dh2 = _mm(dg, w_up[:, :D_FF], tb=True, name=f"d_h2_g_{i}", tk=1408)
    dh2 = _mm(du, w_up[:, D_FF:], tb=True, add=dh2, name=f"d_h2_u_{i}", tk=1408)
    dx_mid, dmb, dw2 = _dnorm(dh2, sv["x_mid"], small["norm2_w"][i], dx_out, name=f"d_norm2_{i}")
    g["w_up"] = jnp.concatenate([_mm(sv["h2"], dg, ta=True, out_dtype=wgrad_dtype, name=f"dw_up_g_{i}", tn=1408),
                                 _mm(sv["h2"], du, ta=True, out_dtype=wgrad_dtype, name=f"dw_up_u_{i}", tn=1408)], axis=1)
    g["norm2_w"] = dw2[0]
    dmixed = _mm(dmb, wts["w_out"], tb=True, out_dtype=BF16, name=f"d_mixed_{i}")
    g["w_out"] = _mm(sv["mixed"], dmb, ta=True, out_dtype=wgrad_dtype, name=f"dw_out_{i}")
    dypre, dz, dwn = _gate_bwd(sv["ypre"].reshape(T, SSM_WIDTH), sv["proj2"], small["ssm_norm_w"][i], dmixed, name=f"d_gate_{i}")
    g["ssm_norm_w"] = jnp.sum(dwn, axis=(0, 1))
    proj = sv["proj2"].reshape(B, L, P_COLS)
    dxs, dbm, dcm, ddt, dalog, ddsk, carried = _ssd_bwd(sv["xbc_act"], sv["dtc"], sv["alog"], sv["dskip"], sv["hs"],
                                                        dypre.reshape(B, L, SSM_WIDTH), name=f"d_ssd_{i}",
                                                        rider=ssd_rider(g) if ssd_rider is not None else None)
    if carried is not None:
        arrived("ssd", carried)
    ddt, ddtb = _dt_bwd(proj, sv["dtb"], ddt, name=f"d_dt_{i}")
    g["dt_bias"] = jnp.sum(ddtb, axis=(0, 1))[:2 * SSM_HEADS].reshape(2, SSM_HEADS)
    g["a_log"] = jnp.sum(dalog, axis=(0, 1, 2))[:2 * SSM_HEADS].reshape(2, SSM_HEADS)
    g["d_skip"] = jnp.sum(ddsk.reshape(B, SSM_HEADS, HEAD_DIM), axis=(0, 2))
    dxbc_act = dxs.at[:, :, SSM_WIDTH:SSM_WIDTH + BC_WIDTH].set(dbm).at[:, :, SSM_WIDTH + BC_WIDTH:].set(dcm)
    dxbc, dcw, dcb = _conv_bwd(proj, 2048 // 256, small["conv_w"][i], small["conv_b"][i], dxbc_act, taps=SSM_CONV, ct=256,
                               name=f"d_ssm_conv_{i}")
    g["conv_w"] = jnp.sum(dcw, axis=0)[:SSM_CONV]
    g["conv_b"] = jnp.sum(dcb, axis=(0, 1))
    dq, dk, dv, dbias, dsink, carried = _attn_bwd(proj, band_bias, sv["sinkcol"], dmixed.reshape(B, L, 2 * D), name=f"d_attn_{i}",
                                                  rider=attn_rider)
    if carried is not None:
        arrived("attn", carried)
    g["attn_sink"] = jnp.sum(dsink.reshape(ATTN_HEADS, BLOCK), axis=1)
    dproj = jnp.concatenate([dq, dz.reshape(B, L, SSM_WIDTH), dxbc, dk.astype(BF16), dv.astype(BF16), ddt], axis=-1).reshape(T, P_COLS)
    dh = _mm(dproj, wts["w_in"], tb=True, name=f"d_h_{i}", tk=1408)
    dx_in, dx_in_b, dw1 = _dnorm(dh, sv["x2"], small["norm1_w"][i], dx_mid, name=f"d_norm1_{i}")
    g["w_in"] = _unperm_in_cols(_mm(sv["h"], dproj, ta=True, out_dtype=wgrad_dtype, name=f"dw_in_{i}", tn=1408))
    g["norm1_w"] = dw1[0]
    return dx_in, dx_in_b, g, dbias


_BIG = ("w_in", "w_out", "w_up", "w_down")
_BIG_AXIS = {"w_in": 2, "w_out": 1, "w_up": 2, "w_down": 1}
_SMALL = ("rel_bias", "norm1_w", "conv_w", "conv_b", "dt_bias", "a_log", "d_skip", "ssm_norm_w", "attn_sink", "norm2_w",
          "ffn_conv_w", "ffn_conv_b", "final_norm_w")
_SMALL_SHARDED = ("conv_w", "ffn_conv_w")
_ORDER = ("rel_bias", "norm1_w", "w_in", "conv_w", "conv_b", "dt_bias", "a_log", "d_skip", "ssm_norm_w", "attn_sink", "w_out",
          "norm2_w", "w_up", "ffn_conv_w", "ffn_conv_b", "w_down", "final_norm_w")


def _local_step(x, target, small, wts=None, exchange=None):
    B, L, D = x.shape
    bucket = _bucket_table()
    band_bias = _bias_expand(small["rel_bias"], bucket, name="band_bias")

    def fetch(spec):
        return exchange["gather"](spec) if exchange is not None and spec else None

    def fetched(spec, carried):
        for (i, k), full in zip(spec, exchange["weights"](spec, carried)):
            wts[i][k] = full

    if exchange is not None:
        wts = [{} for _ in range(DEPTH)]
        fetched([(0, "w_in")], _run_rider(fetch([(0, "w_in")]), name="gather_w_in_0"))
    saved = []
    for i in range(DEPTH):
        nxt = i + 1 < DEPTH
        if i == 0:
            plan = {"ssd": [(0, "w_out"), (0, "w_up"), (0, "w_down")], "attn": [(1, "w_in"), (1, "w_out")] if nxt else []}
        else:
            plan = {"ssd": [(i, "w_down")] + ([(i + 1, "w_in"), (i + 1, "w_out")] if nxt else []), "attn": [(i, "w_up")]}
        x, sv = _layer_fwd(i, x, wts[i], small, band_bias, {c: fetch(s) for c, s in plan.items()}, lambda c, r: fetched(plan[c], r))
        saved.append(sv)
    loss, dx, dxb, dwf = _loss_head(x.reshape(B * L, D), small["final_norm_w"], target.reshape(B * L, D), name="loss_head")
    per_layer = []
    dbias = jnp.zeros((ATTN_HEADS, BLOCK, KEY_SPAN), F32)
    late = None
    for i in reversed(range(DEPTH)):
        own = [(i, "w_down"), (i, "w_up"), (i, "w_out")]
        plan = {"ssd": own, "attn": late[0] if late else []}
        attn_rider = exchange["scatter"](*late) if late else None
        ssd_rider = (lambda g: exchange["scatter"](own, [g[k] for _, k in own])) if exchange is not None else None
        dx, dxb, g, dbias_i = _layer_bwd(i, dx, dxb, saved[i], wts[i], small, band_bias, attn_rider, ssd_rider,
                                    lambda c, r: exchange["collect"](plan[c], r), F32 if exchange is None else BF16)
        if exchange is not None:
            late = ([(i, "w_in")], [g["w_in"]])
            for k in _BIG:
                g.pop(k)
        dbias = dbias + dbias_i
        per_layer.append(g)
    if exchange is not None:
        exchange["collect"](late[0], _run_rider(exchange["scatter"](*late), name="scatter_dw_in_0"))
    per_layer.reverse()
    grads = {k: jnp.stack([g[k] for g in per_layer]) for k in per_layer[0]}
    grads["rel_bias"] = _bias_reduce(dbias, bucket, name="d_rel_bias")
    grads["final_norm_w"] = dwf[0]
    return loss, dx.reshape(B, L, D), grads


def _split_by_chip(g, axis):
    shp = g.shape
    n = shp[axis] // N_CHIP
    g = g.reshape(shp[:axis] + (N_CHIP, n) + shp[axis + 1:])
    return jnp.moveaxis(g, axis, 0)


def _join_chips(a, axis):
    a = jnp.moveaxis(a, 0, axis)
    shp = a.shape
    return a.reshape(shp[:axis] + (shp[axis] * shp[axis + 1],) + shp[axis + 2:])


def kernel(x, rel_bias, norm1_w, w_in, conv_w, conv_b, dt_bias, a_log, d_skip, ssm_norm_w, attn_sink, w_out, norm2_w, w_up, ffn_conv_w, ffn_conv_b, w_down, final_norm_w, loss_target, m_rel_bias, m_norm1_w, m_w_in, m_conv_w, m_conv_b, m_dt_bias, m_a_log, m_d_skip, m_ssm_norm_w, m_attn_sink, m_w_out, m_norm2_w, m_w_up, m_ffn_conv_w, m_ffn_conv_b, m_w_down, m_final_norm_w, v_rel_bias, v_norm1_w, v_w_in, v_conv_w, v_conv_b, v_dt_bias, v_a_log, v_d_skip, v_ssm_norm_w, v_attn_sink, v_w_out, v_norm2_w, v_w_up, v_ffn_conv_w, v_ffn_conv_b, v_w_down, v_final_norm_w):
    w = dict(rel_bias=rel_bias, norm1_w=norm1_w, w_in=w_in, conv_w=conv_w, conv_b=conv_b, dt_bias=dt_bias, a_log=a_log,
             d_skip=d_skip, ssm_norm_w=ssm_norm_w, attn_sink=attn_sink, w_out=w_out, norm2_w=norm2_w, w_up=w_up,
             ffn_conv_w=ffn_conv_w, ffn_conv_b=ffn_conv_b, w_down=w_down, final_norm_w=final_norm_w)
    m = dict(rel_bias=m_rel_bias, norm1_w=m_norm1_w, w_in=m_w_in, conv_w=m_conv_w, conv_b=m_conv_b, dt_bias=m_dt_bias,
             a_log=m_a_log, d_skip=m_d_skip, ssm_norm_w=m_ssm_norm_w, attn_sink=m_attn_sink, w_out=m_w_out, norm2_w=m_norm2_w,
             w_up=m_w_up, ffn_conv_w=m_ffn_conv_w, ffn_conv_b=m_ffn_conv_b, w_down=m_w_down, final_norm_w=m_final_norm_w)
    v = dict(rel_bias=v_rel_bias, norm1_w=v_norm1_w, w_in=v_w_in, conv_w=v_conv_w, conv_b=v_conv_b, dt_bias=v_dt_bias,
             a_log=v_a_log, d_skip=v_d_skip, ssm_norm_w=v_ssm_norm_w, attn_sink=v_attn_sink, w_out=v_w_out, norm2_w=v_norm2_w,
             w_up=v_w_up, ffn_conv_w=v_ffn_conv_w, ffn_conv_b=v_ffn_conv_b, w_down=v_w_down, final_norm_w=v_final_norm_w)
    my_chip = 2 * lax.axis_index("x") + lax.axis_index("y")

    shards = {k: w[k].astype(BF16) for k in _BIG}
    received = {}

    def gather(spec):
        return _gather_rider([shards[k][i] for i, k in spec])

    def weights(spec, carried):
        out = []
        for (i, k), g_ in zip(spec, carried):
            full = _join_chips(g_, _BIG_AXIS[k] - 1)
            out.append(_perm_in_cols(full) if k == "w_in" else full)
        return out

    def scatter(spec, grads):
        layer = spec[0][0]
        bufs = [_split_by_chip(g_, _BIG_AXIS[k] - 1).astype(BF16) for (_, k), g_ in zip(spec, grads)]
        prev = [received[(layer % 2, k)] for _, k in spec] if layer + 2 < DEPTH else []
        return _scatter_rider(bufs, layer, prev)

    def collect(spec, carried):
        for (i, k), pieces in zip(spec, carried):
            received[(i % 2, k)] = pieces

    conv_shapes = [(DEPTH, SSM_CONV, CONV_CH), (DEPTH, FFN_CONV, D_FF)]
    placed = [lax.dynamic_update_slice_in_dim(jnp.zeros(s, F32), w[k], my_chip * w[k].shape[2], axis=2)
              for k, s in zip(_SMALL_SHARDED, conv_shapes)]
    lead = (lax.axis_index("c") == 0).astype(F32)
    conv_full = _unpack(_allreduce_small(_pack([p * lead for p in placed]), name="gather_conv_weights"), conv_shapes)
    small = {k: w[k] for k in _SMALL}
    small["conv_w"], small["ffn_conv_w"] = conv_full

    loss_part, grad_x, gp = _local_step(x, loss_target, small,
                                        exchange=dict(gather=gather, weights=weights, scatter=scatter, collect=collect))

    small_shapes = [small[k].shape for k in _SMALL] + [()]
    red = _unpack(_allreduce_small(_pack([gp[k] for k in _SMALL] + [loss_part[0, :1]]), name="reduce_small"), small_shapes)
    gsmall = dict(zip(_SMALL, red[:-1]))
    loss = red[-1]
    for k in _SMALL_SHARDED:
        n = w[k].shape[2]
        gsmall[k] = lax.dynamic_slice_in_dim(gsmall[k], my_chip * n, n, axis=2)

    fulls = []
    for k in _BIG:
        full = None
        for p in range(DEPTH // 2):
            full = _sum_sources(received[(p, k)], p, full, name=f"sum_{k}_{p}")
        fulls.append(full)
    gbig = dict(zip(_BIG, _join_halves(fulls, name="join_halves")))

    grad, delta, new_m, new_v = {}, {}, {}, {}
    for k in _BIG:
        shp = w[k].shape
        two = lambda a: a.reshape(shp[0] * shp[1], shp[2])
        d_, m_, v_ = _adamw(two(w[k]), two(gbig[k]), two(m[k]), two(v[k]), name=f"adamw_{k}")
        grad[k], delta[k], new_m[k], new_v[k] = gbig[k], d_.reshape(shp), m_.reshape(shp), v_.reshape(shp)
    shapes = [w[k].shape for k in _SMALL]
    d_, m_, v_ = _adamw(_pack([w[k] for k in _SMALL]), _pack([gsmall[k] for k in _SMALL]), _pack([m[k] for k in _SMALL]),
                        _pack([v[k] for k in _SMALL]), name="adamw_small")
    for k, a, b_, c_ in zip(_SMALL, _unpack(d_, shapes), _unpack(m_, shapes), _unpack(v_, shapes)):
        grad[k], delta[k], new_m[k], new_v[k] = gsmall[k], a, b_, c_
    return (loss, grad_x, *[grad[k] for k in _ORDER], *[delta[k] for k in _ORDER], *[new_m[k] for k in _ORDER],
            *[new_v[k] for k in _ORDER])
```

```python
import functools
import math

import jax
import jax.numpy as jnp
import numpy as np
from jax import lax
from jax.experimental import pallas as pl
from jax.experimental.pallas import tpu as pltpu

F32 = jnp.float32
BF16 = jnp.bfloat16
BS = pl.BlockSpec
SDS = jax.ShapeDtypeStruct
MESH = pl.DeviceIdType.MESH

D_MODEL = 1024
DEPTH = 4
SSM_HEADS = 16
SSM_WIDTH = 1024
BC_WIDTH = 256
CONV_CH = 1536
SSM_CONV = 7
CHUNK = 128
ATTN_HEADS = 16
KV_HEADS = 4
HEAD_DIM = 64
WINDOW = 128
BLOCK = 128
KEY_SPAN = 384
REL_BUCKETS = 32
REL_MAX_DIST = 128
D_FF = 2816
FFN_CONV = 3
NORM_EPS = 1e-6
Z_END = 1024
XBC_END = 2560
DT_END = 2592
Q_END = 3616
K_END = 3872
IN_COLS = 4128
P_COLS = 4224
ADAM_LR, ADAM_B1, ADAM_B2, ADAM_EPS, ADAM_WD, ADAM_STEP = 0.001, 0.9, 0.999, 1e-08, 0.01, 10
NEG = -1e30
N_DEV = 8
N_CHIP = 4
LANE = 128
VMEM_LIMIT_BYTES = 48 * 1024 * 1024


def _pc(body, *, name, grid, in_specs, out_specs, out_shape, scratch_shapes=(), aliases=None):
    return pl.pallas_call(
        body, name=name, grid=grid, in_specs=in_specs, out_specs=out_specs, out_shape=out_shape,
        scratch_shapes=list(scratch_shapes), input_output_aliases=aliases or {},
        compiler_params=pltpu.CompilerParams(dimension_semantics=("arbitrary",) * len(grid),
                                             vmem_limit_bytes=VMEM_LIMIT_BYTES))


def _split_rider_refs(refs, n_in, n_out, rider):
    n_rin = len(rider["ins"]) + len(rider["prev"])
    n_rout = len(rider["out_shape"])
    core = refs[:n_in] + refs[n_in + n_rin:n_in + n_rin + n_out] + refs[n_in + n_rin + n_out + n_rout + len(rider["scratch"]):]
    rins = refs[n_in:n_in + len(rider["ins"])]
    routs = refs[n_in + n_rin + n_out:n_in + n_rin + n_out + n_rout]
    sems = refs[n_in + n_rin + n_out + n_rout:n_in + n_rin + n_out + n_rout + len(rider["scratch"])]
    return core, rins, routs, sems


def _pc_carry(body, args, *, name, grid, in_specs, out_specs, out_shape, scratch_shapes=(), rider=None, aliases=None):
    if rider is None:
        return _pc(body, name=name, grid=grid, in_specs=in_specs, out_specs=out_specs, out_shape=out_shape,
                   scratch_shapes=scratch_shapes, aliases=aliases)(*args), None
    n_in, n_out = len(in_specs), len(out_shape)
    any_spec = BS(memory_space=pl.ANY)

    def full(*refs):
        core, rins, routs, sems = _split_rider_refs(refs, n_in, n_out, rider)
        ids = [pl.program_id(d) for d in range(len(grid))]
        first = functools.reduce(jnp.logical_and, [i == 0 for i in ids])
        last = functools.reduce(jnp.logical_and, [i == g - 1 for i, g in zip(ids, grid)])

        @pl.when(first)
        def _():
            rider["start"](rins, routs, sems)

        body(*core)

        @pl.when(last)
        def _():
            rider["finish"](rins, routs, sems)

    n_rin = len(rider["ins"])
    outs = pl.pallas_call(
        full, name=name, grid=grid,
        in_specs=list(in_specs) + [any_spec] * (n_rin + len(rider["prev"])),
        out_specs=list(out_specs) + [any_spec] * len(rider["out_shape"]),
        out_shape=list(out_shape) + list(rider["out_shape"]),
        scratch_shapes=list(rider["scratch"]) + list(scratch_shapes),
        input_output_aliases={**(aliases or {}), **{n_in + n_rin + t: n_out + t for t in range(len(rider["prev"]))}},
        compiler_params=pltpu.CompilerParams(dimension_semantics=("arbitrary",) * len(grid), vmem_limit_bytes=VMEM_LIMIT_BYTES,
                                             has_side_effects=True))(*args, *rider["ins"], *rider["prev"])
    return outs[:n_out], outs[n_out:]


def _run_rider(rider, *, name):
    any_spec = BS(memory_space=pl.ANY)
    n_rin = len(rider["ins"])

    def body(*refs):
        _, rins, routs, sems = _split_rider_refs(refs, 0, 0, rider)
        rider["start"](rins, routs, sems)
        rider["finish"](rins, routs, sems)

    return pl.pallas_call(
        body, name=name, in_specs=[any_spec] * (n_rin + len(rider["prev"])), out_specs=[any_spec] * len(rider["out_shape"]),
        out_shape=list(rider["out_shape"]), scratch_shapes=list(rider["scratch"]),
        input_output_aliases={n_rin + t: t for t in range(len(rider["prev"]))},
        compiler_params=pltpu.CompilerParams(has_side_effects=True))(*rider["ins"], *rider["prev"])


def _div_tile(n, pref, mult):
    t = min(pref, n)
    t -= t % mult
    while t >= mult:
        if n % t == 0:
            return t
        t -= mult
    return n


def _mm(a, b, *, name, ta=False, tb=False, add=None, out_dtype=F32, tm=1024, tn=1024, tk=1024):
    if ta:
        K, M = a.shape
    else:
        M, K = a.shape
    N = b.shape[0] if tb else b.shape[1]
    tm, tn, tk = _div_tile(M, tm, LANE), _div_tile(N, tn, LANE), _div_tile(K, tk, LANE)
    nk = K // tk
    dims = (((0,) if ta else (1,), (1,) if tb else (0,)), ((), ()))

    def body_single(*refs):
        r = lax.dot_general(refs[0][...], refs[1][...], dims, preferred_element_type=F32)
        if add is not None:
            r = r + refs[2][...]
        refs[-1][...] = r.astype(out_dtype)

    def body(*refs):
        if add is None:
            a_ref, b_ref, o_ref, acc_ref = refs
        else:
            a_ref, b_ref, add_ref, o_ref, acc_ref = refs
        k = pl.program_id(2)

        @pl.when(k == 0)
        def _():
            acc_ref[...] = jnp.zeros_like(acc_ref)

        acc_ref[...] += lax.dot_general(a_ref[...], b_ref[...], dims, preferred_element_type=F32)

        @pl.when(k == nk - 1)
        def _():
            r = acc_ref[...]
            if add is not None:
                r = r + add_ref[...]
            o_ref[...] = r.astype(out_dtype)

    a_spec = BS((tk, tm), lambda i, j, k: (k, i)) if ta else BS((tm, tk), lambda i, j, k: (i, k))
    b_spec = BS((tn, tk), lambda i, j, k: (j, k)) if tb else BS((tk, tn), lambda i, j, k: (k, j))
    in_specs, args = [a_spec, b_spec], [a, b]
    if add is not None:
        in_specs.append(BS((tm, tn), lambda i, j, k: (i, j)))
        args.append(add)
    return _pc(body_single if nk == 1 else body, name=name, grid=(M // tm, N // tn, nk), in_specs=in_specs,
               out_specs=BS((tm, tn), lambda i, j, k: (i, j)), out_shape=SDS((M, N), out_dtype),
               scratch_shapes=[] if nk == 1 else [pltpu.VMEM((tm, tn), F32)])(*args)


def _dot(a, b, dims):
    return lax.dot_general(a.astype(BF16), b.astype(BF16), (dims, ((), ())), preferred_element_type=F32)


@jax.custom_vjp
def _nn(a, b):
    return _dot(a, b, ((1,), (0,)))


@jax.custom_vjp
def _nt(a, b):
    return _dot(a, b, ((1,), (1,)))


@jax.custom_vjp
def _tn(a, b):
    return _dot(a, b, ((0,), (0,)))


_nn.defvjp(lambda a, b: (_nn(a, b), (a, b)), lambda r, g: (_nt(g, r[1]), _tn(r[0], g)))
_nt.defvjp(lambda a, b: (_nt(a, b), (a, b)), lambda r, g: (_nn(g, r[1]), _tn(g, r[0])))
_tn.defvjp(lambda a, b: (_tn(a, b), (a, b)), lambda r, g: (_nt(r[1], g), _nn(r[0], g)))


def _hdot(m, x):
    hi = x.astype(BF16)
    r1 = x - hi.astype(F32)
    lo = r1.astype(BF16)
    lo2 = (r1 - lo.astype(F32)).astype(BF16)
    n = x.shape[1]
    out = lax.dot_general(m.astype(BF16), jnp.concatenate([hi, lo, lo2], axis=1), (((1,), (0,)), ((), ())),
                          preferred_element_type=F32)
    return out[:, :n] + out[:, n:2 * n] + out[:, 2 * n:]


@jax.custom_vjp
def _cumdot(m, mt, x):
    return _hdot(m, x)


_cumdot.defvjp(lambda m, mt, x: (_hdot(m, x), (m, mt)),
               lambda r, g: (jnp.zeros_like(r[0]), jnp.zeros_like(r[1]), _hdot(r[1], g)))


def _sigmoid(x):
    return 1.0 / (1.0 + jnp.exp(-x))


def _softplus(x):
    return jnp.maximum(x, 0.0) + jnp.log(1.0 + jnp.exp(-jnp.abs(x)))


def _rms(x, w):
    return x * lax.rsqrt(jnp.mean(x * x, axis=-1, keepdims=True) + NORM_EPS) * w


def _norm_mm(x2, nw, b, *, name, out_dtype=F32, tm=1024, tn=1024):
    T, D = x2.shape
    N = b.shape[1]
    tm, tn = _div_tile(T, tm, LANE), _div_tile(N, tn, LANE)

    def body(x_ref, w_ref, b_ref, h_ref, o_ref):
        @pl.when(pl.program_id(1) == 0)
        def _():
            h_ref[...] = _rms(x_ref[...], w_ref[...]).astype(BF16)

        o_ref[...] = lax.dot_general(h_ref[...], b_ref[...], (((1,), (0,)), ((), ())), preferred_element_type=F32).astype(out_dtype)

    return _pc(body, name=name, grid=(T // tm, N // tn),
               in_specs=[BS((tm, D), lambda i, j: (i, 0)), BS((1, D), lambda i, j: (0, 0)), BS((D, tn), lambda i, j: (0, j))],
               out_specs=[BS((tm, D), lambda i, j: (i, 0)), BS((tm, tn), lambda i, j: (i, j))],
               out_shape=[SDS((T, D), BF16), SDS((T, N), out_dtype)])(x2, nw.reshape(1, D), b)


def _dnorm(dh, x2, nw, resid, *, name):
    T, D = x2.shape
    tr = _div_tile(T, 512, 16)

    def body(x_ref, w_ref, dh_ref, r_ref, dx_ref, dxb_ref, dw_ref):
        _, vjp = jax.vjp(_rms, x_ref[...], w_ref[...])
        dx, dw = vjp(dh_ref[...])
        dx = dx + r_ref[...]
        dx_ref[...] = dx
        dxb_ref[...] = dx.astype(BF16)

        @pl.when(pl.program_id(0) == 0)
        def _():
            dw_ref[...] = jnp.zeros_like(dw_ref)

        dw_ref[...] += dw

    row = BS((tr, D), lambda i: (i, 0))
    one = BS((1, D), lambda i: (0, 0))
    return _pc(body, name=name, grid=(T // tr,), in_specs=[row, one, row, row], out_specs=[row, row, one],
               out_shape=[SDS((T, D), F32), SDS((T, D), BF16), SDS((1, D), F32)])(x2, nw.reshape(1, D), dh, resid)


ROW_PAD = 8


def _pad_rows(x):
    return jnp.concatenate([x, jnp.zeros((ROW_PAD, x.shape[1]), x.dtype)], axis=0)


def _shift_rows(xp, s):
    n = xp.shape[0] - ROW_PAD
    return xp[:n] if s == 0 else pltpu.roll(xp, (-s) % xp.shape[0], 0)[:n]


def _conv_taps(x, taps):
    xp = _pad_rows(x)
    return [_shift_rows(xp, k - taps // 2) for k in range(taps)]


def _conv_pre(xs, w_ref, b_ref):
    c = b_ref[...] + w_ref[0:1, :] * xs[0]
    for k in range(1, len(xs)):
        c = c + w_ref[k:k + 1, :] * xs[k]
    return c


def _conv_fwd(x3, x_blk0, w, b, *, taps, ct, gate_blk0=None, out_dtype, name):
    B, L, _ = x3.shape
    C = w.shape[1]
    wp = jnp.zeros((8, C), F32).at[:taps].set(w)

    def body(*refs):
        if gate_blk0 is None:
            x_ref, w_ref, b_ref, o_ref = refs
        else:
            x_ref, u_ref, w_ref, b_ref, o_ref = refs
        c = _conv_pre(_conv_taps(x_ref[0].astype(F32), taps), w_ref, b_ref)
        y = c * _sigmoid(c)
        if gate_blk0 is not None:
            y = y * u_ref[0].astype(F32)
        o_ref[0] = y.astype(out_dtype)

    in_specs = [BS((1, L, ct), lambda bi, j: (bi, 0, x_blk0 + j))]
    args = [x3]
    if gate_blk0 is not None:
        in_specs.append(BS((1, L, ct), lambda bi, j: (bi, 0, gate_blk0 + j)))
        args.append(x3)
    in_specs += [BS((8, ct), lambda bi, j: (0, j)), BS((1, ct), lambda bi, j: (0, j))]
    args += [wp, b.reshape(1, C)]
    return _pc(body, name=name, grid=(B, C // ct), in_specs=in_specs,
               out_specs=BS((1, L, ct), lambda bi, j: (bi, 0, j)), out_shape=SDS((B, L, C), out_dtype))(*args)


def _conv_bwd(x3, x_blk0, w, b, dy3, *, taps, ct, gate_blk0=None, name):
    B, L, _ = x3.shape
    C = w.shape[1]
    wp = jnp.zeros((8, C), F32).at[:taps].set(w)
    gated = gate_blk0 is not None

    def body(*refs):
        if gated:
            x_ref, u_ref, w_ref, b_ref, dy_ref, dx_ref, du_ref, dw_ref, db_ref = refs
        else:
            x_ref, w_ref, b_ref, dy_ref, dx_ref, dw_ref, db_ref = refs
        xs = _conv_taps(x_ref[0].astype(F32), taps)
        dy = dy_ref[0].astype(F32)
        c = _conv_pre(xs, w_ref, b_ref)
        sg = _sigmoid(c)
        dsilu = sg * (1.0 + c * (1.0 - sg))
        if gated:
            du_ref[0] = (dy * (c * sg)).astype(BF16)
            dc = dy * u_ref[0].astype(F32) * dsilu
        else:
            dc = dy * dsilu
        dcp = _pad_rows(dc)
        dx = jnp.zeros_like(dc)
        dw_ref[0] = jnp.zeros((8, ct), F32)
        for k in range(taps):
            dx = dx + w_ref[k:k + 1, :] * _shift_rows(dcp, taps // 2 - k)
            dw_ref[0, k:k + 1, :] = jnp.sum(dc * xs[k], axis=0, keepdims=True)
        dx_ref[0] = dx.astype(BF16)
        db_ref[0] = jnp.sum(dc, axis=0, keepdims=True)

    xs = BS((1, L, ct), lambda bi, j: (bi, 0, x_blk0 + j))
    ys = BS((1, L, ct), lambda bi, j: (bi, 0, j))
    in_specs, args = [xs], [x3]
    if gated:
        in_specs.append(BS((1, L, ct), lambda bi, j: (bi, 0, gate_blk0 + j)))
        args.append(x3)
    in_specs += [BS((8, ct), lambda bi, j: (0, j)), BS((1, ct), lambda bi, j: (0, j)), ys]
    args += [wp, b.reshape(1, C), dy3]
    out_specs = [ys] + ([ys] if gated else []) + [BS((1, 8, ct), lambda bi, j: (bi, 0, j)), BS((1, 1, ct), lambda bi, j: (bi, 0, j))]
    out_shape = [SDS((B, L, C), BF16)] + ([SDS((B, L, C), BF16)] if gated else []) + [SDS((B, 8, C), F32), SDS((B, 1, C), F32)]
    return _pc(body, name=name, grid=(B, C // ct), in_specs=in_specs, out_specs=out_specs, out_shape=out_shape)(*args)


def _tri(reverse):
    r = lax.broadcasted_iota(jnp.int32, (CHUNK, CHUNK), 0)
    c = lax.broadcasted_iota(jnp.int32, (CHUNK, CHUNK), 1)
    return (c >= r) if reverse else (c <= r)


PAIRS = 2
QUADS = SSM_HEADS // (2 * PAIRS)
QW = PAIRS * LANE


def _ssd_chunk(h0, h1, x0, x1, bm, cm, dtc, alog, *, col0, reverse):
    mask = _tri(reverse)
    eye = lax.broadcasted_iota(jnp.int32, (CHUNK, CHUNK), 0) == lax.broadcasted_iota(jnp.int32, (CHUNK, CHUNK), 1)
    lane = lax.broadcasted_iota(jnp.int32, (1, LANE), 1)
    first = lane < HEAD_DIM
    adt = dtc * (-jnp.exp(alog))
    cumc = _cumdot(mask.astype(F32), _tri(not reverse).astype(F32), adt)
    totc = jnp.sum(adt, axis=0, keepdims=True)
    cb = _nt(cm, bm)

    def col(v, c):
        return jnp.sum(jnp.where(lane == c, v, 0.0), axis=1, keepdims=True)

    outs, states = [], []
    for p, (hprev, xs) in enumerate(((h0, x0), (h1, x1))):
        c0 = col0 + 2 * p
        cj = (col(cumc, c0), col(cumc, c0 + 1))
        cum = jnp.where(first, cj[0], cj[1])
        tot = jnp.where(first, col(totc, c0), col(totc, c0 + 1))
        xdt = xs * jnp.where(first, col(dtc, c0), col(dtc, c0 + 1))
        y = _nn(cm, hprev) * jnp.exp(cum)
        for j in range(2):
            rj = jnp.sum(jnp.where(eye, cj[j], 0.0), axis=0, keepdims=True)
            dec = jnp.exp(jnp.where(mask, cj[j] - rj, NEG))
            y = y + _nn(cb * dec, jnp.where(first if j == 0 else ~first, xdt, 0.0))
        outs.append(y)
        states.append(hprev * jnp.exp(tot) + _tn(bm, xdt * jnp.exp(tot - cum)))
    return outs[0], outs[1], states[0], states[1]


def _ssd_specs(B, L):
    def lanes(w, blk):
        return BS((1, L, w), blk)

    return [
        lanes(QW, lambda b, q: (b, 0, q)),
        lanes(LANE, lambda b, q: (b, 0, 8 + q // 2)),
        lanes(LANE, lambda b, q: (b, 0, 10 + q // 2)),
        lanes(LANE, lambda b, q: (b, 0, 0)),
        BS((1, LANE), lambda b, q: (0, 0)),
        BS((1, QW), lambda b, q: (0, q)),
    ]


def _ssd_slot(d, ci):
    return ci if d == 0 else ci + 1


def _ssd_fwd(xbc_act, dtc, alog, dskip, *, name, rider=None):
    B, L, _ = xbc_act.shape
    nc = L // CHUNK

    def body(xs_ref, b_ref, c_ref, dt_ref, alog_ref, dsk_ref, y_ref, hs_ref):
        q = pl.program_id(1)
        alog_v = alog_ref[...]
        y_ref[0] = dsk_ref[...] * xs_ref[0]
        hs_ref[0, 0, 0, 0] = jnp.zeros((LANE, QW), F32)
        hs_ref[0, 0, 1, nc] = jnp.zeros((LANE, QW), F32)

        def step(i, carry):
            cis = (i, nc - 1 - i)
            rows = [pl.ds(pl.multiple_of(ci * CHUNK, CHUNK), CHUNK) for ci in cis]
            res = []
            for d in range(2):
                cur = _ssd_slot(d, cis[d])
                res.append(_ssd_chunk(
                    hs_ref[0, 0, d, cur, :, :LANE], hs_ref[0, 0, d, cur, :, LANE:], xs_ref[0, rows[d], :LANE],
                    xs_ref[0, rows[d], LANE:], b_ref[0, rows[d], :], c_ref[0, rows[d], :], dt_ref[0, rows[d], :], alog_v,
                    col0=SSM_HEADS * d + 2 * PAIRS * q, reverse=d == 1))
            for d in range(2):
                y0, y1, n0, n1 = res[d]
                nxt = _ssd_slot(d, cis[d] + 1 if d == 0 else cis[d] - 1)
                hs_ref[0, 0, d, nxt, :, :LANE] = n0
                hs_ref[0, 0, d, nxt, :, LANE:] = n1
                y_ref[0, rows[d], :LANE] += y0
                y_ref[0, rows[d], LANE:] += y1
            return carry

        lax.fori_loop(0, nc, step, 0, unroll=2)

    (y, hs), carried = _pc_carry(
        body, (xbc_act, xbc_act, xbc_act, dtc, alog, dskip), name=name, grid=(B, QUADS), in_specs=_ssd_specs(B, L),
        out_specs=[BS((1, L, QW), lambda b, q: (b, 0, q)), BS((1, 1, 2, nc + 1, LANE, QW), lambda b, q: (b, q, 0, 0, 0, 0))],
        out_shape=[SDS((B, L, SSM_WIDTH), F32), SDS((B, QUADS, 2, nc + 1, LANE, QW), F32)], rider=rider)
    return y, hs, carried


def _ssd_bwd(xbc_act, dtc, alog, dskip, hs, dy, *, name, rider=None):
    B, L, _ = xbc_act.shape
    nc = L // CHUNK

    def body(xs_ref, b_ref, c_ref, dt_ref, alog_ref, dsk_ref, hs_ref, dy_ref,
             dxs_ref, db_ref, dc_ref, ddt_ref, dalog_ref, ddsk_ref, dh_ref):
        q = pl.program_id(1)
        alog_v = alog_ref[...]

        @pl.when(q % 2 == 0)
        def _():
            db_ref[...] = jnp.zeros_like(db_ref)
            dc_ref[...] = jnp.zeros_like(dc_ref)

        @pl.when(q == 0)
        def _():
            ddt_ref[...] = jnp.zeros_like(ddt_ref)

        dxs_ref[0] = dy_ref[0] * dsk_ref[...]
        ddsk_ref[0] = jnp.sum(dy_ref[0] * xs_ref[0], axis=0, keepdims=True)
        dh_ref[...] = jnp.zeros_like(dh_ref)

        def step(i, carry):
            g_alog = carry
            cis = (nc - 1 - i, i)
            rows = [pl.ds(pl.multiple_of(ci * CHUNK, CHUNK), CHUNK) for ci in cis]
            res = []
            for d in range(2):
                cur = _ssd_slot(d, cis[d])
                fn = functools.partial(_ssd_chunk, col0=SSM_HEADS * d + 2 * PAIRS * q, reverse=d == 1)
                _, vjp = jax.vjp(fn, hs_ref[0, 0, d, cur, :, :LANE], hs_ref[0, 0, d, cur, :, LANE:], xs_ref[0, rows[d], :LANE],
                                 xs_ref[0, rows[d], LANE:], b_ref[0, rows[d], :], c_ref[0, rows[d], :], dt_ref[0, rows[d], :],
                                 alog_v)
                res.append(vjp((dy_ref[0, rows[d], :LANE], dy_ref[0, rows[d], LANE:], dh_ref[d, :, :LANE], dh_ref[d, :, LANE:])))
            for d in range(2):
                g_h0, g_h1, g_x0, g_x1, g_b, g_c, g_dt, g_alog1 = res[d]
                dh_ref[d, :, :LANE] = g_h0
                dh_ref[d, :, LANE:] = g_h1
                dxs_ref[0, rows[d], :LANE] += g_x0
                dxs_ref[0, rows[d], LANE:] += g_x1
                db_ref[0, rows[d], :] += g_b
                dc_ref[0, rows[d], :] += g_c
                ddt_ref[0, rows[d], :] += g_dt
                g_alog = g_alog + g_alog1
            return g_alog

        dalog_ref[0, 0] = lax.fori_loop(0, nc, step, jnp.zeros((1, LANE), F32))

    lanes = lambda w, blk: BS((1, L, w), blk)
    in_specs = _ssd_specs(B, L) + [BS((1, 1, 2, nc + 1, LANE, QW), lambda b, q: (b, q, 0, 0, 0, 0)), lanes(QW, lambda b, q: (b, 0, q))]
    out_specs = [lanes(QW, lambda b, q: (b, 0, q)), lanes(LANE, lambda b, q: (b, 0, q // 2)), lanes(LANE, lambda b, q: (b, 0, q // 2)),
                 lanes(LANE, lambda b, q: (b, 0, 0)), BS((1, 1, 1, LANE), lambda b, q: (b, q, 0, 0)),
                 BS((1, 1, QW), lambda b, q: (b, 0, q))]
    out_shape = [SDS((B, L, CONV_CH), F32), SDS((B, L, BC_WIDTH), F32), SDS((B, L, BC_WIDTH), F32), SDS((B, L, LANE), F32),
                 SDS((B, QUADS, 1, LANE), F32), SDS((B, 1, SSM_WIDTH), F32)]
    outs, carried = _pc_carry(body, (xbc_act, xbc_act, xbc_act, dtc, alog, dskip, hs, dy), name=name, grid=(B, QUADS),
                              in_specs=in_specs, out_specs=out_specs, out_shape=out_shape,
                              scratch_shapes=[pltpu.VMEM((2, LANE, QW), F32)], rider=rider)
    return (*outs, carried)


def _gate_norm(yp, z, w):
    v = yp * (z * _sigmoid(z))
    return v * lax.rsqrt(jnp.mean(v * v, axis=-1, keepdims=True) + NORM_EPS) * w


def _gate_fwd(ypre2, proj2, w, *, name):
    T = ypre2.shape[0]
    tr = _div_tile(T, 512, 8)
    G = 512

    def body(y_ref, z_ref, w_ref, o_ref):
        o_ref[...] = _gate_norm(y_ref[...], z_ref[...], w_ref[...]).astype(BF16)

    return _pc(body, name=name, grid=(T // tr, 2),
               in_specs=[BS((tr, G), lambda i, g: (i, g)), BS((tr, G), lambda i, g: (i, 2 + g)), BS((1, G), lambda i, g: (0, g))],
               out_specs=BS((tr, G), lambda i, g: (i, g)), out_shape=SDS((T, 2 * SSM_WIDTH), BF16))(ypre2, proj2, w.reshape(1, -1))


def _gate_bwd(ypre2, proj2, w, dy, *, name):
    T = ypre2.shape[0]
    tr = _div_tile(T, 512, 8)
    G = 512

    def body(y_ref, z_ref, w_ref, dy_ref, dyp_ref, dz_ref, dw_ref):
        _, vjp = jax.vjp(_gate_norm, y_ref[...], z_ref[...], w_ref[...])
        dyp, dz, dw = vjp(dy_ref[...].astype(F32))
        dyp_ref[...] = dyp
        dz_ref[...] = dz.astype(BF16)
        dw_ref[0] = dw

    tile = BS((tr, G), lambda i, g: (i, g))
    return _pc(body, name=name, grid=(T // tr, 2),
               in_specs=[tile, BS((tr, G), lambda i, g: (i, 2 + g)), BS((1, G), lambda i, g: (0, g)), tile],
               out_specs=[tile, tile, BS((1, 1, G), lambda i, g: (i, 0, g))],
               out_shape=[SDS((T, SSM_WIDTH), F32), SDS((T, SSM_WIDTH), BF16), SDS((T // tr, 1, SSM_WIDTH), F32)])(
        ypre2, proj2, w.reshape(1, -1), dy)


def _first_half():
    return lax.broadcasted_iota(jnp.int32, (1, LANE), 1) < HEAD_DIM


def _dup_kv_head(pair, odd):
    rolled = pltpu.roll(pair, HEAD_DIM, 1)
    return jnp.where(_first_half(), rolled, pair) if odd else jnp.where(_first_half(), pair, rolled)


def _stack_heads(quad):
    first = _first_half()
    lo, hi = quad[:, :LANE], quad[:, LANE:]
    return jnp.concatenate([jnp.where(first, lo, 0.0), jnp.where(first, 0.0, lo), jnp.where(first, hi, 0.0),
                            jnp.where(first, 0.0, hi)], axis=0)


def _unstack_heads(o):
    first = _first_half()
    return jnp.concatenate([jnp.where(first, o[:BLOCK], o[BLOCK:2 * BLOCK]), jnp.where(first, o[2 * BLOCK:3 * BLOCK], o[3 * BLOCK:])], axis=1)


def _fold_kv_head(d, odd):
    tot = d + pltpu.roll(d, HEAD_DIM, 1)
    return jnp.where(_first_half(), 0.0, tot) if odd else jnp.where(_first_half(), tot, 0.0)


def _attn_softmax(s, sink):
    m = jnp.maximum(jnp.max(s, axis=-1, keepdims=True), sink)
    p = jnp.exp(s - m)
    ps = jnp.exp(sink - m)
    inv = 1.0 / (jnp.sum(p, axis=-1, keepdims=True) + ps)
    return p * inv, ps * inv


def _attn_colneg(n, L):
    kpos = n * BLOCK - WINDOW + lax.broadcasted_iota(jnp.int32, (1, KEY_SPAN), 1)
    return jnp.where((kpos >= 0) & (kpos < L), 0.0, NEG)


def _attn_in_specs(L):
    nblk = L // BLOCK
    kv = lambda o, col: BS((1, BLOCK, 4 * HEAD_DIM), lambda b, n: (b, jnp.clip(n + o, 0, nblk - 1), col))
    kcol, vcol = 3584 // 256, 3840 // 256
    return [BS((1, BLOCK, ATTN_HEADS * HEAD_DIM), lambda b, n: (b, n, 0)), kv(-1, kcol), kv(0, kcol), kv(1, kcol),
            kv(-1, vcol), kv(0, vcol), kv(1, vcol),
            BS((ATTN_HEADS, BLOCK, KEY_SPAN), lambda b, n: (0, 0, 0)), BS((ATTN_HEADS * BLOCK, 1), lambda b, n: (0, 0))]


def _attn_fwd(proj, bias, sinkcol, mixed, *, name, rider=None):
    B, L, _ = proj.shape

    def body(q_ref, k0, k1, k2, v0, v1, v2, bias_ref, sink_ref, _, o_ref):
        colneg = _attn_colneg(pl.program_id(1), L)
        kcat = jnp.concatenate([k0[0], k1[0], k2[0]], axis=0)
        vcat = jnp.concatenate([v0[0], v1[0], v2[0]], axis=0)
        scores, probs, scales = [], [], []
        for g in range(KV_HEADS):
            pair = slice(LANE * (g // 2), LANE * (g // 2) + LANE)
            quad = slice(4 * HEAD_DIM * g, 4 * HEAD_DIM * (g + 1))
            kd = _dup_kv_head(kcat[:, pair], g % 2).astype(BF16)
            qs = (_stack_heads(q_ref[0, :, quad]) * HEAD_DIM ** -0.5).astype(BF16)
            scores.append(lax.dot_general(qs, kd, (((1,), (1,)), ((), ())), preferred_element_type=F32))
        for g in range(KV_HEADS):
            s = scores[g] + bias_ref[4 * g:4 * g + 4].reshape(4 * BLOCK, KEY_SPAN) + colneg
            sink = sink_ref[4 * BLOCK * g:4 * BLOCK * (g + 1)]
            m = jnp.maximum(jnp.max(s, axis=-1, keepdims=True), sink)
            p = jnp.exp(s - m)
            scales.append(1.0 / (jnp.sum(p, axis=-1, keepdims=True) + jnp.exp(sink - m)))
            probs.append(p.astype(BF16))
        for g in range(KV_HEADS):
            pair = slice(LANE * (g // 2), LANE * (g // 2) + LANE)
            quad = slice(4 * HEAD_DIM * g, 4 * HEAD_DIM * (g + 1))
            vd = _dup_kv_head(vcat[:, pair], g % 2).astype(BF16)
            o = lax.dot_general(probs[g], vd, (((1,), (0,)), ((), ())), preferred_element_type=F32) * scales[g]
            o_ref[0, :, quad] = _unstack_heads(o).astype(BF16)

    (out,), carried = _pc_carry(body, (proj, proj, proj, proj, proj, proj, proj, bias, sinkcol, mixed), name=name,
                                grid=(B, L // BLOCK), in_specs=_attn_in_specs(L) + [BS(memory_space=pl.ANY)],
                                out_specs=[BS((1, BLOCK, ATTN_HEADS * HEAD_DIM), lambda b, n: (b, n, 1))],
                                out_shape=[SDS(mixed.shape, BF16)], rider=rider, aliases={9: 0})
    return out, carried


def _attn_bwd(proj, bias, sinkcol, dout, *, name, rider=None):
    B, L, _ = proj.shape
    nblk = L // BLOCK
    nn, nt, tn = (((1,), (0,)), ((), ())), (((1,), (1,)), ((), ())), (((0,), (0,)), ((), ()))

    def body(q_ref, k0, k1, k2, v0, v1, v2, bias_ref, sink_ref, do_ref, dq_ref, dk_ref, dv_ref, dbias_ref, dsink_ref):
        b, n = pl.program_id(0), pl.program_id(1)

        @pl.when(n == 0)
        def _():
            dk_ref[...] = jnp.zeros_like(dk_ref)
            dv_ref[...] = jnp.zeros_like(dv_ref)

        @pl.when((n == 0) & (b == 0))
        def _():
            dbias_ref[...] = jnp.zeros_like(dbias_ref)
            dsink_ref[...] = jnp.zeros_like(dsink_ref)

        colneg = _attn_colneg(n, L)
        kcat = jnp.concatenate([k0[0], k1[0], k2[0]], axis=0)
        vcat = jnp.concatenate([v0[0], v1[0], v2[0]], axis=0)
        krows = [pl.ds(pl.multiple_of(jnp.clip(n + o, 0, nblk - 1) * BLOCK, BLOCK), BLOCK) for o in (-1, 0, 1)]
        ops, mids = [], []
        for g in range(KV_HEADS):
            pair = slice(LANE * (g // 2), LANE * (g // 2) + LANE)
            quad = slice(4 * HEAD_DIM * g, 4 * HEAD_DIM * (g + 1))
            kd = _dup_kv_head(kcat[:, pair], g % 2).astype(BF16)
            vd = _dup_kv_head(vcat[:, pair], g % 2).astype(BF16)
            qs = (_stack_heads(q_ref[0, :, quad]) * HEAD_DIM ** -0.5).astype(BF16)
            dos = _stack_heads(do_ref[0, :, quad].astype(F32)).astype(BF16)
            ops.append((kd, qs, dos, lax.dot_general(qs, kd, nt, preferred_element_type=F32),
                        lax.dot_general(dos, vd, nt, preferred_element_type=F32)))
        for g in range(KV_HEADS):
            rows = slice(4 * BLOCK * g, 4 * BLOCK * (g + 1))
            _, _, _, s, dpn = ops[g]
            pn, psink = _attn_softmax(s + bias_ref[4 * g:4 * g + 4].reshape(4 * BLOCK, KEY_SPAN) + colneg, sink_ref[rows])
            r = jnp.sum(dpn * pn, axis=-1, keepdims=True)
            ds = pn * (dpn - r)
            dbias_ref[4 * g:4 * g + 4] += ds.reshape(4, BLOCK, KEY_SPAN)
            dsink_ref[rows] += -psink * r
            mids.append((pn.astype(BF16), ds.astype(BF16)))
        for g in range(KV_HEADS):
            pair = slice(LANE * (g // 2), LANE * (g // 2) + LANE)
            quad = slice(4 * HEAD_DIM * g, 4 * HEAD_DIM * (g + 1))
            kd, qs, dos, _, _ = ops[g]
            pnb, dsb = mids[g]
            dvd = lax.dot_general(pnb, dos, tn, preferred_element_type=F32)
            dkd = lax.dot_general(dsb, qs, tn, preferred_element_type=F32)
            dqs = lax.dot_general(dsb, kd, nn, preferred_element_type=F32) * HEAD_DIM ** -0.5
            dq_ref[0, :, quad] = _unstack_heads(dqs).astype(BF16)
            dk_g, dv_g = _fold_kv_head(dkd, g % 2), _fold_kv_head(dvd, g % 2)
            for o in range(3):
                dk_ref[0, krows[o], pair] += dk_g[o * BLOCK:(o + 1) * BLOCK]
                dv_ref[0, krows[o], pair] += dv_g[o * BLOCK:(o + 1) * BLOCK]

    qspec = BS((1, BLOCK, ATTN_HEADS * HEAD_DIM), lambda b, n: (b, n, 0))
    kvout = BS((1, L, 4 * HEAD_DIM), lambda b, n: (b, 0, 0))
    outs, carried = _pc_carry(
        body, (proj, proj, proj, proj, proj, proj, proj, bias, sinkcol, dout), name=name, grid=(B, nblk),
        in_specs=_attn_in_specs(L) + [BS((1, BLOCK, ATTN_HEADS * HEAD_DIM), lambda b, n: (b, n, 1))],
        out_specs=[qspec, kvout, kvout, BS((ATTN_HEADS, BLOCK, KEY_SPAN), lambda b, n: (0, 0, 0)),
                   BS((ATTN_HEADS * BLOCK, 1), lambda b, n: (0, 0))],
        out_shape=[SDS((B, L, ATTN_HEADS * HEAD_DIM), BF16), SDS((B, L, 4 * HEAD_DIM), F32), SDS((B, L, 4 * HEAD_DIM), F32),
                   SDS((ATTN_HEADS, BLOCK, KEY_SPAN), F32), SDS((ATTN_HEADS * BLOCK, 1), F32)], rider=rider)
    return (*outs, carried)


def _t5_bucket(rel):
    half = REL_BUCKETS // 2
    max_exact = half // 2
    ret = jnp.where(rel > 0, half, 0)
    n = jnp.abs(rel)
    nf = jnp.maximum(n, 1).astype(F32)
    large = max_exact + (jnp.log(nf / max_exact) / math.log(REL_MAX_DIST / max_exact) * (half - max_exact)).astype(jnp.int32)
    large = jnp.minimum(large, half - 1)
    return ret + jnp.where(n < max_exact, n, large)


def _bucket_table():
    rel = jnp.arange(KEY_SPAN)[None, :] - WINDOW - jnp.arange(BLOCK)[:, None]
    return _t5_bucket(rel).astype(jnp.int32)


def _bias_expand(rel_bias, bucket, *, name):
    rbt = jnp.zeros((ATTN_HEADS, 1, LANE), F32).at[:, 0, :REL_BUCKETS].set(rel_bias.T)

    def body(rb_ref, bk_ref, o_ref):
        lane = lax.broadcasted_iota(jnp.int32, (1, LANE), 1)
        row = rb_ref[0]
        bk = bk_ref[...]
        acc = jnp.zeros((BLOCK, KEY_SPAN), F32)
        for r in range(REL_BUCKETS):
            val = jnp.sum(jnp.where(lane == r, row, 0.0), axis=1, keepdims=True)
            acc = jnp.where(bk == r, val, acc)
        rel = (lax.broadcasted_iota(jnp.int32, (BLOCK, KEY_SPAN), 1) - WINDOW
               - lax.broadcasted_iota(jnp.int32, (BLOCK, KEY_SPAN), 0))
        o_ref[0] = jnp.where(jnp.abs(rel) <= WINDOW, acc, NEG)

    return _pc(body, name=name, grid=(ATTN_HEADS,),
               in_specs=[BS((1, 1, LANE), lambda h: (h, 0, 0)), BS((BLOCK, KEY_SPAN), lambda h: (0, 0))],
               out_specs=BS((1, BLOCK, KEY_SPAN), lambda h: (h, 0, 0)), out_shape=SDS((ATTN_HEADS, BLOCK, KEY_SPAN), F32))(rbt, bucket)


def _bias_reduce(dbias, bucket, *, name):
    def body(db_ref, bk_ref, o_ref):
        lane = lax.broadcasted_iota(jnp.int32, (1, LANE), 1)
        x = db_ref[0]
        bk = bk_ref[...]
        acc = jnp.zeros((1, LANE), F32)
        for r in range(REL_BUCKETS):
            part = jnp.sum(jnp.where(bk == r, x, 0.0), axis=1, keepdims=True)
            acc = jnp.where(lane == r, jnp.sum(part, axis=0, keepdims=True), acc)
        o_ref[0] = acc

    out = _pc(body, name=name, grid=(ATTN_HEADS,),
              in_specs=[BS((1, BLOCK, KEY_SPAN), lambda h: (h, 0, 0)), BS((BLOCK, KEY_SPAN), lambda h: (0, 0))],
              out_specs=BS((1, 1, LANE), lambda h: (h, 0, 0)), out_shape=SDS((ATTN_HEADS, 1, LANE), F32))(dbias, bucket)
    return out[:, 0, :REL_BUCKETS].T


def _loss_head(x2, w, target, *, name):
    T, D = x2.shape
    tr = _div_tile(T, 512, 8)

    def tile_loss(x, w, t):
        err = _rms(x, w) - t
        return 0.5 * jnp.sum(jnp.mean(err * err, axis=-1, keepdims=True), axis=0, keepdims=True)

    def body(x_ref, w_ref, t_ref, loss_ref, dx_ref, dxb_ref, dw_ref):
        t = t_ref[...]
        l, vjp = jax.vjp(lambda x, w: tile_loss(x, w, t), x_ref[...], w_ref[...])
        dx, dw = vjp(jnp.ones((1, 1), F32))
        dx_ref[...] = dx
        dxb_ref[...] = dx.astype(BF16)

        @pl.when(pl.program_id(0) == 0)
        def _():
            dw_ref[...] = jnp.zeros_like(dw_ref)
            loss_ref[...] = jnp.zeros_like(loss_ref)

        dw_ref[...] += dw
        loss_ref[...] += l + jnp.zeros((1, LANE), F32)

    row = BS((tr, D), lambda i: (i, 0))
    one = BS((1, D), lambda i: (0, 0))
    return _pc(body, name=name, grid=(T // tr,), in_specs=[row, one, row],
               out_specs=[BS((1, LANE), lambda i: (0, 0)), row, row, one],
               out_shape=[SDS((1, LANE), F32), SDS((T, D), F32), SDS((T, D), BF16), SDS((1, D), F32)])(x2, w.reshape(1, D), target)


def _adamw(w2, g2, m2, v2, *, name):
    R, C = w2.shape
    tr = _div_tile(R, 256, 8)
    c1 = 1.0 - ADAM_B1 ** ADAM_STEP
    c2 = 1.0 - ADAM_B2 ** ADAM_STEP

    def body(w_ref, g_ref, m_ref, v_ref, d_ref, nm_ref, nv_ref):
        g = g_ref[...]
        m = ADAM_B1 * m_ref[...] + (1.0 - ADAM_B1) * g
        v = ADAM_B2 * v_ref[...] + (1.0 - ADAM_B2) * (g * g)
        d_ref[...] = -ADAM_LR * ((m / c1) / (jnp.sqrt(v / c2) + ADAM_EPS) + ADAM_WD * w_ref[...])
        nm_ref[...] = m
        nv_ref[...] = v

    t = BS((tr, C), lambda i: (i, 0))
    return _pc(body, name=name, grid=(R // tr,), in_specs=[t, t, t, t], out_specs=[t, t, t],
               out_shape=[SDS((R, C), F32)] * 3)(w2, g2, m2, v2)


def _place():
    return lax.axis_index("x"), lax.axis_index("y"), lax.axis_index("c")


def _gather_rider(shards):
    na = len(shards)

    def copies(ins, outs, sems):
        send_sems, recv_sems = sems
        x, y, c = _place()
        for a in range(na):
            for k, peer in enumerate([(1 - x, y, c), (x, 1 - y, c), (1 - x, 1 - y, c), (x, y, 1 - c)]):
                send = functools.partial(pltpu.make_async_remote_copy, ins[a], outs[a].at[2 * x + y], send_sems.at[a, k],
                                         recv_sems.at[a, k], device_id=peer, device_id_type=MESH)
                got = outs[a].at[2 * peer[0] + peer[1]]
                arrived = functools.partial(pltpu.make_async_remote_copy, got, got, send_sems.at[a, k], recv_sems.at[a, k],
                                            device_id=peer, device_id_type=MESH)
                yield send, arrived

    def start(ins, outs, sems):
        for send, _ in copies(ins, outs, sems):
            send().start()

    def finish(ins, outs, sems):
        both = list(copies(ins, outs, sems))
        for _, arrived in both:
            arrived().wait_recv()
        for send, _ in both:
            send().wait_send()

    return dict(ins=list(shards), prev=[], out_shape=[SDS((N_CHIP,) + s.shape, s.dtype) for s in shards],
                scratch=[pltpu.SemaphoreType.DMA((na, 4)), pltpu.SemaphoreType.DMA((na, 4))], start=start, finish=finish)


def _gather_two_level_rider(shards):
    na = len(shards)

    def plan(ins, outs, sems):
        send_sems, recv_sems = sems
        x, y, c = _place()
        sib = (x, y, 1 - c)
        rc = functools.partial(pltpu.make_async_remote_copy, device_id_type=MESH)
        for a in range(na):
            hr = shards[a].shape[0] // 2
            mine, theirs = pl.ds(c * hr, hr), pl.ds((1 - c) * hr, hr)
            own = outs[a].at[2 * x + y]
            yield ("first", functools.partial(rc, ins[a], own, send_sems.at[a, 3], recv_sems.at[a, 3], device_id=sib),
                   functools.partial(rc, own, own, send_sems.at[a, 3], recv_sems.at[a, 3], device_id=sib))
            for k, (px, py) in enumerate([(1 - x, y), (x, 1 - y), (1 - x, 1 - y)]):
                got, fwd = outs[a].at[2 * px + py, mine], outs[a].at[2 * px + py, theirs]
                yield ("first", functools.partial(rc, ins[a].at[mine], own.at[mine], send_sems.at[a, k], recv_sems.at[a, k],
                                                  device_id=(px, py, c)),
                       functools.partial(rc, got, got, send_sems.at[a, k], recv_sems.at[a, k], device_id=(px, py, c)))
                yield ("second", functools.partial(rc, got, got, send_sems.at[a, 4 + k], recv_sems.at[a, 4 + k], device_id=sib),
                       functools.partial(rc, fwd, fwd, send_sems.at[a, 4 + k], recv_sems.at[a, 4 + k], device_id=sib))

    def start(ins, outs, sems):
        for phase, send, _ in plan(ins, outs, sems):
            if phase == "first":
                send().start()

    def finish(ins, outs, sems):
        steps = list(plan(ins, outs, sems))
        for i, (phase, _, arrived) in enumerate(steps):
            if phase == "first":
                arrived().wait_recv()
                if i + 1 < len(steps) and steps[i + 1][0] == "second":
                    steps[i + 1][1]().start()
        for phase, send, arrived in steps:
            if phase == "second":
                arrived().wait_recv()
        for _, send, _ in steps:
            send().wait_send()

    return dict(ins=list(shards), prev=[], out_shape=[SDS((N_CHIP,) + s.shape, s.dtype) for s in shards],
                scratch=[pltpu.SemaphoreType.DMA((na, 7)), pltpu.SemaphoreType.DMA((na, 7))], start=start, finish=finish)


def _scatter_rider(bufs, layer, prev):
    na = len(bufs)
    h = layer // (DEPTH // 2)

    def copies(ins, outs, sems):
        send_sems, recv_sems, local_sems = sems
        x, y, c = _place()
        me = 4 * x + 2 * y + c
        for a in range(na):
            for j in range(N_CHIP):
                is_self = ((2 * x + y) == j) & (c == h)
                local = functools.partial(pltpu.make_async_copy, ins[a].at[j], outs[a].at[me], local_sems.at[a])
                remote = functools.partial(pltpu.make_async_remote_copy, ins[a].at[j], outs[a].at[me], send_sems.at[a, j],
                                           recv_sems.at[a, me], device_id=(j // 2, j % 2, h), device_id_type=MESH)
                yield is_self, local, remote

    def start(ins, outs, sems):
        for is_self, local, remote in copies(ins, outs, sems):
            pl.when(is_self)(lambda: local().start())
            pl.when(jnp.logical_not(is_self))(lambda: remote().start())

    def finish(ins, outs, sems):
        _, recv_sems, _ = sems
        x, y, c = _place()
        me = 4 * x + 2 * y + c
        for a in range(na):
            for s in range(N_DEV):
                got = outs[a].at[s]
                arrived = functools.partial(pltpu.make_async_remote_copy, got, got, recv_sems.at[a, s], recv_sems.at[a, s],
                                            device_id=(s // 4, (s // 2) % 2, s % 2), device_id_type=MESH)
                pl.when((c == h) & (me != s))(lambda: arrived().wait_recv())
        for is_self, local, remote in copies(ins, outs, sems):
            pl.when(is_self)(lambda: local().wait())
            pl.when(jnp.logical_not(is_self))(lambda: remote().wait_send())

    return dict(ins=list(bufs), prev=list(prev), out_shape=[SDS((N_DEV,) + b.shape[1:], b.dtype) for b in bufs],
                scratch=[pltpu.SemaphoreType.DMA((na, N_CHIP)), pltpu.SemaphoreType.DMA((na, N_DEV)), pltpu.SemaphoreType.DMA((na,))],
                start=start, finish=finish)


def _sum_sources(parts, parity, into, *, name):
    _, R, C = parts.shape
    tr = _div_tile(R, 256, 16)

    def body(p_ref, *rest):
        acc = p_ref[0].astype(F32)
        for s in range(1, N_DEV):
            acc = acc + p_ref[s].astype(F32)
        rest[-1][0] = acc

    prev = [] if into is None else [into]
    return _pc(body, name=name, grid=(R // tr,),
               in_specs=[BS((N_DEV, tr, C), lambda i: (0, i, 0))] + [BS(memory_space=pl.ANY)] * len(prev),
               out_specs=BS((1, tr, C), lambda i: ((DEPTH // 2) * lax.axis_index("c") + parity, i, 0)),
               out_shape=SDS((DEPTH, R, C), F32), aliases={1: 0} if prev else None)(parts, *prev)


def _join_halves(fulls, *, name):
    na = len(fulls)
    half = DEPTH // 2

    def body(*refs):
        outs = refs[na:2 * na]
        send_sems, recv_sems = refs[2 * na:]
        x, y, c = _place()
        cps = []
        for a in range(na):
            mine = outs[a].at[pl.ds(c * half, half)]
            cp = pltpu.make_async_remote_copy(mine, mine, send_sems.at[a], recv_sems.at[a],
                                              device_id=(x, y, 1 - c), device_id_type=MESH)
            cp.start()
            cps.append(cp)
        for a in range(na):
            theirs = outs[a].at[pl.ds((1 - c) * half, half)]
            pltpu.make_async_remote_copy(theirs, theirs, send_sems.at[a], recv_sems.at[a],
                                         device_id=(x, y, 1 - c), device_id_type=MESH).wait_recv()
        for cp in cps:
            cp.wait_send()

    any_spec = BS(memory_space=pl.ANY)
    return pl.pallas_call(
        body, name=name, in_specs=[any_spec] * na, out_specs=[any_spec] * na,
        out_shape=[SDS(f.shape, f.dtype) for f in fulls], input_output_aliases={a: a for a in range(na)},
        scratch_shapes=[pltpu.SemaphoreType.DMA((na,)), pltpu.SemaphoreType.DMA((na,))],
        compiler_params=pltpu.CompilerParams(has_side_effects=True))(*fulls)


def _allreduce_small(vec, *, name):
    R = vec.shape[0]

    def body(v_ref, o_ref, all_ref, send_sems, recv_sems):
        x, y, c = _place()
        me = 4 * x + 2 * y + c
        all_ref[me] = v_ref[...]
        sends = []
        for r in range(1, N_DEV):
            tgt = (x ^ (r >> 2), y ^ ((r >> 1) & 1), c ^ (r & 1))
            cp = pltpu.make_async_remote_copy(v_ref, all_ref.at[me], send_sems.at[r - 1], recv_sems.at[r - 1],
                                              device_id=tgt, device_id_type=MESH)
            cp.start()
            sends.append(cp)
        for r in range(1, N_DEV):
            tx, ty, tc = x ^ (r >> 2), y ^ ((r >> 1) & 1), c ^ (r & 1)
            got = all_ref.at[4 * tx + 2 * ty + tc]
            pltpu.make_async_remote_copy(got, got, send_sems.at[r - 1], recv_sems.at[r - 1],
                                         device_id=(tx, ty, tc), device_id_type=MESH).wait_recv()
        for cp in sends:
            cp.wait_send()
        acc = all_ref[0]
        for s in range(1, N_DEV):
            acc = acc + all_ref[s]
        o_ref[...] = acc

    vm = BS(memory_space=pltpu.VMEM)
    return pl.pallas_call(
        body, name=name, in_specs=[vm], out_specs=vm, out_shape=SDS((R, LANE), F32),
        scratch_shapes=[pltpu.VMEM((N_DEV, R, LANE), F32), pltpu.SemaphoreType.DMA((N_DEV - 1,)), pltpu.SemaphoreType.DMA((N_DEV - 1,))],
        compiler_params=pltpu.CompilerParams(has_side_effects=True, vmem_limit_bytes=VMEM_LIMIT_BYTES))(vec)


def _pack(arrs):
    rows = []
    for a in arrs:
        f = a.reshape(-1).astype(F32)
        n = -(-f.shape[0] // LANE) * LANE
        rows.append(jnp.pad(f, (0, n - f.shape[0])).reshape(-1, LANE))
    v = jnp.concatenate(rows, axis=0)
    pad = -v.shape[0] % 8
    return jnp.pad(v, ((0, pad), (0, 0)))


def _unpack(v, shapes):
    out, r = [], 0
    for s in shapes:
        n = int(np.prod(s)) if len(s) else 1
        nr = -(-n // LANE)
        out.append(v[r:r + nr].reshape(-1)[:n].reshape(s))
        r += nr
    return out


def _perm_in_cols(w_full):
    z, xbc, dt, q, k, v = (w_full[..., :Z_END], w_full[..., Z_END:XBC_END], w_full[..., XBC_END:DT_END],
                           w_full[..., DT_END:Q_END], w_full[..., Q_END:K_END], w_full[..., K_END:])
    pad = jnp.zeros(dt.shape[:-1] + (LANE - dt.shape[-1],), dt.dtype)
    return jnp.concatenate([q, z, xbc, k, v, dt, pad], axis=-1)


def _unperm_in_cols(g):
    q, z, xbc, k, v, dt = (g[..., :1024], g[..., 1024:2048], g[..., 2048:3584], g[..., 3584:3840], g[..., 3840:4096],
                           g[..., 4096:4096 + 2 * SSM_HEADS])
    return jnp.concatenate([z, xbc, dt, q, k, v], axis=-1)


def _dt_cols(a):
    return jnp.pad(a.reshape(1, 2 * SSM_HEADS), ((0, 0), (0, LANE - 2 * SSM_HEADS)))


def _dt_fwd(proj, dtb, *, name):
    B, L, _ = proj.shape

    def body(p_ref, b_ref, o_ref):
        o_ref[0] = _softplus(p_ref[0] + b_ref[...])

    return _pc(body, name=name, grid=(B,),
               in_specs=[BS((1, L, LANE), lambda b: (b, 0, P_COLS // LANE - 1)), BS((1, LANE), lambda b: (0, 0))],
               out_specs=BS((1, L, LANE), lambda b: (b, 0, 0)), out_shape=SDS((B, L, LANE), F32))(proj, dtb)


def _dt_bwd(proj, dtb, ddt, *, name):
    B, L, _ = proj.shape

    def body(p_ref, b_ref, g_ref, o_ref, db_ref):
        g = g_ref[0] * _sigmoid(p_ref[0] + b_ref[...])
        o_ref[0] = g.astype(BF16)
        db_ref[0] = jnp.sum(g, axis=0, keepdims=True)

    row = BS((1, L, LANE), lambda b: (b, 0, 0))
    return _pc(body, name=name, grid=(B,),
               in_specs=[BS((1, L, LANE), lambda b: (b, 0, P_COLS // LANE - 1)), BS((1, LANE), lambda b: (0, 0)), row],
               out_specs=[row, BS((1, 1, LANE), lambda b: (b, 0, 0))],
               out_shape=[SDS((B, L, LANE), BF16), SDS((B, 1, LANE), F32)])(proj, dtb, ddt)


def _layer_fwd(i, x, wts, small, band_bias, riders=None, arrived=None):
    riders = riders or {}
    B, L, D = x.shape
    T = B * L
    x2 = x.reshape(T, D)
    h, proj2 = _norm_mm(x2, small["norm1_w"][i], wts["w_in"], name=f"in_proj_{i}", tn=1408)
    proj = proj2.reshape(B, L, P_COLS)
    xbc_act = _conv_fwd(proj, 2048 // 256, small["conv_w"][i], small["conv_b"][i], taps=SSM_CONV, ct=256,
                        out_dtype=F32, name=f"ssm_conv_{i}")
    dtb, alog = _dt_cols(small["dt_bias"][i]), _dt_cols(small["a_log"][i])
    dskip = jnp.repeat(small["d_skip"][i], HEAD_DIM).reshape(1, SSM_WIDTH)
    dtc = _dt_fwd(proj, dtb, name=f"dt_{i}")
    ypre, hs, carried = _ssd_fwd(xbc_act, dtc, alog, dskip, name=f"ssd_{i}", rider=riders.get("ssd"))
    if carried is not None:
        arrived("ssd", carried)
    mixed = _gate_fwd(ypre.reshape(T, SSM_WIDTH), proj2, small["ssm_norm_w"][i], name=f"gate_{i}")
    sinkcol = jnp.repeat(small["attn_sink"][i], BLOCK).reshape(ATTN_HEADS * BLOCK, 1)
    mixed, carried = _attn_fwd(proj, band_bias, sinkcol, mixed.reshape(B, L, 2 * D), name=f"attn_{i}", rider=riders.get("attn"))
    if carried is not None:
        arrived("attn", carried)
    mixed = mixed.reshape(T, 2 * D)
    x_mid = _mm(mixed, wts["w_out"], add=x2, name=f"out_proj_{i}")
    h2, gu2 = _norm_mm(x_mid, small["norm2_w"][i], wts["w_up"], name=f"up_proj_{i}", out_dtype=BF16, tn=1408)
    gu = gu2.reshape(B, L, 2 * D_FF)
    act = _conv_fwd(gu, 0, small["ffn_conv_w"][i], small["ffn_conv_b"][i], taps=FFN_CONV, ct=256, gate_blk0=D_FF // 256,
                    out_dtype=BF16, name=f"ffn_conv_{i}")
    x_out = _mm(act.reshape(T, D_FF), wts["w_down"], add=x_mid, name=f"down_proj_{i}", tk=1408)
    saved = dict(x2=x2, h=h, proj2=proj2, xbc_act=xbc_act, dtb=dtb, dtc=dtc, alog=alog, dskip=dskip, ypre=ypre, hs=hs, mixed=mixed,
                 sinkcol=sinkcol, x_mid=x_mid, h2=h2, gu=gu, act=act)
    return x_out.reshape(B, L, D), saved


def _layer_bwd(i, dx_out, dxb, sv, wts, small, band_bias, attn_rider=None, ssd_rider=None, arrived=None, wgrad_dtype=F32):
    T, D = dx_out.shape
    B, L = sv["gu"].shape[:2]
    g = {}
    dact = _mm(dxb, wts["w_down"], tb=True, out_dtype=BF16, name=f"d_act_{i}", tn=1408)
    g["w_down"] = _mm(sv["act"].reshape(T, D_FF), dxb, ta=True, out_dtype=wgrad_dtype, name=f"dw_down_{i}", tm=1408)
    dg, du, dcw, dcb = _conv_bwd(sv["gu"], 0, small["ffn_conv_w"][i], small["ffn_conv_b"][i], dact.reshape(B, L, D_FF),
                                 taps=FFN_CONV, ct=256, gate_blk0=D_FF // 256, name=f"d_ffn_conv_{i}")
    g["ffn_conv_w"] = jnp.sum(dcw, axis=0)[:FFN_CONV]
    g["ffn_conv_b"] = jnp.sum(dcb, axis=(0, 1))
    dg, du, w_up = dg.reshape(T, D_FF), du.reshape(T, D_FF), wts["w_up"]
    dh2 = _mm(dg, w_up[:, :D_FF], tb=True, name=f"d_h2_g_{i}", tk=1408)
    dh2 = _mm(du, w_up[:, D_FF:], tb=True, add=dh2, name=f"d_h2_u_{i}", tk=1408)
    dx_mid, dmb, dw2 = _dnorm(dh2, sv["x_mid"], small["norm2_w"][i], dx_out, name=f"d_norm2_{i}")
    g["w_up"] = jnp.concatenate([_mm(sv["h2"], dg, ta=True, out_dtype=wgrad_dtype, name=f"dw_up_g_{i}", tn=1408),
                                 _mm(sv["h2"], du, ta=True, out_dtype=wgrad_dtype, name=f"dw_up_u_{i}", tn=1408)], axis=1)
    g["norm2_w"] = dw2[0]
    dmixed = _mm(dmb, wts["w_out"], tb=True, out_dtype=BF16, name=f"d_mixed_{i}")
    g["w_out"] = _mm(sv["mixed"], dmb, ta=True, out_dtype=wgrad_dtype, name=f"dw_out_{i}")
    dypre, dz, dwn = _gate_bwd(sv["ypre"].reshape(T, SSM_WIDTH), sv["proj2"], small["ssm_norm_w"][i], dmixed, name=f"d_gate_{i}")
    g["ssm_norm_w"] = jnp.sum(dwn, axis=(0, 1))
    proj = sv["proj2"].reshape(B, L, P_COLS)
    dxs, dbm, dcm, ddt, dalog, ddsk, carried = _ssd_bwd(sv["xbc_act"], sv["dtc"], sv["alog"], sv["dskip"], sv["hs"],
                                                        dypre.reshape(B, L, SSM_WIDTH), name=f"d_ssd_{i}",
                                                        rider=ssd_rider(g) if ssd_rider is not None else None)
    if carried is not None:
        arrived("ssd", carried)
    ddt, ddtb = _dt_bwd(proj, sv["dtb"], ddt, name=f"d_dt_{i}")
    g["dt_bias"] = jnp.sum(ddtb, axis=(0, 1))[:2 * SSM_HEADS].reshape(2, SSM_HEADS)
    g["a_log"] = jnp.sum(dalog, axis=(0, 1, 2))[:2 * SSM_HEADS].reshape(2, SSM_HEADS)
    g["d_skip"] = jnp.sum(ddsk.reshape(B, SSM_HEADS, HEAD_DIM), axis=(0, 2))
    dxbc_act = dxs.at[:, :, SSM_WIDTH:SSM_WIDTH + BC_WIDTH].set(dbm).at[:, :, SSM_WIDTH + BC_WIDTH:].set(dcm)
    dxbc, dcw, dcb = _conv_bwd(proj, 2048 // 256, small["conv_w"][i], small["conv_b"][i], dxbc_act, taps=SSM_CONV, ct=256,
                               name=f"d_ssm_conv_{i}")
    g["conv_w"] = jnp.sum(dcw, axis=0)[:SSM_CONV]
    g["conv_b"] = jnp.sum(dcb, axis=(0, 1))
    dq, dk, dv, dbias, dsink, carried = _attn_bwd(proj, band_bias, sv["sinkcol"], dmixed.reshape(B, L, 2 * D), name=f"d_attn_{i}",
                                                  rider=attn_rider)
    if carried is not None:
        arrived("attn", carried)
    g["attn_sink"] = jnp.sum(dsink.reshape(ATTN_HEADS, BLOCK), axis=1)
    dproj = jnp.concatenate([dq, dz.reshape(B, L, SSM_WIDTH), dxbc, dk.astype(BF16), dv.astype(BF16), ddt], axis=-1).reshape(T, P_COLS)
    dh = _mm(dproj, wts["w_in"], tb=True, name=f"d_h_{i}", tk=1408)
    dx_in, dx_in_b, dw1 = _dnorm(dh, sv["x2"], small["norm1_w"][i], dx_mid, name=f"d_norm1_{i}")
    g["w_in"] = _unperm_in_cols(_mm(sv["h"], dproj, ta=True, out_dtype=wgrad_dtype, name=f"dw_in_{i}", tn=1408))
    g["norm1_w"] = dw1[0]
    return dx_in, dx_in_b, g, dbias


_BIG = ("w_in", "w_out", "w_up", "w_down")
_BIG_AXIS = {"w_in": 2, "w_out": 1, "w_up": 2, "w_down": 1}
_SMALL = ("rel_bias", "norm1_w", "conv_w", "conv_b", "dt_bias", "a_log", "d_skip", "ssm_norm_w", "attn_sink", "norm2_w",
          "ffn_conv_w", "ffn_conv_b", "final_norm_w")
_SMALL_SHARDED = ("conv_w", "ffn_conv_w")
_ORDER = ("rel_bias", "norm1_w", "w_in", "conv_w", "conv_b", "dt_bias", "a_log", "d_skip", "ssm_norm_w", "attn_sink", "w_out",
          "norm2_w", "w_up", "ffn_conv_w", "ffn_conv_b", "w_down", "final_norm_w")


def _local_step(x, target, small, wts=None, exchange=None):
    B, L, D = x.shape
    bucket = _bucket_table()
    band_bias = _bias_expand(small["rel_bias"], bucket, name="band_bias")

    def fetch(spec):
        return exchange["gather"](spec) if exchange is not None and spec else None

    def fetched(spec, carried):
        for (i, k), full in zip(spec, exchange["weights"](spec, carried)):
            wts[i][k] = full

    if exchange is not None:
        wts = [{} for _ in range(DEPTH)]
        fetched([(0, "w_in")], _run_rider(exchange["gather_alone"]([(0, "w_in")]), name="gather_w_in_0"))
    saved = []
    for i in range(DEPTH):
        nxt = i + 1 < DEPTH
        if i == 0:
            plan = {"ssd": [(0, "w_out"), (0, "w_up"), (0, "w_down")], "attn": [(1, "w_in"), (1, "w_out")] if nxt else []}
        else:
            plan = {"ssd": [(i, "w_down")] + ([(i + 1, "w_in"), (i + 1, "w_out")] if nxt else []), "attn": [(i, "w_up")]}
        x, sv = _layer_fwd(i, x, wts[i], small, band_bias, {c: fetch(s) for c, s in plan.items()}, lambda c, r: fetched(plan[c], r))
        saved.append(sv)
    loss, dx, dxb, dwf = _loss_head(x.reshape(B * L, D), small["final_norm_w"], target.reshape(B * L, D), name="loss_head")
    per_layer = []
    dbias = jnp.zeros((ATTN_HEADS, BLOCK, KEY_SPAN), F32)
    late = None
    for i in reversed(range(DEPTH)):
        own = [(i, "w_down"), (i, "w_up"), (i, "w_out")]
        plan = {"ssd": own, "attn": late[0] if late else []}
        attn_rider = exchange["scatter"](*late) if late else None
        ssd_rider = (lambda g: exchange["scatter"](own, [g[k] for _, k in own])) if exchange is not None else None
        dx, dxb, g, dbias_i = _layer_bwd(i, dx, dxb, saved[i], wts[i], small, band_bias, attn_rider, ssd_rider,
                                    lambda c, r: exchange["collect"](plan[c], r), F32 if exchange is None else BF16)
        if exchange is not None:
            late = ([(i, "w_in")], [g["w_in"]])
            for k in _BIG:
                g.pop(k)
        dbias = dbias + dbias_i
        per_layer.append(g)
    if exchange is not None:
        exchange["collect"](late[0], _run_rider(exchange["scatter"](*late), name="scatter_dw_in_0"))
    per_layer.reverse()
    grads = {k: jnp.stack([g[k] for g in per_layer]) for k in per_layer[0]}
    grads["rel_bias"] = _bias_reduce(dbias, bucket, name="d_rel_bias")
    grads["final_norm_w"] = dwf[0]
    return loss, dx.reshape(B, L, D), grads


def _split_by_chip(g, axis):
    shp = g.shape
    n = shp[axis] // N_CHIP
    g = g.reshape(shp[:axis] + (N_CHIP, n) + shp[axis + 1:])
    return jnp.moveaxis(g, axis, 0)


def _join_chips(a, axis):
    a = jnp.moveaxis(a, 0, axis)
    shp = a.shape
    return a.reshape(shp[:axis] + (shp[axis] * shp[axis + 1],) + shp[axis + 2:])


def kernel(x, rel_bias, norm1_w, w_in, conv_w, conv_b, dt_bias, a_log, d_skip, ssm_norm_w, attn_sink, w_out, norm2_w, w_up, ffn_conv_w, ffn_conv_b, w_down, final_norm_w, loss_target, m_rel_bias, m_norm1_w, m_w_in, m_conv_w, m_conv_b, m_dt_bias, m_a_log, m_d_skip, m_ssm_norm_w, m_attn_sink, m_w_out, m_norm2_w, m_w_up, m_ffn_conv_w, m_ffn_conv_b, m_w_down, m_final_norm_w, v_rel_bias, v_norm1_w, v_w_in, v_conv_w, v_conv_b, v_dt_bias, v_a_log, v_d_skip, v_ssm_norm_w, v_attn_sink, v_w_out, v_norm2_w, v_w_up, v_ffn_conv_w, v_ffn_conv_b, v_w_down, v_final_norm_w):
    w = dict(rel_bias=rel_bias, norm1_w=norm1_w, w_in=w_in, conv_w=conv_w, conv_b=conv_b, dt_bias=dt_bias, a_log=a_log,
             d_skip=d_skip, ssm_norm_w=ssm_norm_w, attn_sink=attn_sink, w_out=w_out, norm2_w=norm2_w, w_up=w_up,
             ffn_conv_w=ffn_conv_w, ffn_conv_b=ffn_conv_b, w_down=w_down, final_norm_w=final_norm_w)
    m = dict(rel_bias=m_rel_bias, norm1_w=m_norm1_w, w_in=m_w_in, conv_w=m_conv_w, conv_b=m_conv_b, dt_bias=m_dt_bias,
             a_log=m_a_log, d_skip=m_d_skip, ssm_norm_w=m_ssm_norm_w, attn_sink=m_attn_sink, w_out=m_w_out, norm2_w=m_norm2_w,
             w_up=m_w_up, ffn_conv_w=m_ffn_conv_w, ffn_conv_b=m_ffn_conv_b, w_down=m_w_down, final_norm_w=m_final_norm_w)
    v = dict(rel_bias=v_rel_bias, norm1_w=v_norm1_w, w_in=v_w_in, conv_w=v_conv_w, conv_b=v_conv_b, dt_bias=v_dt_bias,
             a_log=v_a_log, d_skip=v_d_skip, ssm_norm_w=v_ssm_norm_w, attn_sink=v_attn_sink, w_out=v_w_out, norm2_w=v_norm2_w,
             w_up=v_w_up, ffn_conv_w=v_ffn_conv_w, ffn_conv_b=v_ffn_conv_b, w_down=v_w_down, final_norm_w=v_final_norm_w)
    my_chip = 2 * lax.axis_index("x") + lax.axis_index("y")

    shards = {k: w[k].astype(BF16) for k in _BIG}
    received = {}

    def gather(spec):
        return _gather_rider([shards[k][i] for i, k in spec])

    def weights(spec, carried):
        out = []
        for (i, k), g_ in zip(spec, carried):
            full = _join_chips(g_, _BIG_AXIS[k] - 1)
            out.append(_perm_in_cols(full) if k == "w_in" else full)
        return out

    def scatter(spec, grads):
        layer = spec[0][0]
        bufs = [_split_by_chip(g_, _BIG_AXIS[k] - 1).astype(BF16) for (_, k), g_ in zip(spec, grads)]
        prev = [received[(layer % 2, k)] for _, k in spec] if layer + 2 < DEPTH else []
        return _scatter_rider(bufs, layer, prev)

    def collect(spec, carried):
        for (i, k), pieces in zip(spec, carried):
            received[(i % 2, k)] = pieces

    conv_shapes = [(DEPTH, SSM_CONV, CONV_CH), (DEPTH, FFN_CONV, D_FF)]
    placed = [lax.dynamic_update_slice_in_dim(jnp.zeros(s, F32), w[k], my_chip * w[k].shape[2], axis=2)
              for k, s in zip(_SMALL_SHARDED, conv_shapes)]
    lead = (lax.axis_index("c") == 0).astype(F32)
    conv_full = _unpack(_allreduce_small(_pack([p * lead for p in placed]), name="gather_conv_weights"), conv_shapes)
    small = {k: w[k] for k in _SMALL}
    small["conv_w"], small["ffn_conv_w"] = conv_full

    loss_part, grad_x, gp = _local_step(x, loss_target, small,
                                        exchange=dict(gather=gather, weights=weights, scatter=scatter, collect=collect,
                                                      gather_alone=lambda spec: _gather_two_level_rider([shards[k][i] for i, k in spec])))

    small_shapes = [small[k].shape for k in _SMALL] + [()]
    red = _unpack(_allreduce_small(_pack([gp[k] for k in _SMALL] + [loss_part[0, :1]]), name="reduce_small"), small_shapes)
    gsmall = dict(zip(_SMALL, red[:-1]))
    loss = red[-1]
    for k in _SMALL_SHARDED:
        n = w[k].shape[2]
        gsmall[k] = lax.dynamic_slice_in_dim(gsmall[k], my_chip * n, n, axis=2)

    fulls = []
    for k in _BIG:
        full = None
        for p in range(DEPTH // 2):
            full = _sum_sources(received[(p, k)], p, full, name=f"sum_{k}_{p}")
        fulls.append(full)
    gbig = dict(zip(_BIG, _join_halves(fulls, name="join_halves")))

    grad, delta, new_m, new_v = {}, {}, {}, {}
    for k in _BIG:
        shp = w[k].shape
        two = lambda a: a.reshape(shp[0] * shp[1], shp[2])
        d_, m_, v_ = _adamw(two(w[k]), two(gbig[k]), two(m[k]), two(v[k]), name=f"adamw_{k}")
        grad[k], delta[k], new_m[k], new_v[k] = gbig[k], d_.reshape(shp), m_.reshape(shp), v_.reshape(shp)
    shapes = [w[k].shape for k in _SMALL]
    d_, m_, v_ = _adamw(_pack([w[k] for k in _SMALL]), _pack([gsmall[k] for k in _SMALL]), _pack([m[k] for k in _SMALL]),
                        _pack([v[k] for k in _SMALL]), name="adamw_small")
    for k, a, b_, c_ in zip(_SMALL, _unpack(d_, shapes), _unpack(m_, shapes), _unpack(v_, shapes)):
        grad[k], delta[k], new_m[k], new_v[k] = gsmall[k], a, b_, c_
    return (loss, grad_x, *[grad[k] for k in _ORDER], *[delta[k] for k in _ORDER], *[new_m[k] for k in _ORDER],
            *[new_v[k] for k in _ORDER])
```
